```python
import jax, jax.numpy as jnp
from jax import lax
import numpy as np

D_MODEL = 2048
BATCH = 8
SEQ = 2048
DEPTH = 1

HEAD_DIM = 64
N_Q_HEADS = D_MODEL // HEAD_DIM
N_KV_HEADS = 4
GQA_GROUP = N_Q_HEADS // N_KV_HEADS
WINDOW = 128

GLA_HEADS = 4
GLA_DK = (D_MODEL // 2) // GLA_HEADS
GLA_DV = D_MODEL // GLA_HEADS
GLA_GATE_RANK = 16
GLA_GATE_NORMALIZER = 16.0
GLA_CHUNK = 64

FFN_HIDDEN = ((8 * D_MODEL // 3 + 255) // 256) * 256

RMS_EPS = 1e-6
MASK_VALUE = -1e30

IN_WIDTHS = (
    N_Q_HEADS * HEAD_DIM,
    N_KV_HEADS * HEAD_DIM,
    N_KV_HEADS * HEAD_DIM,
    GLA_HEADS * GLA_DK,
    GLA_HEADS * GLA_DK,
    GLA_HEADS * GLA_DV,
    GLA_GATE_RANK,
    GLA_HEADS * GLA_DV,
    D_MODEL,
    D_MODEL,
)
D_IN = sum(IN_WIDTHS)

kernel_name = "hybrid_swa_sink_gla_swiglu_block"


def _rmsnorm(x, w):
    xf = x.astype(jnp.float32)
    y = xf * lax.rsqrt(jnp.mean(xf * xf, axis=-1, keepdims=True) + RMS_EPS)
    return (y * w.astype(jnp.float32)).astype(x.dtype)


def _split_offsets():
    offs, acc = [], 0
    for w in IN_WIDTHS[:-1]:
        acc += w
        offs.append(acc)
    return offs


def _swa_sink_attention(q, k, v, sinks):
    B, T = q.shape[0], q.shape[1]
    nb = T // WINDOW
    qb = q.reshape(B, nb, WINDOW, N_KV_HEADS, GQA_GROUP, HEAD_DIM)
    kb = k.reshape(B, nb, WINDOW, N_KV_HEADS, HEAD_DIM)
    vb = v.reshape(B, nb, WINDOW, N_KV_HEADS, HEAD_DIM)
    k_prev = jnp.concatenate([jnp.zeros_like(kb[:, :1]), kb[:, :-1]], axis=1)
    v_prev = jnp.concatenate([jnp.zeros_like(vb[:, :1]), vb[:, :-1]], axis=1)
    kk = jnp.concatenate([k_prev, kb], axis=2)
    vv = jnp.concatenate([v_prev, vb], axis=2)
    s = jnp.einsum('bnqhgd,bnkhd->bhgnqk', qb, kk).astype(jnp.float32) * (HEAD_DIM ** -0.5)
    qi = jnp.arange(WINDOW)[:, None]
    ki = jnp.arange(2 * WINDOW)[None, :]
    rel = qi + WINDOW - ki
    band = (rel >= 0) & (rel < WINDOW)
    blk = jnp.arange(nb)[:, None, None]
    mask = band[None] & ((blk > 0) | (ki[None] >= WINDOW))
    s = jnp.where(mask, s, MASK_VALUE)
    sink = sinks.astype(jnp.float32).reshape(N_KV_HEADS, GQA_GROUP)[None, :, :, None, None, None]
    sink = jnp.broadcast_to(sink, s.shape[:-1] + (1,))
    p = jax.nn.softmax(jnp.concatenate([s, sink], axis=-1), axis=-1)[..., :-1]
    o = jnp.einsum('bhgnqk,bnkhd->bnqhgd', p.astype(vv.dtype), vv)
    return o.reshape(B, T, N_Q_HEADS * HEAD_DIM)


def _gla(q, k, v, log_a):
    B, T = q.shape[0], q.shape[1]
    nc = T // GLA_CHUNK

    def chunk(t, d):
        return t.astype(jnp.float32).reshape(B, nc, GLA_CHUNK, GLA_HEADS, d).transpose(0, 3, 1, 2, 4)

    qc = chunk(q, GLA_DK) * (GLA_DK ** -0.5)
    kc = chunk(k, GLA_DK)
    vc = chunk(v, GLA_DV)
    g = jnp.cumsum(chunk(log_a, GLA_DK), axis=3)
    g_last = g[..., -1:, :]
    q_dec = qc * jnp.exp(g)
    k_inv = kc * jnp.exp(-g)
    k_to_end = kc * jnp.exp(g_last - g)
    causal = jnp.tril(jnp.ones((GLA_CHUNK, GLA_CHUNK), dtype=bool))
    att = jnp.where(causal, jnp.einsum('bhnid,bhnjd->bhnij', q_dec, k_inv), 0.0)
    o_intra = jnp.einsum('bhnij,bhnjv->bhniv', att, vc)
    upd = jnp.einsum('bhnjd,bhnjv->bhndv', k_to_end, vc)
    decay = jnp.exp(g_last[..., 0, :])

    def step(state, inp):
        d_c, u_c = inp
        return d_c[..., None] * state + u_c, state

    s0 = jnp.zeros((B, GLA_HEADS, GLA_DK, GLA_DV), jnp.float32)
    _, s_prev = lax.scan(step, s0, (jnp.moveaxis(decay, 2, 0), jnp.moveaxis(upd, 2, 0)))
    s_prev = jnp.moveaxis(s_prev, 0, 2)
    o_inter = jnp.einsum('bhnid,bhndv->bhniv', q_dec, s_prev)
    o = o_intra + o_inter
    return o.transpose(0, 2, 3, 1, 4).reshape(B, T, GLA_HEADS, GLA_DV)


def _fwd_setup_inputs(seed: int = 0) -> dict:
    key = jax.random.key(seed)
    ks = jax.random.split(key, 14)
    f32 = jnp.float32

    def normal(k, shape, scale):
        return jax.random.normal(k, shape, f32) * scale

    return {
        "x": jax.random.normal(ks[0], (BATCH, SEQ, D_MODEL), f32),
        "norm1_w": 1.0 + normal(ks[1], (DEPTH, D_MODEL), 0.02),
        "w_in": normal(ks[2], (DEPTH, D_MODEL, D_IN), D_MODEL ** -0.5),
        "gla_gate_w2": normal(ks[3], (DEPTH, GLA_GATE_RANK, GLA_HEADS * GLA_DK), GLA_GATE_RANK ** -0.5),
        "gla_gate_b": normal(ks[4], (DEPTH, GLA_HEADS * GLA_DK), 0.02),
        "attn_sinks": normal(ks[5], (DEPTH, N_Q_HEADS), 0.5),
        "gla_norm_w": 1.0 + normal(ks[6], (DEPTH, GLA_DV), 0.02),
        "w_out": normal(ks[7], (DEPTH, D_MODEL, D_MODEL), D_MODEL ** -0.5),
        "norm2_w": 1.0 + normal(ks[8], (DEPTH, D_MODEL), 0.02),
        "w_ffn_gate": normal(ks[9], (DEPTH, D_MODEL, FFN_HIDDEN), D_MODEL ** -0.5),
        "w_ffn_up": normal(ks[10], (DEPTH, D_MODEL, FFN_HIDDEN), D_MODEL ** -0.5),
        "w_ffn_down": normal(ks[11], (DEPTH, FFN_HIDDEN, D_MODEL), FFN_HIDDEN ** -0.5),
        "final_norm_w": 1.0 + normal(ks[12], (D_MODEL,), 0.02),
    }


def _fwd_reference(x, norm1_w, w_in, gla_gate_w2, gla_gate_b, attn_sinks, gla_norm_w, w_out,
              norm2_w, w_ffn_gate, w_ffn_up, w_ffn_down, final_norm_w):
    B, T, _ = x.shape
    offs = _split_offsets()
    h = x
    for l in range(DEPTH):
        u = _rmsnorm(h, norm1_w[l])
        proj = u @ w_in[l]
        aq, ak, av, gq, gk, gv, g_lr, g_r, gate_a, gate_b = jnp.split(proj, offs, axis=-1)

        attn_o = _swa_sink_attention(aq, ak, av, attn_sinks[l])

        gate_logit = (g_lr @ gla_gate_w2[l] + gla_gate_b[l]).astype(jnp.float32)
        log_a = jax.nn.log_sigmoid(gate_logit) / GLA_GATE_NORMALIZER
        gla_o = _gla(gq.reshape(B, T, GLA_HEADS, GLA_DK),
                     gk.reshape(B, T, GLA_HEADS, GLA_DK),
                     gv.reshape(B, T, GLA_HEADS, GLA_DV),
                     log_a.reshape(B, T, GLA_HEADS, GLA_DK))
        gla_o = _rmsnorm(gla_o, gla_norm_w[l]).astype(x.dtype)
        gla_o = gla_o.reshape(B, T, GLA_HEADS * GLA_DV) * jax.nn.silu(g_r)

        merged = jax.nn.sigmoid(gate_a) * attn_o + jax.nn.sigmoid(gate_b) * gla_o
        h = h + merged @ w_out[l]

        v2 = _rmsnorm(h, norm2_w[l])
        ff = jax.nn.silu(v2 @ w_ffn_gate[l]) * (v2 @ w_ffn_up[l])
        h = h + ff @ w_ffn_down[l]
    return _rmsnorm(h, final_norm_w)


import jax as _jax
import jax.numpy as _jnp

TWIN_FORMAT = 'train_step'
FWD_PARAMS = ['x', 'norm1_w', 'w_in', 'gla_gate_w2', 'gla_gate_b', 'attn_sinks', 'gla_norm_w', 'w_out', 'norm2_w', 'w_ffn_gate', 'w_ffn_up', 'w_ffn_down', 'final_norm_w']
TWIN_WEIGHTS = ['norm1_w', 'w_in', 'gla_gate_w2', 'gla_gate_b', 'attn_sinks', 'gla_norm_w', 'w_out', 'norm2_w', 'w_ffn_gate', 'w_ffn_up', 'w_ffn_down', 'final_norm_w']
TWIN_DIFF_INPUT = 'x'
TWIN_INPUTS = ['x', 'norm1_w', 'w_in', 'gla_gate_w2', 'gla_gate_b', 'attn_sinks', 'gla_norm_w', 'w_out', 'norm2_w', 'w_ffn_gate', 'w_ffn_up', 'w_ffn_down', 'final_norm_w', 'loss_target', 'm_norm1_w', 'm_w_in', 'm_gla_gate_w2', 'm_gla_gate_b', 'm_attn_sinks', 'm_gla_norm_w', 'm_w_out', 'm_norm2_w', 'm_w_ffn_gate', 'm_w_ffn_up', 'm_w_ffn_down', 'm_final_norm_w', 'v_norm1_w', 'v_w_in', 'v_gla_gate_w2', 'v_gla_gate_b', 'v_attn_sinks', 'v_gla_norm_w', 'v_w_out', 'v_norm2_w', 'v_w_ffn_gate', 'v_w_ffn_up', 'v_w_ffn_down', 'v_final_norm_w']
TWIN_OUTPUTS = ['loss', 'grad_x', 'grad_norm1_w', 'grad_w_in', 'grad_gla_gate_w2', 'grad_gla_gate_b', 'grad_attn_sinks', 'grad_gla_norm_w', 'grad_w_out', 'grad_norm2_w', 'grad_w_ffn_gate', 'grad_w_ffn_up', 'grad_w_ffn_down', 'grad_final_norm_w', 'delta_norm1_w', 'delta_w_in', 'delta_gla_gate_w2', 'delta_gla_gate_b', 'delta_attn_sinks', 'delta_gla_norm_w', 'delta_w_out', 'delta_norm2_w', 'delta_w_ffn_gate', 'delta_w_ffn_up', 'delta_w_ffn_down', 'delta_final_norm_w', 'new_m_norm1_w', 'new_m_w_in', 'new_m_gla_gate_w2', 'new_m_gla_gate_b', 'new_m_attn_sinks', 'new_m_gla_norm_w', 'new_m_w_out', 'new_m_norm2_w', 'new_m_w_ffn_gate', 'new_m_w_ffn_up', 'new_m_w_ffn_down', 'new_m_final_norm_w', 'new_v_norm1_w', 'new_v_w_in', 'new_v_gla_gate_w2', 'new_v_gla_gate_b', 'new_v_attn_sinks', 'new_v_gla_norm_w', 'new_v_w_out', 'new_v_norm2_w', 'new_v_w_ffn_gate', 'new_v_w_ffn_up', 'new_v_w_ffn_down', 'new_v_final_norm_w']
TWIN_LEAF_KINDS = {'loss': 'loss', 'grad_x': 'grad_x', 'grad_norm1_w': 'grad_w', 'grad_w_in': 'grad_w', 'grad_gla_gate_w2': 'grad_w', 'grad_gla_gate_b': 'grad_w', 'grad_attn_sinks': 'grad_w', 'grad_gla_norm_w': 'grad_w', 'grad_w_out': 'grad_w', 'grad_norm2_w': 'grad_w', 'grad_w_ffn_gate': 'grad_w', 'grad_w_ffn_up': 'grad_w', 'grad_w_ffn_down': 'grad_w', 'grad_final_norm_w': 'grad_w', 'delta_norm1_w': 'delta_w', 'delta_w_in': 'delta_w', 'delta_gla_gate_w2': 'delta_w', 'delta_gla_gate_b': 'delta_w', 'delta_attn_sinks': 'delta_w', 'delta_gla_norm_w': 'delta_w', 'delta_w_out': 'delta_w', 'delta_norm2_w': 'delta_w', 'delta_w_ffn_gate': 'delta_w', 'delta_w_ffn_up': 'delta_w', 'delta_w_ffn_down': 'delta_w', 'delta_final_norm_w': 'delta_w', 'new_m_norm1_w': 'new_m', 'new_m_w_in': 'new_m', 'new_m_gla_gate_w2': 'new_m', 'new_m_gla_gate_b': 'new_m', 'new_m_attn_sinks': 'new_m', 'new_m_gla_norm_w': 'new_m', 'new_m_w_out': 'new_m', 'new_m_norm2_w': 'new_m', 'new_m_w_ffn_gate': 'new_m', 'new_m_w_ffn_up': 'new_m', 'new_m_w_ffn_down': 'new_m', 'new_m_final_norm_w': 'new_m', 'new_v_norm1_w': 'new_v', 'new_v_w_in': 'new_v', 'new_v_gla_gate_w2': 'new_v', 'new_v_gla_gate_b': 'new_v', 'new_v_attn_sinks': 'new_v', 'new_v_gla_norm_w': 'new_v', 'new_v_w_out': 'new_v', 'new_v_norm2_w': 'new_v', 'new_v_w_ffn_gate': 'new_v', 'new_v_w_ffn_up': 'new_v', 'new_v_w_ffn_down': 'new_v', 'new_v_final_norm_w': 'new_v'}


def _forward(args):
    return _fwd_reference(*[args[k] for k in FWD_PARAMS])


def _output_shape():
    out = _jax.eval_shape(lambda: _forward(_fwd_setup_inputs(0)))
    return out.shape, out.dtype

N_MICROBATCH = 1
ADAM_LR = 0.001
ADAM_B1 = 0.9
ADAM_B2 = 0.999
ADAM_EPS = 1e-08
ADAM_WD = 0.01
ADAM_STEP = 10
PER_EXAMPLE_BATCH_AXIS = {'x': 0, 'loss_target': 0}
SHARED_INPUTS = []
_WEIGHT_DTYPES = {'norm1_w': _jnp.float32, 'w_in': _jnp.float32, 'gla_gate_w2': _jnp.float32, 'gla_gate_b': _jnp.float32, 'attn_sinks': _jnp.float32, 'gla_norm_w': _jnp.float32, 'w_out': _jnp.float32, 'norm2_w': _jnp.float32, 'w_ffn_gate': _jnp.float32, 'w_ffn_up': _jnp.float32, 'w_ffn_down': _jnp.float32, 'final_norm_w': _jnp.float32}
MOMENT_SCALE = {'norm1_w': 4.668686e-02, 'w_in': 1.854942e-02, 'gla_gate_w2': 3.491953e-03, 'gla_gate_b': 1.426166e-02, 'attn_sinks': 5.757826e-03, 'gla_norm_w': 4.261830e-02, 'w_out': 2.262184e-02, 'norm2_w': 4.443436e-02, 'w_ffn_gate': 1.926292e-02, 'w_ffn_up': 1.865161e-02, 'w_ffn_down': 3.092235e-02, 'final_norm_w': 8.004748e+00}


def _to_microbatches(a, axis):
    t = _jnp.moveaxis(a, axis, 0)
    t = t.reshape((N_MICROBATCH, t.shape[0] // N_MICROBATCH) + t.shape[1:])
    return _jnp.moveaxis(t, 1, axis + 1)


def setup_inputs(seed: int = 0) -> dict:
    inp = _fwd_setup_inputs(seed)
    key = _jax.random.fold_in(_jax.random.key(seed), 7919)
    shape, _ = _output_shape()
    out = dict(inp)
    out["loss_target"] = _jax.random.normal(_jax.random.fold_in(key, 0), shape, _jnp.float32)
    for i, name in enumerate(TWIN_WEIGHTS):
        w = inp[name].astype(_jnp.float32)
        if MOMENT_SCALE is None:
            s = _jnp.sqrt(_jnp.mean(_jnp.square(w)) + 1e-30)
        else:
            s = MOMENT_SCALE[name]
        km, kv = _jax.random.split(_jax.random.fold_in(key, i + 1))
        out[name] = w
        out["m_" + name] = s * _jax.random.normal(km, w.shape, _jnp.float32)
        out["v_" + name] = (s * s) * _jax.random.uniform(kv, w.shape, _jnp.float32, 0.5, 1.5)
    if N_MICROBATCH > 1:
        for name, axis in PER_EXAMPLE_BATCH_AXIS.items():
            out[name] = _to_microbatches(out[name], axis)
    return {'x': out['x'], 'norm1_w': out['norm1_w'], 'w_in': out['w_in'], 'gla_gate_w2': out['gla_gate_w2'], 'gla_gate_b': out['gla_gate_b'], 'attn_sinks': out['attn_sinks'], 'gla_norm_w': out['gla_norm_w'], 'w_out': out['w_out'], 'norm2_w': out['norm2_w'], 'w_ffn_gate': out['w_ffn_gate'], 'w_ffn_up': out['w_ffn_up'], 'w_ffn_down': out['w_ffn_down'], 'final_norm_w': out['final_norm_w'], 'loss_target': out['loss_target'], 'm_norm1_w': out['m_norm1_w'], 'm_w_in': out['m_w_in'], 'm_gla_gate_w2': out['m_gla_gate_w2'], 'm_gla_gate_b': out['m_gla_gate_b'], 'm_attn_sinks': out['m_attn_sinks'], 'm_gla_norm_w': out['m_gla_norm_w'], 'm_w_out': out['m_w_out'], 'm_norm2_w': out['m_norm2_w'], 'm_w_ffn_gate': out['m_w_ffn_gate'], 'm_w_ffn_up': out['m_w_ffn_up'], 'm_w_ffn_down': out['m_w_ffn_down'], 'm_final_norm_w': out['m_final_norm_w'], 'v_norm1_w': out['v_norm1_w'], 'v_w_in': out['v_w_in'], 'v_gla_gate_w2': out['v_gla_gate_w2'], 'v_gla_gate_b': out['v_gla_gate_b'], 'v_attn_sinks': out['v_attn_sinks'], 'v_gla_norm_w': out['v_gla_norm_w'], 'v_w_out': out['v_w_out'], 'v_norm2_w': out['v_norm2_w'], 'v_w_ffn_gate': out['v_w_ffn_gate'], 'v_w_ffn_up': out['v_w_ffn_up'], 'v_w_ffn_down': out['v_w_ffn_down'], 'v_final_norm_w': out['v_final_norm_w']}


def _loss(weights, diff, rest, loss_target):
    with _jax.named_scope("forward"):
        args = {**rest, TWIN_DIFF_INPUT: diff, **{k: w.astype(_WEIGHT_DTYPES[k]) for k, w in weights.items()}}
        y = _forward(args)
    with _jax.named_scope("loss_head"):
        err = _jnp.square(y.astype(_jnp.float32) - loss_target)
        return 0.5 * _jnp.sum(_jnp.mean(err, axis=-1)) if err.ndim else 0.5 * err


def _adamw(w, g, m, v):
    m = ADAM_B1 * m + (1.0 - ADAM_B1) * g
    v = ADAM_B2 * v + (1.0 - ADAM_B2) * _jnp.square(g)
    m_hat = m / (1.0 - ADAM_B1 ** ADAM_STEP)
    v_hat = v / (1.0 - ADAM_B2 ** ADAM_STEP)
    delta = -ADAM_LR * (m_hat / (_jnp.sqrt(v_hat) + ADAM_EPS) + ADAM_WD * w)
    return delta, m, v


def reference(x, norm1_w, w_in, gla_gate_w2, gla_gate_b, attn_sinks, gla_norm_w, w_out, norm2_w, w_ffn_gate, w_ffn_up, w_ffn_down, final_norm_w, loss_target, m_norm1_w, m_w_in, m_gla_gate_w2, m_gla_gate_b, m_attn_sinks, m_gla_norm_w, m_w_out, m_norm2_w, m_w_ffn_gate, m_w_ffn_up, m_w_ffn_down, m_final_norm_w, v_norm1_w, v_w_in, v_gla_gate_w2, v_gla_gate_b, v_attn_sinks, v_gla_norm_w, v_w_out, v_norm2_w, v_w_ffn_gate, v_w_ffn_up, v_w_ffn_down, v_final_norm_w):
    given = dict(x=x, norm1_w=norm1_w, w_in=w_in, gla_gate_w2=gla_gate_w2, gla_gate_b=gla_gate_b, attn_sinks=attn_sinks, gla_norm_w=gla_norm_w, w_out=w_out, norm2_w=norm2_w, w_ffn_gate=w_ffn_gate, w_ffn_up=w_ffn_up, w_ffn_down=w_ffn_down, final_norm_w=final_norm_w, loss_target=loss_target, m_norm1_w=m_norm1_w, m_w_in=m_w_in, m_gla_gate_w2=m_gla_gate_w2, m_gla_gate_b=m_gla_gate_b, m_attn_sinks=m_attn_sinks, m_gla_norm_w=m_gla_norm_w, m_w_out=m_w_out, m_norm2_w=m_norm2_w, m_w_ffn_gate=m_w_ffn_gate, m_w_ffn_up=m_w_ffn_up, m_w_ffn_down=m_w_ffn_down, m_final_norm_w=m_final_norm_w, v_norm1_w=v_norm1_w, v_w_in=v_w_in, v_gla_gate_w2=v_gla_gate_w2, v_gla_gate_b=v_gla_gate_b, v_attn_sinks=v_attn_sinks, v_gla_norm_w=v_gla_norm_w, v_w_out=v_w_out, v_norm2_w=v_norm2_w, v_w_ffn_gate=v_w_ffn_gate, v_w_ffn_up=v_w_ffn_up, v_w_ffn_down=v_w_ffn_down, v_final_norm_w=v_final_norm_w)
    weights = {n: given[n] for n in TWIN_WEIGHTS}
    shared = {n: given[n] for n in SHARED_INPUTS}
    per_example = {n: given[n] for n in ['x']}
    grad_fn = _jax.value_and_grad(_loss, argnums=(0, 1))

    def one_microbatch(ex, loss_target):
        ex = dict(ex)
        diff = ex.pop(TWIN_DIFF_INPUT)
        return grad_fn(weights, diff, {**shared, **ex}, loss_target)

    if N_MICROBATCH == 1:
        loss, (grad_w, grad_x) = one_microbatch(per_example, given["loss_target"])
    else:
        def body(carry, xs):
            loss_sum, grad_sum = carry
            l_k, (gw_k, gx_k) = one_microbatch(xs[0], xs[1])
            with _jax.named_scope("update"):
                return (loss_sum + l_k, _jax.tree.map(_jnp.add, grad_sum, gw_k)), gx_k

        init = (_jnp.zeros((), _jnp.float32), _jax.tree.map(_jnp.zeros_like, weights))
        (loss, grad_w), grad_x = _jax.lax.scan(body, init, (per_example, given["loss_target"]))
    with _jax.named_scope("update"):
        delta_w, new_m, new_v = {}, {}, {}
        for n in TWIN_WEIGHTS:
            delta_w[n], new_m[n], new_v[n] = _adamw(weights[n], grad_w[n], given["m_" + n], given["v_" + n])
    return (loss, grad_x, *[grad_w[n] for n in TWIN_WEIGHTS], *[delta_w[n] for n in TWIN_WEIGHTS],
            *[new_m[n] for n in TWIN_WEIGHTS], *[new_v[n] for n in TWIN_WEIGHTS])
```

```python
import functools

import jax
import jax.numpy as jnp
from jax import lax
from jax.experimental import pallas as pl
from jax.experimental.pallas import tpu as pltpu

F32, BF16 = jnp.float32, jnp.bfloat16
HIGHEST = lax.Precision.HIGHEST

D = 2048
HD, NQ, NKV, GRP, WIN = 64, 32, 4, 8, 128
GH, DK, DV, RANK, GC = 4, 256, 512, 16, 64
FH, NDEV = 5632, 8
FS = FH // NDEV
DIN = 12816
WS = DIN // NDEV
EPS = 1e-6
MASKV = -1e30
LANE = 128

C_GR, C_GA, C_GB, C_AQ, C_GV, C_GQ, C_GK, C_AK, C_AV, NMAIN = 0, 2048, 4096, 6144, 8192, 10240, 11264, 12288, 12544, 12800

LR, B1, B2, AEPS, WD, STEP = 0.001, 0.9, 0.999, 1e-08, 0.01, 10

S_N1, S_GB, S_SK, S_GN, S_N2, S_FN, S_LOSS, SMALL_N = 0, 2048, 3072, 3104, 3616, 5664, 7712, 8192
SMALL_ROWS = SMALL_N // LANE
GW2_ROWS = RANK * GH * DK // LANE
PACK_ROWS = SMALL_ROWS + GW2_ROWS

MESH = pl.DeviceIdType.MESH


def _to_main(w):
    main = jnp.concatenate([w[..., 6672:12816], w[..., 0:2048], w[..., 4608:6656], w[..., 2560:4608], w[..., 2048:2560]], axis=-1)
    return main, w[..., 6656:6672]


def _from_main(m, lr):
    return jnp.concatenate([m[..., C_AQ:C_GV], m[..., C_AK:NMAIN], m[..., C_GQ:C_AK], m[..., C_GV:C_GQ], lr, m[..., 0:C_AQ]], axis=-1)


def _dot(a, b, ta=False, tb=False, prec=None):
    dn = (((0,) if ta else (1,), (1,) if tb else (0,)), ((), ()))
    return lax.dot_general(a, b, dn, preferred_element_type=F32, precision=prec)


def _sigmoid(x):
    return 1.0 / (1.0 + jnp.exp(-x))


VMEM_LIMIT = 56 * 1024 * 1024


def _cp(*sem):
    return pltpu.CompilerParams(dimension_semantics=sem, vmem_limit_bytes=VMEM_LIMIT)


def _mm(a, b, *, ta=False, tb=False, tm, tn, tk, out_dtype=F32, res=None, name):
    M, K = (a.shape[1], a.shape[0]) if ta else a.shape
    N = b.shape[0] if tb else b.shape[1]
    tm, tn, tk = min(tm, M), min(tn, N), min(tk, K)
    nk = K // tk
    assert M % tm == 0 and N % tn == 0 and K % tk == 0
    a_spec = pl.BlockSpec((tk, tm), lambda i, j, k: (k, i)) if ta else pl.BlockSpec((tm, tk), lambda i, j, k: (i, k))
    b_spec = pl.BlockSpec((tn, tk), lambda i, j, k: (j, k)) if tb else pl.BlockSpec((tk, tn), lambda i, j, k: (k, j))
    o_spec = pl.BlockSpec((tm, tn), lambda i, j, k: (i, j))
    has_res = res is not None

    def body(*refs):
        a_ref, b_ref = refs[0], refs[1]
        r_ref = refs[2] if has_res else None
        o_ref = refs[3] if has_res else refs[2]
        p = _dot(a_ref[...].astype(BF16), b_ref[...].astype(BF16), ta, tb)
        if nk == 1:
            if has_res:
                p = p + r_ref[...]
            o_ref[...] = p.astype(out_dtype)
        else:
            acc = refs[-1]
            k = pl.program_id(2)

            @pl.when(k == 0)
            def _():
                acc[...] = (p + r_ref[...]) if has_res else p

            @pl.when(k > 0)
            def _():
                acc[...] += p

            @pl.when(k == nk - 1)
            def _():
                o_ref[...] = acc[...].astype(out_dtype)

    return pl.pallas_call(
        body, name=name,
        out_shape=jax.ShapeDtypeStruct((M, N), out_dtype),
        grid=(M // tm, N // tn, nk),
        in_specs=[a_spec, b_spec] + ([o_spec] if has_res else []),
        out_specs=o_spec,
        scratch_shapes=[pltpu.VMEM((tm, tn), F32)] if nk > 1 else [],
        compiler_params=_cp("parallel", "parallel", "arbitrary"),
    )(*((a, b, res) if has_res else (a, b)))


def _rmsnorm_fwd(x, w, name, tm=256):
    Tn = x.shape[0]

    def body(x_ref, w_ref, o_ref):
        xv = x_ref[...]
        r = lax.rsqrt(jnp.mean(xv * xv, axis=1, keepdims=True) + EPS)
        o_ref[...] = (xv * r * w_ref[...]).astype(BF16)

    return pl.pallas_call(
        body, name=name, out_shape=jax.ShapeDtypeStruct((Tn, D), BF16), grid=(Tn // tm,),
        in_specs=[pl.BlockSpec((tm, D), lambda i: (i, 0)), pl.BlockSpec((1, D), lambda i: (0, 0))],
        out_specs=pl.BlockSpec((tm, D), lambda i: (i, 0)), compiler_params=_cp("parallel"),
    )(x, w)


def _rmsnorm_bwd(dy, h, w, res, name, tm=256):
    Tn = h.shape[0]

    def body(dy_ref, h_ref, w_ref, res_ref, dh_ref, dhb_ref, dw_ref):
        hv, dyv = h_ref[...], dy_ref[...]
        r = lax.rsqrt(jnp.mean(hv * hv, axis=1, keepdims=True) + EPS)
        g = dyv * w_ref[...]
        dh = res_ref[...] + r * g - hv * (r * r * r * jnp.mean(g * hv, axis=1, keepdims=True))
        dh_ref[...] = dh
        dhb_ref[...] = dh.astype(BF16)
        part = jnp.sum(dyv * hv * r, axis=0, keepdims=True)

        @pl.when(pl.program_id(0) == 0)
        def _():
            dw_ref[...] = part

        @pl.when(pl.program_id(0) > 0)
        def _():
            dw_ref[...] += part

    row = pl.BlockSpec((tm, D), lambda i: (i, 0))
    vec = pl.BlockSpec((1, D), lambda i: (0, 0))
    return pl.pallas_call(
        body, name=name,
        out_shape=(jax.ShapeDtypeStruct((Tn, D), F32), jax.ShapeDtypeStruct((Tn, D), BF16), jax.ShapeDtypeStruct((1, D), F32)),
        grid=(Tn // tm,), in_specs=[row, row, vec, row], out_specs=(row, row, vec), compiler_params=_cp("arbitrary"),
    )(dy, h, w, res)


def _loss_head(h2, wf, tgt, name="loss_head", tm=256):
    Tn = h2.shape[0]

    def body(h_ref, w_ref, t_ref, dh_ref, dhb_ref, dw_ref, loss_ref):
        hv, wv = h_ref[...], w_ref[...]
        r = lax.rsqrt(jnp.mean(hv * hv, axis=1, keepdims=True) + EPS)
        hn = hv * r
        e = hn * wv - t_ref[...]
        dy = e * (1.0 / D)
        g = dy * wv
        dh = r * g - hv * (r * r * r * jnp.mean(g * hv, axis=1, keepdims=True))
        dh_ref[...] = dh
        dhb_ref[...] = dh.astype(BF16)
        part = jnp.sum(dy * hn, axis=0, keepdims=True)
        lpart = (0.5 / D) * jnp.sum(jnp.sum(e * e, axis=1, keepdims=True), axis=0, keepdims=True)

        @pl.when(pl.program_id(0) == 0)
        def _():
            dw_ref[...] = part
            loss_ref[...] = lpart

        @pl.when(pl.program_id(0) > 0)
        def _():
            dw_ref[...] += part
            loss_ref[...] += lpart

    row = pl.BlockSpec((tm, D), lambda i: (i, 0))
    vec = pl.BlockSpec((1, D), lambda i: (0, 0))
    one = pl.BlockSpec((1, 1), lambda i: (0, 0))
    return pl.pallas_call(
        body, name=name,
        out_shape=(jax.ShapeDtypeStruct((Tn, D), F32), jax.ShapeDtypeStruct((Tn, D), BF16), jax.ShapeDtypeStruct((1, D), F32),
                   jax.ShapeDtypeStruct((1, 1), F32)),
        grid=(Tn // tm,), in_specs=[row, vec, row], out_specs=(row, row, vec, one), compiler_params=_cp("arbitrary"),
    )(h2, wf, tgt)


def _attn_mask(n):
    qi = lax.broadcasted_iota(jnp.int32, (GRP * WIN, 2 * WIN), 0) % WIN
    ki = lax.broadcasted_iota(jnp.int32, (GRP * WIN, 2 * WIN), 1)
    rel = qi + WIN - ki
    return (rel >= 0) & (rel < WIN) & ((n > 0) | (ki >= WIN))


def _attn_probs(q_ref, kc_ref, kp_ref, sink_ref, h, mask):
    kk = jnp.concatenate([kp_ref[:, h * HD:(h + 1) * HD], kc_ref[:, h * HD:(h + 1) * HD]], axis=0).astype(BF16)
    qs = jnp.concatenate([q_ref[:, (h * GRP + g) * HD:(h * GRP + g + 1) * HD] for g in range(GRP)], axis=0).astype(BF16)
    s = _dot(qs, kk, tb=True) * (HD ** -0.5)
    s = jnp.where(mask, s, MASKV)
    sink = jnp.concatenate([jnp.full((WIN, 1), sink_ref[0, h * GRP + g], F32) for g in range(GRP)], axis=0)
    m = jnp.maximum(jnp.max(s, axis=1, keepdims=True), sink)
    e = jnp.exp(s - m)
    es = jnp.exp(sink - m)
    inv = 1.0 / (jnp.sum(e, axis=1, keepdims=True) + es)
    return e * inv, es * inv, qs, kk


def _attn_specs(nb, last):
    cur = lambda n: jnp.minimum(n, last)
    prev = lambda n: jnp.maximum(jnp.minimum(n, last) - 1, 0)
    return [
        pl.BlockSpec((WIN, NQ * HD), lambda n: (cur(n), C_AQ // (NQ * HD))),
        pl.BlockSpec((WIN, NKV * HD), lambda n: (cur(n), C_AK // (NKV * HD))),
        pl.BlockSpec((WIN, NKV * HD), lambda n: (prev(n), C_AK // (NKV * HD))),
        pl.BlockSpec((WIN, NKV * HD), lambda n: (cur(n), C_AV // (NKV * HD))),
        pl.BlockSpec((WIN, NKV * HD), lambda n: (prev(n), C_AV // (NKV * HD))),
    ]


def _attn_fwd(proj, sinks, name="attn_fwd"):
    Tn = proj.shape[0]
    nb = Tn // WIN

    def body(q_ref, kc_ref, kp_ref, vc_ref, vp_ref, sink_ref, o_ref):
        mask = _attn_mask(pl.program_id(0))
        for h in range(NKV):
            p, _, _, _ = _attn_probs(q_ref, kc_ref, kp_ref, sink_ref, h, mask)
            vv = jnp.concatenate([vp_ref[:, h * HD:(h + 1) * HD], vc_ref[:, h * HD:(h + 1) * HD]], axis=0).astype(BF16)
            o = _dot(p.astype(BF16), vv)
            for g in range(GRP):
                o_ref[:, (h * GRP + g) * HD:(h * GRP + g + 1) * HD] = o[g * WIN:(g + 1) * WIN, :]

    return pl.pallas_call(
        body, name=name, out_shape=jax.ShapeDtypeStruct((Tn, D), F32), grid=(nb,),
        in_specs=_attn_specs(nb, nb - 1) + [pl.BlockSpec(memory_space=pltpu.SMEM)],
        out_specs=pl.BlockSpec((WIN, D), lambda n: (n, 0)), compiler_params=_cp("parallel"),
    )(proj, proj, proj, proj, proj, sinks)


def _attn_bwd(proj, sinks, o, do, name="attn_bwd"):
    Tn = proj.shape[0]
    nb = Tn // WIN
    KW = NKV * HD

    def body(q_ref, kc_ref, kp_ref, vc_ref, vp_ref, o_ref, do_ref, sink_ref, dq_ref, dkv_ref, dsk_ref, carry, cur):
        n = pl.program_id(0)

        @pl.when(n == 0)
        def _():
            carry[...] = jnp.zeros_like(carry)
            dsk_ref[...] = jnp.zeros_like(dsk_ref)

        @pl.when(n < nb)
        def _():
            mask = _attn_mask(n)
            for h in range(NKV):
                p, ps, qs, kk = _attn_probs(q_ref, kc_ref, kp_ref, sink_ref, h, mask)
                vv = jnp.concatenate([vp_ref[:, h * HD:(h + 1) * HD], vc_ref[:, h * HD:(h + 1) * HD]], axis=0).astype(BF16)
                cols = [slice((h * GRP + g) * HD, (h * GRP + g + 1) * HD) for g in range(GRP)]
                dos = jnp.concatenate([do_ref[:, c] for c in cols], axis=0)
                os_ = jnp.concatenate([o_ref[:, c] for c in cols], axis=0)
                delta = jnp.sum(dos * os_, axis=1, keepdims=True)
                dosb = dos.astype(BF16)
                dp = _dot(dosb, vv, tb=True)
                ds = (p * (dp - delta) * (HD ** -0.5)).astype(BF16)
                dq = _dot(ds, kk)
                dkk = _dot(ds, qs, ta=True)
                dvv = _dot(p.astype(BF16), dosb, ta=True)
                dsk = ps * delta
                for g in range(GRP):
                    dq_ref[:, cols[g]] = dq[g * WIN:(g + 1) * WIN, :].astype(BF16)
                    i = h * GRP + g
                    dsk_ref[:, i:i + 1] -= jnp.sum(dsk[g * WIN:(g + 1) * WIN, :], axis=0, keepdims=True)
                dkv_ref[:, h * HD:(h + 1) * HD] = (carry[:, h * HD:(h + 1) * HD] + dkk[:WIN, :]).astype(BF16)
                dkv_ref[:, KW + h * HD:KW + (h + 1) * HD] = (carry[:, KW + h * HD:KW + (h + 1) * HD] + dvv[:WIN, :]).astype(BF16)
                cur[:, h * HD:(h + 1) * HD] = dkk[WIN:, :]
                cur[:, KW + h * HD:KW + (h + 1) * HD] = dvv[WIN:, :]
            carry[...] = cur[...]

        @pl.when(n == nb)
        def _():
            dkv_ref[...] = carry[...].astype(BF16)

    last = nb - 1
    row = pl.BlockSpec((WIN, D), lambda n: (jnp.minimum(n, last), 0))
    return pl.pallas_call(
        body, name=name,
        out_shape=(jax.ShapeDtypeStruct((Tn, D), BF16), jax.ShapeDtypeStruct((Tn, 2 * KW), BF16), jax.ShapeDtypeStruct((1, NQ), F32)),
        grid=(nb + 1,),
        in_specs=_attn_specs(nb, last) + [row, row, pl.BlockSpec(memory_space=pltpu.SMEM)],
        out_specs=(row, pl.BlockSpec((WIN, 2 * KW), lambda n: (jnp.maximum(n - 1, 0), 0)), pl.BlockSpec((1, NQ), lambda n: (0, 0))),
        scratch_shapes=[pltpu.VMEM((WIN, 2 * KW), F32), pltpu.VMEM((WIN, 2 * KW), F32)],
        compiler_params=_cp("arbitrary"),
    )(proj, proj, proj, proj, proj, o, do, sinks)


def _tri(lower):
    r = lax.broadcasted_iota(jnp.int32, (GC, GC), 0)
    c = lax.broadcasted_iota(jnp.int32, (GC, GC), 1)
    return r >= c if lower else r <= c


def _gla_gates(lr, w2_ref, gb_ref, h):
    logit = _dot(lr, w2_ref[:, h * DK:(h + 1) * DK].astype(BF16)) + gb_ref[:, h * DK:(h + 1) * DK]
    la = (jnp.minimum(logit, 0.0) - jnp.log(1.0 + jnp.exp(-jnp.abs(logit)))) * (1.0 / 16.0)
    g = _dot(_tri(True).astype(F32), la, prec=HIGHEST)
    return logit, g


def _gla_specs(nc, rev):
    idx = (lambda n: nc - 1 - n) if rev else (lambda n: n)
    return [
        pl.BlockSpec((GC, GH * DK), lambda n: (idx(n), C_GQ // (GH * DK))),
        pl.BlockSpec((GC, GH * DK), lambda n: (idx(n), C_GK // (GH * DK))),
        pl.BlockSpec((GC, GH * DV), lambda n: (idx(n), C_GV // (GH * DV))),
        pl.BlockSpec((GC, LANE), lambda n: (idx(n), 0)),
        pl.BlockSpec((LANE, GH * DK), lambda n: (0, 0)),
        pl.BlockSpec((1, GH * DK), lambda n: (0, 0)),
    ]


def _gla_fwd(proj, plr, w2p, gb, name="gla_fwd"):
    Tn = proj.shape[0]
    nc = Tn // GC

    def body(q_ref, k_ref, v_ref, lr_ref, w2_ref, gb_ref, o_ref, st_ref, S):
        @pl.when(pl.program_id(0) == 0)
        def _():
            S[...] = jnp.zeros_like(S)

        lr = lr_ref[...].astype(BF16)
        causal = _tri(True)
        for h in range(GH):
            _, g = _gla_gates(lr, w2_ref, gb_ref, h)
            gl = g[GC - 1:GC, :]
            k = k_ref[:, h * DK:(h + 1) * DK]
            v = v_ref[:, h * DV:(h + 1) * DV].astype(BF16)
            qd = (q_ref[:, h * DK:(h + 1) * DK] * (DK ** -0.5) * jnp.exp(g)).astype(BF16)
            ki = (k * jnp.exp(-g)).astype(BF16)
            ke = (k * jnp.exp(gl - g)).astype(BF16)
            att = jnp.where(causal, _dot(qd, ki, tb=True), 0.0).astype(BF16)
            sp = S[h]
            st_ref[0, h] = sp
            o_ref[:, h * DV:(h + 1) * DV] = _dot(att, v) + _dot(qd, sp.astype(BF16), tb=True)
            S[h] = sp * jnp.exp(gl) + _dot(v, ke, ta=True)

    return pl.pallas_call(
        body, name=name,
        out_shape=(jax.ShapeDtypeStruct((Tn, GH * DV), F32), jax.ShapeDtypeStruct((nc, GH, DV, DK), F32)),
        grid=(nc,), in_specs=_gla_specs(nc, False),
        out_specs=(pl.BlockSpec((GC, GH * DV), lambda n: (n, 0)), pl.BlockSpec((1, GH, DV, DK), lambda n: (n, 0, 0, 0))),
        scratch_shapes=[pltpu.VMEM((GH, DV, DK), F32)], compiler_params=_cp("arbitrary"),
    )(proj, proj, proj, plr, w2p, gb)


def _gla_bwd(proj, plr, w2p, gb, states, do, name="gla_bwd"):
    Tn = proj.shape[0]
    nc = Tn // GC

    def body(q_ref, k_ref, v_ref, lr_ref, w2_ref, gb_ref, st_ref, do_ref, dqk_ref, dv_ref, dlr_ref, dw2_ref, dgb_ref, dS):
        @pl.when(pl.program_id(0) == 0)
        def _():
            dS[...] = jnp.zeros_like(dS)
            dw2_ref[...] = jnp.zeros_like(dw2_ref)
            dgb_ref[...] = jnp.zeros_like(dgb_ref)

        lrf = lr_ref[...]
        lr = lrf.astype(BF16)
        causal = _tri(True)
        last_row = lax.broadcasted_iota(jnp.int32, (GC, DK), 0) == GC - 1
        dlr = jnp.zeros((GC, LANE), F32)
        for h in range(GH):
            logit, g = _gla_gates(lr, w2_ref, gb_ref, h)
            gl = g[GC - 1:GC, :]
            egl = jnp.exp(gl)
            eg, eng, ege = jnp.exp(g), jnp.exp(-g), jnp.exp(gl - g)
            k = k_ref[:, h * DK:(h + 1) * DK]
            v = v_ref[:, h * DV:(h + 1) * DV].astype(BF16)
            dob = do_ref[:, h * DV:(h + 1) * DV].astype(BF16)
            qd = q_ref[:, h * DK:(h + 1) * DK] * (DK ** -0.5) * eg
            ki = k * eng
            ke = k * ege
            qdb, kib, keb = qd.astype(BF16), ki.astype(BF16), ke.astype(BF16)
            att = jnp.where(causal, _dot(qdb, kib, tb=True), 0.0).astype(BF16)
            datt = jnp.where(causal, _dot(dob, v, tb=True), 0.0).astype(BF16)
            sp = st_ref[0, h]
            dsn = dS[h]
            dsnb = dsn.astype(BF16)
            dv_ref[:, h * DV:(h + 1) * DV] = (_dot(att, dob, ta=True) + _dot(keb, dsnb, tb=True)).astype(BF16)
            dqd = _dot(datt, kib) + _dot(dob, sp.astype(BF16))
            dki = _dot(datt, qdb, ta=True)
            dke = _dot(v, dsnb)
            ddec = jnp.sum(dsn * sp, axis=0, keepdims=True)
            dS[h] = dsn * egl + _dot(dob, qdb, ta=True)
            dke_ke = dke * ke
            dgl = jnp.sum(dke_ke, axis=0, keepdims=True) + ddec * egl
            dg = dqd * qd - dki * ki - dke_ke + jnp.where(last_row, dgl, 0.0)
            dqk_ref[:, h * DK:(h + 1) * DK] = (dqd * ((DK ** -0.5) * eg)).astype(BF16)
            dqk_ref[:, GH * DK + h * DK:GH * DK + (h + 1) * DK] = (dki * eng + dke * ege).astype(BF16)
            dla = _dot(_tri(False).astype(F32), dg, prec=HIGHEST)
            dlogit = dla * (1.0 / 16.0) * _sigmoid(-logit)
            dlb = dlogit.astype(BF16)
            dlr = dlr + _dot(dlb, w2_ref[:, h * DK:(h + 1) * DK].astype(BF16), tb=True)
            dw2_ref[:, h * DK:(h + 1) * DK] += _dot(lr, dlb, ta=True)
            dgb_ref[:, h * DK:(h + 1) * DK] += jnp.sum(dlogit, axis=0, keepdims=True)
        dlr_ref[...] = dlr.astype(BF16)

    rev = lambda n: nc - 1 - n
    row = pl.BlockSpec((GC, GH * DV), lambda n: (rev(n), 0))
    return pl.pallas_call(
        body, name=name,
        out_shape=(jax.ShapeDtypeStruct((Tn, 2 * GH * DK), BF16), jax.ShapeDtypeStruct((Tn, GH * DV), BF16),
                   jax.ShapeDtypeStruct((Tn, LANE), BF16), jax.ShapeDtypeStruct((LANE, GH * DK), F32),
                   jax.ShapeDtypeStruct((1, GH * DK), F32)),
        grid=(nc,),
        in_specs=_gla_specs(nc, True) + [pl.BlockSpec((1, GH, DV, DK), lambda n: (rev(n), 0, 0, 0)), row],
        out_specs=(row, row, pl.BlockSpec((GC, LANE), lambda n: (rev(n), 0)), pl.BlockSpec((LANE, GH * DK), lambda n: (0, 0)),
                   pl.BlockSpec((1, GH * DK), lambda n: (0, 0))),
        scratch_shapes=[pltpu.VMEM((GH, DV, DK), F32)], compiler_params=_cp("arbitrary"),
    )(proj, proj, proj, plr, w2p, gb, states, do)


def _merge_specs(tm):
    row = pl.BlockSpec((tm, D), lambda i: (i, 0))
    gates = [pl.BlockSpec((tm, D), lambda i, c=c: (i, c // D)) for c in (C_GR, C_GA, C_GB)]
    return row, gates, pl.BlockSpec((1, DV), lambda i: (0, 0))


def _merge_fwd(a, go, proj, gnw, name="merge_fwd", tm=256):
    Tn = a.shape[0]

    def body(a_ref, go_ref, gr_ref, ga_ref, gb_ref, w_ref, m_ref):
        for h in range(GH):
            sl = slice(h * DV, (h + 1) * DV)
            gov = go_ref[:, sl]
            r = lax.rsqrt(jnp.mean(gov * gov, axis=1, keepdims=True) + EPS)
            gr = gr_ref[:, sl]
            g2 = gov * r * w_ref[...] * (gr * _sigmoid(gr))
            m_ref[:, sl] = (_sigmoid(ga_ref[:, sl]) * a_ref[:, sl] + _sigmoid(gb_ref[:, sl]) * g2).astype(BF16)

    row, gates, vec = _merge_specs(tm)
    return pl.pallas_call(
        body, name=name, out_shape=jax.ShapeDtypeStruct((Tn, D), BF16), grid=(Tn // tm,),
        in_specs=[row, row] + gates + [vec], out_specs=row, compiler_params=_cp("parallel"),
    )(a, go, proj, proj, proj, gnw)


def _merge_bwd(dm, a, go, proj, gnw, name="merge_bwd", tm=256):
    Tn = a.shape[0]

    def body(dm_ref, a_ref, go_ref, gr_ref, ga_ref, gb_ref, w_ref, da_ref, dgo_ref, dg_ref, dw_ref):
        wv = w_ref[...]
        dw = jnp.zeros((1, DV), F32)
        for h in range(GH):
            sl = slice(h * DV, (h + 1) * DV)
            dmv, av, gov, gr = dm_ref[:, sl], a_ref[:, sl], go_ref[:, sl], gr_ref[:, sl]
            sa, sb, sg = _sigmoid(ga_ref[:, sl]), _sigmoid(gb_ref[:, sl]), _sigmoid(gr)
            r = lax.rsqrt(jnp.mean(gov * gov, axis=1, keepdims=True) + EPS)
            gn0 = gov * r
            gn = gn0 * wv
            silu = gr * sg
            dg2 = dmv * sb
            da_ref[:, sl] = dmv * sa
            dg_ref[:, D + h * DV:D + (h + 1) * DV] = (dmv * av * sa * (1.0 - sa)).astype(BF16)
            dg_ref[:, 2 * D + h * DV:2 * D + (h + 1) * DV] = (dg2 * gn * silu * (1.0 - sb)).astype(BF16)
            dg_ref[:, sl] = (dg2 * gn * (sg * (1.0 + gr * (1.0 - sg)))).astype(BF16)
            dgn = dg2 * silu
            dw = dw + jnp.sum(dgn * gn0, axis=0, keepdims=True)
            gg = dgn * wv
            dgo_ref[:, sl] = r * gg - gov * (r * r * r * jnp.mean(gg * gov, axis=1, keepdims=True))

        @pl.when(pl.program_id(0) == 0)
        def _():
            dw_ref[...] = dw

        @pl.when(pl.program_id(0) > 0)
        def _():
            dw_ref[...] += dw

    row, gates, vec = _merge_specs(tm)
    return pl.pallas_call(
        body, name=name,
        out_shape=(jax.ShapeDtypeStruct((Tn, D), F32), jax.ShapeDtypeStruct((Tn, D), F32), jax.ShapeDtypeStruct((Tn, 3 * D), BF16),
                   jax.ShapeDtypeStruct((1, DV), F32)),
        grid=(Tn // tm,), in_specs=[row, row, row] + gates + [vec],
        out_specs=(row, row, pl.BlockSpec((tm, 3 * D), lambda i: (i, 0)), vec), compiler_params=_cp("arbitrary"),
    )(dm, a, go, proj, proj, proj, gnw)


def _ffn_up(v2, wg, wu, name="ffn_up", tm=1024):
    Tn = v2.shape[0]
    tm = min(tm, Tn)

    def body(v_ref, wg_ref, wu_ref, a_ref, b_ref, ff_ref):
        vv = v_ref[...]
        a = _dot(vv, wg_ref[...])
        b = _dot(vv, wu_ref[...])
        a_ref[...] = a
        b_ref[...] = b
        ff_ref[...] = (a * _sigmoid(a) * b).astype(BF16)

    w = pl.BlockSpec((None, D, FS), lambda k, i: (k, 0, 0))
    act = pl.BlockSpec((None, tm, FS), lambda k, i: (k, i, 0))
    return pl.pallas_call(
        body, name=name,
        out_shape=(jax.ShapeDtypeStruct((NDEV, Tn, FS), F32), jax.ShapeDtypeStruct((NDEV, Tn, FS), F32),
                   jax.ShapeDtypeStruct((NDEV, Tn, FS), BF16)),
        grid=(NDEV, Tn // tm), in_specs=[pl.BlockSpec((tm, D), lambda k, i: (i, 0)), w, w], out_specs=(act, act, act),
        compiler_params=_cp("parallel", "parallel"),
    )(v2, wg, wu)


def _ffn_down(ff, wd, h1, name="ffn_down", tm=1024, tn=1024):
    Tn = h1.shape[0]
    tm = min(tm, Tn)

    def body(f_ref, w_ref, r_ref, o_ref, acc):
        k = pl.program_id(2)
        p = _dot(f_ref[...], w_ref[...])

        @pl.when(k == 0)
        def _():
            acc[...] = p + r_ref[...]

        @pl.when(k > 0)
        def _():
            acc[...] += p

        @pl.when(k == NDEV - 1)
        def _():
            o_ref[...] = acc[...]

    o = pl.BlockSpec((tm, tn), lambda i, j, k: (i, j))
    return pl.pallas_call(
        body, name=name, out_shape=jax.ShapeDtypeStruct((Tn, D), F32), grid=(Tn // tm, D // tn, NDEV),
        in_specs=[pl.BlockSpec((None, tm, FS), lambda i, j, k: (k, i, 0)), pl.BlockSpec((None, FS, tn), lambda i, j, k: (k, 0, j)), o],
        out_specs=o, scratch_shapes=[pltpu.VMEM((tm, tn), F32)], compiler_params=_cp("parallel", "parallel", "arbitrary"),
    )(ff, wd, h1)


def _ffn_dact(dh2b, wd, a, b, name="ffn_dact", tm=1024):
    Tn = dh2b.shape[0]
    tm = min(tm, Tn)

    def body(d_ref, w_ref, a_ref, b_ref, da_ref, db_ref):
        dff = _dot(d_ref[...], w_ref[...], tb=True)
        av = a_ref[...]
        sg = _sigmoid(av)
        da_ref[...] = (dff * b_ref[...] * (sg * (1.0 + av * (1.0 - sg)))).astype(BF16)
        db_ref[...] = (dff * (av * sg)).astype(BF16)

    act = pl.BlockSpec((None, tm, FS), lambda k, i: (k, i, 0))
    return pl.pallas_call(
        body, name=name,
        out_shape=(jax.ShapeDtypeStruct((NDEV, Tn, FS), BF16), jax.ShapeDtypeStruct((NDEV, Tn, FS), BF16)),
        grid=(NDEV, Tn // tm),
        in_specs=[pl.BlockSpec((tm, D), lambda k, i: (i, 0)), pl.BlockSpec((None, FS, D), lambda k, i: (k, 0, 0)), act, act],
        out_specs=(act, act), compiler_params=_cp("parallel", "parallel"),
    )(dh2b, wd, a, b)


def _ffn_dwd(ff, dh2b, name="ffn_dwd", tn=1024):
    Tn = dh2b.shape[0]

    def body(f_ref, d_ref, o_ref):
        o_ref[...] = _dot(f_ref[...], d_ref[...], ta=True).astype(BF16)

    return pl.pallas_call(
        body, name=name, out_shape=jax.ShapeDtypeStruct((NDEV, FS, D), BF16), grid=(NDEV, D // tn),
        in_specs=[pl.BlockSpec((None, Tn, FS), lambda k, j: (k, 0, 0)), pl.BlockSpec((Tn, tn), lambda k, j: (0, j))],
        out_specs=pl.BlockSpec((None, FS, tn), lambda k, j: (k, 0, j)), compiler_params=_cp("parallel", "parallel"),
    )(ff, dh2b)


def _ffn_dwgu(v2, da, db, name="ffn_dwgu", tm=1024):
    Tn = v2.shape[0]

    def body(v_ref, da_ref, db_ref, og_ref, ou_ref):
        vv = v_ref[...]
        og_ref[...] = _dot(vv, da_ref[...], ta=True).astype(BF16)
        ou_ref[...] = _dot(vv, db_ref[...], ta=True).astype(BF16)

    act = pl.BlockSpec((None, Tn, FS), lambda k, i: (k, 0, 0))
    o = pl.BlockSpec((None, tm, FS), lambda k, i: (k, i, 0))
    return pl.pallas_call(
        body, name=name,
        out_shape=(jax.ShapeDtypeStruct((NDEV, D, FS), BF16), jax.ShapeDtypeStruct((NDEV, D, FS), BF16)),
        grid=(NDEV, D // tm), in_specs=[pl.BlockSpec((Tn, tm), lambda k, i: (0, i)), act, act], out_specs=(o, o),
        compiler_params=_cp("parallel", "parallel"),
    )(v2, da, db)


def _ffn_dv2(da, db, wg, wu, name="ffn_dv2", tm=1024, tn=1024):
    Tn = da.shape[1]
    tm = min(tm, Tn)

    def body(da_ref, db_ref, wg_ref, wu_ref, o_ref, acc):
        k = pl.program_id(2)
        p = _dot(da_ref[...], wg_ref[...], tb=True) + _dot(db_ref[...], wu_ref[...], tb=True)

        @pl.when(k == 0)
        def _():
            acc[...] = p

        @pl.when(k > 0)
        def _():
            acc[...] += p

        @pl.when(k == NDEV - 1)
        def _():
            o_ref[...] = acc[...]

    act = pl.BlockSpec((None, tm, FS), lambda i, j, k: (k, i, 0))
    w = pl.BlockSpec((None, tn, FS), lambda i, j, k: (k, j, 0))
    return pl.pallas_call(
        body, name=name, out_shape=jax.ShapeDtypeStruct((Tn, D), F32), grid=(Tn // tm, D // tn, NDEV),
        in_specs=[act, act, w, w], out_specs=pl.BlockSpec((tm, tn), lambda i, j, k: (i, j)),
        scratch_shapes=[pltpu.VMEM((tm, tn), F32)], compiler_params=_cp("parallel", "parallel", "arbitrary"),
    )(da, db, wg, wu)


def _adam_math(w, g, m, v):
    m2 = B1 * m + (1.0 - B1) * g
    v2 = B2 * v + (1.0 - B2) * (g * g)
    mh = m2 / (1.0 - B1 ** STEP)
    vh = v2 / (1.0 - B2 ** STEP)
    return -LR * (mh / (jnp.sqrt(vh) + AEPS) + WD * w), m2, v2


def _adamw(w, m, v, parts, name, tr):
    R, C = w.shape

    def body(w_ref, m_ref, v_ref, p_ref, g_ref, d_ref, m2_ref, v2_ref):
        g = ((p_ref[3].astype(F32) + p_ref[0].astype(F32)) + p_ref[1].astype(F32)) + p_ref[2].astype(F32)
        d, m2, v2 = _adam_math(w_ref[...], g, m_ref[...], v_ref[...])
        g_ref[...] = g
        d_ref[...] = d
        m2_ref[...] = m2
        v2_ref[...] = v2

    blk = pl.BlockSpec((tr, C), lambda i: (i, 0))
    out = jax.ShapeDtypeStruct((R, C), F32)
    return pl.pallas_call(
        body, name=name, out_shape=(out, out, out, out), grid=(R // tr,),
        in_specs=[blk, blk, blk, pl.BlockSpec((4, tr, C), lambda i: (0, i, 0))], out_specs=(blk, blk, blk, blk),
        compiler_params=_cp("parallel"),
    )(w, m, v, parts)


def _adamw_plain(w, m, v, g, name):
    def body(w_ref, m_ref, v_ref, g_ref, d_ref, m2_ref, v2_ref):
        d, m2, v2 = _adam_math(w_ref[...], g_ref[...], m_ref[...], v_ref[...])
        d_ref[...] = d
        m2_ref[...] = m2
        v2_ref[...] = v2

    out = jax.ShapeDtypeStruct(w.shape, F32)
    return pl.pallas_call(body, name=name, out_shape=(out, out, out))(w, m, v, g)


def _sum_devices(pack_all, name="sum_small"):
    def body(p_ref, o_ref):
        s = p_ref[0]
        for k in range(1, NDEV):
            s = s + p_ref[k]
        o_ref[...] = s

    return pl.pallas_call(body, name=name, out_shape=jax.ShapeDtypeStruct(pack_all.shape[1:], F32))(pack_all)


def _pair_add(g5, recv, c_idx, name, tr):
    _, _, R, C = g5.shape

    def body(c_ref, g_ref, r_ref, o_ref):
        o_ref[...] = (g_ref[...].astype(F32) + r_ref[...].astype(F32)).astype(BF16)

    grid_spec = pltpu.PrefetchScalarGridSpec(
        num_scalar_prefetch=1, grid=(4, R // tr),
        in_specs=[pl.BlockSpec((None, None, tr, C), lambda q, i, c: (q, c[0], i, 0)), pl.BlockSpec((None, tr, C), lambda q, i, c: (q, i, 0))],
        out_specs=pl.BlockSpec((None, tr, C), lambda q, i, c: (q, i, 0)),
    )
    return pl.pallas_call(
        body, name=name, out_shape=jax.ShapeDtypeStruct((4, R, C), BF16), grid_spec=grid_spec,
        compiler_params=_cp("parallel", "parallel"),
    )(c_idx, g5, recv)


_ANY = pl.BlockSpec(memory_space=pl.ANY)


def _mesh_pos():
    x, y, c = lax.axis_index("x"), lax.axis_index("y"), lax.axis_index("c")
    return x, y, c, [(1 - x, y), (x, 1 - y), (1 - x, 1 - y)]


def _all_gather(shards, name="gather_weights"):
    n = len(shards)

    def body(*refs):
        ins, outs = refs[:n], refs[n:2 * n]
        send, recv, loc = refs[2 * n:]
        x, y, c, chips = _mesh_pos()
        me, sib = (x, y, c), (x, y, 1 - c)

        def cp(a, k, block, to, own=False):
            dst = outs[a].at[4 * block[0] + 2 * block[1] + block[2]]
            return pltpu.make_async_remote_copy(src_ref=ins[a] if own else dst, dst_ref=dst, send_sem=send.at[a, k],
                                                recv_sem=recv.at[a, k], device_id=to, device_id_type=MESH)

        local = [pltpu.make_async_copy(ins[a], outs[a].at[4 * x + 2 * y + c], loc.at[a]) for a in range(n)]
        first = []
        for a in range(n):
            local[a].start()
            first.append(cp(a, 0, me, sib, own=True))
            first += [cp(a, 1 + j, me, (*chip, c), own=True) for j, chip in enumerate(chips)]
        for d in first:
            d.start()
        passed = []
        for a in range(n):
            for j, chip in enumerate(chips):
                cp(a, 1 + j, (*chip, c), me).wait_recv()
                d = cp(a, 4 + j, (*chip, c), sib)
                d.start()
                passed.append(d)
        for a in range(n):
            cp(a, 0, sib, me).wait_recv()
            for j, chip in enumerate(chips):
                cp(a, 4 + j, (*chip, 1 - c), me).wait_recv()
        for d in first + passed:
            d.wait_send()
        for d in local:
            d.wait()

    return pl.pallas_call(
        body, name=name,
        out_shape=tuple(jax.ShapeDtypeStruct((NDEV,) + s.shape, s.dtype) for s in shards),
        in_specs=[_ANY] * n, out_specs=tuple([_ANY] * n),
        scratch_shapes=[pltpu.SemaphoreType.DMA((n, 7)), pltpu.SemaphoreType.DMA((n, 7)), pltpu.SemaphoreType.DMA((n,))],
    )(*shards)


def _pair_exchange(grads, pack, name="reduce_pair"):
    n = len(grads)

    def body(*refs):
        ins, pk = refs[:n], refs[n]
        outs, pk_all = refs[n + 1:2 * n + 1], refs[2 * n + 1]
        send, recv, psend, precv, loc = refs[2 * n + 2:]
        x, y, c, chips = _mesh_pos()
        me_slot = 4 * x + 2 * y + c
        sib = (x, y, 1 - c)
        big = [pltpu.make_async_remote_copy(src_ref=ins[a].at[:, 1 - c], dst_ref=outs[a], send_sem=send.at[a], recv_sem=recv.at[a],
                                            device_id=sib, device_id_type=MESH) for a in range(n)]
        for d in big:
            d.start()
        own = pltpu.make_async_copy(pk, pk_all.at[me_slot], loc)
        own.start()
        peers = [sib] + [(*chip, c) for chip in chips] + [(*chip, 1 - c) for chip in chips]
        small = [pltpu.make_async_remote_copy(src_ref=pk, dst_ref=pk_all.at[me_slot], send_sem=psend.at[k], recv_sem=precv.at[k],
                                              device_id=p, device_id_type=MESH) for k, p in enumerate(peers)]
        for d in small:
            d.start()
        for k, p in enumerate(peers):
            pltpu.make_async_remote_copy(src_ref=pk, dst_ref=pk_all.at[4 * p[0] + 2 * p[1] + p[2]], send_sem=psend.at[k],
                                         recv_sem=precv.at[k], device_id=p, device_id_type=MESH).wait_recv()
        for d in big:
            d.wait_recv()
        for d in big + small:
            d.wait_send()
        own.wait()

    return pl.pallas_call(
        body, name=name,
        out_shape=tuple(jax.ShapeDtypeStruct((4,) + g.shape[2:], g.dtype) for g in grads)
        + (jax.ShapeDtypeStruct((NDEV,) + pack.shape, pack.dtype),),
        in_specs=[_ANY] * (n + 1), out_specs=tuple([_ANY] * (n + 1)),
        scratch_shapes=[pltpu.SemaphoreType.DMA((n,)), pltpu.SemaphoreType.DMA((n,)), pltpu.SemaphoreType.DMA((7,)),
                        pltpu.SemaphoreType.DMA((7,)), pltpu.SemaphoreType.DMA(())],
    )(*grads, pack)


def _chip_exchange(psums, name="reduce_chips"):
    n = len(psums)

    def body(*refs):
        ins, outs = refs[:n], refs[n:2 * n]
        send, recv, loc = refs[2 * n:]
        x, y, c, chips = _mesh_pos()
        sends, locs = [], []
        for a in range(n):
            for j, chip in enumerate(chips):
                sends.append(pltpu.make_async_remote_copy(src_ref=ins[a].at[2 * chip[0] + chip[1]], dst_ref=outs[a].at[j],
                                                          send_sem=send.at[a, j], recv_sem=recv.at[a, j],
                                                          device_id=(*chip, c), device_id_type=MESH))
            locs.append(pltpu.make_async_copy(ins[a].at[2 * x + y], outs[a].at[3], loc.at[a]))
        for d in sends + locs:
            d.start()
        for d in sends:
            d.wait_recv()
        for d in sends:
            d.wait_send()
        for d in locs:
            d.wait()

    return pl.pallas_call(
        body, name=name, out_shape=tuple(jax.ShapeDtypeStruct(p.shape, p.dtype) for p in psums),
        in_specs=[_ANY] * n, out_specs=tuple([_ANY] * n),
        scratch_shapes=[pltpu.SemaphoreType.DMA((n, 3)), pltpu.SemaphoreType.DMA((n, 3)), pltpu.SemaphoreType.DMA((n,))],
    )(*psums)


def _pad_to(v, n):
    return jnp.pad(v, [(0, 0)] * (v.ndim - 1) + [(0, n - v.shape[-1])])


def _pack_small(n1, gb, sk, gn, n2, fn, extra=None):
    parts = [n1.reshape(-1), gb.reshape(-1), sk.reshape(-1), gn.reshape(-1), n2.reshape(-1), fn.reshape(-1)]
    flat = jnp.concatenate(parts + ([extra.reshape(-1)] if extra is not None else []))
    return _pad_to(flat, SMALL_N).reshape(SMALL_ROWS, LANE)


def _unpack_small(p):
    f = p.reshape(-1)
    return (f[S_N1:S_GB].reshape(1, D), f[S_GB:S_SK].reshape(1, GH * DK), f[S_SK:S_GN].reshape(1, NQ), f[S_GN:S_N2].reshape(1, DV),
            f[S_N2:S_FN].reshape(1, D), f[S_FN:S_LOSS].reshape(D))


def _local_step(xs, tgt, norm1_w, gla_gate_b, attn_sinks, gla_norm_w, norm2_w, fnw, w_main, w_lr, wo, w2p, wg_all, wu_all, wd_all):
    u = _rmsnorm_fwd(xs, norm1_w, "norm1_fwd")
    proj = _mm(u, w_main, tm=1024, tn=640, tk=D, name="in_proj")
    plr = _mm(u, w_lr, tm=1024, tn=LANE, tk=D, name="in_proj_lr")
    attn_o = _attn_fwd(proj, attn_sinks)
    gla_o, states = _gla_fwd(proj, plr, w2p, gla_gate_b)
    merged = _merge_fwd(attn_o, gla_o, proj, gla_norm_w)
    h1 = _mm(merged, wo, tm=1024, tn=512, tk=D, res=xs, name="out_proj")
    v2 = _rmsnorm_fwd(h1, norm2_w, "norm2_fwd")
    fa, fb, ff = _ffn_up(v2, wg_all, wu_all)
    h2 = _ffn_down(ff, wd_all, h1)
    dh2, dh2b, d_fnw, loss_part = _loss_head(h2, fnw, tgt)

    da, db = _ffn_dact(dh2b, wd_all, fa, fb)
    d_wd = _ffn_dwd(ff, dh2b)
    d_wg, d_wu = _ffn_dwgu(v2, da, db)
    dv2 = _ffn_dv2(da, db, wg_all, wu_all)
    dh1, dh1b, d_n2 = _rmsnorm_bwd(dv2, h1, norm2_w, dh2, "norm2_bwd")
    dmerged = _mm(dh1b, wo, tb=True, tm=1024, tn=512, tk=D, name="out_proj_dx")
    d_wo = _mm(merged, dh1b, ta=True, tm=1024, tn=512, tk=xs.shape[0], out_dtype=BF16, name="out_proj_dw")
    d_attn, d_gla, d_gates, d_gnw = _merge_bwd(dmerged, attn_o, gla_o, proj, gla_norm_w)
    d_q, d_kv, d_sinks = _attn_bwd(proj, attn_sinks, attn_o, d_attn)
    d_gqk, d_gv, d_plr, d_w2p, d_gb = _gla_bwd(proj, plr, w2p, gla_gate_b, states, d_gla)
    dproj = jnp.concatenate([d_gates, d_q, d_gv, d_gqk, d_kv], axis=1)
    du_lr = _mm(d_plr, w_lr, tb=True, tm=1024, tn=1024, tk=LANE, name="in_proj_lr_dx")
    du = _mm(dproj, w_main, tb=True, tm=1024, tn=1024, tk=1280, res=du_lr, name="in_proj_dx")
    d_wmain = _mm(u, dproj, ta=True, tm=1024, tn=640, tk=xs.shape[0], out_dtype=BF16, name="in_proj_dw")
    d_wlr = _mm(u, d_plr, ta=True, tm=1024, tn=LANE, tk=xs.shape[0], out_dtype=BF16, name="in_proj_lr_dw")
    dx, _, d_n1 = _rmsnorm_bwd(du, xs, norm1_w, dh1, "norm1_bwd")
    return dx, loss_part, d_wmain, d_wlr, d_wo, d_wg, d_wu, d_wd, d_w2p, d_gb, d_sinks, d_gnw, d_n1, d_n2, d_fnw


def kernel(x, norm1_w, w_in, gla_gate_w2, gla_gate_b, attn_sinks, gla_norm_w, w_out, norm2_w, w_ffn_gate, w_ffn_up, w_ffn_down, final_norm_w, loss_target, m_norm1_w, m_w_in, m_gla_gate_w2, m_gla_gate_b, m_attn_sinks, m_gla_norm_w, m_w_out, m_norm2_w, m_w_ffn_gate, m_w_ffn_up, m_w_ffn_down, m_final_norm_w, v_norm1_w, v_w_in, v_gla_gate_w2, v_gla_gate_b, v_attn_sinks, v_gla_norm_w, v_w_out, v_norm2_w, v_w_ffn_gate, v_w_ffn_up, v_w_ffn_down, v_final_norm_w):
    xs, tgt = x[0], loss_target[0]
    fnw = final_norm_w.reshape(1, D)
    c_idx = lax.axis_index("c").astype(jnp.int32).reshape(1)
    dev = 4 * lax.axis_index("x") + 2 * lax.axis_index("y") + lax.axis_index("c")

    win_all, wo_all, wg_all, wu_all, wd_all, w2_all = _all_gather(
        [w_in[0].astype(BF16), w_out[0].astype(BF16), w_ffn_gate[0].astype(BF16), w_ffn_up[0].astype(BF16),
         w_ffn_down[0].astype(BF16), gla_gate_w2[0]])
    w_main, w_lr = _to_main(jnp.transpose(win_all, (1, 0, 2)).reshape(D, DIN))
    w_lr = _pad_to(w_lr, LANE)
    wo = wo_all.reshape(D, D)
    w2p = jnp.pad(jnp.transpose(w2_all, (1, 0, 2)).reshape(RANK, GH * DK), ((0, LANE - RANK), (0, 0)))

    (dx, loss_part, d_wmain, d_wlr, d_wo, d_wg, d_wu, d_wd, d_w2p, d_gb, d_sinks, d_gnw, d_n1, d_n2, d_fnw) = _local_step(
        xs, tgt, norm1_w, gla_gate_b, attn_sinks, gla_norm_w, norm2_w, fnw, w_main, w_lr, wo, w2p, wg_all, wu_all, wd_all)

    d_win = _from_main(d_wmain, d_wlr[:, :RANK]).reshape(D, 4, 2, WS).transpose(1, 2, 0, 3)
    pack = jnp.concatenate([_pack_small(d_n1, d_gb, d_sinks, d_gnw, d_n2, d_fnw, loss_part),
                            d_w2p[:RANK].reshape(GW2_ROWS, LANE)], axis=0)
    g5 = [d_win, d_wo.reshape(4, 2, D // NDEV, D), d_wg.reshape(4, 2, D, FS), d_wu.reshape(4, 2, D, FS), d_wd.reshape(4, 2, FS, D)]
    *recv1, pack_all = _pair_exchange(g5, pack)
    rows = [256, 256, 512, 512, 176]
    names = ["w_in", "w_out", "w_ffn_gate", "w_ffn_up", "w_ffn_down"]
    psums = [_pair_add(g, r, c_idx, "pair_add_" + nm, tr) for g, r, nm, tr in zip(g5, recv1, names, rows)]
    parts = _chip_exchange(psums)
    small = _sum_devices(pack_all)

    big = {}
    for nm, w, m, v, p, tr in zip(names, (w_in, w_out, w_ffn_gate, w_ffn_up, w_ffn_down), (m_w_in, m_w_out, m_w_ffn_gate, m_w_ffn_up, m_w_ffn_down),
                                  (v_w_in, v_w_out, v_w_ffn_gate, v_w_ffn_up, v_w_ffn_down), parts, rows):
        big[nm] = [t[None] for t in _adamw(w[0], m[0], v[0], p, "adamw_" + nm, tr)]
    g_small = small[:SMALL_ROWS]
    sm = _adamw_plain(_pack_small(norm1_w, gla_gate_b, attn_sinks, gla_norm_w, norm2_w, final_norm_w),
                      _pack_small(m_norm1_w, m_gla_gate_b, m_attn_sinks, m_gla_norm_w, m_norm2_w, m_final_norm_w),
                      _pack_small(v_norm1_w, v_gla_gate_b, v_attn_sinks, v_gla_norm_w, v_norm2_w, v_final_norm_w), g_small, "adamw_small")
    g_w2 = lax.dynamic_slice_in_dim(small[SMALL_ROWS:].reshape(RANK, GH * DK), dev * LANE, LANE, axis=1)
    w2 = [g_w2[None]] + [t[None] for t in _adamw_plain(gla_gate_w2[0], m_gla_gate_w2[0], v_gla_gate_w2[0], g_w2, "adamw_w2")]
    loss = g_small.reshape(-1)[S_LOSS]

    sg, sd, sm2, sv2 = [_unpack_small(t) for t in (g_small,) + tuple(sm)]

    def group(i, s):
        return (s[0], big["w_in"][i], w2[i], s[1], s[2], s[3], big["w_out"][i], s[4], big["w_ffn_gate"][i], big["w_ffn_up"][i],
                big["w_ffn_down"][i], s[5])

    return (loss, dx[None], *group(0, sg), *group(1, sd), *group(2, sm2), *group(3, sv2))
```

```python
import functools

import jax
import jax.numpy as jnp
from jax import lax
from jax.experimental import pallas as pl
from jax.experimental.pallas import tpu as pltpu

F32, BF16 = jnp.float32, jnp.bfloat16
HIGHEST = lax.Precision.HIGHEST

D = 2048
HD, NQ, NKV, GRP, WIN = 64, 32, 4, 8, 128
GH, DK, DV, RANK, GC = 4, 256, 512, 16, 64
FH, NDEV = 5632, 8
FS = FH // NDEV
DIN = 12816
WS = DIN // NDEV
EPS = 1e-6
MASKV = -1e30
LANE = 128

C_GR, C_GA, C_GB, C_AQ, C_GV, C_GQ, C_GK, C_AK, C_AV, NMAIN = 0, 2048, 4096, 6144, 8192, 10240, 11264, 12288, 12544, 12800

LR, B1, B2, AEPS, WD, STEP = 0.001, 0.9, 0.999, 1e-08, 0.01, 10

S_N1, S_GB, S_SK, S_GN, S_N2, S_FN, S_LOSS, SMALL_N = 0, 2048, 3072, 3104, 3616, 5664, 7712, 8192
SMALL_ROWS = SMALL_N // LANE
GW2_ROWS = RANK * GH * DK // LANE
PACK_ROWS = SMALL_ROWS + GW2_ROWS

MESH = pl.DeviceIdType.MESH


def _to_main(w):
    main = jnp.concatenate([w[..., 6672:12816], w[..., 0:2048], w[..., 4608:6656], w[..., 2560:4608], w[..., 2048:2560]], axis=-1)
    return main, w[..., 6656:6672]


def _from_main(m, lr):
    return jnp.concatenate([m[..., C_AQ:C_GV], m[..., C_AK:NMAIN], m[..., C_GQ:C_AK], m[..., C_GV:C_GQ], lr, m[..., 0:C_AQ]], axis=-1)


def _dot(a, b, ta=False, tb=False, prec=None):
    dn = (((0,) if ta else (1,), (1,) if tb else (0,)), ((), ()))
    return lax.dot_general(a, b, dn, preferred_element_type=F32, precision=prec)


def _sigmoid(x):
    return 1.0 / (1.0 + jnp.exp(-x))


VMEM_LIMIT = 56 * 1024 * 1024


def _cp(*sem):
    return pltpu.CompilerParams(dimension_semantics=sem, vmem_limit_bytes=VMEM_LIMIT)


def _mm(a, b, *, ta=False, tb=False, tm, tn, tk, out_dtype=F32, res=None, name):
    M, K = (a.shape[1], a.shape[0]) if ta else a.shape
    N = b.shape[0] if tb else b.shape[1]
    tm, tn, tk = min(tm, M), min(tn, N), min(tk, K)
    nk = K // tk
    assert M % tm == 0 and N % tn == 0 and K % tk == 0
    a_spec = pl.BlockSpec((tk, tm), lambda i, j, k: (k, i)) if ta else pl.BlockSpec((tm, tk), lambda i, j, k: (i, k))
    b_spec = pl.BlockSpec((tn, tk), lambda i, j, k: (j, k)) if tb else pl.BlockSpec((tk, tn), lambda i, j, k: (k, j))
    o_spec = pl.BlockSpec((tm, tn), lambda i, j, k: (i, j))
    has_res = res is not None

    def body(*refs):
        a_ref, b_ref = refs[0], refs[1]
        r_ref = refs[2] if has_res else None
        o_ref = refs[3] if has_res else refs[2]
        p = _dot(a_ref[...].astype(BF16), b_ref[...].astype(BF16), ta, tb)
        if nk == 1:
            if has_res:
                p = p + r_ref[...]
            o_ref[...] = p.astype(out_dtype)
        else:
            acc = refs[-1]
            k = pl.program_id(2)

            @pl.when(k == 0)
            def _():
                acc[...] = (p + r_ref[...]) if has_res else p

            @pl.when(k > 0)
            def _():
                acc[...] += p

            @pl.when(k == nk - 1)
            def _():
                o_ref[...] = acc[...].astype(out_dtype)

    return pl.pallas_call(
        body, name=name,
        out_shape=jax.ShapeDtypeStruct((M, N), out_dtype),
        grid=(M // tm, N // tn, nk),
        in_specs=[a_spec, b_spec] + ([o_spec] if has_res else []),
        out_specs=o_spec,
        scratch_shapes=[pltpu.VMEM((tm, tn), F32)] if nk > 1 else [],
        compiler_params=_cp("parallel", "parallel", "arbitrary"),
    )(*((a, b, res) if has_res else (a, b)))


def _rmsnorm_fwd(x, w, name, tm=256):
    Tn = x.shape[0]

    def body(x_ref, w_ref, o_ref):
        xv = x_ref[...]
        r = lax.rsqrt(jnp.mean(xv * xv, axis=1, keepdims=True) + EPS)
        o_ref[...] = (xv * r * w_ref[...]).astype(BF16)

    return pl.pallas_call(
        body, name=name, out_shape=jax.ShapeDtypeStruct((Tn, D), BF16), grid=(Tn // tm,),
        in_specs=[pl.BlockSpec((tm, D), lambda i: (i, 0)), pl.BlockSpec((1, D), lambda i: (0, 0))],
        out_specs=pl.BlockSpec((tm, D), lambda i: (i, 0)), compiler_params=_cp("parallel"),
    )(x, w)


def _rmsnorm_bwd(dy, h, w, res, name, tm=256):
    Tn = h.shape[0]

    def body(dy_ref, h_ref, w_ref, res_ref, dh_ref, dhb_ref, dw_ref):
        hv, dyv = h_ref[...], dy_ref[...]
        r = lax.rsqrt(jnp.mean(hv * hv, axis=1, keepdims=True) + EPS)
        g = dyv * w_ref[...]
        dh = res_ref[...] + r * g - hv * (r * r * r * jnp.mean(g * hv, axis=1, keepdims=True))
        dh_ref[...] = dh
        dhb_ref[...] = dh.astype(BF16)
        part = jnp.sum(dyv * hv * r, axis=0, keepdims=True)

        @pl.when(pl.program_id(0) == 0)
        def _():
            dw_ref[...] = part

        @pl.when(pl.program_id(0) > 0)
        def _():
            dw_ref[...] += part

    row = pl.BlockSpec((tm, D), lambda i: (i, 0))
    vec = pl.BlockSpec((1, D), lambda i: (0, 0))
    return pl.pallas_call(
        body, name=name,
        out_shape=(jax.ShapeDtypeStruct((Tn, D), F32), jax.ShapeDtypeStruct((Tn, D), BF16), jax.ShapeDtypeStruct((1, D), F32)),
        grid=(Tn // tm,), in_specs=[row, row, vec, row], out_specs=(row, row, vec), compiler_params=_cp("arbitrary"),
    )(dy, h, w, res)


def _loss_head(h2, wf, tgt, name="loss_head", tm=256):
    Tn = h2.shape[0]

    def body(h_ref, w_ref, t_ref, dh_ref, dhb_ref, dw_ref, loss_ref):
        hv, wv = h_ref[...], w_ref[...]
        r = lax.rsqrt(jnp.mean(hv * hv, axis=1, keepdims=True) + EPS)
        hn = hv * r
        e = hn * wv - t_ref[...]
        dy = e * (1.0 / D)
        g = dy * wv
        dh = r * g - hv * (r * r * r * jnp.mean(g * hv, axis=1, keepdims=True))
        dh_ref[...] = dh
        dhb_ref[...] = dh.astype(BF16)
        part = jnp.sum(dy * hn, axis=0, keepdims=True)
        lpart = (0.5 / D) * jnp.sum(jnp.sum(e * e, axis=1, keepdims=True), axis=0, keepdims=True)

        @pl.when(pl.program_id(0) == 0)
        def _():
            dw_ref[...] = part
            loss_ref[...] = lpart

        @pl.when(pl.program_id(0) > 0)
        def _():
            dw_ref[...] += part
            loss_ref[...] += lpart

    row = pl.BlockSpec((tm, D), lambda i: (i, 0))
    vec = pl.BlockSpec((1, D), lambda i: (0, 0))
    one = pl.BlockSpec((1, 1), lambda i: (0, 0))
    return pl.pallas_call(
        body, name=name,
        out_shape=(jax.ShapeDtypeStruct((Tn, D), F32), jax.ShapeDtypeStruct((Tn, D), BF16), jax.ShapeDtypeStruct((1, D), F32),
                   jax.ShapeDtypeStruct((1, 1), F32)),
        grid=(Tn // tm,), in_specs=[row, vec, row], out_specs=(row, row, vec, one), compiler_params=_cp("arbitrary"),
    )(h2, wf, tgt)


def _attn_mask(n):
    qi = lax.broadcasted_iota(jnp.int32, (GRP * WIN, 2 * WIN), 0) % WIN
    ki = lax.broadcasted_iota(jnp.int32, (GRP * WIN, 2 * WIN), 1)
    rel = qi + WIN - ki
    return (rel >= 0) & (rel < WIN) & ((n > 0) | (ki >= WIN))


def _attn_probs(q_ref, kc_ref, kp_ref, sink_ref, h, mask):
    kk = jnp.concatenate([kp_ref[:, h * HD:(h + 1) * HD], kc_ref[:, h * HD:(h + 1) * HD]], axis=0).astype(BF16)
    qs = jnp.concatenate([q_ref[:, (h * GRP + g) * HD:(h * GRP + g + 1) * HD] for g in range(GRP)], axis=0).astype(BF16)
    s = _dot(qs, kk, tb=True) * (HD ** -0.5)
    s = jnp.where(mask, s, MASKV)
    sink = jnp.concatenate([jnp.full((WIN, 1), sink_ref[0, h * GRP + g], F32) for g in range(GRP)], axis=0)
    m = jnp.maximum(jnp.max(s, axis=1, keepdims=True), sink)
    e = jnp.exp(s - m)
    es = jnp.exp(sink - m)
    inv = 1.0 / (jnp.sum(e, axis=1, keepdims=True) + es)
    return e * inv, es * inv, qs, kk


def _attn_specs(nb, last):
    cur = lambda n: jnp.minimum(n, last)
    prev = lambda n: jnp.maximum(jnp.minimum(n, last) - 1, 0)
    return [
        pl.BlockSpec((WIN, NQ * HD), lambda n: (cur(n), C_AQ // (NQ * HD))),
        pl.BlockSpec((WIN, NKV * HD), lambda n: (cur(n), C_AK // (NKV * HD))),
        pl.BlockSpec((WIN, NKV * HD), lambda n: (prev(n), C_AK // (NKV * HD))),
        pl.BlockSpec((WIN, NKV * HD), lambda n: (cur(n), C_AV // (NKV * HD))),
        pl.BlockSpec((WIN, NKV * HD), lambda n: (prev(n), C_AV // (NKV * HD))),
    ]


def _attn_fwd(proj, sinks, name="attn_fwd"):
    Tn = proj.shape[0]
    nb = Tn // WIN

    def body(q_ref, kc_ref, kp_ref, vc_ref, vp_ref, sink_ref, o_ref):
        mask = _attn_mask(pl.program_id(0))
        for h in range(NKV):
            p, _, _, _ = _attn_probs(q_ref, kc_ref, kp_ref, sink_ref, h, mask)
            vv = jnp.concatenate([vp_ref[:, h * HD:(h + 1) * HD], vc_ref[:, h * HD:(h + 1) * HD]], axis=0).astype(BF16)
            o = _dot(p.astype(BF16), vv)
            for g in range(GRP):
                o_ref[:, (h * GRP + g) * HD:(h * GRP + g + 1) * HD] = o[g * WIN:(g + 1) * WIN, :]

    return pl.pallas_call(
        body, name=name, out_shape=jax.ShapeDtypeStruct((Tn, D), F32), grid=(nb,),
        in_specs=_attn_specs(nb, nb - 1) + [pl.BlockSpec(memory_space=pltpu.SMEM)],
        out_specs=pl.BlockSpec((WIN, D), lambda n: (n, 0)), compiler_params=_cp("parallel"),
    )(proj, proj, proj, proj, proj, sinks)


def _attn_bwd(proj, sinks, o, do, name="attn_bwd"):
    Tn = proj.shape[0]
    nb = Tn // WIN
    KW = NKV * HD

    def body(q_ref, kc_ref, kp_ref, vc_ref, vp_ref, o_ref, do_ref, sink_ref, dq_ref, dkv_ref, dsk_ref, carry, cur):
        n = pl.program_id(0)

        @pl.when(n == 0)
        def _():
            carry[...] = jnp.zeros_like(carry)
            dsk_ref[...] = jnp.zeros_like(dsk_ref)

        @pl.when(n < nb)
        def _():
            mask = _attn_mask(n)
            for h in range(NKV):
                p, ps, qs, kk = _attn_probs(q_ref, kc_ref, kp_ref, sink_ref, h, mask)
                vv = jnp.concatenate([vp_ref[:, h * HD:(h + 1) * HD], vc_ref[:, h * HD:(h + 1) * HD]], axis=0).astype(BF16)
                cols = [slice((h * GRP + g) * HD, (h * GRP + g + 1) * HD) for g in range(GRP)]
                dos = jnp.concatenate([do_ref[:, c] for c in cols], axis=0)
                os_ = jnp.concatenate([o_ref[:, c] for c in cols], axis=0)
                delta = jnp.sum(dos * os_, axis=1, keepdims=True)
                dosb = dos.astype(BF16)
                dp = _dot(dosb, vv, tb=True)
                ds = (p * (dp - delta) * (HD ** -0.5)).astype(BF16)
                dq = _dot(ds, kk)
                dkk = _dot(ds, qs, ta=True)
                dvv = _dot(p.astype(BF16), dosb, ta=True)
                dsk = ps * delta
                for g in range(GRP):
                    dq_ref[:, cols[g]] = dq[g * WIN:(g + 1) * WIN, :].astype(BF16)
                    i = h * GRP + g
                    dsk_ref[:, i:i + 1] -= jnp.sum(dsk[g * WIN:(g + 1) * WIN, :], axis=0, keepdims=True)
                dkv_ref[:, h * HD:(h + 1) * HD] = (carry[:, h * HD:(h + 1) * HD] + dkk[:WIN, :]).astype(BF16)
                dkv_ref[:, KW + h * HD:KW + (h + 1) * HD] = (carry[:, KW + h * HD:KW + (h + 1) * HD] + dvv[:WIN, :]).astype(BF16)
                cur[:, h * HD:(h + 1) * HD] = dkk[WIN:, :]
                cur[:, KW + h * HD:KW + (h + 1) * HD] = dvv[WIN:, :]
            carry[...] = cur[...]

        @pl.when(n == nb)
        def _():
            dkv_ref[...] = carry[...].astype(BF16)

    last = nb - 1
    row = pl.BlockSpec((WIN, D), lambda n: (jnp.minimum(n, last), 0))
    return pl.pallas_call(
        body, name=name,
        out_shape=(jax.ShapeDtypeStruct((Tn, D), BF16), jax.ShapeDtypeStruct((Tn, 2 * KW), BF16), jax.ShapeDtypeStruct((1, NQ), F32)),
        grid=(nb + 1,),
        in_specs=_attn_specs(nb, last) + [row, row, pl.BlockSpec(memory_space=pltpu.SMEM)],
        out_specs=(row, pl.BlockSpec((WIN, 2 * KW), lambda n: (jnp.maximum(n - 1, 0), 0)), pl.BlockSpec((1, NQ), lambda n: (0, 0))),
        scratch_shapes=[pltpu.VMEM((WIN, 2 * KW), F32), pltpu.VMEM((WIN, 2 * KW), F32)],
        compiler_params=_cp("arbitrary"),
    )(proj, proj, proj, proj, proj, o, do, sinks)


def _tri(lower):
    r = lax.broadcasted_iota(jnp.int32, (GC, GC), 0)
    c = lax.broadcasted_iota(jnp.int32, (GC, GC), 1)
    return r >= c if lower else r <= c


def _gla_gates(lr, w2_ref, gb_ref, h):
    logit = _dot(lr, w2_ref[:, h * DK:(h + 1) * DK].astype(BF16)) + gb_ref[:, h * DK:(h + 1) * DK]
    la = (jnp.minimum(logit, 0.0) - jnp.log(1.0 + jnp.exp(-jnp.abs(logit)))) * (1.0 / 16.0)
    g = _dot(_tri(True).astype(F32), la, prec=HIGHEST)
    return logit, g


def _gla_specs(nc, rev):
    idx = (lambda n: nc - 1 - n) if rev else (lambda n: n)
    return [
        pl.BlockSpec((GC, GH * DK), lambda n: (idx(n), C_GQ // (GH * DK))),
        pl.BlockSpec((GC, GH * DK), lambda n: (idx(n), C_GK // (GH * DK))),
        pl.BlockSpec((GC, GH * DV), lambda n: (idx(n), C_GV // (GH * DV))),
        pl.BlockSpec((GC, LANE), lambda n: (idx(n), 0)),
        pl.BlockSpec((LANE, GH * DK), lambda n: (0, 0)),
        pl.BlockSpec((1, GH * DK), lambda n: (0, 0)),
    ]


def _gla_fwd(proj, plr, w2p, gb, name="gla_fwd"):
    Tn = proj.shape[0]
    nc = Tn // GC

    def body(q_ref, k_ref, v_ref, lr_ref, w2_ref, gb_ref, o_ref, st_ref, S):
        @pl.when(pl.program_id(0) == 0)
        def _():
            S[...] = jnp.zeros_like(S)

        lr = lr_ref[...].astype(BF16)
        causal = _tri(True)
        for h in range(GH):
            _, g = _gla_gates(lr, w2_ref, gb_ref, h)
            gl = g[GC - 1:GC, :]
            k = k_ref[:, h * DK:(h + 1) * DK]
            v = v_ref[:, h * DV:(h + 1) * DV].astype(BF16)
            qd = (q_ref[:, h * DK:(h + 1) * DK] * (DK ** -0.5) * jnp.exp(g)).astype(BF16)
            ki = (k * jnp.exp(-g)).astype(BF16)
            ke = (k * jnp.exp(gl - g)).astype(BF16)
            att = jnp.where(causal, _dot(qd, ki, tb=True), 0.0).astype(BF16)
            sp = S[h]
            st_ref[0, h] = sp
            o_ref[:, h * DV:(h + 1) * DV] = _dot(att, v) + _dot(qd, sp.astype(BF16), tb=True)
            S[h] = sp * jnp.exp(gl) + _dot(v, ke, ta=True)

    return pl.pallas_call(
        body, name=name,
        out_shape=(jax.ShapeDtypeStruct((Tn, GH * DV), F32), jax.ShapeDtypeStruct((nc, GH, DV, DK), F32)),
        grid=(nc,), in_specs=_gla_specs(nc, False),
        out_specs=(pl.BlockSpec((GC, GH * DV), lambda n: (n, 0)), pl.BlockSpec((1, GH, DV, DK), lambda n: (n, 0, 0, 0))),
        scratch_shapes=[pltpu.VMEM((GH, DV, DK), F32)], compiler_params=_cp("arbitrary"),
    )(proj, proj, proj, plr, w2p, gb)


def _gla_bwd(proj, plr, w2p, gb, states, do, name="gla_bwd"):
    Tn = proj.shape[0]
    nc = Tn // GC

    def body(q_ref, k_ref, v_ref, lr_ref, w2_ref, gb_ref, st_ref, do_ref, dqk_ref, dv_ref, dlr_ref, dw2_ref, dgb_ref, dS):
        @pl.when(pl.program_id(0) == 0)
        def _():
            dS[...] = jnp.zeros_like(dS)
            dw2_ref[...] = jnp.zeros_like(dw2_ref)
            dgb_ref[...] = jnp.zeros_like(dgb_ref)

        lrf = lr_ref[...]
        lr = lrf.astype(BF16)
        causal = _tri(True)
        last_row = lax.broadcasted_iota(jnp.int32, (GC, DK), 0) == GC - 1
        dlr = jnp.zeros((GC, LANE), F32)
        for h in range(GH):
            logit, g = _gla_gates(lr, w2_ref, gb_ref, h)
            gl = g[GC - 1:GC, :]
            egl = jnp.exp(gl)
            eg, eng, ege = jnp.exp(g), jnp.exp(-g), jnp.exp(gl - g)
            k = k_ref[:, h * DK:(h + 1) * DK]
            v = v_ref[:, h * DV:(h + 1) * DV].astype(BF16)
            dob = do_ref[:, h * DV:(h + 1) * DV].astype(BF16)
            qd = q_ref[:, h * DK:(h + 1) * DK] * (DK ** -0.5) * eg
            ki = k * eng
            ke = k * ege
            qdb, kib, keb = qd.astype(BF16), ki.astype(BF16), ke.astype(BF16)
            att = jnp.where(causal, _dot(qdb, kib, tb=True), 0.0).astype(BF16)
            datt = jnp.where(causal, _dot(dob, v, tb=True), 0.0).astype(BF16)
            sp = st_ref[0, h]
            dsn = dS[h]
            dsnb = dsn.astype(BF16)
            dv_ref[:, h * DV:(h + 1) * DV] = (_dot(att, dob, ta=True) + _dot(keb, dsnb, tb=True)).astype(BF16)
            dqd = _dot(datt, kib) + _dot(dob, sp.astype(BF16))
            dki = _dot(datt, qdb, ta=True)
            dke = _dot(v, dsnb)
            ddec = jnp.sum(dsn * sp, axis=0, keepdims=True)
            dS[h] = dsn * egl + _dot(dob, qdb, ta=True)
            dke_ke = dke * ke
            dgl = jnp.sum(dke_ke, axis=0, keepdims=True) + ddec * egl
            dg = dqd * qd - dki * ki - dke_ke + jnp.where(last_row, dgl, 0.0)
            dqk_ref[:, h * DK:(h + 1) * DK] = (dqd * ((DK ** -0.5) * eg)).astype(BF16)
            dqk_ref[:, GH * DK + h * DK:GH * DK + (h + 1) * DK] = (dki * eng + dke * ege).astype(BF16)
            dla = _dot(_tri(False).astype(F32), dg, prec=HIGHEST)
            dlogit = dla * (1.0 / 16.0) * _sigmoid(-logit)
            dlb = dlogit.astype(BF16)
            dlr = dlr + _dot(dlb, w2_ref[:, h * DK:(h + 1) * DK].astype(BF16), tb=True)
            dw2_ref[:, h * DK:(h + 1) * DK] += _dot(lr, dlb, ta=True)
            dgb_ref[:, h * DK:(h + 1) * DK] += jnp.sum(dlogit, axis=0, keepdims=True)
        dlr_ref[...] = dlr.astype(BF16)

    rev = lambda n: nc - 1 - n
    row = pl.BlockSpec((GC, GH * DV), lambda n: (rev(n), 0))
    return pl.pallas_call(
        body, name=name,
        out_shape=(jax.ShapeDtypeStruct((Tn, 2 * GH * DK), BF16), jax.ShapeDtypeStruct((Tn, GH * DV), BF16),
                   jax.ShapeDtypeStruct((Tn, LANE), BF16), jax.ShapeDtypeStruct((LANE, GH * DK), F32),
                   jax.ShapeDtypeStruct((1, GH * DK), F32)),
        grid=(nc,),
        in_specs=_gla_specs(nc, True) + [pl.BlockSpec((1, GH, DV, DK), lambda n: (rev(n), 0, 0, 0)), row],
        out_specs=(row, row, pl.BlockSpec((GC, LANE), lambda n: (rev(n), 0)), pl.BlockSpec((LANE, GH * DK), lambda n: (0, 0)),
                   pl.BlockSpec((1, GH * DK), lambda n: (0, 0))),
        scratch_shapes=[pltpu.VMEM((GH, DV, DK), F32)], compiler_params=_cp("arbitrary"),
    )(proj, proj, proj, plr, w2p, gb, states, do)


def _merge_specs(tm):
    row = pl.BlockSpec((tm, D), lambda i: (i, 0))
    gates = [pl.BlockSpec((tm, D), lambda i, c=c: (i, c // D)) for c in (C_GR, C_GA, C_GB)]
    return row, gates, pl.BlockSpec((1, DV), lambda i: (0, 0))


def _merge_fwd(a, go, proj, gnw, name="merge_fwd", tm=256):
    Tn = a.shape[0]

    def body(a_ref, go_ref, gr_ref, ga_ref, gb_ref, w_ref, m_ref):
        for h in range(GH):
            sl = slice(h * DV, (h + 1) * DV)
            gov = go_ref[:, sl]
            r = lax.rsqrt(jnp.mean(gov * gov, axis=1, keepdims=True) + EPS)
            gr = gr_ref[:, sl]
            g2 = gov * r * w_ref[...] * (gr * _sigmoid(gr))
            m_ref[:, sl] = (_sigmoid(ga_ref[:, sl]) * a_ref[:, sl] + _sigmoid(gb_ref[:, sl]) * g2).astype(BF16)

    row, gates, vec = _merge_specs(tm)
    return pl.pallas_call(
        body, name=name, out_shape=jax.ShapeDtypeStruct((Tn, D), BF16), grid=(Tn // tm,),
        in_specs=[row, row] + gates + [vec], out_specs=row, compiler_params=_cp("parallel"),
    )(a, go, proj, proj, proj, gnw)


def _merge_bwd(dm, a, go, proj, gnw, name="merge_bwd", tm=256):
    Tn = a.shape[0]

    def body(dm_ref, a_ref, go_ref, gr_ref, ga_ref, gb_ref, w_ref, da_ref, dgo_ref, dg_ref, dw_ref):
        wv = w_ref[...]
        dw = jnp.zeros((1, DV), F32)
        for h in range(GH):
            sl = slice(h * DV, (h + 1) * DV)
            dmv, av, gov, gr = dm_ref[:, sl], a_ref[:, sl], go_ref[:, sl], gr_ref[:, sl]
            sa, sb, sg = _sigmoid(ga_ref[:, sl]), _sigmoid(gb_ref[:, sl]), _sigmoid(gr)
            r = lax.rsqrt(jnp.mean(gov * gov, axis=1, keepdims=True) + EPS)
            gn0 = gov * r
            gn = gn0 * wv
            silu = gr * sg
            dg2 = dmv * sb
            da_ref[:, sl] = dmv * sa
            dg_ref[:, D + h * DV:D + (h + 1) * DV] = (dmv * av * sa * (1.0 - sa)).astype(BF16)
            dg_ref[:, 2 * D + h * DV:2 * D + (h + 1) * DV] = (dg2 * gn * silu * (1.0 - sb)).astype(BF16)
            dg_ref[:, sl] = (dg2 * gn * (sg * (1.0 + gr * (1.0 - sg)))).astype(BF16)
            dgn = dg2 * silu
            dw = dw + jnp.sum(dgn * gn0, axis=0, keepdims=True)
            gg = dgn * wv
            dgo_ref[:, sl] = r * gg - gov * (r * r * r * jnp.mean(gg * gov, axis=1, keepdims=True))

        @pl.when(pl.program_id(0) == 0)
        def _():
            dw_ref[...] = dw

        @pl.when(pl.program_id(0) > 0)
        def _():
            dw_ref[...] += dw

    row, gates, vec = _merge_specs(tm)
    return pl.pallas_call(
        body, name=name,
        out_shape=(jax.ShapeDtypeStruct((Tn, D), F32), jax.ShapeDtypeStruct((Tn, D), F32), jax.ShapeDtypeStruct((Tn, 3 * D), BF16),
                   jax.ShapeDtypeStruct((1, DV), F32)),
        grid=(Tn // tm,), in_specs=[row, row, row] + gates + [vec],
        out_specs=(row, row, pl.BlockSpec((tm, 3 * D), lambda i: (i, 0)), vec), compiler_params=_cp("arbitrary"),
    )(dm, a, go, proj, proj, proj, gnw)


def _ffn_up(v2, wg, wu, name="ffn_up", tm=1024):
    Tn = v2.shape[0]
    tm = min(tm, Tn)

    def body(v_ref, wg_ref, wu_ref, a_ref, b_ref, ff_ref):
        vv = v_ref[...]
        a = _dot(vv, wg_ref[...])
        b = _dot(vv, wu_ref[...])
        a_ref[...] = a
        b_ref[...] = b
        ff_ref[...] = (a * _sigmoid(a) * b).astype(BF16)

    w = pl.BlockSpec((None, D, FS), lambda k, i: (k, 0, 0))
    act = pl.BlockSpec((None, tm, FS), lambda k, i: (k, i, 0))
    return pl.pallas_call(
        body, name=name,
        out_shape=(jax.ShapeDtypeStruct((NDEV, Tn, FS), F32), jax.ShapeDtypeStruct((NDEV, Tn, FS), F32),
                   jax.ShapeDtypeStruct((NDEV, Tn, FS), BF16)),
        grid=(NDEV, Tn // tm), in_specs=[pl.BlockSpec((tm, D), lambda k, i: (i, 0)), w, w], out_specs=(act, act, act),
        compiler_params=_cp("parallel", "parallel"),
    )(v2, wg, wu)


def _ffn_down(ff, wd, h1, name="ffn_down", tm=1024, tn=1024):
    Tn = h1.shape[0]
    tm = min(tm, Tn)

    def body(f_ref, w_ref, r_ref, o_ref, acc):
        k = pl.program_id(2)
        p = _dot(f_ref[...], w_ref[...])

        @pl.when(k == 0)
        def _():
            acc[...] = p + r_ref[...]

        @pl.when(k > 0)
        def _():
            acc[...] += p

        @pl.when(k == NDEV - 1)
        def _():
            o_ref[...] = acc[...]

    o = pl.BlockSpec((tm, tn), lambda i, j, k: (i, j))
    return pl.pallas_call(
        body, name=name, out_shape=jax.ShapeDtypeStruct((Tn, D), F32), grid=(Tn // tm, D // tn, NDEV),
        in_specs=[pl.BlockSpec((None, tm, FS), lambda i, j, k: (k, i, 0)), pl.BlockSpec((None, FS, tn), lambda i, j, k: (k, 0, j)), o],
        out_specs=o, scratch_shapes=[pltpu.VMEM((tm, tn), F32)], compiler_params=_cp("parallel", "parallel", "arbitrary"),
    )(ff, wd, h1)


def _ffn_dact(dh2b, wd, a, b, name="ffn_dact", tm=1024):
    Tn = dh2b.shape[0]
    tm = min(tm, Tn)

    def body(d_ref, w_ref, a_ref, b_ref, da_ref, db_ref):
        dff = _dot(d_ref[...], w_ref[...], tb=True)
        av = a_ref[...]
        sg = _sigmoid(av)
        da_ref[...] = (dff * b_ref[...] * (sg * (1.0 + av * (1.0 - sg)))).astype(BF16)
        db_ref[...] = (dff * (av * sg)).astype(BF16)

    act = pl.BlockSpec((None, tm, FS), lambda k, i: (k, i, 0))
    return pl.pallas_call(
        body, name=name,
        out_shape=(jax.ShapeDtypeStruct((NDEV, Tn, FS), BF16), jax.ShapeDtypeStruct((NDEV, Tn, FS), BF16)),
        grid=(NDEV, Tn // tm),
        in_specs=[pl.BlockSpec((tm, D), lambda k, i: (i, 0)), pl.BlockSpec((None, FS, D), lambda k, i: (k, 0, 0)), act, act],
        out_specs=(act, act), compiler_params=_cp("parallel", "parallel"),
    )(dh2b, wd, a, b)


def _ffn_dwd(ff, dh2b, name="ffn_dwd", tn=1024):
    Tn = dh2b.shape[0]

    def body(f_ref, d_ref, o_ref):
        o_ref[...] = _dot(f_ref[...], d_ref[...], ta=True).astype(BF16)

    return pl.pallas_call(
        body, name=name, out_shape=jax.ShapeDtypeStruct((NDEV, FS, D), BF16), grid=(NDEV, D // tn),
        in_specs=[pl.BlockSpec((None, Tn, FS), lambda k, j: (k, 0, 0)), pl.BlockSpec((Tn, tn), lambda k, j: (0, j))],
        out_specs=pl.BlockSpec((None, FS, tn), lambda k, j: (k, 0, j)), compiler_params=_cp("parallel", "parallel"),
    )(ff, dh2b)


def _ffn_dwgu(v2, da, db, name="ffn_dwgu", tm=1024):
    Tn = v2.shape[0]

    def body(v_ref, da_ref, db_ref, og_ref, ou_ref):
        vv = v_ref[...]
        og_ref[...] = _dot(vv, da_ref[...], ta=True).astype(BF16)
        ou_ref[...] = _dot(vv, db_ref[...], ta=True).astype(BF16)

    act = pl.BlockSpec((None, Tn, FS), lambda k, i: (k, 0, 0))
    o = pl.BlockSpec((None, tm, FS), lambda k, i: (k, i, 0))
    return pl.pallas_call(
        body, name=name,
        out_shape=(jax.ShapeDtypeStruct((NDEV, D, FS), BF16), jax.ShapeDtypeStruct((NDEV, D, FS), BF16)),
        grid=(NDEV, D // tm), in_specs=[pl.BlockSpec((Tn, tm), lambda k, i: (0, i)), act, act], out_specs=(o, o),
        compiler_params=_cp("parallel", "parallel"),
    )(v2, da, db)


def _ffn_dv2(da, db, wg, wu, name="ffn_dv2", tm=1024, tn=1024):
    Tn = da.shape[1]
    tm = min(tm, Tn)

    def body(da_ref, db_ref, wg_ref, wu_ref, o_ref, acc):
        k = pl.program_id(2)
        p = _dot(da_ref[...], wg_ref[...], tb=True) + _dot(db_ref[...], wu_ref[...], tb=True)

        @pl.when(k == 0)
        def _():
            acc[...] = p

        @pl.when(k > 0)
        def _():
            acc[...] += p

        @pl.when(k == NDEV - 1)
        def _():
            o_ref[...] = acc[...]

    act = pl.BlockSpec((None, tm, FS), lambda i, j, k: (k, i, 0))
    w = pl.BlockSpec((None, tn, FS), lambda i, j, k: (k, j, 0))
    return pl.pallas_call(
        body, name=name, out_shape=jax.ShapeDtypeStruct((Tn, D), F32), grid=(Tn // tm, D // tn, NDEV),
        in_specs=[act, act, w, w], out_specs=pl.BlockSpec((tm, tn), lambda i, j, k: (i, j)),
        scratch_shapes=[pltpu.VMEM((tm, tn), F32)], compiler_params=_cp("parallel", "parallel", "arbitrary"),
    )(da, db, wg, wu)


def _adam_math(w, g, m, v):
    m2 = B1 * m + (1.0 - B1) * g
    v2 = B2 * v + (1.0 - B2) * (g * g)
    mh = m2 / (1.0 - B1 ** STEP)
    vh = v2 / (1.0 - B2 ** STEP)
    return -LR * (mh / (jnp.sqrt(vh) + AEPS) + WD * w), m2, v2


def _adamw(w, m, v, psums, parts, chip_idx, name, tr):
    R, C = w.shape

    def body(s_ref, w_ref, m_ref, v_ref, o_ref, p_ref, g_ref, d_ref, m2_ref, v2_ref):
        g = ((o_ref[...].astype(F32) + p_ref[0].astype(F32)) + p_ref[1].astype(F32)) + p_ref[2].astype(F32)
        d, m2, v2 = _adam_math(w_ref[...], g, m_ref[...], v_ref[...])
        g_ref[...] = g
        d_ref[...] = d
        m2_ref[...] = m2
        v2_ref[...] = v2

    blk = pl.BlockSpec((tr, C), lambda i, s: (i, 0))
    out = jax.ShapeDtypeStruct((R, C), F32)
    grid_spec = pltpu.PrefetchScalarGridSpec(
        num_scalar_prefetch=1, grid=(R // tr,),
        in_specs=[blk, blk, blk, pl.BlockSpec((None, tr, C), lambda i, s: (s[0], i, 0)), pl.BlockSpec((3, tr, C), lambda i, s: (0, i, 0))],
        out_specs=(blk, blk, blk, blk),
    )
    return pl.pallas_call(body, name=name, out_shape=(out, out, out, out), grid_spec=grid_spec, compiler_params=_cp("parallel"),
                          )(chip_idx, w, m, v, psums, parts)


def _adamw_plain(w, m, v, g, name):
    def body(w_ref, m_ref, v_ref, g_ref, d_ref, m2_ref, v2_ref):
        d, m2, v2 = _adam_math(w_ref[...], g_ref[...], m_ref[...], v_ref[...])
        d_ref[...] = d
        m2_ref[...] = m2
        v2_ref[...] = v2

    out = jax.ShapeDtypeStruct(w.shape, F32)
    return pl.pallas_call(body, name=name, out_shape=(out, out, out))(w, m, v, g)


def _sum_devices(pack_all, name="sum_small"):
    def body(p_ref, o_ref):
        s = p_ref[0]
        for k in range(1, NDEV):
            s = s + p_ref[k]
        o_ref[...] = s

    return pl.pallas_call(body, name=name, out_shape=jax.ShapeDtypeStruct(pack_all.shape[1:], F32))(pack_all)


def _pair_add(g5, recv, c_idx, name, tr):
    _, _, R, C = g5.shape

    def body(c_ref, g_ref, r_ref, o_ref):
        o_ref[...] = (g_ref[...].astype(F32) + r_ref[...].astype(F32)).astype(BF16)

    grid_spec = pltpu.PrefetchScalarGridSpec(
        num_scalar_prefetch=1, grid=(4, R // tr),
        in_specs=[pl.BlockSpec((None, None, tr, C), lambda q, i, c: (q, c[0], i, 0)), pl.BlockSpec((None, tr, C), lambda q, i, c: (q, i, 0))],
        out_specs=pl.BlockSpec((None, tr, C), lambda q, i, c: (q, i, 0)),
    )
    return pl.pallas_call(
        body, name=name, out_shape=jax.ShapeDtypeStruct((4, R, C), BF16), grid_spec=grid_spec,
        compiler_params=_cp("parallel", "parallel"),
    )(c_idx, g5, recv)


_ANY = pl.BlockSpec(memory_space=pl.ANY)


def _mesh_pos():
    x, y, c = lax.axis_index("x"), lax.axis_index("y"), lax.axis_index("c")
    return x, y, c, [(1 - x, y), (x, 1 - y), (1 - x, 1 - y)]


def _all_gather(shards, name="gather_weights"):
    n = len(shards)

    def body(*refs):
        ins, outs = refs[:n], refs[n:2 * n]
        send, recv, loc = refs[2 * n:]
        x, y, c, chips = _mesh_pos()
        me, sib = (x, y, c), (x, y, 1 - c)

        def cp(a, k, block, to, own=False):
            dst = outs[a].at[4 * block[0] + 2 * block[1] + block[2]]
            return pltpu.make_async_remote_copy(src_ref=ins[a] if own else dst, dst_ref=dst, send_sem=send.at[a, k],
                                                recv_sem=recv.at[a, k], device_id=to, device_id_type=MESH)

        local = [pltpu.make_async_copy(ins[a], outs[a].at[4 * x + 2 * y + c], loc.at[a]) for a in range(n)]
        first = []
        for a in range(n):
            local[a].start()
            first.append(cp(a, 0, me, sib, own=True))
            first += [cp(a, 1 + j, me, (*chip, c), own=True) for j, chip in enumerate(chips)]
        for d in first:
            d.start()
        passed = []
        for a in range(n):
            for j, chip in enumerate(chips):
                cp(a, 1 + j, (*chip, c), me).wait_recv()
                d = cp(a, 4 + j, (*chip, c), sib)
                d.start()
                passed.append(d)
        for a in range(n):
            cp(a, 0, sib, me).wait_recv()
            for j, chip in enumerate(chips):
                cp(a, 4 + j, (*chip, 1 - c), me).wait_recv()
        for d in first + passed:
            d.wait_send()
        for d in local:
            d.wait()

    return pl.pallas_call(
        body, name=name,
        out_shape=tuple(jax.ShapeDtypeStruct((NDEV,) + s.shape, s.dtype) for s in shards),
        in_specs=[_ANY] * n, out_specs=tuple([_ANY] * n),
        scratch_shapes=[pltpu.SemaphoreType.DMA((n, 7)), pltpu.SemaphoreType.DMA((n, 7)), pltpu.SemaphoreType.DMA((n,))],
    )(*shards)


def _pair_exchange(grads, name):
    n = len(grads)

    def body(*refs):
        ins, outs = refs[:n], refs[n:2 * n]
        send, recv = refs[2 * n:]
        x, y, c, _ = _mesh_pos()
        big = [pltpu.make_async_remote_copy(src_ref=ins[a].at[:, 1 - c], dst_ref=outs[a], send_sem=send.at[a], recv_sem=recv.at[a],
                                            device_id=(x, y, 1 - c), device_id_type=MESH) for a in range(n)]
        for d in big:
            d.start()
        for d in big:
            d.wait_recv()
        for d in big:
            d.wait_send()

    return pl.pallas_call(
        body, name=name, out_shape=tuple(jax.ShapeDtypeStruct((4,) + g.shape[2:], g.dtype) for g in grads),
        in_specs=[_ANY] * n, out_specs=tuple([_ANY] * n),
        scratch_shapes=[pltpu.SemaphoreType.DMA((n,)), pltpu.SemaphoreType.DMA((n,))],
    )(*grads)


def _gather_small(pack, name="gather_small"):
    def body(pk, pk_all, psend, precv, loc):
        x, y, c, chips = _mesh_pos()
        me_slot = 4 * x + 2 * y + c
        sib = (x, y, 1 - c)
        own = pltpu.make_async_copy(pk, pk_all.at[me_slot], loc)
        own.start()
        peers = [sib] + [(*chip, c) for chip in chips] + [(*chip, 1 - c) for chip in chips]
        small = [pltpu.make_async_remote_copy(src_ref=pk, dst_ref=pk_all.at[me_slot], send_sem=psend.at[k], recv_sem=precv.at[k],
                                              device_id=p, device_id_type=MESH) for k, p in enumerate(peers)]
        for d in small:
            d.start()
        for k, p in enumerate(peers):
            pltpu.make_async_remote_copy(src_ref=pk, dst_ref=pk_all.at[4 * p[0] + 2 * p[1] + p[2]], send_sem=psend.at[k],
                                         recv_sem=precv.at[k], device_id=p, device_id_type=MESH).wait_recv()
        for d in small:
            d.wait_send()
        own.wait()

    return pl.pallas_call(
        body, name=name, out_shape=jax.ShapeDtypeStruct((NDEV,) + pack.shape, pack.dtype), in_specs=[_ANY], out_specs=_ANY,
        scratch_shapes=[pltpu.SemaphoreType.DMA((7,)), pltpu.SemaphoreType.DMA((7,)), pltpu.SemaphoreType.DMA(())],
    )(pack)


_HBM = pl.BlockSpec(memory_space=pltpu.HBM)
_SEM = pl.BlockSpec(memory_space=pltpu.SEMAPHORE)
_VMEM = pl.BlockSpec(memory_space=pltpu.VMEM)
_SIDE = pltpu.CompilerParams(has_side_effects=pltpu.SideEffectType.DATAFLOW_SIDE_EFFECTING)
_TOKEN = jax.ShapeDtypeStruct((8, LANE), F32)


def _hbm(a):
    return pltpu.with_memory_space_constraint(a, pltpu.HBM)


def _hbm_like(arrs):
    return tuple(pltpu.HBM(a.shape, a.dtype) for a in arrs)


def _after(x, token):
    return lax.optimization_barrier((x, token))[0]


def _chip_copies(ins, lands, send, recv):
    x, y, c, chips = _mesh_pos()
    return [pltpu.make_async_remote_copy(src_ref=ins[a].at[2 * chip[0] + chip[1]], dst_ref=lands[a].at[j], send_sem=send.at[3 * a + j],
                                         recv_sem=recv.at[3 * a + j], device_id=(*chip, c), device_id_type=MESH)
            for a in range(len(ins)) for j, chip in enumerate(chips)]


def _chip_start(psums, name):
    n = len(psums)
    lands = [lax.empty((3,) + p.shape[1:], p.dtype) for p in psums]

    def body(*refs):
        for d in _chip_copies(refs[:n], refs[n:2 * n], refs[2 * n], refs[2 * n + 1]):
            d.start()
        refs[-1][...] = jnp.zeros_like(refs[-1])

    sems = pltpu.SemaphoreType.DMA((3 * n,))
    out = pl.pallas_call(
        body, name=name, out_shape=(sems, sems) + _hbm_like(psums) + _hbm_like(lands) + (_TOKEN,),
        in_specs=[_HBM] * (2 * n), out_specs=(_SEM, _SEM) + (_HBM,) * (2 * n) + (_VMEM,),
        input_output_aliases={i: 2 + i for i in range(2 * n)}, compiler_params=_SIDE,
    )(*[_hbm(a) for a in list(psums) + lands])
    return out[0], out[1], list(out[2:2 + n]), list(out[2 + n:2 + 2 * n]), out[-1]


def _chip_wait(send, recv, psums, lands, after, name):
    n = len(psums)

    def body(*refs):
        for d in _chip_copies(refs[:n], refs[n:2 * n], refs[2 * n], refs[2 * n + 1]):
            d.wait_send()
            d.wait_recv()

    out = pl.pallas_call(
        body, name=name, out_shape=_hbm_like(psums) + _hbm_like(lands),
        in_specs=[_HBM] * (2 * n) + [_SEM, _SEM, _ANY], out_specs=(_HBM,) * (2 * n),
        input_output_aliases={i: i for i in range(2 * n)}, compiler_params=_SIDE,
    )(*psums, *lands, send, recv, after)
    return list(out[:n]), list(out[n:])


def _slot(chip, c):
    return 4 * chip[0] + 2 * chip[1] + c


def _gather_start(shards, dev, name):
    n = len(shards)
    lands = [lax.dynamic_update_slice(lax.empty((NDEV,) + s.shape, s.dtype), s[None], (dev,) + (0,) * s.ndim) for s in shards]

    def body(*refs):
        src, land, send, recv = refs[:n], refs[n:2 * n], refs[2 * n], refs[2 * n + 1]
        x, y, c, chips = _mesh_pos()
        for a in range(n):
            for k, to in enumerate([(x, y, 1 - c)] + [(*chip, c) for chip in chips]):
                pltpu.make_async_remote_copy(src_ref=src[a], dst_ref=land[a].at[_slot((x, y), c)], send_sem=send.at[4 * a + k],
                                             recv_sem=recv.at[4 * a + k], device_id=to, device_id_type=MESH).start()
        refs[-1][...] = jnp.zeros_like(refs[-1])

    sems = pltpu.SemaphoreType.DMA((4 * n,))
    out = pl.pallas_call(
        body, name=name, out_shape=(sems, sems) + _hbm_like(shards) + _hbm_like(lands) + (_TOKEN,),
        in_specs=[_HBM] * (2 * n), out_specs=(_SEM, _SEM) + (_HBM,) * (2 * n) + (_VMEM,),
        input_output_aliases={i: 2 + i for i in range(2 * n)}, compiler_params=_SIDE,
    )(*[_hbm(a) for a in list(shards) + lands])
    return out[0], out[1], list(out[2:2 + n]), list(out[2 + n:2 + 2 * n]), out[-1]


def _gather_pass(lands, recv, after, name):
    n = len(lands)

    def body(*refs):
        land, recv1 = refs[:n], refs[n]
        send2, recv2 = refs[n + 2], refs[n + 3]
        x, y, c, chips = _mesh_pos()
        for a in range(n):
            for j, chip in enumerate(chips):
                blk = land[a].at[_slot(chip, c)]
                pltpu.make_async_remote_copy(src_ref=blk, dst_ref=blk, send_sem=send2.at[3 * a + j], recv_sem=recv1.at[4 * a + 1 + j],
                                             device_id=(*chip, c), device_id_type=MESH).wait_recv()
                pltpu.make_async_remote_copy(src_ref=blk, dst_ref=blk, send_sem=send2.at[3 * a + j], recv_sem=recv2.at[3 * a + j],
                                             device_id=(x, y, 1 - c), device_id_type=MESH).start()
        refs[-1][...] = jnp.zeros_like(refs[-1])

    sems = pltpu.SemaphoreType.DMA((3 * n,))
    out = pl.pallas_call(
        body, name=name, out_shape=(sems, sems) + _hbm_like(lands) + (_TOKEN,),
        in_specs=[_HBM] * n + [_SEM, _ANY], out_specs=(_SEM, _SEM) + (_HBM,) * n + (_VMEM,),
        input_output_aliases={i: 2 + i for i in range(n)}, compiler_params=_SIDE,
    )(*lands, recv, after)
    return out[0], out[1], list(out[2:2 + n]), out[-1]


def _gather_wait(shards, lands, send, recv, send2, recv2, after, name):
    n = len(lands)

    def body(*refs):
        src, land = refs[:n], refs[n:2 * n]
        send1, recv1, snd2, rcv2 = refs[2 * n:2 * n + 4]
        x, y, c, chips = _mesh_pos()
        sib = (x, y, 1 - c)
        for a in range(n):
            for k in range(4):
                pltpu.make_async_remote_copy(src_ref=src[a], dst_ref=land[a].at[_slot((x, y), c)], send_sem=send1.at[4 * a + k],
                                             recv_sem=recv1.at[4 * a + k], device_id=sib, device_id_type=MESH).wait_send()
            blk = land[a].at[_slot((x, y), 1 - c)]
            pltpu.make_async_remote_copy(src_ref=blk, dst_ref=blk, send_sem=send1.at[4 * a], recv_sem=recv1.at[4 * a],
                                         device_id=sib, device_id_type=MESH).wait_recv()
            for j, chip in enumerate(chips):
                mine, theirs = land[a].at[_slot(chip, c)], land[a].at[_slot(chip, 1 - c)]
                pltpu.make_async_remote_copy(src_ref=mine, dst_ref=mine, send_sem=snd2.at[3 * a + j], recv_sem=rcv2.at[3 * a + j],
                                             device_id=sib, device_id_type=MESH).wait_send()
                pltpu.make_async_remote_copy(src_ref=theirs, dst_ref=theirs, send_sem=snd2.at[3 * a + j], recv_sem=rcv2.at[3 * a + j],
                                             device_id=sib, device_id_type=MESH).wait_recv()

    out = pl.pallas_call(
        body, name=name, out_shape=_hbm_like(shards) + _hbm_like(lands),
        in_specs=[_HBM] * (2 * n) + [_SEM] * 4 + [_ANY], out_specs=(_HBM,) * (2 * n),
        input_output_aliases={i: i for i in range(2 * n)}, compiler_params=_SIDE,
    )(*shards, *lands, send, recv, send2, recv2, after)
    return list(out[n:])


def _pad_to(v, n):
    return jnp.pad(v, [(0, 0)] * (v.ndim - 1) + [(0, n - v.shape[-1])])


def _pack_small(n1, gb, sk, gn, n2, fn, extra=None):
    parts = [n1.reshape(-1), gb.reshape(-1), sk.reshape(-1), gn.reshape(-1), n2.reshape(-1), fn.reshape(-1)]
    flat = jnp.concatenate(parts + ([extra.reshape(-1)] if extra is not None else []))
    return _pad_to(flat, SMALL_N).reshape(SMALL_ROWS, LANE)


def _unpack_small(p):
    f = p.reshape(-1)
    return (f[S_N1:S_GB].reshape(1, D), f[S_GB:S_SK].reshape(1, GH * DK), f[S_SK:S_GN].reshape(1, NQ), f[S_GN:S_N2].reshape(1, DV),
            f[S_N2:S_FN].reshape(1, D), f[S_FN:S_LOSS].reshape(D))


class _NoComm:
    def __init__(self, wo, wg_all, wu_all, wd_all):
        self.rest = (wo, wg_all, wu_all, wd_all)

    def mixed(self, gla_o, gla_norm_w):
        return gla_norm_w

    def rest_weights(self, merged):
        return self.rest

    def ffn_grads(self, d_wg, d_wu, d_wd, da):
        self.ffn = (d_wg, d_wu, d_wd)
        return da

    def in_grads(self, d_wmain, d_wlr, d_wo, dproj):
        self.inw = (d_wmain, d_wlr, d_wo)
        return dproj


class _Comm:
    def __init__(self, rest_shards, dev, c_idx):
        self.c_idx = c_idx
        self.send, self.recv, self.shards, self.lands, self.token = _gather_start(rest_shards, dev, "gather_rest_start")

    def mixed(self, gla_o, gla_norm_w):
        self.send2, self.recv2, self.lands, token = _gather_pass(self.lands, self.recv, gla_o, "gather_rest_pass")
        return _after(gla_norm_w, token)

    def rest_weights(self, merged):
        wo_all, wg_all, wu_all, wd_all = _gather_wait(self.shards, self.lands, self.send, self.recv, self.send2, self.recv2,
                                                      merged, "gather_rest_wait")
        return wo_all.reshape(D, D), wg_all, wu_all, wd_all

    def _reduce(self, tag, names, grads, rows):
        recv1 = _pair_exchange(grads, "reduce_pair_" + tag)
        psums = [_pair_add(g, r, self.c_idx, "pair_add_" + nm, tr) for g, r, nm, tr in zip(grads, recv1, names, rows)]
        *flight, token = _chip_start(psums, "reduce_chips_start_" + tag)
        return dict(tag=tag, names=names, rows=rows, flight=flight), token

    def ffn_grads(self, d_wg, d_wu, d_wd, da):
        self.ffn, token = self._reduce("ffn", ["w_ffn_gate", "w_ffn_up", "w_ffn_down"],
                                       [d_wg.reshape(4, 2, D, FS), d_wu.reshape(4, 2, D, FS), d_wd.reshape(4, 2, FS, D)], [512, 512, 176])
        return _after(da, token)

    def in_grads(self, d_wmain, d_wlr, d_wo, dproj):
        d_win = _from_main(d_wmain, d_wlr[:, :RANK]).reshape(D, 4, 2, WS).transpose(1, 2, 0, 3)
        self.inw, token = self._reduce("in", ["w_in", "w_out"], [d_win, d_wo.reshape(4, 2, D // NDEV, D)], [256, 256])
        return _after(dproj, token)


def _local_step(xs, tgt, norm1_w, gla_gate_b, attn_sinks, gla_norm_w, norm2_w, fnw, w_main, w_lr, w2p, comm):
    u = _rmsnorm_fwd(xs, norm1_w, "norm1_fwd")
    proj = _mm(u, w_main, tm=1024, tn=640, tk=D, name="in_proj")
    plr = _mm(u, w_lr, tm=1024, tn=LANE, tk=D, name="in_proj_lr")
    attn_o = _attn_fwd(proj, attn_sinks)
    gla_o, states = _gla_fwd(proj, plr, w2p, gla_gate_b)
    merged = _merge_fwd(attn_o, gla_o, proj, comm.mixed(gla_o, gla_norm_w))
    wo, wg_all, wu_all, wd_all = comm.rest_weights(merged)
    h1 = _mm(merged, wo, tm=1024, tn=512, tk=D, res=xs, name="out_proj")
    v2 = _rmsnorm_fwd(h1, norm2_w, "norm2_fwd")
    fa, fb, ff = _ffn_up(v2, wg_all, wu_all)
    h2 = _ffn_down(ff, wd_all, h1)
    dh2, dh2b, d_fnw, loss_part = _loss_head(h2, fnw, tgt)

    da, db = _ffn_dact(dh2b, wd_all, fa, fb)
    d_wd = _ffn_dwd(ff, dh2b)
    d_wg, d_wu = _ffn_dwgu(v2, da, db)
    da = comm.ffn_grads(d_wg, d_wu, d_wd, da)
    dv2 = _ffn_dv2(da, db, wg_all, wu_all)
    dh1, dh1b, d_n2 = _rmsnorm_bwd(dv2, h1, norm2_w, dh2, "norm2_bwd")
    dmerged = _mm(dh1b, wo, tb=True, tm=1024, tn=512, tk=D, name="out_proj_dx")
    d_wo = _mm(merged, dh1b, ta=True, tm=1024, tn=512, tk=xs.shape[0], out_dtype=BF16, name="out_proj_dw")
    d_attn, d_gla, d_gates, d_gnw = _merge_bwd(dmerged, attn_o, gla_o, proj, gla_norm_w)
    d_q, d_kv, d_sinks = _attn_bwd(proj, attn_sinks, attn_o, d_attn)
    d_gqk, d_gv, d_plr, d_w2p, d_gb = _gla_bwd(proj, plr, w2p, gla_gate_b, states, d_gla)
    dproj = jnp.concatenate([d_gates, d_q, d_gv, d_gqk, d_kv], axis=1)
    d_wmain = _mm(u, dproj, ta=True, tm=1024, tn=640, tk=xs.shape[0], out_dtype=BF16, name="in_proj_dw")
    d_wlr = _mm(u, d_plr, ta=True, tm=1024, tn=LANE, tk=xs.shape[0], out_dtype=BF16, name="in_proj_lr_dw")
    dproj = comm.in_grads(d_wmain, d_wlr, d_wo, dproj)
    du_lr = _mm(d_plr, w_lr, tb=True, tm=1024, tn=1024, tk=LANE, name="in_proj_lr_dx")
    du = _mm(dproj, w_main, tb=True, tm=1024, tn=1024, tk=1280, res=du_lr, name="in_proj_dx")
    dx, _, d_n1 = _rmsnorm_bwd(du, xs, norm1_w, dh1, "norm1_bwd")
    return dx, loss_part, d_w2p, d_gb, d_sinks, d_gnw, d_n1, d_n2, d_fnw


def kernel(x, norm1_w, w_in, gla_gate_w2, gla_gate_b, attn_sinks, gla_norm_w, w_out, norm2_w, w_ffn_gate, w_ffn_up, w_ffn_down, final_norm_w, loss_target, m_norm1_w, m_w_in, m_gla_gate_w2, m_gla_gate_b, m_attn_sinks, m_gla_norm_w, m_w_out, m_norm2_w, m_w_ffn_gate, m_w_ffn_up, m_w_ffn_down, m_final_norm_w, v_norm1_w, v_w_in, v_gla_gate_w2, v_gla_gate_b, v_attn_sinks, v_gla_norm_w, v_w_out, v_norm2_w, v_w_ffn_gate, v_w_ffn_up, v_w_ffn_down, v_final_norm_w):
    xs, tgt = x[0], loss_target[0]
    fnw = final_norm_w.reshape(1, D)
    c_idx = lax.axis_index("c").astype(jnp.int32).reshape(1)
    dev = 4 * lax.axis_index("x") + 2 * lax.axis_index("y") + lax.axis_index("c")

    chip_idx = (2 * lax.axis_index("x") + lax.axis_index("y")).astype(jnp.int32).reshape(1)

    win_all, w2_all = _all_gather([w_in[0].astype(BF16), gla_gate_w2[0]], name="gather_w_in")
    rest = [_after(w[0].astype(BF16), win_all) for w in (w_out, w_ffn_gate, w_ffn_up, w_ffn_down)]
    comm = _Comm(rest, dev, c_idx)
    w_main, w_lr = _to_main(jnp.transpose(win_all, (1, 0, 2)).reshape(D, DIN))
    w_lr = _pad_to(w_lr, LANE)
    w2p = jnp.pad(jnp.transpose(w2_all, (1, 0, 2)).reshape(RANK, GH * DK), ((0, LANE - RANK), (0, 0)))

    dx, loss_part, d_w2p, d_gb, d_sinks, d_gnw, d_n1, d_n2, d_fnw = _local_step(
        xs, tgt, _after(norm1_w, comm.token), gla_gate_b, attn_sinks, gla_norm_w, norm2_w, fnw, w_main, w_lr, w2p, comm)

    pack = jnp.concatenate([_pack_small(d_n1, d_gb, d_sinks, d_gnw, d_n2, d_fnw, loss_part),
                            d_w2p[:RANK].reshape(GW2_ROWS, LANE)], axis=0)
    small = _sum_devices(_gather_small(pack))

    big = {}
    after = dx
    for grp in (comm.ffn, comm.inw):
        psums, parts = _chip_wait(*grp["flight"], after, "reduce_chips_wait_" + grp["tag"])
        for nm, ps, pt, tr in zip(grp["names"], psums, parts, grp["rows"]):
            w, m, v = {"w_in": (w_in, m_w_in, v_w_in), "w_out": (w_out, m_w_out, v_w_out), "w_ffn_gate": (w_ffn_gate, m_w_ffn_gate, v_w_ffn_gate),
                       "w_ffn_up": (w_ffn_up, m_w_ffn_up, v_w_ffn_up), "w_ffn_down": (w_ffn_down, m_w_ffn_down, v_w_ffn_down)}[nm]
            big[nm] = [t[None] for t in _adamw(w[0], m[0], v[0], ps, pt, chip_idx, "adamw_" + nm, tr)]
            after = big[nm][0]
    g_small = small[:SMALL_ROWS]
    sm = _adamw_plain(_pack_small(norm1_w, gla_gate_b, attn_sinks, gla_norm_w, norm2_w, final_norm_w),
                      _pack_small(m_norm1_w, m_gla_gate_b, m_attn_sinks, m_gla_norm_w, m_norm2_w, m_final_norm_w),
                      _pack_small(v_norm1_w, v_gla_gate_b, v_attn_sinks, v_gla_norm_w, v_norm2_w, v_final_norm_w), g_small, "adamw_small")
    g_w2 = lax.dynamic_slice_in_dim(small[SMALL_ROWS:].reshape(RANK, GH * DK), dev * LANE, LANE, axis=1)
    w2 = [g_w2[None]] + [t[None] for t in _adamw_plain(gla_gate_w2[0], m_gla_gate_w2[0], v_gla_gate_w2[0], g_w2, "adamw_w2")]
    loss = g_small.reshape(-1)[S_LOSS]

    sg, sd, sm2, sv2 = [_unpack_small(t) for t in (g_small,) + tuple(sm)]

    def group(i, s):
        return (s[0], big["w_in"][i], w2[i], s[1], s[2], s[3], big["w_out"][i], s[4], big["w_ffn_gate"][i], big["w_ffn_up"][i],
                big["w_ffn_down"][i], s[5])

    return (loss, dx[None], *group(0, sg), *group(1, sd), *group(2, sm2), *group(3, sv2))
```

```python
import functools

import jax
import jax.numpy as jnp
from jax import lax
from jax.experimental import pallas as pl
from jax.experimental.pallas import tpu as pltpu

F32, BF16 = jnp.float32, jnp.bfloat16
HIGHEST = lax.Precision.HIGHEST

D = 2048
HD, NQ, NKV, GRP, WIN = 64, 32, 4, 8, 128
GH, DK, DV, RANK, GC = 4, 256, 512, 16, 64
FH, NDEV = 5632, 8
FS = FH // NDEV
DIN = 12816
WS = DIN // NDEV
EPS = 1e-6
MASKV = -1e30
LANE = 128

C_GR, C_GA, C_GB, C_AQ, C_GV, C_GQ, C_GK, C_AK, C_AV, NMAIN = 0, 2048, 4096, 6144, 8192, 10240, 11264, 12288, 12544, 12800

LR, B1, B2, AEPS, WD, STEP = 0.001, 0.9, 0.999, 1e-08, 0.01, 10

S_N1, S_GB, S_SK, S_GN, S_N2, S_FN, S_LOSS, SMALL_N = 0, 2048, 3072, 3104, 3616, 5664, 7712, 8192
SMALL_ROWS = SMALL_N // LANE
GW2_ROWS = RANK * GH * DK // LANE
PACK_ROWS = SMALL_ROWS + GW2_ROWS

MESH = pl.DeviceIdType.MESH


def _to_main(w):
    main = jnp.concatenate([w[..., 6672:12816], w[..., 0:2048], w[..., 4608:6656], w[..., 2560:4608], w[..., 2048:2560]], axis=-1)
    return main, w[..., 6656:6672]


def _from_main(m, lr):
    return jnp.concatenate([m[..., C_AQ:C_GV], m[..., C_AK:NMAIN], m[..., C_GQ:C_AK], m[..., C_GV:C_GQ], lr, m[..., 0:C_AQ]], axis=-1)


def _dot(a, b, ta=False, tb=False, prec=None):
    dn = (((0,) if ta else (1,), (1,) if tb else (0,)), ((), ()))
    return lax.dot_general(a, b, dn, preferred_element_type=F32, precision=prec)


def _sigmoid(x):
    return 1.0 / (1.0 + jnp.exp(-x))


VMEM_LIMIT = 56 * 1024 * 1024


def _cp(*sem):
    return pltpu.CompilerParams(dimension_semantics=sem, vmem_limit_bytes=VMEM_LIMIT)


def _mm(a, b, *, ta=False, tb=False, tm, tn, tk, out_dtype=F32, res=None, name):
    M, K = (a.shape[1], a.shape[0]) if ta else a.shape
    N = b.shape[0] if tb else b.shape[1]
    tm, tn, tk = min(tm, M), min(tn, N), min(tk, K)
    nk = K // tk
    assert M % tm == 0 and N % tn == 0 and K % tk == 0
    a_spec = pl.BlockSpec((tk, tm), lambda i, j, k: (k, i)) if ta else pl.BlockSpec((tm, tk), lambda i, j, k: (i, k))
    b_spec = pl.BlockSpec((tn, tk), lambda i, j, k: (j, k)) if tb else pl.BlockSpec((tk, tn), lambda i, j, k: (k, j))
    o_spec = pl.BlockSpec((tm, tn), lambda i, j, k: (i, j))
    has_res = res is not None

    def body(*refs):
        a_ref, b_ref = refs[0], refs[1]
        r_ref = refs[2] if has_res else None
        o_ref = refs[3] if has_res else refs[2]
        p = _dot(a_ref[...].astype(BF16), b_ref[...].astype(BF16), ta, tb)
        if nk == 1:
            if has_res:
                p = p + r_ref[...]
            o_ref[...] = p.astype(out_dtype)
        else:
            acc = refs[-1]
            k = pl.program_id(2)

            @pl.when(k == 0)
            def _():
                acc[...] = (p + r_ref[...]) if has_res else p

            @pl.when(k > 0)
            def _():
                acc[...] += p

            @pl.when(k == nk - 1)
            def _():
                o_ref[...] = acc[...].astype(out_dtype)

    return pl.pallas_call(
        body, name=name,
        out_shape=jax.ShapeDtypeStruct((M, N), out_dtype),
        grid=(M // tm, N // tn, nk),
        in_specs=[a_spec, b_spec] + ([o_spec] if has_res else []),
        out_specs=o_spec,
        scratch_shapes=[pltpu.VMEM((tm, tn), F32)] if nk > 1 else [],
        compiler_params=_cp("parallel", "parallel", "arbitrary"),
    )(*((a, b, res) if has_res else (a, b)))


def _rmsnorm_fwd(x, w, name, tm=256):
    Tn = x.shape[0]

    def body(x_ref, w_ref, o_ref):
        xv = x_ref[...]
        r = lax.rsqrt(jnp.mean(xv * xv, axis=1, keepdims=True) + EPS)
        o_ref[...] = (xv * r * w_ref[...]).astype(BF16)

    return pl.pallas_call(
        body, name=name, out_shape=jax.ShapeDtypeStruct((Tn, D), BF16), grid=(Tn // tm,),
        in_specs=[pl.BlockSpec((tm, D), lambda i: (i, 0)), pl.BlockSpec((1, D), lambda i: (0, 0))],
        out_specs=pl.BlockSpec((tm, D), lambda i: (i, 0)), compiler_params=_cp("parallel"),
    )(x, w)


def _rmsnorm_bwd(dy, h, w, res, name, tm=256):
    Tn = h.shape[0]

    def body(dy_ref, h_ref, w_ref, res_ref, dh_ref, dhb_ref, dw_ref):
        hv, dyv = h_ref[...], dy_ref[...]
        r = lax.rsqrt(jnp.mean(hv * hv, axis=1, keepdims=True) + EPS)
        g = dyv * w_ref[...]
        dh = res_ref[...] + r * g - hv * (r * r * r * jnp.mean(g * hv, axis=1, keepdims=True))
        dh_ref[...] = dh
        dhb_ref[...] = dh.astype(BF16)
        part = jnp.sum(dyv * hv * r, axis=0, keepdims=True)

        @pl.when(pl.program_id(0) == 0)
        def _():
            dw_ref[...] = part

        @pl.when(pl.program_id(0) > 0)
        def _():
            dw_ref[...] += part

    row = pl.BlockSpec((tm, D), lambda i: (i, 0))
    vec = pl.BlockSpec((1, D), lambda i: (0, 0))
    return pl.pallas_call(
        body, name=name,
        out_shape=(jax.ShapeDtypeStruct((Tn, D), F32), jax.ShapeDtypeStruct((Tn, D), BF16), jax.ShapeDtypeStruct((1, D), F32)),
        grid=(Tn // tm,), in_specs=[row, row, vec, row], out_specs=(row, row, vec), compiler_params=_cp("arbitrary"),
    )(dy, h, w, res)


def _loss_head(h2, wf, tgt, name="loss_head", tm=256):
    Tn = h2.shape[0]

    def body(h_ref, w_ref, t_ref, dh_ref, dhb_ref, dw_ref, loss_ref):
        hv, wv = h_ref[...], w_ref[...]
        r = lax.rsqrt(jnp.mean(hv * hv, axis=1, keepdims=True) + EPS)
        hn = hv * r
        e = hn * wv - t_ref[...]
        dy = e * (1.0 / D)
        g = dy * wv
        dh = r * g - hv * (r * r * r * jnp.mean(g * hv, axis=1, keepdims=True))
        dh_ref[...] = dh
        dhb_ref[...] = dh.astype(BF16)
        part = jnp.sum(dy * hn, axis=0, keepdims=True)
        lpart = (0.5 / D) * jnp.sum(jnp.sum(e * e, axis=1, keepdims=True), axis=0, keepdims=True)

        @pl.when(pl.program_id(0) == 0)
        def _():
            dw_ref[...] = part
            loss_ref[...] = lpart

        @pl.when(pl.program_id(0) > 0)
        def _():
            dw_ref[...] += part
            loss_ref[...] += lpart

    row = pl.BlockSpec((tm, D), lambda i: (i, 0))
    vec = pl.BlockSpec((1, D), lambda i: (0, 0))
    one = pl.BlockSpec((1, 1), lambda i: (0, 0))
    return pl.pallas_call(
        body, name=name,
        out_shape=(jax.ShapeDtypeStruct((Tn, D), F32), jax.ShapeDtypeStruct((Tn, D), BF16), jax.ShapeDtypeStruct((1, D), F32),
                   jax.ShapeDtypeStruct((1, 1), F32)),
        grid=(Tn // tm,), in_specs=[row, vec, row], out_specs=(row, row, vec, one), compiler_params=_cp("arbitrary"),
    )(h2, wf, tgt)


def _attn_mask(n):
    qi = lax.broadcasted_iota(jnp.int32, (GRP * WIN, 2 * WIN), 0) % WIN
    ki = lax.broadcasted_iota(jnp.int32, (GRP * WIN, 2 * WIN), 1)
    rel = qi + WIN - ki
    return (rel >= 0) & (rel < WIN) & ((n > 0) | (ki >= WIN))


def _attn_probs(q_ref, kc_ref, kp_ref, sink_ref, h, mask):
    kk = jnp.concatenate([kp_ref[:, h * HD:(h + 1) * HD], kc_ref[:, h * HD:(h + 1) * HD]], axis=0).astype(BF16)
    qs = jnp.concatenate([q_ref[:, (h * GRP + g) * HD:(h * GRP + g + 1) * HD] for g in range(GRP)], axis=0).astype(BF16)
    s = _dot(qs, kk, tb=True) * (HD ** -0.5)
    s = jnp.where(mask, s, MASKV)
    sink = jnp.concatenate([jnp.full((WIN, 1), sink_ref[0, h * GRP + g], F32) for g in range(GRP)], axis=0)
    m = jnp.maximum(jnp.max(s, axis=1, keepdims=True), sink)
    e = jnp.exp(s - m)
    es = jnp.exp(sink - m)
    inv = 1.0 / (jnp.sum(e, axis=1, keepdims=True) + es)
    return e * inv, es * inv, qs, kk


def _attn_specs(nb, last):
    cur = lambda n: jnp.minimum(n, last)
    prev = lambda n: jnp.maximum(jnp.minimum(n, last) - 1, 0)
    return [
        pl.BlockSpec((WIN, NQ * HD), lambda n: (cur(n), C_AQ // (NQ * HD))),
        pl.BlockSpec((WIN, NKV * HD), lambda n: (cur(n), C_AK // (NKV * HD))),
        pl.BlockSpec((WIN, NKV * HD), lambda n: (prev(n), C_AK // (NKV * HD))),
        pl.BlockSpec((WIN, NKV * HD), lambda n: (cur(n), C_AV // (NKV * HD))),
        pl.BlockSpec((WIN, NKV * HD), lambda n: (prev(n), C_AV // (NKV * HD))),
    ]


def _attn_fwd(proj, sinks, name="attn_fwd"):
    Tn = proj.shape[0]
    nb = Tn // WIN

    def body(q_ref, kc_ref, kp_ref, vc_ref, vp_ref, sink_ref, o_ref):
        mask = _attn_mask(pl.program_id(0))
        for h in range(NKV):
            p, _, _, _ = _attn_probs(q_ref, kc_ref, kp_ref, sink_ref, h, mask)
            vv = jnp.concatenate([vp_ref[:, h * HD:(h + 1) * HD], vc_ref[:, h * HD:(h + 1) * HD]], axis=0).astype(BF16)
            o = _dot(p.astype(BF16), vv)
            for g in range(GRP):
                o_ref[:, (h * GRP + g) * HD:(h * GRP + g + 1) * HD] = o[g * WIN:(g + 1) * WIN, :]

    return pl.pallas_call(
        body, name=name, out_shape=jax.ShapeDtypeStruct((Tn, D), F32), grid=(nb,),
        in_specs=_attn_specs(nb, nb - 1) + [pl.BlockSpec(memory_space=pltpu.SMEM)],
        out_specs=pl.BlockSpec((WIN, D), lambda n: (n, 0)), compiler_params=_cp("parallel"),
    )(proj, proj, proj, proj, proj, sinks)


def _attn_bwd(proj, sinks, o, do, name="attn_bwd"):
    Tn = proj.shape[0]
    nb = Tn // WIN
    KW = NKV * HD

    def body(q_ref, kc_ref, kp_ref, vc_ref, vp_ref, o_ref, do_ref, sink_ref, dq_ref, dkv_ref, dsk_ref, carry, cur):
        n = pl.program_id(0)

        @pl.when(n == 0)
        def _():
            carry[...] = jnp.zeros_like(carry)
            dsk_ref[...] = jnp.zeros_like(dsk_ref)

        @pl.when(n < nb)
        def _():
            mask = _attn_mask(n)
            for h in range(NKV):
                p, ps, qs, kk = _attn_probs(q_ref, kc_ref, kp_ref, sink_ref, h, mask)
                vv = jnp.concatenate([vp_ref[:, h * HD:(h + 1) * HD], vc_ref[:, h * HD:(h + 1) * HD]], axis=0).astype(BF16)
                cols = [slice((h * GRP + g) * HD, (h * GRP + g + 1) * HD) for g in range(GRP)]
                dos = jnp.concatenate([do_ref[:, c] for c in cols], axis=0)
                os_ = jnp.concatenate([o_ref[:, c] for c in cols], axis=0)
                delta = jnp.sum(dos * os_, axis=1, keepdims=True)
                dosb = dos.astype(BF16)
                dp = _dot(dosb, vv, tb=True)
                ds = (p * (dp - delta) * (HD ** -0.5)).astype(BF16)
                dq = _dot(ds, kk)
                dkk = _dot(ds, qs, ta=True)
                dvv = _dot(p.astype(BF16), dosb, ta=True)
                dsk = ps * delta
                for g in range(GRP):
                    dq_ref[:, cols[g]] = dq[g * WIN:(g + 1) * WIN, :].astype(BF16)
                    i = h * GRP + g
                    dsk_ref[:, i:i + 1] -= jnp.sum(dsk[g * WIN:(g + 1) * WIN, :], axis=0, keepdims=True)
                dkv_ref[:, h * HD:(h + 1) * HD] = (carry[:, h * HD:(h + 1) * HD] + dkk[:WIN, :]).astype(BF16)
                dkv_ref[:, KW + h * HD:KW + (h + 1) * HD] = (carry[:, KW + h * HD:KW + (h + 1) * HD] + dvv[:WIN, :]).astype(BF16)
                cur[:, h * HD:(h + 1) * HD] = dkk[WIN:, :]
                cur[:, KW + h * HD:KW + (h + 1) * HD] = dvv[WIN:, :]
            carry[...] = cur[...]

        @pl.when(n == nb)
        def _():
            dkv_ref[...] = carry[...].astype(BF16)

    last = nb - 1
    row = pl.BlockSpec((WIN, D), lambda n: (jnp.minimum(n, last), 0))
    return pl.pallas_call(
        body, name=name,
        out_shape=(jax.ShapeDtypeStruct((Tn, D), BF16), jax.ShapeDtypeStruct((Tn, 2 * KW), BF16), jax.ShapeDtypeStruct((1, NQ), F32)),
        grid=(nb + 1,),
        in_specs=_attn_specs(nb, last) + [row, row, pl.BlockSpec(memory_space=pltpu.SMEM)],
        out_specs=(row, pl.BlockSpec((WIN, 2 * KW), lambda n: (jnp.maximum(n - 1, 0), 0)), pl.BlockSpec((1, NQ), lambda n: (0, 0))),
        scratch_shapes=[pltpu.VMEM((WIN, 2 * KW), F32), pltpu.VMEM((WIN, 2 * KW), F32)],
        compiler_params=_cp("arbitrary"),
    )(proj, proj, proj, proj, proj, o, do, sinks)


def _tri(lower):
    r = lax.broadcasted_iota(jnp.int32, (GC, GC), 0)
    c = lax.broadcasted_iota(jnp.int32, (GC, GC), 1)
    return r >= c if lower else r <= c


def _gla_gates(lr, w2_ref, gb_ref, h):
    logit = _dot(lr, w2_ref[:, h * DK:(h + 1) * DK].astype(BF16)) + gb_ref[:, h * DK:(h + 1) * DK]
    la = (jnp.minimum(logit, 0.0) - jnp.log(1.0 + jnp.exp(-jnp.abs(logit)))) * (1.0 / 16.0)
    g = _dot(_tri(True).astype(F32), la, prec=HIGHEST)
    return logit, g


def _gla_specs(nc, rev):
    idx = (lambda n: nc - 1 - n) if rev else (lambda n: n)
    return [
        pl.BlockSpec((GC, GH * DK), lambda n: (idx(n), C_GQ // (GH * DK))),
        pl.BlockSpec((GC, GH * DK), lambda n: (idx(n), C_GK // (GH * DK))),
        pl.BlockSpec((GC, GH * DV), lambda n: (idx(n), C_GV // (GH * DV))),
        pl.BlockSpec((GC, LANE), lambda n: (idx(n), 0)),
        pl.BlockSpec((LANE, GH * DK), lambda n: (0, 0)),
        pl.BlockSpec((1, GH * DK), lambda n: (0, 0)),
    ]


def _gla_fwd(proj, plr, w2p, gb, name="gla_fwd"):
    Tn = proj.shape[0]
    nc = Tn // GC

    def body(q_ref, k_ref, v_ref, lr_ref, w2_ref, gb_ref, o_ref, st_ref, S):
        @pl.when(pl.program_id(0) == 0)
        def _():
            S[...] = jnp.zeros_like(S)

        lr = lr_ref[...].astype(BF16)
        causal = _tri(True)
        for h in range(GH):
            _, g = _gla_gates(lr, w2_ref, gb_ref, h)
            gl = g[GC - 1:GC, :]
            k = k_ref[:, h * DK:(h + 1) * DK]
            v = v_ref[:, h * DV:(h + 1) * DV].astype(BF16)
            qd = (q_ref[:, h * DK:(h + 1) * DK] * (DK ** -0.5) * jnp.exp(g)).astype(BF16)
            ki = (k * jnp.exp(-g)).astype(BF16)
            ke = (k * jnp.exp(gl - g)).astype(BF16)
            att = jnp.where(causal, _dot(qd, ki, tb=True), 0.0).astype(BF16)
            sp = S[h]
            st_ref[0, h] = sp
            o_ref[:, h * DV:(h + 1) * DV] = _dot(att, v) + _dot(qd, sp.astype(BF16), tb=True)
            S[h] = sp * jnp.exp(gl) + _dot(v, ke, ta=True)

    return pl.pallas_call(
        body, name=name,
        out_shape=(jax.ShapeDtypeStruct((Tn, GH * DV), F32), jax.ShapeDtypeStruct((nc, GH, DV, DK), F32)),
        grid=(nc,), in_specs=_gla_specs(nc, False),
        out_specs=(pl.BlockSpec((GC, GH * DV), lambda n: (n, 0)), pl.BlockSpec((1, GH, DV, DK), lambda n: (n, 0, 0, 0))),
        scratch_shapes=[pltpu.VMEM((GH, DV, DK), F32)], compiler_params=_cp("arbitrary"),
    )(proj, proj, proj, plr, w2p, gb)


def _gla_bwd(proj, plr, w2p, gb, states, do, name="gla_bwd"):
    Tn = proj.shape[0]
    nc = Tn // GC

    def body(q_ref, k_ref, v_ref, lr_ref, w2_ref, gb_ref, st_ref, do_ref, dqk_ref, dv_ref, dlr_ref, dw2_ref, dgb_ref, dS):
        @pl.when(pl.program_id(0) == 0)
        def _():
            dS[...] = jnp.zeros_like(dS)
            dw2_ref[...] = jnp.zeros_like(dw2_ref)
            dgb_ref[...] = jnp.zeros_like(dgb_ref)

        lrf = lr_ref[...]
        lr = lrf.astype(BF16)
        causal = _tri(True)
        last_row = lax.broadcasted_iota(jnp.int32, (GC, DK), 0) == GC - 1
        dlr = jnp.zeros((GC, LANE), F32)
        for h in range(GH):
            logit, g = _gla_gates(lr, w2_ref, gb_ref, h)
            gl = g[GC - 1:GC, :]
            egl = jnp.exp(gl)
            eg, eng, ege = jnp.exp(g), jnp.exp(-g), jnp.exp(gl - g)
            k = k_ref[:, h * DK:(h + 1) * DK]
            v = v_ref[:, h * DV:(h + 1) * DV].astype(BF16)
            dob = do_ref[:, h * DV:(h + 1) * DV].astype(BF16)
            qd = q_ref[:, h * DK:(h + 1) * DK] * (DK ** -0.5) * eg
            ki = k * eng
            ke = k * ege
            qdb, kib, keb = qd.astype(BF16), ki.astype(BF16), ke.astype(BF16)
            att = jnp.where(causal, _dot(qdb, kib, tb=True), 0.0).astype(BF16)
            datt = jnp.where(causal, _dot(dob, v, tb=True), 0.0).astype(BF16)
            sp = st_ref[0, h]
            dsn = dS[h]
            dsnb = dsn.astype(BF16)
            dv_ref[:, h * DV:(h + 1) * DV] = (_dot(att, dob, ta=True) + _dot(keb, dsnb, tb=True)).astype(BF16)
            dqd = _dot(datt, kib) + _dot(dob, sp.astype(BF16))
            dki = _dot(datt, qdb, ta=True)
            dke = _dot(v, dsnb)
            ddec = jnp.sum(dsn * sp, axis=0, keepdims=True)
            dS[h] = dsn * egl + _dot(dob, qdb, ta=True)
            dke_ke = dke * ke
            dgl = jnp.sum(dke_ke, axis=0, keepdims=True) + ddec * egl
            dg = dqd * qd - dki * ki - dke_ke + jnp.where(last_row, dgl, 0.0)
            dqk_ref[:, h * DK:(h + 1) * DK] = (dqd * ((DK ** -0.5) * eg)).astype(BF16)
            dqk_ref[:, GH * DK + h * DK:GH * DK + (h + 1) * DK] = (dki * eng + dke * ege).astype(BF16)
            dla = _dot(_tri(False).astype(F32), dg, prec=HIGHEST)
            dlogit = dla * (1.0 / 16.0) * _sigmoid(-logit)
            dlb = dlogit.astype(BF16)
            dlr = dlr + _dot(dlb, w2_ref[:, h * DK:(h + 1) * DK].astype(BF16), tb=True)
            dw2_ref[:, h * DK:(h + 1) * DK] += _dot(lr, dlb, ta=True)
            dgb_ref[:, h * DK:(h + 1) * DK] += jnp.sum(dlogit, axis=0, keepdims=True)
        dlr_ref[...] = dlr.astype(BF16)

    rev = lambda n: nc - 1 - n
    row = pl.BlockSpec((GC, GH * DV), lambda n: (rev(n), 0))
    return pl.pallas_call(
        body, name=name,
        out_shape=(jax.ShapeDtypeStruct((Tn, 2 * GH * DK), BF16), jax.ShapeDtypeStruct((Tn, GH * DV), BF16),
                   jax.ShapeDtypeStruct((Tn, LANE), BF16), jax.ShapeDtypeStruct((LANE, GH * DK), F32),
                   jax.ShapeDtypeStruct((1, GH * DK), F32)),
        grid=(nc,),
        in_specs=_gla_specs(nc, True) + [pl.BlockSpec((1, GH, DV, DK), lambda n: (rev(n), 0, 0, 0)), row],
        out_specs=(row, row, pl.BlockSpec((GC, LANE), lambda n: (rev(n), 0)), pl.BlockSpec((LANE, GH * DK), lambda n: (0, 0)),
                   pl.BlockSpec((1, GH * DK), lambda n: (0, 0))),
        scratch_shapes=[pltpu.VMEM((GH, DV, DK), F32)], compiler_params=_cp("arbitrary"),
    )(proj, proj, proj, plr, w2p, gb, states, do)


def _merge_specs(tm):
    row = pl.BlockSpec((tm, D), lambda i: (i, 0))
    gates = [pl.BlockSpec((tm, D), lambda i, c=c: (i, c // D)) for c in (C_GR, C_GA, C_GB)]
    return row, gates, pl.BlockSpec((1, DV), lambda i: (0, 0))


def _merge_fwd(a, go, proj, gnw, name="merge_fwd", tm=256):
    Tn = a.shape[0]

    def body(a_ref, go_ref, gr_ref, ga_ref, gb_ref, w_ref, m_ref):
        for h in range(GH):
            sl = slice(h * DV, (h + 1) * DV)
            gov = go_ref[:, sl]
            r = lax.rsqrt(jnp.mean(gov * gov, axis=1, keepdims=True) + EPS)
            gr = gr_ref[:, sl]
            g2 = gov * r * w_ref[...] * (gr * _sigmoid(gr))
            m_ref[:, sl] = (_sigmoid(ga_ref[:, sl]) * a_ref[:, sl] + _sigmoid(gb_ref[:, sl]) * g2).astype(BF16)

    row, gates, vec = _merge_specs(tm)
    return pl.pallas_call(
        body, name=name, out_shape=jax.ShapeDtypeStruct((Tn, D), BF16), grid=(Tn // tm,),
        in_specs=[row, row] + gates + [vec], out_specs=row, compiler_params=_cp("parallel"),
    )(a, go, proj, proj, proj, gnw)


def _merge_bwd(dm, a, go, proj, gnw, name="merge_bwd", tm=256):
    Tn = a.shape[0]

    def body(dm_ref, a_ref, go_ref, gr_ref, ga_ref, gb_ref, w_ref, da_ref, dgo_ref, dg_ref, dw_ref):
        wv = w_ref[...]
        dw = jnp.zeros((1, DV), F32)
        for h in range(GH):
            sl = slice(h * DV, (h + 1) * DV)
            dmv, av, gov, gr = dm_ref[:, sl], a_ref[:, sl], go_ref[:, sl], gr_ref[:, sl]
            sa, sb, sg = _sigmoid(ga_ref[:, sl]), _sigmoid(gb_ref[:, sl]), _sigmoid(gr)
            r = lax.rsqrt(jnp.mean(gov * gov, axis=1, keepdims=True) + EPS)
            gn0 = gov * r
            gn = gn0 * wv
            silu = gr * sg
            dg2 = dmv * sb
            da_ref[:, sl] = dmv * sa
            dg_ref[:, D + h * DV:D + (h + 1) * DV] = (dmv * av * sa * (1.0 - sa)).astype(BF16)
            dg_ref[:, 2 * D + h * DV:2 * D + (h + 1) * DV] = (dg2 * gn * silu * (1.0 - sb)).astype(BF16)
            dg_ref[:, sl] = (dg2 * gn * (sg * (1.0 + gr * (1.0 - sg)))).astype(BF16)
            dgn = dg2 * silu
            dw = dw + jnp.sum(dgn * gn0, axis=0, keepdims=True)
            gg = dgn * wv
            dgo_ref[:, sl] = r * gg - gov * (r * r * r * jnp.mean(gg * gov, axis=1, keepdims=True))

        @pl.when(pl.program_id(0) == 0)
        def _():
            dw_ref[...] = dw

        @pl.when(pl.program_id(0) > 0)
        def _():
            dw_ref[...] += dw

    row, gates, vec = _merge_specs(tm)
    return pl.pallas_call(
        body, name=name,
        out_shape=(jax.ShapeDtypeStruct((Tn, D), F32), jax.ShapeDtypeStruct((Tn, D), F32), jax.ShapeDtypeStruct((Tn, 3 * D), BF16),
                   jax.ShapeDtypeStruct((1, DV), F32)),
        grid=(Tn // tm,), in_specs=[row, row, row] + gates + [vec],
        out_specs=(row, row, pl.BlockSpec((tm, 3 * D), lambda i: (i, 0)), vec), compiler_params=_cp("arbitrary"),
    )(dm, a, go, proj, proj, proj, gnw)


def _ffn_up(v2, wg, wu, name="ffn_up", tm=1024):
    Tn = v2.shape[0]
    tm = min(tm, Tn)

    def body(v_ref, wg_ref, wu_ref, a_ref, b_ref, ff_ref):
        vv = v_ref[...]
        a = _dot(vv, wg_ref[...])
        b = _dot(vv, wu_ref[...])
        a_ref[...] = a
        b_ref[...] = b
        ff_ref[...] = (a * _sigmoid(a) * b).astype(BF16)

    w = pl.BlockSpec((None, D, FS), lambda k, i: (k, 0, 0))
    act = pl.BlockSpec((None, tm, FS), lambda k, i: (k, i, 0))
    return pl.pallas_call(
        body, name=name,
        out_shape=(jax.ShapeDtypeStruct((NDEV, Tn, FS), F32), jax.ShapeDtypeStruct((NDEV, Tn, FS), F32),
                   jax.ShapeDtypeStruct((NDEV, Tn, FS), BF16)),
        grid=(NDEV, Tn // tm), in_specs=[pl.BlockSpec((tm, D), lambda k, i: (i, 0)), w, w], out_specs=(act, act, act),
        compiler_params=_cp("parallel", "parallel"),
    )(v2, wg, wu)


def _ffn_down(ff, wd, h1, name="ffn_down", tm=1024, tn=1024):
    Tn = h1.shape[0]
    tm = min(tm, Tn)

    def body(f_ref, w_ref, r_ref, o_ref, acc):
        k = pl.program_id(2)
        p = _dot(f_ref[...], w_ref[...])

        @pl.when(k == 0)
        def _():
            acc[...] = p + r_ref[...]

        @pl.when(k > 0)
        def _():
            acc[...] += p

        @pl.when(k == NDEV - 1)
        def _():
            o_ref[...] = acc[...]

    o = pl.BlockSpec((tm, tn), lambda i, j, k: (i, j))
    return pl.pallas_call(
        body, name=name, out_shape=jax.ShapeDtypeStruct((Tn, D), F32), grid=(Tn // tm, D // tn, NDEV),
        in_specs=[pl.BlockSpec((None, tm, FS), lambda i, j, k: (k, i, 0)), pl.BlockSpec((None, FS, tn), lambda i, j, k: (k, 0, j)), o],
        out_specs=o, scratch_shapes=[pltpu.VMEM((tm, tn), F32)], compiler_params=_cp("parallel", "parallel", "arbitrary"),
    )(ff, wd, h1)


def _ffn_dact(dh2b, wd, a, b, name="ffn_dact", tm=1024):
    Tn = dh2b.shape[0]
    tm = min(tm, Tn)

    def body(d_ref, w_ref, a_ref, b_ref, da_ref, db_ref):
        dff = _dot(d_ref[...], w_ref[...], tb=True)
        av = a_ref[...]
        sg = _sigmoid(av)
        da_ref[...] = (dff * b_ref[...] * (sg * (1.0 + av * (1.0 - sg)))).astype(BF16)
        db_ref[...] = (dff * (av * sg)).astype(BF16)

    act = pl.BlockSpec((None, tm, FS), lambda k, i: (k, i, 0))
    return pl.pallas_call(
        body, name=name,
        out_shape=(jax.ShapeDtypeStruct((NDEV, Tn, FS), BF16), jax.ShapeDtypeStruct((NDEV, Tn, FS), BF16)),
        grid=(NDEV, Tn // tm),
        in_specs=[pl.BlockSpec((tm, D), lambda k, i: (i, 0)), pl.BlockSpec((None, FS, D), lambda k, i: (k, 0, 0)), act, act],
        out_specs=(act, act), compiler_params=_cp("parallel", "parallel"),
    )(dh2b, wd, a, b)


def _ffn_dwd(ff, dh2b, name="ffn_dwd", tn=1024):
    Tn = dh2b.shape[0]

    def body(f_ref, d_ref, o_ref):
        o_ref[...] = _dot(f_ref[...], d_ref[...], ta=True).astype(BF16)

    return pl.pallas_call(
        body, name=name, out_shape=jax.ShapeDtypeStruct((NDEV, FS, D), BF16), grid=(NDEV, D // tn),
        in_specs=[pl.BlockSpec((None, Tn, FS), lambda k, j: (k, 0, 0)), pl.BlockSpec((Tn, tn), lambda k, j: (0, j))],
        out_specs=pl.BlockSpec((None, FS, tn), lambda k, j: (k, 0, j)), compiler_params=_cp("parallel", "parallel"),
    )(ff, dh2b)


def _ffn_dwgu(v2, da, db, name="ffn_dwgu", tm=1024):
    Tn = v2.shape[0]

    def body(v_ref, da_ref, db_ref, og_ref, ou_ref):
        vv = v_ref[...]
        og_ref[...] = _dot(vv, da_ref[...], ta=True).astype(BF16)
        ou_ref[...] = _dot(vv, db_ref[...], ta=True).astype(BF16)

    act = pl.BlockSpec((None, Tn, FS), lambda k, i: (k, 0, 0))
    o = pl.BlockSpec((None, tm, FS), lambda k, i: (k, i, 0))
    return pl.pallas_call(
        body, name=name,
        out_shape=(jax.ShapeDtypeStruct((NDEV, D, FS), BF16), jax.ShapeDtypeStruct((NDEV, D, FS), BF16)),
        grid=(NDEV, D // tm), in_specs=[pl.BlockSpec((Tn, tm), lambda k, i: (0, i)), act, act], out_specs=(o, o),
        compiler_params=_cp("parallel", "parallel"),
    )(v2, da, db)


def _ffn_dv2(da, db, wg, wu, name="ffn_dv2", tm=1024, tn=1024):
    Tn = da.shape[1]
    tm = min(tm, Tn)

    def body(da_ref, db_ref, wg_ref, wu_ref, o_ref, acc):
        k = pl.program_id(2)
        p = _dot(da_ref[...], wg_ref[...], tb=True) + _dot(db_ref[...], wu_ref[...], tb=True)

        @pl.when(k == 0)
        def _():
            acc[...] = p

        @pl.when(k > 0)
        def _():
            acc[...] += p

        @pl.when(k == NDEV - 1)
        def _():
            o_ref[...] = acc[...]

    act = pl.BlockSpec((None, tm, FS), lambda i, j, k: (k, i, 0))
    w = pl.BlockSpec((None, tn, FS), lambda i, j, k: (k, j, 0))
    return pl.pallas_call(
        body, name=name, out_shape=jax.ShapeDtypeStruct((Tn, D), F32), grid=(Tn // tm, D // tn, NDEV),
        in_specs=[act, act, w, w], out_specs=pl.BlockSpec((tm, tn), lambda i, j, k: (i, j)),
        scratch_shapes=[pltpu.VMEM((tm, tn), F32)], compiler_params=_cp("parallel", "parallel", "arbitrary"),
    )(da, db, wg, wu)


def _adam_math(w, g, m, v):
    m2 = B1 * m + (1.0 - B1) * g
    v2 = B2 * v + (1.0 - B2) * (g * g)
    mh = m2 / (1.0 - B1 ** STEP)
    vh = v2 / (1.0 - B2 ** STEP)
    return -LR * (mh / (jnp.sqrt(vh) + AEPS) + WD * w), m2, v2


def _adamw(w, m, v, psums, parts, chip_idx, name, tr):
    R, C = w.shape

    def body(s_ref, w_ref, m_ref, v_ref, o_ref, p_ref, g_ref, d_ref, m2_ref, v2_ref):
        g = ((o_ref[...].astype(F32) + p_ref[0].astype(F32)) + p_ref[1].astype(F32)) + p_ref[2].astype(F32)
        d, m2, v2 = _adam_math(w_ref[...], g, m_ref[...], v_ref[...])
        g_ref[...] = g
        d_ref[...] = d
        m2_ref[...] = m2
        v2_ref[...] = v2

    blk = pl.BlockSpec((tr, C), lambda i, s: (i, 0))
    out = jax.ShapeDtypeStruct((R, C), F32)
    grid_spec = pltpu.PrefetchScalarGridSpec(
        num_scalar_prefetch=1, grid=(R // tr,),
        in_specs=[blk, blk, blk, pl.BlockSpec((None, tr, C), lambda i, s: (s[0], i, 0)), pl.BlockSpec((3, tr, C), lambda i, s: (0, i, 0))],
        out_specs=(blk, blk, blk, blk),
    )
    return pl.pallas_call(body, name=name, out_shape=(out, out, out, out), grid_spec=grid_spec, compiler_params=_cp("parallel"),
                          )(chip_idx, w, m, v, psums, parts)


def _adamw_plain(w, m, v, g, name):
    def body(w_ref, m_ref, v_ref, g_ref, d_ref, m2_ref, v2_ref):
        d, m2, v2 = _adam_math(w_ref[...], g_ref[...], m_ref[...], v_ref[...])
        d_ref[...] = d
        m2_ref[...] = m2
        v2_ref[...] = v2

    out = jax.ShapeDtypeStruct(w.shape, F32)
    return pl.pallas_call(body, name=name, out_shape=(out, out, out))(w, m, v, g)


def _sum_devices(pack_all, name="sum_small"):
    def body(p_ref, o_ref):
        s = p_ref[0]
        for k in range(1, NDEV):
            s = s + p_ref[k]
        o_ref[...] = s

    return pl.pallas_call(body, name=name, out_shape=jax.ShapeDtypeStruct(pack_all.shape[1:], F32))(pack_all)


def _pair_add(g5, recv, c_idx, name, tr):
    _, _, R, C = g5.shape

    def body(c_ref, g_ref, r_ref, o_ref):
        o_ref[...] = (g_ref[...].astype(F32) + r_ref[...].astype(F32)).astype(BF16)

    grid_spec = pltpu.PrefetchScalarGridSpec(
        num_scalar_prefetch=1, grid=(4, R // tr),
        in_specs=[pl.BlockSpec((None, None, tr, C), lambda q, i, c: (q, c[0], i, 0)), pl.BlockSpec((None, tr, C), lambda q, i, c: (q, i, 0))],
        out_specs=pl.BlockSpec((None, tr, C), lambda q, i, c: (q, i, 0)),
    )
    return pl.pallas_call(
        body, name=name, out_shape=jax.ShapeDtypeStruct((4, R, C), BF16), grid_spec=grid_spec,
        compiler_params=_cp("parallel", "parallel"),
    )(c_idx, g5, recv)


_ANY = pl.BlockSpec(memory_space=pl.ANY)


def _mesh_pos():
    x, y, c = lax.axis_index("x"), lax.axis_index("y"), lax.axis_index("c")
    return x, y, c, [(1 - x, y), (x, 1 - y), (1 - x, 1 - y)]


def _all_gather(shards, name="gather_weights"):
    n = len(shards)

    def body(*refs):
        ins, outs = refs[:n], refs[n:2 * n]
        send, recv, loc = refs[2 * n + 1:]
        x, y, c, chips = _mesh_pos()
        me, sib = (x, y, c), (x, y, 1 - c)

        def cp(a, k, block, to, own=False):
            dst = outs[a].at[4 * block[0] + 2 * block[1] + block[2]]
            return pltpu.make_async_remote_copy(src_ref=ins[a] if own else dst, dst_ref=dst, send_sem=send.at[a, k],
                                                recv_sem=recv.at[a, k], device_id=to, device_id_type=MESH)

        local = [pltpu.make_async_copy(ins[a], outs[a].at[4 * x + 2 * y + c], loc.at[a]) for a in range(n)]
        first = []
        for a in range(n):
            local[a].start()
            first.append(cp(a, 0, me, sib, own=True))
            first += [cp(a, 1 + j, me, (*chip, c), own=True) for j, chip in enumerate(chips)]
        for d in first:
            d.start()
        passed = []
        for a in range(n):
            for j, chip in enumerate(chips):
                cp(a, 1 + j, (*chip, c), me).wait_recv()
                d = cp(a, 4 + j, (*chip, c), sib)
                d.start()
                passed.append(d)
        for a in range(n):
            cp(a, 0, sib, me).wait_recv()
            for j, chip in enumerate(chips):
                cp(a, 4 + j, (*chip, 1 - c), me).wait_recv()
        for d in first + passed:
            d.wait_send()
        for d in local:
            d.wait()
        refs[2 * n][...] = jnp.zeros_like(refs[2 * n])

    return pl.pallas_call(
        body, name=name,
        out_shape=tuple(jax.ShapeDtypeStruct((NDEV,) + s.shape, s.dtype) for s in shards) + (jax.ShapeDtypeStruct((8, LANE), F32),),
        in_specs=[_ANY] * n, out_specs=tuple([_ANY] * n) + (pl.BlockSpec(memory_space=pltpu.VMEM),),
        scratch_shapes=[pltpu.SemaphoreType.DMA((n, 7)), pltpu.SemaphoreType.DMA((n, 7)), pltpu.SemaphoreType.DMA((n,))],
    )(*shards)


def _pair_exchange(grads, name):
    n = len(grads)

    def body(*refs):
        ins, outs = refs[:n], refs[n:2 * n]
        send, recv = refs[2 * n:]
        x, y, c, _ = _mesh_pos()
        big = [pltpu.make_async_remote_copy(src_ref=ins[a].at[:, 1 - c], dst_ref=outs[a], send_sem=send.at[a], recv_sem=recv.at[a],
                                            device_id=(x, y, 1 - c), device_id_type=MESH) for a in range(n)]
        for d in big:
            d.start()
        for d in big:
            d.wait_recv()
        for d in big:
            d.wait_send()

    return pl.pallas_call(
        body, name=name, out_shape=tuple(jax.ShapeDtypeStruct((4,) + g.shape[2:], g.dtype) for g in grads),
        in_specs=[_ANY] * n, out_specs=tuple([_ANY] * n),
        scratch_shapes=[pltpu.SemaphoreType.DMA((n,)), pltpu.SemaphoreType.DMA((n,))],
    )(*grads)


def _gather_small(pack, name="gather_small"):
    def body(pk, pk_all, psend, precv, loc):
        x, y, c, chips = _mesh_pos()
        me_slot = 4 * x + 2 * y + c
        sib = (x, y, 1 - c)
        own = pltpu.make_async_copy(pk, pk_all.at[me_slot], loc)
        own.start()
        peers = [sib] + [(*chip, c) for chip in chips] + [(*chip, 1 - c) for chip in chips]
        small = [pltpu.make_async_remote_copy(src_ref=pk, dst_ref=pk_all.at[me_slot], send_sem=psend.at[k], recv_sem=precv.at[k],
                                              device_id=p, device_id_type=MESH) for k, p in enumerate(peers)]
        for d in small:
            d.start()
        for k, p in enumerate(peers):
            pltpu.make_async_remote_copy(src_ref=pk, dst_ref=pk_all.at[4 * p[0] + 2 * p[1] + p[2]], send_sem=psend.at[k],
                                         recv_sem=precv.at[k], device_id=p, device_id_type=MESH).wait_recv()
        for d in small:
            d.wait_send()
        own.wait()

    return pl.pallas_call(
        body, name=name, out_shape=jax.ShapeDtypeStruct((NDEV,) + pack.shape, pack.dtype), in_specs=[_ANY], out_specs=_ANY,
        scratch_shapes=[pltpu.SemaphoreType.DMA((7,)), pltpu.SemaphoreType.DMA((7,)), pltpu.SemaphoreType.DMA(())],
    )(pack)


_HBM = pl.BlockSpec(memory_space=pltpu.HBM)
_SEM = pl.BlockSpec(memory_space=pltpu.SEMAPHORE)
_VMEM = pl.BlockSpec(memory_space=pltpu.VMEM)
_SIDE = pltpu.CompilerParams(has_side_effects=pltpu.SideEffectType.DATAFLOW_SIDE_EFFECTING)
_TOKEN = jax.ShapeDtypeStruct((8, LANE), F32)


def _hbm(a):
    return pltpu.with_memory_space_constraint(a, pltpu.HBM)


def _hbm_like(arrs):
    return tuple(pltpu.HBM(a.shape, a.dtype) for a in arrs)


def _tie(x, token):
    return x + token[0, 0].astype(x.dtype)


def _chip_copies(ins, lands, send, recv):
    x, y, c, chips = _mesh_pos()
    return [pltpu.make_async_remote_copy(src_ref=ins[a].at[2 * chip[0] + chip[1]], dst_ref=lands[a].at[j], send_sem=send.at[3 * a + j],
                                         recv_sem=recv.at[3 * a + j], device_id=(*chip, c), device_id_type=MESH)
            for a in range(len(ins)) for j, chip in enumerate(chips)]


def _chip_start(psums, name):
    n = len(psums)
    lands = [lax.empty((3,) + p.shape[1:], p.dtype) for p in psums]

    def body(*refs):
        for d in _chip_copies(refs[:n], refs[n:2 * n], refs[2 * n], refs[2 * n + 1]):
            d.start()
        refs[-1][...] = jnp.zeros_like(refs[-1])

    sems = pltpu.SemaphoreType.DMA((3 * n,))
    out = pl.pallas_call(
        body, name=name, out_shape=(sems, sems) + _hbm_like(psums) + _hbm_like(lands) + (_TOKEN,),
        in_specs=[_HBM] * (2 * n), out_specs=(_SEM, _SEM) + (_HBM,) * (2 * n) + (_VMEM,),
        input_output_aliases={i: 2 + i for i in range(2 * n)}, compiler_params=_SIDE,
    )(*[_hbm(a) for a in list(psums) + lands])
    return out[0], out[1], list(out[2:2 + n]), list(out[2 + n:2 + 2 * n]), out[-1]


def _chip_wait(send, recv, psums, lands, after, name):
    n = len(psums)

    def body(*refs):
        for d in _chip_copies(refs[:n], refs[n:2 * n], refs[2 * n], refs[2 * n + 1]):
            d.wait_send()
            d.wait_recv()

    out = pl.pallas_call(
        body, name=name, out_shape=_hbm_like(psums) + _hbm_like(lands),
        in_specs=[_HBM] * (2 * n) + [_SEM, _SEM, _ANY], out_specs=(_HBM,) * (2 * n),
        input_output_aliases={i: i for i in range(2 * n)}, compiler_params=_SIDE,
    )(*psums, *lands, send, recv, after)
    return list(out[:n]), list(out[n:])


def _slot(chip, c):
    return 4 * chip[0] + 2 * chip[1] + c


def _gather_start(shards, dev, name):
    n = len(shards)
    lands = [lax.dynamic_update_slice(lax.empty((NDEV,) + s.shape, s.dtype), s[None], (dev,) + (0,) * s.ndim) for s in shards]

    def body(*refs):
        src, land, send, recv = refs[:n], refs[n:2 * n], refs[2 * n], refs[2 * n + 1]
        x, y, c, chips = _mesh_pos()
        for a in range(n):
            for k, to in enumerate([(x, y, 1 - c)] + [(*chip, c) for chip in chips]):
                pltpu.make_async_remote_copy(src_ref=src[a], dst_ref=land[a].at[_slot((x, y), c)], send_sem=send.at[4 * a + k],
                                             recv_sem=recv.at[4 * a + k], device_id=to, device_id_type=MESH).start()
        refs[-1][...] = jnp.zeros_like(refs[-1])

    sems = pltpu.SemaphoreType.DMA((4 * n,))
    out = pl.pallas_call(
        body, name=name, out_shape=(sems, sems) + _hbm_like(shards) + _hbm_like(lands) + (_TOKEN,),
        in_specs=[_HBM] * (2 * n), out_specs=(_SEM, _SEM) + (_HBM,) * (2 * n) + (_VMEM,),
        input_output_aliases={i: 2 + i for i in range(2 * n)}, compiler_params=_SIDE,
    )(*[_hbm(a) for a in list(shards) + lands])
    return out[0], out[1], list(out[2:2 + n]), list(out[2 + n:2 + 2 * n]), out[-1]


def _gather_pass(lands, recv, after, name):
    n = len(lands)

    def body(*refs):
        land, recv1 = refs[:n], refs[n]
        send2, recv2 = refs[n + 2], refs[n + 3]
        x, y, c, chips = _mesh_pos()
        for a in range(n):
            for j, chip in enumerate(chips):
                blk = land[a].at[_slot(chip, c)]
                pltpu.make_async_remote_copy(src_ref=blk, dst_ref=blk, send_sem=send2.at[3 * a + j], recv_sem=recv1.at[4 * a + 1 + j],
                                             device_id=(*chip, c), device_id_type=MESH).wait_recv()
                pltpu.make_async_remote_copy(src_ref=blk, dst_ref=blk, send_sem=send2.at[3 * a + j], recv_sem=recv2.at[3 * a + j],
                                             device_id=(x, y, 1 - c), device_id_type=MESH).start()
        refs[-1][...] = jnp.zeros_like(refs[-1])

    sems = pltpu.SemaphoreType.DMA((3 * n,))
    out = pl.pallas_call(
        body, name=name, out_shape=(sems, sems) + _hbm_like(lands) + (_TOKEN,),
        in_specs=[_HBM] * n + [_SEM, _ANY], out_specs=(_SEM, _SEM) + (_HBM,) * n + (_VMEM,),
        input_output_aliases={i: 2 + i for i in range(n)}, compiler_params=_SIDE,
    )(*lands, recv, after)
    return out[0], out[1], list(out[2:2 + n]), out[-1]


def _gather_wait(shards, lands, send, recv, send2, recv2, after, name):
    n = len(lands)

    def body(*refs):
        src, land = refs[:n], refs[n:2 * n]
        send1, recv1, snd2, rcv2 = refs[2 * n:2 * n + 4]
        x, y, c, chips = _mesh_pos()
        sib = (x, y, 1 - c)
        for a in range(n):
            for k in range(4):
                pltpu.make_async_remote_copy(src_ref=src[a], dst_ref=land[a].at[_slot((x, y), c)], send_sem=send1.at[4 * a + k],
                                             recv_sem=recv1.at[4 * a + k], device_id=sib, device_id_type=MESH).wait_send()
            blk = land[a].at[_slot((x, y), 1 - c)]
            pltpu.make_async_remote_copy(src_ref=blk, dst_ref=blk, send_sem=send1.at[4 * a], recv_sem=recv1.at[4 * a],
                                         device_id=sib, device_id_type=MESH).wait_recv()
            for j, chip in enumerate(chips):
                mine, theirs = land[a].at[_slot(chip, c)], land[a].at[_slot(chip, 1 - c)]
                pltpu.make_async_remote_copy(src_ref=mine, dst_ref=mine, send_sem=snd2.at[3 * a + j], recv_sem=rcv2.at[3 * a + j],
                                             device_id=sib, device_id_type=MESH).wait_send()
                pltpu.make_async_remote_copy(src_ref=theirs, dst_ref=theirs, send_sem=snd2.at[3 * a + j], recv_sem=rcv2.at[3 * a + j],
                                             device_id=sib, device_id_type=MESH).wait_recv()

    out = pl.pallas_call(
        body, name=name, out_shape=_hbm_like(shards) + _hbm_like(lands),
        in_specs=[_HBM] * (2 * n) + [_SEM] * 4 + [_ANY], out_specs=(_HBM,) * (2 * n),
        input_output_aliases={i: i for i in range(2 * n)}, compiler_params=_SIDE,
    )(*shards, *lands, send, recv, send2, recv2, after)
    return list(out[n:])


def _pad_to(v, n):
    return jnp.pad(v, [(0, 0)] * (v.ndim - 1) + [(0, n - v.shape[-1])])


def _pack_small(n1, gb, sk, gn, n2, fn, extra=None):
    parts = [n1.reshape(-1), gb.reshape(-1), sk.reshape(-1), gn.reshape(-1), n2.reshape(-1), fn.reshape(-1)]
    flat = jnp.concatenate(parts + ([extra.reshape(-1)] if extra is not None else []))
    return _pad_to(flat, SMALL_N).reshape(SMALL_ROWS, LANE)


def _unpack_small(p):
    f = p.reshape(-1)
    return (f[S_N1:S_GB].reshape(1, D), f[S_GB:S_SK].reshape(1, GH * DK), f[S_SK:S_GN].reshape(1, NQ), f[S_GN:S_N2].reshape(1, DV),
            f[S_N2:S_FN].reshape(1, D), f[S_FN:S_LOSS].reshape(D))


class _NoComm:
    def __init__(self, wo, wg_all, wu_all, wd_all):
        self.rest = (wo, wg_all, wu_all, wd_all)

    def mixed(self, gla_o, gla_norm_w):
        return gla_norm_w

    def rest_weights(self, merged):
        return self.rest

    def ffn_grads(self, d_wg, d_wu, d_wd, norm2_w):
        self.ffn = (d_wg, d_wu, d_wd)
        return norm2_w

    def in_grads(self, d_wmain, d_wlr, d_wo, w_lr):
        self.inw = (d_wmain, d_wlr, d_wo)
        return w_lr


class _Comm:
    def __init__(self, rest_shards, dev, c_idx):
        self.c_idx = c_idx
        self.send, self.recv, self.shards, self.lands, self.token = _gather_start(rest_shards, dev, "gather_rest_start")

    def mixed(self, gla_o, gla_norm_w):
        self.send2, self.recv2, self.lands, token = _gather_pass(self.lands, self.recv, gla_o, "gather_rest_pass")
        return _tie(gla_norm_w, token)

    def rest_weights(self, merged):
        wo_all, wg_all, wu_all, wd_all = _gather_wait(self.shards, self.lands, self.send, self.recv, self.send2, self.recv2,
                                                      merged, "gather_rest_wait")
        return wo_all.reshape(D, D), wg_all, wu_all, wd_all

    def _reduce(self, tag, names, grads, rows):
        recv1 = _pair_exchange(grads, "reduce_pair_" + tag)
        psums = [_pair_add(g, r, self.c_idx, "pair_add_" + nm, tr) for g, r, nm, tr in zip(grads, recv1, names, rows)]
        *flight, token = _chip_start(psums, "reduce_chips_start_" + tag)
        return dict(tag=tag, names=names, rows=rows, flight=flight), token

    def ffn_grads(self, d_wg, d_wu, d_wd, norm2_w):
        self.ffn, token = self._reduce("ffn", ["w_ffn_gate", "w_ffn_up", "w_ffn_down"],
                                       [d_wg.reshape(4, 2, D, FS), d_wu.reshape(4, 2, D, FS), d_wd.reshape(4, 2, FS, D)], [512, 512, 176])
        return _tie(norm2_w, token)

    def in_grads(self, d_wmain, d_wlr, d_wo, w_lr):
        d_win = _from_main(d_wmain, d_wlr[:, :RANK]).reshape(D, 4, 2, WS).transpose(1, 2, 0, 3)
        self.inw, token = self._reduce("in", ["w_in", "w_out"], [d_win, d_wo.reshape(4, 2, D // NDEV, D)], [256, 256])
        return _tie(w_lr, token)


def _local_step(xs, tgt, norm1_w, gla_gate_b, attn_sinks, gla_norm_w, norm2_w, fnw, w_main, w_lr, w2p, comm):
    u = _rmsnorm_fwd(xs, norm1_w, "norm1_fwd")
    proj = _mm(u, w_main, tm=1024, tn=640, tk=D, name="in_proj")
    plr = _mm(u, w_lr, tm=1024, tn=LANE, tk=D, name="in_proj_lr")
    attn_o = _attn_fwd(proj, attn_sinks)
    gla_o, states = _gla_fwd(proj, plr, w2p, gla_gate_b)
    merged = _merge_fwd(attn_o, gla_o, proj, comm.mixed(gla_o, gla_norm_w))
    wo, wg_all, wu_all, wd_all = comm.rest_weights(merged)
    h1 = _mm(merged, wo, tm=1024, tn=512, tk=D, res=xs, name="out_proj")
    v2 = _rmsnorm_fwd(h1, norm2_w, "norm2_fwd")
    fa, fb, ff = _ffn_up(v2, wg_all, wu_all)
    h2 = _ffn_down(ff, wd_all, h1)
    dh2, dh2b, d_fnw, loss_part = _loss_head(h2, fnw, tgt)

    da, db = _ffn_dact(dh2b, wd_all, fa, fb)
    d_wd = _ffn_dwd(ff, dh2b)
    d_wg, d_wu = _ffn_dwgu(v2, da, db)
    norm2_w = comm.ffn_grads(d_wg, d_wu, d_wd, norm2_w)
    dv2 = _ffn_dv2(da, db, wg_all, wu_all)
    dh1, dh1b, d_n2 = _rmsnorm_bwd(dv2, h1, norm2_w, dh2, "norm2_bwd")
    dmerged = _mm(dh1b, wo, tb=True, tm=1024, tn=512, tk=D, name="out_proj_dx")
    d_wo = _mm(merged, dh1b, ta=True, tm=1024, tn=512, tk=xs.shape[0], out_dtype=BF16, name="out_proj_dw")
    d_attn, d_gla, d_gates, d_gnw = _merge_bwd(dmerged, attn_o, gla_o, proj, gla_norm_w)
    d_q, d_kv, d_sinks = _attn_bwd(proj, attn_sinks, attn_o, d_attn)
    d_gqk, d_gv, d_plr, d_w2p, d_gb = _gla_bwd(proj, plr, w2p, gla_gate_b, states, d_gla)
    dproj = jnp.concatenate([d_gates, d_q, d_gv, d_gqk, d_kv], axis=1)
    d_wmain = _mm(u, dproj, ta=True, tm=1024, tn=640, tk=xs.shape[0], out_dtype=BF16, name="in_proj_dw")
    d_wlr = _mm(u, d_plr, ta=True, tm=1024, tn=LANE, tk=xs.shape[0], out_dtype=BF16, name="in_proj_lr_dw")
    du_lr = _mm(d_plr, comm.in_grads(d_wmain, d_wlr, d_wo, w_lr), tb=True, tm=1024, tn=1024, tk=LANE, name="in_proj_lr_dx")
    du = _mm(dproj, w_main, tb=True, tm=1024, tn=1024, tk=1280, res=du_lr, name="in_proj_dx")
    dx, _, d_n1 = _rmsnorm_bwd(du, xs, norm1_w, dh1, "norm1_bwd")
    return dx, loss_part, d_w2p, d_gb, d_sinks, d_gnw, d_n1, d_n2, d_fnw


def kernel(x, norm1_w, w_in, gla_gate_w2, gla_gate_b, attn_sinks, gla_norm_w, w_out, norm2_w, w_ffn_gate, w_ffn_up, w_ffn_down, final_norm_w, loss_target, m_norm1_w, m_w_in, m_gla_gate_w2, m_gla_gate_b, m_attn_sinks, m_gla_norm_w, m_w_out, m_norm2_w, m_w_ffn_gate, m_w_ffn_up, m_w_ffn_down, m_final_norm_w, v_norm1_w, v_w_in, v_gla_gate_w2, v_gla_gate_b, v_attn_sinks, v_gla_norm_w, v_w_out, v_norm2_w, v_w_ffn_gate, v_w_ffn_up, v_w_ffn_down, v_final_norm_w):
    xs, tgt = x[0], loss_target[0]
    fnw = final_norm_w.reshape(1, D)
    c_idx = lax.axis_index("c").astype(jnp.int32).reshape(1)
    dev = 4 * lax.axis_index("x") + 2 * lax.axis_index("y") + lax.axis_index("c")

    chip_idx = (2 * lax.axis_index("x") + lax.axis_index("y")).astype(jnp.int32).reshape(1)

    win_all, w2_all, tok = _all_gather([w_in[0].astype(BF16), gla_gate_w2[0]], name="gather_w_in")
    rest = [(w[0] + tok[0, 0]).astype(BF16) for w in (w_out, w_ffn_gate, w_ffn_up, w_ffn_down)]
    comm = _Comm(rest, dev, c_idx)
    w_main, w_lr = _to_main(jnp.transpose(win_all, (1, 0, 2)).reshape(D, DIN))
    w_lr = _pad_to(w_lr, LANE)
    w2p = jnp.pad(jnp.transpose(w2_all, (1, 0, 2)).reshape(RANK, GH * DK), ((0, LANE - RANK), (0, 0)))

    dx, loss_part, d_w2p, d_gb, d_sinks, d_gnw, d_n1, d_n2, d_fnw = _local_step(
        xs, tgt, _tie(norm1_w, comm.token), gla_gate_b, attn_sinks, gla_norm_w, norm2_w, fnw, w_main, w_lr, w2p, comm)

    pack = jnp.concatenate([_pack_small(d_n1, d_gb, d_sinks, d_gnw, d_n2, d_fnw, loss_part),
                            d_w2p[:RANK].reshape(GW2_ROWS, LANE)], axis=0)
    small = _sum_devices(_gather_small(pack))

    big = {}
    after = dx
    for grp in (comm.ffn, comm.inw):
        psums, parts = _chip_wait(*grp["flight"], after, "reduce_chips_wait_" + grp["tag"])
        for nm, ps, pt, tr in zip(grp["names"], psums, parts, grp["rows"]):
            w, m, v = {"w_in": (w_in, m_w_in, v_w_in), "w_out": (w_out, m_w_out, v_w_out), "w_ffn_gate": (w_ffn_gate, m_w_ffn_gate, v_w_ffn_gate),
                       "w_ffn_up": (w_ffn_up, m_w_ffn_up, v_w_ffn_up), "w_ffn_down": (w_ffn_down, m_w_ffn_down, v_w_ffn_down)}[nm]
            big[nm] = [t[None] for t in _adamw(w[0], m[0], v[0], ps, pt, chip_idx, "adamw_" + nm, tr)]
            after = big[nm][0]
    g_small = small[:SMALL_ROWS]
    sm = _adamw_plain(_pack_small(norm1_w, gla_gate_b, attn_sinks, gla_norm_w, norm2_w, final_norm_w),
                      _pack_small(m_norm1_w, m_gla_gate_b, m_attn_sinks, m_gla_norm_w, m_norm2_w, m_final_norm_w),
                      _pack_small(v_norm1_w, v_gla_gate_b, v_attn_sinks, v_gla_norm_w, v_norm2_w, v_final_norm_w), g_small, "adamw_small")
    g_w2 = lax.dynamic_slice_in_dim(small[SMALL_ROWS:].reshape(RANK, GH * DK), dev * LANE, LANE, axis=1)
    w2 = [g_w2[None]] + [t[None] for t in _adamw_plain(gla_gate_w2[0], m_gla_gate_w2[0], v_gla_gate_w2[0], g_w2, "adamw_w2")]
    loss = g_small.reshape(-1)[S_LOSS]

    sg, sd, sm2, sv2 = [_unpack_small(t) for t in (g_small,) + tuple(sm)]

    def group(i, s):
        return (s[0], big["w_in"][i], w2[i], s[1], s[2], s[3], big["w_out"][i], s[4], big["w_ffn_gate"][i], big["w_ffn_up"][i],
                big["w_ffn_down"][i], s[5])

    return (loss, dx[None], *group(0, sg), *group(1, sd), *group(2, sm2), *group(3, sv2))
```

```python
import functools

import jax
import jax.numpy as jnp
from jax import lax
from jax.experimental import pallas as pl
from jax.experimental.pallas import tpu as pltpu

F32, BF16 = jnp.float32, jnp.bfloat16
HIGHEST = lax.Precision.HIGHEST

D = 2048
HD, NQ, NKV, GRP, WIN = 64, 32, 4, 8, 128
GH, DK, DV, RANK, GC = 4, 256, 512, 16, 64
FH, NDEV = 5632, 8
FS = FH // NDEV
DIN = 12816
WS = DIN // NDEV
EPS = 1e-6
MASKV = -1e30
LANE = 128

C_AQ, C_AK, C_AV, C_GQ, C_GK, C_GV, C_GR, C_GA, C_GB, NMAIN = 0, 2048, 2304, 2560, 3584, 4608, 6656, 8704, 10752, 12800
C_LR = 6656
WSTEP, WWIN = 1600, 1616

LR, B1, B2, AEPS, WD, STEP = 0.001, 0.9, 0.999, 1e-08, 0.01, 10

S_N1, S_GB, S_SK, S_GN, S_N2, S_FN, S_LOSS, SMALL_N = 0, 2048, 3072, 3104, 3616, 5664, 7712, 8192
SMALL_ROWS = SMALL_N // LANE
GW2_ROWS = RANK * GH * DK // LANE
PACK_ROWS = SMALL_ROWS + GW2_ROWS

MESH = pl.DeviceIdType.MESH


def _dot(a, b, ta=False, tb=False, prec=None):
    dn = (((0,) if ta else (1,), (1,) if tb else (0,)), ((), ()))
    return lax.dot_general(a, b, dn, preferred_element_type=F32, precision=prec)


def _sigmoid(x):
    return 1.0 / (1.0 + jnp.exp(-x))


VMEM_LIMIT = 56 * 1024 * 1024


def _cp(*sem):
    return pltpu.CompilerParams(dimension_semantics=sem, vmem_limit_bytes=VMEM_LIMIT)


def _mm(a, b, *, ta=False, tb=False, tm, tn, tk, out_dtype=F32, res=None, name):
    M, K = (a.shape[1], a.shape[0]) if ta else a.shape
    N = b.shape[0] if tb else b.shape[1]
    tm, tn, tk = min(tm, M), min(tn, N), min(tk, K)
    nk = K // tk
    assert M % tm == 0 and N % tn == 0 and K % tk == 0
    a_spec = pl.BlockSpec((tk, tm), lambda i, j, k: (k, i)) if ta else pl.BlockSpec((tm, tk), lambda i, j, k: (i, k))
    b_spec = pl.BlockSpec((tn, tk), lambda i, j, k: (j, k)) if tb else pl.BlockSpec((tk, tn), lambda i, j, k: (k, j))
    o_spec = pl.BlockSpec((tm, tn), lambda i, j, k: (i, j))
    has_res = res is not None

    def body(*refs):
        a_ref, b_ref = refs[0], refs[1]
        r_ref = refs[2] if has_res else None
        o_ref = refs[3] if has_res else refs[2]
        p = _dot(a_ref[...].astype(BF16), b_ref[...].astype(BF16), ta, tb)
        if nk == 1:
            if has_res:
                p = p + r_ref[...]
            o_ref[...] = p.astype(out_dtype)
        else:
            acc = refs[-1]
            k = pl.program_id(2)

            @pl.when(k == 0)
            def _():
                acc[...] = (p + r_ref[...]) if has_res else p

            @pl.when(k > 0)
            def _():
                acc[...] += p

            @pl.when(k == nk - 1)
            def _():
                o_ref[...] = acc[...].astype(out_dtype)

    return pl.pallas_call(
        body, name=name,
        out_shape=jax.ShapeDtypeStruct((M, N), out_dtype),
        grid=(M // tm, N // tn, nk),
        in_specs=[a_spec, b_spec] + ([o_spec] if has_res else []),
        out_specs=o_spec,
        scratch_shapes=[pltpu.VMEM((tm, tn), F32)] if nk > 1 else [],
        compiler_params=_cp("parallel", "parallel", "arbitrary"),
    )(*((a, b, res) if has_res else (a, b)))


def _rmsnorm_fwd(x, w, name, tm=256):
    Tn = x.shape[0]

    def body(x_ref, w_ref, o_ref):
        xv = x_ref[...]
        r = lax.rsqrt(jnp.mean(xv * xv, axis=1, keepdims=True) + EPS)
        o_ref[...] = (xv * r * w_ref[...]).astype(BF16)

    return pl.pallas_call(
        body, name=name, out_shape=jax.ShapeDtypeStruct((Tn, D), BF16), grid=(Tn // tm,),
        in_specs=[pl.BlockSpec((tm, D), lambda i: (i, 0)), pl.BlockSpec((1, D), lambda i: (0, 0))],
        out_specs=pl.BlockSpec((tm, D), lambda i: (i, 0)), compiler_params=_cp("parallel"),
    )(x, w)


def _rmsnorm_bwd(dy, h, w, res, name, tm=256):
    Tn = h.shape[0]

    def body(dy_ref, h_ref, w_ref, res_ref, dh_ref, dhb_ref, dw_ref):
        hv, dyv = h_ref[...], dy_ref[...]
        r = lax.rsqrt(jnp.mean(hv * hv, axis=1, keepdims=True) + EPS)
        g = dyv * w_ref[...]
        dh = res_ref[...] + r * g - hv * (r * r * r * jnp.mean(g * hv, axis=1, keepdims=True))
        dh_ref[...] = dh
        dhb_ref[...] = dh.astype(BF16)
        part = jnp.sum(dyv * hv * r, axis=0, keepdims=True)

        @pl.when(pl.program_id(0) == 0)
        def _():
            dw_ref[...] = part

        @pl.when(pl.program_id(0) > 0)
        def _():
            dw_ref[...] += part

    row = pl.BlockSpec((tm, D), lambda i: (i, 0))
    vec = pl.BlockSpec((1, D), lambda i: (0, 0))
    return pl.pallas_call(
        body, name=name,
        out_shape=(jax.ShapeDtypeStruct((Tn, D), F32), jax.ShapeDtypeStruct((Tn, D), BF16), jax.ShapeDtypeStruct((1, D), F32)),
        grid=(Tn // tm,), in_specs=[row, row, vec, row], out_specs=(row, row, vec), compiler_params=_cp("arbitrary"),
    )(dy, h, w, res)


def _loss_head(h2, wf, tgt, name="loss_head", tm=256):
    Tn = h2.shape[0]

    def body(h_ref, w_ref, t_ref, dh_ref, dhb_ref, dw_ref, loss_ref):
        hv, wv = h_ref[...], w_ref[...]
        r = lax.rsqrt(jnp.mean(hv * hv, axis=1, keepdims=True) + EPS)
        hn = hv * r
        e = hn * wv - t_ref[...]
        dy = e * (1.0 / D)
        g = dy * wv
        dh = r * g - hv * (r * r * r * jnp.mean(g * hv, axis=1, keepdims=True))
        dh_ref[...] = dh
        dhb_ref[...] = dh.astype(BF16)
        part = jnp.sum(dy * hn, axis=0, keepdims=True)
        lpart = (0.5 / D) * jnp.sum(jnp.sum(e * e, axis=1, keepdims=True), axis=0, keepdims=True)

        @pl.when(pl.program_id(0) == 0)
        def _():
            dw_ref[...] = part
            loss_ref[...] = lpart

        @pl.when(pl.program_id(0) > 0)
        def _():
            dw_ref[...] += part
            loss_ref[...] += lpart

    row = pl.BlockSpec((tm, D), lambda i: (i, 0))
    vec = pl.BlockSpec((1, D), lambda i: (0, 0))
    one = pl.BlockSpec((1, 1), lambda i: (0, 0))
    return pl.pallas_call(
        body, name=name,
        out_shape=(jax.ShapeDtypeStruct((Tn, D), F32), jax.ShapeDtypeStruct((Tn, D), BF16), jax.ShapeDtypeStruct((1, D), F32),
                   jax.ShapeDtypeStruct((1, 1), F32)),
        grid=(Tn // tm,), in_specs=[row, vec, row], out_specs=(row, row, vec, one), compiler_params=_cp("arbitrary"),
    )(h2, wf, tgt)


def _attn_mask(n):
    qi = lax.broadcasted_iota(jnp.int32, (GRP * WIN, 2 * WIN), 0) % WIN
    ki = lax.broadcasted_iota(jnp.int32, (GRP * WIN, 2 * WIN), 1)
    rel = qi + WIN - ki
    return (rel >= 0) & (rel < WIN) & ((n > 0) | (ki >= WIN))


def _attn_probs(q_ref, kc_ref, kp_ref, sink_ref, h, mask):
    kk = jnp.concatenate([kp_ref[:, h * HD:(h + 1) * HD], kc_ref[:, h * HD:(h + 1) * HD]], axis=0).astype(BF16)
    qs = jnp.concatenate([q_ref[:, (h * GRP + g) * HD:(h * GRP + g + 1) * HD] for g in range(GRP)], axis=0).astype(BF16)
    s = _dot(qs, kk, tb=True) * (HD ** -0.5)
    s = jnp.where(mask, s, MASKV)
    sink = jnp.concatenate([jnp.full((WIN, 1), sink_ref[0, h * GRP + g], F32) for g in range(GRP)], axis=0)
    m = jnp.maximum(jnp.max(s, axis=1, keepdims=True), sink)
    e = jnp.exp(s - m)
    es = jnp.exp(sink - m)
    inv = 1.0 / (jnp.sum(e, axis=1, keepdims=True) + es)
    return e * inv, es * inv, qs, kk


def _attn_specs(nb, last):
    cur = lambda n: jnp.minimum(n, last)
    prev = lambda n: jnp.maximum(jnp.minimum(n, last) - 1, 0)
    return [
        pl.BlockSpec((WIN, NQ * HD), lambda n: (cur(n), C_AQ // (NQ * HD))),
        pl.BlockSpec((WIN, NKV * HD), lambda n: (cur(n), C_AK // (NKV * HD))),
        pl.BlockSpec((WIN, NKV * HD), lambda n: (prev(n), C_AK // (NKV * HD))),
        pl.BlockSpec((WIN, NKV * HD), lambda n: (cur(n), C_AV // (NKV * HD))),
        pl.BlockSpec((WIN, NKV * HD), lambda n: (prev(n), C_AV // (NKV * HD))),
    ]


def _attn_fwd(proj, sinks, name="attn_fwd"):
    Tn = proj.shape[0]
    nb = Tn // WIN

    def body(q_ref, kc_ref, kp_ref, vc_ref, vp_ref, sink_ref, o_ref):
        mask = _attn_mask(pl.program_id(0))
        for h in range(NKV):
            p, _, _, _ = _attn_probs(q_ref, kc_ref, kp_ref, sink_ref, h, mask)
            vv = jnp.concatenate([vp_ref[:, h * HD:(h + 1) * HD], vc_ref[:, h * HD:(h + 1) * HD]], axis=0).astype(BF16)
            o = _dot(p.astype(BF16), vv)
            for g in range(GRP):
                o_ref[:, (h * GRP + g) * HD:(h * GRP + g + 1) * HD] = o[g * WIN:(g + 1) * WIN, :]

    return pl.pallas_call(
        body, name=name, out_shape=jax.ShapeDtypeStruct((Tn, D), F32), grid=(nb,),
        in_specs=_attn_specs(nb, nb - 1) + [pl.BlockSpec(memory_space=pltpu.SMEM)],
        out_specs=pl.BlockSpec((WIN, D), lambda n: (n, 0)), compiler_params=_cp("parallel"),
    )(proj, proj, proj, proj, proj, sinks)


def _attn_bwd(proj, sinks, o, do, name="attn_bwd"):
    Tn = proj.shape[0]
    nb = Tn // WIN
    KW = NKV * HD

    def body(q_ref, kc_ref, kp_ref, vc_ref, vp_ref, o_ref, do_ref, sink_ref, dq_ref, dkv_ref, dsk_ref, carry, cur):
        n = pl.program_id(0)

        @pl.when(n == 0)
        def _():
            carry[...] = jnp.zeros_like(carry)
            dsk_ref[...] = jnp.zeros_like(dsk_ref)

        @pl.when(n < nb)
        def _():
            mask = _attn_mask(n)
            for h in range(NKV):
                p, ps, qs, kk = _attn_probs(q_ref, kc_ref, kp_ref, sink_ref, h, mask)
                vv = jnp.concatenate([vp_ref[:, h * HD:(h + 1) * HD], vc_ref[:, h * HD:(h + 1) * HD]], axis=0).astype(BF16)
                cols = [slice((h * GRP + g) * HD, (h * GRP + g + 1) * HD) for g in range(GRP)]
                dos = jnp.concatenate([do_ref[:, c] for c in cols], axis=0)
                os_ = jnp.concatenate([o_ref[:, c] for c in cols], axis=0)
                delta = jnp.sum(dos * os_, axis=1, keepdims=True)
                dosb = dos.astype(BF16)
                dp = _dot(dosb, vv, tb=True)
                ds = (p * (dp - delta) * (HD ** -0.5)).astype(BF16)
                dq = _dot(ds, kk)
                dkk = _dot(ds, qs, ta=True)
                dvv = _dot(p.astype(BF16), dosb, ta=True)
                dsk = ps * delta
                for g in range(GRP):
                    dq_ref[:, cols[g]] = dq[g * WIN:(g + 1) * WIN, :].astype(BF16)
                    i = h * GRP + g
                    dsk_ref[:, i:i + 1] -= jnp.sum(dsk[g * WIN:(g + 1) * WIN, :], axis=0, keepdims=True)
                dkv_ref[:, h * HD:(h + 1) * HD] = (carry[:, h * HD:(h + 1) * HD] + dkk[:WIN, :]).astype(BF16)
                dkv_ref[:, KW + h * HD:KW + (h + 1) * HD] = (carry[:, KW + h * HD:KW + (h + 1) * HD] + dvv[:WIN, :]).astype(BF16)
                cur[:, h * HD:(h + 1) * HD] = dkk[WIN:, :]
                cur[:, KW + h * HD:KW + (h + 1) * HD] = dvv[WIN:, :]
            carry[...] = cur[...]

        @pl.when(n == nb)
        def _():
            dkv_ref[...] = carry[...].astype(BF16)

    last = nb - 1
    row = pl.BlockSpec((WIN, D), lambda n: (jnp.minimum(n, last), 0))
    return pl.pallas_call(
        body, name=name,
        out_shape=(jax.ShapeDtypeStruct((Tn, D), BF16), jax.ShapeDtypeStruct((Tn, 2 * KW), BF16), jax.ShapeDtypeStruct((1, NQ), F32)),
        grid=(nb + 1,),
        in_specs=_attn_specs(nb, last) + [row, row, pl.BlockSpec(memory_space=pltpu.SMEM)],
        out_specs=(row, pl.BlockSpec((WIN, 2 * KW), lambda n: (jnp.maximum(n - 1, 0), 0)), pl.BlockSpec((1, NQ), lambda n: (0, 0))),
        scratch_shapes=[pltpu.VMEM((WIN, 2 * KW), F32), pltpu.VMEM((WIN, 2 * KW), F32)],
        compiler_params=_cp("arbitrary"),
    )(proj, proj, proj, proj, proj, o, do, sinks)


def _tri(lower):
    r = lax.broadcasted_iota(jnp.int32, (GC, GC), 0)
    c = lax.broadcasted_iota(jnp.int32, (GC, GC), 1)
    return r >= c if lower else r <= c


def _gla_gates(lr, w2_ref, gb_ref, h):
    logit = _dot(lr, w2_ref[:, h * DK:(h + 1) * DK].astype(BF16)) + gb_ref[:, h * DK:(h + 1) * DK]
    la = (jnp.minimum(logit, 0.0) - jnp.log(1.0 + jnp.exp(-jnp.abs(logit)))) * (1.0 / 16.0)
    g = _dot(_tri(True).astype(F32), la, prec=HIGHEST)
    return logit, g


def _gla_specs(nc, rev):
    idx = (lambda n: nc - 1 - n) if rev else (lambda n: n)
    half = 2 * DK
    return (
        [pl.BlockSpec((GC, half), lambda n, j=j: (idx(n), C_GQ // half + j)) for j in range(2)]
        + [pl.BlockSpec((GC, half), lambda n, j=j: (idx(n), C_GK // half + j)) for j in range(2)]
        + [pl.BlockSpec((GC, DV), lambda n, h=h: (idx(n), C_GV // DV + h)) for h in range(GH)]
        + [pl.BlockSpec((GC, LANE), lambda n: (idx(n), 0)), pl.BlockSpec((LANE, GH * DK), lambda n: (0, 0)),
           pl.BlockSpec((1, GH * DK), lambda n: (0, 0))])


def _gla_heads(refs):
    return (lambda h: refs[h // 2][:, (h % 2) * DK:(h % 2 + 1) * DK], lambda h: refs[2 + h // 2][:, (h % 2) * DK:(h % 2 + 1) * DK],
            lambda h: refs[4 + h][...])


def _gla_fwd(proj, plr, w2p, gb, name="gla_fwd"):
    Tn = proj.shape[0]
    nc = Tn // GC

    def body(*refs):
        qh, kh, vh = _gla_heads(refs)
        lr_ref, w2_ref, gb_ref, o_ref, st_ref, S = refs[8:]

        @pl.when(pl.program_id(0) == 0)
        def _():
            S[...] = jnp.zeros_like(S)

        lr = lr_ref[...].astype(BF16)
        causal = _tri(True)
        for h in range(GH):
            _, g = _gla_gates(lr, w2_ref, gb_ref, h)
            gl = g[GC - 1:GC, :]
            k = kh(h)
            v = vh(h).astype(BF16)
            qd = (qh(h) * (DK ** -0.5) * jnp.exp(g)).astype(BF16)
            ki = (k * jnp.exp(-g)).astype(BF16)
            ke = (k * jnp.exp(gl - g)).astype(BF16)
            att = jnp.where(causal, _dot(qd, ki, tb=True), 0.0).astype(BF16)
            sp = S[h]
            st_ref[0, h] = sp
            o_ref[:, h * DV:(h + 1) * DV] = _dot(att, v) + _dot(qd, sp.astype(BF16), tb=True)
            S[h] = sp * jnp.exp(gl) + _dot(v, ke, ta=True)

    return pl.pallas_call(
        body, name=name,
        out_shape=(jax.ShapeDtypeStruct((Tn, GH * DV), F32), jax.ShapeDtypeStruct((nc, GH, DV, DK), F32)),
        grid=(nc,), in_specs=_gla_specs(nc, False),
        out_specs=(pl.BlockSpec((GC, GH * DV), lambda n: (n, 0)), pl.BlockSpec((1, GH, DV, DK), lambda n: (n, 0, 0, 0))),
        scratch_shapes=[pltpu.VMEM((GH, DV, DK), F32)], compiler_params=_cp("arbitrary"),
    )(*([proj] * 8), plr, w2p, gb)


def _gla_bwd(proj, plr, w2p, gb, states, do, name="gla_bwd"):
    Tn = proj.shape[0]
    nc = Tn // GC

    def body(*refs):
        qh, kh, vh = _gla_heads(refs)
        lr_ref, w2_ref, gb_ref, st_ref, do_ref, dqk_ref, dv_ref, dlr_ref, dw2_ref, dgb_ref, dS = refs[8:]

        @pl.when(pl.program_id(0) == 0)
        def _():
            dS[...] = jnp.zeros_like(dS)
            dw2_ref[...] = jnp.zeros_like(dw2_ref)
            dgb_ref[...] = jnp.zeros_like(dgb_ref)

        lrf = lr_ref[...]
        lr = lrf.astype(BF16)
        causal = _tri(True)
        last_row = lax.broadcasted_iota(jnp.int32, (GC, DK), 0) == GC - 1
        dlr = jnp.zeros((GC, LANE), F32)
        for h in range(GH):
            logit, g = _gla_gates(lr, w2_ref, gb_ref, h)
            gl = g[GC - 1:GC, :]
            egl = jnp.exp(gl)
            eg, eng, ege = jnp.exp(g), jnp.exp(-g), jnp.exp(gl - g)
            k = kh(h)
            v = vh(h).astype(BF16)
            dob = do_ref[:, h * DV:(h + 1) * DV].astype(BF16)
            qd = qh(h) * (DK ** -0.5) * eg
            ki = k * eng
            ke = k * ege
            qdb, kib, keb = qd.astype(BF16), ki.astype(BF16), ke.astype(BF16)
            att = jnp.where(causal, _dot(qdb, kib, tb=True), 0.0).astype(BF16)
            datt = jnp.where(causal, _dot(dob, v, tb=True), 0.0).astype(BF16)
            sp = st_ref[0, h]
            dsn = dS[h]
            dsnb = dsn.astype(BF16)
            dv_ref[:, h * DV:(h + 1) * DV] = (_dot(att, dob, ta=True) + _dot(keb, dsnb, tb=True)).astype(BF16)
            dqd = _dot(datt, kib) + _dot(dob, sp.astype(BF16))
            dki = _dot(datt, qdb, ta=True)
            dke = _dot(v, dsnb)
            ddec = jnp.sum(dsn * sp, axis=0, keepdims=True)
            dS[h] = dsn * egl + _dot(dob, qdb, ta=True)
            dke_ke = dke * ke
            dgl = jnp.sum(dke_ke, axis=0, keepdims=True) + ddec * egl
            dg = dqd * qd - dki * ki - dke_ke + jnp.where(last_row, dgl, 0.0)
            dqk_ref[:, h * DK:(h + 1) * DK] = (dqd * ((DK ** -0.5) * eg)).astype(BF16)
            dqk_ref[:, GH * DK + h * DK:GH * DK + (h + 1) * DK] = (dki * eng + dke * ege).astype(BF16)
            dla = _dot(_tri(False).astype(F32), dg, prec=HIGHEST)
            dlogit = dla * (1.0 / 16.0) * _sigmoid(-logit)
            dlb = dlogit.astype(BF16)
            dlr = dlr + _dot(dlb, w2_ref[:, h * DK:(h + 1) * DK].astype(BF16), tb=True)
            dw2_ref[:, h * DK:(h + 1) * DK] += _dot(lr, dlb, ta=True)
            dgb_ref[:, h * DK:(h + 1) * DK] += jnp.sum(dlogit, axis=0, keepdims=True)
        dlr_ref[...] = dlr.astype(BF16)

    rev = lambda n: nc - 1 - n
    row = pl.BlockSpec((GC, GH * DV), lambda n: (rev(n), 0))
    return pl.pallas_call(
        body, name=name,
        out_shape=(jax.ShapeDtypeStruct((Tn, 2 * GH * DK), BF16), jax.ShapeDtypeStruct((Tn, GH * DV), BF16),
                   jax.ShapeDtypeStruct((Tn, LANE), BF16), jax.ShapeDtypeStruct((LANE, GH * DK), F32),
                   jax.ShapeDtypeStruct((1, GH * DK), F32)),
        grid=(nc,),
        in_specs=_gla_specs(nc, True) + [pl.BlockSpec((1, GH, DV, DK), lambda n: (rev(n), 0, 0, 0)), row],
        out_specs=(row, row, pl.BlockSpec((GC, LANE), lambda n: (rev(n), 0)), pl.BlockSpec((LANE, GH * DK), lambda n: (0, 0)),
                   pl.BlockSpec((1, GH * DK), lambda n: (0, 0))),
        scratch_shapes=[pltpu.VMEM((GH, DV, DK), F32)], compiler_params=_cp("arbitrary"),
    )(*([proj] * 8), plr, w2p, gb, states, do)


def _merge_specs(tm):
    row = pl.BlockSpec((tm, D), lambda i: (i, 0))
    gates = [pl.BlockSpec((tm, DV), lambda i, j=c // DV + h: (i, j)) for c in (C_GR, C_GA, C_GB) for h in range(GH)]
    return row, gates, pl.BlockSpec((1, DV), lambda i: (0, 0))


def _merge_fwd(a, go, proj, gnw, name="merge_fwd", tm=256):
    Tn = a.shape[0]

    def body(a_ref, go_ref, *rest):
        gates, w_ref, m_ref = rest[:3 * GH], rest[3 * GH], rest[3 * GH + 1]
        for h in range(GH):
            sl = slice(h * DV, (h + 1) * DV)
            gov = go_ref[:, sl]
            r = lax.rsqrt(jnp.mean(gov * gov, axis=1, keepdims=True) + EPS)
            gr = gates[h][...]
            g2 = gov * r * w_ref[...] * (gr * _sigmoid(gr))
            m_ref[:, sl] = (_sigmoid(gates[GH + h][...]) * a_ref[:, sl] + _sigmoid(gates[2 * GH + h][...]) * g2).astype(BF16)

    row, gates, vec = _merge_specs(tm)
    return pl.pallas_call(
        body, name=name, out_shape=jax.ShapeDtypeStruct((Tn, D), BF16), grid=(Tn // tm,),
        in_specs=[row, row] + gates + [vec], out_specs=row, compiler_params=_cp("parallel"),
    )(a, go, *([proj] * (3 * GH)), gnw)


def _merge_bwd(dm, a, go, proj, gnw, name="merge_bwd", tm=256):
    Tn = a.shape[0]

    def body(dm_ref, a_ref, go_ref, *rest):
        gates = rest[:3 * GH]
        w_ref, da_ref, dgo_ref, dg_ref, dw_ref = rest[3 * GH:]
        wv = w_ref[...]
        dw = jnp.zeros((1, DV), F32)
        for h in range(GH):
            sl = slice(h * DV, (h + 1) * DV)
            dmv, av, gov, gr = dm_ref[:, sl], a_ref[:, sl], go_ref[:, sl], gates[h][...]
            sa, sb, sg = _sigmoid(gates[GH + h][...]), _sigmoid(gates[2 * GH + h][...]), _sigmoid(gr)
            r = lax.rsqrt(jnp.mean(gov * gov, axis=1, keepdims=True) + EPS)
            gn0 = gov * r
            gn = gn0 * wv
            silu = gr * sg
            dg2 = dmv * sb
            da_ref[:, sl] = dmv * sa
            dg_ref[:, D + h * DV:D + (h + 1) * DV] = (dmv * av * sa * (1.0 - sa)).astype(BF16)
            dg_ref[:, 2 * D + h * DV:2 * D + (h + 1) * DV] = (dg2 * gn * silu * (1.0 - sb)).astype(BF16)
            dg_ref[:, sl] = (dg2 * gn * (sg * (1.0 + gr * (1.0 - sg)))).astype(BF16)
            dgn = dg2 * silu
            dw = dw + jnp.sum(dgn * gn0, axis=0, keepdims=True)
            gg = dgn * wv
            dgo_ref[:, sl] = r * gg - gov * (r * r * r * jnp.mean(gg * gov, axis=1, keepdims=True))

        @pl.when(pl.program_id(0) == 0)
        def _():
            dw_ref[...] = dw

        @pl.when(pl.program_id(0) > 0)
        def _():
            dw_ref[...] += dw

    row, gates, vec = _merge_specs(tm)
    return pl.pallas_call(
        body, name=name,
        out_shape=(jax.ShapeDtypeStruct((Tn, D), F32), jax.ShapeDtypeStruct((Tn, D), F32), jax.ShapeDtypeStruct((Tn, 3 * D), BF16),
                   jax.ShapeDtypeStruct((1, DV), F32)),
        grid=(Tn // tm,), in_specs=[row, row, row] + gates + [vec],
        out_specs=(row, row, pl.BlockSpec((tm, 3 * D), lambda i: (i, 0)), vec), compiler_params=_cp("arbitrary"),
    )(dm, a, go, *([proj] * (3 * GH)), gnw)


def _ffn_up(v2, wg, wu, name="ffn_up", tm=1024):
    Tn = v2.shape[0]
    tm = min(tm, Tn)

    def body(v_ref, wg_ref, wu_ref, a_ref, b_ref, ff_ref):
        vv = v_ref[...]
        a = _dot(vv, wg_ref[...])
        b = _dot(vv, wu_ref[...])
        a_ref[...] = a
        b_ref[...] = b
        ff_ref[...] = (a * _sigmoid(a) * b).astype(BF16)

    w = pl.BlockSpec((None, D, FS), lambda k, i: (k, 0, 0))
    act = pl.BlockSpec((None, tm, FS), lambda k, i: (k, i, 0))
    return pl.pallas_call(
        body, name=name,
        out_shape=(jax.ShapeDtypeStruct((NDEV, Tn, FS), F32), jax.ShapeDtypeStruct((NDEV, Tn, FS), F32),
                   jax.ShapeDtypeStruct((NDEV, Tn, FS), BF16)),
        grid=(NDEV, Tn // tm), in_specs=[pl.BlockSpec((tm, D), lambda k, i: (i, 0)), w, w], out_specs=(act, act, act),
        compiler_params=_cp("parallel", "parallel"),
    )(v2, wg, wu)


def _ffn_down(ff, wd, h1, name="ffn_down", tm=1024, tn=1024):
    Tn = h1.shape[0]
    tm = min(tm, Tn)

    def body(f_ref, w_ref, r_ref, o_ref, acc):
        k = pl.program_id(2)
        p = _dot(f_ref[...], w_ref[...])

        @pl.when(k == 0)
        def _():
            acc[...] = p + r_ref[...]

        @pl.when(k > 0)
        def _():
            acc[...] += p

        @pl.when(k == NDEV - 1)
        def _():
            o_ref[...] = acc[...]

    o = pl.BlockSpec((tm, tn), lambda i, j, k: (i, j))
    return pl.pallas_call(
        body, name=name, out_shape=jax.ShapeDtypeStruct((Tn, D), F32), grid=(Tn // tm, D // tn, NDEV),
        in_specs=[pl.BlockSpec((None, tm, FS), lambda i, j, k: (k, i, 0)), pl.BlockSpec((None, FS, tn), lambda i, j, k: (k, 0, j)), o],
        out_specs=o, scratch_shapes=[pltpu.VMEM((tm, tn), F32)], compiler_params=_cp("parallel", "parallel", "arbitrary"),
    )(ff, wd, h1)


def _ffn_dact(dh2b, wd, a, b, name="ffn_dact", tm=1024):
    Tn = dh2b.shape[0]
    tm = min(tm, Tn)

    def body(d_ref, w_ref, a_ref, b_ref, da_ref, db_ref):
        dff = _dot(d_ref[...], w_ref[...], tb=True)
        av = a_ref[...]
        sg = _sigmoid(av)
        da_ref[...] = (dff * b_ref[...] * (sg * (1.0 + av * (1.0 - sg)))).astype(BF16)
        db_ref[...] = (dff * (av * sg)).astype(BF16)

    act = pl.BlockSpec((None, tm, FS), lambda k, i: (k, i, 0))
    return pl.pallas_call(
        body, name=name,
        out_shape=(jax.ShapeDtypeStruct((NDEV, Tn, FS), BF16), jax.ShapeDtypeStruct((NDEV, Tn, FS), BF16)),
        grid=(NDEV, Tn // tm),
        in_specs=[pl.BlockSpec((tm, D), lambda k, i: (i, 0)), pl.BlockSpec((None, FS, D), lambda k, i: (k, 0, 0)), act, act],
        out_specs=(act, act), compiler_params=_cp("parallel", "parallel"),
    )(dh2b, wd, a, b)


def _ffn_dwd(ff, dh2b, name="ffn_dwd", tn=1024):
    Tn = dh2b.shape[0]

    def body(f_ref, d_ref, o_ref):
        o_ref[...] = _dot(f_ref[...], d_ref[...], ta=True).astype(BF16)

    return pl.pallas_call(
        body, name=name, out_shape=jax.ShapeDtypeStruct((NDEV, FS, D), BF16), grid=(NDEV, D // tn),
        in_specs=[pl.BlockSpec((None, Tn, FS), lambda k, j: (k, 0, 0)), pl.BlockSpec((Tn, tn), lambda k, j: (0, j))],
        out_specs=pl.BlockSpec((None, FS, tn), lambda k, j: (k, 0, j)), compiler_params=_cp("parallel", "parallel"),
    )(ff, dh2b)


def _ffn_dwgu(v2, da, db, name="ffn_dwgu", tm=1024):
    Tn = v2.shape[0]

    def body(v_ref, da_ref, db_ref, og_ref, ou_ref):
        vv = v_ref[...]
        og_ref[...] = _dot(vv, da_ref[...], ta=True).astype(BF16)
        ou_ref[...] = _dot(vv, db_ref[...], ta=True).astype(BF16)

    act = pl.BlockSpec((None, Tn, FS), lambda k, i: (k, 0, 0))
    o = pl.BlockSpec((None, tm, FS), lambda k, i: (k, i, 0))
    return pl.pallas_call(
        body, name=name,
        out_shape=(jax.ShapeDtypeStruct((NDEV, D, FS), BF16), jax.ShapeDtypeStruct((NDEV, D, FS), BF16)),
        grid=(NDEV, D // tm), in_specs=[pl.BlockSpec((Tn, tm), lambda k, i: (0, i)), act, act], out_specs=(o, o),
        compiler_params=_cp("parallel", "parallel"),
    )(v2, da, db)


def _ffn_dv2(da, db, wg, wu, name="ffn_dv2", tm=1024, tn=1024):
    Tn = da.shape[1]
    tm = min(tm, Tn)

    def body(da_ref, db_ref, wg_ref, wu_ref, o_ref, acc):
        k = pl.program_id(2)
        p = _dot(da_ref[...], wg_ref[...], tb=True) + _dot(db_ref[...], wu_ref[...], tb=True)

        @pl.when(k == 0)
        def _():
            acc[...] = p

        @pl.when(k > 0)
        def _():
            acc[...] += p

        @pl.when(k == NDEV - 1)
        def _():
            o_ref[...] = acc[...]

    act = pl.BlockSpec((None, tm, FS), lambda i, j, k: (k, i, 0))
    w = pl.BlockSpec((None, tn, FS), lambda i, j, k: (k, j, 0))
    return pl.pallas_call(
        body, name=name, out_shape=jax.ShapeDtypeStruct((Tn, D), F32), grid=(Tn // tm, D // tn, NDEV),
        in_specs=[act, act, w, w], out_specs=pl.BlockSpec((tm, tn), lambda i, j, k: (i, j)),
        scratch_shapes=[pltpu.VMEM((tm, tn), F32)], compiler_params=_cp("parallel", "parallel", "arbitrary"),
    )(da, db, wg, wu)


def _adam_math(w, g, m, v):
    m2 = B1 * m + (1.0 - B1) * g
    v2 = B2 * v + (1.0 - B2) * (g * g)
    mh = m2 / (1.0 - B1 ** STEP)
    vh = v2 / (1.0 - B2 ** STEP)
    return -LR * (mh / (jnp.sqrt(vh) + AEPS) + WD * w), m2, v2


def _adamw(w, m, v, psums, parts, chip_idx, name, tr):
    R, C = w.shape

    def body(s_ref, w_ref, m_ref, v_ref, o_ref, p_ref, g_ref, d_ref, m2_ref, v2_ref):
        g = ((o_ref[...].astype(F32) + p_ref[0].astype(F32)) + p_ref[1].astype(F32)) + p_ref[2].astype(F32)
        d, m2, v2 = _adam_math(w_ref[...], g, m_ref[...], v_ref[...])
        g_ref[...] = g
        d_ref[...] = d
        m2_ref[...] = m2
        v2_ref[...] = v2

    blk = pl.BlockSpec((tr, C), lambda i, s: (i, 0))
    out = jax.ShapeDtypeStruct((R, C), F32)
    grid_spec = pltpu.PrefetchScalarGridSpec(
        num_scalar_prefetch=1, grid=(R // tr,),
        in_specs=[blk, blk, blk, pl.BlockSpec((None, tr, C), lambda i, s: (s[0], i, 0)), pl.BlockSpec((3, tr, C), lambda i, s: (0, i, 0))],
        out_specs=(blk, blk, blk, blk),
    )
    return pl.pallas_call(body, name=name, out_shape=(out, out, out, out), grid_spec=grid_spec, compiler_params=_cp("parallel"),
                          )(chip_idx, w, m, v, psums, parts)


def _adamw_rows(w, m, v, g, name, tr):
    R = w.shape[0]

    def body(w_ref, m_ref, v_ref, g_ref, d_ref, m2_ref, v2_ref):
        d, m2, v2 = _adam_math(w_ref[...], g_ref[...], m_ref[...], v_ref[...])
        d_ref[...] = d
        m2_ref[...] = m2
        v2_ref[...] = v2

    blk = pl.BlockSpec((tr,) + w.shape[1:], lambda i: (i, 0, 0))
    out = jax.ShapeDtypeStruct(w.shape, F32)
    return pl.pallas_call(body, name=name, out_shape=(out, out, out), grid=(R // tr,), in_specs=[blk] * 4, out_specs=(blk, blk, blk),
                          compiler_params=_cp("parallel"))(w, m, v, g)


def _sum_parts(psums, parts, chip_idx, name, tr, tc):
    _, R, C = psums.shape

    def body(s_ref, o_ref, p_ref, g_ref):
        g_ref[...] = ((o_ref[...].astype(F32) + p_ref[0].astype(F32)) + p_ref[1].astype(F32)) + p_ref[2].astype(F32)

    grid_spec = pltpu.PrefetchScalarGridSpec(
        num_scalar_prefetch=1, grid=(R // tr, C // tc),
        in_specs=[pl.BlockSpec((None, tr, tc), lambda i, j, s: (s[0], i, j)), pl.BlockSpec((3, tr, tc), lambda i, j, s: (0, i, j))],
        out_specs=pl.BlockSpec((tr, tc), lambda i, j, s: (i, j)),
    )
    return pl.pallas_call(body, name=name, out_shape=jax.ShapeDtypeStruct((R, C), F32), grid_spec=grid_spec,
                          compiler_params=_cp("parallel", "parallel"))(chip_idx, psums, parts)


def _adamw_plain(w, m, v, g, name):
    def body(w_ref, m_ref, v_ref, g_ref, d_ref, m2_ref, v2_ref):
        d, m2, v2 = _adam_math(w_ref[...], g_ref[...], m_ref[...], v_ref[...])
        d_ref[...] = d
        m2_ref[...] = m2
        v2_ref[...] = v2

    out = jax.ShapeDtypeStruct(w.shape, F32)
    return pl.pallas_call(body, name=name, out_shape=(out, out, out))(w, m, v, g)


def _sum_devices(pack_all, name="sum_small"):
    def body(p_ref, o_ref):
        s = p_ref[0]
        for k in range(1, NDEV):
            s = s + p_ref[k]
        o_ref[...] = s

    return pl.pallas_call(body, name=name, out_shape=jax.ShapeDtypeStruct(pack_all.shape[1:], F32))(pack_all)


def _pair_add(g5, recv, c_idx, name, tr):
    _, _, R, C = g5.shape

    def body(c_ref, g_ref, r_ref, o_ref):
        o_ref[...] = (g_ref[...].astype(F32) + r_ref[...].astype(F32)).astype(BF16)

    grid_spec = pltpu.PrefetchScalarGridSpec(
        num_scalar_prefetch=1, grid=(4, R // tr),
        in_specs=[pl.BlockSpec((None, None, tr, C), lambda q, i, c: (q, c[0], i, 0)), pl.BlockSpec((None, tr, C), lambda q, i, c: (q, i, 0))],
        out_specs=pl.BlockSpec((None, tr, C), lambda q, i, c: (q, i, 0)),
    )
    return pl.pallas_call(
        body, name=name, out_shape=jax.ShapeDtypeStruct((4, R, C), BF16), grid_spec=grid_spec,
        compiler_params=_cp("parallel", "parallel"),
    )(c_idx, g5, recv)


_ANY = pl.BlockSpec(memory_space=pl.ANY)


def _mesh_pos():
    x, y, c = lax.axis_index("x"), lax.axis_index("y"), lax.axis_index("c")
    return x, y, c, [(1 - x, y), (x, 1 - y), (1 - x, 1 - y)]


def _all_gather(shards, name="gather_weights"):
    n = len(shards)

    def body(*refs):
        ins, outs = refs[:n], refs[n:2 * n]
        send, recv, loc = refs[2 * n + 1:]
        x, y, c, chips = _mesh_pos()
        me, sib = (x, y, c), (x, y, 1 - c)

        def cp(a, k, block, to, own=False):
            dst = outs[a].at[4 * block[0] + 2 * block[1] + block[2]]
            return pltpu.make_async_remote_copy(src_ref=ins[a] if own else dst, dst_ref=dst, send_sem=send.at[a, k],
                                                recv_sem=recv.at[a, k], device_id=to, device_id_type=MESH)

        local = [pltpu.make_async_copy(ins[a], outs[a].at[4 * x + 2 * y + c], loc.at[a]) for a in range(n)]
        first = []
        for a in range(n):
            local[a].start()
            first.append(cp(a, 0, me, sib, own=True))
            first += [cp(a, 1 + j, me, (*chip, c), own=True) for j, chip in enumerate(chips)]
        for d in first:
            d.start()
        passed = []
        for a in range(n):
            for j, chip in enumerate(chips):
                cp(a, 1 + j, (*chip, c), me).wait_recv()
                d = cp(a, 4 + j, (*chip, c), sib)
                d.start()
                passed.append(d)
        for a in range(n):
            cp(a, 0, sib, me).wait_recv()
            for j, chip in enumerate(chips):
                cp(a, 4 + j, (*chip, 1 - c), me).wait_recv()
        for d in first + passed:
            d.wait_send()
        for d in local:
            d.wait()
        refs[2 * n][...] = jnp.zeros_like(refs[2 * n])

    return pl.pallas_call(
        body, name=name,
        out_shape=tuple(jax.ShapeDtypeStruct((NDEV,) + s.shape, s.dtype) for s in shards) + (jax.ShapeDtypeStruct((8, LANE), F32),),
        in_specs=[_ANY] * n, out_specs=tuple([_ANY] * n) + (pl.BlockSpec(memory_space=pltpu.VMEM),),
        scratch_shapes=[pltpu.SemaphoreType.DMA((n, 7)), pltpu.SemaphoreType.DMA((n, 7)), pltpu.SemaphoreType.DMA((n,))],
    )(*shards)


def _pair_exchange(grads, name):
    n = len(grads)

    def body(*refs):
        ins, outs = refs[:n], refs[n:2 * n]
        send, recv = refs[2 * n:]
        x, y, c, _ = _mesh_pos()
        big = [pltpu.make_async_remote_copy(src_ref=ins[a].at[:, 1 - c], dst_ref=outs[a], send_sem=send.at[a], recv_sem=recv.at[a],
                                            device_id=(x, y, 1 - c), device_id_type=MESH) for a in range(n)]
        for d in big:
            d.start()
        for d in big:
            d.wait_recv()
        for d in big:
            d.wait_send()

    return pl.pallas_call(
        body, name=name, out_shape=tuple(jax.ShapeDtypeStruct((4,) + g.shape[2:], g.dtype) for g in grads),
        in_specs=[_ANY] * n, out_specs=tuple([_ANY] * n),
        scratch_shapes=[pltpu.SemaphoreType.DMA((n,)), pltpu.SemaphoreType.DMA((n,))],
    )(*grads)


def _gather_small(pack, name="gather_small"):
    def body(pk, pk_all, psend, precv, loc):
        x, y, c, chips = _mesh_pos()
        me_slot = 4 * x + 2 * y + c
        sib = (x, y, 1 - c)
        own = pltpu.make_async_copy(pk, pk_all.at[me_slot], loc)
        own.start()
        peers = [sib] + [(*chip, c) for chip in chips] + [(*chip, 1 - c) for chip in chips]
        small = [pltpu.make_async_remote_copy(src_ref=pk, dst_ref=pk_all.at[me_slot], send_sem=psend.at[k], recv_sem=precv.at[k],
                                              device_id=p, device_id_type=MESH) for k, p in enumerate(peers)]
        for d in small:
            d.start()
        for k, p in enumerate(peers):
            pltpu.make_async_remote_copy(src_ref=pk, dst_ref=pk_all.at[4 * p[0] + 2 * p[1] + p[2]], send_sem=psend.at[k],
                                         recv_sem=precv.at[k], device_id=p, device_id_type=MESH).wait_recv()
        for d in small:
            d.wait_send()
        own.wait()

    return pl.pallas_call(
        body, name=name, out_shape=jax.ShapeDtypeStruct((NDEV,) + pack.shape, pack.dtype), in_specs=[_ANY], out_specs=_ANY,
        scratch_shapes=[pltpu.SemaphoreType.DMA((7,)), pltpu.SemaphoreType.DMA((7,)), pltpu.SemaphoreType.DMA(())],
    )(pack)


def _main_row(g):
    return g if g < C_LR else g - RANK


def _window_pieces(lo, hi):
    out = []
    for a, b, where in ((lo, min(hi, C_LR), "main"), (max(lo, C_LR), min(hi, C_LR + RANK), "lr"), (max(lo, C_LR + RANK), hi, "main")):
        if a < b:
            out.append((a, b, where, _main_row(a) if where == "main" else a - C_LR))
    return out


def _assemble_w_in(windows, name="assemble_w_in"):
    edges = NDEV - 1

    def body(b_ref, main_ref, lr_ref, e1, e2, sems, esems):
        copies = []
        for k in range(NDEV):
            lo = WSTEP * k + (16 if k else 0)
            hi = WSTEP * k + (WWIN if k == NDEV - 1 else WSTEP)
            for a, b, where, dst in _window_pieces(lo, hi):
                src = b_ref.at[k, pl.ds(a - WSTEP * k, b - a)]
                dref = main_ref.at[pl.ds(dst, b - a)] if where == "main" else lr_ref.at[pl.ds(dst, b - a)]
                copies.append(pltpu.make_async_copy(src, dref, sems.at[len(copies)]))
        ein = []
        for j in range(edges):
            ein.append(pltpu.make_async_copy(b_ref.at[j, pl.ds(WSTEP, 16)], e1.at[j], esems.at[2 * j]))
            ein.append(pltpu.make_async_copy(b_ref.at[j + 1, pl.ds(0, 16)], e2.at[j], esems.at[2 * j + 1]))
        for d in copies + ein:
            d.start()
        lr_ref[RANK:, :] = jnp.zeros((LANE - RANK, D), BF16)
        for d in ein:
            d.wait()
        e1[...] = e1[...] + e2[...]
        eout = [pltpu.make_async_copy(e1.at[j], main_ref.at[pl.ds(_main_row(WSTEP * (j + 1)), 16)], esems.at[2 * j]) for j in range(edges)]
        for d in eout:
            d.start()
        for d in copies + eout:
            d.wait()

    n_copies = 11
    return pl.pallas_call(
        body, name=name,
        out_shape=(jax.ShapeDtypeStruct((NMAIN, D), BF16), jax.ShapeDtypeStruct((LANE, D), BF16)),
        in_specs=[_ANY], out_specs=(_ANY, pl.BlockSpec(memory_space=pltpu.VMEM)),
        scratch_shapes=[pltpu.VMEM((edges, 16, D), BF16), pltpu.VMEM((edges, 16, D), BF16), pltpu.SemaphoreType.DMA((n_copies,)),
                        pltpu.SemaphoreType.DMA((2 * edges,))],
    )(windows)


def _disassemble_w_in(d_main, d_lr, name="disassemble_w_in"):
    def body(main_ref, lr_ref, g_ref, sems):
        copies = []
        for k in range(NDEV):
            for a, b, where, src0 in _window_pieces(WSTEP * k, WSTEP * k + WWIN):
                src = main_ref.at[pl.ds(src0, b - a)] if where == "main" else lr_ref.at[pl.ds(src0, b - a)]
                copies.append(pltpu.make_async_copy(src, g_ref.at[k, pl.ds(a - WSTEP * k, b - a)], sems.at[len(copies)]))
        for d in copies:
            d.start()
        for d in copies:
            d.wait()

    return pl.pallas_call(
        body, name=name, out_shape=jax.ShapeDtypeStruct((NDEV, WWIN, D), BF16), in_specs=[_ANY, _ANY], out_specs=_ANY,
        scratch_shapes=[pltpu.SemaphoreType.DMA((10,))],
    )(d_main, d_lr)


_HBM = pl.BlockSpec(memory_space=pltpu.HBM)
_SEM = pl.BlockSpec(memory_space=pltpu.SEMAPHORE)
_VMEM = pl.BlockSpec(memory_space=pltpu.VMEM)
_SIDE = pltpu.CompilerParams(has_side_effects=pltpu.SideEffectType.DATAFLOW_SIDE_EFFECTING)
_TOKEN = jax.ShapeDtypeStruct((8, LANE), F32)


def _hbm(a):
    return pltpu.with_memory_space_constraint(a, pltpu.HBM)


def _hbm_like(arrs):
    return tuple(pltpu.HBM(a.shape, a.dtype) for a in arrs)


def _tie(x, token):
    return x + token[0, 0].astype(x.dtype)


def _chip_copies(ins, lands, send, recv):
    x, y, c, chips = _mesh_pos()
    return [pltpu.make_async_remote_copy(src_ref=ins[a].at[2 * chip[0] + chip[1]], dst_ref=lands[a].at[j], send_sem=send.at[3 * a + j],
                                         recv_sem=recv.at[3 * a + j], device_id=(*chip, c), device_id_type=MESH)
            for a in range(len(ins)) for j, chip in enumerate(chips)]


def _chip_start(psums, name):
    n = len(psums)
    lands = [lax.empty((3,) + p.shape[1:], p.dtype) for p in psums]

    def body(*refs):
        for d in _chip_copies(refs[:n], refs[n:2 * n], refs[2 * n], refs[2 * n + 1]):
            d.start()
        refs[-1][...] = jnp.zeros_like(refs[-1])

    sems = pltpu.SemaphoreType.DMA((3 * n,))
    out = pl.pallas_call(
        body, name=name, out_shape=(sems, sems) + _hbm_like(psums) + _hbm_like(lands) + (_TOKEN,),
        in_specs=[_HBM] * (2 * n), out_specs=(_SEM, _SEM) + (_HBM,) * (2 * n) + (_VMEM,),
        input_output_aliases={i: 2 + i for i in range(2 * n)}, compiler_params=_SIDE,
    )(*[_hbm(a) for a in list(psums) + lands])
    return out[0], out[1], list(out[2:2 + n]), list(out[2 + n:2 + 2 * n]), out[-1]


def _chip_wait(send, recv, psums, lands, after, name):
    n = len(psums)

    def body(*refs):
        for d in _chip_copies(refs[:n], refs[n:2 * n], refs[2 * n], refs[2 * n + 1]):
            d.wait_send()
            d.wait_recv()

    out = pl.pallas_call(
        body, name=name, out_shape=_hbm_like(psums) + _hbm_like(lands),
        in_specs=[_HBM] * (2 * n) + [_SEM, _SEM, _ANY], out_specs=(_HBM,) * (2 * n),
        input_output_aliases={i: i for i in range(2 * n)}, compiler_params=_SIDE,
    )(*psums, *lands, send, recv, after)
    return list(out[:n]), list(out[n:])


def _slot(chip, c):
    return 4 * chip[0] + 2 * chip[1] + c


def _gather_start(shards, dev, name):
    n = len(shards)
    lands = [lax.dynamic_update_slice(lax.empty((NDEV,) + s.shape, s.dtype), s[None], (dev,) + (0,) * s.ndim) for s in shards]

    def body(*refs):
        src, land, send, recv = refs[:n], refs[n:2 * n], refs[2 * n], refs[2 * n + 1]
        x, y, c, chips = _mesh_pos()
        for a in range(n):
            for k, to in enumerate([(x, y, 1 - c)] + [(*chip, c) for chip in chips]):
                pltpu.make_async_remote_copy(src_ref=src[a], dst_ref=land[a].at[_slot((x, y), c)], send_sem=send.at[4 * a + k],
                                             recv_sem=recv.at[4 * a + k], device_id=to, device_id_type=MESH).start()
        refs[-1][...] = jnp.zeros_like(refs[-1])

    sems = pltpu.SemaphoreType.DMA((4 * n,))
    out = pl.pallas_call(
        body, name=name, out_shape=(sems, sems) + _hbm_like(shards) + _hbm_like(lands) + (_TOKEN,),
        in_specs=[_HBM] * (2 * n), out_specs=(_SEM, _SEM) + (_HBM,) * (2 * n) + (_VMEM,),
        input_output_aliases={i: 2 + i for i in range(2 * n)}, compiler_params=_SIDE,
    )(*[_hbm(a) for a in list(shards) + lands])
    return out[0], out[1], list(out[2:2 + n]), list(out[2 + n:2 + 2 * n]), out[-1]


def _gather_pass(lands, recv, after, name):
    n = len(lands)

    def body(*refs):
        land, recv1 = refs[:n], refs[n]
        send2, recv2 = refs[n + 2], refs[n + 3]
        x, y, c, chips = _mesh_pos()
        for a in range(n):
            for j, chip in enumerate(chips):
                blk = land[a].at[_slot(chip, c)]
                pltpu.make_async_remote_copy(src_ref=blk, dst_ref=blk, send_sem=send2.at[3 * a + j], recv_sem=recv1.at[4 * a + 1 + j],
                                             device_id=(*chip, c), device_id_type=MESH).wait_recv()
                pltpu.make_async_remote_copy(src_ref=blk, dst_ref=blk, send_sem=send2.at[3 * a + j], recv_sem=recv2.at[3 * a + j],
                                             device_id=(x, y, 1 - c), device_id_type=MESH).start()
        refs[-1][...] = jnp.zeros_like(refs[-1])

    sems = pltpu.SemaphoreType.DMA((3 * n,))
    out = pl.pallas_call(
        body, name=name, out_shape=(sems, sems) + _hbm_like(lands) + (_TOKEN,),
        in_specs=[_HBM] * n + [_SEM, _ANY], out_specs=(_SEM, _SEM) + (_HBM,) * n + (_VMEM,),
        input_output_aliases={i: 2 + i for i in range(n)}, compiler_params=_SIDE,
    )(*lands, recv, after)
    return out[0], out[1], list(out[2:2 + n]), out[-1]


def _gather_wait(shards, lands, send, recv, send2, recv2, after, name):
    n = len(lands)

    def body(*refs):
        src, land = refs[:n], refs[n:2 * n]
        send1, recv1, snd2, rcv2 = refs[2 * n:2 * n + 4]
        x, y, c, chips = _mesh_pos()
        sib = (x, y, 1 - c)
        for a in range(n):
            for k in range(4):
                pltpu.make_async_remote_copy(src_ref=src[a], dst_ref=land[a].at[_slot((x, y), c)], send_sem=send1.at[4 * a + k],
                                             recv_sem=recv1.at[4 * a + k], device_id=sib, device_id_type=MESH).wait_send()
            blk = land[a].at[_slot((x, y), 1 - c)]
            pltpu.make_async_remote_copy(src_ref=blk, dst_ref=blk, send_sem=send1.at[4 * a], recv_sem=recv1.at[4 * a],
                                         device_id=sib, device_id_type=MESH).wait_recv()
            for j, chip in enumerate(chips):
                mine, theirs = land[a].at[_slot(chip, c)], land[a].at[_slot(chip, 1 - c)]
                pltpu.make_async_remote_copy(src_ref=mine, dst_ref=mine, send_sem=snd2.at[3 * a + j], recv_sem=rcv2.at[3 * a + j],
                                             device_id=sib, device_id_type=MESH).wait_send()
                pltpu.make_async_remote_copy(src_ref=theirs, dst_ref=theirs, send_sem=snd2.at[3 * a + j], recv_sem=rcv2.at[3 * a + j],
                                             device_id=sib, device_id_type=MESH).wait_recv()

    out = pl.pallas_call(
        body, name=name, out_shape=_hbm_like(shards) + _hbm_like(lands),
        in_specs=[_HBM] * (2 * n) + [_SEM] * 4 + [_ANY], out_specs=(_HBM,) * (2 * n),
        input_output_aliases={i: i for i in range(2 * n)}, compiler_params=_SIDE,
    )(*shards, *lands, send, recv, send2, recv2, after)
    return list(out[n:])


def _pad_to(v, n):
    return jnp.pad(v, [(0, 0)] * (v.ndim - 1) + [(0, n - v.shape[-1])])


def _pack_small(n1, gb, sk, gn, n2, fn, extra=None):
    parts = [n1.reshape(-1), gb.reshape(-1), sk.reshape(-1), gn.reshape(-1), n2.reshape(-1), fn.reshape(-1)]
    flat = jnp.concatenate(parts + ([extra.reshape(-1)] if extra is not None else []))
    return _pad_to(flat, SMALL_N).reshape(SMALL_ROWS, LANE)


def _unpack_small(p):
    f = p.reshape(-1)
    return (f[S_N1:S_GB].reshape(1, D), f[S_GB:S_SK].reshape(1, GH * DK), f[S_SK:S_GN].reshape(1, NQ), f[S_GN:S_N2].reshape(1, DV),
            f[S_N2:S_FN].reshape(1, D), f[S_FN:S_LOSS].reshape(D))


class _NoComm:
    def __init__(self, wo, wg_all, wu_all, wd_all):
        self.rest = (wo, wg_all, wu_all, wd_all)

    def mixed(self, gla_o, gla_norm_w):
        return gla_norm_w

    def rest_weights(self, merged):
        return self.rest

    def ffn_grads(self, d_wg, d_wu, d_wd, norm2_w):
        self.ffn = (d_wg, d_wu, d_wd)
        return norm2_w

    def in_grads(self, d_wmain, d_wlr, d_wo, w_lr):
        self.inw = (d_wmain, d_wlr, d_wo)
        return w_lr


class _Comm:
    def __init__(self, rest_shards, dev, c_idx):
        self.c_idx = c_idx
        self.send, self.recv, self.shards, self.lands, self.token = _gather_start(rest_shards, dev, "gather_rest_start")

    def mixed(self, gla_o, gla_norm_w):
        self.send2, self.recv2, self.lands, token = _gather_pass(self.lands, self.recv, gla_o, "gather_rest_pass")
        return _tie(gla_norm_w, token)

    def rest_weights(self, merged):
        wo_all, wg_all, wu_all, wd_all = _gather_wait(self.shards, self.lands, self.send, self.recv, self.send2, self.recv2,
                                                      merged, "gather_rest_wait")
        return wo_all.reshape(D, D), wg_all, wu_all, wd_all

    def _reduce(self, tag, names, grads, rows):
        recv1 = _pair_exchange(grads, "reduce_pair_" + tag)
        psums = [_pair_add(g, r, self.c_idx, "pair_add_" + nm, tr) for g, r, nm, tr in zip(grads, recv1, names, rows)]
        *flight, token = _chip_start(psums, "reduce_chips_start_" + tag)
        return dict(tag=tag, names=names, rows=rows, flight=flight), token

    def ffn_grads(self, d_wg, d_wu, d_wd, norm2_w):
        self.ffn, token = self._reduce("ffn", ["w_ffn_gate", "w_ffn_up", "w_ffn_down"],
                                       [d_wg.reshape(4, 2, D, FS), d_wu.reshape(4, 2, D, FS), d_wd.reshape(4, 2, FS, D)], [512, 512, 176])
        return _tie(norm2_w, token)

    def in_grads(self, d_wmain, d_wlr, d_wo, w_lr):
        d_win = _disassemble_w_in(d_wmain, d_wlr).reshape(4, 2, WWIN, D)
        self.inw, token = self._reduce("in", ["w_in", "w_out"], [d_win, d_wo.reshape(4, 2, D // NDEV, D)], [808, 256])
        return _tie(w_lr, token)


def _local_step(xs, tgt, norm1_w, gla_gate_b, attn_sinks, gla_norm_w, norm2_w, fnw, w_main, w_lr, w2p, comm):
    u = _rmsnorm_fwd(xs, norm1_w, "norm1_fwd")
    proj = _mm(u, w_main, tb=True, tm=1024, tn=640, tk=D, name="in_proj")
    plr = _mm(u, w_lr, tb=True, tm=1024, tn=LANE, tk=D, name="in_proj_lr")
    attn_o = _attn_fwd(proj, attn_sinks)
    gla_o, states = _gla_fwd(proj, plr, w2p, gla_gate_b)
    merged = _merge_fwd(attn_o, gla_o, proj, comm.mixed(gla_o, gla_norm_w))
    wo, wg_all, wu_all, wd_all = comm.rest_weights(merged)
    h1 = _mm(merged, wo, tm=1024, tn=512, tk=D, res=xs, name="out_proj")
    v2 = _rmsnorm_fwd(h1, norm2_w, "norm2_fwd")
    fa, fb, ff = _ffn_up(v2, wg_all, wu_all)
    h2 = _ffn_down(ff, wd_all, h1)
    dh2, dh2b, d_fnw, loss_part = _loss_head(h2, fnw, tgt)

    da, db = _ffn_dact(dh2b, wd_all, fa, fb)
    d_wd = _ffn_dwd(ff, dh2b)
    d_wg, d_wu = _ffn_dwgu(v2, da, db)
    norm2_w = comm.ffn_grads(d_wg, d_wu, d_wd, norm2_w)
    dv2 = _ffn_dv2(da, db, wg_all, wu_all)
    dh1, dh1b, d_n2 = _rmsnorm_bwd(dv2, h1, norm2_w, dh2, "norm2_bwd")
    dmerged = _mm(dh1b, wo, tb=True, tm=1024, tn=512, tk=D, name="out_proj_dx")
    d_wo = _mm(merged, dh1b, ta=True, tm=1024, tn=512, tk=xs.shape[0], out_dtype=BF16, name="out_proj_dw")
    d_attn, d_gla, d_gates, d_gnw = _merge_bwd(dmerged, attn_o, gla_o, proj, gla_norm_w)
    d_q, d_kv, d_sinks = _attn_bwd(proj, attn_sinks, attn_o, d_attn)
    d_gqk, d_gv, d_plr, d_w2p, d_gb = _gla_bwd(proj, plr, w2p, gla_gate_b, states, d_gla)
    dproj = jnp.concatenate([d_q, d_kv, d_gqk, d_gv, d_gates], axis=1)
    d_wmain = _mm(dproj, u, ta=True, tm=640, tn=1024, tk=xs.shape[0], out_dtype=BF16, name="in_proj_dw")
    d_wlr = _mm(d_plr, u, ta=True, tm=LANE, tn=1024, tk=xs.shape[0], out_dtype=BF16, name="in_proj_lr_dw")
    du_lr = _mm(d_plr, comm.in_grads(d_wmain, d_wlr, d_wo, w_lr), tm=1024, tn=1024, tk=LANE, name="in_proj_lr_dx")
    du = _mm(dproj, w_main, tm=1024, tn=1024, tk=1280, res=du_lr, name="in_proj_dx")
    dx, _, d_n1 = _rmsnorm_bwd(du, xs, norm1_w, dh1, "norm1_bwd")
    return dx, loss_part, d_w2p, d_gb, d_sinks, d_gnw, d_n1, d_n2, d_fnw


def kernel(x, norm1_w, w_in, gla_gate_w2, gla_gate_b, attn_sinks, gla_norm_w, w_out, norm2_w, w_ffn_gate, w_ffn_up, w_ffn_down, final_norm_w, loss_target, m_norm1_w, m_w_in, m_gla_gate_w2, m_gla_gate_b, m_attn_sinks, m_gla_norm_w, m_w_out, m_norm2_w, m_w_ffn_gate, m_w_ffn_up, m_w_ffn_down, m_final_norm_w, v_norm1_w, v_w_in, v_gla_gate_w2, v_gla_gate_b, v_attn_sinks, v_gla_norm_w, v_w_out, v_norm2_w, v_w_ffn_gate, v_w_ffn_up, v_w_ffn_down, v_final_norm_w):
    xs, tgt = x[0], loss_target[0]
    fnw = final_norm_w.reshape(1, D)
    c_idx = lax.axis_index("c").astype(jnp.int32).reshape(1)
    dev = 4 * lax.axis_index("x") + 2 * lax.axis_index("y") + lax.axis_index("c")

    chip_idx = (2 * lax.axis_index("x") + lax.axis_index("y")).astype(jnp.int32).reshape(1)

    shift = (WS - WSTEP) * dev
    window = lax.dynamic_update_slice(jnp.zeros((WWIN, D), BF16), jnp.transpose(w_in[0]).astype(BF16), (shift, 0))
    win_all, w2_all, tok = _all_gather([window, gla_gate_w2[0]], name="gather_w_in")
    rest = [(w[0] + tok[0, 0]).astype(BF16) for w in (w_out, w_ffn_gate, w_ffn_up, w_ffn_down)]
    comm = _Comm(rest, dev, c_idx)
    w_main, w_lr = _assemble_w_in(win_all)
    w2p =jnp.pad(jnp.transpose(w2_all, (1, 0, 2)).reshape(RANK, GH * DK), ((0, LANE - RANK), (0, 0)))

    dx, loss_part, d_w2p, d_gb, d_sinks, d_gnw, d_n1, d_n2, d_fnw = _local_step(
        xs, tgt, _tie(norm1_w, comm.token), gla_gate_b, attn_sinks, gla_norm_w, norm2_w, fnw, w_main, w_lr, w2p, comm)

    pack = jnp.concatenate([_pack_small(d_n1, d_gb, d_sinks, d_gnw, d_n2, d_fnw, loss_part),
                            d_w2p[:RANK].reshape(GW2_ROWS, LANE)], axis=0)
    small = _sum_devices(_gather_small(pack))

    big = {}
    after = dx
    for grp in (comm.ffn, comm.inw):
        psums, parts = _chip_wait(*grp["flight"], after, "reduce_chips_wait_" + grp["tag"])
        for nm, ps, pt, tr in zip(grp["names"], psums, parts, grp["rows"]):
            w, m, v = {"w_in": (w_in, m_w_in, v_w_in), "w_out": (w_out, m_w_out, v_w_out), "w_ffn_gate": (w_ffn_gate, m_w_ffn_gate, v_w_ffn_gate),
                       "w_ffn_up": (w_ffn_up, m_w_ffn_up, v_w_ffn_up), "w_ffn_down": (w_ffn_down, m_w_ffn_down, v_w_ffn_down)}[nm]
            if nm == "w_in":
                g_win = _sum_parts(ps, pt, chip_idx, "sum_w_in", tr, 1024)
                rows3 = lambda t: jnp.transpose(t[0]).reshape(WS, D // LANE, LANE)
                g3 = lax.dynamic_slice(g_win, (shift, 0), (WS, D)).reshape(WS, D // LANE, LANE)
                out3 = (g3,) + tuple(_adamw_rows(rows3(w), rows3(m), rows3(v), g3, "adamw_w_in", 178))
                big[nm] = [jnp.transpose(t.reshape(WS, D))[None] for t in out3]
            else:
                big[nm] = [t[None] for t in _adamw(w[0], m[0], v[0], ps, pt, chip_idx, "adamw_" + nm, tr)]
            after = big[nm][0]
    g_small = small[:SMALL_ROWS]
    sm = _adamw_plain(_pack_small(norm1_w, gla_gate_b, attn_sinks, gla_norm_w, norm2_w, final_norm_w),
                      _pack_small(m_norm1_w, m_gla_gate_b, m_attn_sinks, m_gla_norm_w, m_norm2_w, m_final_norm_w),
                      _pack_small(v_norm1_w, v_gla_gate_b, v_attn_sinks, v_gla_norm_w, v_norm2_w, v_final_norm_w), g_small, "adamw_small")
    g_w2 = lax.dynamic_slice_in_dim(small[SMALL_ROWS:].reshape(RANK, GH * DK), dev * LANE, LANE, axis=1)
    w2 = [g_w2[None]] + [t[None] for t in _adamw_plain(gla_gate_w2[0], m_gla_gate_w2[0], v_gla_gate_w2[0], g_w2, "adamw_w2")]
    loss = g_small.reshape(-1)[S_LOSS]

    sg, sd, sm2, sv2 = [_unpack_small(t) for t in (g_small,) + tuple(sm)]

    def group(i, s):
        return (s[0], big["w_in"][i], w2[i], s[1], s[2], s[3], big["w_out"][i], s[4], big["w_ffn_gate"][i], big["w_ffn_up"][i],
                big["w_ffn_down"][i], s[5])

    return (loss, dx[None], *group(0, sg), *group(1, sd), *group(2, sm2), *group(3, sv2))
```

```python
import functools

import jax
import jax.numpy as jnp
from jax import lax
from jax.experimental import pallas as pl
from jax.experimental.pallas import tpu as pltpu

F32, BF16 = jnp.float32, jnp.bfloat16
HIGHEST = lax.Precision.HIGHEST

D = 2048
HD, NQ, NKV, GRP, WIN = 64, 32, 4, 8, 128
GH, DK, DV, RANK, GC = 4, 256, 512, 16, 64
FH, NDEV = 5632, 8
FS = FH // NDEV
DIN = 12816
WS = DIN // NDEV
EPS = 1e-6
MASKV = -1e30
LANE = 128

C_AQ, C_AK, C_AV, C_GQ, C_GK, C_GV, C_GR, C_GA, C_GB, NMAIN = 0, 2048, 2304, 2560, 3584, 4608, 6656, 8704, 10752, 12800
C_LR = 6656
WSTEP, WWIN = 1600, 1616

LR, B1, B2, AEPS, WD, STEP = 0.001, 0.9, 0.999, 1e-08, 0.01, 10

S_N1, S_GB, S_SK, S_GN, S_N2, S_FN, S_LOSS, SMALL_N = 0, 2048, 3072, 3104, 3616, 5664, 7712, 8192
SMALL_ROWS = SMALL_N // LANE
GW2_ROWS = RANK * GH * DK // LANE
PACK_ROWS = SMALL_ROWS + GW2_ROWS

MESH = pl.DeviceIdType.MESH


def _dot(a, b, ta=False, tb=False, prec=None):
    dn = (((0,) if ta else (1,), (1,) if tb else (0,)), ((), ()))
    return lax.dot_general(a, b, dn, preferred_element_type=F32, precision=prec)


def _sigmoid(x):
    return 1.0 / (1.0 + jnp.exp(-x))


VMEM_LIMIT = 56 * 1024 * 1024


def _cp(*sem):
    return pltpu.CompilerParams(dimension_semantics=sem, vmem_limit_bytes=VMEM_LIMIT)


def _mm(a, b, *, ta=False, tb=False, tm, tn, tk, out_dtype=F32, res=None, name):
    M, K = (a.shape[1], a.shape[0]) if ta else a.shape
    N = b.shape[0] if tb else b.shape[1]
    tm, tn, tk = min(tm, M), min(tn, N), min(tk, K)
    nk = K // tk
    assert M % tm == 0 and N % tn == 0 and K % tk == 0
    a_spec = pl.BlockSpec((tk, tm), lambda i, j, k: (k, i)) if ta else pl.BlockSpec((tm, tk), lambda i, j, k: (i, k))
    b_spec = pl.BlockSpec((tn, tk), lambda i, j, k: (j, k)) if tb else pl.BlockSpec((tk, tn), lambda i, j, k: (k, j))
    o_spec = pl.BlockSpec((tm, tn), lambda i, j, k: (i, j))
    has_res = res is not None

    def body(*refs):
        a_ref, b_ref = refs[0], refs[1]
        r_ref = refs[2] if has_res else None
        o_ref = refs[3] if has_res else refs[2]
        p = _dot(a_ref[...].astype(BF16), b_ref[...].astype(BF16), ta, tb)
        if nk == 1:
            if has_res:
                p = p + r_ref[...]
            o_ref[...] = p.astype(out_dtype)
        else:
            acc = refs[-1]
            k = pl.program_id(2)

            @pl.when(k == 0)
            def _():
                acc[...] = (p + r_ref[...]) if has_res else p

            @pl.when(k > 0)
            def _():
                acc[...] += p

            @pl.when(k == nk - 1)
            def _():
                o_ref[...] = acc[...].astype(out_dtype)

    return pl.pallas_call(
        body, name=name,
        out_shape=jax.ShapeDtypeStruct((M, N), out_dtype),
        grid=(M // tm, N // tn, nk),
        in_specs=[a_spec, b_spec] + ([o_spec] if has_res else []),
        out_specs=o_spec,
        scratch_shapes=[pltpu.VMEM((tm, tn), F32)] if nk > 1 else [],
        compiler_params=_cp("parallel", "parallel", "arbitrary"),
    )(*((a, b, res) if has_res else (a, b)))


def _rmsnorm_fwd(x, w, name, tm=256):
    Tn = x.shape[0]

    def body(x_ref, w_ref, o_ref):
        xv = x_ref[...]
        r = lax.rsqrt(jnp.mean(xv * xv, axis=1, keepdims=True) + EPS)
        o_ref[...] = (xv * r * w_ref[...]).astype(BF16)

    return pl.pallas_call(
        body, name=name, out_shape=jax.ShapeDtypeStruct((Tn, D), BF16), grid=(Tn // tm,),
        in_specs=[pl.BlockSpec((tm, D), lambda i: (i, 0)), pl.BlockSpec((1, D), lambda i: (0, 0))],
        out_specs=pl.BlockSpec((tm, D), lambda i: (i, 0)), compiler_params=_cp("parallel"),
    )(x, w)


def _rmsnorm_bwd(dy, h, w, res, name, tm=256):
    Tn = h.shape[0]

    def body(dy_ref, h_ref, w_ref, res_ref, dh_ref, dhb_ref, dw_ref):
        hv, dyv = h_ref[...], dy_ref[...]
        r = lax.rsqrt(jnp.mean(hv * hv, axis=1, keepdims=True) + EPS)
        g = dyv * w_ref[...]
        dh = res_ref[...] + r * g - hv * (r * r * r * jnp.mean(g * hv, axis=1, keepdims=True))
        dh_ref[...] = dh
        dhb_ref[...] = dh.astype(BF16)
        part = jnp.sum(dyv * hv * r, axis=0, keepdims=True)

        @pl.when(pl.program_id(0) == 0)
        def _():
            dw_ref[...] = part

        @pl.when(pl.program_id(0) > 0)
        def _():
            dw_ref[...] += part

    row = pl.BlockSpec((tm, D), lambda i: (i, 0))
    vec = pl.BlockSpec((1, D), lambda i: (0, 0))
    return pl.pallas_call(
        body, name=name,
        out_shape=(jax.ShapeDtypeStruct((Tn, D), F32), jax.ShapeDtypeStruct((Tn, D), BF16), jax.ShapeDtypeStruct((1, D), F32)),
        grid=(Tn // tm,), in_specs=[row, row, vec, row], out_specs=(row, row, vec), compiler_params=_cp("arbitrary"),
    )(dy, h, w, res)


def _loss_head(h2, wf, tgt, name="loss_head", tm=256):
    Tn = h2.shape[0]

    def body(h_ref, w_ref, t_ref, dh_ref, dhb_ref, dw_ref, loss_ref):
        hv, wv = h_ref[...], w_ref[...]
        r = lax.rsqrt(jnp.mean(hv * hv, axis=1, keepdims=True) + EPS)
        hn = hv * r
        e = hn * wv - t_ref[...]
        dy = e * (1.0 / D)
        g = dy * wv
        dh = r * g - hv * (r * r * r * jnp.mean(g * hv, axis=1, keepdims=True))
        dh_ref[...] = dh
        dhb_ref[...] = dh.astype(BF16)
        part = jnp.sum(dy * hn, axis=0, keepdims=True)
        lpart = (0.5 / D) * jnp.sum(jnp.sum(e * e, axis=1, keepdims=True), axis=0, keepdims=True)

        @pl.when(pl.program_id(0) == 0)
        def _():
            dw_ref[...] = part
            loss_ref[...] = lpart

        @pl.when(pl.program_id(0) > 0)
        def _():
            dw_ref[...] += part
            loss_ref[...] += lpart

    row = pl.BlockSpec((tm, D), lambda i: (i, 0))
    vec = pl.BlockSpec((1, D), lambda i: (0, 0))
    one = pl.BlockSpec((1, 1), lambda i: (0, 0))
    return pl.pallas_call(
        body, name=name,
        out_shape=(jax.ShapeDtypeStruct((Tn, D), F32), jax.ShapeDtypeStruct((Tn, D), BF16), jax.ShapeDtypeStruct((1, D), F32),
                   jax.ShapeDtypeStruct((1, 1), F32)),
        grid=(Tn // tm,), in_specs=[row, vec, row], out_specs=(row, row, vec, one), compiler_params=_cp("arbitrary"),
    )(h2, wf, tgt)


def _attn_mask(n):
    qi = lax.broadcasted_iota(jnp.int32, (GRP * WIN, 2 * WIN), 0) % WIN
    ki = lax.broadcasted_iota(jnp.int32, (GRP * WIN, 2 * WIN), 1)
    rel = qi + WIN - ki
    return (rel >= 0) & (rel < WIN) & ((n > 0) | (ki >= WIN))


def _attn_probs(q_ref, kc_ref, kp_ref, sink_ref, h, mask):
    kk = jnp.concatenate([kp_ref[:, h * HD:(h + 1) * HD], kc_ref[:, h * HD:(h + 1) * HD]], axis=0).astype(BF16)
    qs = jnp.concatenate([q_ref[:, (h * GRP + g) * HD:(h * GRP + g + 1) * HD] for g in range(GRP)], axis=0).astype(BF16)
    s = _dot(qs, kk, tb=True) * (HD ** -0.5)
    s = jnp.where(mask, s, MASKV)
    sink = jnp.concatenate([jnp.full((WIN, 1), sink_ref[0, h * GRP + g], F32) for g in range(GRP)], axis=0)
    m = jnp.maximum(jnp.max(s, axis=1, keepdims=True), sink)
    e = jnp.exp(s - m)
    es = jnp.exp(sink - m)
    inv = 1.0 / (jnp.sum(e, axis=1, keepdims=True) + es)
    return e * inv, es * inv, qs, kk


def _attn_specs(nb, last):
    cur = lambda n: jnp.minimum(n, last)
    prev = lambda n: jnp.maximum(jnp.minimum(n, last) - 1, 0)
    return [
        pl.BlockSpec((WIN, NQ * HD), lambda n: (cur(n), C_AQ // (NQ * HD))),
        pl.BlockSpec((WIN, NKV * HD), lambda n: (cur(n), C_AK // (NKV * HD))),
        pl.BlockSpec((WIN, NKV * HD), lambda n: (prev(n), C_AK // (NKV * HD))),
        pl.BlockSpec((WIN, NKV * HD), lambda n: (cur(n), C_AV // (NKV * HD))),
        pl.BlockSpec((WIN, NKV * HD), lambda n: (prev(n), C_AV // (NKV * HD))),
    ]


def _attn_fwd(proj, sinks, name="attn_fwd"):
    Tn = proj.shape[0]
    nb = Tn // WIN

    def body(q_ref, kc_ref, kp_ref, vc_ref, vp_ref, sink_ref, o_ref):
        mask = _attn_mask(pl.program_id(0))
        for h in range(NKV):
            p, _, _, _ = _attn_probs(q_ref, kc_ref, kp_ref, sink_ref, h, mask)
            vv = jnp.concatenate([vp_ref[:, h * HD:(h + 1) * HD], vc_ref[:, h * HD:(h + 1) * HD]], axis=0).astype(BF16)
            o = _dot(p.astype(BF16), vv)
            for g in range(GRP):
                o_ref[:, (h * GRP + g) * HD:(h * GRP + g + 1) * HD] = o[g * WIN:(g + 1) * WIN, :]

    return pl.pallas_call(
        body, name=name, out_shape=jax.ShapeDtypeStruct((Tn, D), F32), grid=(nb,),
        in_specs=_attn_specs(nb, nb - 1) + [pl.BlockSpec(memory_space=pltpu.SMEM)],
        out_specs=pl.BlockSpec((WIN, D), lambda n: (n, 0)), compiler_params=_cp("parallel"),
    )(proj, proj, proj, proj, proj, sinks)


def _attn_bwd(proj, sinks, o, do, name="attn_bwd"):
    Tn = proj.shape[0]
    nb = Tn // WIN
    KW = NKV * HD

    def body(q_ref, kc_ref, kp_ref, vc_ref, vp_ref, o_ref, do_ref, sink_ref, dq_ref, dkv_ref, dsk_ref, carry, cur):
        n = pl.program_id(0)

        @pl.when(n == 0)
        def _():
            carry[...] = jnp.zeros_like(carry)
            dsk_ref[...] = jnp.zeros_like(dsk_ref)

        @pl.when(n < nb)
        def _():
            mask = _attn_mask(n)
            for h in range(NKV):
                p, ps, qs, kk = _attn_probs(q_ref, kc_ref, kp_ref, sink_ref, h, mask)
                vv = jnp.concatenate([vp_ref[:, h * HD:(h + 1) * HD], vc_ref[:, h * HD:(h + 1) * HD]], axis=0).astype(BF16)
                cols = [slice((h * GRP + g) * HD, (h * GRP + g + 1) * HD) for g in range(GRP)]
                dos = jnp.concatenate([do_ref[:, c] for c in cols], axis=0)
                os_ = jnp.concatenate([o_ref[:, c] for c in cols], axis=0)
                delta = jnp.sum(dos * os_, axis=1, keepdims=True)
                dosb = dos.astype(BF16)
                dp = _dot(dosb, vv, tb=True)
                ds = (p * (dp - delta) * (HD ** -0.5)).astype(BF16)
                dq = _dot(ds, kk)
                dkk = _dot(ds, qs, ta=True)
                dvv = _dot(p.astype(BF16), dosb, ta=True)
                dsk = ps * delta
                for g in range(GRP):
                    dq_ref[:, cols[g]] = dq[g * WIN:(g + 1) * WIN, :].astype(BF16)
                    i = h * GRP + g
                    dsk_ref[:, i:i + 1] -= jnp.sum(dsk[g * WIN:(g + 1) * WIN, :], axis=0, keepdims=True)
                dkv_ref[:, h * HD:(h + 1) * HD] = (carry[:, h * HD:(h + 1) * HD] + dkk[:WIN, :]).astype(BF16)
                dkv_ref[:, KW + h * HD:KW + (h + 1) * HD] = (carry[:, KW + h * HD:KW + (h + 1) * HD] + dvv[:WIN, :]).astype(BF16)
                cur[:, h * HD:(h + 1) * HD] = dkk[WIN:, :]
                cur[:, KW + h * HD:KW + (h + 1) * HD] = dvv[WIN:, :]
            carry[...] = cur[...]

        @pl.when(n == nb)
        def _():
            dkv_ref[...] = carry[...].astype(BF16)

    last = nb - 1
    row = pl.BlockSpec((WIN, D), lambda n: (jnp.minimum(n, last), 0))
    return pl.pallas_call(
        body, name=name,
        out_shape=(jax.ShapeDtypeStruct((Tn, D), BF16), jax.ShapeDtypeStruct((Tn, 2 * KW), BF16), jax.ShapeDtypeStruct((1, NQ), F32)),
        grid=(nb + 1,),
        in_specs=_attn_specs(nb, last) + [row, row, pl.BlockSpec(memory_space=pltpu.SMEM)],
        out_specs=(row, pl.BlockSpec((WIN, 2 * KW), lambda n: (jnp.maximum(n - 1, 0), 0)), pl.BlockSpec((1, NQ), lambda n: (0, 0))),
        scratch_shapes=[pltpu.VMEM((WIN, 2 * KW), F32), pltpu.VMEM((WIN, 2 * KW), F32)],
        compiler_params=_cp("arbitrary"),
    )(proj, proj, proj, proj, proj, o, do, sinks)


def _tri(lower):
    r = lax.broadcasted_iota(jnp.int32, (GC, GC), 0)
    c = lax.broadcasted_iota(jnp.int32, (GC, GC), 1)
    return r >= c if lower else r <= c


def _gla_gates(lr, w2_ref, gb_ref, h):
    logit = _dot(lr, w2_ref[:, h * DK:(h + 1) * DK].astype(BF16)) + gb_ref[:, h * DK:(h + 1) * DK]
    la = (jnp.minimum(logit, 0.0) - jnp.log(1.0 + jnp.exp(-jnp.abs(logit)))) * (1.0 / 16.0)
    g = _dot(_tri(True).astype(F32), la, prec=HIGHEST)
    return logit, g


def _gla_specs(nc, rev):
    idx = (lambda n: nc - 1 - n) if rev else (lambda n: n)
    half = 2 * DK
    return (
        [pl.BlockSpec((GC, half), lambda n, j=j: (idx(n), C_GQ // half + j)) for j in range(2)]
        + [pl.BlockSpec((GC, half), lambda n, j=j: (idx(n), C_GK // half + j)) for j in range(2)]
        + [pl.BlockSpec((GC, DV), lambda n, h=h: (idx(n), C_GV // DV + h)) for h in range(GH)]
        + [pl.BlockSpec((GC, LANE), lambda n: (idx(n), 0)), pl.BlockSpec((LANE, GH * DK), lambda n: (0, 0)),
           pl.BlockSpec((1, GH * DK), lambda n: (0, 0))])


def _gla_heads(refs):
    return (lambda h: refs[h // 2][:, (h % 2) * DK:(h % 2 + 1) * DK], lambda h: refs[2 + h // 2][:, (h % 2) * DK:(h % 2 + 1) * DK],
            lambda h: refs[4 + h][...])


def _gla_fwd(proj, plr, w2p, gb, name="gla_fwd"):
    Tn = proj.shape[0]
    nc = Tn // GC

    def body(*refs):
        qh, kh, vh = _gla_heads(refs)
        lr_ref, w2_ref, gb_ref, o_ref, st_ref, S = refs[8:]

        @pl.when(pl.program_id(0) == 0)
        def _():
            S[...] = jnp.zeros_like(S)

        lr = lr_ref[...].astype(BF16)
        causal = _tri(True)
        for h in range(GH):
            _, g = _gla_gates(lr, w2_ref, gb_ref, h)
            gl = g[GC - 1:GC, :]
            k = kh(h)
            v = vh(h).astype(BF16)
            qd = (qh(h) * (DK ** -0.5) * jnp.exp(g)).astype(BF16)
            ki = (k * jnp.exp(-g)).astype(BF16)
            ke = (k * jnp.exp(gl - g)).astype(BF16)
            att = jnp.where(causal, _dot(qd, ki, tb=True), 0.0).astype(BF16)
            sp = S[h]
            st_ref[0, h] = sp
            o_ref[:, h * DV:(h + 1) * DV] = _dot(att, v) + _dot(qd, sp.astype(BF16), tb=True)
            S[h] = sp * jnp.exp(gl) + _dot(v, ke, ta=True)

    return pl.pallas_call(
        body, name=name,
        out_shape=(jax.ShapeDtypeStruct((Tn, GH * DV), F32), jax.ShapeDtypeStruct((nc, GH, DV, DK), F32)),
        grid=(nc,), in_specs=_gla_specs(nc, False),
        out_specs=(pl.BlockSpec((GC, GH * DV), lambda n: (n, 0)), pl.BlockSpec((1, GH, DV, DK), lambda n: (n, 0, 0, 0))),
        scratch_shapes=[pltpu.VMEM((GH, DV, DK), F32)], compiler_params=_cp("arbitrary"),
    )(*([proj] * 8), plr, w2p, gb)


def _gla_bwd(proj, plr, w2p, gb, states, do, name="gla_bwd"):
    Tn = proj.shape[0]
    nc = Tn // GC

    def body(*refs):
        qh, kh, vh = _gla_heads(refs)
        lr_ref, w2_ref, gb_ref, st_ref, do_ref, dqk_ref, dv_ref, dlr_ref, dw2_ref, dgb_ref, dS = refs[8:]

        @pl.when(pl.program_id(0) == 0)
        def _():
            dS[...] = jnp.zeros_like(dS)
            dw2_ref[...] = jnp.zeros_like(dw2_ref)
            dgb_ref[...] = jnp.zeros_like(dgb_ref)

        lrf = lr_ref[...]
        lr = lrf.astype(BF16)
        causal = _tri(True)
        last_row = lax.broadcasted_iota(jnp.int32, (GC, DK), 0) == GC - 1
        dlr = jnp.zeros((GC, LANE), F32)
        for h in range(GH):
            logit, g = _gla_gates(lr, w2_ref, gb_ref, h)
            gl = g[GC - 1:GC, :]
            egl = jnp.exp(gl)
            eg, eng, ege = jnp.exp(g), jnp.exp(-g), jnp.exp(gl - g)
            k = kh(h)
            v = vh(h).astype(BF16)
            dob = do_ref[:, h * DV:(h + 1) * DV].astype(BF16)
            qd = qh(h) * (DK ** -0.5) * eg
            ki = k * eng
            ke = k * ege
            qdb, kib, keb = qd.astype(BF16), ki.astype(BF16), ke.astype(BF16)
            att = jnp.where(causal, _dot(qdb, kib, tb=True), 0.0).astype(BF16)
            datt = jnp.where(causal, _dot(dob, v, tb=True), 0.0).astype(BF16)
            sp = st_ref[0, h]
            dsn = dS[h]
            dsnb = dsn.astype(BF16)
            dv_ref[:, h * DV:(h + 1) * DV] = (_dot(att, dob, ta=True) + _dot(keb, dsnb, tb=True)).astype(BF16)
            dqd = _dot(datt, kib) + _dot(dob, sp.astype(BF16))
            dki = _dot(datt, qdb, ta=True)
            dke = _dot(v, dsnb)
            ddec = jnp.sum(dsn * sp, axis=0, keepdims=True)
            dS[h] = dsn * egl + _dot(dob, qdb, ta=True)
            dke_ke = dke * ke
            dgl = jnp.sum(dke_ke, axis=0, keepdims=True) + ddec * egl
            dg = dqd * qd - dki * ki - dke_ke + jnp.where(last_row, dgl, 0.0)
            dqk_ref[:, h * DK:(h + 1) * DK] = (dqd * ((DK ** -0.5) * eg)).astype(BF16)
            dqk_ref[:, GH * DK + h * DK:GH * DK + (h + 1) * DK] = (dki * eng + dke * ege).astype(BF16)
            dla = _dot(_tri(False).astype(F32), dg, prec=HIGHEST)
            dlogit = dla * (1.0 / 16.0) * _sigmoid(-logit)
            dlb = dlogit.astype(BF16)
            dlr = dlr + _dot(dlb, w2_ref[:, h * DK:(h + 1) * DK].astype(BF16), tb=True)
            dw2_ref[:, h * DK:(h + 1) * DK] += _dot(lr, dlb, ta=True)
            dgb_ref[:, h * DK:(h + 1) * DK] += jnp.sum(dlogit, axis=0, keepdims=True)
        dlr_ref[...] = dlr.astype(BF16)

    rev = lambda n: nc - 1 - n
    row = pl.BlockSpec((GC, GH * DV), lambda n: (rev(n), 0))
    return pl.pallas_call(
        body, name=name,
        out_shape=(jax.ShapeDtypeStruct((Tn, 2 * GH * DK), BF16), jax.ShapeDtypeStruct((Tn, GH * DV), BF16),
                   jax.ShapeDtypeStruct((Tn, LANE), BF16), jax.ShapeDtypeStruct((LANE, GH * DK), F32),
                   jax.ShapeDtypeStruct((1, GH * DK), F32)),
        grid=(nc,),
        in_specs=_gla_specs(nc, True) + [pl.BlockSpec((1, GH, DV, DK), lambda n: (rev(n), 0, 0, 0)), row],
        out_specs=(row, row, pl.BlockSpec((GC, LANE), lambda n: (rev(n), 0)), pl.BlockSpec((LANE, GH * DK), lambda n: (0, 0)),
                   pl.BlockSpec((1, GH * DK), lambda n: (0, 0))),
        scratch_shapes=[pltpu.VMEM((GH, DV, DK), F32)], compiler_params=_cp("arbitrary"),
    )(*([proj] * 8), plr, w2p, gb, states, do)


def _merge_specs(tm):
    row = pl.BlockSpec((tm, D), lambda i: (i, 0))
    gates = [pl.BlockSpec((tm, DV), lambda i, j=c // DV + h: (i, j)) for c in (C_GR, C_GA, C_GB) for h in range(GH)]
    return row, gates, pl.BlockSpec((1, DV), lambda i: (0, 0))


def _merge_fwd(a, go, proj, gnw, name="merge_fwd", tm=256):
    Tn = a.shape[0]

    def body(a_ref, go_ref, *rest):
        gates, w_ref, m_ref = rest[:3 * GH], rest[3 * GH], rest[3 * GH + 1]
        for h in range(GH):
            sl = slice(h * DV, (h + 1) * DV)
            gov = go_ref[:, sl]
            r = lax.rsqrt(jnp.mean(gov * gov, axis=1, keepdims=True) + EPS)
            gr = gates[h][...]
            g2 = gov * r * w_ref[...] * (gr * _sigmoid(gr))
            m_ref[:, sl] = (_sigmoid(gates[GH + h][...]) * a_ref[:, sl] + _sigmoid(gates[2 * GH + h][...]) * g2).astype(BF16)

    row, gates, vec = _merge_specs(tm)
    return pl.pallas_call(
        body, name=name, out_shape=jax.ShapeDtypeStruct((Tn, D), BF16), grid=(Tn // tm,),
        in_specs=[row, row] + gates + [vec], out_specs=row, compiler_params=_cp("parallel"),
    )(a, go, *([proj] * (3 * GH)), gnw)


def _merge_bwd(dm, a, go, proj, gnw, name="merge_bwd", tm=256):
    Tn = a.shape[0]

    def body(dm_ref, a_ref, go_ref, *rest):
        gates = rest[:3 * GH]
        w_ref, da_ref, dgo_ref, dg_ref, dw_ref = rest[3 * GH:]
        wv = w_ref[...]
        dw = jnp.zeros((1, DV), F32)
        for h in range(GH):
            sl = slice(h * DV, (h + 1) * DV)
            dmv, av, gov, gr = dm_ref[:, sl], a_ref[:, sl], go_ref[:, sl], gates[h][...]
            sa, sb, sg = _sigmoid(gates[GH + h][...]), _sigmoid(gates[2 * GH + h][...]), _sigmoid(gr)
            r = lax.rsqrt(jnp.mean(gov * gov, axis=1, keepdims=True) + EPS)
            gn0 = gov * r
            gn = gn0 * wv
            silu = gr * sg
            dg2 = dmv * sb
            da_ref[:, sl] = dmv * sa
            dg_ref[:, D + h * DV:D + (h + 1) * DV] = (dmv * av * sa * (1.0 - sa)).astype(BF16)
            dg_ref[:, 2 * D + h * DV:2 * D + (h + 1) * DV] = (dg2 * gn * silu * (1.0 - sb)).astype(BF16)
            dg_ref[:, sl] = (dg2 * gn * (sg * (1.0 + gr * (1.0 - sg)))).astype(BF16)
            dgn = dg2 * silu
            dw = dw + jnp.sum(dgn * gn0, axis=0, keepdims=True)
            gg = dgn * wv
            dgo_ref[:, sl] = r * gg - gov * (r * r * r * jnp.mean(gg * gov, axis=1, keepdims=True))

        @pl.when(pl.program_id(0) == 0)
        def _():
            dw_ref[...] = dw

        @pl.when(pl.program_id(0) > 0)
        def _():
            dw_ref[...] += dw

    row, gates, vec = _merge_specs(tm)
    return pl.pallas_call(
        body, name=name,
        out_shape=(jax.ShapeDtypeStruct((Tn, D), F32), jax.ShapeDtypeStruct((Tn, D), F32), jax.ShapeDtypeStruct((Tn, 3 * D), BF16),
                   jax.ShapeDtypeStruct((1, DV), F32)),
        grid=(Tn // tm,), in_specs=[row, row, row] + gates + [vec],
        out_specs=(row, row, pl.BlockSpec((tm, 3 * D), lambda i: (i, 0)), vec), compiler_params=_cp("arbitrary"),
    )(dm, a, go, *([proj] * (3 * GH)), gnw)


def _ffn_up(v2, wg, wu, name="ffn_up", tm=1024):
    Tn = v2.shape[0]
    tm = min(tm, Tn)

    def body(v_ref, wg_ref, wu_ref, a_ref, b_ref, ff_ref):
        vv = v_ref[...]
        a = _dot(vv, wg_ref[...])
        b = _dot(vv, wu_ref[...])
        a_ref[...] = a
        b_ref[...] = b
        ff_ref[...] = (a * _sigmoid(a) * b).astype(BF16)

    w = pl.BlockSpec((None, D, FS), lambda k, i: (k, 0, 0))
    act = pl.BlockSpec((None, tm, FS), lambda k, i: (k, i, 0))
    return pl.pallas_call(
        body, name=name,
        out_shape=(jax.ShapeDtypeStruct((NDEV, Tn, FS), F32), jax.ShapeDtypeStruct((NDEV, Tn, FS), F32),
                   jax.ShapeDtypeStruct((NDEV, Tn, FS), BF16)),
        grid=(NDEV, Tn // tm), in_specs=[pl.BlockSpec((tm, D), lambda k, i: (i, 0)), w, w], out_specs=(act, act, act),
        compiler_params=_cp("parallel", "parallel"),
    )(v2, wg, wu)


def _ffn_down(ff, wd, h1, name="ffn_down", tm=1024, tn=1024):
    Tn = h1.shape[0]
    tm = min(tm, Tn)

    def body(f_ref, w_ref, r_ref, o_ref, acc):
        k = pl.program_id(2)
        p = _dot(f_ref[...], w_ref[...])

        @pl.when(k == 0)
        def _():
            acc[...] = p + r_ref[...]

        @pl.when(k > 0)
        def _():
            acc[...] += p

        @pl.when(k == NDEV - 1)
        def _():
            o_ref[...] = acc[...]

    o = pl.BlockSpec((tm, tn), lambda i, j, k: (i, j))
    return pl.pallas_call(
        body, name=name, out_shape=jax.ShapeDtypeStruct((Tn, D), F32), grid=(Tn // tm, D // tn, NDEV),
        in_specs=[pl.BlockSpec((None, tm, FS), lambda i, j, k: (k, i, 0)), pl.BlockSpec((None, FS, tn), lambda i, j, k: (k, 0, j)), o],
        out_specs=o, scratch_shapes=[pltpu.VMEM((tm, tn), F32)], compiler_params=_cp("parallel", "parallel", "arbitrary"),
    )(ff, wd, h1)


def _ffn_dact(dh2b, wd, a, b, name="ffn_dact", tm=1024):
    Tn = dh2b.shape[0]
    tm = min(tm, Tn)

    def body(d_ref, w_ref, a_ref, b_ref, da_ref, db_ref):
        dff = _dot(d_ref[...], w_ref[...], tb=True)
        av = a_ref[...]
        sg = _sigmoid(av)
        da_ref[...] = (dff * b_ref[...] * (sg * (1.0 + av * (1.0 - sg)))).astype(BF16)
        db_ref[...] = (dff * (av * sg)).astype(BF16)

    act = pl.BlockSpec((None, tm, FS), lambda k, i: (k, i, 0))
    return pl.pallas_call(
        body, name=name,
        out_shape=(jax.ShapeDtypeStruct((NDEV, Tn, FS), BF16), jax.ShapeDtypeStruct((NDEV, Tn, FS), BF16)),
        grid=(NDEV, Tn // tm),
        in_specs=[pl.BlockSpec((tm, D), lambda k, i: (i, 0)), pl.BlockSpec((None, FS, D), lambda k, i: (k, 0, 0)), act, act],
        out_specs=(act, act), compiler_params=_cp("parallel", "parallel"),
    )(dh2b, wd, a, b)


def _ffn_dwd(ff, dh2b, name="ffn_dwd", tn=1024):
    Tn = dh2b.shape[0]

    def body(f_ref, d_ref, o_ref):
        o_ref[...] = _dot(f_ref[...], d_ref[...], ta=True).astype(BF16)

    return pl.pallas_call(
        body, name=name, out_shape=jax.ShapeDtypeStruct((NDEV, FS, D), BF16), grid=(NDEV, D // tn),
        in_specs=[pl.BlockSpec((None, Tn, FS), lambda k, j: (k, 0, 0)), pl.BlockSpec((Tn, tn), lambda k, j: (0, j))],
        out_specs=pl.BlockSpec((None, FS, tn), lambda k, j: (k, 0, j)), compiler_params=_cp("parallel", "parallel"),
    )(ff, dh2b)


def _ffn_dwgu(v2, da, db, name="ffn_dwgu", tm=1024):
    Tn = v2.shape[0]

    def body(v_ref, da_ref, db_ref, og_ref, ou_ref):
        vv = v_ref[...]
        og_ref[...] = _dot(vv, da_ref[...], ta=True).astype(BF16)
        ou_ref[...] = _dot(vv, db_ref[...], ta=True).astype(BF16)

    act = pl.BlockSpec((None, Tn, FS), lambda k, i: (k, 0, 0))
    o = pl.BlockSpec((None, tm, FS), lambda k, i: (k, i, 0))
    return pl.pallas_call(
        body, name=name,
        out_shape=(jax.ShapeDtypeStruct((NDEV, D, FS), BF16), jax.ShapeDtypeStruct((NDEV, D, FS), BF16)),
        grid=(NDEV, D // tm), in_specs=[pl.BlockSpec((Tn, tm), lambda k, i: (0, i)), act, act], out_specs=(o, o),
        compiler_params=_cp("parallel", "parallel"),
    )(v2, da, db)


def _ffn_dv2(da, db, wg, wu, name="ffn_dv2", tm=1024, tn=1024):
    Tn = da.shape[1]
    tm = min(tm, Tn)

    def body(da_ref, db_ref, wg_ref, wu_ref, o_ref, acc):
        k = pl.program_id(2)
        p = _dot(da_ref[...], wg_ref[...], tb=True) + _dot(db_ref[...], wu_ref[...], tb=True)

        @pl.when(k == 0)
        def _():
            acc[...] = p

        @pl.when(k > 0)
        def _():
            acc[...] += p

        @pl.when(k == NDEV - 1)
        def _():
            o_ref[...] = acc[...]

    act = pl.BlockSpec((None, tm, FS), lambda i, j, k: (k, i, 0))
    w = pl.BlockSpec((None, tn, FS), lambda i, j, k: (k, j, 0))
    return pl.pallas_call(
        body, name=name, out_shape=jax.ShapeDtypeStruct((Tn, D), F32), grid=(Tn // tm, D // tn, NDEV),
        in_specs=[act, act, w, w], out_specs=pl.BlockSpec((tm, tn), lambda i, j, k: (i, j)),
        scratch_shapes=[pltpu.VMEM((tm, tn), F32)], compiler_params=_cp("parallel", "parallel", "arbitrary"),
    )(da, db, wg, wu)


def _adam_math(w, g, m, v):
    m2 = B1 * m + (1.0 - B1) * g
    v2 = B2 * v + (1.0 - B2) * (g * g)
    mh = m2 / (1.0 - B1 ** STEP)
    vh = v2 / (1.0 - B2 ** STEP)
    return -LR * (mh / (jnp.sqrt(vh) + AEPS) + WD * w), m2, v2


def _adamw(w, m, v, psums, parts, chip_idx, name, tr):
    R, C = w.shape

    def body(s_ref, w_ref, m_ref, v_ref, o_ref, p_ref, g_ref, d_ref, m2_ref, v2_ref):
        g = ((o_ref[...].astype(F32) + p_ref[0].astype(F32)) + p_ref[1].astype(F32)) + p_ref[2].astype(F32)
        d, m2, v2 = _adam_math(w_ref[...], g, m_ref[...], v_ref[...])
        g_ref[...] = g
        d_ref[...] = d
        m2_ref[...] = m2
        v2_ref[...] = v2

    blk = pl.BlockSpec((tr, C), lambda i, s: (i, 0))
    out = jax.ShapeDtypeStruct((R, C), F32)
    grid_spec = pltpu.PrefetchScalarGridSpec(
        num_scalar_prefetch=1, grid=(R // tr,),
        in_specs=[blk, blk, blk, pl.BlockSpec((None, tr, C), lambda i, s: (s[0], i, 0)), pl.BlockSpec((3, tr, C), lambda i, s: (0, i, 0))],
        out_specs=(blk, blk, blk, blk),
    )
    return pl.pallas_call(body, name=name, out_shape=(out, out, out, out), grid_spec=grid_spec, compiler_params=_cp("parallel"),
                          )(chip_idx, w, m, v, psums, parts)


def _adamw_rows(w, m, v, g, name, tr):
    R = w.shape[0]

    def body(w_ref, m_ref, v_ref, g_ref, d_ref, m2_ref, v2_ref):
        d, m2, v2 = _adam_math(w_ref[...], g_ref[...], m_ref[...], v_ref[...])
        d_ref[...] = d
        m2_ref[...] = m2
        v2_ref[...] = v2

    blk = pl.BlockSpec((tr,) + w.shape[1:], lambda i: (i, 0, 0))
    out = jax.ShapeDtypeStruct(w.shape, F32)
    return pl.pallas_call(body, name=name, out_shape=(out, out, out), grid=(R // tr,), in_specs=[blk] * 4, out_specs=(blk, blk, blk),
                          compiler_params=_cp("parallel"))(w, m, v, g)


def _sum_parts(psums, parts, chip_idx, name, tr, tc):
    _, R, C = psums.shape

    def body(s_ref, o_ref, p_ref, g_ref):
        g_ref[...] = ((o_ref[...].astype(F32) + p_ref[0].astype(F32)) + p_ref[1].astype(F32)) + p_ref[2].astype(F32)

    grid_spec = pltpu.PrefetchScalarGridSpec(
        num_scalar_prefetch=1, grid=(R // tr, C // tc),
        in_specs=[pl.BlockSpec((None, tr, tc), lambda i, j, s: (s[0], i, j)), pl.BlockSpec((3, tr, tc), lambda i, j, s: (0, i, j))],
        out_specs=pl.BlockSpec((tr, tc), lambda i, j, s: (i, j)),
    )
    return pl.pallas_call(body, name=name, out_shape=jax.ShapeDtypeStruct((R, C), F32), grid_spec=grid_spec,
                          compiler_params=_cp("parallel", "parallel"))(chip_idx, psums, parts)


def _adamw_plain(w, m, v, g, name):
    def body(w_ref, m_ref, v_ref, g_ref, d_ref, m2_ref, v2_ref):
        d, m2, v2 = _adam_math(w_ref[...], g_ref[...], m_ref[...], v_ref[...])
        d_ref[...] = d
        m2_ref[...] = m2
        v2_ref[...] = v2

    out = jax.ShapeDtypeStruct(w.shape, F32)
    return pl.pallas_call(body, name=name, out_shape=(out, out, out))(w, m, v, g)


def _sum_devices(pack_all, name="sum_small"):
    def body(p_ref, o_ref):
        s = p_ref[0]
        for k in range(1, NDEV):
            s = s + p_ref[k]
        o_ref[...] = s

    return pl.pallas_call(body, name=name, out_shape=jax.ShapeDtypeStruct(pack_all.shape[1:], F32))(pack_all)


def _pair_add(g5, recv, c_idx, name, tr):
    _, _, R, C = g5.shape

    def body(c_ref, g_ref, r_ref, o_ref):
        o_ref[...] = (g_ref[...].astype(F32) + r_ref[...].astype(F32)).astype(BF16)

    grid_spec = pltpu.PrefetchScalarGridSpec(
        num_scalar_prefetch=1, grid=(4, R // tr),
        in_specs=[pl.BlockSpec((None, None, tr, C), lambda q, i, c: (q, c[0], i, 0)), pl.BlockSpec((None, tr, C), lambda q, i, c: (q, i, 0))],
        out_specs=pl.BlockSpec((None, tr, C), lambda q, i, c: (q, i, 0)),
    )
    return pl.pallas_call(
        body, name=name, out_shape=jax.ShapeDtypeStruct((4, R, C), BF16), grid_spec=grid_spec,
        compiler_params=_cp("parallel", "parallel"),
    )(c_idx, g5, recv)


_ANY = pl.BlockSpec(memory_space=pl.ANY)


def _mesh_pos():
    x, y, c = lax.axis_index("x"), lax.axis_index("y"), lax.axis_index("c")
    return x, y, c, [(1 - x, y), (x, 1 - y), (1 - x, 1 - y)]


def _all_gather(shards, name="gather_weights"):
    n = len(shards)

    def body(*refs):
        ins, outs = refs[:n], refs[n:2 * n]
        send, recv, loc = refs[2 * n + 1:]
        x, y, c, chips = _mesh_pos()
        me, sib = (x, y, c), (x, y, 1 - c)

        def cp(a, k, block, to, own=False):
            dst = outs[a].at[4 * block[0] + 2 * block[1] + block[2]]
            return pltpu.make_async_remote_copy(src_ref=ins[a] if own else dst, dst_ref=dst, send_sem=send.at[a, k],
                                                recv_sem=recv.at[a, k], device_id=to, device_id_type=MESH)

        local = [pltpu.make_async_copy(ins[a], outs[a].at[4 * x + 2 * y + c], loc.at[a]) for a in range(n)]
        first = []
        for a in range(n):
            local[a].start()
            first.append(cp(a, 0, me, sib, own=True))
            first += [cp(a, 1 + j, me, (*chip, c), own=True) for j, chip in enumerate(chips)]
        for d in first:
            d.start()
        passed = []
        for a in range(n):
            for j, chip in enumerate(chips):
                cp(a, 1 + j, (*chip, c), me).wait_recv()
                d = cp(a, 4 + j, (*chip, c), sib)
                d.start()
                passed.append(d)
        for a in range(n):
            cp(a, 0, sib, me).wait_recv()
            for j, chip in enumerate(chips):
                cp(a, 4 + j, (*chip, 1 - c), me).wait_recv()
        for d in first + passed:
            d.wait_send()
        for d in local:
            d.wait()
        refs[2 * n][...] = jnp.zeros_like(refs[2 * n])

    return pl.pallas_call(
        body, name=name,
        out_shape=tuple(jax.ShapeDtypeStruct((NDEV,) + s.shape, s.dtype) for s in shards) + (jax.ShapeDtypeStruct((8, LANE), F32),),
        in_specs=[_ANY] * n, out_specs=tuple([_ANY] * n) + (pl.BlockSpec(memory_space=pltpu.VMEM),),
        scratch_shapes=[pltpu.SemaphoreType.DMA((n, 7)), pltpu.SemaphoreType.DMA((n, 7)), pltpu.SemaphoreType.DMA((n,))],
    )(*shards)


def _pair_exchange(grads, name):
    n = len(grads)

    def body(*refs):
        ins, outs = refs[:n], refs[n:2 * n]
        send, recv = refs[2 * n:]
        x, y, c, _ = _mesh_pos()
        big = [pltpu.make_async_remote_copy(src_ref=ins[a].at[:, 1 - c], dst_ref=outs[a], send_sem=send.at[a], recv_sem=recv.at[a],
                                            device_id=(x, y, 1 - c), device_id_type=MESH) for a in range(n)]
        for d in big:
            d.start()
        for d in big:
            d.wait_recv()
        for d in big:
            d.wait_send()

    return pl.pallas_call(
        body, name=name, out_shape=tuple(jax.ShapeDtypeStruct((4,) + g.shape[2:], g.dtype) for g in grads),
        in_specs=[_ANY] * n, out_specs=tuple([_ANY] * n),
        scratch_shapes=[pltpu.SemaphoreType.DMA((n,)), pltpu.SemaphoreType.DMA((n,))],
    )(*grads)


def _gather_small(pack, name="gather_small"):
    def body(pk, pk_all, psend, precv, loc):
        x, y, c, chips = _mesh_pos()
        me_slot = 4 * x + 2 * y + c
        sib = (x, y, 1 - c)
        own = pltpu.make_async_copy(pk, pk_all.at[me_slot], loc)
        own.start()
        peers = [sib] + [(*chip, c) for chip in chips] + [(*chip, 1 - c) for chip in chips]
        small = [pltpu.make_async_remote_copy(src_ref=pk, dst_ref=pk_all.at[me_slot], send_sem=psend.at[k], recv_sem=precv.at[k],
                                              device_id=p, device_id_type=MESH) for k, p in enumerate(peers)]
        for d in small:
            d.start()
        for k, p in enumerate(peers):
            pltpu.make_async_remote_copy(src_ref=pk, dst_ref=pk_all.at[4 * p[0] + 2 * p[1] + p[2]], send_sem=psend.at[k],
                                         recv_sem=precv.at[k], device_id=p, device_id_type=MESH).wait_recv()
        for d in small:
            d.wait_send()
        own.wait()

    return pl.pallas_call(
        body, name=name, out_shape=jax.ShapeDtypeStruct((NDEV,) + pack.shape, pack.dtype), in_specs=[_ANY], out_specs=_ANY,
        scratch_shapes=[pltpu.SemaphoreType.DMA((7,)), pltpu.SemaphoreType.DMA((7,)), pltpu.SemaphoreType.DMA(())],
    )(pack)


def _main_row(g):
    return g if g < C_LR else g - RANK


def _window_pieces(lo, hi):
    out = []
    for a, b, where in ((lo, min(hi, C_LR), "main"), (max(lo, C_LR), min(hi, C_LR + RANK), "lr"), (max(lo, C_LR + RANK), hi, "main")):
        if a < b:
            out.append((a, b, where, _main_row(a) if where == "main" else a - C_LR))
    return out


def _assemble_w_in(windows, name="assemble_w_in"):
    edges = NDEV - 1

    def body(b_ref, main_ref, lr_ref, buf, ebuf, in_sems, out_sems, esems):
        def load(k):
            return pltpu.make_async_copy(b_ref.at[k], buf.at[k % 2], in_sems.at[k % 2])

        lr_ref[RANK:, :] = jnp.zeros((LANE - RANK, D), BF16)
        load(0).start()
        pending, edge_out = [], []
        for k in range(NDEV):
            s = k % 2
            load(k).wait()
            if k:
                ebuf[k - 1] = buf[1 - s, WSTEP:WWIN, :] + buf[s, 0:16, :]
                edge_out.append(pltpu.make_async_copy(ebuf.at[k - 1], main_ref.at[pl.ds(_main_row(WSTEP * k), 16)], esems.at[k - 1]))
                edge_out[-1].start()
                for d in pending:
                    d.wait()
            if k + 1 < NDEV:
                load(k + 1).start()
            pending = []
            lo = WSTEP * k + (16 if k else 0)
            hi = WSTEP * k + (WWIN if k == NDEV - 1 else WSTEP)
            for a, b, where, dst in _window_pieces(lo, hi):
                if where == "lr":
                    lr_ref[dst:dst + b - a, :] = buf[s, a - WSTEP * k:b - WSTEP * k, :]
                else:
                    pending.append(pltpu.make_async_copy(buf.at[s, pl.ds(a - WSTEP * k, b - a)], main_ref.at[pl.ds(dst, b - a)],
                                                         out_sems.at[2 * s + len(pending)]))
                    pending[-1].start()
        for d in pending + edge_out:
            d.wait()

    return pl.pallas_call(
        body, name=name,
        out_shape=(jax.ShapeDtypeStruct((NMAIN, D), BF16), jax.ShapeDtypeStruct((LANE, D), BF16)),
        in_specs=[_ANY], out_specs=(_ANY, pl.BlockSpec(memory_space=pltpu.VMEM)),
        scratch_shapes=[pltpu.VMEM((2, WWIN, D), BF16), pltpu.VMEM((edges, 16, D), BF16), pltpu.SemaphoreType.DMA((2,)),
                        pltpu.SemaphoreType.DMA((4,)), pltpu.SemaphoreType.DMA((edges,))],
        compiler_params=pltpu.CompilerParams(vmem_limit_bytes=VMEM_LIMIT),
    )(windows)


def _disassemble_w_in(d_main, d_lr, name="disassemble_w_in"):
    def body(main_ref, lr_ref, g_ref, buf, in_sems, out_sems):
        def loads(k):
            s, out = k % 2, []
            for a, b, where, src0 in _window_pieces(WSTEP * k, WSTEP * k + WWIN):
                if where == "main":
                    out.append(pltpu.make_async_copy(main_ref.at[pl.ds(src0, b - a)], buf.at[s, pl.ds(a - WSTEP * k, b - a)],
                                                     in_sems.at[2 * s + len(out)]))
            return out

        def store(k):
            return pltpu.make_async_copy(buf.at[k % 2], g_ref.at[k], out_sems.at[k % 2])

        for d in loads(0):
            d.start()
        for k in range(NDEV):
            for d in loads(k):
                d.wait()
            for a, b, where, src0 in _window_pieces(WSTEP * k, WSTEP * k + WWIN):
                if where == "lr":
                    buf[k % 2, a - WSTEP * k:b - WSTEP * k, :] = lr_ref[src0:src0 + b - a, :]
            if k:
                store(k - 1).wait()
            if k + 1 < NDEV:
                for d in loads(k + 1):
                    d.start()
            store(k).start()
        store(NDEV - 1).wait()

    return pl.pallas_call(
        body, name=name, out_shape=jax.ShapeDtypeStruct((NDEV, WWIN, D), BF16),
        in_specs=[_ANY, pl.BlockSpec(memory_space=pltpu.VMEM)], out_specs=_ANY,
        scratch_shapes=[pltpu.VMEM((2, WWIN, D), BF16), pltpu.SemaphoreType.DMA((4,)), pltpu.SemaphoreType.DMA((2,))],
        compiler_params=pltpu.CompilerParams(vmem_limit_bytes=VMEM_LIMIT),
    )(d_main, d_lr)


_HBM = pl.BlockSpec(memory_space=pltpu.HBM)
_SEM = pl.BlockSpec(memory_space=pltpu.SEMAPHORE)
_VMEM = pl.BlockSpec(memory_space=pltpu.VMEM)
_SIDE = pltpu.CompilerParams(has_side_effects=pltpu.SideEffectType.DATAFLOW_SIDE_EFFECTING)
_TOKEN = jax.ShapeDtypeStruct((8, LANE), F32)


def _hbm(a):
    return pltpu.with_memory_space_constraint(a, pltpu.HBM)


def _hbm_like(arrs):
    return tuple(pltpu.HBM(a.shape, a.dtype) for a in arrs)


def _tie(x, token):
    return x + token[0, 0].astype(x.dtype)


def _chip_copies(ins, lands, send, recv):
    x, y, c, chips = _mesh_pos()
    return [pltpu.make_async_remote_copy(src_ref=ins[a].at[2 * chip[0] + chip[1]], dst_ref=lands[a].at[j], send_sem=send.at[3 * a + j],
                                         recv_sem=recv.at[3 * a + j], device_id=(*chip, c), device_id_type=MESH)
            for a in range(len(ins)) for j, chip in enumerate(chips)]


def _chip_start(psums, name):
    n = len(psums)
    lands = [lax.empty((3,) + p.shape[1:], p.dtype) for p in psums]

    def body(*refs):
        for d in _chip_copies(refs[:n], refs[n:2 * n], refs[2 * n], refs[2 * n + 1]):
            d.start()
        refs[-1][...] = jnp.zeros_like(refs[-1])

    sems = pltpu.SemaphoreType.DMA((3 * n,))
    out = pl.pallas_call(
        body, name=name, out_shape=(sems, sems) + _hbm_like(psums) + _hbm_like(lands) + (_TOKEN,),
        in_specs=[_HBM] * (2 * n), out_specs=(_SEM, _SEM) + (_HBM,) * (2 * n) + (_VMEM,),
        input_output_aliases={i: 2 + i for i in range(2 * n)}, compiler_params=_SIDE,
    )(*[_hbm(a) for a in list(psums) + lands])
    return out[0], out[1], list(out[2:2 + n]), list(out[2 + n:2 + 2 * n]), out[-1]


def _chip_wait(send, recv, psums, lands, after, name):
    n = len(psums)

    def body(*refs):
        for d in _chip_copies(refs[:n], refs[n:2 * n], refs[2 * n], refs[2 * n + 1]):
            d.wait_send()
            d.wait_recv()

    out = pl.pallas_call(
        body, name=name, out_shape=_hbm_like(psums) + _hbm_like(lands),
        in_specs=[_HBM] * (2 * n) + [_SEM, _SEM, _ANY], out_specs=(_HBM,) * (2 * n),
        input_output_aliases={i: i for i in range(2 * n)}, compiler_params=_SIDE,
    )(*psums, *lands, send, recv, after)
    return list(out[:n]), list(out[n:])


def _slot(chip, c):
    return 4 * chip[0] + 2 * chip[1] + c


def _gather_start(shards, dev, name):
    n = len(shards)
    lands = [lax.dynamic_update_slice(lax.empty((NDEV,) + s.shape, s.dtype), s[None], (dev,) + (0,) * s.ndim) for s in shards]

    def body(*refs):
        src, land, send, recv = refs[:n], refs[n:2 * n], refs[2 * n], refs[2 * n + 1]
        x, y, c, chips = _mesh_pos()
        for a in range(n):
            for k, to in enumerate([(x, y, 1 - c)] + [(*chip, c) for chip in chips]):
                pltpu.make_async_remote_copy(src_ref=src[a], dst_ref=land[a].at[_slot((x, y), c)], send_sem=send.at[4 * a + k],
                                             recv_sem=recv.at[4 * a + k], device_id=to, device_id_type=MESH).start()
        refs[-1][...] = jnp.zeros_like(refs[-1])

    sems = pltpu.SemaphoreType.DMA((4 * n,))
    out = pl.pallas_call(
        body, name=name, out_shape=(sems, sems) + _hbm_like(shards) + _hbm_like(lands) + (_TOKEN,),
        in_specs=[_HBM] * (2 * n), out_specs=(_SEM, _SEM) + (_HBM,) * (2 * n) + (_VMEM,),
        input_output_aliases={i: 2 + i for i in range(2 * n)}, compiler_params=_SIDE,
    )(*[_hbm(a) for a in list(shards) + lands])
    return out[0], out[1], list(out[2:2 + n]), list(out[2 + n:2 + 2 * n]), out[-1]


def _gather_pass(lands, recv, after, name):
    n = len(lands)

    def body(*refs):
        land, recv1 = refs[:n], refs[n]
        send2, recv2 = refs[n + 2], refs[n + 3]
        x, y, c, chips = _mesh_pos()
        for a in range(n):
            for j, chip in enumerate(chips):
                blk = land[a].at[_slot(chip, c)]
                pltpu.make_async_remote_copy(src_ref=blk, dst_ref=blk, send_sem=send2.at[3 * a + j], recv_sem=recv1.at[4 * a + 1 + j],
                                             device_id=(*chip, c), device_id_type=MESH).wait_recv()
                pltpu.make_async_remote_copy(src_ref=blk, dst_ref=blk, send_sem=send2.at[3 * a + j], recv_sem=recv2.at[3 * a + j],
                                             device_id=(x, y, 1 - c), device_id_type=MESH).start()
        refs[-1][...] = jnp.zeros_like(refs[-1])

    sems = pltpu.SemaphoreType.DMA((3 * n,))
    out = pl.pallas_call(
        body, name=name, out_shape=(sems, sems) + _hbm_like(lands) + (_TOKEN,),
        in_specs=[_HBM] * n + [_SEM, _ANY], out_specs=(_SEM, _SEM) + (_HBM,) * n + (_VMEM,),
        input_output_aliases={i: 2 + i for i in range(n)}, compiler_params=_SIDE,
    )(*lands, recv, after)
    return out[0], out[1], list(out[2:2 + n]), out[-1]


def _gather_wait(shards, lands, send, recv, send2, recv2, after, name):
    n = len(lands)

    def body(*refs):
        src, land = refs[:n], refs[n:2 * n]
        send1, recv1, snd2, rcv2 = refs[2 * n:2 * n + 4]
        x, y, c, chips = _mesh_pos()
        sib = (x, y, 1 - c)
        for a in range(n):
            for k in range(4):
                pltpu.make_async_remote_copy(src_ref=src[a], dst_ref=land[a].at[_slot((x, y), c)], send_sem=send1.at[4 * a + k],
                                             recv_sem=recv1.at[4 * a + k], device_id=sib, device_id_type=MESH).wait_send()
            blk = land[a].at[_slot((x, y), 1 - c)]
            pltpu.make_async_remote_copy(src_ref=blk, dst_ref=blk, send_sem=send1.at[4 * a], recv_sem=recv1.at[4 * a],
                                         device_id=sib, device_id_type=MESH).wait_recv()
            for j, chip in enumerate(chips):
                mine, theirs = land[a].at[_slot(chip, c)], land[a].at[_slot(chip, 1 - c)]
                pltpu.make_async_remote_copy(src_ref=mine, dst_ref=mine, send_sem=snd2.at[3 * a + j], recv_sem=rcv2.at[3 * a + j],
                                             device_id=sib, device_id_type=MESH).wait_send()
                pltpu.make_async_remote_copy(src_ref=theirs, dst_ref=theirs, send_sem=snd2.at[3 * a + j], recv_sem=rcv2.at[3 * a + j],
                                             device_id=sib, device_id_type=MESH).wait_recv()

    out = pl.pallas_call(
        body, name=name, out_shape=_hbm_like(shards) + _hbm_like(lands),
        in_specs=[_HBM] * (2 * n) + [_SEM] * 4 + [_ANY], out_specs=(_HBM,) * (2 * n),
        input_output_aliases={i: i for i in range(2 * n)}, compiler_params=_SIDE,
    )(*shards, *lands, send, recv, send2, recv2, after)
    return list(out[n:])


def _pad_to(v, n):
    return jnp.pad(v, [(0, 0)] * (v.ndim - 1) + [(0, n - v.shape[-1])])


def _pack_small(n1, gb, sk, gn, n2, fn, extra=None):
    parts = [n1.reshape(-1), gb.reshape(-1), sk.reshape(-1), gn.reshape(-1), n2.reshape(-1), fn.reshape(-1)]
    flat = jnp.concatenate(parts + ([extra.reshape(-1)] if extra is not None else []))
    return _pad_to(flat, SMALL_N).reshape(SMALL_ROWS, LANE)


def _unpack_small(p):
    f = p.reshape(-1)
    return (f[S_N1:S_GB].reshape(1, D), f[S_GB:S_SK].reshape(1, GH * DK), f[S_SK:S_GN].reshape(1, NQ), f[S_GN:S_N2].reshape(1, DV),
            f[S_N2:S_FN].reshape(1, D), f[S_FN:S_LOSS].reshape(D))


class _NoComm:
    def __init__(self, wo, wg_all, wu_all, wd_all):
        self.rest = (wo, wg_all, wu_all, wd_all)

    def mixed(self, gla_o, gla_norm_w):
        return gla_norm_w

    def rest_weights(self, merged):
        return self.rest

    def ffn_grads(self, d_wg, d_wu, d_wd, norm2_w):
        self.ffn = (d_wg, d_wu, d_wd)
        return norm2_w

    def in_grads(self, d_wmain, d_wlr, d_wo, w_lr):
        self.inw = (d_wmain, d_wlr, d_wo)
        return w_lr


class _Comm:
    def __init__(self, rest_shards, dev, c_idx):
        self.c_idx = c_idx
        self.send, self.recv, self.shards, self.lands, self.token = _gather_start(rest_shards, dev, "gather_rest_start")

    def mixed(self, gla_o, gla_norm_w):
        self.send2, self.recv2, self.lands, token = _gather_pass(self.lands, self.recv, gla_o, "gather_rest_pass")
        return _tie(gla_norm_w, token)

    def rest_weights(self, merged):
        wo_all, wg_all, wu_all, wd_all = _gather_wait(self.shards, self.lands, self.send, self.recv, self.send2, self.recv2,
                                                      merged, "gather_rest_wait")
        return wo_all.reshape(D, D), wg_all, wu_all, wd_all

    def _reduce(self, tag, names, grads, rows):
        recv1 = _pair_exchange(grads, "reduce_pair_" + tag)
        psums = [_pair_add(g, r, self.c_idx, "pair_add_" + nm, tr) for g, r, nm, tr in zip(grads, recv1, names, rows)]
        *flight, token = _chip_start(psums, "reduce_chips_start_" + tag)
        return dict(tag=tag, names=names, rows=rows, flight=flight), token

    def ffn_grads(self, d_wg, d_wu, d_wd, norm2_w):
        self.ffn, token = self._reduce("ffn", ["w_ffn_gate", "w_ffn_up", "w_ffn_down"],
                                       [d_wg.reshape(4, 2, D, FS), d_wu.reshape(4, 2, D, FS), d_wd.reshape(4, 2, FS, D)], [512, 512, 176])
        return _tie(norm2_w, token)

    def in_grads(self, d_wmain, d_wlr, d_wo, w_lr):
        d_win = _disassemble_w_in(d_wmain, d_wlr).reshape(4, 2, WWIN, D)
        self.inw, token = self._reduce("in", ["w_in", "w_out"], [d_win, d_wo.reshape(4, 2, D // NDEV, D)], [808, 256])
        return _tie(w_lr, token)


def _local_step(xs, tgt, norm1_w, gla_gate_b, attn_sinks, gla_norm_w, norm2_w, fnw, w_main, w_lr, w2p, comm):
    u = _rmsnorm_fwd(xs, norm1_w, "norm1_fwd")
    proj = _mm(u, w_main, tb=True, tm=1024, tn=640, tk=D, name="in_proj")
    plr = _mm(u, w_lr, tb=True, tm=1024, tn=LANE, tk=D, name="in_proj_lr")
    attn_o = _attn_fwd(proj, attn_sinks)
    gla_o, states = _gla_fwd(proj, plr, w2p, gla_gate_b)
    merged = _merge_fwd(attn_o, gla_o, proj, comm.mixed(gla_o, gla_norm_w))
    wo, wg_all, wu_all, wd_all = comm.rest_weights(merged)
    h1 = _mm(merged, wo, tm=1024, tn=512, tk=D, res=xs, name="out_proj")
    v2 = _rmsnorm_fwd(h1, norm2_w, "norm2_fwd")
    fa, fb, ff = _ffn_up(v2, wg_all, wu_all)
    h2 = _ffn_down(ff, wd_all, h1)
    dh2, dh2b, d_fnw, loss_part = _loss_head(h2, fnw, tgt)

    da, db = _ffn_dact(dh2b, wd_all, fa, fb)
    d_wd = _ffn_dwd(ff, dh2b)
    d_wg, d_wu = _ffn_dwgu(v2, da, db)
    norm2_w = comm.ffn_grads(d_wg, d_wu, d_wd, norm2_w)
    dv2 = _ffn_dv2(da, db, wg_all, wu_all)
    dh1, dh1b, d_n2 = _rmsnorm_bwd(dv2, h1, norm2_w, dh2, "norm2_bwd")
    dmerged = _mm(dh1b, wo, tb=True, tm=1024, tn=512, tk=D, name="out_proj_dx")
    d_wo = _mm(merged, dh1b, ta=True, tm=1024, tn=512, tk=xs.shape[0], out_dtype=BF16, name="out_proj_dw")
    d_attn, d_gla, d_gates, d_gnw = _merge_bwd(dmerged, attn_o, gla_o, proj, gla_norm_w)
    d_q, d_kv, d_sinks = _attn_bwd(proj, attn_sinks, attn_o, d_attn)
    d_gqk, d_gv, d_plr, d_w2p, d_gb = _gla_bwd(proj, plr, w2p, gla_gate_b, states, d_gla)
    dproj = jnp.concatenate([d_q, d_kv, d_gqk, d_gv, d_gates], axis=1)
    d_wmain = _mm(dproj, u, ta=True, tm=640, tn=1024, tk=xs.shape[0], out_dtype=BF16, name="in_proj_dw")
    d_wlr = _mm(d_plr, u, ta=True, tm=LANE, tn=1024, tk=xs.shape[0], out_dtype=BF16, name="in_proj_lr_dw")
    du_lr = _mm(d_plr, comm.in_grads(d_wmain, d_wlr, d_wo, w_lr), tm=1024, tn=1024, tk=LANE, name="in_proj_lr_dx")
    du = _mm(dproj, w_main, tm=1024, tn=1024, tk=1280, res=du_lr, name="in_proj_dx")
    dx, _, d_n1 = _rmsnorm_bwd(du, xs, norm1_w, dh1, "norm1_bwd")
    return dx, loss_part, d_w2p, d_gb, d_sinks, d_gnw, d_n1, d_n2, d_fnw


def kernel(x, norm1_w, w_in, gla_gate_w2, gla_gate_b, attn_sinks, gla_norm_w, w_out, norm2_w, w_ffn_gate, w_ffn_up, w_ffn_down, final_norm_w, loss_target, m_norm1_w, m_w_in, m_gla_gate_w2, m_gla_gate_b, m_attn_sinks, m_gla_norm_w, m_w_out, m_norm2_w, m_w_ffn_gate, m_w_ffn_up, m_w_ffn_down, m_final_norm_w, v_norm1_w, v_w_in, v_gla_gate_w2, v_gla_gate_b, v_attn_sinks, v_gla_norm_w, v_w_out, v_norm2_w, v_w_ffn_gate, v_w_ffn_up, v_w_ffn_down, v_final_norm_w):
    xs, tgt = x[0], loss_target[0]
    fnw = final_norm_w.reshape(1, D)
    c_idx = lax.axis_index("c").astype(jnp.int32).reshape(1)
    dev = 4 * lax.axis_index("x") + 2 * lax.axis_index("y") + lax.axis_index("c")

    chip_idx = (2 * lax.axis_index("x") + lax.axis_index("y")).astype(jnp.int32).reshape(1)

    shift = (WS - WSTEP) * dev
    window = lax.dynamic_update_slice(jnp.zeros((WWIN, D), BF16), jnp.transpose(w_in[0]).astype(BF16), (shift, 0))
    win_all, w2_all, tok = _all_gather([window, gla_gate_w2[0]], name="gather_w_in")
    rest = [(w[0] + tok[0, 0]).astype(BF16) for w in (w_out, w_ffn_gate, w_ffn_up, w_ffn_down)]
    comm = _Comm(rest, dev, c_idx)
    w_main, w_lr = _assemble_w_in(win_all)
    w2p =jnp.pad(jnp.transpose(w2_all, (1, 0, 2)).reshape(RANK, GH * DK), ((0, LANE - RANK), (0, 0)))

    dx, loss_part, d_w2p, d_gb, d_sinks, d_gnw, d_n1, d_n2, d_fnw = _local_step(
        xs, tgt, _tie(norm1_w, comm.token), gla_gate_b, attn_sinks, gla_norm_w, norm2_w, fnw, w_main, w_lr, w2p, comm)

    pack = jnp.concatenate([_pack_small(d_n1, d_gb, d_sinks, d_gnw, d_n2, d_fnw, loss_part),
                            d_w2p[:RANK].reshape(GW2_ROWS, LANE)], axis=0)
    small = _sum_devices(_gather_small(pack))

    big = {}
    after = dx
    for grp in (comm.ffn, comm.inw):
        psums, parts = _chip_wait(*grp["flight"], after, "reduce_chips_wait_" + grp["tag"])
        for nm, ps, pt, tr in zip(grp["names"], psums, parts, grp["rows"]):
            w, m, v = {"w_in": (w_in, m_w_in, v_w_in), "w_out": (w_out, m_w_out, v_w_out), "w_ffn_gate": (w_ffn_gate, m_w_ffn_gate, v_w_ffn_gate),
                       "w_ffn_up": (w_ffn_up, m_w_ffn_up, v_w_ffn_up), "w_ffn_down": (w_ffn_down, m_w_ffn_down, v_w_ffn_down)}[nm]
            if nm == "w_in":
                g_win = _sum_parts(ps, pt, chip_idx, "sum_w_in", tr, 1024)
                rows3 = lambda t: jnp.transpose(t[0]).reshape(WS, D // LANE, LANE)
                g3 = lax.dynamic_slice(g_win, (shift, 0), (WS, D)).reshape(WS, D // LANE, LANE)
                out3 = (g3,) + tuple(_adamw_rows(rows3(w), rows3(m), rows3(v), g3, "adamw_w_in", 178))
                big[nm] = [jnp.transpose(t.reshape(WS, D))[None] for t in out3]
            else:
                big[nm] = [t[None] for t in _adamw(w[0], m[0], v[0], ps, pt, chip_idx, "adamw_" + nm, tr)]
            after = big[nm][0]
    g_small = small[:SMALL_ROWS]
    sm = _adamw_plain(_pack_small(norm1_w, gla_gate_b, attn_sinks, gla_norm_w, norm2_w, final_norm_w),
                      _pack_small(m_norm1_w, m_gla_gate_b, m_attn_sinks, m_gla_norm_w, m_norm2_w, m_final_norm_w),
                      _pack_small(v_norm1_w, v_gla_gate_b, v_attn_sinks, v_gla_norm_w, v_norm2_w, v_final_norm_w), g_small, "adamw_small")
    g_w2 = lax.dynamic_slice_in_dim(small[SMALL_ROWS:].reshape(RANK, GH * DK), dev * LANE, LANE, axis=1)
    w2 = [g_w2[None]] + [t[None] for t in _adamw_plain(gla_gate_w2[0], m_gla_gate_w2[0], v_gla_gate_w2[0], g_w2, "adamw_w2")]
    loss = g_small.reshape(-1)[S_LOSS]

    sg, sd, sm2, sv2 = [_unpack_small(t) for t in (g_small,) + tuple(sm)]

    def group(i, s):
        return (s[0], big["w_in"][i], w2[i], s[1], s[2], s[3], big["w_out"][i], s[4], big["w_ffn_gate"][i], big["w_ffn_up"][i],
                big["w_ffn_down"][i], s[5])

    return (loss, dx[None], *group(0, sg), *group(1, sd), *group(2, sm2), *group(3, sv2))
```

```python
import functools

import jax
import jax.numpy as jnp
from jax import lax
from jax.experimental import pallas as pl
from jax.experimental.pallas import tpu as pltpu

F32, BF16 = jnp.float32, jnp.bfloat16
HIGHEST = lax.Precision.HIGHEST

D = 2048
HD, NQ, NKV, GRP, WIN = 64, 32, 4, 8, 128
GH, DK, DV, RANK, GC = 4, 256, 512, 16, 64
FH, NDEV = 5632, 8
FS = FH // NDEV
DIN = 12816
WS = DIN // NDEV
EPS = 1e-6
MASKV = -1e30
LANE = 128

C_AQ, C_AK, C_AV, C_GQ, C_GK, C_GV, C_GR, C_GA, C_GB, NMAIN = 0, 2048, 2304, 2560, 3584, 4608, 6656, 8704, 10752, 12800
C_LR = 6656
WSTEP, WWIN = 1600, 1616

LR, B1, B2, AEPS, WD, STEP = 0.001, 0.9, 0.999, 1e-08, 0.01, 10

S_N1, S_GB, S_SK, S_GN, S_N2, S_FN, S_LOSS, SMALL_N = 0, 2048, 3072, 3104, 3616, 5664, 7712, 8192
SMALL_ROWS = SMALL_N // LANE
GW2_ROWS = RANK * GH * DK // LANE
PACK_ROWS = SMALL_ROWS + GW2_ROWS

MESH = pl.DeviceIdType.MESH


def _dot(a, b, ta=False, tb=False, prec=None):
    dn = (((0,) if ta else (1,), (1,) if tb else (0,)), ((), ()))
    return lax.dot_general(a, b, dn, preferred_element_type=F32, precision=prec)


def _sigmoid(x):
    return 1.0 / (1.0 + jnp.exp(-x))


VMEM_LIMIT = 56 * 1024 * 1024


def _cp(*sem):
    return pltpu.CompilerParams(dimension_semantics=sem, vmem_limit_bytes=VMEM_LIMIT)


def _mm(a, b, *, ta=False, tb=False, tm, tn, tk, out_dtype=F32, res=None, name):
    M, K = (a.shape[1], a.shape[0]) if ta else a.shape
    N = b.shape[0] if tb else b.shape[1]
    tm, tn, tk = min(tm, M), min(tn, N), min(tk, K)
    nk = K // tk
    assert M % tm == 0 and N % tn == 0 and K % tk == 0
    a_spec = pl.BlockSpec((tk, tm), lambda i, j, k: (k, i)) if ta else pl.BlockSpec((tm, tk), lambda i, j, k: (i, k))
    b_spec = pl.BlockSpec((tn, tk), lambda i, j, k: (j, k)) if tb else pl.BlockSpec((tk, tn), lambda i, j, k: (k, j))
    o_spec = pl.BlockSpec((tm, tn), lambda i, j, k: (i, j))
    has_res = res is not None

    def body(*refs):
        a_ref, b_ref = refs[0], refs[1]
        r_ref = refs[2] if has_res else None
        o_ref = refs[3] if has_res else refs[2]
        p = _dot(a_ref[...].astype(BF16), b_ref[...].astype(BF16), ta, tb)
        if nk == 1:
            if has_res:
                p = p + r_ref[...]
            o_ref[...] = p.astype(out_dtype)
        else:
            acc = refs[-1]
            k = pl.program_id(2)

            @pl.when(k == 0)
            def _():
                acc[...] = (p + r_ref[...]) if has_res else p

            @pl.when(k > 0)
            def _():
                acc[...] += p

            @pl.when(k == nk - 1)
            def _():
                o_ref[...] = acc[...].astype(out_dtype)

    return pl.pallas_call(
        body, name=name,
        out_shape=jax.ShapeDtypeStruct((M, N), out_dtype),
        grid=(M // tm, N // tn, nk),
        in_specs=[a_spec, b_spec] + ([o_spec] if has_res else []),
        out_specs=o_spec,
        scratch_shapes=[pltpu.VMEM((tm, tn), F32)] if nk > 1 else [],
        compiler_params=_cp("parallel", "parallel", "arbitrary"),
    )(*((a, b, res) if has_res else (a, b)))


def _rmsnorm_fwd(x, w, name, tm=256):
    Tn = x.shape[0]

    def body(x_ref, w_ref, o_ref):
        xv = x_ref[...]
        r = lax.rsqrt(jnp.mean(xv * xv, axis=1, keepdims=True) + EPS)
        o_ref[...] = (xv * r * w_ref[...]).astype(BF16)

    return pl.pallas_call(
        body, name=name, out_shape=jax.ShapeDtypeStruct((Tn, D), BF16), grid=(Tn // tm,),
        in_specs=[pl.BlockSpec((tm, D), lambda i: (i, 0)), pl.BlockSpec((1, D), lambda i: (0, 0))],
        out_specs=pl.BlockSpec((tm, D), lambda i: (i, 0)), compiler_params=_cp("parallel"),
    )(x, w)


def _rmsnorm_bwd(dy, h, w, res, name, tm=256):
    Tn = h.shape[0]

    def body(dy_ref, h_ref, w_ref, res_ref, dh_ref, dhb_ref, dw_ref):
        hv, dyv = h_ref[...], dy_ref[...]
        r = lax.rsqrt(jnp.mean(hv * hv, axis=1, keepdims=True) + EPS)
        g = dyv * w_ref[...]
        dh = res_ref[...] + r * g - hv * (r * r * r * jnp.mean(g * hv, axis=1, keepdims=True))
        dh_ref[...] = dh
        dhb_ref[...] = dh.astype(BF16)
        part = jnp.sum(dyv * hv * r, axis=0, keepdims=True)

        @pl.when(pl.program_id(0) == 0)
        def _():
            dw_ref[...] = part

        @pl.when(pl.program_id(0) > 0)
        def _():
            dw_ref[...] += part

    row = pl.BlockSpec((tm, D), lambda i: (i, 0))
    vec = pl.BlockSpec((1, D), lambda i: (0, 0))
    return pl.pallas_call(
        body, name=name,
        out_shape=(jax.ShapeDtypeStruct((Tn, D), F32), jax.ShapeDtypeStruct((Tn, D), BF16), jax.ShapeDtypeStruct((1, D), F32)),
        grid=(Tn // tm,), in_specs=[row, row, vec, row], out_specs=(row, row, vec), compiler_params=_cp("arbitrary"),
    )(dy, h, w, res)


def _loss_head(h2, wf, tgt, name="loss_head", tm=256):
    Tn = h2.shape[0]

    def body(h_ref, w_ref, t_ref, dh_ref, dhb_ref, dw_ref, loss_ref):
        hv, wv = h_ref[...], w_ref[...]
        r = lax.rsqrt(jnp.mean(hv * hv, axis=1, keepdims=True) + EPS)
        hn = hv * r
        e = hn * wv - t_ref[...]
        dy = e * (1.0 / D)
        g = dy * wv
        dh = r * g - hv * (r * r * r * jnp.mean(g * hv, axis=1, keepdims=True))
        dh_ref[...] = dh
        dhb_ref[...] = dh.astype(BF16)
        part = jnp.sum(dy * hn, axis=0, keepdims=True)
        lpart = (0.5 / D) * jnp.sum(jnp.sum(e * e, axis=1, keepdims=True), axis=0, keepdims=True)

        @pl.when(pl.program_id(0) == 0)
        def _():
            dw_ref[...] = part
            loss_ref[...] = lpart

        @pl.when(pl.program_id(0) > 0)
        def _():
            dw_ref[...] += part
            loss_ref[...] += lpart

    row = pl.BlockSpec((tm, D), lambda i: (i, 0))
    vec = pl.BlockSpec((1, D), lambda i: (0, 0))
    one = pl.BlockSpec((1, 1), lambda i: (0, 0))
    return pl.pallas_call(
        body, name=name,
        out_shape=(jax.ShapeDtypeStruct((Tn, D), F32), jax.ShapeDtypeStruct((Tn, D), BF16), jax.ShapeDtypeStruct((1, D), F32),
                   jax.ShapeDtypeStruct((1, 1), F32)),
        grid=(Tn // tm,), in_specs=[row, vec, row], out_specs=(row, row, vec, one), compiler_params=_cp("arbitrary"),
    )(h2, wf, tgt)


def _attn_mask(n):
    qi = lax.broadcasted_iota(jnp.int32, (NKV, GRP * WIN, 2 * WIN), 1) % WIN
    ki = lax.broadcasted_iota(jnp.int32, (NKV, GRP * WIN, 2 * WIN), 2)
    rel = qi + WIN - ki
    return (rel >= 0) & (rel < WIN) & ((n > 0) | (ki >= WIN))


def _kv_heads(prev_ref, cur_ref):
    return jnp.stack([jnp.concatenate([prev_ref[:, h * HD:(h + 1) * HD], cur_ref[:, h * HD:(h + 1) * HD]], axis=0) for h in range(NKV)])


def _q_heads(ref):
    return jnp.stack([jnp.concatenate([ref[:, (h * GRP + g) * HD:(h * GRP + g + 1) * HD] for g in range(GRP)], axis=0) for h in range(NKV)])


def _attn_probs(q_ref, kc_ref, kp_ref, sink_ref, mask):
    kk = _kv_heads(kp_ref, kc_ref).astype(BF16)
    qs = _q_heads(q_ref).astype(BF16)
    s = jnp.einsum('hqd,hkd->hqk', qs, kk, preferred_element_type=F32) * (HD ** -0.5)
    s = jnp.where(mask, s, MASKV)
    sink = jnp.stack([jnp.concatenate([jnp.full((WIN, 1), sink_ref[0, h * GRP + g], F32) for g in range(GRP)], axis=0) for h in range(NKV)])
    m = jnp.maximum(jnp.max(s, axis=2, keepdims=True), sink)
    e = jnp.exp(s - m)
    es = jnp.exp(sink - m)
    inv = 1.0 / (jnp.sum(e, axis=2, keepdims=True) + es)
    return e * inv, es * inv, qs, kk


def _attn_specs(nb, last):
    cur = lambda n: jnp.minimum(n, last)
    prev = lambda n: jnp.maximum(jnp.minimum(n, last) - 1, 0)
    return [
        pl.BlockSpec((WIN, NQ * HD), lambda n: (cur(n), C_AQ // (NQ * HD))),
        pl.BlockSpec((WIN, NKV * HD), lambda n: (cur(n), C_AK // (NKV * HD))),
        pl.BlockSpec((WIN, NKV * HD), lambda n: (prev(n), C_AK // (NKV * HD))),
        pl.BlockSpec((WIN, NKV * HD), lambda n: (cur(n), C_AV // (NKV * HD))),
        pl.BlockSpec((WIN, NKV * HD), lambda n: (prev(n), C_AV // (NKV * HD))),
    ]


def _attn_fwd(proj, sinks, name="attn_fwd"):
    Tn = proj.shape[0]
    nb = Tn // WIN

    def body(q_ref, kc_ref, kp_ref, vc_ref, vp_ref, sink_ref, o_ref):
        p, _, _, _ = _attn_probs(q_ref, kc_ref, kp_ref, sink_ref, _attn_mask(pl.program_id(0)))
        o = jnp.einsum('hqk,hkd->hqd', p.astype(BF16), _kv_heads(vp_ref, vc_ref).astype(BF16), preferred_element_type=F32)
        for h in range(NKV):
            for g in range(GRP):
                o_ref[:, (h * GRP + g) * HD:(h * GRP + g + 1) * HD] = o[h, g * WIN:(g + 1) * WIN, :]

    return pl.pallas_call(
        body, name=name, out_shape=jax.ShapeDtypeStruct((Tn, D), F32), grid=(nb,),
        in_specs=_attn_specs(nb, nb - 1) + [pl.BlockSpec(memory_space=pltpu.SMEM)],
        out_specs=pl.BlockSpec((WIN, D), lambda n: (n, 0)), compiler_params=_cp("parallel"),
    )(proj, proj, proj, proj, proj, sinks)


def _attn_bwd(proj, sinks, o, do, name="attn_bwd"):
    Tn = proj.shape[0]
    nb = Tn // WIN
    KW = NKV * HD

    def body(q_ref, kc_ref, kp_ref, vc_ref, vp_ref, o_ref, do_ref, sink_ref, dq_ref, dkv_ref, dsk_ref, carry, cur):
        n = pl.program_id(0)

        @pl.when(n == 0)
        def _():
            carry[...] = jnp.zeros_like(carry)
            dsk_ref[...] = jnp.zeros_like(dsk_ref)

        @pl.when(n < nb)
        def _():
            p, ps, qs, kk = _attn_probs(q_ref, kc_ref, kp_ref, sink_ref, _attn_mask(n))
            vv = _kv_heads(vp_ref, vc_ref).astype(BF16)
            dos = _q_heads(do_ref)
            delta = jnp.sum(dos * _q_heads(o_ref), axis=2, keepdims=True)
            dosb = dos.astype(BF16)
            dp = jnp.einsum('hqd,hkd->hqk', dosb, vv, preferred_element_type=F32)
            ds = (p * (dp - delta) * (HD ** -0.5)).astype(BF16)
            dq = jnp.einsum('hqk,hkd->hqd', ds, kk, preferred_element_type=F32)
            dkk = jnp.einsum('hqk,hqd->hkd', ds, qs, preferred_element_type=F32)
            dvv = jnp.einsum('hqk,hqd->hkd', p.astype(BF16), dosb, preferred_element_type=F32)
            dsk = ps * delta
            for h in range(NKV):
                for g in range(GRP):
                    i = h * GRP + g
                    dq_ref[:, i * HD:(i + 1) * HD] = dq[h, g * WIN:(g + 1) * WIN, :].astype(BF16)
                    dsk_ref[:, i:i + 1] -= jnp.sum(dsk[h, g * WIN:(g + 1) * WIN, :], axis=0, keepdims=True)
                dkv_ref[:, h * HD:(h + 1) * HD] = (carry[:, h * HD:(h + 1) * HD] + dkk[h, :WIN, :]).astype(BF16)
                dkv_ref[:, KW + h * HD:KW + (h + 1) * HD] = (carry[:, KW + h * HD:KW + (h + 1) * HD] + dvv[h, :WIN, :]).astype(BF16)
                cur[:, h * HD:(h + 1) * HD] = dkk[h, WIN:, :]
                cur[:, KW + h * HD:KW + (h + 1) * HD] = dvv[h, WIN:, :]
            carry[...] = cur[...]

        @pl.when(n == nb)
        def _():
            dkv_ref[...] = carry[...].astype(BF16)

    last = nb - 1
    row = pl.BlockSpec((WIN, D), lambda n: (jnp.minimum(n, last), 0))
    return pl.pallas_call(
        body, name=name,
        out_shape=(jax.ShapeDtypeStruct((Tn, D), BF16), jax.ShapeDtypeStruct((Tn, 2 * KW), BF16), jax.ShapeDtypeStruct((1, NQ), F32)),
        grid=(nb + 1,),
        in_specs=_attn_specs(nb, last) + [row, row, pl.BlockSpec(memory_space=pltpu.SMEM)],
        out_specs=(row, pl.BlockSpec((WIN, 2 * KW), lambda n: (jnp.maximum(n - 1, 0), 0)), pl.BlockSpec((1, NQ), lambda n: (0, 0))),
        scratch_shapes=[pltpu.VMEM((WIN, 2 * KW), F32), pltpu.VMEM((WIN, 2 * KW), F32)],
        compiler_params=_cp("arbitrary"),
    )(proj, proj, proj, proj, proj, o, do, sinks)


def _tri(lower):
    r = lax.broadcasted_iota(jnp.int32, (GC, GC), 0)
    c = lax.broadcasted_iota(jnp.int32, (GC, GC), 1)
    return r >= c if lower else r <= c


def _per_head(a):
    return jnp.stack([a[:, h * DK:(h + 1) * DK] for h in range(GH)])


def _all_heads(a):
    return jnp.concatenate([a[h] for h in range(GH)], axis=1)


def _gla_gates(lr, w2_ref, gb_ref):
    logit = _dot(lr, w2_ref[...].astype(BF16)) + gb_ref[...]
    la = (jnp.minimum(logit, 0.0) - jnp.log(1.0 + jnp.exp(-jnp.abs(logit)))) * (1.0 / 16.0)
    g = _dot(_tri(True).astype(F32), la, prec=HIGHEST)
    return logit, g


def _bmm(spec, a, b):
    return jnp.einsum(spec, a, b, preferred_element_type=F32)


def _gla_specs(nc, rev):
    idx = (lambda n: nc - 1 - n) if rev else (lambda n: n)
    half = 2 * DK
    return (
        [pl.BlockSpec((GC, half), lambda n, j=j: (idx(n), C_GQ // half + j)) for j in range(2)]
        + [pl.BlockSpec((GC, half), lambda n, j=j: (idx(n), C_GK // half + j)) for j in range(2)]
        + [pl.BlockSpec((GC, DV), lambda n, h=h: (idx(n), C_GV // DV + h)) for h in range(GH)]
        + [pl.BlockSpec((GC, LANE), lambda n: (idx(n), 0)), pl.BlockSpec((LANE, GH * DK), lambda n: (0, 0)),
           pl.BlockSpec((1, GH * DK), lambda n: (0, 0))])


def _gla_heads(refs):
    return (lambda h: refs[h // 2][:, (h % 2) * DK:(h % 2 + 1) * DK], lambda h: refs[2 + h // 2][:, (h % 2) * DK:(h % 2 + 1) * DK],
            lambda h: refs[4 + h][...])


def _gla_fwd(proj, plr, w2p, gb, name="gla_fwd"):
    Tn = proj.shape[0]
    nc = Tn // GC

    def body(*refs):
        qh, kh, vh = _gla_heads(refs)
        lr_ref, w2_ref, gb_ref, o_ref, st_ref, S = refs[8:]

        @pl.when(pl.program_id(0) == 0)
        def _():
            S[...] = jnp.zeros_like(S)

        heads = lambda f: jnp.stack([f(h) for h in range(GH)])
        _, g_all = _gla_gates(lr_ref[...].astype(BF16), w2_ref, gb_ref)
        g = _per_head(g_all)
        gl = g[:, GC - 1:GC, :]
        k = heads(kh)
        v = heads(vh).astype(BF16)
        qd = (heads(qh) * (DK ** -0.5) * jnp.exp(g)).astype(BF16)
        ki = (k * jnp.exp(-g)).astype(BF16)
        ke = (k * jnp.exp(gl - g)).astype(BF16)
        att = jnp.where(_tri(True)[None], _bmm('hid,hjd->hij', qd, ki), 0.0).astype(BF16)
        sp = S[...]
        st_ref[0] = sp
        o = _bmm('hij,hjv->hiv', att, v) + _bmm('hid,hvd->hiv', qd, sp.astype(BF16))
        for h in range(GH):
            o_ref[:, h * DV:(h + 1) * DV] = o[h]
        S[...] = sp * jnp.exp(gl) + _bmm('hjv,hjd->hvd', v, ke)

    return pl.pallas_call(
        body, name=name,
        out_shape=(jax.ShapeDtypeStruct((Tn, GH * DV), F32), jax.ShapeDtypeStruct((nc, GH, DV, DK), F32)),
        grid=(nc,), in_specs=_gla_specs(nc, False),
        out_specs=(pl.BlockSpec((GC, GH * DV), lambda n: (n, 0)), pl.BlockSpec((1, GH, DV, DK), lambda n: (n, 0, 0, 0))),
        scratch_shapes=[pltpu.VMEM((GH, DV, DK), F32)], compiler_params=_cp("arbitrary"),
    )(*([proj] * 8), plr, w2p, gb)


def _gla_bwd(proj, plr, w2p, gb, states, do, name="gla_bwd"):
    Tn = proj.shape[0]
    nc = Tn // GC

    def body(*refs):
        qh, kh, vh = _gla_heads(refs)
        lr_ref, w2_ref, gb_ref, st_ref, do_ref, dqk_ref, dv_ref, dlr_ref, dw2_ref, dgb_ref, dS = refs[8:]

        @pl.when(pl.program_id(0) == 0)
        def _():
            dS[...] = jnp.zeros_like(dS)
            dw2_ref[...] = jnp.zeros_like(dw2_ref)
            dgb_ref[...] = jnp.zeros_like(dgb_ref)

        heads = lambda f: jnp.stack([f(h) for h in range(GH)])
        lr = lr_ref[...].astype(BF16)
        causal = _tri(True)[None]
        last_row = lax.broadcasted_iota(jnp.int32, (GH, GC, DK), 1) == GC - 1
        logit, g_all = _gla_gates(lr, w2_ref, gb_ref)
        g = _per_head(g_all)
        gl = g[:, GC - 1:GC, :]
        egl = jnp.exp(gl)
        eg, eng, ege = jnp.exp(g), jnp.exp(-g), jnp.exp(gl - g)
        k = heads(kh)
        v = heads(vh).astype(BF16)
        dob = heads(lambda h: do_ref[:, h * DV:(h + 1) * DV]).astype(BF16)
        qd = heads(qh) * (DK ** -0.5) * eg
        ki = k * eng
        ke = k * ege
        qdb, kib, keb = qd.astype(BF16), ki.astype(BF16), ke.astype(BF16)
        att = jnp.where(causal, _bmm('hid,hjd->hij', qdb, kib), 0.0).astype(BF16)
        datt = jnp.where(causal, _bmm('hiv,hjv->hij', dob, v), 0.0).astype(BF16)
        sp = st_ref[0]
        dsn = dS[...]
        dsnb = dsn.astype(BF16)
        dv = (_bmm('hij,hiv->hjv', att, dob) + _bmm('hjd,hvd->hjv', keb, dsnb)).astype(BF16)
        dqd = _bmm('hij,hjd->hid', datt, kib) + _bmm('hiv,hvd->hid', dob, sp.astype(BF16))
        dki = _bmm('hij,hid->hjd', datt, qdb)
        dke = _bmm('hjv,hvd->hjd', v, dsnb)
        ddec = jnp.sum(dsn * sp, axis=1, keepdims=True)
        dS[...] = dsn * egl + _bmm('hiv,hid->hvd', dob, qdb)
        dke_ke = dke * ke
        dgl = jnp.sum(dke_ke, axis=1, keepdims=True) + ddec * egl
        dg = dqd * qd - dki * ki - dke_ke + jnp.where(last_row, dgl, 0.0)
        dq = (dqd * ((DK ** -0.5) * eg)).astype(BF16)
        dk = (dki * eng + dke * ege).astype(BF16)
        for h in range(GH):
            dv_ref[:, h * DV:(h + 1) * DV] = dv[h]
            dqk_ref[:, h * DK:(h + 1) * DK] = dq[h]
            dqk_ref[:, GH * DK + h * DK:GH * DK + (h + 1) * DK] = dk[h]
        dla = _dot(_tri(False).astype(F32), _all_heads(dg), prec=HIGHEST)
        dlogit = dla * (1.0 / 16.0) * _sigmoid(-logit)
        dlb = dlogit.astype(BF16)
        dlr_ref[...] = _dot(dlb, w2_ref[...].astype(BF16), tb=True).astype(BF16)
        dw2_ref[...] += _dot(lr, dlb, ta=True)
        dgb_ref[...] += jnp.sum(dlogit, axis=0, keepdims=True)

    rev = lambda n: nc - 1 - n
    row = pl.BlockSpec((GC, GH * DV), lambda n: (rev(n), 0))
    return pl.pallas_call(
        body, name=name,
        out_shape=(jax.ShapeDtypeStruct((Tn, 2 * GH * DK), BF16), jax.ShapeDtypeStruct((Tn, GH * DV), BF16),
                   jax.ShapeDtypeStruct((Tn, LANE), BF16), jax.ShapeDtypeStruct((LANE, GH * DK), F32),
                   jax.ShapeDtypeStruct((1, GH * DK), F32)),
        grid=(nc,),
        in_specs=_gla_specs(nc, True) + [pl.BlockSpec((1, GH, DV, DK), lambda n: (rev(n), 0, 0, 0)), row],
        out_specs=(row, row, pl.BlockSpec((GC, LANE), lambda n: (rev(n), 0)), pl.BlockSpec((LANE, GH * DK), lambda n: (0, 0)),
                   pl.BlockSpec((1, GH * DK), lambda n: (0, 0))),
        scratch_shapes=[pltpu.VMEM((GH, DV, DK), F32)], compiler_params=_cp("arbitrary"),
    )(*([proj] * 8), plr, w2p, gb, states, do)


def _merge_specs(tm):
    row = pl.BlockSpec((tm, D), lambda i: (i, 0))
    gates = [pl.BlockSpec((tm, DV), lambda i, j=c // DV + h: (i, j)) for c in (C_GR, C_GA, C_GB) for h in range(GH)]
    return row, gates, pl.BlockSpec((1, DV), lambda i: (0, 0))


def _merge_fwd(a, go, proj, gnw, name="merge_fwd", tm=256):
    Tn = a.shape[0]

    def body(a_ref, go_ref, *rest):
        gates, w_ref, m_ref = rest[:3 * GH], rest[3 * GH], rest[3 * GH + 1]
        for h in range(GH):
            sl = slice(h * DV, (h + 1) * DV)
            gov = go_ref[:, sl]
            r = lax.rsqrt(jnp.mean(gov * gov, axis=1, keepdims=True) + EPS)
            gr = gates[h][...]
            g2 = gov * r * w_ref[...] * (gr * _sigmoid(gr))
            m_ref[:, sl] = (_sigmoid(gates[GH + h][...]) * a_ref[:, sl] + _sigmoid(gates[2 * GH + h][...]) * g2).astype(BF16)

    row, gates, vec = _merge_specs(tm)
    return pl.pallas_call(
        body, name=name, out_shape=jax.ShapeDtypeStruct((Tn, D), BF16), grid=(Tn // tm,),
        in_specs=[row, row] + gates + [vec], out_specs=row, compiler_params=_cp("parallel"),
    )(a, go, *([proj] * (3 * GH)), gnw)


def _merge_bwd(dm, a, go, proj, gnw, name="merge_bwd", tm=256):
    Tn = a.shape[0]

    def body(dm_ref, a_ref, go_ref, *rest):
        gates = rest[:3 * GH]
        w_ref, da_ref, dgo_ref, dg_ref, dw_ref = rest[3 * GH:]
        wv = w_ref[...]
        dw = jnp.zeros((1, DV), F32)
        for h in range(GH):
            sl = slice(h * DV, (h + 1) * DV)
            dmv, av, gov, gr = dm_ref[:, sl], a_ref[:, sl], go_ref[:, sl], gates[h][...]
            sa, sb, sg = _sigmoid(gates[GH + h][...]), _sigmoid(gates[2 * GH + h][...]), _sigmoid(gr)
            r = lax.rsqrt(jnp.mean(gov * gov, axis=1, keepdims=True) + EPS)
            gn0 = gov * r
            gn = gn0 * wv
            silu = gr * sg
            dg2 = dmv * sb
            da_ref[:, sl] = dmv * sa
            dg_ref[:, D + h * DV:D + (h + 1) * DV] = (dmv * av * sa * (1.0 - sa)).astype(BF16)
            dg_ref[:, 2 * D + h * DV:2 * D + (h + 1) * DV] = (dg2 * gn * silu * (1.0 - sb)).astype(BF16)
            dg_ref[:, sl] = (dg2 * gn * (sg * (1.0 + gr * (1.0 - sg)))).astype(BF16)
            dgn = dg2 * silu
            dw = dw + jnp.sum(dgn * gn0, axis=0, keepdims=True)
            gg = dgn * wv
            dgo_ref[:, sl] = r * gg - gov * (r * r * r * jnp.mean(gg * gov, axis=1, keepdims=True))

        @pl.when(pl.program_id(0) == 0)
        def _():
            dw_ref[...] = dw

        @pl.when(pl.program_id(0) > 0)
        def _():
            dw_ref[...] += dw

    row, gates, vec = _merge_specs(tm)
    return pl.pallas_call(
        body, name=name,
        out_shape=(jax.ShapeDtypeStruct((Tn, D), F32), jax.ShapeDtypeStruct((Tn, D), F32), jax.ShapeDtypeStruct((Tn, 3 * D), BF16),
                   jax.ShapeDtypeStruct((1, DV), F32)),
        grid=(Tn // tm,), in_specs=[row, row, row] + gates + [vec],
        out_specs=(row, row, pl.BlockSpec((tm, 3 * D), lambda i: (i, 0)), vec), compiler_params=_cp("arbitrary"),
    )(dm, a, go, *([proj] * (3 * GH)), gnw)


def _ffn_up(v2, wg, wu, name="ffn_up", tm=1024):
    Tn = v2.shape[0]
    tm = min(tm, Tn)

    def body(v_ref, wg_ref, wu_ref, a_ref, b_ref, ff_ref):
        vv = v_ref[...]
        a = _dot(vv, wg_ref[...])
        b = _dot(vv, wu_ref[...])
        a_ref[...] = a
        b_ref[...] = b
        ff_ref[...] = (a * _sigmoid(a) * b).astype(BF16)

    w = pl.BlockSpec((None, D, FS), lambda k, i: (k, 0, 0))
    act = pl.BlockSpec((None, tm, FS), lambda k, i: (k, i, 0))
    return pl.pallas_call(
        body, name=name,
        out_shape=(jax.ShapeDtypeStruct((NDEV, Tn, FS), F32), jax.ShapeDtypeStruct((NDEV, Tn, FS), F32),
                   jax.ShapeDtypeStruct((NDEV, Tn, FS), BF16)),
        grid=(NDEV, Tn // tm), in_specs=[pl.BlockSpec((tm, D), lambda k, i: (i, 0)), w, w], out_specs=(act, act, act),
        compiler_params=_cp("parallel", "parallel"),
    )(v2, wg, wu)


def _ffn_down(ff, wd, h1, name="ffn_down", tm=1024, tn=1024):
    Tn = h1.shape[0]
    tm = min(tm, Tn)

    def body(f_ref, w_ref, r_ref, o_ref, acc):
        k = pl.program_id(2)
        p = _dot(f_ref[...], w_ref[...])

        @pl.when(k == 0)
        def _():
            acc[...] = p + r_ref[...]

        @pl.when(k > 0)
        def _():
            acc[...] += p

        @pl.when(k == NDEV - 1)
        def _():
            o_ref[...] = acc[...]

    o = pl.BlockSpec((tm, tn), lambda i, j, k: (i, j))
    return pl.pallas_call(
        body, name=name, out_shape=jax.ShapeDtypeStruct((Tn, D), F32), grid=(Tn // tm, D // tn, NDEV),
        in_specs=[pl.BlockSpec((None, tm, FS), lambda i, j, k: (k, i, 0)), pl.BlockSpec((None, FS, tn), lambda i, j, k: (k, 0, j)), o],
        out_specs=o, scratch_shapes=[pltpu.VMEM((tm, tn), F32)], compiler_params=_cp("parallel", "parallel", "arbitrary"),
    )(ff, wd, h1)


def _ffn_dact(dh2b, wd, a, b, name="ffn_dact", tm=1024):
    Tn = dh2b.shape[0]
    tm = min(tm, Tn)

    def body(d_ref, w_ref, a_ref, b_ref, da_ref, db_ref):
        dff = _dot(d_ref[...], w_ref[...], tb=True)
        av = a_ref[...]
        sg = _sigmoid(av)
        da_ref[...] = (dff * b_ref[...] * (sg * (1.0 + av * (1.0 - sg)))).astype(BF16)
        db_ref[...] = (dff * (av * sg)).astype(BF16)

    act = pl.BlockSpec((None, tm, FS), lambda k, i: (k, i, 0))
    return pl.pallas_call(
        body, name=name,
        out_shape=(jax.ShapeDtypeStruct((NDEV, Tn, FS), BF16), jax.ShapeDtypeStruct((NDEV, Tn, FS), BF16)),
        grid=(NDEV, Tn // tm),
        in_specs=[pl.BlockSpec((tm, D), lambda k, i: (i, 0)), pl.BlockSpec((None, FS, D), lambda k, i: (k, 0, 0)), act, act],
        out_specs=(act, act), compiler_params=_cp("parallel", "parallel"),
    )(dh2b, wd, a, b)


def _ffn_dwd(ff, dh2b, name="ffn_dwd", tn=1024):
    Tn = dh2b.shape[0]

    def body(f_ref, d_ref, o_ref):
        o_ref[...] = _dot(f_ref[...], d_ref[...], ta=True).astype(BF16)

    return pl.pallas_call(
        body, name=name, out_shape=jax.ShapeDtypeStruct((NDEV, FS, D), BF16), grid=(NDEV, D // tn),
        in_specs=[pl.BlockSpec((None, Tn, FS), lambda k, j: (k, 0, 0)), pl.BlockSpec((Tn, tn), lambda k, j: (0, j))],
        out_specs=pl.BlockSpec((None, FS, tn), lambda k, j: (k, 0, j)), compiler_params=_cp("parallel", "parallel"),
    )(ff, dh2b)


def _ffn_dwgu(v2, da, db, name="ffn_dwgu", tm=1024):
    Tn = v2.shape[0]

    def body(v_ref, da_ref, db_ref, og_ref, ou_ref):
        vv = v_ref[...]
        og_ref[...] = _dot(vv, da_ref[...], ta=True).astype(BF16)
        ou_ref[...] = _dot(vv, db_ref[...], ta=True).astype(BF16)

    act = pl.BlockSpec((None, Tn, FS), lambda k, i: (k, 0, 0))
    o = pl.BlockSpec((None, tm, FS), lambda k, i: (k, i, 0))
    return pl.pallas_call(
        body, name=name,
        out_shape=(jax.ShapeDtypeStruct((NDEV, D, FS), BF16), jax.ShapeDtypeStruct((NDEV, D, FS), BF16)),
        grid=(NDEV, D // tm), in_specs=[pl.BlockSpec((Tn, tm), lambda k, i: (0, i)), act, act], out_specs=(o, o),
        compiler_params=_cp("parallel", "parallel"),
    )(v2, da, db)


def _ffn_dv2(da, db, wg, wu, name="ffn_dv2", tm=1024, tn=1024):
    Tn = da.shape[1]
    tm = min(tm, Tn)

    def body(da_ref, db_ref, wg_ref, wu_ref, o_ref, acc):
        k = pl.program_id(2)
        p = _dot(da_ref[...], wg_ref[...], tb=True) + _dot(db_ref[...], wu_ref[...], tb=True)

        @pl.when(k == 0)
        def _():
            acc[...] = p

        @pl.when(k > 0)
        def _():
            acc[...] += p

        @pl.when(k == NDEV - 1)
        def _():
            o_ref[...] = acc[...]

    act = pl.BlockSpec((None, tm, FS), lambda i, j, k: (k, i, 0))
    w = pl.BlockSpec((None, tn, FS), lambda i, j, k: (k, j, 0))
    return pl.pallas_call(
        body, name=name, out_shape=jax.ShapeDtypeStruct((Tn, D), F32), grid=(Tn // tm, D // tn, NDEV),
        in_specs=[act, act, w, w], out_specs=pl.BlockSpec((tm, tn), lambda i, j, k: (i, j)),
        scratch_shapes=[pltpu.VMEM((tm, tn), F32)], compiler_params=_cp("parallel", "parallel", "arbitrary"),
    )(da, db, wg, wu)


def _adam_math(w, g, m, v):
    m2 = B1 * m + (1.0 - B1) * g
    v2 = B2 * v + (1.0 - B2) * (g * g)
    mh = m2 / (1.0 - B1 ** STEP)
    vh = v2 / (1.0 - B2 ** STEP)
    return -LR * (mh / (jnp.sqrt(vh) + AEPS) + WD * w), m2, v2


def _adamw(w, m, v, psums, parts, chip_idx, name, tr):
    R, C = w.shape

    def body(s_ref, w_ref, m_ref, v_ref, o_ref, p_ref, g_ref, d_ref, m2_ref, v2_ref):
        g = ((o_ref[...].astype(F32) + p_ref[0].astype(F32)) + p_ref[1].astype(F32)) + p_ref[2].astype(F32)
        d, m2, v2 = _adam_math(w_ref[...], g, m_ref[...], v_ref[...])
        g_ref[...] = g
        d_ref[...] = d
        m2_ref[...] = m2
        v2_ref[...] = v2

    blk = pl.BlockSpec((tr, C), lambda i, s: (i, 0))
    out = jax.ShapeDtypeStruct((R, C), F32)
    grid_spec = pltpu.PrefetchScalarGridSpec(
        num_scalar_prefetch=1, grid=(R // tr,),
        in_specs=[blk, blk, blk, pl.BlockSpec((None, tr, C), lambda i, s: (s[0], i, 0)), pl.BlockSpec((3, tr, C), lambda i, s: (0, i, 0))],
        out_specs=(blk, blk, blk, blk),
    )
    return pl.pallas_call(body, name=name, out_shape=(out, out, out, out), grid_spec=grid_spec, compiler_params=_cp("parallel"),
                          )(chip_idx, w, m, v, psums, parts)


def _adamw_rows(w, m, v, g, name, tr):
    R = w.shape[0]

    def body(w_ref, m_ref, v_ref, g_ref, d_ref, m2_ref, v2_ref):
        d, m2, v2 = _adam_math(w_ref[...], g_ref[...], m_ref[...], v_ref[...])
        d_ref[...] = d
        m2_ref[...] = m2
        v2_ref[...] = v2

    blk = pl.BlockSpec((tr,) + w.shape[1:], lambda i: (i, 0, 0))
    out = jax.ShapeDtypeStruct(w.shape, F32)
    return pl.pallas_call(body, name=name, out_shape=(out, out, out), grid=(R // tr,), in_specs=[blk] * 4, out_specs=(blk, blk, blk),
                          compiler_params=_cp("parallel"))(w, m, v, g)


def _sum_parts(psums, parts, chip_idx, name, tr, tc):
    _, R, C = psums.shape

    def body(s_ref, o_ref, p_ref, g_ref):
        g_ref[...] = ((o_ref[...].astype(F32) + p_ref[0].astype(F32)) + p_ref[1].astype(F32)) + p_ref[2].astype(F32)

    grid_spec = pltpu.PrefetchScalarGridSpec(
        num_scalar_prefetch=1, grid=(R // tr, C // tc),
        in_specs=[pl.BlockSpec((None, tr, tc), lambda i, j, s: (s[0], i, j)), pl.BlockSpec((3, tr, tc), lambda i, j, s: (0, i, j))],
        out_specs=pl.BlockSpec((tr, tc), lambda i, j, s: (i, j)),
    )
    return pl.pallas_call(body, name=name, out_shape=jax.ShapeDtypeStruct((R, C), F32), grid_spec=grid_spec,
                          compiler_params=_cp("parallel", "parallel"))(chip_idx, psums, parts)


def _adamw_plain(w, m, v, g, name):
    def body(w_ref, m_ref, v_ref, g_ref, d_ref, m2_ref, v2_ref):
        d, m2, v2 = _adam_math(w_ref[...], g_ref[...], m_ref[...], v_ref[...])
        d_ref[...] = d
        m2_ref[...] = m2
        v2_ref[...] = v2

    out = jax.ShapeDtypeStruct(w.shape, F32)
    return pl.pallas_call(body, name=name, out_shape=(out, out, out))(w, m, v, g)


def _sum_devices(pack_all, name="sum_small"):
    def body(p_ref, o_ref):
        s = p_ref[0]
        for k in range(1, NDEV):
            s = s + p_ref[k]
        o_ref[...] = s

    return pl.pallas_call(body, name=name, out_shape=jax.ShapeDtypeStruct(pack_all.shape[1:], F32))(pack_all)


def _pair_add(g5, recv, c_idx, name, tr):
    _, _, R, C = g5.shape

    def body(c_ref, g_ref, r_ref, o_ref):
        o_ref[...] = (g_ref[...].astype(F32) + r_ref[...].astype(F32)).astype(BF16)

    grid_spec = pltpu.PrefetchScalarGridSpec(
        num_scalar_prefetch=1, grid=(4, R // tr),
        in_specs=[pl.BlockSpec((None, None, tr, C), lambda q, i, c: (q, c[0], i, 0)), pl.BlockSpec((None, tr, C), lambda q, i, c: (q, i, 0))],
        out_specs=pl.BlockSpec((None, tr, C), lambda q, i, c: (q, i, 0)),
    )
    return pl.pallas_call(
        body, name=name, out_shape=jax.ShapeDtypeStruct((4, R, C), BF16), grid_spec=grid_spec,
        compiler_params=_cp("parallel", "parallel"),
    )(c_idx, g5, recv)


_ANY = pl.BlockSpec(memory_space=pl.ANY)


def _mesh_pos():
    x, y, c = lax.axis_index("x"), lax.axis_index("y"), lax.axis_index("c")
    return x, y, c, [(1 - x, y), (x, 1 - y), (1 - x, 1 - y)]


def _all_gather(shards, name="gather_weights"):
    n = len(shards)

    def body(*refs):
        ins, outs = refs[:n], refs[n:2 * n]
        send, recv, loc = refs[2 * n + 1:]
        x, y, c, chips = _mesh_pos()
        me, sib = (x, y, c), (x, y, 1 - c)

        def cp(a, k, block, to, own=False):
            dst = outs[a].at[4 * block[0] + 2 * block[1] + block[2]]
            return pltpu.make_async_remote_copy(src_ref=ins[a] if own else dst, dst_ref=dst, send_sem=send.at[a, k],
                                                recv_sem=recv.at[a, k], device_id=to, device_id_type=MESH)

        local = [pltpu.make_async_copy(ins[a], outs[a].at[4 * x + 2 * y + c], loc.at[a]) for a in range(n)]
        first = []
        for a in range(n):
            local[a].start()
            first.append(cp(a, 0, me, sib, own=True))
            first += [cp(a, 1 + j, me, (*chip, c), own=True) for j, chip in enumerate(chips)]
        for d in first:
            d.start()
        passed = []
        for a in range(n):
            for j, chip in enumerate(chips):
                cp(a, 1 + j, (*chip, c), me).wait_recv()
                d = cp(a, 4 + j, (*chip, c), sib)
                d.start()
                passed.append(d)
        for a in range(n):
            cp(a, 0, sib, me).wait_recv()
            for j, chip in enumerate(chips):
                cp(a, 4 + j, (*chip, 1 - c), me).wait_recv()
        for d in first + passed:
            d.wait_send()
        for d in local:
            d.wait()
        refs[2 * n][...] = jnp.zeros_like(refs[2 * n])

    return pl.pallas_call(
        body, name=name,
        out_shape=tuple(jax.ShapeDtypeStruct((NDEV,) + s.shape, s.dtype) for s in shards) + (jax.ShapeDtypeStruct((8, LANE), F32),),
        in_specs=[_ANY] * n, out_specs=tuple([_ANY] * n) + (pl.BlockSpec(memory_space=pltpu.VMEM),),
        scratch_shapes=[pltpu.SemaphoreType.DMA((n, 7)), pltpu.SemaphoreType.DMA((n, 7)), pltpu.SemaphoreType.DMA((n,))],
    )(*shards)


def _pair_exchange(grads, name):
    n = len(grads)

    def body(*refs):
        ins, outs = refs[:n], refs[n:2 * n]
        send, recv = refs[2 * n:]
        x, y, c, _ = _mesh_pos()
        big = [pltpu.make_async_remote_copy(src_ref=ins[a].at[:, 1 - c], dst_ref=outs[a], send_sem=send.at[a], recv_sem=recv.at[a],
                                            device_id=(x, y, 1 - c), device_id_type=MESH) for a in range(n)]
        for d in big:
            d.start()
        for d in big:
            d.wait_recv()
        for d in big:
            d.wait_send()

    return pl.pallas_call(
        body, name=name, out_shape=tuple(jax.ShapeDtypeStruct((4,) + g.shape[2:], g.dtype) for g in grads),
        in_specs=[_ANY] * n, out_specs=tuple([_ANY] * n),
        scratch_shapes=[pltpu.SemaphoreType.DMA((n,)), pltpu.SemaphoreType.DMA((n,))],
    )(*grads)


def _gather_small(pack, name="gather_small"):
    def body(pk, pk_all, psend, precv, loc):
        x, y, c, chips = _mesh_pos()
        me_slot = 4 * x + 2 * y + c
        sib = (x, y, 1 - c)
        own = pltpu.make_async_copy(pk, pk_all.at[me_slot], loc)
        own.start()
        peers = [sib] + [(*chip, c) for chip in chips] + [(*chip, 1 - c) for chip in chips]
        small = [pltpu.make_async_remote_copy(src_ref=pk, dst_ref=pk_all.at[me_slot], send_sem=psend.at[k], recv_sem=precv.at[k],
                                              device_id=p, device_id_type=MESH) for k, p in enumerate(peers)]
        for d in small:
            d.start()
        for k, p in enumerate(peers):
            pltpu.make_async_remote_copy(src_ref=pk, dst_ref=pk_all.at[4 * p[0] + 2 * p[1] + p[2]], send_sem=psend.at[k],
                                         recv_sem=precv.at[k], device_id=p, device_id_type=MESH).wait_recv()
        for d in small:
            d.wait_send()
        own.wait()

    return pl.pallas_call(
        body, name=name, out_shape=jax.ShapeDtypeStruct((NDEV,) + pack.shape, pack.dtype), in_specs=[_ANY], out_specs=_ANY,
        scratch_shapes=[pltpu.SemaphoreType.DMA((7,)), pltpu.SemaphoreType.DMA((7,)), pltpu.SemaphoreType.DMA(())],
    )(pack)


def _main_row(g):
    return g if g < C_LR else g - RANK


def _window_pieces(lo, hi):
    out = []
    for a, b, where in ((lo, min(hi, C_LR), "main"), (max(lo, C_LR), min(hi, C_LR + RANK), "lr"), (max(lo, C_LR + RANK), hi, "main")):
        if a < b:
            out.append((a, b, where, _main_row(a) if where == "main" else a - C_LR))
    return out


def _assemble_w_in(windows, name="assemble_w_in"):
    edges = NDEV - 1

    def body(b_ref, main_ref, lr_ref, buf, ebuf, in_sems, out_sems, esems):
        def load(k):
            return pltpu.make_async_copy(b_ref.at[k], buf.at[k % 2], in_sems.at[k % 2])

        lr_ref[RANK:, :] = jnp.zeros((LANE - RANK, D), BF16)
        load(0).start()
        pending, edge_out = [], []
        for k in range(NDEV):
            s = k % 2
            load(k).wait()
            if k:
                ebuf[k - 1] = buf[1 - s, WSTEP:WWIN, :] + buf[s, 0:16, :]
                edge_out.append(pltpu.make_async_copy(ebuf.at[k - 1], main_ref.at[pl.ds(_main_row(WSTEP * k), 16)], esems.at[k - 1]))
                edge_out[-1].start()
                for d in pending:
                    d.wait()
            if k + 1 < NDEV:
                load(k + 1).start()
            pending = []
            lo = WSTEP * k + (16 if k else 0)
            hi = WSTEP * k + (WWIN if k == NDEV - 1 else WSTEP)
            for a, b, where, dst in _window_pieces(lo, hi):
                if where == "lr":
                    lr_ref[dst:dst + b - a, :] = buf[s, a - WSTEP * k:b - WSTEP * k, :]
                else:
                    pending.append(pltpu.make_async_copy(buf.at[s, pl.ds(a - WSTEP * k, b - a)], main_ref.at[pl.ds(dst, b - a)],
                                                         out_sems.at[2 * s + len(pending)]))
                    pending[-1].start()
        for d in pending + edge_out:
            d.wait()

    return pl.pallas_call(
        body, name=name,
        out_shape=(jax.ShapeDtypeStruct((NMAIN, D), BF16), jax.ShapeDtypeStruct((LANE, D), BF16)),
        in_specs=[_ANY], out_specs=(_ANY, pl.BlockSpec(memory_space=pltpu.VMEM)),
        scratch_shapes=[pltpu.VMEM((2, WWIN, D), BF16), pltpu.VMEM((edges, 16, D), BF16), pltpu.SemaphoreType.DMA((2,)),
                        pltpu.SemaphoreType.DMA((4,)), pltpu.SemaphoreType.DMA((edges,))],
        compiler_params=pltpu.CompilerParams(vmem_limit_bytes=VMEM_LIMIT),
    )(windows)


def _disassemble_w_in(d_main, d_lr, name="disassemble_w_in"):
    def body(main_ref, lr_ref, g_ref, buf, in_sems, out_sems):
        def loads(k):
            s, out = k % 2, []
            for a, b, where, src0 in _window_pieces(WSTEP * k, WSTEP * k + WWIN):
                if where == "main":
                    out.append(pltpu.make_async_copy(main_ref.at[pl.ds(src0, b - a)], buf.at[s, pl.ds(a - WSTEP * k, b - a)],
                                                     in_sems.at[2 * s + len(out)]))
            return out

        def store(k):
            return pltpu.make_async_copy(buf.at[k % 2], g_ref.at[k], out_sems.at[k % 2])

        for d in loads(0):
            d.start()
        for k in range(NDEV):
            for d in loads(k):
                d.wait()
            for a, b, where, src0 in _window_pieces(WSTEP * k, WSTEP * k + WWIN):
                if where == "lr":
                    buf[k % 2, a - WSTEP * k:b - WSTEP * k, :] = lr_ref[src0:src0 + b - a, :]
            if k:
                store(k - 1).wait()
            if k + 1 < NDEV:
                for d in loads(k + 1):
                    d.start()
            store(k).start()
        store(NDEV - 1).wait()

    return pl.pallas_call(
        body, name=name, out_shape=jax.ShapeDtypeStruct((NDEV, WWIN, D), BF16),
        in_specs=[_ANY, pl.BlockSpec(memory_space=pltpu.VMEM)], out_specs=_ANY,
        scratch_shapes=[pltpu.VMEM((2, WWIN, D), BF16), pltpu.SemaphoreType.DMA((4,)), pltpu.SemaphoreType.DMA((2,))],
        compiler_params=pltpu.CompilerParams(vmem_limit_bytes=VMEM_LIMIT),
    )(d_main, d_lr)


_HBM = pl.BlockSpec(memory_space=pltpu.HBM)
_SEM = pl.BlockSpec(memory_space=pltpu.SEMAPHORE)
_VMEM = pl.BlockSpec(memory_space=pltpu.VMEM)
_SIDE = pltpu.CompilerParams(has_side_effects=pltpu.SideEffectType.DATAFLOW_SIDE_EFFECTING)
_TOKEN = jax.ShapeDtypeStruct((8, LANE), F32)


def _hbm(a):
    return pltpu.with_memory_space_constraint(a, pltpu.HBM)


def _hbm_like(arrs):
    return tuple(pltpu.HBM(a.shape, a.dtype) for a in arrs)


def _tie(x, token):
    return x + token[0, 0].astype(x.dtype)


def _chip_copies(ins, lands, send, recv):
    x, y, c, chips = _mesh_pos()
    return [pltpu.make_async_remote_copy(src_ref=ins[a].at[2 * chip[0] + chip[1]], dst_ref=lands[a].at[j], send_sem=send.at[3 * a + j],
                                         recv_sem=recv.at[3 * a + j], device_id=(*chip, c), device_id_type=MESH)
            for a in range(len(ins)) for j, chip in enumerate(chips)]


def _chip_start(psums, name):
    n = len(psums)
    lands = [lax.empty((3,) + p.shape[1:], p.dtype) for p in psums]

    def body(*refs):
        for d in _chip_copies(refs[:n], refs[n:2 * n], refs[2 * n], refs[2 * n + 1]):
            d.start()
        refs[-1][...] = jnp.zeros_like(refs[-1])

    sems = pltpu.SemaphoreType.DMA((3 * n,))
    out = pl.pallas_call(
        body, name=name, out_shape=(sems, sems) + _hbm_like(psums) + _hbm_like(lands) + (_TOKEN,),
        in_specs=[_HBM] * (2 * n), out_specs=(_SEM, _SEM) + (_HBM,) * (2 * n) + (_VMEM,),
        input_output_aliases={i: 2 + i for i in range(2 * n)}, compiler_params=_SIDE,
    )(*[_hbm(a) for a in list(psums) + lands])
    return out[0], out[1], list(out[2:2 + n]), list(out[2 + n:2 + 2 * n]), out[-1]


def _chip_wait(send, recv, psums, lands, after, name):
    n = len(psums)

    def body(*refs):
        for d in _chip_copies(refs[:n], refs[n:2 * n], refs[2 * n], refs[2 * n + 1]):
            d.wait_send()
            d.wait_recv()

    out = pl.pallas_call(
        body, name=name, out_shape=_hbm_like(psums) + _hbm_like(lands),
        in_specs=[_HBM] * (2 * n) + [_SEM, _SEM, _ANY], out_specs=(_HBM,) * (2 * n),
        input_output_aliases={i: i for i in range(2 * n)}, compiler_params=_SIDE,
    )(*psums, *lands, send, recv, after)
    return list(out[:n]), list(out[n:])


def _slot(chip, c):
    return 4 * chip[0] + 2 * chip[1] + c


def _gather_start(shards, dev, name):
    n = len(shards)
    lands = [lax.dynamic_update_slice(lax.empty((NDEV,) + s.shape, s.dtype), s[None], (dev,) + (0,) * s.ndim) for s in shards]

    def body(*refs):
        src, land, send, recv = refs[:n], refs[n:2 * n], refs[2 * n], refs[2 * n + 1]
        x, y, c, chips = _mesh_pos()
        for a in range(n):
            for k, to in enumerate([(x, y, 1 - c)] + [(*chip, c) for chip in chips]):
                pltpu.make_async_remote_copy(src_ref=src[a], dst_ref=land[a].at[_slot((x, y), c)], send_sem=send.at[4 * a + k],
                                             recv_sem=recv.at[4 * a + k], device_id=to, device_id_type=MESH).start()
        refs[-1][...] = jnp.zeros_like(refs[-1])

    sems = pltpu.SemaphoreType.DMA((4 * n,))
    out = pl.pallas_call(
        body, name=name, out_shape=(sems, sems) + _hbm_like(shards) + _hbm_like(lands) + (_TOKEN,),
        in_specs=[_HBM] * (2 * n), out_specs=(_SEM, _SEM) + (_HBM,) * (2 * n) + (_VMEM,),
        input_output_aliases={i: 2 + i for i in range(2 * n)}, compiler_params=_SIDE,
    )(*[_hbm(a) for a in list(shards) + lands])
    return out[0], out[1], list(out[2:2 + n]), list(out[2 + n:2 + 2 * n]), out[-1]


def _gather_pass(lands, recv, after, name, first=0):
    n = len(lands)

    def body(*refs):
        land, recv1 = refs[:n], refs[n]
        send2, recv2 = refs[n + 2], refs[n + 3]
        x, y, c, chips = _mesh_pos()
        for a in range(n):
            for j, chip in enumerate(chips):
                blk = land[a].at[_slot(chip, c)]
                pltpu.make_async_remote_copy(src_ref=blk, dst_ref=blk, send_sem=send2.at[3 * a + j], recv_sem=recv1.at[4 * (first + a) + 1 + j],
                                             device_id=(*chip, c), device_id_type=MESH).wait_recv()
                pltpu.make_async_remote_copy(src_ref=blk, dst_ref=blk, send_sem=send2.at[3 * a + j], recv_sem=recv2.at[3 * a + j],
                                             device_id=(x, y, 1 - c), device_id_type=MESH).start()
        refs[-1][...] = jnp.zeros_like(refs[-1])

    sems = pltpu.SemaphoreType.DMA((3 * n,))
    out = pl.pallas_call(
        body, name=name, out_shape=(sems, sems) + _hbm_like(lands) + (_TOKEN,),
        in_specs=[_HBM] * n + [_SEM, _ANY], out_specs=(_SEM, _SEM) + (_HBM,) * n + (_VMEM,),
        input_output_aliases={i: 2 + i for i in range(n)}, compiler_params=_SIDE,
    )(*lands, recv, after)
    return out[0], out[1], list(out[2:2 + n]), out[-1]


def _gather_wait(shards, lands, send, recv, send2, recv2, after, name, first=0):
    n = len(lands)

    def body(*refs):
        src, land = refs[:n], refs[n:2 * n]
        send1, recv1, snd2, rcv2 = refs[2 * n:2 * n + 4]
        x, y, c, chips = _mesh_pos()
        sib = (x, y, 1 - c)
        for a in range(n):
            for k in range(4):
                pltpu.make_async_remote_copy(src_ref=src[a], dst_ref=land[a].at[_slot((x, y), c)], send_sem=send1.at[4 * (first + a) + k],
                                             recv_sem=recv1.at[4 * (first + a) + k], device_id=sib, device_id_type=MESH).wait_send()
            blk = land[a].at[_slot((x, y), 1 - c)]
            pltpu.make_async_remote_copy(src_ref=blk, dst_ref=blk, send_sem=send1.at[4 * (first + a)], recv_sem=recv1.at[4 * (first + a)],
                                         device_id=sib, device_id_type=MESH).wait_recv()
            for j, chip in enumerate(chips):
                mine, theirs = land[a].at[_slot(chip, c)], land[a].at[_slot(chip, 1 - c)]
                pltpu.make_async_remote_copy(src_ref=mine, dst_ref=mine, send_sem=snd2.at[3 * a + j], recv_sem=rcv2.at[3 * a + j],
                                             device_id=sib, device_id_type=MESH).wait_send()
                pltpu.make_async_remote_copy(src_ref=theirs, dst_ref=theirs, send_sem=snd2.at[3 * a + j], recv_sem=rcv2.at[3 * a + j],
                                             device_id=sib, device_id_type=MESH).wait_recv()

    out = pl.pallas_call(
        body, name=name, out_shape=_hbm_like(shards) + _hbm_like(lands),
        in_specs=[_HBM] * (2 * n) + [_SEM] * 4 + [_ANY], out_specs=(_HBM,) * (2 * n),
        input_output_aliases={i: i for i in range(2 * n)}, compiler_params=_SIDE,
    )(*shards, *lands, send, recv, send2, recv2, after)
    return list(out[n:])


def _pad_to(v, n):
    return jnp.pad(v, [(0, 0)] * (v.ndim - 1) + [(0, n - v.shape[-1])])


def _pack_small(n1, gb, sk, gn, n2, fn, extra=None):
    parts = [n1.reshape(-1), gb.reshape(-1), sk.reshape(-1), gn.reshape(-1), n2.reshape(-1), fn.reshape(-1)]
    flat = jnp.concatenate(parts + ([extra.reshape(-1)] if extra is not None else []))
    return _pad_to(flat, SMALL_N).reshape(SMALL_ROWS, LANE)


def _unpack_small(p):
    f = p.reshape(-1)
    return (f[S_N1:S_GB].reshape(1, D), f[S_GB:S_SK].reshape(1, GH * DK), f[S_SK:S_GN].reshape(1, NQ), f[S_GN:S_N2].reshape(1, DV),
            f[S_N2:S_FN].reshape(1, D), f[S_FN:S_LOSS].reshape(D))


class _NoComm:
    def __init__(self, wo, wg_all, wu_all, wd_all):
        self.rest = (wo, wg_all, wu_all, wd_all)

    def mixed(self, gla_o, gla_norm_w):
        return gla_norm_w

    def w_out(self, merged, norm2_w):
        return self.rest[0], norm2_w

    def w_up(self, v2):
        return self.rest[1], self.rest[2]

    def w_down(self, ff):
        return self.rest[3]

    def ffn_grads(self, d_wg, d_wu, d_wd, norm2_w):
        self.ffn = (d_wg, d_wu, d_wd)
        return norm2_w

    def in_grads(self, d_wmain, d_wlr, d_wo, w_lr):
        self.inw = (d_wmain, d_wlr, d_wo)
        return w_lr


class _Comm:
    def __init__(self, rest_shards, dev, c_idx):
        self.c_idx = c_idx
        self.send, self.recv, self.shards, self.lands, self.token = _gather_start(rest_shards, dev, "gather_rest_start")

    def _pass(self, lo, hi, after, tag):
        send2, recv2, lands, token = _gather_pass(self.lands[lo:hi], self.recv, after, "gather_pass_" + tag, first=lo)
        self.passed = (lo, hi, send2, recv2, lands)
        return token

    def _wait(self, after, tag):
        lo, hi, send2, recv2, lands = self.passed
        return _gather_wait(self.shards[lo:hi], lands, self.send, self.recv, send2, recv2, after, "gather_wait_" + tag, first=lo)

    def mixed(self, gla_o, gla_norm_w):
        return _tie(gla_norm_w, self._pass(0, 1, gla_o, "out"))

    def w_out(self, merged, norm2_w):
        (wo_all,) = self._wait(merged, "out")
        return wo_all.reshape(D, D), _tie(norm2_w, self._pass(1, 3, merged, "up"))

    def w_up(self, v2):
        wg_all, wu_all = self._wait(v2, "up")
        self._pass(3, 4, v2, "down")
        return wg_all, wu_all

    def w_down(self, ff):
        return self._wait(ff, "down")[0]

    def _reduce(self, tag, names, grads, rows):
        recv1 = _pair_exchange(grads, "reduce_pair_" + tag)
        psums = [_pair_add(g, r, self.c_idx, "pair_add_" + nm, tr) for g, r, nm, tr in zip(grads, recv1, names, rows)]
        *flight, token = _chip_start(psums, "reduce_chips_start_" + tag)
        return dict(tag=tag, names=names, rows=rows, flight=flight), token

    def ffn_grads(self, d_wg, d_wu, d_wd, norm2_w):
        self.ffn, token = self._reduce("ffn", ["w_ffn_gate", "w_ffn_up", "w_ffn_down"],
                                       [d_wg.reshape(4, 2, D, FS), d_wu.reshape(4, 2, D, FS), d_wd.reshape(4, 2, FS, D)], [512, 512, 176])
        return _tie(norm2_w, token)

    def in_grads(self, d_wmain, d_wlr, d_wo, w_lr):
        d_win = _disassemble_w_in(d_wmain, d_wlr).reshape(4, 2, WWIN, D)
        self.inw, token = self._reduce("in", ["w_in", "w_out"], [d_win, d_wo.reshape(4, 2, D // NDEV, D)], [808, 256])
        return _tie(w_lr, token)


def _local_step(xs, tgt, norm1_w, gla_gate_b, attn_sinks, gla_norm_w, norm2_w, fnw, w_main, w_lr, w2p, comm):
    u = _rmsnorm_fwd(xs, norm1_w, "norm1_fwd")
    proj = _mm(u, w_main, tb=True, tm=1024, tn=640, tk=D, name="in_proj")
    plr = _mm(u, w_lr, tb=True, tm=1024, tn=LANE, tk=D, name="in_proj_lr")
    attn_o = _attn_fwd(proj, attn_sinks)
    gla_o, states = _gla_fwd(proj, plr, w2p, gla_gate_b)
    merged = _merge_fwd(attn_o, gla_o, proj, comm.mixed(gla_o, gla_norm_w))
    wo, norm2_w = comm.w_out(merged, norm2_w)
    h1 = _mm(merged, wo, tm=1024, tn=512, tk=D, res=xs, name="out_proj")
    v2 = _rmsnorm_fwd(h1, norm2_w, "norm2_fwd")
    wg_all, wu_all = comm.w_up(v2)
    fa, fb, ff = _ffn_up(v2, wg_all, wu_all)
    wd_all = comm.w_down(ff)
    h2 = _ffn_down(ff, wd_all, h1)
    dh2, dh2b, d_fnw, loss_part = _loss_head(h2, fnw, tgt)

    da, db = _ffn_dact(dh2b, wd_all, fa, fb)
    d_wd = _ffn_dwd(ff, dh2b)
    d_wg, d_wu = _ffn_dwgu(v2, da, db)
    norm2_w = comm.ffn_grads(d_wg, d_wu, d_wd, norm2_w)
    dv2 = _ffn_dv2(da, db, wg_all, wu_all)
    dh1, dh1b, d_n2 = _rmsnorm_bwd(dv2, h1, norm2_w, dh2, "norm2_bwd")
    dmerged = _mm(dh1b, wo, tb=True, tm=1024, tn=512, tk=D, name="out_proj_dx")
    d_wo = _mm(merged, dh1b, ta=True, tm=1024, tn=512, tk=xs.shape[0], out_dtype=BF16, name="out_proj_dw")
    d_attn, d_gla, d_gates, d_gnw = _merge_bwd(dmerged, attn_o, gla_o, proj, gla_norm_w)
    d_q, d_kv, d_sinks = _attn_bwd(proj, attn_sinks, attn_o, d_attn)
    d_gqk, d_gv, d_plr, d_w2p, d_gb = _gla_bwd(proj, plr, w2p, gla_gate_b, states, d_gla)
    dproj = jnp.concatenate([d_q, d_kv, d_gqk, d_gv, d_gates], axis=1)
    d_wmain = _mm(dproj, u, ta=True, tm=640, tn=1024, tk=xs.shape[0], out_dtype=BF16, name="in_proj_dw")
    d_wlr = _mm(d_plr, u, ta=True, tm=LANE, tn=1024, tk=xs.shape[0], out_dtype=BF16, name="in_proj_lr_dw")
    du_lr = _mm(d_plr, comm.in_grads(d_wmain, d_wlr, d_wo, w_lr), tm=1024, tn=1024, tk=LANE, name="in_proj_lr_dx")
    du = _mm(dproj, w_main, tm=1024, tn=1024, tk=1280, res=du_lr, name="in_proj_dx")
    dx, _, d_n1 = _rmsnorm_bwd(du, xs, norm1_w, dh1, "norm1_bwd")
    return dx, loss_part, d_w2p, d_gb, d_sinks, d_gnw, d_n1, d_n2, d_fnw


def kernel(x, norm1_w, w_in, gla_gate_w2, gla_gate_b, attn_sinks, gla_norm_w, w_out, norm2_w, w_ffn_gate, w_ffn_up, w_ffn_down, final_norm_w, loss_target, m_norm1_w, m_w_in, m_gla_gate_w2, m_gla_gate_b, m_attn_sinks, m_gla_norm_w, m_w_out, m_norm2_w, m_w_ffn_gate, m_w_ffn_up, m_w_ffn_down, m_final_norm_w, v_norm1_w, v_w_in, v_gla_gate_w2, v_gla_gate_b, v_attn_sinks, v_gla_norm_w, v_w_out, v_norm2_w, v_w_ffn_gate, v_w_ffn_up, v_w_ffn_down, v_final_norm_w):
    xs, tgt = x[0], loss_target[0]
    fnw = final_norm_w.reshape(1, D)
    c_idx = lax.axis_index("c").astype(jnp.int32).reshape(1)
    dev = 4 * lax.axis_index("x") + 2 * lax.axis_index("y") + lax.axis_index("c")

    chip_idx = (2 * lax.axis_index("x") + lax.axis_index("y")).astype(jnp.int32).reshape(1)

    shift = (WS - WSTEP) * dev
    window = lax.dynamic_update_slice(jnp.zeros((WWIN, D), BF16), jnp.transpose(w_in[0]).astype(BF16), (shift, 0))
    win_all, w2_all, tok = _all_gather([window, gla_gate_w2[0]], name="gather_w_in")
    rest = [(w[0] + tok[0, 0]).astype(BF16) for w in (w_out, w_ffn_gate, w_ffn_up, w_ffn_down)]
    comm = _Comm(rest, dev, c_idx)
    w_main, w_lr = _assemble_w_in(win_all)
    w2p =jnp.pad(jnp.transpose(w2_all, (1, 0, 2)).reshape(RANK, GH * DK), ((0, LANE - RANK), (0, 0)))

    dx, loss_part, d_w2p, d_gb, d_sinks, d_gnw, d_n1, d_n2, d_fnw = _local_step(
        xs, tgt, _tie(norm1_w, comm.token), gla_gate_b, attn_sinks, gla_norm_w, norm2_w, fnw, w_main, w_lr, w2p, comm)

    pack = jnp.concatenate([_pack_small(d_n1, d_gb, d_sinks, d_gnw, d_n2, d_fnw, loss_part),
                            d_w2p[:RANK].reshape(GW2_ROWS, LANE)], axis=0)
    small = _sum_devices(_gather_small(pack))

    big = {}
    after = dx
    for grp in (comm.ffn, comm.inw):
        psums, parts = _chip_wait(*grp["flight"], after, "reduce_chips_wait_" + grp["tag"])
        for nm, ps, pt, tr in zip(grp["names"], psums, parts, grp["rows"]):
            w, m, v = {"w_in": (w_in, m_w_in, v_w_in), "w_out": (w_out, m_w_out, v_w_out), "w_ffn_gate": (w_ffn_gate, m_w_ffn_gate, v_w_ffn_gate),
                       "w_ffn_up": (w_ffn_up, m_w_ffn_up, v_w_ffn_up), "w_ffn_down": (w_ffn_down, m_w_ffn_down, v_w_ffn_down)}[nm]
            if nm == "w_in":
                g_win = _sum_parts(ps, pt, chip_idx, "sum_w_in", tr, 1024)
                rows3 = lambda t: jnp.transpose(t[0]).reshape(WS, D // LANE, LANE)
                g3 = lax.dynamic_slice(g_win, (shift, 0), (WS, D)).reshape(WS, D // LANE, LANE)
                out3 = (g3,) + tuple(_adamw_rows(rows3(w), rows3(m), rows3(v), g3, "adamw_w_in", 178))
                big[nm] = [jnp.transpose(t.reshape(WS, D))[None] for t in out3]
            else:
                big[nm] = [t[None] for t in _adamw(w[0], m[0], v[0], ps, pt, chip_idx, "adamw_" + nm, tr)]
            after = big[nm][0]
    g_small = small[:SMALL_ROWS]
    sm = _adamw_plain(_pack_small(norm1_w, gla_gate_b, attn_sinks, gla_norm_w, norm2_w, final_norm_w),
                      _pack_small(m_norm1_w, m_gla_gate_b, m_attn_sinks, m_gla_norm_w, m_norm2_w, m_final_norm_w),
                      _pack_small(v_norm1_w, v_gla_gate_b, v_attn_sinks, v_gla_norm_w, v_norm2_w, v_final_norm_w), g_small, "adamw_small")
    g_w2 = lax.dynamic_slice_in_dim(small[SMALL_ROWS:].reshape(RANK, GH * DK), dev * LANE, LANE, axis=1)
    w2 = [g_w2[None]] + [t[None] for t in _adamw_plain(gla_gate_w2[0], m_gla_gate_w2[0], v_gla_gate_w2[0], g_w2, "adamw_w2")]
    loss = g_small.reshape(-1)[S_LOSS]

    sg, sd, sm2, sv2 = [_unpack_small(t) for t in (g_small,) + tuple(sm)]

    def group(i, s):
        return (s[0], big["w_in"][i], w2[i], s[1], s[2], s[3], big["w_out"][i], s[4], big["w_ffn_gate"][i], big["w_ffn_up"][i],
                big["w_ffn_down"][i], s[5])

    return (loss, dx[None], *group(0, sg), *group(1, sd), *group(2, sm2), *group(3, sv2))
```

```python
import functools

import jax
import jax.numpy as jnp
from jax import lax
from jax.experimental import pallas as pl
from jax.experimental.pallas import tpu as pltpu

F32, BF16 = jnp.float32, jnp.bfloat16
HIGHEST = lax.Precision.HIGHEST

D = 2048
HD, NQ, NKV, GRP, WIN = 64, 32, 4, 8, 128
GH, DK, DV, RANK, GC = 4, 256, 512, 16, 64
FH, NDEV = 5632, 8
FS = FH // NDEV
DIN = 12816
WS = DIN // NDEV
EPS = 1e-6
MASKV = -1e30
LANE = 128

C_AQ, C_AK, C_AV, C_GQ, C_GK, C_GV, C_GR, C_GA, C_GB, NMAIN = 0, 2048, 2304, 2560, 3584, 4608, 6656, 8704, 10752, 12800
C_LR = 6656
WSTEP, WWIN = 1600, 1616

LR, B1, B2, AEPS, WD, STEP = 0.001, 0.9, 0.999, 1e-08, 0.01, 10

S_N1, S_GB, S_SK, S_GN, S_N2, S_FN, S_LOSS, SMALL_N = 0, 2048, 3072, 3104, 3616, 5664, 7712, 8192
SMALL_ROWS = SMALL_N // LANE
GW2_ROWS = RANK * GH * DK // LANE
PACK_ROWS = SMALL_ROWS + GW2_ROWS

MESH = pl.DeviceIdType.MESH


def _dot(a, b, ta=False, tb=False, prec=None):
    dn = (((0,) if ta else (1,), (1,) if tb else (0,)), ((), ()))
    return lax.dot_general(a, b, dn, preferred_element_type=F32, precision=prec)


def _sigmoid(x):
    return 1.0 / (1.0 + jnp.exp(-x))


VMEM_LIMIT = 56 * 1024 * 1024


def _cp(*sem):
    return pltpu.CompilerParams(dimension_semantics=sem, vmem_limit_bytes=VMEM_LIMIT)


def _mm(a, b, *, ta=False, tb=False, tm, tn, tk, out_dtype=F32, res=None, name):
    M, K = (a.shape[1], a.shape[0]) if ta else a.shape
    N = b.shape[0] if tb else b.shape[1]
    tm, tn, tk = min(tm, M), min(tn, N), min(tk, K)
    nk = K // tk
    assert M % tm == 0 and N % tn == 0 and K % tk == 0
    a_spec = pl.BlockSpec((tk, tm), lambda i, j, k: (k, i)) if ta else pl.BlockSpec((tm, tk), lambda i, j, k: (i, k))
    b_spec = pl.BlockSpec((tn, tk), lambda i, j, k: (j, k)) if tb else pl.BlockSpec((tk, tn), lambda i, j, k: (k, j))
    o_spec = pl.BlockSpec((tm, tn), lambda i, j, k: (i, j))
    has_res = res is not None

    def body(*refs):
        a_ref, b_ref = refs[0], refs[1]
        r_ref = refs[2] if has_res else None
        o_ref = refs[3] if has_res else refs[2]
        p = _dot(a_ref[...].astype(BF16), b_ref[...].astype(BF16), ta, tb)
        if nk == 1:
            if has_res:
                p = p + r_ref[...]
            o_ref[...] = p.astype(out_dtype)
        else:
            acc = refs[-1]
            k = pl.program_id(2)

            @pl.when(k == 0)
            def _():
                acc[...] = (p + r_ref[...]) if has_res else p

            @pl.when(k > 0)
            def _():
                acc[...] += p

            @pl.when(k == nk - 1)
            def _():
                o_ref[...] = acc[...].astype(out_dtype)

    return pl.pallas_call(
        body, name=name,
        out_shape=jax.ShapeDtypeStruct((M, N), out_dtype),
        grid=(M // tm, N // tn, nk),
        in_specs=[a_spec, b_spec] + ([o_spec] if has_res else []),
        out_specs=o_spec,
        scratch_shapes=[pltpu.VMEM((tm, tn), F32)] if nk > 1 else [],
        compiler_params=_cp("parallel", "parallel", "arbitrary"),
    )(*((a, b, res) if has_res else (a, b)))


def _rmsnorm_fwd(x, w, name, tm=256):
    Tn = x.shape[0]

    def body(x_ref, w_ref, o_ref):
        xv = x_ref[...]
        r = lax.rsqrt(jnp.mean(xv * xv, axis=1, keepdims=True) + EPS)
        o_ref[...] = (xv * r * w_ref[...]).astype(BF16)

    return pl.pallas_call(
        body, name=name, out_shape=jax.ShapeDtypeStruct((Tn, D), BF16), grid=(Tn // tm,),
        in_specs=[pl.BlockSpec((tm, D), lambda i: (i, 0)), pl.BlockSpec((1, D), lambda i: (0, 0))],
        out_specs=pl.BlockSpec((tm, D), lambda i: (i, 0)), compiler_params=_cp("parallel"),
    )(x, w)


def _rmsnorm_bwd(dy, h, w, res, name, tm=256):
    Tn = h.shape[0]

    def body(dy_ref, h_ref, w_ref, res_ref, dh_ref, dhb_ref, dw_ref):
        hv, dyv = h_ref[...], dy_ref[...]
        r = lax.rsqrt(jnp.mean(hv * hv, axis=1, keepdims=True) + EPS)
        g = dyv * w_ref[...]
        dh = res_ref[...] + r * g - hv * (r * r * r * jnp.mean(g * hv, axis=1, keepdims=True))
        dh_ref[...] = dh
        dhb_ref[...] = dh.astype(BF16)
        part = jnp.sum(dyv * hv * r, axis=0, keepdims=True)

        @pl.when(pl.program_id(0) == 0)
        def _():
            dw_ref[...] = part

        @pl.when(pl.program_id(0) > 0)
        def _():
            dw_ref[...] += part

    row = pl.BlockSpec((tm, D), lambda i: (i, 0))
    vec = pl.BlockSpec((1, D), lambda i: (0, 0))
    return pl.pallas_call(
        body, name=name,
        out_shape=(jax.ShapeDtypeStruct((Tn, D), F32), jax.ShapeDtypeStruct((Tn, D), BF16), jax.ShapeDtypeStruct((1, D), F32)),
        grid=(Tn // tm,), in_specs=[row, row, vec, row], out_specs=(row, row, vec), compiler_params=_cp("arbitrary"),
    )(dy, h, w, res)


def _loss_head(h2, wf, tgt, name="loss_head", tm=256):
    Tn = h2.shape[0]

    def body(h_ref, w_ref, t_ref, dh_ref, dhb_ref, dw_ref, loss_ref):
        hv, wv = h_ref[...], w_ref[...]
        r = lax.rsqrt(jnp.mean(hv * hv, axis=1, keepdims=True) + EPS)
        hn = hv * r
        e = hn * wv - t_ref[...]
        dy = e * (1.0 / D)
        g = dy * wv
        dh = r * g - hv * (r * r * r * jnp.mean(g * hv, axis=1, keepdims=True))
        dh_ref[...] = dh
        dhb_ref[...] = dh.astype(BF16)
        part = jnp.sum(dy * hn, axis=0, keepdims=True)
        lpart = (0.5 / D) * jnp.sum(jnp.sum(e * e, axis=1, keepdims=True), axis=0, keepdims=True)

        @pl.when(pl.program_id(0) == 0)
        def _():
            dw_ref[...] = part
            loss_ref[...] = lpart

        @pl.when(pl.program_id(0) > 0)
        def _():
            dw_ref[...] += part
            loss_ref[...] += lpart

    row = pl.BlockSpec((tm, D), lambda i: (i, 0))
    vec = pl.BlockSpec((1, D), lambda i: (0, 0))
    one = pl.BlockSpec((1, 1), lambda i: (0, 0))
    return pl.pallas_call(
        body, name=name,
        out_shape=(jax.ShapeDtypeStruct((Tn, D), F32), jax.ShapeDtypeStruct((Tn, D), BF16), jax.ShapeDtypeStruct((1, D), F32),
                   jax.ShapeDtypeStruct((1, 1), F32)),
        grid=(Tn // tm,), in_specs=[row, vec, row], out_specs=(row, row, vec, one), compiler_params=_cp("arbitrary"),
    )(h2, wf, tgt)


def _attn_mask(n):
    qi = lax.broadcasted_iota(jnp.int32, (NKV, GRP * WIN, 2 * WIN), 1) % WIN
    ki = lax.broadcasted_iota(jnp.int32, (NKV, GRP * WIN, 2 * WIN), 2)
    rel = qi + WIN - ki
    return (rel >= 0) & (rel < WIN) & ((n > 0) | (ki >= WIN))


def _kv_heads(prev_ref, cur_ref):
    return jnp.stack([jnp.concatenate([prev_ref[:, h * HD:(h + 1) * HD], cur_ref[:, h * HD:(h + 1) * HD]], axis=0) for h in range(NKV)])


def _q_heads(ref):
    return jnp.stack([jnp.concatenate([ref[:, (h * GRP + g) * HD:(h * GRP + g + 1) * HD] for g in range(GRP)], axis=0) for h in range(NKV)])


def _attn_probs(q_ref, kc_ref, kp_ref, sink_ref, mask):
    kk = _kv_heads(kp_ref, kc_ref).astype(BF16)
    qs = _q_heads(q_ref).astype(BF16)
    s = jnp.einsum('hqd,hkd->hqk', qs, kk, preferred_element_type=F32) * (HD ** -0.5)
    s = jnp.where(mask, s, MASKV)
    sink = jnp.stack([jnp.concatenate([jnp.full((WIN, 1), sink_ref[0, h * GRP + g], F32) for g in range(GRP)], axis=0) for h in range(NKV)])
    m = jnp.maximum(jnp.max(s, axis=2, keepdims=True), sink)
    e = jnp.exp(s - m)
    es = jnp.exp(sink - m)
    inv = 1.0 / (jnp.sum(e, axis=2, keepdims=True) + es)
    return e * inv, es * inv, qs, kk


def _attn_specs(nb, last):
    cur = lambda n: jnp.minimum(n, last)
    prev = lambda n: jnp.maximum(jnp.minimum(n, last) - 1, 0)
    return [
        pl.BlockSpec((WIN, NQ * HD), lambda n: (cur(n), C_AQ // (NQ * HD))),
        pl.BlockSpec((WIN, NKV * HD), lambda n: (cur(n), C_AK // (NKV * HD))),
        pl.BlockSpec((WIN, NKV * HD), lambda n: (prev(n), C_AK // (NKV * HD))),
        pl.BlockSpec((WIN, NKV * HD), lambda n: (cur(n), C_AV // (NKV * HD))),
        pl.BlockSpec((WIN, NKV * HD), lambda n: (prev(n), C_AV // (NKV * HD))),
    ]


def _attn_fwd(proj, sinks, name="attn_fwd"):
    Tn = proj.shape[0]
    nb = Tn // WIN

    def body(q_ref, kc_ref, kp_ref, vc_ref, vp_ref, sink_ref, o_ref):
        p, _, _, _ = _attn_probs(q_ref, kc_ref, kp_ref, sink_ref, _attn_mask(pl.program_id(0)))
        o = jnp.einsum('hqk,hkd->hqd', p.astype(BF16), _kv_heads(vp_ref, vc_ref).astype(BF16), preferred_element_type=F32)
        for h in range(NKV):
            for g in range(GRP):
                o_ref[:, (h * GRP + g) * HD:(h * GRP + g + 1) * HD] = o[h, g * WIN:(g + 1) * WIN, :]

    return pl.pallas_call(
        body, name=name, out_shape=jax.ShapeDtypeStruct((Tn, D), F32), grid=(nb,),
        in_specs=_attn_specs(nb, nb - 1) + [pl.BlockSpec(memory_space=pltpu.SMEM)],
        out_specs=pl.BlockSpec((WIN, D), lambda n: (n, 0)), compiler_params=_cp("parallel"),
    )(proj, proj, proj, proj, proj, sinks)


def _attn_bwd(proj, sinks, o, do, name="attn_bwd"):
    Tn = proj.shape[0]
    nb = Tn // WIN
    KW = NKV * HD

    def body(q_ref, kc_ref, kp_ref, vc_ref, vp_ref, o_ref, do_ref, sink_ref, dq_ref, dkv_ref, dsk_ref, carry, cur):
        n = pl.program_id(0)

        @pl.when(n == 0)
        def _():
            carry[...] = jnp.zeros_like(carry)
            dsk_ref[...] = jnp.zeros_like(dsk_ref)

        @pl.when(n < nb)
        def _():
            p, ps, qs, kk = _attn_probs(q_ref, kc_ref, kp_ref, sink_ref, _attn_mask(n))
            vv = _kv_heads(vp_ref, vc_ref).astype(BF16)
            dos = _q_heads(do_ref)
            delta = jnp.sum(dos * _q_heads(o_ref), axis=2, keepdims=True)
            dosb = dos.astype(BF16)
            dp = jnp.einsum('hqd,hkd->hqk', dosb, vv, preferred_element_type=F32)
            ds = (p * (dp - delta) * (HD ** -0.5)).astype(BF16)
            dq = jnp.einsum('hqk,hkd->hqd', ds, kk, preferred_element_type=F32)
            dkk = jnp.einsum('hqk,hqd->hkd', ds, qs, preferred_element_type=F32)
            dvv = jnp.einsum('hqk,hqd->hkd', p.astype(BF16), dosb, preferred_element_type=F32)
            dsk = ps * delta
            for h in range(NKV):
                for g in range(GRP):
                    i = h * GRP + g
                    dq_ref[:, i * HD:(i + 1) * HD] = dq[h, g * WIN:(g + 1) * WIN, :].astype(BF16)
                    dsk_ref[:, i:i + 1] -= jnp.sum(dsk[h, g * WIN:(g + 1) * WIN, :], axis=0, keepdims=True)
                dkv_ref[:, h * HD:(h + 1) * HD] = (carry[:, h * HD:(h + 1) * HD] + dkk[h, :WIN, :]).astype(BF16)
                dkv_ref[:, KW + h * HD:KW + (h + 1) * HD] = (carry[:, KW + h * HD:KW + (h + 1) * HD] + dvv[h, :WIN, :]).astype(BF16)
                cur[:, h * HD:(h + 1) * HD] = dkk[h, WIN:, :]
                cur[:, KW + h * HD:KW + (h + 1) * HD] = dvv[h, WIN:, :]
            carry[...] = cur[...]

        @pl.when(n == nb)
        def _():
            dkv_ref[...] = carry[...].astype(BF16)

    last = nb - 1
    row = pl.BlockSpec((WIN, D), lambda n: (jnp.minimum(n, last), 0))
    return pl.pallas_call(
        body, name=name,
        out_shape=(jax.ShapeDtypeStruct((Tn, D), BF16), jax.ShapeDtypeStruct((Tn, 2 * KW), BF16), jax.ShapeDtypeStruct((1, NQ), F32)),
        grid=(nb + 1,),
        in_specs=_attn_specs(nb, last) + [row, row, pl.BlockSpec(memory_space=pltpu.SMEM)],
        out_specs=(row, pl.BlockSpec((WIN, 2 * KW), lambda n: (jnp.maximum(n - 1, 0), 0)), pl.BlockSpec((1, NQ), lambda n: (0, 0))),
        scratch_shapes=[pltpu.VMEM((WIN, 2 * KW), F32), pltpu.VMEM((WIN, 2 * KW), F32)],
        compiler_params=_cp("arbitrary"),
    )(proj, proj, proj, proj, proj, o, do, sinks)


def _tri(lower):
    r = lax.broadcasted_iota(jnp.int32, (GC, GC), 0)
    c = lax.broadcasted_iota(jnp.int32, (GC, GC), 1)
    return r >= c if lower else r <= c


def _per_head(a):
    return jnp.stack([a[:, h * DK:(h + 1) * DK] for h in range(GH)])


def _all_heads(a):
    return jnp.concatenate([a[h] for h in range(GH)], axis=1)


def _gla_gates(lr, w2_ref, gb_ref):
    logit = _dot(lr, w2_ref[...].astype(BF16)) + gb_ref[...]
    la = (jnp.minimum(logit, 0.0) - jnp.log(1.0 + jnp.exp(-jnp.abs(logit)))) * (1.0 / 16.0)
    g = _dot(_tri(True).astype(F32), la, prec=HIGHEST)
    return logit, g


def _bmm(spec, a, b):
    return jnp.einsum(spec, a, b, preferred_element_type=F32)


def _gla_specs(nc, rev):
    idx = (lambda n: nc - 1 - n) if rev else (lambda n: n)
    half = 2 * DK
    return (
        [pl.BlockSpec((GC, half), lambda n, j=j: (idx(n), C_GQ // half + j)) for j in range(2)]
        + [pl.BlockSpec((GC, half), lambda n, j=j: (idx(n), C_GK // half + j)) for j in range(2)]
        + [pl.BlockSpec((GC, DV), lambda n, h=h: (idx(n), C_GV // DV + h)) for h in range(GH)]
        + [pl.BlockSpec((GC, LANE), lambda n: (idx(n), 0)), pl.BlockSpec((LANE, GH * DK), lambda n: (0, 0)),
           pl.BlockSpec((1, GH * DK), lambda n: (0, 0))])


def _gla_heads(refs):
    return (lambda h: refs[h // 2][:, (h % 2) * DK:(h % 2 + 1) * DK], lambda h: refs[2 + h // 2][:, (h % 2) * DK:(h % 2 + 1) * DK],
            lambda h: refs[4 + h][...])


def _gla_fwd(proj, plr, w2p, gb, name="gla_fwd"):
    Tn = proj.shape[0]
    nc = Tn // GC

    def body(*refs):
        qh, kh, vh = _gla_heads(refs)
        lr_ref, w2_ref, gb_ref, o_ref, st_ref, S = refs[8:]

        @pl.when(pl.program_id(0) == 0)
        def _():
            S[...] = jnp.zeros_like(S)

        heads = lambda f: jnp.stack([f(h) for h in range(GH)])
        _, g_all = _gla_gates(lr_ref[...].astype(BF16), w2_ref, gb_ref)
        g = _per_head(g_all)
        gl = g[:, GC - 1:GC, :]
        k = heads(kh)
        v = heads(vh).astype(BF16)
        qd = (heads(qh) * (DK ** -0.5) * jnp.exp(g)).astype(BF16)
        ki = (k * jnp.exp(-g)).astype(BF16)
        ke = (k * jnp.exp(gl - g)).astype(BF16)
        att = jnp.where(_tri(True)[None], _bmm('hid,hjd->hij', qd, ki), 0.0).astype(BF16)
        sp = S[...]
        st_ref[0] = sp
        o = _bmm('hij,hjv->hiv', att, v) + _bmm('hid,hvd->hiv', qd, sp.astype(BF16))
        for h in range(GH):
            o_ref[:, h * DV:(h + 1) * DV] = o[h]
        S[...] = sp * jnp.exp(gl) + _bmm('hjv,hjd->hvd', v, ke)

    return pl.pallas_call(
        body, name=name,
        out_shape=(jax.ShapeDtypeStruct((Tn, GH * DV), F32), jax.ShapeDtypeStruct((nc, GH, DV, DK), F32)),
        grid=(nc,), in_specs=_gla_specs(nc, False),
        out_specs=(pl.BlockSpec((GC, GH * DV), lambda n: (n, 0)), pl.BlockSpec((1, GH, DV, DK), lambda n: (n, 0, 0, 0))),
        scratch_shapes=[pltpu.VMEM((GH, DV, DK), F32)], compiler_params=_cp("arbitrary"),
    )(*([proj] * 8), plr, w2p, gb)


def _gla_bwd(proj, plr, w2p, gb, states, do, name="gla_bwd"):
    Tn = proj.shape[0]
    nc = Tn // GC

    def body(*refs):
        qh, kh, vh = _gla_heads(refs)
        lr_ref, w2_ref, gb_ref, st_ref, do_ref, dqk_ref, dv_ref, dlr_ref, dw2_ref, dgb_ref, dS = refs[8:]

        @pl.when(pl.program_id(0) == 0)
        def _():
            dS[...] = jnp.zeros_like(dS)
            dw2_ref[...] = jnp.zeros_like(dw2_ref)
            dgb_ref[...] = jnp.zeros_like(dgb_ref)

        heads = lambda f: jnp.stack([f(h) for h in range(GH)])
        lr = lr_ref[...].astype(BF16)
        causal = _tri(True)[None]
        last_row = lax.broadcasted_iota(jnp.int32, (GH, GC, DK), 1) == GC - 1
        logit, g_all = _gla_gates(lr, w2_ref, gb_ref)
        g = _per_head(g_all)
        gl = g[:, GC - 1:GC, :]
        egl = jnp.exp(gl)
        eg, eng, ege = jnp.exp(g), jnp.exp(-g), jnp.exp(gl - g)
        k = heads(kh)
        v = heads(vh).astype(BF16)
        dob = heads(lambda h: do_ref[:, h * DV:(h + 1) * DV]).astype(BF16)
        qd = heads(qh) * (DK ** -0.5) * eg
        ki = k * eng
        ke = k * ege
        qdb, kib, keb = qd.astype(BF16), ki.astype(BF16), ke.astype(BF16)
        att = jnp.where(causal, _bmm('hid,hjd->hij', qdb, kib), 0.0).astype(BF16)
        datt = jnp.where(causal, _bmm('hiv,hjv->hij', dob, v), 0.0).astype(BF16)
        sp = st_ref[0]
        dsn = dS[...]
        dsnb = dsn.astype(BF16)
        dv = (_bmm('hij,hiv->hjv', att, dob) + _bmm('hjd,hvd->hjv', keb, dsnb)).astype(BF16)
        dqd = _bmm('hij,hjd->hid', datt, kib) + _bmm('hiv,hvd->hid', dob, sp.astype(BF16))
        dki = _bmm('hij,hid->hjd', datt, qdb)
        dke = _bmm('hjv,hvd->hjd', v, dsnb)
        ddec = jnp.sum(dsn * sp, axis=1, keepdims=True)
        dS[...] = dsn * egl + _bmm('hiv,hid->hvd', dob, qdb)
        dke_ke = dke * ke
        dgl = jnp.sum(dke_ke, axis=1, keepdims=True) + ddec * egl
        dg = dqd * qd - dki * ki - dke_ke + jnp.where(last_row, dgl, 0.0)
        dq = (dqd * ((DK ** -0.5) * eg)).astype(BF16)
        dk = (dki * eng + dke * ege).astype(BF16)
        for h in range(GH):
            dv_ref[:, h * DV:(h + 1) * DV] = dv[h]
            dqk_ref[:, h * DK:(h + 1) * DK] = dq[h]
            dqk_ref[:, GH * DK + h * DK:GH * DK + (h + 1) * DK] = dk[h]
        dla = _dot(_tri(False).astype(F32), _all_heads(dg), prec=HIGHEST)
        dlogit = dla * (1.0 / 16.0) * _sigmoid(-logit)
        dlb = dlogit.astype(BF16)
        dlr_ref[...] = _dot(dlb, w2_ref[...].astype(BF16), tb=True).astype(BF16)
        dw2_ref[...] += _dot(lr, dlb, ta=True)
        dgb_ref[...] += jnp.sum(dlogit, axis=0, keepdims=True)

    rev = lambda n: nc - 1 - n
    row = pl.BlockSpec((GC, GH * DV), lambda n: (rev(n), 0))
    return pl.pallas_call(
        body, name=name,
        out_shape=(jax.ShapeDtypeStruct((Tn, 2 * GH * DK), BF16), jax.ShapeDtypeStruct((Tn, GH * DV), BF16),
                   jax.ShapeDtypeStruct((Tn, LANE), BF16), jax.ShapeDtypeStruct((LANE, GH * DK), F32),
                   jax.ShapeDtypeStruct((1, GH * DK), F32)),
        grid=(nc,),
        in_specs=_gla_specs(nc, True) + [pl.BlockSpec((1, GH, DV, DK), lambda n: (rev(n), 0, 0, 0)), row],
        out_specs=(row, row, pl.BlockSpec((GC, LANE), lambda n: (rev(n), 0)), pl.BlockSpec((LANE, GH * DK), lambda n: (0, 0)),
                   pl.BlockSpec((1, GH * DK), lambda n: (0, 0))),
        scratch_shapes=[pltpu.VMEM((GH, DV, DK), F32)], compiler_params=_cp("arbitrary"),
    )(*([proj] * 8), plr, w2p, gb, states, do)


def _merge_specs(tm):
    row = pl.BlockSpec((tm, D), lambda i: (i, 0))
    gates = [pl.BlockSpec((tm, DV), lambda i, j=c // DV + h: (i, j)) for c in (C_GR, C_GA, C_GB) for h in range(GH)]
    return row, gates, pl.BlockSpec((1, DV), lambda i: (0, 0))


def _merge_fwd(a, go, proj, gnw, name="merge_fwd", tm=256):
    Tn = a.shape[0]

    def body(a_ref, go_ref, *rest):
        gates, w_ref, m_ref = rest[:3 * GH], rest[3 * GH], rest[3 * GH + 1]
        for h in range(GH):
            sl = slice(h * DV, (h + 1) * DV)
            gov = go_ref[:, sl]
            r = lax.rsqrt(jnp.mean(gov * gov, axis=1, keepdims=True) + EPS)
            gr = gates[h][...]
            g2 = gov * r * w_ref[...] * (gr * _sigmoid(gr))
            m_ref[:, sl] = (_sigmoid(gates[GH + h][...]) * a_ref[:, sl] + _sigmoid(gates[2 * GH + h][...]) * g2).astype(BF16)

    row, gates, vec = _merge_specs(tm)
    return pl.pallas_call(
        body, name=name, out_shape=jax.ShapeDtypeStruct((Tn, D), BF16), grid=(Tn // tm,),
        in_specs=[row, row] + gates + [vec], out_specs=row, compiler_params=_cp("parallel"),
    )(a, go, *([proj] * (3 * GH)), gnw)


def _merge_bwd(dm, a, go, proj, gnw, name="merge_bwd", tm=256):
    Tn = a.shape[0]

    def body(dm_ref, a_ref, go_ref, *rest):
        gates = rest[:3 * GH]
        w_ref, da_ref, dgo_ref, dg_ref, dw_ref = rest[3 * GH:]
        wv = w_ref[...]
        dw = jnp.zeros((1, DV), F32)
        for h in range(GH):
            sl = slice(h * DV, (h + 1) * DV)
            dmv, av, gov, gr = dm_ref[:, sl], a_ref[:, sl], go_ref[:, sl], gates[h][...]
            sa, sb, sg = _sigmoid(gates[GH + h][...]), _sigmoid(gates[2 * GH + h][...]), _sigmoid(gr)
            r = lax.rsqrt(jnp.mean(gov * gov, axis=1, keepdims=True) + EPS)
            gn0 = gov * r
            gn = gn0 * wv
            silu = gr * sg
            dg2 = dmv * sb
            da_ref[:, sl] = dmv * sa
            dg_ref[:, D + h * DV:D + (h + 1) * DV] = (dmv * av * sa * (1.0 - sa)).astype(BF16)
            dg_ref[:, 2 * D + h * DV:2 * D + (h + 1) * DV] = (dg2 * gn * silu * (1.0 - sb)).astype(BF16)
            dg_ref[:, sl] = (dg2 * gn * (sg * (1.0 + gr * (1.0 - sg)))).astype(BF16)
            dgn = dg2 * silu
            dw = dw + jnp.sum(dgn * gn0, axis=0, keepdims=True)
            gg = dgn * wv
            dgo_ref[:, sl] = r * gg - gov * (r * r * r * jnp.mean(gg * gov, axis=1, keepdims=True))

        @pl.when(pl.program_id(0) == 0)
        def _():
            dw_ref[...] = dw

        @pl.when(pl.program_id(0) > 0)
        def _():
            dw_ref[...] += dw

    row, gates, vec = _merge_specs(tm)
    return pl.pallas_call(
        body, name=name,
        out_shape=(jax.ShapeDtypeStruct((Tn, D), F32), jax.ShapeDtypeStruct((Tn, D), F32), jax.ShapeDtypeStruct((Tn, 3 * D), BF16),
                   jax.ShapeDtypeStruct((1, DV), F32)),
        grid=(Tn // tm,), in_specs=[row, row, row] + gates + [vec],
        out_specs=(row, row, pl.BlockSpec((tm, 3 * D), lambda i: (i, 0)), vec), compiler_params=_cp("arbitrary"),
    )(dm, a, go, *([proj] * (3 * GH)), gnw)


def _ffn_up(v2, wgt, wut, name="ffn_up", tm=1024, tn=512):
    Tn = v2.shape[0]
    tm = min(tm, Tn)

    def body(v_ref, wg_ref, wu_ref, a_ref, b_ref, ff_ref):
        vv = v_ref[...]
        a = _dot(vv, wg_ref[...], tb=True)
        b = _dot(vv, wu_ref[...], tb=True)
        a_ref[...] = a
        b_ref[...] = b
        ff_ref[...] = (a * _sigmoid(a) * b).astype(BF16)

    w = pl.BlockSpec((tn, D), lambda j, i: (j, 0))
    act = pl.BlockSpec((tm, tn), lambda j, i: (i, j))
    return pl.pallas_call(
        body, name=name,
        out_shape=(jax.ShapeDtypeStruct((Tn, FH), F32), jax.ShapeDtypeStruct((Tn, FH), F32), jax.ShapeDtypeStruct((Tn, FH), BF16)),
        grid=(FH // tn, Tn // tm), in_specs=[pl.BlockSpec((tm, D), lambda j, i: (i, 0)), w, w], out_specs=(act, act, act),
        compiler_params=_cp("parallel", "parallel"),
    )(v2, wgt, wut)


def _ffn_dact(dh2b, wd, a, b, name="ffn_dact", tm=1024, tn=512):
    Tn = dh2b.shape[0]
    tm = min(tm, Tn)

    def body(d_ref, w_ref, a_ref, b_ref, da_ref, db_ref):
        dff = _dot(d_ref[...], w_ref[...], tb=True)
        av = a_ref[...]
        sg = _sigmoid(av)
        da_ref[...] = (dff * b_ref[...] * (sg * (1.0 + av * (1.0 - sg)))).astype(BF16)
        db_ref[...] = (dff * (av * sg)).astype(BF16)

    act = pl.BlockSpec((tm, tn), lambda j, i: (i, j))
    return pl.pallas_call(
        body, name=name,
        out_shape=(jax.ShapeDtypeStruct((Tn, FH), BF16), jax.ShapeDtypeStruct((Tn, FH), BF16)),
        grid=(FH // tn, Tn // tm),
        in_specs=[pl.BlockSpec((tm, D), lambda j, i: (i, 0)), pl.BlockSpec((tn, D), lambda j, i: (j, 0)), act, act],
        out_specs=(act, act), compiler_params=_cp("parallel", "parallel"),
    )(dh2b, wd, a, b)


def _adam_math(w, g, m, v):
    m2 = B1 * m + (1.0 - B1) * g
    v2 = B2 * v + (1.0 - B2) * (g * g)
    mh = m2 / (1.0 - B1 ** STEP)
    vh = v2 / (1.0 - B2 ** STEP)
    return -LR * (mh / (jnp.sqrt(vh) + AEPS) + WD * w), m2, v2


def _adamw(w, m, v, psums, parts, chip_idx, name, tr):
    R, C = w.shape

    def body(s_ref, w_ref, m_ref, v_ref, o_ref, p_ref, g_ref, d_ref, m2_ref, v2_ref):
        g = ((o_ref[...].astype(F32) + p_ref[0].astype(F32)) + p_ref[1].astype(F32)) + p_ref[2].astype(F32)
        d, m2, v2 = _adam_math(w_ref[...], g, m_ref[...], v_ref[...])
        g_ref[...] = g
        d_ref[...] = d
        m2_ref[...] = m2
        v2_ref[...] = v2

    blk = pl.BlockSpec((tr, C), lambda i, s: (i, 0))
    out = jax.ShapeDtypeStruct((R, C), F32)
    grid_spec = pltpu.PrefetchScalarGridSpec(
        num_scalar_prefetch=1, grid=(R // tr,),
        in_specs=[blk, blk, blk, pl.BlockSpec((None, tr, C), lambda i, s: (s[0], i, 0)), pl.BlockSpec((3, tr, C), lambda i, s: (0, i, 0))],
        out_specs=(blk, blk, blk, blk),
    )
    return pl.pallas_call(body, name=name, out_shape=(out, out, out, out), grid_spec=grid_spec, compiler_params=_cp("parallel"),
                          )(chip_idx, w, m, v, psums, parts)


def _adamw_rows(w, m, v, g, name, tr):
    R = w.shape[0]

    def body(w_ref, m_ref, v_ref, g_ref, d_ref, m2_ref, v2_ref):
        d, m2, v2 = _adam_math(w_ref[...], g_ref[...], m_ref[...], v_ref[...])
        d_ref[...] = d
        m2_ref[...] = m2
        v2_ref[...] = v2

    blk = pl.BlockSpec((tr,) + w.shape[1:], lambda i: (i, 0, 0))
    out = jax.ShapeDtypeStruct(w.shape, F32)
    return pl.pallas_call(body, name=name, out_shape=(out, out, out), grid=(R // tr,), in_specs=[blk] * 4, out_specs=(blk, blk, blk),
                          compiler_params=_cp("parallel"))(w, m, v, g)


def _sum_parts(psums, parts, chip_idx, name, tr, tc):
    _, R, C = psums.shape

    def body(s_ref, o_ref, p_ref, g_ref):
        g_ref[...] = ((o_ref[...].astype(F32) + p_ref[0].astype(F32)) + p_ref[1].astype(F32)) + p_ref[2].astype(F32)

    grid_spec = pltpu.PrefetchScalarGridSpec(
        num_scalar_prefetch=1, grid=(R // tr, C // tc),
        in_specs=[pl.BlockSpec((None, tr, tc), lambda i, j, s: (s[0], i, j)), pl.BlockSpec((3, tr, tc), lambda i, j, s: (0, i, j))],
        out_specs=pl.BlockSpec((tr, tc), lambda i, j, s: (i, j)),
    )
    return pl.pallas_call(body, name=name, out_shape=jax.ShapeDtypeStruct((R, C), F32), grid_spec=grid_spec,
                          compiler_params=_cp("parallel", "parallel"))(chip_idx, psums, parts)


def _adamw_plain(w, m, v, g, name):
    def body(w_ref, m_ref, v_ref, g_ref, d_ref, m2_ref, v2_ref):
        d, m2, v2 = _adam_math(w_ref[...], g_ref[...], m_ref[...], v_ref[...])
        d_ref[...] = d
        m2_ref[...] = m2
        v2_ref[...] = v2

    out = jax.ShapeDtypeStruct(w.shape, F32)
    return pl.pallas_call(body, name=name, out_shape=(out, out, out))(w, m, v, g)


def _sum_devices(pack_all, name="sum_small"):
    def body(p_ref, o_ref):
        s = p_ref[0]
        for k in range(1, NDEV):
            s = s + p_ref[k]
        o_ref[...] = s

    return pl.pallas_call(body, name=name, out_shape=jax.ShapeDtypeStruct(pack_all.shape[1:], F32))(pack_all)


def _pair_add(g5, recv, c_idx, name, tr):
    _, _, R, C = g5.shape

    def body(c_ref, g_ref, r_ref, o_ref):
        o_ref[...] = (g_ref[...].astype(F32) + r_ref[...].astype(F32)).astype(BF16)

    grid_spec = pltpu.PrefetchScalarGridSpec(
        num_scalar_prefetch=1, grid=(4, R // tr),
        in_specs=[pl.BlockSpec((None, None, tr, C), lambda q, i, c: (q, c[0], i, 0)), pl.BlockSpec((None, tr, C), lambda q, i, c: (q, i, 0))],
        out_specs=pl.BlockSpec((None, tr, C), lambda q, i, c: (q, i, 0)),
    )
    return pl.pallas_call(
        body, name=name, out_shape=jax.ShapeDtypeStruct((4, R, C), BF16), grid_spec=grid_spec,
        compiler_params=_cp("parallel", "parallel"),
    )(c_idx, g5, recv)


_ANY = pl.BlockSpec(memory_space=pl.ANY)


def _mesh_pos():
    x, y, c = lax.axis_index("x"), lax.axis_index("y"), lax.axis_index("c")
    return x, y, c, [(1 - x, y), (x, 1 - y), (1 - x, 1 - y)]


def _all_gather(shards, name="gather_weights"):
    n = len(shards)

    def body(*refs):
        ins, outs = refs[:n], refs[n:2 * n]
        send, recv, loc = refs[2 * n + 1:]
        x, y, c, chips = _mesh_pos()
        me, sib = (x, y, c), (x, y, 1 - c)

        def cp(a, k, block, to, own=False):
            dst = outs[a].at[4 * block[0] + 2 * block[1] + block[2]]
            return pltpu.make_async_remote_copy(src_ref=ins[a] if own else dst, dst_ref=dst, send_sem=send.at[a, k],
                                                recv_sem=recv.at[a, k], device_id=to, device_id_type=MESH)

        local = [pltpu.make_async_copy(ins[a], outs[a].at[4 * x + 2 * y + c], loc.at[a]) for a in range(n)]
        first = []
        for a in range(n):
            local[a].start()
            first.append(cp(a, 0, me, sib, own=True))
            first += [cp(a, 1 + j, me, (*chip, c), own=True) for j, chip in enumerate(chips)]
        for d in first:
            d.start()
        passed = []
        for a in range(n):
            for j, chip in enumerate(chips):
                cp(a, 1 + j, (*chip, c), me).wait_recv()
                d = cp(a, 4 + j, (*chip, c), sib)
                d.start()
                passed.append(d)
        for a in range(n):
            cp(a, 0, sib, me).wait_recv()
            for j, chip in enumerate(chips):
                cp(a, 4 + j, (*chip, 1 - c), me).wait_recv()
        for d in first + passed:
            d.wait_send()
        for d in local:
            d.wait()
        refs[2 * n][...] = jnp.zeros_like(refs[2 * n])

    return pl.pallas_call(
        body, name=name,
        out_shape=tuple(jax.ShapeDtypeStruct((NDEV,) + s.shape, s.dtype) for s in shards) + (jax.ShapeDtypeStruct((8, LANE), F32),),
        in_specs=[_ANY] * n, out_specs=tuple([_ANY] * n) + (pl.BlockSpec(memory_space=pltpu.VMEM),),
        scratch_shapes=[pltpu.SemaphoreType.DMA((n, 7)), pltpu.SemaphoreType.DMA((n, 7)), pltpu.SemaphoreType.DMA((n,))],
    )(*shards)


def _pair_exchange(grads, name):
    n = len(grads)

    def body(*refs):
        ins, outs = refs[:n], refs[n:2 * n]
        send, recv = refs[2 * n:]
        x, y, c, _ = _mesh_pos()
        big = [pltpu.make_async_remote_copy(src_ref=ins[a].at[:, 1 - c], dst_ref=outs[a], send_sem=send.at[a], recv_sem=recv.at[a],
                                            device_id=(x, y, 1 - c), device_id_type=MESH) for a in range(n)]
        for d in big:
            d.start()
        for d in big:
            d.wait_recv()
        for d in big:
            d.wait_send()

    return pl.pallas_call(
        body, name=name, out_shape=tuple(jax.ShapeDtypeStruct((4,) + g.shape[2:], g.dtype) for g in grads),
        in_specs=[_ANY] * n, out_specs=tuple([_ANY] * n),
        scratch_shapes=[pltpu.SemaphoreType.DMA((n,)), pltpu.SemaphoreType.DMA((n,))],
    )(*grads)


def _gather_small(pack, name="gather_small"):
    def body(pk, pk_all, psend, precv, loc):
        x, y, c, chips = _mesh_pos()
        me_slot = 4 * x + 2 * y + c
        sib = (x, y, 1 - c)
        own = pltpu.make_async_copy(pk, pk_all.at[me_slot], loc)
        own.start()
        peers = [sib] + [(*chip, c) for chip in chips] + [(*chip, 1 - c) for chip in chips]
        small = [pltpu.make_async_remote_copy(src_ref=pk, dst_ref=pk_all.at[me_slot], send_sem=psend.at[k], recv_sem=precv.at[k],
                                              device_id=p, device_id_type=MESH) for k, p in enumerate(peers)]
        for d in small:
            d.start()
        for k, p in enumerate(peers):
            pltpu.make_async_remote_copy(src_ref=pk, dst_ref=pk_all.at[4 * p[0] + 2 * p[1] + p[2]], send_sem=psend.at[k],
                                         recv_sem=precv.at[k], device_id=p, device_id_type=MESH).wait_recv()
        for d in small:
            d.wait_send()
        own.wait()

    return pl.pallas_call(
        body, name=name, out_shape=jax.ShapeDtypeStruct((NDEV,) + pack.shape, pack.dtype), in_specs=[_ANY], out_specs=_ANY,
        scratch_shapes=[pltpu.SemaphoreType.DMA((7,)), pltpu.SemaphoreType.DMA((7,)), pltpu.SemaphoreType.DMA(())],
    )(pack)


def _main_row(g):
    return g if g < C_LR else g - RANK


def _window_pieces(lo, hi):
    out = []
    for a, b, where in ((lo, min(hi, C_LR), "main"), (max(lo, C_LR), min(hi, C_LR + RANK), "lr"), (max(lo, C_LR + RANK), hi, "main")):
        if a < b:
            out.append((a, b, where, _main_row(a) if where == "main" else a - C_LR))
    return out


def _assemble_w_in(windows, name="assemble_w_in"):
    edges = NDEV - 1

    def body(b_ref, main_ref, lr_ref, buf, ebuf, in_sems, out_sems, esems):
        def load(k):
            return pltpu.make_async_copy(b_ref.at[k], buf.at[k % 2], in_sems.at[k % 2])

        lr_ref[RANK:, :] = jnp.zeros((LANE - RANK, D), BF16)
        load(0).start()
        pending, edge_out = [], []
        for k in range(NDEV):
            s = k % 2
            load(k).wait()
            if k:
                ebuf[k - 1] = buf[1 - s, WSTEP:WWIN, :] + buf[s, 0:16, :]
                edge_out.append(pltpu.make_async_copy(ebuf.at[k - 1], main_ref.at[pl.ds(_main_row(WSTEP * k), 16)], esems.at[k - 1]))
                edge_out[-1].start()
                for d in pending:
                    d.wait()
            if k + 1 < NDEV:
                load(k + 1).start()
            pending = []
            lo = WSTEP * k + (16 if k else 0)
            hi = WSTEP * k + (WWIN if k == NDEV - 1 else WSTEP)
            for a, b, where, dst in _window_pieces(lo, hi):
                if where == "lr":
                    lr_ref[dst:dst + b - a, :] = buf[s, a - WSTEP * k:b - WSTEP * k, :]
                else:
                    pending.append(pltpu.make_async_copy(buf.at[s, pl.ds(a - WSTEP * k, b - a)], main_ref.at[pl.ds(dst, b - a)],
                                                         out_sems.at[2 * s + len(pending)]))
                    pending[-1].start()
        for d in pending + edge_out:
            d.wait()

    return pl.pallas_call(
        body, name=name,
        out_shape=(jax.ShapeDtypeStruct((NMAIN, D), BF16), jax.ShapeDtypeStruct((LANE, D), BF16)),
        in_specs=[_ANY], out_specs=(_ANY, pl.BlockSpec(memory_space=pltpu.VMEM)),
        scratch_shapes=[pltpu.VMEM((2, WWIN, D), BF16), pltpu.VMEM((edges, 16, D), BF16), pltpu.SemaphoreType.DMA((2,)),
                        pltpu.SemaphoreType.DMA((4,)), pltpu.SemaphoreType.DMA((edges,))],
        compiler_params=pltpu.CompilerParams(vmem_limit_bytes=VMEM_LIMIT),
    )(windows)


def _disassemble_w_in(d_main, d_lr, name="disassemble_w_in"):
    def body(main_ref, lr_ref, g_ref, buf, in_sems, out_sems):
        def loads(k):
            s, out = k % 2, []
            for a, b, where, src0 in _window_pieces(WSTEP * k, WSTEP * k + WWIN):
                if where == "main":
                    out.append(pltpu.make_async_copy(main_ref.at[pl.ds(src0, b - a)], buf.at[s, pl.ds(a - WSTEP * k, b - a)],
                                                     in_sems.at[2 * s + len(out)]))
            return out

        def store(k):
            return pltpu.make_async_copy(buf.at[k % 2], g_ref.at[k], out_sems.at[k % 2])

        for d in loads(0):
            d.start()
        for k in range(NDEV):
            for d in loads(k):
                d.wait()
            for a, b, where, src0 in _window_pieces(WSTEP * k, WSTEP * k + WWIN):
                if where == "lr":
                    buf[k % 2, a - WSTEP * k:b - WSTEP * k, :] = lr_ref[src0:src0 + b - a, :]
            if k:
                store(k - 1).wait()
            if k + 1 < NDEV:
                for d in loads(k + 1):
                    d.start()
            store(k).start()
        store(NDEV - 1).wait()

    return pl.pallas_call(
        body, name=name, out_shape=jax.ShapeDtypeStruct((NDEV, WWIN, D), BF16),
        in_specs=[_ANY, pl.BlockSpec(memory_space=pltpu.VMEM)], out_specs=_ANY,
        scratch_shapes=[pltpu.VMEM((2, WWIN, D), BF16), pltpu.SemaphoreType.DMA((4,)), pltpu.SemaphoreType.DMA((2,))],
        compiler_params=pltpu.CompilerParams(vmem_limit_bytes=VMEM_LIMIT),
    )(d_main, d_lr)


_HBM = pl.BlockSpec(memory_space=pltpu.HBM)
_SEM = pl.BlockSpec(memory_space=pltpu.SEMAPHORE)
_VMEM = pl.BlockSpec(memory_space=pltpu.VMEM)
_SIDE = pltpu.CompilerParams(has_side_effects=pltpu.SideEffectType.DATAFLOW_SIDE_EFFECTING)
_TOKEN = jax.ShapeDtypeStruct((8, LANE), F32)


def _hbm(a):
    return pltpu.with_memory_space_constraint(a, pltpu.HBM)


def _hbm_like(arrs):
    return tuple(pltpu.HBM(a.shape, a.dtype) for a in arrs)


def _tie(x, token):
    return x + token[0, 0].astype(x.dtype)


def _chip_copies(ins, lands, send, recv):
    x, y, c, chips = _mesh_pos()
    return [pltpu.make_async_remote_copy(src_ref=ins[a].at[2 * chip[0] + chip[1]], dst_ref=lands[a].at[j], send_sem=send.at[3 * a + j],
                                         recv_sem=recv.at[3 * a + j], device_id=(*chip, c), device_id_type=MESH)
            for a in range(len(ins)) for j, chip in enumerate(chips)]


def _chip_start(psums, name):
    n = len(psums)
    lands = [lax.empty((3,) + p.shape[1:], p.dtype) for p in psums]

    def body(*refs):
        for d in _chip_copies(refs[:n], refs[n:2 * n], refs[2 * n], refs[2 * n + 1]):
            d.start()
        refs[-1][...] = jnp.zeros_like(refs[-1])

    sems = pltpu.SemaphoreType.DMA((3 * n,))
    out = pl.pallas_call(
        body, name=name, out_shape=(sems, sems) + _hbm_like(psums) + _hbm_like(lands) + (_TOKEN,),
        in_specs=[_HBM] * (2 * n), out_specs=(_SEM, _SEM) + (_HBM,) * (2 * n) + (_VMEM,),
        input_output_aliases={i: 2 + i for i in range(2 * n)}, compiler_params=_SIDE,
    )(*[_hbm(a) for a in list(psums) + lands])
    return out[0], out[1], list(out[2:2 + n]), list(out[2 + n:2 + 2 * n]), out[-1]


def _chip_wait(send, recv, psums, lands, after, name):
    n = len(psums)

    def body(*refs):
        for d in _chip_copies(refs[:n], refs[n:2 * n], refs[2 * n], refs[2 * n + 1]):
            d.wait_send()
            d.wait_recv()

    out = pl.pallas_call(
        body, name=name, out_shape=_hbm_like(psums) + _hbm_like(lands),
        in_specs=[_HBM] * (2 * n) + [_SEM, _SEM, _ANY], out_specs=(_HBM,) * (2 * n),
        input_output_aliases={i: i for i in range(2 * n)}, compiler_params=_SIDE,
    )(*psums, *lands, send, recv, after)
    return list(out[:n]), list(out[n:])


def _slot(chip, c):
    return 4 * chip[0] + 2 * chip[1] + c


def _gather_start(shards, dev, name):
    n = len(shards)
    lands = [lax.dynamic_update_slice(lax.empty((NDEV,) + s.shape, s.dtype), s[None], (dev,) + (0,) * s.ndim) for s in shards]

    def body(*refs):
        src, land, send, recv = refs[:n], refs[n:2 * n], refs[2 * n], refs[2 * n + 1]
        x, y, c, chips = _mesh_pos()
        for a in range(n):
            for k, to in enumerate([(x, y, 1 - c)] + [(*chip, c) for chip in chips]):
                pltpu.make_async_remote_copy(src_ref=src[a], dst_ref=land[a].at[_slot((x, y), c)], send_sem=send.at[4 * a + k],
                                             recv_sem=recv.at[4 * a + k], device_id=to, device_id_type=MESH).start()
        refs[-1][...] = jnp.zeros_like(refs[-1])

    sems = pltpu.SemaphoreType.DMA((4 * n,))
    out = pl.pallas_call(
        body, name=name, out_shape=(sems, sems) + _hbm_like(shards) + _hbm_like(lands) + (_TOKEN,),
        in_specs=[_HBM] * (2 * n), out_specs=(_SEM, _SEM) + (_HBM,) * (2 * n) + (_VMEM,),
        input_output_aliases={i: 2 + i for i in range(2 * n)}, compiler_params=_SIDE,
    )(*[_hbm(a) for a in list(shards) + lands])
    return out[0], out[1], list(out[2:2 + n]), list(out[2 + n:2 + 2 * n]), out[-1]


def _gather_pass(lands, recv, after, name, first=0):
    n = len(lands)

    def body(*refs):
        land, recv1 = refs[:n], refs[n]
        send2, recv2 = refs[n + 2], refs[n + 3]
        x, y, c, chips = _mesh_pos()
        for a in range(n):
            for j, chip in enumerate(chips):
                blk = land[a].at[_slot(chip, c)]
                pltpu.make_async_remote_copy(src_ref=blk, dst_ref=blk, send_sem=send2.at[3 * a + j], recv_sem=recv1.at[4 * (first + a) + 1 + j],
                                             device_id=(*chip, c), device_id_type=MESH).wait_recv()
                pltpu.make_async_remote_copy(src_ref=blk, dst_ref=blk, send_sem=send2.at[3 * a + j], recv_sem=recv2.at[3 * a + j],
                                             device_id=(x, y, 1 - c), device_id_type=MESH).start()
        refs[-1][...] = jnp.zeros_like(refs[-1])

    sems = pltpu.SemaphoreType.DMA((3 * n,))
    out = pl.pallas_call(
        body, name=name, out_shape=(sems, sems) + _hbm_like(lands) + (_TOKEN,),
        in_specs=[_HBM] * n + [_SEM, _ANY], out_specs=(_SEM, _SEM) + (_HBM,) * n + (_VMEM,),
        input_output_aliases={i: 2 + i for i in range(n)}, compiler_params=_SIDE,
    )(*lands, recv, after)
    return out[0], out[1], list(out[2:2 + n]), out[-1]


def _gather_wait(shards, lands, send, recv, send2, recv2, after, name, first=0):
    n = len(lands)

    def body(*refs):
        src, land = refs[:n], refs[n:2 * n]
        send1, recv1, snd2, rcv2 = refs[2 * n:2 * n + 4]
        x, y, c, chips = _mesh_pos()
        sib = (x, y, 1 - c)
        for a in range(n):
            for k in range(4):
                pltpu.make_async_remote_copy(src_ref=src[a], dst_ref=land[a].at[_slot((x, y), c)], send_sem=send1.at[4 * (first + a) + k],
                                             recv_sem=recv1.at[4 * (first + a) + k], device_id=sib, device_id_type=MESH).wait_send()
            blk = land[a].at[_slot((x, y), 1 - c)]
            pltpu.make_async_remote_copy(src_ref=blk, dst_ref=blk, send_sem=send1.at[4 * (first + a)], recv_sem=recv1.at[4 * (first + a)],
                                         device_id=sib, device_id_type=MESH).wait_recv()
            for j, chip in enumerate(chips):
                mine, theirs = land[a].at[_slot(chip, c)], land[a].at[_slot(chip, 1 - c)]
                pltpu.make_async_remote_copy(src_ref=mine, dst_ref=mine, send_sem=snd2.at[3 * a + j], recv_sem=rcv2.at[3 * a + j],
                                             device_id=sib, device_id_type=MESH).wait_send()
                pltpu.make_async_remote_copy(src_ref=theirs, dst_ref=theirs, send_sem=snd2.at[3 * a + j], recv_sem=rcv2.at[3 * a + j],
                                             device_id=sib, device_id_type=MESH).wait_recv()

    out = pl.pallas_call(
        body, name=name, out_shape=_hbm_like(shards) + _hbm_like(lands),
        in_specs=[_HBM] * (2 * n) + [_SEM] * 4 + [_ANY], out_specs=(_HBM,) * (2 * n),
        input_output_aliases={i: i for i in range(2 * n)}, compiler_params=_SIDE,
    )(*shards, *lands, send, recv, send2, recv2, after)
    return list(out[n:])


def _pad_to(v, n):
    return jnp.pad(v, [(0, 0)] * (v.ndim - 1) + [(0, n - v.shape[-1])])


def _pack_small(n1, gb, sk, gn, n2, fn, extra=None):
    parts = [n1.reshape(-1), gb.reshape(-1), sk.reshape(-1), gn.reshape(-1), n2.reshape(-1), fn.reshape(-1)]
    flat = jnp.concatenate(parts + ([extra.reshape(-1)] if extra is not None else []))
    return _pad_to(flat, SMALL_N).reshape(SMALL_ROWS, LANE)


def _unpack_small(p):
    f = p.reshape(-1)
    return (f[S_N1:S_GB].reshape(1, D), f[S_GB:S_SK].reshape(1, GH * DK), f[S_SK:S_GN].reshape(1, NQ), f[S_GN:S_N2].reshape(1, DV),
            f[S_N2:S_FN].reshape(1, D), f[S_FN:S_LOSS].reshape(D))


class _NoComm:
    def __init__(self, wo, wg_all, wu_all, wd_all):
        self.rest = (wo, wg_all, wu_all, wd_all)

    def mixed(self, gla_o, gla_norm_w):
        return gla_norm_w

    def w_out(self, merged, norm2_w):
        return self.rest[0], norm2_w

    def w_up(self, v2):
        return self.rest[1], self.rest[2]

    def w_down(self, ff):
        return self.rest[3]

    def ffn_grads(self, d_wg, d_wu, d_wd, norm2_w):
        self.ffn = (d_wg, d_wu, d_wd)
        return norm2_w

    def in_grads(self, d_wmain, d_wlr, d_wo, w_lr):
        self.inw = (d_wmain, d_wlr, d_wo)
        return w_lr


class _Comm:
    def __init__(self, rest_shards, dev, c_idx):
        self.c_idx = c_idx
        self.send, self.recv, self.shards, self.lands, self.token = _gather_start(rest_shards, dev, "gather_rest_start")

    def _pass(self, lo, hi, after, tag):
        send2, recv2, lands, token = _gather_pass(self.lands[lo:hi], self.recv, after, "gather_pass_" + tag, first=lo)
        self.passed = (lo, hi, send2, recv2, lands)
        return token

    def _wait(self, after, tag):
        lo, hi, send2, recv2, lands = self.passed
        return _gather_wait(self.shards[lo:hi], lands, self.send, self.recv, send2, recv2, after, "gather_wait_" + tag, first=lo)

    def mixed(self, gla_o, gla_norm_w):
        return _tie(gla_norm_w, self._pass(0, 1, gla_o, "out"))

    def w_out(self, merged, norm2_w):
        (wo_all,) = self._wait(merged, "out")
        return wo_all.reshape(D, D), _tie(norm2_w, self._pass(1, 3, merged, "up"))

    def w_up(self, v2):
        wg_all, wu_all = self._wait(v2, "up")
        self._pass(3, 4, v2, "down")
        return wg_all.reshape(FH, D), wu_all.reshape(FH, D)

    def w_down(self, ff):
        return self._wait(ff, "down")[0].reshape(FH, D)

    def _reduce(self, tag, names, grads, rows):
        recv1 = _pair_exchange(grads, "reduce_pair_" + tag)
        psums = [_pair_add(g, r, self.c_idx, "pair_add_" + nm, tr) for g, r, nm, tr in zip(grads, recv1, names, rows)]
        *flight, token = _chip_start(psums, "reduce_chips_start_" + tag)
        return dict(tag=tag, names=names, rows=rows, flight=flight), token

    def ffn_grads(self, d_wg, d_wu, d_wd, norm2_w):
        self.ffn, token = self._reduce("ffn", ["w_ffn_gate", "w_ffn_up", "w_ffn_down"],
                                       [d.reshape(4, 2, FS, D) for d in (d_wg, d_wu, d_wd)], [176, 176, 176])
        return _tie(norm2_w, token)

    def in_grads(self, d_wmain, d_wlr, d_wo, w_lr):
        d_win = _disassemble_w_in(d_wmain, d_wlr).reshape(4, 2, WWIN, D)
        self.inw, token = self._reduce("in", ["w_in", "w_out"], [d_win, d_wo.reshape(4, 2, D // NDEV, D)], [808, 256])
        return _tie(w_lr, token)


def _local_step(xs, tgt, norm1_w, gla_gate_b, attn_sinks, gla_norm_w, norm2_w, fnw, w_main, w_lr, w2p, comm):
    u = _rmsnorm_fwd(xs, norm1_w, "norm1_fwd")
    proj = _mm(u, w_main, tb=True, tm=1024, tn=1280, tk=D, name="in_proj")
    plr = _mm(u, w_lr, tb=True, tm=1024, tn=LANE, tk=D, name="in_proj_lr")
    attn_o = _attn_fwd(proj, attn_sinks)
    gla_o, states = _gla_fwd(proj, plr, w2p, gla_gate_b)
    merged = _merge_fwd(attn_o, gla_o, proj, comm.mixed(gla_o, gla_norm_w))
    wo, norm2_w = comm.w_out(merged, norm2_w)
    h1 = _mm(merged, wo, tm=1024, tn=512, tk=D, res=xs, name="out_proj")
    v2 = _rmsnorm_fwd(h1, norm2_w, "norm2_fwd")
    wg_all, wu_all = comm.w_up(v2)
    fa, fb, ff = _ffn_up(v2, wg_all, wu_all)
    wd_all = comm.w_down(ff)
    h2 = _mm(ff, wd_all, tm=1024, tn=1024, tk=FH // 2, res=h1, name="ffn_down")
    dh2, dh2b, d_fnw, loss_part = _loss_head(h2, fnw, tgt)

    da, db = _ffn_dact(dh2b, wd_all, fa, fb)
    Tn = xs.shape[0]
    d_wd = _mm(ff, dh2b, ta=True, tm=512, tn=D, tk=Tn, out_dtype=BF16, name="ffn_dwd")
    d_wg = _mm(da, v2, ta=True, tm=512, tn=D, tk=Tn, out_dtype=BF16, name="ffn_dwg")
    d_wu = _mm(db, v2, ta=True, tm=512, tn=D, tk=Tn, out_dtype=BF16, name="ffn_dwu")
    norm2_w = comm.ffn_grads(d_wg, d_wu, d_wd, norm2_w)
    dv2 = _mm(da, wg_all, tm=1024, tn=1024, tk=FH // 2, name="ffn_dv2_gate")
    dv2 = _mm(db, wu_all, tm=1024, tn=1024, tk=FH // 2, res=dv2, name="ffn_dv2_up")
    dh1, dh1b, d_n2 = _rmsnorm_bwd(dv2, h1, norm2_w, dh2, "norm2_bwd")
    dmerged = _mm(dh1b, wo, tb=True, tm=1024, tn=512, tk=D, name="out_proj_dx")
    d_wo = _mm(merged, dh1b, ta=True, tm=1024, tn=512, tk=xs.shape[0], out_dtype=BF16, name="out_proj_dw")
    d_attn, d_gla, d_gates, d_gnw = _merge_bwd(dmerged, attn_o, gla_o, proj, gla_norm_w)
    d_q, d_kv, d_sinks = _attn_bwd(proj, attn_sinks, attn_o, d_attn)
    d_gqk, d_gv, d_plr, d_w2p, d_gb = _gla_bwd(proj, plr, w2p, gla_gate_b, states, d_gla)
    dproj = jnp.concatenate([d_q, d_kv, d_gqk, d_gv, d_gates], axis=1)
    d_wmain = _mm(dproj, u, ta=True, tm=640, tn=D, tk=xs.shape[0], out_dtype=BF16, name="in_proj_dw")
    d_wlr = _mm(d_plr, u, ta=True, tm=LANE, tn=1024, tk=xs.shape[0], out_dtype=BF16, name="in_proj_lr_dw")
    du_lr = _mm(d_plr, comm.in_grads(d_wmain, d_wlr, d_wo, w_lr), tm=1024, tn=1024, tk=LANE, name="in_proj_lr_dx")
    du = _mm(dproj, w_main, tm=1024, tn=1024, tk=2560, res=du_lr, name="in_proj_dx")
    dx, _, d_n1 = _rmsnorm_bwd(du, xs, norm1_w, dh1, "norm1_bwd")
    return dx, loss_part, d_w2p, d_gb, d_sinks, d_gnw, d_n1, d_n2, d_fnw


def kernel(x, norm1_w, w_in, gla_gate_w2, gla_gate_b, attn_sinks, gla_norm_w, w_out, norm2_w, w_ffn_gate, w_ffn_up, w_ffn_down, final_norm_w, loss_target, m_norm1_w, m_w_in, m_gla_gate_w2, m_gla_gate_b, m_attn_sinks, m_gla_norm_w, m_w_out, m_norm2_w, m_w_ffn_gate, m_w_ffn_up, m_w_ffn_down, m_final_norm_w, v_norm1_w, v_w_in, v_gla_gate_w2, v_gla_gate_b, v_attn_sinks, v_gla_norm_w, v_w_out, v_norm2_w, v_w_ffn_gate, v_w_ffn_up, v_w_ffn_down, v_final_norm_w):
    xs, tgt = x[0], loss_target[0]
    fnw = final_norm_w.reshape(1, D)
    c_idx = lax.axis_index("c").astype(jnp.int32).reshape(1)
    dev = 4 * lax.axis_index("x") + 2 * lax.axis_index("y") + lax.axis_index("c")

    chip_idx = (2 * lax.axis_index("x") + lax.axis_index("y")).astype(jnp.int32).reshape(1)

    shift = (WS - WSTEP) * dev
    window = lax.dynamic_update_slice(jnp.zeros((WWIN, D), BF16), jnp.transpose(w_in[0]).astype(BF16), (shift, 0))
    win_all, w2_all, tok = _all_gather([window, gla_gate_w2[0]], name="gather_w_in")
    tr2 = lambda t: jnp.transpose(t[0])
    rest = [(w + tok[0, 0]).astype(BF16) for w in (w_out[0], tr2(w_ffn_gate), tr2(w_ffn_up), w_ffn_down[0])]
    comm = _Comm(rest, dev, c_idx)
    w_main, w_lr = _assemble_w_in(win_all)
    w2p =jnp.pad(jnp.transpose(w2_all, (1, 0, 2)).reshape(RANK, GH * DK), ((0, LANE - RANK), (0, 0)))

    dx, loss_part, d_w2p, d_gb, d_sinks, d_gnw, d_n1, d_n2, d_fnw = _local_step(
        xs, tgt, _tie(norm1_w, comm.token), gla_gate_b, attn_sinks, gla_norm_w, norm2_w, fnw, w_main, w_lr, w2p, comm)

    pack = jnp.concatenate([_pack_small(d_n1, d_gb, d_sinks, d_gnw, d_n2, d_fnw, loss_part),
                            d_w2p[:RANK].reshape(GW2_ROWS, LANE)], axis=0)
    small = _sum_devices(_gather_small(pack))

    big = {}
    after = dx
    for grp in (comm.ffn, comm.inw):
        psums, parts = _chip_wait(*grp["flight"], after, "reduce_chips_wait_" + grp["tag"])
        for nm, ps, pt, tr in zip(grp["names"], psums, parts, grp["rows"]):
            w, m, v = {"w_in": (w_in, m_w_in, v_w_in), "w_out": (w_out, m_w_out, v_w_out), "w_ffn_gate": (w_ffn_gate, m_w_ffn_gate, v_w_ffn_gate),
                       "w_ffn_up": (w_ffn_up, m_w_ffn_up, v_w_ffn_up), "w_ffn_down": (w_ffn_down, m_w_ffn_down, v_w_ffn_down)}[nm]
            if nm == "w_in":
                g_win = _sum_parts(ps, pt, chip_idx, "sum_w_in", tr, 1024)
                rows3 = lambda t: jnp.transpose(t[0]).reshape(WS, D // LANE, LANE)
                g3 = lax.dynamic_slice(g_win, (shift, 0), (WS, D)).reshape(WS, D // LANE, LANE)
                out3 = (g3,) + tuple(_adamw_rows(rows3(w), rows3(m), rows3(v), g3, "adamw_w_in", 178))
                big[nm] = [jnp.transpose(t.reshape(WS, D))[None] for t in out3]
            elif nm in ("w_ffn_gate", "w_ffn_up"):
                big[nm] = [jnp.transpose(t)[None] for t in _adamw(tr2(w), tr2(m), tr2(v), ps, pt, chip_idx, "adamw_" + nm, tr)]
            else:
                big[nm] = [t[None] for t in _adamw(w[0], m[0], v[0], ps, pt, chip_idx, "adamw_" + nm, tr)]
            after = big[nm][0]
    g_small = small[:SMALL_ROWS]
    sm = _adamw_plain(_pack_small(norm1_w, gla_gate_b, attn_sinks, gla_norm_w, norm2_w, final_norm_w),
                      _pack_small(m_norm1_w, m_gla_gate_b, m_attn_sinks, m_gla_norm_w, m_norm2_w, m_final_norm_w),
                      _pack_small(v_norm1_w, v_gla_gate_b, v_attn_sinks, v_gla_norm_w, v_norm2_w, v_final_norm_w), g_small, "adamw_small")
    g_w2 = lax.dynamic_slice_in_dim(small[SMALL_ROWS:].reshape(RANK, GH * DK), dev * LANE, LANE, axis=1)
    w2 = [g_w2[None]] + [t[None] for t in _adamw_plain(gla_gate_w2[0], m_gla_gate_w2[0], v_gla_gate_w2[0], g_w2, "adamw_w2")]
    loss = g_small.reshape(-1)[S_LOSS]

    sg, sd, sm2, sv2 = [_unpack_small(t) for t in (g_small,) + tuple(sm)]

    def group(i, s):
        return (s[0], big["w_in"][i], w2[i], s[1], s[2], s[3], big["w_out"][i], s[4], big["w_ffn_gate"][i], big["w_ffn_up"][i],
                big["w_ffn_down"][i], s[5])

    return (loss, dx[None], *group(0, sg), *group(1, sd), *group(2, sm2), *group(3, sv2))
```

```python
import functools

import jax
import jax.numpy as jnp
from jax import lax
from jax.experimental import pallas as pl
from jax.experimental.pallas import tpu as pltpu

F32, BF16 = jnp.float32, jnp.bfloat16
HIGHEST = lax.Precision.HIGHEST

D = 2048
HD, NQ, NKV, GRP, WIN = 64, 32, 4, 8, 128
GH, DK, DV, RANK, GC = 4, 256, 512, 16, 64
FH, NDEV = 5632, 8
FS = FH // NDEV
DIN = 12816
WS = DIN // NDEV
EPS = 1e-6
MASKV = -1e30
LANE = 128

C_AQ, C_AK, C_AV, C_GQ, C_GK, C_GV, C_GR, C_GA, C_GB, NMAIN = 0, 2048, 2304, 2560, 3584, 4608, 6656, 8704, 10752, 12800
C_LR = 6656
WSTEP, WWIN = 1600, 1616

LR, B1, B2, AEPS, WD, STEP = 0.001, 0.9, 0.999, 1e-08, 0.01, 10

S_N1, S_GB, S_SK, S_GN, S_N2, S_FN, S_LOSS, SMALL_N = 0, 2048, 3072, 3104, 3616, 5664, 7712, 8192
SMALL_ROWS = SMALL_N // LANE
GW2_ROWS = RANK * GH * DK // LANE
PACK_ROWS = SMALL_ROWS + GW2_ROWS

MESH = pl.DeviceIdType.MESH


def _dot(a, b, ta=False, tb=False, prec=None):
    dn = (((0,) if ta else (1,), (1,) if tb else (0,)), ((), ()))
    return lax.dot_general(a, b, dn, preferred_element_type=F32, precision=prec)


def _sigmoid(x):
    return 1.0 / (1.0 + jnp.exp(-x))


VMEM_LIMIT = 56 * 1024 * 1024


def _cp(*sem):
    return pltpu.CompilerParams(dimension_semantics=sem, vmem_limit_bytes=VMEM_LIMIT)


def _mm(a, b, *, ta=False, tb=False, tm, tn, tk, out_dtype=F32, res=None, after=None, name):
    M, K = (a.shape[1], a.shape[0]) if ta else a.shape
    N = b.shape[0] if tb else b.shape[1]
    tm, tn, tk = min(tm, M), min(tn, N), min(tk, K)
    nk = K // tk
    assert M % tm == 0 and N % tn == 0 and K % tk == 0
    a_spec = pl.BlockSpec((tk, tm), lambda i, j, k: (k, i)) if ta else pl.BlockSpec((tm, tk), lambda i, j, k: (i, k))
    b_spec = pl.BlockSpec((tn, tk), lambda i, j, k: (j, k)) if tb else pl.BlockSpec((tk, tn), lambda i, j, k: (k, j))
    o_spec = pl.BlockSpec((tm, tn), lambda i, j, k: (i, j))
    has_res = res is not None

    def body(*refs):
        a_ref, b_ref = refs[0], refs[1]
        r_ref = refs[2] if has_res else None
        o_ref = refs[2 + has_res + (after is not None)]
        p = _dot(a_ref[...].astype(BF16), b_ref[...].astype(BF16), ta, tb)
        if nk == 1:
            if has_res:
                p = p + r_ref[...]
            o_ref[...] = p.astype(out_dtype)
        else:
            acc = refs[-1]
            k = pl.program_id(2)

            @pl.when(k == 0)
            def _():
                acc[...] = (p + r_ref[...]) if has_res else p

            @pl.when(k > 0)
            def _():
                acc[...] += p

            @pl.when(k == nk - 1)
            def _():
                o_ref[...] = acc[...].astype(out_dtype)

    return pl.pallas_call(
        body, name=name,
        out_shape=jax.ShapeDtypeStruct((M, N), out_dtype),
        grid=(M // tm, N // tn, nk),
        in_specs=[a_spec, b_spec] + ([o_spec] if has_res else []) + ([pl.BlockSpec(memory_space=pl.ANY)] if after is not None else []),
        out_specs=o_spec,
        scratch_shapes=[pltpu.VMEM((tm, tn), F32)] if nk > 1 else [],
        compiler_params=_cp("parallel", "parallel", "arbitrary"),
    )(*((a, b) + ((res,) if has_res else ()) + ((after,) if after is not None else ())))


def _rmsnorm_fwd(x, w, name, tm=256):
    Tn = x.shape[0]

    def body(x_ref, w_ref, o_ref):
        xv = x_ref[...]
        r = lax.rsqrt(jnp.mean(xv * xv, axis=1, keepdims=True) + EPS)
        o_ref[...] = (xv * r * w_ref[...]).astype(BF16)

    return pl.pallas_call(
        body, name=name, out_shape=jax.ShapeDtypeStruct((Tn, D), BF16), grid=(Tn // tm,),
        in_specs=[pl.BlockSpec((tm, D), lambda i: (i, 0)), pl.BlockSpec((1, D), lambda i: (0, 0))],
        out_specs=pl.BlockSpec((tm, D), lambda i: (i, 0)), compiler_params=_cp("parallel"),
    )(x, w)


def _rmsnorm_bwd(dy, h, w, res, name, tm=256):
    Tn = h.shape[0]

    def body(dy_ref, h_ref, w_ref, res_ref, dh_ref, dhb_ref, dw_ref):
        hv, dyv = h_ref[...], dy_ref[...]
        r = lax.rsqrt(jnp.mean(hv * hv, axis=1, keepdims=True) + EPS)
        g = dyv * w_ref[...]
        dh = res_ref[...] + r * g - hv * (r * r * r * jnp.mean(g * hv, axis=1, keepdims=True))
        dh_ref[...] = dh
        dhb_ref[...] = dh.astype(BF16)
        part = jnp.sum(dyv * hv * r, axis=0, keepdims=True)

        @pl.when(pl.program_id(0) == 0)
        def _():
            dw_ref[...] = part

        @pl.when(pl.program_id(0) > 0)
        def _():
            dw_ref[...] += part

    row = pl.BlockSpec((tm, D), lambda i: (i, 0))
    vec = pl.BlockSpec((1, D), lambda i: (0, 0))
    return pl.pallas_call(
        body, name=name,
        out_shape=(jax.ShapeDtypeStruct((Tn, D), F32), jax.ShapeDtypeStruct((Tn, D), BF16), jax.ShapeDtypeStruct((1, D), F32)),
        grid=(Tn // tm,), in_specs=[row, row, vec, row], out_specs=(row, row, vec), compiler_params=_cp("arbitrary"),
    )(dy, h, w, res)


def _loss_head(h2, wf, tgt, name="loss_head", tm=256):
    Tn = h2.shape[0]

    def body(h_ref, w_ref, t_ref, dh_ref, dhb_ref, dw_ref, loss_ref):
        hv, wv = h_ref[...], w_ref[...]
        r = lax.rsqrt(jnp.mean(hv * hv, axis=1, keepdims=True) + EPS)
        hn = hv * r
        e = hn * wv - t_ref[...]
        dy = e * (1.0 / D)
        g = dy * wv
        dh = r * g - hv * (r * r * r * jnp.mean(g * hv, axis=1, keepdims=True))
        dh_ref[...] = dh
        dhb_ref[...] = dh.astype(BF16)
        part = jnp.sum(dy * hn, axis=0, keepdims=True)
        lpart = (0.5 / D) * jnp.sum(jnp.sum(e * e, axis=1, keepdims=True), axis=0, keepdims=True)

        @pl.when(pl.program_id(0) == 0)
        def _():
            dw_ref[...] = part
            loss_ref[...] = lpart

        @pl.when(pl.program_id(0) > 0)
        def _():
            dw_ref[...] += part
            loss_ref[...] += lpart

    row = pl.BlockSpec((tm, D), lambda i: (i, 0))
    vec = pl.BlockSpec((1, D), lambda i: (0, 0))
    one = pl.BlockSpec((1, 1), lambda i: (0, 0))
    return pl.pallas_call(
        body, name=name,
        out_shape=(jax.ShapeDtypeStruct((Tn, D), F32), jax.ShapeDtypeStruct((Tn, D), BF16), jax.ShapeDtypeStruct((1, D), F32),
                   jax.ShapeDtypeStruct((1, 1), F32)),
        grid=(Tn // tm,), in_specs=[row, vec, row], out_specs=(row, row, vec, one), compiler_params=_cp("arbitrary"),
    )(h2, wf, tgt)


def _attn_mask(n):
    qi = lax.broadcasted_iota(jnp.int32, (NKV, GRP * WIN, 2 * WIN), 1) % WIN
    ki = lax.broadcasted_iota(jnp.int32, (NKV, GRP * WIN, 2 * WIN), 2)
    rel = qi + WIN - ki
    return (rel >= 0) & (rel < WIN) & ((n > 0) | (ki >= WIN))


def _kv_heads(prev_ref, cur_ref):
    return jnp.stack([jnp.concatenate([prev_ref[:, h * HD:(h + 1) * HD], cur_ref[:, h * HD:(h + 1) * HD]], axis=0) for h in range(NKV)])


def _q_heads(ref):
    return jnp.stack([jnp.concatenate([ref[:, (h * GRP + g) * HD:(h * GRP + g + 1) * HD] for g in range(GRP)], axis=0) for h in range(NKV)])


def _attn_probs(q_ref, kc_ref, kp_ref, sink_ref, mask):
    kk = _kv_heads(kp_ref, kc_ref).astype(BF16)
    qs = _q_heads(q_ref).astype(BF16)
    s = jnp.einsum('hqd,hkd->hqk', qs, kk, preferred_element_type=F32) * (HD ** -0.5)
    s = jnp.where(mask, s, MASKV)
    sink = jnp.stack([jnp.concatenate([jnp.full((WIN, 1), sink_ref[0, h * GRP + g], F32) for g in range(GRP)], axis=0) for h in range(NKV)])
    m = jnp.maximum(jnp.max(s, axis=2, keepdims=True), sink)
    e = jnp.exp(s - m)
    es = jnp.exp(sink - m)
    inv = 1.0 / (jnp.sum(e, axis=2, keepdims=True) + es)
    return e * inv, es * inv, qs, kk


def _attn_specs(nb, last):
    cur = lambda n: jnp.minimum(n, last)
    prev = lambda n: jnp.maximum(jnp.minimum(n, last) - 1, 0)
    return [
        pl.BlockSpec((WIN, NQ * HD), lambda n: (cur(n), C_AQ // (NQ * HD))),
        pl.BlockSpec((WIN, NKV * HD), lambda n: (cur(n), C_AK // (NKV * HD))),
        pl.BlockSpec((WIN, NKV * HD), lambda n: (prev(n), C_AK // (NKV * HD))),
        pl.BlockSpec((WIN, NKV * HD), lambda n: (cur(n), C_AV // (NKV * HD))),
        pl.BlockSpec((WIN, NKV * HD), lambda n: (prev(n), C_AV // (NKV * HD))),
    ]


def _attn_fwd(proj, sinks, name="attn_fwd"):
    Tn = proj.shape[0]
    nb = Tn // WIN

    def body(q_ref, kc_ref, kp_ref, vc_ref, vp_ref, sink_ref, o_ref):
        p, _, _, _ = _attn_probs(q_ref, kc_ref, kp_ref, sink_ref, _attn_mask(pl.program_id(0)))
        o = jnp.einsum('hqk,hkd->hqd', p.astype(BF16), _kv_heads(vp_ref, vc_ref).astype(BF16), preferred_element_type=F32)
        for h in range(NKV):
            for g in range(GRP):
                o_ref[:, (h * GRP + g) * HD:(h * GRP + g + 1) * HD] = o[h, g * WIN:(g + 1) * WIN, :]

    return pl.pallas_call(
        body, name=name, out_shape=jax.ShapeDtypeStruct((Tn, D), F32), grid=(nb,),
        in_specs=_attn_specs(nb, nb - 1) + [pl.BlockSpec(memory_space=pltpu.SMEM)],
        out_specs=pl.BlockSpec((WIN, D), lambda n: (n, 0)), compiler_params=_cp("parallel"),
    )(proj, proj, proj, proj, proj, sinks)


def _attn_bwd(proj, sinks, o, do, name="attn_bwd"):
    Tn = proj.shape[0]
    nb = Tn // WIN
    KW = NKV * HD

    def body(q_ref, kc_ref, kp_ref, vc_ref, vp_ref, o_ref, do_ref, sink_ref, dq_ref, dkv_ref, dsk_ref, carry, cur):
        n = pl.program_id(0)

        @pl.when(n == 0)
        def _():
            carry[...] = jnp.zeros_like(carry)
            dsk_ref[...] = jnp.zeros_like(dsk_ref)

        @pl.when(n < nb)
        def _():
            p, ps, qs, kk = _attn_probs(q_ref, kc_ref, kp_ref, sink_ref, _attn_mask(n))
            vv = _kv_heads(vp_ref, vc_ref).astype(BF16)
            dos = _q_heads(do_ref)
            delta = jnp.sum(dos * _q_heads(o_ref), axis=2, keepdims=True)
            dosb = dos.astype(BF16)
            dp = jnp.einsum('hqd,hkd->hqk', dosb, vv, preferred_element_type=F32)
            ds = (p * (dp - delta) * (HD ** -0.5)).astype(BF16)
            dq = jnp.einsum('hqk,hkd->hqd', ds, kk, preferred_element_type=F32)
            dkk = jnp.einsum('hqk,hqd->hkd', ds, qs, preferred_element_type=F32)
            dvv = jnp.einsum('hqk,hqd->hkd', p.astype(BF16), dosb, preferred_element_type=F32)
            dsk = ps * delta
            for h in range(NKV):
                for g in range(GRP):
                    i = h * GRP + g
                    dq_ref[:, i * HD:(i + 1) * HD] = dq[h, g * WIN:(g + 1) * WIN, :].astype(BF16)
                    dsk_ref[:, i:i + 1] -= jnp.sum(dsk[h, g * WIN:(g + 1) * WIN, :], axis=0, keepdims=True)
                dkv_ref[:, h * HD:(h + 1) * HD] = (carry[:, h * HD:(h + 1) * HD] + dkk[h, :WIN, :]).astype(BF16)
                dkv_ref[:, KW + h * HD:KW + (h + 1) * HD] = (carry[:, KW + h * HD:KW + (h + 1) * HD] + dvv[h, :WIN, :]).astype(BF16)
                cur[:, h * HD:(h + 1) * HD] = dkk[h, WIN:, :]
                cur[:, KW + h * HD:KW + (h + 1) * HD] = dvv[h, WIN:, :]
            carry[...] = cur[...]

        @pl.when(n == nb)
        def _():
            dkv_ref[...] = carry[...].astype(BF16)

    last = nb - 1
    row = pl.BlockSpec((WIN, D), lambda n: (jnp.minimum(n, last), 0))
    return pl.pallas_call(
        body, name=name,
        out_shape=(jax.ShapeDtypeStruct((Tn, D), BF16), jax.ShapeDtypeStruct((Tn, 2 * KW), BF16), jax.ShapeDtypeStruct((1, NQ), F32)),
        grid=(nb + 1,),
        in_specs=_attn_specs(nb, last) + [row, row, pl.BlockSpec(memory_space=pltpu.SMEM)],
        out_specs=(row, pl.BlockSpec((WIN, 2 * KW), lambda n: (jnp.maximum(n - 1, 0), 0)), pl.BlockSpec((1, NQ), lambda n: (0, 0))),
        scratch_shapes=[pltpu.VMEM((WIN, 2 * KW), F32), pltpu.VMEM((WIN, 2 * KW), F32)],
        compiler_params=_cp("arbitrary"),
    )(proj, proj, proj, proj, proj, o, do, sinks)


def _tri(lower):
    r = lax.broadcasted_iota(jnp.int32, (GC, GC), 0)
    c = lax.broadcasted_iota(jnp.int32, (GC, GC), 1)
    return r >= c if lower else r <= c


def _per_head(a):
    return jnp.stack([a[:, h * DK:(h + 1) * DK] for h in range(GH)])


def _all_heads(a):
    return jnp.concatenate([a[h] for h in range(GH)], axis=1)


def _gla_gates(lr, w2_ref, gb_ref):
    logit = _dot(lr, w2_ref[...].astype(BF16)) + gb_ref[...]
    la = (jnp.minimum(logit, 0.0) - jnp.log(1.0 + jnp.exp(-jnp.abs(logit)))) * (1.0 / 16.0)
    g = _dot(_tri(True).astype(F32), la, prec=HIGHEST)
    return logit, g


def _bmm(spec, a, b):
    return jnp.einsum(spec, a, b, preferred_element_type=F32)


def _gla_specs(nc, rev):
    idx = (lambda n: nc - 1 - n) if rev else (lambda n: n)
    half = 2 * DK
    return (
        [pl.BlockSpec((GC, half), lambda n, j=j: (idx(n), C_GQ // half + j)) for j in range(2)]
        + [pl.BlockSpec((GC, half), lambda n, j=j: (idx(n), C_GK // half + j)) for j in range(2)]
        + [pl.BlockSpec((GC, DV), lambda n, h=h: (idx(n), C_GV // DV + h)) for h in range(GH)]
        + [pl.BlockSpec((GC, LANE), lambda n: (idx(n), 0)), pl.BlockSpec((LANE, GH * DK), lambda n: (0, 0)),
           pl.BlockSpec((1, GH * DK), lambda n: (0, 0))])


def _gla_heads(refs):
    return (lambda h: refs[h // 2][:, (h % 2) * DK:(h % 2 + 1) * DK], lambda h: refs[2 + h // 2][:, (h % 2) * DK:(h % 2 + 1) * DK],
            lambda h: refs[4 + h][...])


def _gla_fwd(proj, plr, w2p, gb, name="gla_fwd"):
    Tn = proj.shape[0]
    nc = Tn // GC

    def body(*refs):
        qh, kh, vh = _gla_heads(refs)
        lr_ref, w2_ref, gb_ref, o_ref, st_ref, S = refs[8:]

        @pl.when(pl.program_id(0) == 0)
        def _():
            S[...] = jnp.zeros_like(S)

        heads = lambda f: jnp.stack([f(h) for h in range(GH)])
        _, g_all = _gla_gates(lr_ref[...].astype(BF16), w2_ref, gb_ref)
        g = _per_head(g_all)
        gl = g[:, GC - 1:GC, :]
        k = heads(kh)
        v = heads(vh).astype(BF16)
        qd = (heads(qh) * (DK ** -0.5) * jnp.exp(g)).astype(BF16)
        ki = (k * jnp.exp(-g)).astype(BF16)
        ke = (k * jnp.exp(gl - g)).astype(BF16)
        att = jnp.where(_tri(True)[None], _bmm('hid,hjd->hij', qd, ki), 0.0).astype(BF16)
        sp = S[...]
        st_ref[0] = sp
        o = _bmm('hij,hjv->hiv', att, v) + _bmm('hid,hvd->hiv', qd, sp.astype(BF16))
        for h in range(GH):
            o_ref[:, h * DV:(h + 1) * DV] = o[h]
        S[...] = sp * jnp.exp(gl) + _bmm('hjv,hjd->hvd', v, ke)

    return pl.pallas_call(
        body, name=name,
        out_shape=(jax.ShapeDtypeStruct((Tn, GH * DV), F32), jax.ShapeDtypeStruct((nc, GH, DV, DK), F32)),
        grid=(nc,), in_specs=_gla_specs(nc, False),
        out_specs=(pl.BlockSpec((GC, GH * DV), lambda n: (n, 0)), pl.BlockSpec((1, GH, DV, DK), lambda n: (n, 0, 0, 0))),
        scratch_shapes=[pltpu.VMEM((GH, DV, DK), F32)], compiler_params=_cp("arbitrary"),
    )(*([proj] * 8), plr, w2p, gb)


def _gla_bwd(proj, plr, w2p, gb, states, do, name="gla_bwd"):
    Tn = proj.shape[0]
    nc = Tn // GC

    def body(*refs):
        qh, kh, vh = _gla_heads(refs)
        lr_ref, w2_ref, gb_ref, st_ref, do_ref, dqk_ref, dv_ref, dlr_ref, dw2_ref, dgb_ref, dS = refs[8:]

        @pl.when(pl.program_id(0) == 0)
        def _():
            dS[...] = jnp.zeros_like(dS)
            dw2_ref[...] = jnp.zeros_like(dw2_ref)
            dgb_ref[...] = jnp.zeros_like(dgb_ref)

        heads = lambda f: jnp.stack([f(h) for h in range(GH)])
        lr = lr_ref[...].astype(BF16)
        causal = _tri(True)[None]
        last_row = lax.broadcasted_iota(jnp.int32, (GH, GC, DK), 1) == GC - 1
        logit, g_all = _gla_gates(lr, w2_ref, gb_ref)
        g = _per_head(g_all)
        gl = g[:, GC - 1:GC, :]
        egl = jnp.exp(gl)
        eg, eng, ege = jnp.exp(g), jnp.exp(-g), jnp.exp(gl - g)
        k = heads(kh)
        v = heads(vh).astype(BF16)
        dob = heads(lambda h: do_ref[:, h * DV:(h + 1) * DV]).astype(BF16)
        qd = heads(qh) * (DK ** -0.5) * eg
        ki = k * eng
        ke = k * ege
        qdb, kib, keb = qd.astype(BF16), ki.astype(BF16), ke.astype(BF16)
        att = jnp.where(causal, _bmm('hid,hjd->hij', qdb, kib), 0.0).astype(BF16)
        datt = jnp.where(causal, _bmm('hiv,hjv->hij', dob, v), 0.0).astype(BF16)
        sp = st_ref[0]
        dsn = dS[...]
        dsnb = dsn.astype(BF16)
        dv = (_bmm('hij,hiv->hjv', att, dob) + _bmm('hjd,hvd->hjv', keb, dsnb)).astype(BF16)
        dqd = _bmm('hij,hjd->hid', datt, kib) + _bmm('hiv,hvd->hid', dob, sp.astype(BF16))
        dki = _bmm('hij,hid->hjd', datt, qdb)
        dke = _bmm('hjv,hvd->hjd', v, dsnb)
        ddec = jnp.sum(dsn * sp, axis=1, keepdims=True)
        dS[...] = dsn * egl + _bmm('hiv,hid->hvd', dob, qdb)
        dke_ke = dke * ke
        dgl = jnp.sum(dke_ke, axis=1, keepdims=True) + ddec * egl
        dg = dqd * qd - dki * ki - dke_ke + jnp.where(last_row, dgl, 0.0)
        dq = (dqd * ((DK ** -0.5) * eg)).astype(BF16)
        dk = (dki * eng + dke * ege).astype(BF16)
        for h in range(GH):
            dv_ref[:, h * DV:(h + 1) * DV] = dv[h]
            dqk_ref[:, h * DK:(h + 1) * DK] = dq[h]
            dqk_ref[:, GH * DK + h * DK:GH * DK + (h + 1) * DK] = dk[h]
        dla = _dot(_tri(False).astype(F32), _all_heads(dg), prec=HIGHEST)
        dlogit = dla * (1.0 / 16.0) * _sigmoid(-logit)
        dlb = dlogit.astype(BF16)
        dlr_ref[...] = _dot(dlb, w2_ref[...].astype(BF16), tb=True).astype(BF16)
        dw2_ref[...] += _dot(lr, dlb, ta=True)
        dgb_ref[...] += jnp.sum(dlogit, axis=0, keepdims=True)

    rev = lambda n: nc - 1 - n
    row = pl.BlockSpec((GC, GH * DV), lambda n: (rev(n), 0))
    return pl.pallas_call(
        body, name=name,
        out_shape=(jax.ShapeDtypeStruct((Tn, 2 * GH * DK), BF16), jax.ShapeDtypeStruct((Tn, GH * DV), BF16),
                   jax.ShapeDtypeStruct((Tn, LANE), BF16), jax.ShapeDtypeStruct((LANE, GH * DK), F32),
                   jax.ShapeDtypeStruct((1, GH * DK), F32)),
        grid=(nc,),
        in_specs=_gla_specs(nc, True) + [pl.BlockSpec((1, GH, DV, DK), lambda n: (rev(n), 0, 0, 0)), row],
        out_specs=(row, row, pl.BlockSpec((GC, LANE), lambda n: (rev(n), 0)), pl.BlockSpec((LANE, GH * DK), lambda n: (0, 0)),
                   pl.BlockSpec((1, GH * DK), lambda n: (0, 0))),
        scratch_shapes=[pltpu.VMEM((GH, DV, DK), F32)], compiler_params=_cp("arbitrary"),
    )(*([proj] * 8), plr, w2p, gb, states, do)


def _merge_specs(tm):
    row = pl.BlockSpec((tm, D), lambda i: (i, 0))
    gates = [pl.BlockSpec((tm, DV), lambda i, j=c // DV + h: (i, j)) for c in (C_GR, C_GA, C_GB) for h in range(GH)]
    return row, gates, pl.BlockSpec((1, DV), lambda i: (0, 0))


def _merge_fwd(a, go, proj, gnw, name="merge_fwd", tm=256):
    Tn = a.shape[0]

    def body(a_ref, go_ref, *rest):
        gates, w_ref, m_ref = rest[:3 * GH], rest[3 * GH], rest[3 * GH + 1]
        for h in range(GH):
            sl = slice(h * DV, (h + 1) * DV)
            gov = go_ref[:, sl]
            r = lax.rsqrt(jnp.mean(gov * gov, axis=1, keepdims=True) + EPS)
            gr = gates[h][...]
            g2 = gov * r * w_ref[...] * (gr * _sigmoid(gr))
            m_ref[:, sl] = (_sigmoid(gates[GH + h][...]) * a_ref[:, sl] + _sigmoid(gates[2 * GH + h][...]) * g2).astype(BF16)

    row, gates, vec = _merge_specs(tm)
    return pl.pallas_call(
        body, name=name, out_shape=jax.ShapeDtypeStruct((Tn, D), BF16), grid=(Tn // tm,),
        in_specs=[row, row] + gates + [vec], out_specs=row, compiler_params=_cp("parallel"),
    )(a, go, *([proj] * (3 * GH)), gnw)


def _merge_bwd(dm, a, go, proj, gnw, name="merge_bwd", tm=256):
    Tn = a.shape[0]

    def body(dm_ref, a_ref, go_ref, *rest):
        gates = rest[:3 * GH]
        w_ref, da_ref, dgo_ref, dg_ref, dw_ref = rest[3 * GH:]
        wv = w_ref[...]
        dw = jnp.zeros((1, DV), F32)
        for h in range(GH):
            sl = slice(h * DV, (h + 1) * DV)
            dmv, av, gov, gr = dm_ref[:, sl], a_ref[:, sl], go_ref[:, sl], gates[h][...]
            sa, sb, sg = _sigmoid(gates[GH + h][...]), _sigmoid(gates[2 * GH + h][...]), _sigmoid(gr)
            r = lax.rsqrt(jnp.mean(gov * gov, axis=1, keepdims=True) + EPS)
            gn0 = gov * r
            gn = gn0 * wv
            silu = gr * sg
            dg2 = dmv * sb
            da_ref[:, sl] = dmv * sa
            dg_ref[:, D + h * DV:D + (h + 1) * DV] = (dmv * av * sa * (1.0 - sa)).astype(BF16)
            dg_ref[:, 2 * D + h * DV:2 * D + (h + 1) * DV] = (dg2 * gn * silu * (1.0 - sb)).astype(BF16)
            dg_ref[:, sl] = (dg2 * gn * (sg * (1.0 + gr * (1.0 - sg)))).astype(BF16)
            dgn = dg2 * silu
            dw = dw + jnp.sum(dgn * gn0, axis=0, keepdims=True)
            gg = dgn * wv
            dgo_ref[:, sl] = r * gg - gov * (r * r * r * jnp.mean(gg * gov, axis=1, keepdims=True))

        @pl.when(pl.program_id(0) == 0)
        def _():
            dw_ref[...] = dw

        @pl.when(pl.program_id(0) > 0)
        def _():
            dw_ref[...] += dw

    row, gates, vec = _merge_specs(tm)
    return pl.pallas_call(
        body, name=name,
        out_shape=(jax.ShapeDtypeStruct((Tn, D), F32), jax.ShapeDtypeStruct((Tn, D), F32), jax.ShapeDtypeStruct((Tn, 3 * D), BF16),
                   jax.ShapeDtypeStruct((1, DV), F32)),
        grid=(Tn // tm,), in_specs=[row, row, row] + gates + [vec],
        out_specs=(row, row, pl.BlockSpec((tm, 3 * D), lambda i: (i, 0)), vec), compiler_params=_cp("arbitrary"),
    )(dm, a, go, *([proj] * (3 * GH)), gnw)


def _ffn_up(v2, wgt, wut, name="ffn_up", tm=1024, tn=512):
    Tn = v2.shape[0]
    tm = min(tm, Tn)

    def body(v_ref, wg_ref, wu_ref, a_ref, b_ref, ff_ref):
        vv = v_ref[...]
        a = _dot(vv, wg_ref[...], tb=True)
        b = _dot(vv, wu_ref[...], tb=True)
        a_ref[...] = a
        b_ref[...] = b
        ff_ref[...] = (a * _sigmoid(a) * b).astype(BF16)

    w = pl.BlockSpec((tn, D), lambda j, i: (j, 0))
    act = pl.BlockSpec((tm, tn), lambda j, i: (i, j))
    return pl.pallas_call(
        body, name=name,
        out_shape=(jax.ShapeDtypeStruct((Tn, FH), F32), jax.ShapeDtypeStruct((Tn, FH), F32), jax.ShapeDtypeStruct((Tn, FH), BF16)),
        grid=(FH // tn, Tn // tm), in_specs=[pl.BlockSpec((tm, D), lambda j, i: (i, 0)), w, w], out_specs=(act, act, act),
        compiler_params=_cp("parallel", "parallel"),
    )(v2, wgt, wut)


def _ffn_dact(dh2b, wd, a, b, name="ffn_dact", tm=1024, tn=512):
    Tn = dh2b.shape[0]
    tm = min(tm, Tn)

    def body(d_ref, w_ref, a_ref, b_ref, da_ref, db_ref):
        dff = _dot(d_ref[...], w_ref[...], tb=True)
        av = a_ref[...]
        sg = _sigmoid(av)
        da_ref[...] = (dff * b_ref[...] * (sg * (1.0 + av * (1.0 - sg)))).astype(BF16)
        db_ref[...] = (dff * (av * sg)).astype(BF16)

    act = pl.BlockSpec((tm, tn), lambda j, i: (i, j))
    return pl.pallas_call(
        body, name=name,
        out_shape=(jax.ShapeDtypeStruct((Tn, FH), BF16), jax.ShapeDtypeStruct((Tn, FH), BF16)),
        grid=(FH // tn, Tn // tm),
        in_specs=[pl.BlockSpec((tm, D), lambda j, i: (i, 0)), pl.BlockSpec((tn, D), lambda j, i: (j, 0)), act, act],
        out_specs=(act, act), compiler_params=_cp("parallel", "parallel"),
    )(dh2b, wd, a, b)


def _adam_math(w, g, m, v):
    m2 = B1 * m + (1.0 - B1) * g
    v2 = B2 * v + (1.0 - B2) * (g * g)
    mh = m2 / (1.0 - B1 ** STEP)
    vh = v2 / (1.0 - B2 ** STEP)
    return -LR * (mh / (jnp.sqrt(vh) + AEPS) + WD * w), m2, v2


def _adamw(w, m, v, psums, parts, chip_idx, name, tr):
    R, C = w.shape

    def body(s_ref, w_ref, m_ref, v_ref, o_ref, p_ref, g_ref, d_ref, m2_ref, v2_ref):
        g = ((o_ref[...].astype(F32) + p_ref[0].astype(F32)) + p_ref[1].astype(F32)) + p_ref[2].astype(F32)
        d, m2, v2 = _adam_math(w_ref[...], g, m_ref[...], v_ref[...])
        g_ref[...] = g
        d_ref[...] = d
        m2_ref[...] = m2
        v2_ref[...] = v2

    blk = pl.BlockSpec((tr, C), lambda i, s: (i, 0))
    out = jax.ShapeDtypeStruct((R, C), F32)
    grid_spec = pltpu.PrefetchScalarGridSpec(
        num_scalar_prefetch=1, grid=(R // tr,),
        in_specs=[blk, blk, blk, pl.BlockSpec((None, tr, C), lambda i, s: (s[0], i, 0)), pl.BlockSpec((3, tr, C), lambda i, s: (0, i, 0))],
        out_specs=(blk, blk, blk, blk),
    )
    return pl.pallas_call(body, name=name, out_shape=(out, out, out, out), grid_spec=grid_spec, compiler_params=_cp("parallel"),
                          )(chip_idx, w, m, v, psums, parts)


def _adamw_rows(w, m, v, g, name, tr):
    R = w.shape[0]

    def body(w_ref, m_ref, v_ref, g_ref, d_ref, m2_ref, v2_ref):
        d, m2, v2 = _adam_math(w_ref[...], g_ref[...], m_ref[...], v_ref[...])
        d_ref[...] = d
        m2_ref[...] = m2
        v2_ref[...] = v2

    blk = pl.BlockSpec((tr,) + w.shape[1:], lambda i: (i, 0, 0))
    out = jax.ShapeDtypeStruct(w.shape, F32)
    return pl.pallas_call(body, name=name, out_shape=(out, out, out), grid=(R // tr,), in_specs=[blk] * 4, out_specs=(blk, blk, blk),
                          compiler_params=_cp("parallel"))(w, m, v, g)


def _sum_parts(psums, parts, chip_idx, name, tr, tc):
    _, R, C = psums.shape

    def body(s_ref, o_ref, p_ref, g_ref):
        g_ref[...] = ((o_ref[...].astype(F32) + p_ref[0].astype(F32)) + p_ref[1].astype(F32)) + p_ref[2].astype(F32)

    grid_spec = pltpu.PrefetchScalarGridSpec(
        num_scalar_prefetch=1, grid=(R // tr, C // tc),
        in_specs=[pl.BlockSpec((None, tr, tc), lambda i, j, s: (s[0], i, j)), pl.BlockSpec((3, tr, tc), lambda i, j, s: (0, i, j))],
        out_specs=pl.BlockSpec((tr, tc), lambda i, j, s: (i, j)),
    )
    return pl.pallas_call(body, name=name, out_shape=jax.ShapeDtypeStruct((R, C), F32), grid_spec=grid_spec,
                          compiler_params=_cp("parallel", "parallel"))(chip_idx, psums, parts)


def _adamw_plain(w, m, v, g, name):
    def body(w_ref, m_ref, v_ref, g_ref, d_ref, m2_ref, v2_ref):
        d, m2, v2 = _adam_math(w_ref[...], g_ref[...], m_ref[...], v_ref[...])
        d_ref[...] = d
        m2_ref[...] = m2
        v2_ref[...] = v2

    out = jax.ShapeDtypeStruct(w.shape, F32)
    return pl.pallas_call(body, name=name, out_shape=(out, out, out))(w, m, v, g)


def _sum_devices(pack_all, name="sum_small"):
    def body(p_ref, o_ref):
        s = p_ref[0]
        for k in range(1, NDEV):
            s = s + p_ref[k]
        o_ref[...] = s

    return pl.pallas_call(body, name=name, out_shape=jax.ShapeDtypeStruct(pack_all.shape[1:], F32))(pack_all)


def _pair_add(g5, recv, c_idx, name, tr):
    _, _, R, C = g5.shape

    def body(c_ref, g_ref, r_ref, o_ref):
        o_ref[...] = (g_ref[...].astype(F32) + r_ref[...].astype(F32)).astype(BF16)

    grid_spec = pltpu.PrefetchScalarGridSpec(
        num_scalar_prefetch=1, grid=(4, R // tr),
        in_specs=[pl.BlockSpec((None, None, tr, C), lambda q, i, c: (q, c[0], i, 0)), pl.BlockSpec((None, tr, C), lambda q, i, c: (q, i, 0))],
        out_specs=pl.BlockSpec((None, tr, C), lambda q, i, c: (q, i, 0)),
    )
    return pl.pallas_call(
        body, name=name, out_shape=jax.ShapeDtypeStruct((4, R, C), BF16), grid_spec=grid_spec,
        compiler_params=_cp("parallel", "parallel"),
    )(c_idx, g5, recv)


_ANY = pl.BlockSpec(memory_space=pl.ANY)


def _mesh_pos():
    x, y, c = lax.axis_index("x"), lax.axis_index("y"), lax.axis_index("c")
    return x, y, c, [(1 - x, y), (x, 1 - y), (1 - x, 1 - y)]


def _all_gather(shards, name="gather_weights"):
    n = len(shards)

    def body(*refs):
        ins, outs = refs[:n], refs[n:2 * n]
        send, recv, loc = refs[2 * n + 1:]
        x, y, c, chips = _mesh_pos()
        me, sib = (x, y, c), (x, y, 1 - c)

        def cp(a, k, block, to, own=False):
            dst = outs[a].at[4 * block[0] + 2 * block[1] + block[2]]
            return pltpu.make_async_remote_copy(src_ref=ins[a] if own else dst, dst_ref=dst, send_sem=send.at[a, k],
                                                recv_sem=recv.at[a, k], device_id=to, device_id_type=MESH)

        north = c == 1
        handed = (jnp.where(north, 1 - x, x), jnp.where(north, y, 1 - y))
        hand_to = (jnp.where(north, x, 1 - x), jnp.where(north, 1 - y, y))
        local = [pltpu.make_async_copy(ins[a], outs[a].at[4 * x + 2 * y + c], loc.at[a]) for a in range(n)]
        first = []
        for a in range(n):
            local[a].start()
            first.append(cp(a, 0, me, sib, own=True))
            first += [cp(a, 1 + j, me, (*chip, c), own=True) for j, chip in enumerate(chips[:2])]
        for d in first:
            d.start()
        passed = []
        for a in range(n):
            for j, chip in enumerate(chips[:2]):
                cp(a, 1 + j, (*chip, c), me).wait_recv()
            passed.append(cp(a, 3, (*handed, c), (*hand_to, c)))
            passed += [cp(a, 4 + j, (*chip, c), sib) for j, chip in enumerate(chips[:2])]
            for d in passed[-3:]:
                d.start()
        for a in range(n):
            cp(a, 3, (*chips[2], c), me).wait_recv()
            passed.append(cp(a, 6, (*chips[2], c), sib))
            passed[-1].start()
        for a in range(n):
            cp(a, 0, sib, me).wait_recv()
            for j, chip in enumerate(chips):
                cp(a, 4 + j, (*chip, 1 - c), me).wait_recv()
        for d in first + passed:
            d.wait_send()
        for d in local:
            d.wait()
        refs[2 * n][...] = jnp.zeros_like(refs[2 * n])

    return pl.pallas_call(
        body, name=name,
        out_shape=tuple(jax.ShapeDtypeStruct((NDEV,) + s.shape, s.dtype) for s in shards) + (jax.ShapeDtypeStruct((8, LANE), F32),),
        in_specs=[_ANY] * n, out_specs=tuple([_ANY] * n) + (pl.BlockSpec(memory_space=pltpu.VMEM),),
        scratch_shapes=[pltpu.SemaphoreType.DMA((n, 7)), pltpu.SemaphoreType.DMA((n, 7)), pltpu.SemaphoreType.DMA((n,))],
    )(*shards)


def _pair_exchange(grads, name):
    n = len(grads)

    def body(*refs):
        ins, outs = refs[:n], refs[n:2 * n]
        send, recv = refs[2 * n:]
        x, y, c, _ = _mesh_pos()
        big = [pltpu.make_async_remote_copy(src_ref=ins[a].at[:, 1 - c], dst_ref=outs[a], send_sem=send.at[a], recv_sem=recv.at[a],
                                            device_id=(x, y, 1 - c), device_id_type=MESH) for a in range(n)]
        for d in big:
            d.start()
        for d in big:
            d.wait_recv()
        for d in big:
            d.wait_send()

    return pl.pallas_call(
        body, name=name, out_shape=tuple(jax.ShapeDtypeStruct((4,) + g.shape[2:], g.dtype) for g in grads),
        in_specs=[_ANY] * n, out_specs=tuple([_ANY] * n),
        scratch_shapes=[pltpu.SemaphoreType.DMA((n,)), pltpu.SemaphoreType.DMA((n,))],
    )(*grads)


def _gather_small(pack, name="gather_small"):
    def body(pk, pk_all, psend, precv, loc):
        x, y, c, chips = _mesh_pos()
        me_slot = 4 * x + 2 * y + c
        sib = (x, y, 1 - c)
        own = pltpu.make_async_copy(pk, pk_all.at[me_slot], loc)
        own.start()
        peers = [sib] + [(*chip, c) for chip in chips] + [(*chip, 1 - c) for chip in chips]
        small = [pltpu.make_async_remote_copy(src_ref=pk, dst_ref=pk_all.at[me_slot], send_sem=psend.at[k], recv_sem=precv.at[k],
                                              device_id=p, device_id_type=MESH) for k, p in enumerate(peers)]
        for d in small:
            d.start()
        for k, p in enumerate(peers):
            pltpu.make_async_remote_copy(src_ref=pk, dst_ref=pk_all.at[4 * p[0] + 2 * p[1] + p[2]], send_sem=psend.at[k],
                                         recv_sem=precv.at[k], device_id=p, device_id_type=MESH).wait_recv()
        for d in small:
            d.wait_send()
        own.wait()

    return pl.pallas_call(
        body, name=name, out_shape=jax.ShapeDtypeStruct((NDEV,) + pack.shape, pack.dtype), in_specs=[_ANY], out_specs=_ANY,
        scratch_shapes=[pltpu.SemaphoreType.DMA((7,)), pltpu.SemaphoreType.DMA((7,)), pltpu.SemaphoreType.DMA(())],
    )(pack)


def _main_row(g):
    return g if g < C_LR else g - RANK


def _window_pieces(lo, hi):
    out = []
    for a, b, where in ((lo, min(hi, C_LR), "main"), (max(lo, C_LR), min(hi, C_LR + RANK), "lr"), (max(lo, C_LR + RANK), hi, "main")):
        if a < b:
            out.append((a, b, where, _main_row(a) if where == "main" else a - C_LR))
    return out


def _assemble_w_in(windows, name="assemble_w_in"):
    edges = NDEV - 1

    def body(b_ref, main_ref, lr_ref, buf, ebuf, in_sems, out_sems, esems):
        def load(k):
            return pltpu.make_async_copy(b_ref.at[k], buf.at[k % 2], in_sems.at[k % 2])

        lr_ref[RANK:, :] = jnp.zeros((LANE - RANK, D), BF16)
        load(0).start()
        pending, edge_out = [], []
        for k in range(NDEV):
            s = k % 2
            load(k).wait()
            if k:
                ebuf[k - 1] = buf[1 - s, WSTEP:WWIN, :] + buf[s, 0:16, :]
                edge_out.append(pltpu.make_async_copy(ebuf.at[k - 1], main_ref.at[pl.ds(_main_row(WSTEP * k), 16)], esems.at[k - 1]))
                edge_out[-1].start()
                for d in pending:
                    d.wait()
            if k + 1 < NDEV:
                load(k + 1).start()
            pending = []
            lo = WSTEP * k + (16 if k else 0)
            hi = WSTEP * k + (WWIN if k == NDEV - 1 else WSTEP)
            for a, b, where, dst in _window_pieces(lo, hi):
                if where == "lr":
                    lr_ref[dst:dst + b - a, :] = buf[s, a - WSTEP * k:b - WSTEP * k, :]
                else:
                    pending.append(pltpu.make_async_copy(buf.at[s, pl.ds(a - WSTEP * k, b - a)], main_ref.at[pl.ds(dst, b - a)],
                                                         out_sems.at[2 * s + len(pending)]))
                    pending[-1].start()
        for d in pending + edge_out:
            d.wait()

    return pl.pallas_call(
        body, name=name,
        out_shape=(jax.ShapeDtypeStruct((NMAIN, D), BF16), jax.ShapeDtypeStruct((LANE, D), BF16)),
        in_specs=[_ANY], out_specs=(_ANY, pl.BlockSpec(memory_space=pltpu.VMEM)),
        scratch_shapes=[pltpu.VMEM((2, WWIN, D), BF16), pltpu.VMEM((edges, 16, D), BF16), pltpu.SemaphoreType.DMA((2,)),
                        pltpu.SemaphoreType.DMA((4,)), pltpu.SemaphoreType.DMA((edges,))],
        compiler_params=pltpu.CompilerParams(vmem_limit_bytes=VMEM_LIMIT),
    )(windows)


def _disassemble_w_in(d_main, d_lr, name="disassemble_w_in"):
    def body(main_ref, lr_ref, g_ref, buf, in_sems, out_sems):
        def loads(k):
            s, out = k % 2, []
            for a, b, where, src0 in _window_pieces(WSTEP * k, WSTEP * k + WWIN):
                if where == "main":
                    out.append(pltpu.make_async_copy(main_ref.at[pl.ds(src0, b - a)], buf.at[s, pl.ds(a - WSTEP * k, b - a)],
                                                     in_sems.at[2 * s + len(out)]))
            return out

        def store(k):
            return pltpu.make_async_copy(buf.at[k % 2], g_ref.at[k], out_sems.at[k % 2])

        for d in loads(0):
            d.start()
        for k in range(NDEV):
            for d in loads(k):
                d.wait()
            for a, b, where, src0 in _window_pieces(WSTEP * k, WSTEP * k + WWIN):
                if where == "lr":
                    buf[k % 2, a - WSTEP * k:b - WSTEP * k, :] = lr_ref[src0:src0 + b - a, :]
            if k:
                store(k - 1).wait()
            if k + 1 < NDEV:
                for d in loads(k + 1):
                    d.start()
            store(k).start()
        store(NDEV - 1).wait()

    return pl.pallas_call(
        body, name=name, out_shape=jax.ShapeDtypeStruct((NDEV, WWIN, D), BF16),
        in_specs=[_ANY, pl.BlockSpec(memory_space=pltpu.VMEM)], out_specs=_ANY,
        scratch_shapes=[pltpu.VMEM((2, WWIN, D), BF16), pltpu.SemaphoreType.DMA((4,)), pltpu.SemaphoreType.DMA((2,))],
        compiler_params=pltpu.CompilerParams(vmem_limit_bytes=VMEM_LIMIT),
    )(d_main, d_lr)


_HBM = pl.BlockSpec(memory_space=pltpu.HBM)
_SEM = pl.BlockSpec(memory_space=pltpu.SEMAPHORE)
_VMEM = pl.BlockSpec(memory_space=pltpu.VMEM)
_SIDE = pltpu.CompilerParams(has_side_effects=pltpu.SideEffectType.DATAFLOW_SIDE_EFFECTING)
_TOKEN = jax.ShapeDtypeStruct((8, LANE), F32)


def _hbm(a):
    return pltpu.with_memory_space_constraint(a, pltpu.HBM)


def _hbm_like(arrs):
    return tuple(pltpu.HBM(a.shape, a.dtype) for a in arrs)


def _tie(x, token):
    return x + token[0, 0].astype(x.dtype)


def _chip_copies(ins, lands, send, recv):
    x, y, c, chips = _mesh_pos()
    return [pltpu.make_async_remote_copy(src_ref=ins[a].at[2 * chip[0] + chip[1]], dst_ref=lands[a].at[j], send_sem=send.at[3 * a + j],
                                         recv_sem=recv.at[3 * a + j], device_id=(*chip, c), device_id_type=MESH)
            for a in range(len(ins)) for j, chip in enumerate(chips)]


def _chip_start(psums, name):
    n = len(psums)
    lands = [lax.empty((3,) + p.shape[1:], p.dtype) for p in psums]

    def body(*refs):
        for d in _chip_copies(refs[:n], refs[n:2 * n], refs[2 * n], refs[2 * n + 1]):
            d.start()
        refs[-1][...] = jnp.zeros_like(refs[-1])

    sems = pltpu.SemaphoreType.DMA((3 * n,))
    out = pl.pallas_call(
        body, name=name, out_shape=(sems, sems) + _hbm_like(psums) + _hbm_like(lands) + (_TOKEN,),
        in_specs=[_HBM] * (2 * n), out_specs=(_SEM, _SEM) + (_HBM,) * (2 * n) + (_VMEM,),
        input_output_aliases={i: 2 + i for i in range(2 * n)}, compiler_params=_SIDE,
    )(*[_hbm(a) for a in list(psums) + lands])
    return out[0], out[1], list(out[2:2 + n]), list(out[2 + n:2 + 2 * n]), out[-1]


def _chip_wait(send, recv, psums, lands, after, name):
    n = len(psums)

    def body(*refs):
        for d in _chip_copies(refs[:n], refs[n:2 * n], refs[2 * n], refs[2 * n + 1]):
            d.wait_send()
            d.wait_recv()

    out = pl.pallas_call(
        body, name=name, out_shape=_hbm_like(psums) + _hbm_like(lands),
        in_specs=[_HBM] * (2 * n) + [_SEM, _SEM, _ANY], out_specs=(_HBM,) * (2 * n),
        input_output_aliases={i: i for i in range(2 * n)}, compiler_params=_SIDE,
    )(*psums, *lands, send, recv, after)
    return list(out[:n]), list(out[n:])


def _pair_copies(ins, lands, send, recv):
    x, y, c, _ = _mesh_pos()
    return [pltpu.make_async_remote_copy(src_ref=ins[a].at[:, 1 - c], dst_ref=lands[a], send_sem=send.at[a], recv_sem=recv.at[a],
                                         device_id=(x, y, 1 - c), device_id_type=MESH) for a in range(len(ins))]


def _pair_start(grads, name):
    n = len(grads)
    lands = [lax.empty((4,) + g.shape[2:], g.dtype) for g in grads]

    def body(*refs):
        for d in _pair_copies(refs[:n], refs[n:2 * n], refs[2 * n], refs[2 * n + 1]):
            d.start()
        refs[-1][...] = jnp.zeros_like(refs[-1])

    sems = pltpu.SemaphoreType.DMA((n,))
    out = pl.pallas_call(
        body, name=name, out_shape=(sems, sems) + _hbm_like(grads) + _hbm_like(lands) + (_TOKEN,),
        in_specs=[_HBM] * (2 * n), out_specs=(_SEM, _SEM) + (_HBM,) * (2 * n) + (_VMEM,),
        input_output_aliases={i: 2 + i for i in range(2 * n)}, compiler_params=_SIDE,
    )(*[_hbm(a) for a in list(grads) + lands])
    return out[0], out[1], list(out[2:2 + n]), list(out[2 + n:2 + 2 * n]), out[-1]


def _pair_wait(send, recv, grads, lands, after, name):
    n = len(grads)

    def body(*refs):
        for d in _pair_copies(refs[:n], refs[n:2 * n], refs[2 * n], refs[2 * n + 1]):
            d.wait_send()
            d.wait_recv()

    out = pl.pallas_call(
        body, name=name, out_shape=_hbm_like(grads) + _hbm_like(lands),
        in_specs=[_HBM] * (2 * n) + [_SEM, _SEM, _ANY], out_specs=(_HBM,) * (2 * n),
        input_output_aliases={i: i for i in range(2 * n)}, compiler_params=_SIDE,
    )(*grads, *lands, send, recv, after)
    return list(out[:n]), list(out[n:])


def _slot(chip, c):
    return 4 * chip[0] + 2 * chip[1] + c


def _gather_start(shards, dev, name):
    n = len(shards)
    lands = [lax.dynamic_update_slice(lax.empty((NDEV,) + s.shape, s.dtype), s[None], (dev,) + (0,) * s.ndim) for s in shards]

    def body(*refs):
        src, land, send, recv = refs[:n], refs[n:2 * n], refs[2 * n], refs[2 * n + 1]
        x, y, c, chips = _mesh_pos()
        for a in range(n):
            for k, to in enumerate([(x, y, 1 - c)] + [(*chip, c) for chip in chips]):
                pltpu.make_async_remote_copy(src_ref=src[a], dst_ref=land[a].at[_slot((x, y), c)], send_sem=send.at[4 * a + k],
                                             recv_sem=recv.at[4 * a + k], device_id=to, device_id_type=MESH).start()
        refs[-1][...] = jnp.zeros_like(refs[-1])

    sems = pltpu.SemaphoreType.DMA((4 * n,))
    out = pl.pallas_call(
        body, name=name, out_shape=(sems, sems) + _hbm_like(shards) + _hbm_like(lands) + (_TOKEN,),
        in_specs=[_HBM] * (2 * n), out_specs=(_SEM, _SEM) + (_HBM,) * (2 * n) + (_VMEM,),
        input_output_aliases={i: 2 + i for i in range(2 * n)}, compiler_params=_SIDE,
    )(*[_hbm(a) for a in list(shards) + lands])
    return out[0], out[1], list(out[2:2 + n]), list(out[2 + n:2 + 2 * n]), out[-1]


def _gather_pass(lands, recv, after, name, first=0):
    n = len(lands)

    def body(*refs):
        land, recv1 = refs[:n], refs[n]
        send2, recv2 = refs[n + 2], refs[n + 3]
        x, y, c, chips = _mesh_pos()
        for a in range(n):
            for j, chip in enumerate(chips):
                blk = land[a].at[_slot(chip, c)]
                pltpu.make_async_remote_copy(src_ref=blk, dst_ref=blk, send_sem=send2.at[3 * a + j], recv_sem=recv1.at[4 * (first + a) + 1 + j],
                                             device_id=(*chip, c), device_id_type=MESH).wait_recv()
                pltpu.make_async_remote_copy(src_ref=blk, dst_ref=blk, send_sem=send2.at[3 * a + j], recv_sem=recv2.at[3 * a + j],
                                             device_id=(x, y, 1 - c), device_id_type=MESH).start()
        refs[-1][...] = jnp.zeros_like(refs[-1])

    sems = pltpu.SemaphoreType.DMA((3 * n,))
    out = pl.pallas_call(
        body, name=name, out_shape=(sems, sems) + _hbm_like(lands) + (_TOKEN,),
        in_specs=[_HBM] * n + [_SEM, _ANY], out_specs=(_SEM, _SEM) + (_HBM,) * n + (_VMEM,),
        input_output_aliases={i: 2 + i for i in range(n)}, compiler_params=_SIDE,
    )(*lands, recv, after)
    return out[0], out[1], list(out[2:2 + n]), out[-1]


def _gather_wait(shards, lands, send, recv, send2, recv2, after, name, first=0):
    n = len(lands)

    def body(*refs):
        src, land = refs[:n], refs[n:2 * n]
        send1, recv1, snd2, rcv2 = refs[2 * n:2 * n + 4]
        x, y, c, chips = _mesh_pos()
        sib = (x, y, 1 - c)
        for a in range(n):
            for k in range(4):
                pltpu.make_async_remote_copy(src_ref=src[a], dst_ref=land[a].at[_slot((x, y), c)], send_sem=send1.at[4 * (first + a) + k],
                                             recv_sem=recv1.at[4 * (first + a) + k], device_id=sib, device_id_type=MESH).wait_send()
            blk = land[a].at[_slot((x, y), 1 - c)]
            pltpu.make_async_remote_copy(src_ref=blk, dst_ref=blk, send_sem=send1.at[4 * (first + a)], recv_sem=recv1.at[4 * (first + a)],
                                         device_id=sib, device_id_type=MESH).wait_recv()
            for j, chip in enumerate(chips):
                mine, theirs = land[a].at[_slot(chip, c)], land[a].at[_slot(chip, 1 - c)]
                pltpu.make_async_remote_copy(src_ref=mine, dst_ref=mine, send_sem=snd2.at[3 * a + j], recv_sem=rcv2.at[3 * a + j],
                                             device_id=sib, device_id_type=MESH).wait_send()
                pltpu.make_async_remote_copy(src_ref=theirs, dst_ref=theirs, send_sem=snd2.at[3 * a + j], recv_sem=rcv2.at[3 * a + j],
                                             device_id=sib, device_id_type=MESH).wait_recv()

    out = pl.pallas_call(
        body, name=name, out_shape=_hbm_like(shards) + _hbm_like(lands),
        in_specs=[_HBM] * (2 * n) + [_SEM] * 4 + [_ANY], out_specs=(_HBM,) * (2 * n),
        input_output_aliases={i: i for i in range(2 * n)}, compiler_params=_SIDE,
    )(*shards, *lands, send, recv, send2, recv2, after)
    return list(out[n:])


def _pad_to(v, n):
    return jnp.pad(v, [(0, 0)] * (v.ndim - 1) + [(0, n - v.shape[-1])])


def _pack_small(n1, gb, sk, gn, n2, fn, extra=None):
    parts = [n1.reshape(-1), gb.reshape(-1), sk.reshape(-1), gn.reshape(-1), n2.reshape(-1), fn.reshape(-1)]
    flat = jnp.concatenate(parts + ([extra.reshape(-1)] if extra is not None else []))
    return _pad_to(flat, SMALL_N).reshape(SMALL_ROWS, LANE)


def _unpack_small(p):
    f = p.reshape(-1)
    return (f[S_N1:S_GB].reshape(1, D), f[S_GB:S_SK].reshape(1, GH * DK), f[S_SK:S_GN].reshape(1, NQ), f[S_GN:S_N2].reshape(1, DV),
            f[S_N2:S_FN].reshape(1, D), f[S_FN:S_LOSS].reshape(D))


class _NoComm:
    def __init__(self, wo, wg_all, wu_all, wd_all):
        self.rest = (wo, wg_all, wu_all, wd_all)

    def mixed(self, gla_o, gla_norm_w):
        return gla_norm_w

    def w_out(self, merged, norm2_w):
        return self.rest[0], norm2_w

    def w_up(self, v2):
        return self.rest[1], self.rest[2]

    def w_down(self, ff):
        return self.rest[3]

    def ffn_grads(self, d_wg, d_wu, d_wd):
        self.ffn = (d_wg, d_wu, d_wd)

    def ffn_reduce(self, dv2, norm2_w):
        return norm2_w

    def in_grads(self, d_wmain, d_wlr, d_wo, w_lr):
        self.inw = (d_wmain, d_wlr, d_wo)
        return w_lr


class _Comm:
    def __init__(self, rest_shards, dev, c_idx):
        self.c_idx = c_idx
        self.send, self.recv, self.shards, self.lands, self.token = _gather_start(rest_shards, dev, "gather_rest_start")

    def _pass(self, lo, hi, after, tag):
        send2, recv2, lands, token = _gather_pass(self.lands[lo:hi], self.recv, after, "gather_pass_" + tag, first=lo)
        self.passed = (lo, hi, send2, recv2, lands)
        return token

    def _wait(self, after, tag):
        lo, hi, send2, recv2, lands = self.passed
        return _gather_wait(self.shards[lo:hi], lands, self.send, self.recv, send2, recv2, after, "gather_wait_" + tag, first=lo)

    def mixed(self, gla_o, gla_norm_w):
        return _tie(gla_norm_w, self._pass(0, 1, gla_o, "out"))

    def w_out(self, merged, norm2_w):
        (wo_all,) = self._wait(merged, "out")
        return wo_all.reshape(D, D), _tie(norm2_w, self._pass(1, 3, merged, "up"))

    def w_up(self, v2):
        wg_all, wu_all = self._wait(v2, "up")
        self._pass(3, 4, v2, "down")
        return wg_all.reshape(FH, D), wu_all.reshape(FH, D)

    def w_down(self, ff):
        return self._wait(ff, "down")[0].reshape(FH, D)

    def _reduce(self, tag, names, grads, recv1, rows):
        psums = [_pair_add(g, r, self.c_idx, "pair_add_" + nm, tr) for g, r, nm, tr in zip(grads, recv1, names, rows)]
        *flight, token = _chip_start(psums, "reduce_chips_start_" + tag)
        return dict(tag=tag, names=names, rows=rows, flight=flight), token

    def ffn_grads(self, d_wg, d_wu, d_wd):
        self.ffn_pair = _pair_start([d.reshape(4, 2, FS, D) for d in (d_wg, d_wu, d_wd)], "reduce_pair_start_ffn")
        return self.ffn_pair[-1]

    def ffn_reduce(self, dv2, norm2_w):
        send, recv, grads, lands, _ = self.ffn_pair
        grads, recv1 = _pair_wait(send, recv, grads, lands, dv2, "reduce_pair_wait_ffn")
        self.ffn, token = self._reduce("ffn", ["w_ffn_gate", "w_ffn_up", "w_ffn_down"], grads, recv1, [176, 176, 176])
        return _tie(norm2_w, token)

    def in_grads(self, d_wmain, d_wlr, d_wo, w_lr):
        grads = [_disassemble_w_in(d_wmain, d_wlr).reshape(4, 2, WWIN, D), d_wo.reshape(4, 2, D // NDEV, D)]
        self.inw, token = self._reduce("in", ["w_in", "w_out"], grads, _pair_exchange(grads, "reduce_pair_in"), [808, 256])
        return _tie(w_lr, token)


def _local_step(xs, tgt, norm1_w, gla_gate_b, attn_sinks, gla_norm_w, norm2_w, fnw, w_main, w_lr, w2p, comm):
    u = _rmsnorm_fwd(xs, norm1_w, "norm1_fwd")
    proj = _mm(u, w_main, tb=True, tm=1024, tn=1280, tk=D, name="in_proj")
    plr = _mm(u, w_lr, tb=True, tm=1024, tn=LANE, tk=D, name="in_proj_lr")
    attn_o = _attn_fwd(proj, attn_sinks)
    gla_o, states = _gla_fwd(proj, plr, w2p, gla_gate_b)
    merged = _merge_fwd(attn_o, gla_o, proj, comm.mixed(gla_o, gla_norm_w))
    wo, norm2_w = comm.w_out(merged, norm2_w)
    h1 = _mm(merged, wo, tm=1024, tn=512, tk=D, res=xs, name="out_proj")
    v2 = _rmsnorm_fwd(h1, norm2_w, "norm2_fwd")
    wg_all, wu_all = comm.w_up(v2)
    fa, fb, ff = _ffn_up(v2, wg_all, wu_all)
    wd_all = comm.w_down(ff)
    h2 = _mm(ff, wd_all, tm=1024, tn=1024, tk=FH // 2, res=h1, name="ffn_down")
    dh2, dh2b, d_fnw, loss_part = _loss_head(h2, fnw, tgt)

    da, db = _ffn_dact(dh2b, wd_all, fa, fb)
    Tn = xs.shape[0]
    d_wd = _mm(ff, dh2b, ta=True, tm=512, tn=D, tk=Tn, out_dtype=BF16, name="ffn_dwd")
    d_wg = _mm(da, v2, ta=True, tm=512, tn=D, tk=Tn, out_dtype=BF16, name="ffn_dwg")
    d_wu = _mm(db, v2, ta=True, tm=512, tn=D, tk=Tn, out_dtype=BF16, name="ffn_dwu")
    dv2 = _mm(da, wg_all, tm=1024, tn=1024, tk=FH // 2, after=comm.ffn_grads(d_wg, d_wu, d_wd), name="ffn_dv2_gate")
    dv2 = _mm(db, wu_all, tm=1024, tn=1024, tk=FH // 2, res=dv2, name="ffn_dv2_up")
    norm2_w = comm.ffn_reduce(dv2, norm2_w)
    dh1, dh1b, d_n2 = _rmsnorm_bwd(dv2, h1, norm2_w, dh2, "norm2_bwd")
    dmerged = _mm(dh1b, wo, tb=True, tm=1024, tn=512, tk=D, name="out_proj_dx")
    d_wo = _mm(merged, dh1b, ta=True, tm=1024, tn=512, tk=xs.shape[0], out_dtype=BF16, name="out_proj_dw")
    d_attn, d_gla, d_gates, d_gnw = _merge_bwd(dmerged, attn_o, gla_o, proj, gla_norm_w)
    d_q, d_kv, d_sinks = _attn_bwd(proj, attn_sinks, attn_o, d_attn)
    d_gqk, d_gv, d_plr, d_w2p, d_gb = _gla_bwd(proj, plr, w2p, gla_gate_b, states, d_gla)
    dproj = jnp.concatenate([d_q, d_kv, d_gqk, d_gv, d_gates], axis=1)
    d_wmain = _mm(dproj, u, ta=True, tm=640, tn=D, tk=xs.shape[0], out_dtype=BF16, name="in_proj_dw")
    d_wlr = _mm(d_plr, u, ta=True, tm=LANE, tn=1024, tk=xs.shape[0], out_dtype=BF16, name="in_proj_lr_dw")
    du_lr = _mm(d_plr, comm.in_grads(d_wmain, d_wlr, d_wo, w_lr), tm=1024, tn=1024, tk=LANE, name="in_proj_lr_dx")
    du = _mm(dproj, w_main, tm=1024, tn=1024, tk=2560, res=du_lr, name="in_proj_dx")
    dx, _, d_n1 = _rmsnorm_bwd(du, xs, norm1_w, dh1, "norm1_bwd")
    return dx, loss_part, d_w2p, d_gb, d_sinks, d_gnw, d_n1, d_n2, d_fnw


def kernel(x, norm1_w, w_in, gla_gate_w2, gla_gate_b, attn_sinks, gla_norm_w, w_out, norm2_w, w_ffn_gate, w_ffn_up, w_ffn_down, final_norm_w, loss_target, m_norm1_w, m_w_in, m_gla_gate_w2, m_gla_gate_b, m_attn_sinks, m_gla_norm_w, m_w_out, m_norm2_w, m_w_ffn_gate, m_w_ffn_up, m_w_ffn_down, m_final_norm_w, v_norm1_w, v_w_in, v_gla_gate_w2, v_gla_gate_b, v_attn_sinks, v_gla_norm_w, v_w_out, v_norm2_w, v_w_ffn_gate, v_w_ffn_up, v_w_ffn_down, v_final_norm_w):
    xs, tgt = x[0], loss_target[0]
    fnw = final_norm_w.reshape(1, D)
    c_idx = lax.axis_index("c").astype(jnp.int32).reshape(1)
    dev = 4 * lax.axis_index("x") + 2 * lax.axis_index("y") + lax.axis_index("c")

    chip_idx = (2 * lax.axis_index("x") + lax.axis_index("y")).astype(jnp.int32).reshape(1)

    shift = (WS - WSTEP) * dev
    window = lax.dynamic_update_slice(jnp.zeros((WWIN, D), BF16), jnp.transpose(w_in[0]).astype(BF16), (shift, 0))
    win_all, w2_all, tok = _all_gather([window, gla_gate_w2[0]], name="gather_w_in")
    tr2 = lambda t: jnp.transpose(t[0])
    rest = [(w + tok[0, 0]).astype(BF16) for w in (w_out[0], tr2(w_ffn_gate), tr2(w_ffn_up), w_ffn_down[0])]
    comm = _Comm(rest, dev, c_idx)
    w_main, w_lr = _assemble_w_in(win_all)
    w2p =jnp.pad(jnp.transpose(w2_all, (1, 0, 2)).reshape(RANK, GH * DK), ((0, LANE - RANK), (0, 0)))

    dx, loss_part, d_w2p, d_gb, d_sinks, d_gnw, d_n1, d_n2, d_fnw = _local_step(
        xs, tgt, _tie(norm1_w, comm.token), gla_gate_b, attn_sinks, gla_norm_w, norm2_w, fnw, w_main, w_lr, w2p, comm)

    pack = jnp.concatenate([_pack_small(d_n1, d_gb, d_sinks, d_gnw, d_n2, d_fnw, loss_part),
                            d_w2p[:RANK].reshape(GW2_ROWS, LANE)], axis=0)
    small = _sum_devices(_gather_small(pack))

    big = {}
    after = dx
    for grp in (comm.ffn, comm.inw):
        psums, parts = _chip_wait(*grp["flight"], after, "reduce_chips_wait_" + grp["tag"])
        for nm, ps, pt, tr in zip(grp["names"], psums, parts, grp["rows"]):
            w, m, v = {"w_in": (w_in, m_w_in, v_w_in), "w_out": (w_out, m_w_out, v_w_out), "w_ffn_gate": (w_ffn_gate, m_w_ffn_gate, v_w_ffn_gate),
                       "w_ffn_up": (w_ffn_up, m_w_ffn_up, v_w_ffn_up), "w_ffn_down": (w_ffn_down, m_w_ffn_down, v_w_ffn_down)}[nm]
            if nm == "w_in":
                g_win = _sum_parts(ps, pt, chip_idx, "sum_w_in", tr, 1024)
                rows3 = lambda t: jnp.transpose(t[0]).reshape(WS, D // LANE, LANE)
                g3 = lax.dynamic_slice(g_win, (shift, 0), (WS, D)).reshape(WS, D // LANE, LANE)
                out3 = (g3,) + tuple(_adamw_rows(rows3(w), rows3(m), rows3(v), g3, "adamw_w_in", 178))
                big[nm] = [jnp.transpose(t.reshape(WS, D))[None] for t in out3]
            elif nm in ("w_ffn_gate", "w_ffn_up"):
                big[nm] = [jnp.transpose(t)[None] for t in _adamw(tr2(w), tr2(m), tr2(v), ps, pt, chip_idx, "adamw_" + nm, tr)]
            else:
                big[nm] = [t[None] for t in _adamw(w[0], m[0], v[0], ps, pt, chip_idx, "adamw_" + nm, tr)]
            after = big[nm][0]
    g_small = small[:SMALL_ROWS]
    sm = _adamw_plain(_pack_small(norm1_w, gla_gate_b, attn_sinks, gla_norm_w, norm2_w, final_norm_w),
                      _pack_small(m_norm1_w, m_gla_gate_b, m_attn_sinks, m_gla_norm_w, m_norm2_w, m_final_norm_w),
                      _pack_small(v_norm1_w, v_gla_gate_b, v_attn_sinks, v_gla_norm_w, v_norm2_w, v_final_norm_w), g_small, "adamw_small")
    g_w2 = lax.dynamic_slice_in_dim(small[SMALL_ROWS:].reshape(RANK, GH * DK), dev * LANE, LANE, axis=1)
    w2 = [g_w2[None]] + [t[None] for t in _adamw_plain(gla_gate_w2[0], m_gla_gate_w2[0], v_gla_gate_w2[0], g_w2, "adamw_w2")]
    loss = g_small.reshape(-1)[S_LOSS]

    sg, sd, sm2, sv2 = [_unpack_small(t) for t in (g_small,) + tuple(sm)]

    def group(i, s):
        return (s[0], big["w_in"][i], w2[i], s[1], s[2], s[3], big["w_out"][i], s[4], big["w_ffn_gate"][i], big["w_ffn_up"][i],
                big["w_ffn_down"][i], s[5])

    return (loss, dx[None], *group(0, sg), *group(1, sd), *group(2, sm2), *group(3, sv2))
```

```python
import functools

import jax
import jax.numpy as jnp
from jax import lax
from jax.experimental import pallas as pl
from jax.experimental.pallas import tpu as pltpu

F32, BF16 = jnp.float32, jnp.bfloat16
HIGHEST = lax.Precision.HIGHEST

D = 2048
HD, NQ, NKV, GRP, WIN = 64, 32, 4, 8, 128
GH, DK, DV, RANK, GC = 4, 256, 512, 16, 64
FH, NDEV = 5632, 8
FS = FH // NDEV
DIN = 12816
WS = DIN // NDEV
EPS = 1e-6
MASKV = -1e30
LANE = 128

C_AQ, C_AK, C_AV, C_GQ, C_GK, C_GV, C_GR, C_GA, C_GB, NMAIN = 0, 2048, 2304, 2560, 3584, 4608, 6656, 8704, 10752, 12800
C_LR = 6656
WSTEP, WWIN = 1600, 1616

LR, B1, B2, AEPS, WD, STEP = 0.001, 0.9, 0.999, 1e-08, 0.01, 10

S_N1, S_GB, S_SK, S_GN, S_N2, S_FN, S_LOSS, SMALL_N = 0, 2048, 3072, 3104, 3616, 5664, 7712, 8192
SMALL_ROWS = SMALL_N // LANE
GW2_ROWS = RANK * GH * DK // LANE
PACK_ROWS = SMALL_ROWS + GW2_ROWS

MESH = pl.DeviceIdType.MESH


def _dot(a, b, ta=False, tb=False, prec=None):
    dn = (((0,) if ta else (1,), (1,) if tb else (0,)), ((), ()))
    return lax.dot_general(a, b, dn, preferred_element_type=F32, precision=prec)


def _sigmoid(x):
    return 1.0 / (1.0 + jnp.exp(-x))


VMEM_LIMIT = 56 * 1024 * 1024


def _cp(*sem):
    return pltpu.CompilerParams(dimension_semantics=sem, vmem_limit_bytes=VMEM_LIMIT)


def _mm(a, b, *, ta=False, tb=False, tm, tn, tk, out_dtype=F32, res=None, after=None, name):
    M, K = (a.shape[1], a.shape[0]) if ta else a.shape
    N = b.shape[0] if tb else b.shape[1]
    tm, tn, tk = min(tm, M), min(tn, N), min(tk, K)
    nk = K // tk
    assert M % tm == 0 and N % tn == 0 and K % tk == 0
    a_spec = pl.BlockSpec((tk, tm), lambda i, j, k: (k, i)) if ta else pl.BlockSpec((tm, tk), lambda i, j, k: (i, k))
    b_spec = pl.BlockSpec((tn, tk), lambda i, j, k: (j, k)) if tb else pl.BlockSpec((tk, tn), lambda i, j, k: (k, j))
    o_spec = pl.BlockSpec((tm, tn), lambda i, j, k: (i, j))
    has_res = res is not None

    def body(*refs):
        a_ref, b_ref = refs[0], refs[1]
        r_ref = refs[2] if has_res else None
        o_ref = refs[2 + has_res + (after is not None)]
        p = _dot(a_ref[...].astype(BF16), b_ref[...].astype(BF16), ta, tb)
        if nk == 1:
            if has_res:
                p = p + r_ref[...]
            o_ref[...] = p.astype(out_dtype)
        else:
            acc = refs[-1]
            k = pl.program_id(2)

            @pl.when(k == 0)
            def _():
                acc[...] = (p + r_ref[...]) if has_res else p

            @pl.when(k > 0)
            def _():
                acc[...] += p

            @pl.when(k == nk - 1)
            def _():
                o_ref[...] = acc[...].astype(out_dtype)

    return pl.pallas_call(
        body, name=name,
        out_shape=jax.ShapeDtypeStruct((M, N), out_dtype),
        grid=(M // tm, N // tn, nk),
        in_specs=[a_spec, b_spec] + ([o_spec] if has_res else []) + ([pl.BlockSpec(memory_space=pl.ANY)] if after is not None else []),
        out_specs=o_spec,
        scratch_shapes=[pltpu.VMEM((tm, tn), F32)] if nk > 1 else [],
        compiler_params=_cp("parallel", "parallel", "arbitrary"),
    )(*((a, b) + ((res,) if has_res else ()) + ((after,) if after is not None else ())))


def _rmsnorm_fwd(x, w, name, tm=256):
    Tn = x.shape[0]

    def body(x_ref, w_ref, o_ref):
        xv = x_ref[...]
        r = lax.rsqrt(jnp.mean(xv * xv, axis=1, keepdims=True) + EPS)
        o_ref[...] = (xv * r * w_ref[...]).astype(BF16)

    return pl.pallas_call(
        body, name=name, out_shape=jax.ShapeDtypeStruct((Tn, D), BF16), grid=(Tn // tm,),
        in_specs=[pl.BlockSpec((tm, D), lambda i: (i, 0)), pl.BlockSpec((1, D), lambda i: (0, 0))],
        out_specs=pl.BlockSpec((tm, D), lambda i: (i, 0)), compiler_params=_cp("parallel"),
    )(x, w)


def _rmsnorm_bwd(dy, h, w, res, name, tm=256):
    Tn = h.shape[0]

    def body(dy_ref, h_ref, w_ref, res_ref, dh_ref, dhb_ref, dw_ref):
        hv, dyv = h_ref[...], dy_ref[...]
        r = lax.rsqrt(jnp.mean(hv * hv, axis=1, keepdims=True) + EPS)
        g = dyv * w_ref[...]
        dh = res_ref[...] + r * g - hv * (r * r * r * jnp.mean(g * hv, axis=1, keepdims=True))
        dh_ref[...] = dh
        dhb_ref[...] = dh.astype(BF16)
        part = jnp.sum(dyv * hv * r, axis=0, keepdims=True)

        @pl.when(pl.program_id(0) == 0)
        def _():
            dw_ref[...] = part

        @pl.when(pl.program_id(0) > 0)
        def _():
            dw_ref[...] += part

    row = pl.BlockSpec((tm, D), lambda i: (i, 0))
    vec = pl.BlockSpec((1, D), lambda i: (0, 0))
    return pl.pallas_call(
        body, name=name,
        out_shape=(jax.ShapeDtypeStruct((Tn, D), F32), jax.ShapeDtypeStruct((Tn, D), BF16), jax.ShapeDtypeStruct((1, D), F32)),
        grid=(Tn // tm,), in_specs=[row, row, vec, row], out_specs=(row, row, vec), compiler_params=_cp("arbitrary"),
    )(dy, h, w, res)


def _loss_head(h2, wf, tgt, name="loss_head", tm=256):
    Tn = h2.shape[0]

    def body(h_ref, w_ref, t_ref, dh_ref, dhb_ref, dw_ref, loss_ref):
        hv, wv = h_ref[...], w_ref[...]
        r = lax.rsqrt(jnp.mean(hv * hv, axis=1, keepdims=True) + EPS)
        hn = hv * r
        e = hn * wv - t_ref[...]
        dy = e * (1.0 / D)
        g = dy * wv
        dh = r * g - hv * (r * r * r * jnp.mean(g * hv, axis=1, keepdims=True))
        dh_ref[...] = dh
        dhb_ref[...] = dh.astype(BF16)
        part = jnp.sum(dy * hn, axis=0, keepdims=True)
        lpart = (0.5 / D) * jnp.sum(jnp.sum(e * e, axis=1, keepdims=True), axis=0, keepdims=True)

        @pl.when(pl.program_id(0) == 0)
        def _():
            dw_ref[...] = part
            loss_ref[...] = lpart

        @pl.when(pl.program_id(0) > 0)
        def _():
            dw_ref[...] += part
            loss_ref[...] += lpart

    row = pl.BlockSpec((tm, D), lambda i: (i, 0))
    vec = pl.BlockSpec((1, D), lambda i: (0, 0))
    one = pl.BlockSpec((1, 1), lambda i: (0, 0))
    return pl.pallas_call(
        body, name=name,
        out_shape=(jax.ShapeDtypeStruct((Tn, D), F32), jax.ShapeDtypeStruct((Tn, D), BF16), jax.ShapeDtypeStruct((1, D), F32),
                   jax.ShapeDtypeStruct((1, 1), F32)),
        grid=(Tn // tm,), in_specs=[row, vec, row], out_specs=(row, row, vec, one), compiler_params=_cp("arbitrary"),
    )(h2, wf, tgt)


def _attn_mask(n):
    qi = lax.broadcasted_iota(jnp.int32, (NKV, GRP * WIN, 2 * WIN), 1) % WIN
    ki = lax.broadcasted_iota(jnp.int32, (NKV, GRP * WIN, 2 * WIN), 2)
    rel = qi + WIN - ki
    return (rel >= 0) & (rel < WIN) & ((n > 0) | (ki >= WIN))


def _kv_heads(prev_ref, cur_ref):
    return jnp.stack([jnp.concatenate([prev_ref[:, h * HD:(h + 1) * HD], cur_ref[:, h * HD:(h + 1) * HD]], axis=0) for h in range(NKV)])


def _q_heads(ref):
    return jnp.stack([jnp.concatenate([ref[:, (h * GRP + g) * HD:(h * GRP + g + 1) * HD] for g in range(GRP)], axis=0) for h in range(NKV)])


def _attn_probs(q_ref, kc_ref, kp_ref, sink_ref, mask):
    kk = _kv_heads(kp_ref, kc_ref).astype(BF16)
    qs = _q_heads(q_ref).astype(BF16)
    s = jnp.einsum('hqd,hkd->hqk', qs, kk, preferred_element_type=F32) * (HD ** -0.5)
    s = jnp.where(mask, s, MASKV)
    sink = jnp.stack([jnp.concatenate([jnp.full((WIN, 1), sink_ref[0, h * GRP + g], F32) for g in range(GRP)], axis=0) for h in range(NKV)])
    m = jnp.maximum(jnp.max(s, axis=2, keepdims=True), sink)
    e = jnp.exp(s - m)
    es = jnp.exp(sink - m)
    inv = 1.0 / (jnp.sum(e, axis=2, keepdims=True) + es)
    return e * inv, es * inv, qs, kk


def _attn_specs(nb, last):
    cur = lambda n: jnp.minimum(n, last)
    prev = lambda n: jnp.maximum(jnp.minimum(n, last) - 1, 0)
    return [
        pl.BlockSpec((WIN, NQ * HD), lambda n: (cur(n), C_AQ // (NQ * HD))),
        pl.BlockSpec((WIN, NKV * HD), lambda n: (cur(n), C_AK // (NKV * HD))),
        pl.BlockSpec((WIN, NKV * HD), lambda n: (prev(n), C_AK // (NKV * HD))),
        pl.BlockSpec((WIN, NKV * HD), lambda n: (cur(n), C_AV // (NKV * HD))),
        pl.BlockSpec((WIN, NKV * HD), lambda n: (prev(n), C_AV // (NKV * HD))),
    ]


def _attn_fwd(proj, sinks, name="attn_fwd"):
    Tn = proj.shape[0]
    nb = Tn // WIN

    def body(q_ref, kc_ref, kp_ref, vc_ref, vp_ref, sink_ref, o_ref):
        p, _, _, _ = _attn_probs(q_ref, kc_ref, kp_ref, sink_ref, _attn_mask(pl.program_id(0)))
        o = jnp.einsum('hqk,hkd->hqd', p.astype(BF16), _kv_heads(vp_ref, vc_ref).astype(BF16), preferred_element_type=F32)
        for h in range(NKV):
            for g in range(GRP):
                o_ref[:, (h * GRP + g) * HD:(h * GRP + g + 1) * HD] = o[h, g * WIN:(g + 1) * WIN, :]

    return pl.pallas_call(
        body, name=name, out_shape=jax.ShapeDtypeStruct((Tn, D), F32), grid=(nb,),
        in_specs=_attn_specs(nb, nb - 1) + [pl.BlockSpec(memory_space=pltpu.SMEM)],
        out_specs=pl.BlockSpec((WIN, D), lambda n: (n, 0)), compiler_params=_cp("parallel"),
    )(proj, proj, proj, proj, proj, sinks)


def _attn_bwd(proj, sinks, o, do, name="attn_bwd"):
    Tn = proj.shape[0]
    nb = Tn // WIN
    KW = NKV * HD

    def body(q_ref, kc_ref, kp_ref, vc_ref, vp_ref, o_ref, do_ref, sink_ref, dq_ref, dkv_ref, dsk_ref, carry, cur):
        n = pl.program_id(0)

        @pl.when(n == 0)
        def _():
            carry[...] = jnp.zeros_like(carry)
            dsk_ref[...] = jnp.zeros_like(dsk_ref)

        @pl.when(n < nb)
        def _():
            p, ps, qs, kk = _attn_probs(q_ref, kc_ref, kp_ref, sink_ref, _attn_mask(n))
            vv = _kv_heads(vp_ref, vc_ref).astype(BF16)
            dos = _q_heads(do_ref)
            delta = jnp.sum(dos * _q_heads(o_ref), axis=2, keepdims=True)
            dosb = dos.astype(BF16)
            dp = jnp.einsum('hqd,hkd->hqk', dosb, vv, preferred_element_type=F32)
            ds = (p * (dp - delta) * (HD ** -0.5)).astype(BF16)
            dq = jnp.einsum('hqk,hkd->hqd', ds, kk, preferred_element_type=F32)
            dkk = jnp.einsum('hqk,hqd->hkd', ds, qs, preferred_element_type=F32)
            dvv = jnp.einsum('hqk,hqd->hkd', p.astype(BF16), dosb, preferred_element_type=F32)
            dsk = ps * delta
            for h in range(NKV):
                for g in range(GRP):
                    i = h * GRP + g
                    dq_ref[:, i * HD:(i + 1) * HD] = dq[h, g * WIN:(g + 1) * WIN, :].astype(BF16)
                    dsk_ref[:, i:i + 1] -= jnp.sum(dsk[h, g * WIN:(g + 1) * WIN, :], axis=0, keepdims=True)
                dkv_ref[:, h * HD:(h + 1) * HD] = (carry[:, h * HD:(h + 1) * HD] + dkk[h, :WIN, :]).astype(BF16)
                dkv_ref[:, KW + h * HD:KW + (h + 1) * HD] = (carry[:, KW + h * HD:KW + (h + 1) * HD] + dvv[h, :WIN, :]).astype(BF16)
                cur[:, h * HD:(h + 1) * HD] = dkk[h, WIN:, :]
                cur[:, KW + h * HD:KW + (h + 1) * HD] = dvv[h, WIN:, :]
            carry[...] = cur[...]

        @pl.when(n == nb)
        def _():
            dkv_ref[...] = carry[...].astype(BF16)

    last = nb - 1
    row = pl.BlockSpec((WIN, D), lambda n: (jnp.minimum(n, last), 0))
    return pl.pallas_call(
        body, name=name,
        out_shape=(jax.ShapeDtypeStruct((Tn, D), BF16), jax.ShapeDtypeStruct((Tn, 2 * KW), BF16), jax.ShapeDtypeStruct((1, NQ), F32)),
        grid=(nb + 1,),
        in_specs=_attn_specs(nb, last) + [row, row, pl.BlockSpec(memory_space=pltpu.SMEM)],
        out_specs=(row, pl.BlockSpec((WIN, 2 * KW), lambda n: (jnp.maximum(n - 1, 0), 0)), pl.BlockSpec((1, NQ), lambda n: (0, 0))),
        scratch_shapes=[pltpu.VMEM((WIN, 2 * KW), F32), pltpu.VMEM((WIN, 2 * KW), F32)],
        compiler_params=_cp("arbitrary"),
    )(proj, proj, proj, proj, proj, o, do, sinks)


def _tri(lower):
    r = lax.broadcasted_iota(jnp.int32, (GC, GC), 0)
    c = lax.broadcasted_iota(jnp.int32, (GC, GC), 1)
    return r >= c if lower else r <= c


def _per_head(a):
    return jnp.stack([a[:, h * DK:(h + 1) * DK] for h in range(GH)])


def _all_heads(a):
    return jnp.concatenate([a[h] for h in range(GH)], axis=1)


def _gla_gates(lr, w2_ref, gb_ref):
    logit = _dot(lr, w2_ref[...].astype(BF16)) + gb_ref[...]
    la = (jnp.minimum(logit, 0.0) - jnp.log(1.0 + jnp.exp(-jnp.abs(logit)))) * (1.0 / 16.0)
    g = _dot(_tri(True).astype(F32), la, prec=HIGHEST)
    return logit, g


def _bmm(spec, a, b):
    return jnp.einsum(spec, a, b, preferred_element_type=F32)


def _gla_specs(nc, rev):
    idx = (lambda n: nc - 1 - n) if rev else (lambda n: n)
    half = 2 * DK
    return (
        [pl.BlockSpec((GC, half), lambda n, j=j: (idx(n), C_GQ // half + j)) for j in range(2)]
        + [pl.BlockSpec((GC, half), lambda n, j=j: (idx(n), C_GK // half + j)) for j in range(2)]
        + [pl.BlockSpec((GC, DV), lambda n, h=h: (idx(n), C_GV // DV + h)) for h in range(GH)]
        + [pl.BlockSpec((GC, LANE), lambda n: (idx(n), 0)), pl.BlockSpec((LANE, GH * DK), lambda n: (0, 0)),
           pl.BlockSpec((1, GH * DK), lambda n: (0, 0))])


def _gla_heads(refs):
    return (lambda h: refs[h // 2][:, (h % 2) * DK:(h % 2 + 1) * DK], lambda h: refs[2 + h // 2][:, (h % 2) * DK:(h % 2 + 1) * DK],
            lambda h: refs[4 + h][...])


def _gla_fwd(proj, plr, w2p, gb, name="gla_fwd"):
    Tn = proj.shape[0]
    nc = Tn // GC

    def body(*refs):
        qh, kh, vh = _gla_heads(refs)
        lr_ref, w2_ref, gb_ref, o_ref, st_ref, S = refs[8:]

        @pl.when(pl.program_id(0) == 0)
        def _():
            S[...] = jnp.zeros_like(S)

        heads = lambda f: jnp.stack([f(h) for h in range(GH)])
        _, g_all = _gla_gates(lr_ref[...].astype(BF16), w2_ref, gb_ref)
        g = _per_head(g_all)
        gl = g[:, GC - 1:GC, :]
        k = heads(kh)
        v = heads(vh).astype(BF16)
        qd = (heads(qh) * (DK ** -0.5) * jnp.exp(g)).astype(BF16)
        ki = (k * jnp.exp(-g)).astype(BF16)
        ke = (k * jnp.exp(gl - g)).astype(BF16)
        att = jnp.where(_tri(True)[None], _bmm('hid,hjd->hij', qd, ki), 0.0).astype(BF16)
        sp = S[...]
        st_ref[0] = sp
        o = _bmm('hij,hjv->hiv', att, v) + _bmm('hid,hvd->hiv', qd, sp.astype(BF16))
        for h in range(GH):
            o_ref[:, h * DV:(h + 1) * DV] = o[h]
        S[...] = sp * jnp.exp(gl) + _bmm('hjv,hjd->hvd', v, ke)

    return pl.pallas_call(
        body, name=name,
        out_shape=(jax.ShapeDtypeStruct((Tn, GH * DV), F32), jax.ShapeDtypeStruct((nc, GH, DV, DK), F32)),
        grid=(nc,), in_specs=_gla_specs(nc, False),
        out_specs=(pl.BlockSpec((GC, GH * DV), lambda n: (n, 0)), pl.BlockSpec((1, GH, DV, DK), lambda n: (n, 0, 0, 0))),
        scratch_shapes=[pltpu.VMEM((GH, DV, DK), F32)], compiler_params=_cp("arbitrary"),
    )(*([proj] * 8), plr, w2p, gb)


def _gla_bwd(proj, plr, w2p, gb, states, do, name="gla_bwd"):
    Tn = proj.shape[0]
    nc = Tn // GC

    def body(*refs):
        qh, kh, vh = _gla_heads(refs)
        lr_ref, w2_ref, gb_ref, st_ref, do_ref, dqk_ref, dv_ref, dlr_ref, dw2_ref, dgb_ref, dS = refs[8:]

        @pl.when(pl.program_id(0) == 0)
        def _():
            dS[...] = jnp.zeros_like(dS)
            dw2_ref[...] = jnp.zeros_like(dw2_ref)
            dgb_ref[...] = jnp.zeros_like(dgb_ref)

        heads = lambda f: jnp.stack([f(h) for h in range(GH)])
        lr = lr_ref[...].astype(BF16)
        causal = _tri(True)[None]
        last_row = lax.broadcasted_iota(jnp.int32, (GH, GC, DK), 1) == GC - 1
        logit, g_all = _gla_gates(lr, w2_ref, gb_ref)
        g = _per_head(g_all)
        gl = g[:, GC - 1:GC, :]
        egl = jnp.exp(gl)
        eg, eng, ege = jnp.exp(g), jnp.exp(-g), jnp.exp(gl - g)
        k = heads(kh)
        v = heads(vh).astype(BF16)
        dob = heads(lambda h: do_ref[:, h * DV:(h + 1) * DV]).astype(BF16)
        qd = heads(qh) * (DK ** -0.5) * eg
        ki = k * eng
        ke = k * ege
        qdb, kib, keb = qd.astype(BF16), ki.astype(BF16), ke.astype(BF16)
        att = jnp.where(causal, _bmm('hid,hjd->hij', qdb, kib), 0.0).astype(BF16)
        datt = jnp.where(causal, _bmm('hiv,hjv->hij', dob, v), 0.0).astype(BF16)
        sp = st_ref[0]
        dsn = dS[...]
        dsnb = dsn.astype(BF16)
        dv = (_bmm('hij,hiv->hjv', att, dob) + _bmm('hjd,hvd->hjv', keb, dsnb)).astype(BF16)
        dqd = _bmm('hij,hjd->hid', datt, kib) + _bmm('hiv,hvd->hid', dob, sp.astype(BF16))
        dki = _bmm('hij,hid->hjd', datt, qdb)
        dke = _bmm('hjv,hvd->hjd', v, dsnb)
        ddec = jnp.sum(dsn * sp, axis=1, keepdims=True)
        dS[...] = dsn * egl + _bmm('hiv,hid->hvd', dob, qdb)
        dke_ke = dke * ke
        dgl = jnp.sum(dke_ke, axis=1, keepdims=True) + ddec * egl
        dg = dqd * qd - dki * ki - dke_ke + jnp.where(last_row, dgl, 0.0)
        dq = (dqd * ((DK ** -0.5) * eg)).astype(BF16)
        dk = (dki * eng + dke * ege).astype(BF16)
        for h in range(GH):
            dv_ref[:, h * DV:(h + 1) * DV] = dv[h]
            dqk_ref[:, h * DK:(h + 1) * DK] = dq[h]
            dqk_ref[:, GH * DK + h * DK:GH * DK + (h + 1) * DK] = dk[h]
        dla = _dot(_tri(False).astype(F32), _all_heads(dg), prec=HIGHEST)
        dlogit = dla * (1.0 / 16.0) * _sigmoid(-logit)
        dlb = dlogit.astype(BF16)
        dlr_ref[...] = _dot(dlb, w2_ref[...].astype(BF16), tb=True).astype(BF16)
        dw2_ref[...] += _dot(lr, dlb, ta=True)
        dgb_ref[...] += jnp.sum(dlogit, axis=0, keepdims=True)

    rev = lambda n: nc - 1 - n
    row = pl.BlockSpec((GC, GH * DV), lambda n: (rev(n), 0))
    return pl.pallas_call(
        body, name=name,
        out_shape=(jax.ShapeDtypeStruct((Tn, 2 * GH * DK), BF16), jax.ShapeDtypeStruct((Tn, GH * DV), BF16),
                   jax.ShapeDtypeStruct((Tn, LANE), BF16), jax.ShapeDtypeStruct((LANE, GH * DK), F32),
                   jax.ShapeDtypeStruct((1, GH * DK), F32)),
        grid=(nc,),
        in_specs=_gla_specs(nc, True) + [pl.BlockSpec((1, GH, DV, DK), lambda n: (rev(n), 0, 0, 0)), row],
        out_specs=(row, row, pl.BlockSpec((GC, LANE), lambda n: (rev(n), 0)), pl.BlockSpec((LANE, GH * DK), lambda n: (0, 0)),
                   pl.BlockSpec((1, GH * DK), lambda n: (0, 0))),
        scratch_shapes=[pltpu.VMEM((GH, DV, DK), F32)], compiler_params=_cp("arbitrary"),
    )(*([proj] * 8), plr, w2p, gb, states, do)


def _merge_specs(tm):
    row = pl.BlockSpec((tm, D), lambda i: (i, 0))
    gates = [pl.BlockSpec((tm, DV), lambda i, j=c // DV + h: (i, j)) for c in (C_GR, C_GA, C_GB) for h in range(GH)]
    return row, gates, pl.BlockSpec((1, DV), lambda i: (0, 0))


def _merge_fwd(a, go, proj, gnw, name="merge_fwd", tm=256):
    Tn = a.shape[0]

    def body(a_ref, go_ref, *rest):
        gates, w_ref, m_ref = rest[:3 * GH], rest[3 * GH], rest[3 * GH + 1]
        for h in range(GH):
            sl = slice(h * DV, (h + 1) * DV)
            gov = go_ref[:, sl]
            r = lax.rsqrt(jnp.mean(gov * gov, axis=1, keepdims=True) + EPS)
            gr = gates[h][...]
            g2 = gov * r * w_ref[...] * (gr * _sigmoid(gr))
            m_ref[:, sl] = (_sigmoid(gates[GH + h][...]) * a_ref[:, sl] + _sigmoid(gates[2 * GH + h][...]) * g2).astype(BF16)

    row, gates, vec = _merge_specs(tm)
    return pl.pallas_call(
        body, name=name, out_shape=jax.ShapeDtypeStruct((Tn, D), BF16), grid=(Tn // tm,),
        in_specs=[row, row] + gates + [vec], out_specs=row, compiler_params=_cp("parallel"),
    )(a, go, *([proj] * (3 * GH)), gnw)


def _merge_bwd(dm, a, go, proj, gnw, name="merge_bwd", tm=256):
    Tn = a.shape[0]

    def body(dm_ref, a_ref, go_ref, *rest):
        gates = rest[:3 * GH]
        w_ref, da_ref, dgo_ref, dg_ref, dw_ref = rest[3 * GH:]
        wv = w_ref[...]
        dw = jnp.zeros((1, DV), F32)
        for h in range(GH):
            sl = slice(h * DV, (h + 1) * DV)
            dmv, av, gov, gr = dm_ref[:, sl], a_ref[:, sl], go_ref[:, sl], gates[h][...]
            sa, sb, sg = _sigmoid(gates[GH + h][...]), _sigmoid(gates[2 * GH + h][...]), _sigmoid(gr)
            r = lax.rsqrt(jnp.mean(gov * gov, axis=1, keepdims=True) + EPS)
            gn0 = gov * r
            gn = gn0 * wv
            silu = gr * sg
            dg2 = dmv * sb
            da_ref[:, sl] = dmv * sa
            dg_ref[:, D + h * DV:D + (h + 1) * DV] = (dmv * av * sa * (1.0 - sa)).astype(BF16)
            dg_ref[:, 2 * D + h * DV:2 * D + (h + 1) * DV] = (dg2 * gn * silu * (1.0 - sb)).astype(BF16)
            dg_ref[:, sl] = (dg2 * gn * (sg * (1.0 + gr * (1.0 - sg)))).astype(BF16)
            dgn = dg2 * silu
            dw = dw + jnp.sum(dgn * gn0, axis=0, keepdims=True)
            gg = dgn * wv
            dgo_ref[:, sl] = r * gg - gov * (r * r * r * jnp.mean(gg * gov, axis=1, keepdims=True))

        @pl.when(pl.program_id(0) == 0)
        def _():
            dw_ref[...] = dw

        @pl.when(pl.program_id(0) > 0)
        def _():
            dw_ref[...] += dw

    row, gates, vec = _merge_specs(tm)
    return pl.pallas_call(
        body, name=name,
        out_shape=(jax.ShapeDtypeStruct((Tn, D), F32), jax.ShapeDtypeStruct((Tn, D), F32), jax.ShapeDtypeStruct((Tn, 3 * D), BF16),
                   jax.ShapeDtypeStruct((1, DV), F32)),
        grid=(Tn // tm,), in_specs=[row, row, row] + gates + [vec],
        out_specs=(row, row, pl.BlockSpec((tm, 3 * D), lambda i: (i, 0)), vec), compiler_params=_cp("arbitrary"),
    )(dm, a, go, *([proj] * (3 * GH)), gnw)


def _ffn_up(v2, wgt, wut, name="ffn_up", tm=1024, tn=512):
    Tn = v2.shape[0]
    tm = min(tm, Tn)

    def body(v_ref, wg_ref, wu_ref, a_ref, b_ref, ff_ref):
        vv = v_ref[...]
        a = _dot(vv, wg_ref[...], tb=True)
        b = _dot(vv, wu_ref[...], tb=True)
        a_ref[...] = a
        b_ref[...] = b
        ff_ref[...] = (a * _sigmoid(a) * b).astype(BF16)

    w = pl.BlockSpec((tn, D), lambda j, i: (j, 0))
    act = pl.BlockSpec((tm, tn), lambda j, i: (i, j))
    return pl.pallas_call(
        body, name=name,
        out_shape=(jax.ShapeDtypeStruct((Tn, FH), F32), jax.ShapeDtypeStruct((Tn, FH), F32), jax.ShapeDtypeStruct((Tn, FH), BF16)),
        grid=(FH // tn, Tn // tm), in_specs=[pl.BlockSpec((tm, D), lambda j, i: (i, 0)), w, w], out_specs=(act, act, act),
        compiler_params=_cp("parallel", "parallel"),
    )(v2, wgt, wut)


def _ffn_dact(dh2b, wd, a, b, name="ffn_dact", tm=1024, tn=512):
    Tn = dh2b.shape[0]
    tm = min(tm, Tn)

    def body(d_ref, w_ref, a_ref, b_ref, da_ref, db_ref):
        dff = _dot(d_ref[...], w_ref[...], tb=True)
        av = a_ref[...]
        sg = _sigmoid(av)
        da_ref[...] = (dff * b_ref[...] * (sg * (1.0 + av * (1.0 - sg)))).astype(BF16)
        db_ref[...] = (dff * (av * sg)).astype(BF16)

    act = pl.BlockSpec((tm, tn), lambda j, i: (i, j))
    return pl.pallas_call(
        body, name=name,
        out_shape=(jax.ShapeDtypeStruct((Tn, FH), BF16), jax.ShapeDtypeStruct((Tn, FH), BF16)),
        grid=(FH // tn, Tn // tm),
        in_specs=[pl.BlockSpec((tm, D), lambda j, i: (i, 0)), pl.BlockSpec((tn, D), lambda j, i: (j, 0)), act, act],
        out_specs=(act, act), compiler_params=_cp("parallel", "parallel"),
    )(dh2b, wd, a, b)


def _adam_math(w, g, m, v):
    m2 = B1 * m + (1.0 - B1) * g
    v2 = B2 * v + (1.0 - B2) * (g * g)
    mh = m2 / (1.0 - B1 ** STEP)
    vh = v2 / (1.0 - B2 ** STEP)
    return -LR * (mh / (jnp.sqrt(vh) + AEPS) + WD * w), m2, v2


def _sum_blocks(o_ref, p_ref):
    g = o_ref[...].astype(F32)
    for j in range(p_ref.shape[0]):
        g = g + p_ref[j].astype(F32)
    return g


def _adamw(w, m, v, psums, parts, chip_idx, name, tr):
    R, C = w.shape

    def body(s_ref, w_ref, m_ref, v_ref, o_ref, p_ref, g_ref, d_ref, m2_ref, v2_ref):
        g = _sum_blocks(o_ref, p_ref)
        d, m2, v2 = _adam_math(w_ref[...], g, m_ref[...], v_ref[...])
        g_ref[...] = g
        d_ref[...] = d
        m2_ref[...] = m2
        v2_ref[...] = v2

    blk = pl.BlockSpec((tr, C), lambda i, s: (i, 0))
    out = jax.ShapeDtypeStruct((R, C), F32)
    grid_spec = pltpu.PrefetchScalarGridSpec(
        num_scalar_prefetch=1, grid=(R // tr,),
        in_specs=[blk, blk, blk, pl.BlockSpec((None, tr, C), lambda i, s: (s[0], i, 0)),
                  pl.BlockSpec((parts.shape[0], tr, C), lambda i, s: (0, i, 0))],
        out_specs=(blk, blk, blk, blk),
    )
    return pl.pallas_call(body, name=name, out_shape=(out, out, out, out), grid_spec=grid_spec, compiler_params=_cp("parallel"),
                          )(chip_idx, w, m, v, psums, parts)


def _adamw_rows(w, m, v, g, name, tr):
    R = w.shape[0]

    def body(w_ref, m_ref, v_ref, g_ref, d_ref, m2_ref, v2_ref):
        d, m2, v2 = _adam_math(w_ref[...], g_ref[...], m_ref[...], v_ref[...])
        d_ref[...] = d
        m2_ref[...] = m2
        v2_ref[...] = v2

    blk = pl.BlockSpec((tr,) + w.shape[1:], lambda i: (i, 0, 0))
    out = jax.ShapeDtypeStruct(w.shape, F32)
    return pl.pallas_call(body, name=name, out_shape=(out, out, out), grid=(R // tr,), in_specs=[blk] * 4, out_specs=(blk, blk, blk),
                          compiler_params=_cp("parallel"))(w, m, v, g)


def _sum_parts(psums, parts, chip_idx, name, tr, tc):
    _, R, C = psums.shape

    def body(s_ref, o_ref, p_ref, g_ref):
        g_ref[...] = _sum_blocks(o_ref, p_ref)

    grid_spec = pltpu.PrefetchScalarGridSpec(
        num_scalar_prefetch=1, grid=(R // tr, C // tc),
        in_specs=[pl.BlockSpec((None, tr, tc), lambda i, j, s: (s[0], i, j)),
                  pl.BlockSpec((parts.shape[0], tr, tc), lambda i, j, s: (0, i, j))],
        out_specs=pl.BlockSpec((tr, tc), lambda i, j, s: (i, j)),
    )
    return pl.pallas_call(body, name=name, out_shape=jax.ShapeDtypeStruct((R, C), F32), grid_spec=grid_spec,
                          compiler_params=_cp("parallel", "parallel"))(chip_idx, psums, parts)


def _adamw_plain(w, m, v, g, name):
    def body(w_ref, m_ref, v_ref, g_ref, d_ref, m2_ref, v2_ref):
        d, m2, v2 = _adam_math(w_ref[...], g_ref[...], m_ref[...], v_ref[...])
        d_ref[...] = d
        m2_ref[...] = m2
        v2_ref[...] = v2

    out = jax.ShapeDtypeStruct(w.shape, F32)
    return pl.pallas_call(body, name=name, out_shape=(out, out, out))(w, m, v, g)


def _sum_devices(pack_all, name="sum_small"):
    def body(p_ref, o_ref):
        s = p_ref[0]
        for k in range(1, NDEV):
            s = s + p_ref[k]
        o_ref[...] = s

    return pl.pallas_call(body, name=name, out_shape=jax.ShapeDtypeStruct(pack_all.shape[1:], F32))(pack_all)


def _pair_add(g5, recv, c_idx, name, tr):
    _, _, R, C = g5.shape

    def body(c_ref, g_ref, r_ref, o_ref):
        o_ref[...] = (g_ref[...].astype(F32) + r_ref[...].astype(F32)).astype(BF16)

    grid_spec = pltpu.PrefetchScalarGridSpec(
        num_scalar_prefetch=1, grid=(4, R // tr),
        in_specs=[pl.BlockSpec((None, None, tr, C), lambda q, i, c: (q, c[0], i, 0)), pl.BlockSpec((None, tr, C), lambda q, i, c: (q, i, 0))],
        out_specs=pl.BlockSpec((None, tr, C), lambda q, i, c: (q, i, 0)),
    )
    return pl.pallas_call(
        body, name=name, out_shape=jax.ShapeDtypeStruct((4, R, C), BF16), grid_spec=grid_spec,
        compiler_params=_cp("parallel", "parallel"),
    )(c_idx, g5, recv)


_ANY = pl.BlockSpec(memory_space=pl.ANY)


def _mesh_pos():
    x, y, c = lax.axis_index("x"), lax.axis_index("y"), lax.axis_index("c")
    return x, y, c, [(1 - x, y), (x, 1 - y), (1 - x, 1 - y)]


def _all_gather(shards, name="gather_weights"):
    n = len(shards)

    def body(*refs):
        ins, outs = refs[:n], refs[n:2 * n]
        send, recv, loc = refs[2 * n + 1:]
        x, y, c, chips = _mesh_pos()
        me, sib = (x, y, c), (x, y, 1 - c)

        def cp(a, k, block, to, own=False):
            dst = outs[a].at[4 * block[0] + 2 * block[1] + block[2]]
            return pltpu.make_async_remote_copy(src_ref=ins[a] if own else dst, dst_ref=dst, send_sem=send.at[a, k],
                                                recv_sem=recv.at[a, k], device_id=to, device_id_type=MESH)

        north = c == 1
        handed = (jnp.where(north, 1 - x, x), jnp.where(north, y, 1 - y))
        hand_to = (jnp.where(north, x, 1 - x), jnp.where(north, 1 - y, y))
        local = [pltpu.make_async_copy(ins[a], outs[a].at[4 * x + 2 * y + c], loc.at[a]) for a in range(n)]
        first = []
        for a in range(n):
            local[a].start()
            first.append(cp(a, 0, me, sib, own=True))
            first += [cp(a, 1 + j, me, (*chip, c), own=True) for j, chip in enumerate(chips[:2])]
        for d in first:
            d.start()
        passed = []
        for a in range(n):
            for j, chip in enumerate(chips[:2]):
                cp(a, 1 + j, (*chip, c), me).wait_recv()
            passed.append(cp(a, 3, (*handed, c), (*hand_to, c)))
            passed += [cp(a, 4 + j, (*chip, c), sib) for j, chip in enumerate(chips[:2])]
            for d in passed[-3:]:
                d.start()
        for a in range(n):
            cp(a, 3, (*chips[2], c), me).wait_recv()
            passed.append(cp(a, 6, (*chips[2], c), sib))
            passed[-1].start()
        for a in range(n):
            cp(a, 0, sib, me).wait_recv()
            for j, chip in enumerate(chips):
                cp(a, 4 + j, (*chip, 1 - c), me).wait_recv()
        for d in first + passed:
            d.wait_send()
        for d in local:
            d.wait()
        refs[2 * n][...] = jnp.zeros_like(refs[2 * n])

    return pl.pallas_call(
        body, name=name,
        out_shape=tuple(jax.ShapeDtypeStruct((NDEV,) + s.shape, s.dtype) for s in shards) + (jax.ShapeDtypeStruct((8, LANE), F32),),
        in_specs=[_ANY] * n, out_specs=tuple([_ANY] * n) + (pl.BlockSpec(memory_space=pltpu.VMEM),),
        scratch_shapes=[pltpu.SemaphoreType.DMA((n, 7)), pltpu.SemaphoreType.DMA((n, 7)), pltpu.SemaphoreType.DMA((n,))],
    )(*shards)


def _pair_exchange(grads, name):
    n = len(grads)

    def body(*refs):
        ins, outs = refs[:n], refs[n:2 * n]
        send, recv = refs[2 * n:]
        x, y, c, _ = _mesh_pos()
        big = [pltpu.make_async_remote_copy(src_ref=ins[a].at[:, 1 - c], dst_ref=outs[a], send_sem=send.at[a], recv_sem=recv.at[a],
                                            device_id=(x, y, 1 - c), device_id_type=MESH) for a in range(n)]
        for d in big:
            d.start()
        for d in big:
            d.wait_recv()
        for d in big:
            d.wait_send()

    return pl.pallas_call(
        body, name=name, out_shape=tuple(jax.ShapeDtypeStruct((4,) + g.shape[2:], g.dtype) for g in grads),
        in_specs=[_ANY] * n, out_specs=tuple([_ANY] * n),
        scratch_shapes=[pltpu.SemaphoreType.DMA((n,)), pltpu.SemaphoreType.DMA((n,))],
    )(*grads)


def _gather_small(pack, name="gather_small"):
    def body(pk, pk_all, psend, precv, loc):
        x, y, c, chips = _mesh_pos()
        me_slot = 4 * x + 2 * y + c
        sib = (x, y, 1 - c)
        own = pltpu.make_async_copy(pk, pk_all.at[me_slot], loc)
        own.start()
        peers = [sib] + [(*chip, c) for chip in chips] + [(*chip, 1 - c) for chip in chips]
        small = [pltpu.make_async_remote_copy(src_ref=pk, dst_ref=pk_all.at[me_slot], send_sem=psend.at[k], recv_sem=precv.at[k],
                                              device_id=p, device_id_type=MESH) for k, p in enumerate(peers)]
        for d in small:
            d.start()
        for k, p in enumerate(peers):
            pltpu.make_async_remote_copy(src_ref=pk, dst_ref=pk_all.at[4 * p[0] + 2 * p[1] + p[2]], send_sem=psend.at[k],
                                         recv_sem=precv.at[k], device_id=p, device_id_type=MESH).wait_recv()
        for d in small:
            d.wait_send()
        own.wait()

    return pl.pallas_call(
        body, name=name, out_shape=jax.ShapeDtypeStruct((NDEV,) + pack.shape, pack.dtype), in_specs=[_ANY], out_specs=_ANY,
        scratch_shapes=[pltpu.SemaphoreType.DMA((7,)), pltpu.SemaphoreType.DMA((7,)), pltpu.SemaphoreType.DMA(())],
    )(pack)


def _main_row(g):
    return g if g < C_LR else g - RANK


def _window_pieces(lo, hi):
    out = []
    for a, b, where in ((lo, min(hi, C_LR), "main"), (max(lo, C_LR), min(hi, C_LR + RANK), "lr"), (max(lo, C_LR + RANK), hi, "main")):
        if a < b:
            out.append((a, b, where, _main_row(a) if where == "main" else a - C_LR))
    return out


def _assemble_w_in(windows, name="assemble_w_in"):
    edges = NDEV - 1

    def body(b_ref, main_ref, lr_ref, buf, ebuf, in_sems, out_sems, esems):
        def load(k):
            return pltpu.make_async_copy(b_ref.at[k], buf.at[k % 2], in_sems.at[k % 2])

        lr_ref[RANK:, :] = jnp.zeros((LANE - RANK, D), BF16)
        load(0).start()
        pending, edge_out = [], []
        for k in range(NDEV):
            s = k % 2
            load(k).wait()
            if k:
                ebuf[k - 1] = buf[1 - s, WSTEP:WWIN, :] + buf[s, 0:16, :]
                edge_out.append(pltpu.make_async_copy(ebuf.at[k - 1], main_ref.at[pl.ds(_main_row(WSTEP * k), 16)], esems.at[k - 1]))
                edge_out[-1].start()
                for d in pending:
                    d.wait()
            if k + 1 < NDEV:
                load(k + 1).start()
            pending = []
            lo = WSTEP * k + (16 if k else 0)
            hi = WSTEP * k + (WWIN if k == NDEV - 1 else WSTEP)
            for a, b, where, dst in _window_pieces(lo, hi):
                if where == "lr":
                    lr_ref[dst:dst + b - a, :] = buf[s, a - WSTEP * k:b - WSTEP * k, :]
                else:
                    pending.append(pltpu.make_async_copy(buf.at[s, pl.ds(a - WSTEP * k, b - a)], main_ref.at[pl.ds(dst, b - a)],
                                                         out_sems.at[2 * s + len(pending)]))
                    pending[-1].start()
        for d in pending + edge_out:
            d.wait()

    return pl.pallas_call(
        body, name=name,
        out_shape=(jax.ShapeDtypeStruct((NMAIN, D), BF16), jax.ShapeDtypeStruct((LANE, D), BF16)),
        in_specs=[_ANY], out_specs=(_ANY, pl.BlockSpec(memory_space=pltpu.VMEM)),
        scratch_shapes=[pltpu.VMEM((2, WWIN, D), BF16), pltpu.VMEM((edges, 16, D), BF16), pltpu.SemaphoreType.DMA((2,)),
                        pltpu.SemaphoreType.DMA((4,)), pltpu.SemaphoreType.DMA((edges,))],
        compiler_params=pltpu.CompilerParams(vmem_limit_bytes=VMEM_LIMIT),
    )(windows)


def _disassemble_w_in(d_main, d_lr, name="disassemble_w_in"):
    def body(main_ref, lr_ref, g_ref, buf, in_sems, out_sems):
        def loads(k):
            s, out = k % 2, []
            for a, b, where, src0 in _window_pieces(WSTEP * k, WSTEP * k + WWIN):
                if where == "main":
                    out.append(pltpu.make_async_copy(main_ref.at[pl.ds(src0, b - a)], buf.at[s, pl.ds(a - WSTEP * k, b - a)],
                                                     in_sems.at[2 * s + len(out)]))
            return out

        def store(k):
            return pltpu.make_async_copy(buf.at[k % 2], g_ref.at[k], out_sems.at[k % 2])

        for d in loads(0):
            d.start()
        for k in range(NDEV):
            for d in loads(k):
                d.wait()
            for a, b, where, src0 in _window_pieces(WSTEP * k, WSTEP * k + WWIN):
                if where == "lr":
                    buf[k % 2, a - WSTEP * k:b - WSTEP * k, :] = lr_ref[src0:src0 + b - a, :]
            if k:
                store(k - 1).wait()
            if k + 1 < NDEV:
                for d in loads(k + 1):
                    d.start()
            store(k).start()
        store(NDEV - 1).wait()

    return pl.pallas_call(
        body, name=name, out_shape=jax.ShapeDtypeStruct((NDEV, WWIN, D), BF16),
        in_specs=[_ANY, pl.BlockSpec(memory_space=pltpu.VMEM)], out_specs=_ANY,
        scratch_shapes=[pltpu.VMEM((2, WWIN, D), BF16), pltpu.SemaphoreType.DMA((4,)), pltpu.SemaphoreType.DMA((2,))],
        compiler_params=pltpu.CompilerParams(vmem_limit_bytes=VMEM_LIMIT),
    )(d_main, d_lr)


_HBM = pl.BlockSpec(memory_space=pltpu.HBM)
_SEM = pl.BlockSpec(memory_space=pltpu.SEMAPHORE)
_VMEM = pl.BlockSpec(memory_space=pltpu.VMEM)
_SIDE = pltpu.CompilerParams(has_side_effects=pltpu.SideEffectType.DATAFLOW_SIDE_EFFECTING)
_TOKEN = jax.ShapeDtypeStruct((8, LANE), F32)


def _hbm(a):
    return pltpu.with_memory_space_constraint(a, pltpu.HBM)


def _hbm_like(arrs):
    return tuple(pltpu.HBM(a.shape, a.dtype) for a in arrs)


def _tie(x, token):
    return x + token[0, 0].astype(x.dtype)


def _chip_copies(ins, lands, send, recv, nrel):
    x, y, c, chips = _mesh_pos()
    return [pltpu.make_async_remote_copy(src_ref=ins[a].at[2 * chip[0] + chip[1]], dst_ref=lands[a].at[j], send_sem=send.at[nrel * a + j],
                                         recv_sem=recv.at[nrel * a + j], device_id=(*chip, c), device_id_type=MESH)
            for a in range(len(ins)) for j, chip in enumerate(chips[:nrel])]


def _chip_start(psums, name, nrel=3):
    n = len(psums)
    lands = [lax.empty((nrel,) + p.shape[1:], p.dtype) for p in psums]

    def body(*refs):
        for d in _chip_copies(refs[:n], refs[n:2 * n], refs[2 * n], refs[2 * n + 1], nrel):
            d.start()
        refs[-1][...] = jnp.zeros_like(refs[-1])

    sems = pltpu.SemaphoreType.DMA((nrel * n,))
    out = pl.pallas_call(
        body, name=name, out_shape=(sems, sems) + _hbm_like(psums) + _hbm_like(lands) + (_TOKEN,),
        in_specs=[_HBM] * (2 * n), out_specs=(_SEM, _SEM) + (_HBM,) * (2 * n) + (_VMEM,),
        input_output_aliases={i: 2 + i for i in range(2 * n)}, compiler_params=_SIDE,
    )(*[_hbm(a) for a in list(psums) + lands])
    return out[0], out[1], list(out[2:2 + n]), list(out[2 + n:2 + 2 * n]), out[-1]


def _chip_wait(send, recv, psums, lands, after, name):
    n = len(psums)
    nrel = lands[0].shape[0]

    def body(*refs):
        for d in _chip_copies(refs[:n], refs[n:2 * n], refs[2 * n], refs[2 * n + 1], nrel):
            d.wait_send()
            d.wait_recv()

    out = pl.pallas_call(
        body, name=name, out_shape=_hbm_like(psums) + _hbm_like(lands),
        in_specs=[_HBM] * (2 * n) + [_SEM, _SEM, _ANY], out_specs=(_HBM,) * (2 * n),
        input_output_aliases={i: i for i in range(2 * n)}, compiler_params=_SIDE,
    )(*psums, *lands, send, recv, after)
    return list(out[:n]), list(out[n:])


def _hop_pos():
    x, y, c, _ = _mesh_pos()
    north = c == 1
    via = (jnp.where(north, 1 - x, x), jnp.where(north, y, 1 - y))
    return (*via, c), 2 * (1 - x) + (1 - y), jnp.where(north, 2 * x + (1 - y), 2 * (1 - x) + y)


def _hop_copies(ins, lands, send, recv):
    to, mine, _ = _hop_pos()
    return [pltpu.make_async_remote_copy(src_ref=ins[a].at[mine], dst_ref=lands[a], send_sem=send.at[a], recv_sem=recv.at[a],
                                         device_id=to, device_id_type=MESH) for a in range(len(ins))]


def _hop_start(psums, name):
    n = len(psums)
    lands = [lax.empty(p.shape[1:], p.dtype) for p in psums]

    def body(*refs):
        for d in _hop_copies(refs[:n], refs[n:2 * n], refs[2 * n], refs[2 * n + 1]):
            d.start()
        refs[-1][...] = jnp.zeros_like(refs[-1])

    sems = pltpu.SemaphoreType.DMA((n,))
    out = pl.pallas_call(
        body, name=name, out_shape=(sems, sems) + _hbm_like(psums) + _hbm_like(lands) + (_TOKEN,),
        in_specs=[_HBM] * (2 * n), out_specs=(_SEM, _SEM) + (_HBM,) * (2 * n) + (_VMEM,),
        input_output_aliases={i: 2 + i for i in range(2 * n)}, compiler_params=_SIDE,
    )(*[_hbm(a) for a in list(psums) + lands])
    return out[0], out[1], list(out[2:2 + n]), list(out[2 + n:2 + 2 * n]), out[-1]


def _hop_wait(send, recv, psums, lands, after, name):
    n = len(psums)

    def body(*refs):
        for d in _hop_copies(refs[:n], refs[n:2 * n], refs[2 * n], refs[2 * n + 1]):
            d.wait_send()
            d.wait_recv()

    out = pl.pallas_call(
        body, name=name, out_shape=_hbm_like(psums) + _hbm_like(lands),
        in_specs=[_HBM] * (2 * n) + [_SEM, _SEM, _ANY], out_specs=(_HBM,) * (2 * n),
        input_output_aliases={i: i for i in range(2 * n)}, compiler_params=_SIDE,
    )(*psums, *lands, send, recv, after)
    return list(out[:n]), list(out[n:])


def _hop_add(psums, land, idx, name, tr):
    _, R, C = psums.shape

    def body(s_ref, p_ref, l_ref, o_ref):
        o_ref[...] = (p_ref[...].astype(F32) + l_ref[...].astype(F32)).astype(BF16)

    blk = pl.BlockSpec((None, tr, C), lambda i, s: (s[0], i, 0))
    grid_spec = pltpu.PrefetchScalarGridSpec(num_scalar_prefetch=1, grid=(R // tr,),
                                             in_specs=[blk, pl.BlockSpec((tr, C), lambda i, s: (i, 0))], out_specs=blk)
    return pl.pallas_call(body, name=name, out_shape=jax.ShapeDtypeStruct(psums.shape, BF16), grid_spec=grid_spec,
                          input_output_aliases={1: 0}, compiler_params=_cp("parallel"))(idx, psums, land)


def _pair_copies(ins, lands, send, recv):
    x, y, c, _ = _mesh_pos()
    return [pltpu.make_async_remote_copy(src_ref=ins[a].at[:, 1 - c], dst_ref=lands[a], send_sem=send.at[a], recv_sem=recv.at[a],
                                         device_id=(x, y, 1 - c), device_id_type=MESH) for a in range(len(ins))]


def _pair_start(grads, name):
    n = len(grads)
    lands = [lax.empty((4,) + g.shape[2:], g.dtype) for g in grads]

    def body(*refs):
        for d in _pair_copies(refs[:n], refs[n:2 * n], refs[2 * n], refs[2 * n + 1]):
            d.start()
        refs[-1][...] = jnp.zeros_like(refs[-1])

    sems = pltpu.SemaphoreType.DMA((n,))
    out = pl.pallas_call(
        body, name=name, out_shape=(sems, sems) + _hbm_like(grads) + _hbm_like(lands) + (_TOKEN,),
        in_specs=[_HBM] * (2 * n), out_specs=(_SEM, _SEM) + (_HBM,) * (2 * n) + (_VMEM,),
        input_output_aliases={i: 2 + i for i in range(2 * n)}, compiler_params=_SIDE,
    )(*[_hbm(a) for a in list(grads) + lands])
    return out[0], out[1], list(out[2:2 + n]), list(out[2 + n:2 + 2 * n]), out[-1]


def _pair_wait(send, recv, grads, lands, after, name):
    n = len(grads)

    def body(*refs):
        for d in _pair_copies(refs[:n], refs[n:2 * n], refs[2 * n], refs[2 * n + 1]):
            d.wait_send()
            d.wait_recv()

    out = pl.pallas_call(
        body, name=name, out_shape=_hbm_like(grads) + _hbm_like(lands),
        in_specs=[_HBM] * (2 * n) + [_SEM, _SEM, _ANY], out_specs=(_HBM,) * (2 * n),
        input_output_aliases={i: i for i in range(2 * n)}, compiler_params=_SIDE,
    )(*grads, *lands, send, recv, after)
    return list(out[:n]), list(out[n:])


def _slot(chip, c):
    return 4 * chip[0] + 2 * chip[1] + c


def _gather_start(shards, dev, name):
    n = len(shards)
    lands = [lax.dynamic_update_slice(lax.empty((NDEV,) + s.shape, s.dtype), s[None], (dev,) + (0,) * s.ndim) for s in shards]

    def body(*refs):
        src, land, send, recv = refs[:n], refs[n:2 * n], refs[2 * n], refs[2 * n + 1]
        x, y, c, chips = _mesh_pos()
        for a in range(n):
            for k, to in enumerate([(x, y, 1 - c)] + [(*chip, c) for chip in chips]):
                pltpu.make_async_remote_copy(src_ref=src[a], dst_ref=land[a].at[_slot((x, y), c)], send_sem=send.at[4 * a + k],
                                             recv_sem=recv.at[4 * a + k], device_id=to, device_id_type=MESH).start()
        refs[-1][...] = jnp.zeros_like(refs[-1])

    sems = pltpu.SemaphoreType.DMA((4 * n,))
    out = pl.pallas_call(
        body, name=name, out_shape=(sems, sems) + _hbm_like(shards) + _hbm_like(lands) + (_TOKEN,),
        in_specs=[_HBM] * (2 * n), out_specs=(_SEM, _SEM) + (_HBM,) * (2 * n) + (_VMEM,),
        input_output_aliases={i: 2 + i for i in range(2 * n)}, compiler_params=_SIDE,
    )(*[_hbm(a) for a in list(shards) + lands])
    return out[0], out[1], list(out[2:2 + n]), list(out[2 + n:2 + 2 * n]), out[-1]


def _gather_pass(lands, recv, after, name, first=0):
    n = len(lands)

    def body(*refs):
        land, recv1 = refs[:n], refs[n]
        send2, recv2 = refs[n + 2], refs[n + 3]
        x, y, c, chips = _mesh_pos()
        for a in range(n):
            for j, chip in enumerate(chips):
                blk = land[a].at[_slot(chip, c)]
                pltpu.make_async_remote_copy(src_ref=blk, dst_ref=blk, send_sem=send2.at[3 * a + j], recv_sem=recv1.at[4 * (first + a) + 1 + j],
                                             device_id=(*chip, c), device_id_type=MESH).wait_recv()
                pltpu.make_async_remote_copy(src_ref=blk, dst_ref=blk, send_sem=send2.at[3 * a + j], recv_sem=recv2.at[3 * a + j],
                                             device_id=(x, y, 1 - c), device_id_type=MESH).start()
        refs[-1][...] = jnp.zeros_like(refs[-1])

    sems = pltpu.SemaphoreType.DMA((3 * n,))
    out = pl.pallas_call(
        body, name=name, out_shape=(sems, sems) + _hbm_like(lands) + (_TOKEN,),
        in_specs=[_HBM] * n + [_SEM, _ANY], out_specs=(_SEM, _SEM) + (_HBM,) * n + (_VMEM,),
        input_output_aliases={i: 2 + i for i in range(n)}, compiler_params=_SIDE,
    )(*lands, recv, after)
    return out[0], out[1], list(out[2:2 + n]), out[-1]


def _gather_wait(shards, lands, send, recv, send2, recv2, after, name, first=0):
    n = len(lands)

    def body(*refs):
        src, land = refs[:n], refs[n:2 * n]
        send1, recv1, snd2, rcv2 = refs[2 * n:2 * n + 4]
        x, y, c, chips = _mesh_pos()
        sib = (x, y, 1 - c)
        for a in range(n):
            for k in range(4):
                pltpu.make_async_remote_copy(src_ref=src[a], dst_ref=land[a].at[_slot((x, y), c)], send_sem=send1.at[4 * (first + a) + k],
                                             recv_sem=recv1.at[4 * (first + a) + k], device_id=sib, device_id_type=MESH).wait_send()
            blk = land[a].at[_slot((x, y), 1 - c)]
            pltpu.make_async_remote_copy(src_ref=blk, dst_ref=blk, send_sem=send1.at[4 * (first + a)], recv_sem=recv1.at[4 * (first + a)],
                                         device_id=sib, device_id_type=MESH).wait_recv()
            for j, chip in enumerate(chips):
                mine, theirs = land[a].at[_slot(chip, c)], land[a].at[_slot(chip, 1 - c)]
                pltpu.make_async_remote_copy(src_ref=mine, dst_ref=mine, send_sem=snd2.at[3 * a + j], recv_sem=rcv2.at[3 * a + j],
                                             device_id=sib, device_id_type=MESH).wait_send()
                pltpu.make_async_remote_copy(src_ref=theirs, dst_ref=theirs, send_sem=snd2.at[3 * a + j], recv_sem=rcv2.at[3 * a + j],
                                             device_id=sib, device_id_type=MESH).wait_recv()

    out = pl.pallas_call(
        body, name=name, out_shape=_hbm_like(shards) + _hbm_like(lands),
        in_specs=[_HBM] * (2 * n) + [_SEM] * 4 + [_ANY], out_specs=(_HBM,) * (2 * n),
        input_output_aliases={i: i for i in range(2 * n)}, compiler_params=_SIDE,
    )(*shards, *lands, send, recv, send2, recv2, after)
    return list(out[n:])


def _pad_to(v, n):
    return jnp.pad(v, [(0, 0)] * (v.ndim - 1) + [(0, n - v.shape[-1])])


def _pack_small(n1, gb, sk, gn, n2, fn, extra=None):
    parts = [n1.reshape(-1), gb.reshape(-1), sk.reshape(-1), gn.reshape(-1), n2.reshape(-1), fn.reshape(-1)]
    flat = jnp.concatenate(parts + ([extra.reshape(-1)] if extra is not None else []))
    return _pad_to(flat, SMALL_N).reshape(SMALL_ROWS, LANE)


def _unpack_small(p):
    f = p.reshape(-1)
    return (f[S_N1:S_GB].reshape(1, D), f[S_GB:S_SK].reshape(1, GH * DK), f[S_SK:S_GN].reshape(1, NQ), f[S_GN:S_N2].reshape(1, DV),
            f[S_N2:S_FN].reshape(1, D), f[S_FN:S_LOSS].reshape(D))


class _NoComm:
    def __init__(self, wo, wg_all, wu_all, wd_all):
        self.rest = (wo, wg_all, wu_all, wd_all)

    def mixed(self, gla_o, gla_norm_w):
        return gla_norm_w

    def w_out(self, merged, norm2_w):
        return self.rest[0], norm2_w

    def w_up(self, v2):
        return self.rest[1], self.rest[2]

    def w_down(self, ff):
        return self.rest[3]

    def ffn_grads(self, d_wg, d_wu, d_wd):
        self.ffn = (d_wg, d_wu, d_wd)

    def ffn_reduce(self, dv2, norm2_w):
        return norm2_w

    def in_grads(self, d_wmain, d_wlr, d_wo, w_lr):
        self.inw = (d_wmain, d_wlr, d_wo)
        return w_lr

    def in_reduce(self, du, norm1_w):
        return norm1_w


class _Comm:
    def __init__(self, rest_shards, dev, c_idx):
        self.c_idx = c_idx
        self.send, self.recv, self.shards, self.lands, self.token = _gather_start(rest_shards, dev, "gather_rest_start")

    def _pass(self, lo, hi, after, tag):
        send2, recv2, lands, token = _gather_pass(self.lands[lo:hi], self.recv, after, "gather_pass_" + tag, first=lo)
        self.passed = (lo, hi, send2, recv2, lands)
        return token

    def _wait(self, after, tag):
        lo, hi, send2, recv2, lands = self.passed
        return _gather_wait(self.shards[lo:hi], lands, self.send, self.recv, send2, recv2, after, "gather_wait_" + tag, first=lo)

    def mixed(self, gla_o, gla_norm_w):
        return _tie(gla_norm_w, self._pass(0, 1, gla_o, "out"))

    def w_out(self, merged, norm2_w):
        (wo_all,) = self._wait(merged, "out")
        return wo_all.reshape(D, D), _tie(norm2_w, self._pass(1, 3, merged, "up"))

    def w_up(self, v2):
        wg_all, wu_all = self._wait(v2, "up")
        self._pass(3, 4, v2, "down")
        return wg_all.reshape(FH, D), wu_all.reshape(FH, D)

    def w_down(self, ff):
        return self._wait(ff, "down")[0].reshape(FH, D)

    def _reduce(self, tag, names, grads, recv1, rows):
        psums = [_pair_add(g, r, self.c_idx, "pair_add_" + nm, tr) for g, r, nm, tr in zip(grads, recv1, names, rows)]
        *flight, token = _chip_start(psums, "reduce_chips_start_" + tag)
        return dict(tag=tag, names=names, rows=rows, flight=flight), token

    def ffn_grads(self, d_wg, d_wu, d_wd):
        self.ffn_pair = _pair_start([d.reshape(4, 2, FS, D) for d in (d_wg, d_wu, d_wd)], "reduce_pair_start_ffn")
        return self.ffn_pair[-1]

    def ffn_reduce(self, dv2, norm2_w):
        send, recv, grads, lands, _ = self.ffn_pair
        grads, recv1 = _pair_wait(send, recv, grads, lands, dv2, "reduce_pair_wait_ffn")
        self.ffn, token = self._reduce("ffn", ["w_ffn_gate", "w_ffn_up", "w_ffn_down"], grads, recv1, [176, 176, 176])
        return _tie(norm2_w, token)

    def in_grads(self, d_wmain, d_wlr, d_wo, w_lr):
        grads = [_disassemble_w_in(d_wmain, d_wlr).reshape(4, 2, WWIN, D), d_wo.reshape(4, 2, D // NDEV, D)]
        self.in_names, self.in_rows = ["w_in", "w_out"], [808, 256]
        psums = [_pair_add(g, r, self.c_idx, "pair_add_" + nm, tr)
                 for g, r, nm, tr in zip(grads, _pair_exchange(grads, "reduce_pair_in"), self.in_names, self.in_rows)]
        *self.in_hop, token = _hop_start(psums, "reduce_hop_start_in")
        return _tie(w_lr, token)

    def in_reduce(self, du, norm1_w):
        psums, lands = _hop_wait(*self.in_hop, du, "reduce_hop_wait_in")
        idx = _hop_pos()[2].astype(jnp.int32).reshape(1)
        psums = [_hop_add(p, l, idx, "hop_add_" + nm, tr) for p, l, nm, tr in zip(psums, lands, self.in_names, self.in_rows)]
        *flight, token = _chip_start(psums, "reduce_chips_start_in", nrel=2)
        self.inw = dict(tag="in", names=self.in_names, rows=self.in_rows, flight=flight)
        return _tie(norm1_w, token)


def _local_step(xs, tgt, norm1_w, gla_gate_b, attn_sinks, gla_norm_w, norm2_w, fnw, w_main, w_lr, w2p, comm):
    u = _rmsnorm_fwd(xs, norm1_w, "norm1_fwd")
    proj = _mm(u, w_main, tb=True, tm=1024, tn=1280, tk=D, name="in_proj")
    plr = _mm(u, w_lr, tb=True, tm=1024, tn=LANE, tk=D, name="in_proj_lr")
    attn_o = _attn_fwd(proj, attn_sinks)
    gla_o, states = _gla_fwd(proj, plr, w2p, gla_gate_b)
    merged = _merge_fwd(attn_o, gla_o, proj, comm.mixed(gla_o, gla_norm_w))
    wo, norm2_w = comm.w_out(merged, norm2_w)
    h1 = _mm(merged, wo, tm=1024, tn=512, tk=D, res=xs, name="out_proj")
    v2 = _rmsnorm_fwd(h1, norm2_w, "norm2_fwd")
    wg_all, wu_all = comm.w_up(v2)
    fa, fb, ff = _ffn_up(v2, wg_all, wu_all)
    wd_all = comm.w_down(ff)
    h2 = _mm(ff, wd_all, tm=1024, tn=1024, tk=FH // 2, res=h1, name="ffn_down")
    dh2, dh2b, d_fnw, loss_part = _loss_head(h2, fnw, tgt)

    da, db = _ffn_dact(dh2b, wd_all, fa, fb)
    Tn = xs.shape[0]
    d_wd = _mm(ff, dh2b, ta=True, tm=512, tn=D, tk=Tn, out_dtype=BF16, name="ffn_dwd")
    d_wg = _mm(da, v2, ta=True, tm=512, tn=D, tk=Tn, out_dtype=BF16, name="ffn_dwg")
    d_wu = _mm(db, v2, ta=True, tm=512, tn=D, tk=Tn, out_dtype=BF16, name="ffn_dwu")
    dv2 = _mm(da, wg_all, tm=1024, tn=1024, tk=FH // 2, after=comm.ffn_grads(d_wg, d_wu, d_wd), name="ffn_dv2_gate")
    dv2 = _mm(db, wu_all, tm=1024, tn=1024, tk=FH // 2, res=dv2, name="ffn_dv2_up")
    norm2_w = comm.ffn_reduce(dv2, norm2_w)
    dh1, dh1b, d_n2 = _rmsnorm_bwd(dv2, h1, norm2_w, dh2, "norm2_bwd")
    dmerged = _mm(dh1b, wo, tb=True, tm=1024, tn=512, tk=D, name="out_proj_dx")
    d_wo = _mm(merged, dh1b, ta=True, tm=1024, tn=512, tk=xs.shape[0], out_dtype=BF16, name="out_proj_dw")
    d_attn, d_gla, d_gates, d_gnw = _merge_bwd(dmerged, attn_o, gla_o, proj, gla_norm_w)
    d_q, d_kv, d_sinks = _attn_bwd(proj, attn_sinks, attn_o, d_attn)
    d_gqk, d_gv, d_plr, d_w2p, d_gb = _gla_bwd(proj, plr, w2p, gla_gate_b, states, d_gla)
    dproj = jnp.concatenate([d_q, d_kv, d_gqk, d_gv, d_gates], axis=1)
    d_wmain = _mm(dproj, u, ta=True, tm=640, tn=D, tk=xs.shape[0], out_dtype=BF16, name="in_proj_dw")
    d_wlr = _mm(d_plr, u, ta=True, tm=LANE, tn=1024, tk=xs.shape[0], out_dtype=BF16, name="in_proj_lr_dw")
    du_lr = _mm(d_plr, comm.in_grads(d_wmain, d_wlr, d_wo, w_lr), tm=1024, tn=1024, tk=LANE, name="in_proj_lr_dx")
    du = _mm(dproj, w_main, tm=1024, tn=1024, tk=2560, res=du_lr, name="in_proj_dx")
    dx, _, d_n1 = _rmsnorm_bwd(du, xs, comm.in_reduce(du, norm1_w), dh1, "norm1_bwd")
    return dx, loss_part, d_w2p, d_gb, d_sinks, d_gnw, d_n1, d_n2, d_fnw


def kernel(x, norm1_w, w_in, gla_gate_w2, gla_gate_b, attn_sinks, gla_norm_w, w_out, norm2_w, w_ffn_gate, w_ffn_up, w_ffn_down, final_norm_w, loss_target, m_norm1_w, m_w_in, m_gla_gate_w2, m_gla_gate_b, m_attn_sinks, m_gla_norm_w, m_w_out, m_norm2_w, m_w_ffn_gate, m_w_ffn_up, m_w_ffn_down, m_final_norm_w, v_norm1_w, v_w_in, v_gla_gate_w2, v_gla_gate_b, v_attn_sinks, v_gla_norm_w, v_w_out, v_norm2_w, v_w_ffn_gate, v_w_ffn_up, v_w_ffn_down, v_final_norm_w):
    xs, tgt = x[0], loss_target[0]
    fnw = final_norm_w.reshape(1, D)
    c_idx = lax.axis_index("c").astype(jnp.int32).reshape(1)
    dev = 4 * lax.axis_index("x") + 2 * lax.axis_index("y") + lax.axis_index("c")

    chip_idx = (2 * lax.axis_index("x") + lax.axis_index("y")).astype(jnp.int32).reshape(1)

    shift = (WS - WSTEP) * dev
    window = lax.dynamic_update_slice(jnp.zeros((WWIN, D), BF16), jnp.transpose(w_in[0]).astype(BF16), (shift, 0))
    win_all, w2_all, tok = _all_gather([window, gla_gate_w2[0]], name="gather_w_in")
    tr2 = lambda t: jnp.transpose(t[0])
    rest = [(w + tok[0, 0]).astype(BF16) for w in (w_out[0], tr2(w_ffn_gate), tr2(w_ffn_up), w_ffn_down[0])]
    comm = _Comm(rest, dev, c_idx)
    w_main, w_lr = _assemble_w_in(win_all)
    w2p =jnp.pad(jnp.transpose(w2_all, (1, 0, 2)).reshape(RANK, GH * DK), ((0, LANE - RANK), (0, 0)))

    dx, loss_part, d_w2p, d_gb, d_sinks, d_gnw, d_n1, d_n2, d_fnw = _local_step(
        xs, tgt, _tie(norm1_w, comm.token), gla_gate_b, attn_sinks, gla_norm_w, norm2_w, fnw, w_main, w_lr, w2p, comm)

    pack = jnp.concatenate([_pack_small(d_n1, d_gb, d_sinks, d_gnw, d_n2, d_fnw, loss_part),
                            d_w2p[:RANK].reshape(GW2_ROWS, LANE)], axis=0)
    small = _sum_devices(_gather_small(pack))

    big = {}
    after = dx
    for grp in (comm.ffn, comm.inw):
        psums, parts = _chip_wait(*grp["flight"], after, "reduce_chips_wait_" + grp["tag"])
        for nm, ps, pt, tr in zip(grp["names"], psums, parts, grp["rows"]):
            w, m, v = {"w_in": (w_in, m_w_in, v_w_in), "w_out": (w_out, m_w_out, v_w_out), "w_ffn_gate": (w_ffn_gate, m_w_ffn_gate, v_w_ffn_gate),
                       "w_ffn_up": (w_ffn_up, m_w_ffn_up, v_w_ffn_up), "w_ffn_down": (w_ffn_down, m_w_ffn_down, v_w_ffn_down)}[nm]
            if nm == "w_in":
                g_win = _sum_parts(ps, pt, chip_idx, "sum_w_in", tr, 1024)
                rows3 = lambda t: jnp.transpose(t[0]).reshape(WS, D // LANE, LANE)
                g3 = lax.dynamic_slice(g_win, (shift, 0), (WS, D)).reshape(WS, D // LANE, LANE)
                out3 = (g3,) + tuple(_adamw_rows(rows3(w), rows3(m), rows3(v), g3, "adamw_w_in", 178))
                big[nm] = [jnp.transpose(t.reshape(WS, D))[None] for t in out3]
            elif nm in ("w_ffn_gate", "w_ffn_up"):
                big[nm] = [jnp.transpose(t)[None] for t in _adamw(tr2(w), tr2(m), tr2(v), ps, pt, chip_idx, "adamw_" + nm, tr)]
            else:
                big[nm] = [t[None] for t in _adamw(w[0], m[0], v[0], ps, pt, chip_idx, "adamw_" + nm, tr)]
            after = big[nm][0]
    g_small = small[:SMALL_ROWS]
    sm = _adamw_plain(_pack_small(norm1_w, gla_gate_b, attn_sinks, gla_norm_w, norm2_w, final_norm_w),
                      _pack_small(m_norm1_w, m_gla_gate_b, m_attn_sinks, m_gla_norm_w, m_norm2_w, m_final_norm_w),
                      _pack_small(v_norm1_w, v_gla_gate_b, v_attn_sinks, v_gla_norm_w, v_norm2_w, v_final_norm_w), g_small, "adamw_small")
    g_w2 = lax.dynamic_slice_in_dim(small[SMALL_ROWS:].reshape(RANK, GH * DK), dev * LANE, LANE, axis=1)
    w2 = [g_w2[None]] + [t[None] for t in _adamw_plain(gla_gate_w2[0], m_gla_gate_w2[0], v_gla_gate_w2[0], g_w2, "adamw_w2")]
    loss = g_small.reshape(-1)[S_LOSS]

    sg, sd, sm2, sv2 = [_unpack_small(t) for t in (g_small,) + tuple(sm)]

    def group(i, s):
        return (s[0], big["w_in"][i], w2[i], s[1], s[2], s[3], big["w_out"][i], s[4], big["w_ffn_gate"][i], big["w_ffn_up"][i],
                big["w_ffn_down"][i], s[5])

    return (loss, dx[None], *group(0, sg), *group(1, sd), *group(2, sm2), *group(3, sv2))
```

```python
import functools

import jax
import jax.numpy as jnp
from jax import lax
from jax.experimental import pallas as pl
from jax.experimental.pallas import tpu as pltpu

F32, BF16 = jnp.float32, jnp.bfloat16
HIGHEST = lax.Precision.HIGHEST

D = 2048
HD, NQ, NKV, GRP, WIN = 64, 32, 4, 8, 128
GH, DK, DV, RANK, GC = 4, 256, 512, 16, 64
FH, NDEV = 5632, 8
FS = FH // NDEV
DIN = 12816
WS = DIN // NDEV
EPS = 1e-6
MASKV = -1e30
LANE = 128

C_AQ, C_AK, C_AV, C_GQ, C_GK, C_GV, C_GR, C_GA, C_GB, NMAIN = 0, 2048, 2304, 2560, 3584, 4608, 6656, 8704, 10752, 12800
C_LR = 6656
WSTEP, WWIN = 1600, 1616

LR, B1, B2, AEPS, WD, STEP = 0.001, 0.9, 0.999, 1e-08, 0.01, 10

S_N1, S_GB, S_SK, S_GN, S_N2, S_FN, S_LOSS, SMALL_N = 0, 2048, 3072, 3104, 3616, 5664, 7712, 8192
SMALL_ROWS = SMALL_N // LANE
GW2_ROWS = RANK * GH * DK // LANE
PACK_ROWS = SMALL_ROWS + GW2_ROWS

MESH = pl.DeviceIdType.MESH


def _dot(a, b, ta=False, tb=False, prec=None):
    dn = (((0,) if ta else (1,), (1,) if tb else (0,)), ((), ()))
    return lax.dot_general(a, b, dn, preferred_element_type=F32, precision=prec)


def _sigmoid(x):
    return 1.0 / (1.0 + jnp.exp(-x))


VMEM_LIMIT = 56 * 1024 * 1024


def _cp(*sem):
    return pltpu.CompilerParams(dimension_semantics=sem, vmem_limit_bytes=VMEM_LIMIT)


def _mm(a, b, *, ta=False, tb=False, tm, tn, tk, out_dtype=F32, res=None, after=None, name):
    M, K = (a.shape[1], a.shape[0]) if ta else a.shape
    N = b.shape[0] if tb else b.shape[1]
    tm, tn, tk = min(tm, M), min(tn, N), min(tk, K)
    nk = K // tk
    assert M % tm == 0 and N % tn == 0 and K % tk == 0
    a_spec = pl.BlockSpec((tk, tm), lambda i, j, k: (k, i)) if ta else pl.BlockSpec((tm, tk), lambda i, j, k: (i, k))
    b_spec = pl.BlockSpec((tn, tk), lambda i, j, k: (j, k)) if tb else pl.BlockSpec((tk, tn), lambda i, j, k: (k, j))
    o_spec = pl.BlockSpec((tm, tn), lambda i, j, k: (i, j))
    has_res = res is not None

    def body(*refs):
        a_ref, b_ref = refs[0], refs[1]
        r_ref = refs[2] if has_res else None
        o_ref = refs[2 + has_res + (after is not None)]
        p = _dot(a_ref[...].astype(BF16), b_ref[...].astype(BF16), ta, tb)
        if nk == 1:
            if has_res:
                p = p + r_ref[...]
            o_ref[...] = p.astype(out_dtype)
        else:
            acc = refs[-1]
            k = pl.program_id(2)

            @pl.when(k == 0)
            def _():
                acc[...] = (p + r_ref[...]) if has_res else p

            @pl.when(k > 0)
            def _():
                acc[...] += p

            @pl.when(k == nk - 1)
            def _():
                o_ref[...] = acc[...].astype(out_dtype)

    return pl.pallas_call(
        body, name=name,
        out_shape=jax.ShapeDtypeStruct((M, N), out_dtype),
        grid=(M // tm, N // tn, nk),
        in_specs=[a_spec, b_spec] + ([o_spec] if has_res else []) + ([pl.BlockSpec(memory_space=pl.ANY)] if after is not None else []),
        out_specs=o_spec,
        scratch_shapes=[pltpu.VMEM((tm, tn), F32)] if nk > 1 else [],
        compiler_params=_cp("parallel", "parallel", "arbitrary"),
    )(*((a, b) + ((res,) if has_res else ()) + ((after,) if after is not None else ())))


def _rmsnorm_fwd(x, w, name, tm=256):
    Tn = x.shape[0]

    def body(x_ref, w_ref, o_ref):
        xv = x_ref[...]
        r = lax.rsqrt(jnp.mean(xv * xv, axis=1, keepdims=True) + EPS)
        o_ref[...] = (xv * r * w_ref[...]).astype(BF16)

    return pl.pallas_call(
        body, name=name, out_shape=jax.ShapeDtypeStruct((Tn, D), BF16), grid=(Tn // tm,),
        in_specs=[pl.BlockSpec((tm, D), lambda i: (i, 0)), pl.BlockSpec((1, D), lambda i: (0, 0))],
        out_specs=pl.BlockSpec((tm, D), lambda i: (i, 0)), compiler_params=_cp("parallel"),
    )(x, w)


def _rmsnorm_bwd(dy, h, w, res, name, tm=256):
    Tn = h.shape[0]

    def body(dy_ref, h_ref, w_ref, res_ref, dh_ref, dhb_ref, dw_ref):
        hv, dyv = h_ref[...], dy_ref[...]
        r = lax.rsqrt(jnp.mean(hv * hv, axis=1, keepdims=True) + EPS)
        g = dyv * w_ref[...]
        dh = res_ref[...] + r * g - hv * (r * r * r * jnp.mean(g * hv, axis=1, keepdims=True))
        dh_ref[...] = dh
        dhb_ref[...] = dh.astype(BF16)
        part = jnp.sum(dyv * hv * r, axis=0, keepdims=True)

        @pl.when(pl.program_id(0) == 0)
        def _():
            dw_ref[...] = part

        @pl.when(pl.program_id(0) > 0)
        def _():
            dw_ref[...] += part

    row = pl.BlockSpec((tm, D), lambda i: (i, 0))
    vec = pl.BlockSpec((1, D), lambda i: (0, 0))
    return pl.pallas_call(
        body, name=name,
        out_shape=(jax.ShapeDtypeStruct((Tn, D), F32), jax.ShapeDtypeStruct((Tn, D), BF16), jax.ShapeDtypeStruct((1, D), F32)),
        grid=(Tn // tm,), in_specs=[row, row, vec, row], out_specs=(row, row, vec), compiler_params=_cp("arbitrary"),
    )(dy, h, w, res)


def _loss_head(h2, wf, tgt, name="loss_head", tm=256):
    Tn = h2.shape[0]

    def body(h_ref, w_ref, t_ref, dh_ref, dhb_ref, dw_ref, loss_ref):
        hv, wv = h_ref[...], w_ref[...]
        r = lax.rsqrt(jnp.mean(hv * hv, axis=1, keepdims=True) + EPS)
        hn = hv * r
        e = hn * wv - t_ref[...]
        dy = e * (1.0 / D)
        g = dy * wv
        dh = r * g - hv * (r * r * r * jnp.mean(g * hv, axis=1, keepdims=True))
        dh_ref[...] = dh
        dhb_ref[...] = dh.astype(BF16)
        part = jnp.sum(dy * hn, axis=0, keepdims=True)
        lpart = (0.5 / D) * jnp.sum(jnp.sum(e * e, axis=1, keepdims=True), axis=0, keepdims=True)

        @pl.when(pl.program_id(0) == 0)
        def _():
            dw_ref[...] = part
            loss_ref[...] = lpart

        @pl.when(pl.program_id(0) > 0)
        def _():
            dw_ref[...] += part
            loss_ref[...] += lpart

    row = pl.BlockSpec((tm, D), lambda i: (i, 0))
    vec = pl.BlockSpec((1, D), lambda i: (0, 0))
    one = pl.BlockSpec((1, 1), lambda i: (0, 0))
    return pl.pallas_call(
        body, name=name,
        out_shape=(jax.ShapeDtypeStruct((Tn, D), F32), jax.ShapeDtypeStruct((Tn, D), BF16), jax.ShapeDtypeStruct((1, D), F32),
                   jax.ShapeDtypeStruct((1, 1), F32)),
        grid=(Tn // tm,), in_specs=[row, vec, row], out_specs=(row, row, vec, one), compiler_params=_cp("arbitrary"),
    )(h2, wf, tgt)


def _attn_mask(n):
    qi = lax.broadcasted_iota(jnp.int32, (NKV, GRP * WIN, 2 * WIN), 1) % WIN
    ki = lax.broadcasted_iota(jnp.int32, (NKV, GRP * WIN, 2 * WIN), 2)
    rel = qi + WIN - ki
    return (rel >= 0) & (rel < WIN) & ((n > 0) | (ki >= WIN))


def _kv_heads(prev_ref, cur_ref):
    return jnp.stack([jnp.concatenate([prev_ref[:, h * HD:(h + 1) * HD], cur_ref[:, h * HD:(h + 1) * HD]], axis=0) for h in range(NKV)])


def _q_heads(ref):
    return jnp.stack([jnp.concatenate([ref[:, (h * GRP + g) * HD:(h * GRP + g + 1) * HD] for g in range(GRP)], axis=0) for h in range(NKV)])


def _attn_probs(q_ref, kc_ref, kp_ref, sink_ref, mask):
    kk = _kv_heads(kp_ref, kc_ref).astype(BF16)
    qs = _q_heads(q_ref).astype(BF16)
    s = jnp.einsum('hqd,hkd->hqk', qs, kk, preferred_element_type=F32) * (HD ** -0.5)
    s = jnp.where(mask, s, MASKV)
    sink = jnp.stack([jnp.concatenate([jnp.full((WIN, 1), sink_ref[0, h * GRP + g], F32) for g in range(GRP)], axis=0) for h in range(NKV)])
    m = jnp.maximum(jnp.max(s, axis=2, keepdims=True), sink)
    e = jnp.exp(s - m)
    es = jnp.exp(sink - m)
    inv = 1.0 / (jnp.sum(e, axis=2, keepdims=True) + es)
    return e * inv, es * inv, qs, kk


def _attn_specs(nb, last):
    cur = lambda n: jnp.minimum(n, last)
    prev = lambda n: jnp.maximum(jnp.minimum(n, last) - 1, 0)
    return [
        pl.BlockSpec((WIN, NQ * HD), lambda n: (cur(n), C_AQ // (NQ * HD))),
        pl.BlockSpec((WIN, NKV * HD), lambda n: (cur(n), C_AK // (NKV * HD))),
        pl.BlockSpec((WIN, NKV * HD), lambda n: (prev(n), C_AK // (NKV * HD))),
        pl.BlockSpec((WIN, NKV * HD), lambda n: (cur(n), C_AV // (NKV * HD))),
        pl.BlockSpec((WIN, NKV * HD), lambda n: (prev(n), C_AV // (NKV * HD))),
    ]


def _attn_fwd(proj, sinks, name="attn_fwd"):
    Tn = proj.shape[0]
    nb = Tn // WIN

    def body(q_ref, kc_ref, kp_ref, vc_ref, vp_ref, sink_ref, o_ref):
        p, _, _, _ = _attn_probs(q_ref, kc_ref, kp_ref, sink_ref, _attn_mask(pl.program_id(0)))
        o = jnp.einsum('hqk,hkd->hqd', p.astype(BF16), _kv_heads(vp_ref, vc_ref).astype(BF16), preferred_element_type=F32)
        for h in range(NKV):
            for g in range(GRP):
                o_ref[:, (h * GRP + g) * HD:(h * GRP + g + 1) * HD] = o[h, g * WIN:(g + 1) * WIN, :]

    return pl.pallas_call(
        body, name=name, out_shape=jax.ShapeDtypeStruct((Tn, D), F32), grid=(nb,),
        in_specs=_attn_specs(nb, nb - 1) + [pl.BlockSpec(memory_space=pltpu.SMEM)],
        out_specs=pl.BlockSpec((WIN, D), lambda n: (n, 0)), compiler_params=_cp("parallel"),
    )(proj, proj, proj, proj, proj, sinks)


def _attn_bwd(proj, sinks, o, do, name="attn_bwd"):
    Tn = proj.shape[0]
    nb = Tn // WIN
    KW = NKV * HD

    def body(q_ref, kc_ref, kp_ref, vc_ref, vp_ref, o_ref, do_ref, sink_ref, dq_ref, dkv_ref, dsk_ref, carry, cur):
        n = pl.program_id(0)

        @pl.when(n == 0)
        def _():
            carry[...] = jnp.zeros_like(carry)
            dsk_ref[...] = jnp.zeros_like(dsk_ref)

        @pl.when(n < nb)
        def _():
            p, ps, qs, kk = _attn_probs(q_ref, kc_ref, kp_ref, sink_ref, _attn_mask(n))
            vv = _kv_heads(vp_ref, vc_ref).astype(BF16)
            dos = _q_heads(do_ref)
            delta = jnp.sum(dos * _q_heads(o_ref), axis=2, keepdims=True)
            dosb = dos.astype(BF16)
            dp = jnp.einsum('hqd,hkd->hqk', dosb, vv, preferred_element_type=F32)
            ds = (p * (dp - delta) * (HD ** -0.5)).astype(BF16)
            dq = jnp.einsum('hqk,hkd->hqd', ds, kk, preferred_element_type=F32)
            dkk = jnp.einsum('hqk,hqd->hkd', ds, qs, preferred_element_type=F32)
            dvv = jnp.einsum('hqk,hqd->hkd', p.astype(BF16), dosb, preferred_element_type=F32)
            dsk = ps * delta
            for h in range(NKV):
                for g in range(GRP):
                    i = h * GRP + g
                    dq_ref[:, i * HD:(i + 1) * HD] = dq[h, g * WIN:(g + 1) * WIN, :].astype(BF16)
                    dsk_ref[:, i:i + 1] -= jnp.sum(dsk[h, g * WIN:(g + 1) * WIN, :], axis=0, keepdims=True)
                dkv_ref[:, h * HD:(h + 1) * HD] = (carry[:, h * HD:(h + 1) * HD] + dkk[h, :WIN, :]).astype(BF16)
                dkv_ref[:, KW + h * HD:KW + (h + 1) * HD] = (carry[:, KW + h * HD:KW + (h + 1) * HD] + dvv[h, :WIN, :]).astype(BF16)
                cur[:, h * HD:(h + 1) * HD] = dkk[h, WIN:, :]
                cur[:, KW + h * HD:KW + (h + 1) * HD] = dvv[h, WIN:, :]
            carry[...] = cur[...]

        @pl.when(n == nb)
        def _():
            dkv_ref[...] = carry[...].astype(BF16)

    last = nb - 1
    row = pl.BlockSpec((WIN, D), lambda n: (jnp.minimum(n, last), 0))
    return pl.pallas_call(
        body, name=name,
        out_shape=(jax.ShapeDtypeStruct((Tn, D), BF16), jax.ShapeDtypeStruct((Tn, 2 * KW), BF16), jax.ShapeDtypeStruct((1, NQ), F32)),
        grid=(nb + 1,),
        in_specs=_attn_specs(nb, last) + [row, row, pl.BlockSpec(memory_space=pltpu.SMEM)],
        out_specs=(row, pl.BlockSpec((WIN, 2 * KW), lambda n: (jnp.maximum(n - 1, 0), 0)), pl.BlockSpec((1, NQ), lambda n: (0, 0))),
        scratch_shapes=[pltpu.VMEM((WIN, 2 * KW), F32), pltpu.VMEM((WIN, 2 * KW), F32)],
        compiler_params=_cp("arbitrary"),
    )(proj, proj, proj, proj, proj, o, do, sinks)


def _tri(lower):
    r = lax.broadcasted_iota(jnp.int32, (GC, GC), 0)
    c = lax.broadcasted_iota(jnp.int32, (GC, GC), 1)
    return r >= c if lower else r <= c


def _per_head(a):
    return jnp.stack([a[:, h * DK:(h + 1) * DK] for h in range(GH)])


def _all_heads(a):
    return jnp.concatenate([a[h] for h in range(GH)], axis=1)


def _gla_gates(lr, w2_ref, gb_ref):
    logit = _dot(lr, w2_ref[...].astype(BF16)) + gb_ref[...]
    la = (jnp.minimum(logit, 0.0) - jnp.log(1.0 + jnp.exp(-jnp.abs(logit)))) * (1.0 / 16.0)
    g = _dot(_tri(True).astype(F32), la, prec=HIGHEST)
    return logit, g


def _bmm(spec, a, b):
    return jnp.einsum(spec, a, b, preferred_element_type=F32)


def _gla_specs(nc, rev):
    idx = (lambda n: nc - 1 - n) if rev else (lambda n: n)
    half = 2 * DK
    return (
        [pl.BlockSpec((GC, half), lambda n, j=j: (idx(n), C_GQ // half + j)) for j in range(2)]
        + [pl.BlockSpec((GC, half), lambda n, j=j: (idx(n), C_GK // half + j)) for j in range(2)]
        + [pl.BlockSpec((GC, DV), lambda n, h=h: (idx(n), C_GV // DV + h)) for h in range(GH)]
        + [pl.BlockSpec((GC, LANE), lambda n: (idx(n), 0)), pl.BlockSpec((LANE, GH * DK), lambda n: (0, 0)),
           pl.BlockSpec((1, GH * DK), lambda n: (0, 0))])


def _gla_heads(refs):
    return (lambda h: refs[h // 2][:, (h % 2) * DK:(h % 2 + 1) * DK], lambda h: refs[2 + h // 2][:, (h % 2) * DK:(h % 2 + 1) * DK],
            lambda h: refs[4 + h][...])


def _gla_fwd(proj, plr, w2p, gb, name="gla_fwd"):
    Tn = proj.shape[0]
    nc = Tn // GC

    def body(*refs):
        qh, kh, vh = _gla_heads(refs)
        lr_ref, w2_ref, gb_ref, o_ref, st_ref, S = refs[8:]

        @pl.when(pl.program_id(0) == 0)
        def _():
            S[...] = jnp.zeros_like(S)

        heads = lambda f: jnp.stack([f(h) for h in range(GH)])
        _, g_all = _gla_gates(lr_ref[...].astype(BF16), w2_ref, gb_ref)
        g = _per_head(g_all)
        gl = g[:, GC - 1:GC, :]
        k = heads(kh)
        v = heads(vh).astype(BF16)
        qd = (heads(qh) * (DK ** -0.5) * jnp.exp(g)).astype(BF16)
        ki = (k * jnp.exp(-g)).astype(BF16)
        ke = (k * jnp.exp(gl - g)).astype(BF16)
        att = jnp.where(_tri(True)[None], _bmm('hid,hjd->hij', qd, ki), 0.0).astype(BF16)
        sp = S[...]
        st_ref[0] = sp
        o = _bmm('hij,hjv->hiv', att, v) + _bmm('hid,hvd->hiv', qd, sp.astype(BF16))
        for h in range(GH):
            o_ref[:, h * DV:(h + 1) * DV] = o[h]
        S[...] = sp * jnp.exp(gl) + _bmm('hjv,hjd->hvd', v, ke)

    return pl.pallas_call(
        body, name=name,
        out_shape=(jax.ShapeDtypeStruct((Tn, GH * DV), F32), jax.ShapeDtypeStruct((nc, GH, DV, DK), F32)),
        grid=(nc,), in_specs=_gla_specs(nc, False),
        out_specs=(pl.BlockSpec((GC, GH * DV), lambda n: (n, 0)), pl.BlockSpec((1, GH, DV, DK), lambda n: (n, 0, 0, 0))),
        scratch_shapes=[pltpu.VMEM((GH, DV, DK), F32)], compiler_params=_cp("arbitrary"),
    )(*([proj] * 8), plr, w2p, gb)


def _gla_bwd(proj, plr, w2p, gb, states, do, name="gla_bwd"):
    Tn = proj.shape[0]
    nc = Tn // GC

    def body(*refs):
        qh, kh, vh = _gla_heads(refs)
        lr_ref, w2_ref, gb_ref, st_ref, do_ref, dqk_ref, dv_ref, dlr_ref, dw2_ref, dgb_ref, dS = refs[8:]

        @pl.when(pl.program_id(0) == 0)
        def _():
            dS[...] = jnp.zeros_like(dS)
            dw2_ref[...] = jnp.zeros_like(dw2_ref)
            dgb_ref[...] = jnp.zeros_like(dgb_ref)

        heads = lambda f: jnp.stack([f(h) for h in range(GH)])
        lr = lr_ref[...].astype(BF16)
        causal = _tri(True)[None]
        last_row = lax.broadcasted_iota(jnp.int32, (GH, GC, DK), 1) == GC - 1
        logit, g_all = _gla_gates(lr, w2_ref, gb_ref)
        g = _per_head(g_all)
        gl = g[:, GC - 1:GC, :]
        egl = jnp.exp(gl)
        eg, eng, ege = jnp.exp(g), jnp.exp(-g), jnp.exp(gl - g)
        k = heads(kh)
        v = heads(vh).astype(BF16)
        dob = heads(lambda h: do_ref[:, h * DV:(h + 1) * DV]).astype(BF16)
        qd = heads(qh) * (DK ** -0.5) * eg
        ki = k * eng
        ke = k * ege
        qdb, kib, keb = qd.astype(BF16), ki.astype(BF16), ke.astype(BF16)
        att = jnp.where(causal, _bmm('hid,hjd->hij', qdb, kib), 0.0).astype(BF16)
        datt = jnp.where(causal, _bmm('hiv,hjv->hij', dob, v), 0.0).astype(BF16)
        sp = st_ref[0]
        dsn = dS[...]
        dsnb = dsn.astype(BF16)
        dv = (_bmm('hij,hiv->hjv', att, dob) + _bmm('hjd,hvd->hjv', keb, dsnb)).astype(BF16)
        dqd = _bmm('hij,hjd->hid', datt, kib) + _bmm('hiv,hvd->hid', dob, sp.astype(BF16))
        dki = _bmm('hij,hid->hjd', datt, qdb)
        dke = _bmm('hjv,hvd->hjd', v, dsnb)
        ddec = jnp.sum(dsn * sp, axis=1, keepdims=True)
        dS[...] = dsn * egl + _bmm('hiv,hid->hvd', dob, qdb)
        dke_ke = dke * ke
        dgl = jnp.sum(dke_ke, axis=1, keepdims=True) + ddec * egl
        dg = dqd * qd - dki * ki - dke_ke + jnp.where(last_row, dgl, 0.0)
        dq = (dqd * ((DK ** -0.5) * eg)).astype(BF16)
        dk = (dki * eng + dke * ege).astype(BF16)
        for h in range(GH):
            dv_ref[:, h * DV:(h + 1) * DV] = dv[h]
            dqk_ref[:, h * DK:(h + 1) * DK] = dq[h]
            dqk_ref[:, GH * DK + h * DK:GH * DK + (h + 1) * DK] = dk[h]
        dla = _dot(_tri(False).astype(F32), _all_heads(dg), prec=HIGHEST)
        dlogit = dla * (1.0 / 16.0) * _sigmoid(-logit)
        dlb = dlogit.astype(BF16)
        dlr_ref[...] = _dot(dlb, w2_ref[...].astype(BF16), tb=True).astype(BF16)
        dw2_ref[...] += _dot(lr, dlb, ta=True)
        dgb_ref[...] += jnp.sum(dlogit, axis=0, keepdims=True)

    rev = lambda n: nc - 1 - n
    row = pl.BlockSpec((GC, GH * DV), lambda n: (rev(n), 0))
    return pl.pallas_call(
        body, name=name,
        out_shape=(jax.ShapeDtypeStruct((Tn, 2 * GH * DK), BF16), jax.ShapeDtypeStruct((Tn, GH * DV), BF16),
                   jax.ShapeDtypeStruct((Tn, LANE), BF16), jax.ShapeDtypeStruct((LANE, GH * DK), F32),
                   jax.ShapeDtypeStruct((1, GH * DK), F32)),
        grid=(nc,),
        in_specs=_gla_specs(nc, True) + [pl.BlockSpec((1, GH, DV, DK), lambda n: (rev(n), 0, 0, 0)), row],
        out_specs=(row, row, pl.BlockSpec((GC, LANE), lambda n: (rev(n), 0)), pl.BlockSpec((LANE, GH * DK), lambda n: (0, 0)),
                   pl.BlockSpec((1, GH * DK), lambda n: (0, 0))),
        scratch_shapes=[pltpu.VMEM((GH, DV, DK), F32)], compiler_params=_cp("arbitrary"),
    )(*([proj] * 8), plr, w2p, gb, states, do)


def _merge_specs(tm):
    row = pl.BlockSpec((tm, D), lambda i: (i, 0))
    gates = [pl.BlockSpec((tm, DV), lambda i, j=c // DV + h: (i, j)) for c in (C_GR, C_GA, C_GB) for h in range(GH)]
    return row, gates, pl.BlockSpec((1, DV), lambda i: (0, 0))


def _merge_fwd(a, go, proj, gnw, name="merge_fwd", tm=256):
    Tn = a.shape[0]

    def body(a_ref, go_ref, *rest):
        gates, w_ref, m_ref = rest[:3 * GH], rest[3 * GH], rest[3 * GH + 1]
        for h in range(GH):
            sl = slice(h * DV, (h + 1) * DV)
            gov = go_ref[:, sl]
            r = lax.rsqrt(jnp.mean(gov * gov, axis=1, keepdims=True) + EPS)
            gr = gates[h][...]
            g2 = gov * r * w_ref[...] * (gr * _sigmoid(gr))
            m_ref[:, sl] = (_sigmoid(gates[GH + h][...]) * a_ref[:, sl] + _sigmoid(gates[2 * GH + h][...]) * g2).astype(BF16)

    row, gates, vec = _merge_specs(tm)
    return pl.pallas_call(
        body, name=name, out_shape=jax.ShapeDtypeStruct((Tn, D), BF16), grid=(Tn // tm,),
        in_specs=[row, row] + gates + [vec], out_specs=row, compiler_params=_cp("parallel"),
    )(a, go, *([proj] * (3 * GH)), gnw)


def _merge_bwd(dm, a, go, proj, gnw, name="merge_bwd", tm=256):
    Tn = a.shape[0]

    def body(dm_ref, a_ref, go_ref, *rest):
        gates = rest[:3 * GH]
        w_ref, da_ref, dgo_ref, dg_ref, dw_ref = rest[3 * GH:]
        wv = w_ref[...]
        dw = jnp.zeros((1, DV), F32)
        for h in range(GH):
            sl = slice(h * DV, (h + 1) * DV)
            dmv, av, gov, gr = dm_ref[:, sl], a_ref[:, sl], go_ref[:, sl], gates[h][...]
            sa, sb, sg = _sigmoid(gates[GH + h][...]), _sigmoid(gates[2 * GH + h][...]), _sigmoid(gr)
            r = lax.rsqrt(jnp.mean(gov * gov, axis=1, keepdims=True) + EPS)
            gn0 = gov * r
            gn = gn0 * wv
            silu = gr * sg
            dg2 = dmv * sb
            da_ref[:, sl] = dmv * sa
            dg_ref[:, D + h * DV:D + (h + 1) * DV] = (dmv * av * sa * (1.0 - sa)).astype(BF16)
            dg_ref[:, 2 * D + h * DV:2 * D + (h + 1) * DV] = (dg2 * gn * silu * (1.0 - sb)).astype(BF16)
            dg_ref[:, sl] = (dg2 * gn * (sg * (1.0 + gr * (1.0 - sg)))).astype(BF16)
            dgn = dg2 * silu
            dw = dw + jnp.sum(dgn * gn0, axis=0, keepdims=True)
            gg = dgn * wv
            dgo_ref[:, sl] = r * gg - gov * (r * r * r * jnp.mean(gg * gov, axis=1, keepdims=True))

        @pl.when(pl.program_id(0) == 0)
        def _():
            dw_ref[...] = dw

        @pl.when(pl.program_id(0) > 0)
        def _():
            dw_ref[...] += dw

    row, gates, vec = _merge_specs(tm)
    return pl.pallas_call(
        body, name=name,
        out_shape=(jax.ShapeDtypeStruct((Tn, D), F32), jax.ShapeDtypeStruct((Tn, D), F32), jax.ShapeDtypeStruct((Tn, 3 * D), BF16),
                   jax.ShapeDtypeStruct((1, DV), F32)),
        grid=(Tn // tm,), in_specs=[row, row, row] + gates + [vec],
        out_specs=(row, row, pl.BlockSpec((tm, 3 * D), lambda i: (i, 0)), vec), compiler_params=_cp("arbitrary"),
    )(dm, a, go, *([proj] * (3 * GH)), gnw)


def _ffn_up(v2, wgt, wut, name="ffn_up", tm=1024, tn=512):
    Tn = v2.shape[0]
    tm = min(tm, Tn)

    def body(v_ref, wg_ref, wu_ref, a_ref, b_ref, ff_ref):
        vv = v_ref[...]
        a = _dot(vv, wg_ref[...], tb=True)
        b = _dot(vv, wu_ref[...], tb=True)
        a_ref[...] = a.astype(BF16)
        b_ref[...] = b.astype(BF16)
        ff_ref[...] = (a * _sigmoid(a) * b).astype(BF16)

    w = pl.BlockSpec((tn, D), lambda j, i: (j, 0))
    act = pl.BlockSpec((tm, tn), lambda j, i: (i, j))
    return pl.pallas_call(
        body, name=name,
        out_shape=(jax.ShapeDtypeStruct((Tn, FH), BF16), jax.ShapeDtypeStruct((Tn, FH), BF16), jax.ShapeDtypeStruct((Tn, FH), BF16)),
        grid=(FH // tn, Tn // tm), in_specs=[pl.BlockSpec((tm, D), lambda j, i: (i, 0)), w, w], out_specs=(act, act, act),
        compiler_params=_cp("parallel", "parallel"),
    )(v2, wgt, wut)


def _ffn_dact(dh2b, wd, a, b, name="ffn_dact", tm=1024, tn=512):
    Tn = dh2b.shape[0]
    tm = min(tm, Tn)

    def body(d_ref, w_ref, a_ref, b_ref, da_ref, db_ref):
        dff = _dot(d_ref[...], w_ref[...], tb=True)
        av = a_ref[...].astype(F32)
        sg = _sigmoid(av)
        da_ref[...] = (dff * b_ref[...].astype(F32) * (sg * (1.0 + av * (1.0 - sg)))).astype(BF16)
        db_ref[...] = (dff * (av * sg)).astype(BF16)

    act = pl.BlockSpec((tm, tn), lambda j, i: (i, j))
    return pl.pallas_call(
        body, name=name,
        out_shape=(jax.ShapeDtypeStruct((Tn, FH), BF16), jax.ShapeDtypeStruct((Tn, FH), BF16)),
        grid=(FH // tn, Tn // tm),
        in_specs=[pl.BlockSpec((tm, D), lambda j, i: (i, 0)), pl.BlockSpec((tn, D), lambda j, i: (j, 0)), act, act],
        out_specs=(act, act), compiler_params=_cp("parallel", "parallel"),
    )(dh2b, wd, a, b)


def _adam_math(w, g, m, v):
    m2 = B1 * m + (1.0 - B1) * g
    v2 = B2 * v + (1.0 - B2) * (g * g)
    mh = m2 / (1.0 - B1 ** STEP)
    vh = v2 / (1.0 - B2 ** STEP)
    return -LR * (mh / (jnp.sqrt(vh) + AEPS) + WD * w), m2, v2


def _sum_blocks(o_ref, p_ref):
    g = o_ref[...].astype(F32)
    for j in range(p_ref.shape[0]):
        g = g + p_ref[j].astype(F32)
    return g


def _adamw(w, m, v, psums, parts, chip_idx, name, tr):
    R, C = w.shape

    def body(s_ref, w_ref, m_ref, v_ref, o_ref, p_ref, g_ref, d_ref, m2_ref, v2_ref):
        g = _sum_blocks(o_ref, p_ref)
        d, m2, v2 = _adam_math(w_ref[...], g, m_ref[...], v_ref[...])
        g_ref[...] = g
        d_ref[...] = d
        m2_ref[...] = m2
        v2_ref[...] = v2

    blk = pl.BlockSpec((tr, C), lambda i, s: (i, 0))
    out = jax.ShapeDtypeStruct((R, C), F32)
    grid_spec = pltpu.PrefetchScalarGridSpec(
        num_scalar_prefetch=1, grid=(R // tr,),
        in_specs=[blk, blk, blk, pl.BlockSpec((None, tr, C), lambda i, s: (s[0], i, 0)),
                  pl.BlockSpec((parts.shape[0], tr, C), lambda i, s: (0, i, 0))],
        out_specs=(blk, blk, blk, blk),
    )
    return pl.pallas_call(body, name=name, out_shape=(out, out, out, out), grid_spec=grid_spec, compiler_params=_cp("parallel"),
                          )(chip_idx, w, m, v, psums, parts)


def _adamw_rows(w, m, v, g, name, tr):
    R = w.shape[0]

    def body(w_ref, m_ref, v_ref, g_ref, d_ref, m2_ref, v2_ref):
        d, m2, v2 = _adam_math(w_ref[...], g_ref[...], m_ref[...], v_ref[...])
        d_ref[...] = d
        m2_ref[...] = m2
        v2_ref[...] = v2

    blk = pl.BlockSpec((tr,) + w.shape[1:], lambda i: (i, 0, 0))
    out = jax.ShapeDtypeStruct(w.shape, F32)
    return pl.pallas_call(body, name=name, out_shape=(out, out, out), grid=(R // tr,), in_specs=[blk] * 4, out_specs=(blk, blk, blk),
                          compiler_params=_cp("parallel"))(w, m, v, g)


def _sum_parts(psums, parts, chip_idx, name, tr, tc):
    _, R, C = psums.shape

    def body(s_ref, o_ref, p_ref, g_ref):
        g_ref[...] = _sum_blocks(o_ref, p_ref)

    grid_spec = pltpu.PrefetchScalarGridSpec(
        num_scalar_prefetch=1, grid=(R // tr, C // tc),
        in_specs=[pl.BlockSpec((None, tr, tc), lambda i, j, s: (s[0], i, j)),
                  pl.BlockSpec((parts.shape[0], tr, tc), lambda i, j, s: (0, i, j))],
        out_specs=pl.BlockSpec((tr, tc), lambda i, j, s: (i, j)),
    )
    return pl.pallas_call(body, name=name, out_shape=jax.ShapeDtypeStruct((R, C), F32), grid_spec=grid_spec,
                          compiler_params=_cp("parallel", "parallel"))(chip_idx, psums, parts)


def _adamw_plain(w, m, v, g, name):
    def body(w_ref, m_ref, v_ref, g_ref, d_ref, m2_ref, v2_ref):
        d, m2, v2 = _adam_math(w_ref[...], g_ref[...], m_ref[...], v_ref[...])
        d_ref[...] = d
        m2_ref[...] = m2
        v2_ref[...] = v2

    out = jax.ShapeDtypeStruct(w.shape, F32)
    return pl.pallas_call(body, name=name, out_shape=(out, out, out))(w, m, v, g)


def _sum_devices(pack_all, name="sum_small"):
    def body(p_ref, o_ref):
        s = p_ref[0]
        for k in range(1, NDEV):
            s = s + p_ref[k]
        o_ref[...] = s

    return pl.pallas_call(body, name=name, out_shape=jax.ShapeDtypeStruct(pack_all.shape[1:], F32))(pack_all)


def _pair_add(g5, recv, c_idx, name, tr):
    _, _, R, C = g5.shape

    def body(c_ref, g_ref, r_ref, o_ref):
        o_ref[...] = (g_ref[...].astype(F32) + r_ref[...].astype(F32)).astype(BF16)

    grid_spec = pltpu.PrefetchScalarGridSpec(
        num_scalar_prefetch=1, grid=(4, R // tr),
        in_specs=[pl.BlockSpec((None, None, tr, C), lambda q, i, c: (q, c[0], i, 0)), pl.BlockSpec((None, tr, C), lambda q, i, c: (q, i, 0))],
        out_specs=pl.BlockSpec((None, tr, C), lambda q, i, c: (q, i, 0)),
    )
    return pl.pallas_call(
        body, name=name, out_shape=jax.ShapeDtypeStruct((4, R, C), BF16), grid_spec=grid_spec,
        compiler_params=_cp("parallel", "parallel"),
    )(c_idx, g5, recv)


_ANY = pl.BlockSpec(memory_space=pl.ANY)


def _mesh_pos():
    x, y, c = lax.axis_index("x"), lax.axis_index("y"), lax.axis_index("c")
    return x, y, c, [(1 - x, y), (x, 1 - y), (1 - x, 1 - y)]


def _all_gather(shards, name="gather_weights"):
    n = len(shards)

    def body(*refs):
        ins, outs = refs[:n], refs[n:2 * n]
        send, recv, loc = refs[2 * n + 1:]
        x, y, c, chips = _mesh_pos()
        me, sib = (x, y, c), (x, y, 1 - c)

        def cp(a, k, block, to, own=False):
            dst = outs[a].at[4 * block[0] + 2 * block[1] + block[2]]
            return pltpu.make_async_remote_copy(src_ref=ins[a] if own else dst, dst_ref=dst, send_sem=send.at[a, k],
                                                recv_sem=recv.at[a, k], device_id=to, device_id_type=MESH)

        north = c == 1
        handed = (jnp.where(north, 1 - x, x), jnp.where(north, y, 1 - y))
        hand_to = (jnp.where(north, x, 1 - x), jnp.where(north, 1 - y, y))
        local = [pltpu.make_async_copy(ins[a], outs[a].at[4 * x + 2 * y + c], loc.at[a]) for a in range(n)]
        first = []
        for a in range(n):
            local[a].start()
            first.append(cp(a, 0, me, sib, own=True))
            first += [cp(a, 1 + j, me, (*chip, c), own=True) for j, chip in enumerate(chips[:2])]
        for d in first:
            d.start()
        passed = []
        for a in range(n):
            for j, chip in enumerate(chips[:2]):
                cp(a, 1 + j, (*chip, c), me).wait_recv()
            passed.append(cp(a, 3, (*handed, c), (*hand_to, c)))
            passed += [cp(a, 4 + j, (*chip, c), sib) for j, chip in enumerate(chips[:2])]
            for d in passed[-3:]:
                d.start()
        for a in range(n):
            cp(a, 3, (*chips[2], c), me).wait_recv()
            passed.append(cp(a, 6, (*chips[2], c), sib))
            passed[-1].start()
        for a in range(n):
            cp(a, 0, sib, me).wait_recv()
            for j, chip in enumerate(chips):
                cp(a, 4 + j, (*chip, 1 - c), me).wait_recv()
        for d in first + passed:
            d.wait_send()
        for d in local:
            d.wait()
        refs[2 * n][...] = jnp.zeros_like(refs[2 * n])

    return pl.pallas_call(
        body, name=name,
        out_shape=tuple(jax.ShapeDtypeStruct((NDEV,) + s.shape, s.dtype) for s in shards) + (jax.ShapeDtypeStruct((8, LANE), F32),),
        in_specs=[_ANY] * n, out_specs=tuple([_ANY] * n) + (pl.BlockSpec(memory_space=pltpu.VMEM),),
        scratch_shapes=[pltpu.SemaphoreType.DMA((n, 7)), pltpu.SemaphoreType.DMA((n, 7)), pltpu.SemaphoreType.DMA((n,))],
    )(*shards)


def _pair_exchange(grads, name):
    n = len(grads)

    def body(*refs):
        ins, outs = refs[:n], refs[n:2 * n]
        send, recv = refs[2 * n:]
        x, y, c, _ = _mesh_pos()
        big = [pltpu.make_async_remote_copy(src_ref=ins[a].at[:, 1 - c], dst_ref=outs[a], send_sem=send.at[a], recv_sem=recv.at[a],
                                            device_id=(x, y, 1 - c), device_id_type=MESH) for a in range(n)]
        for d in big:
            d.start()
        for d in big:
            d.wait_recv()
        for d in big:
            d.wait_send()

    return pl.pallas_call(
        body, name=name, out_shape=tuple(jax.ShapeDtypeStruct((4,) + g.shape[2:], g.dtype) for g in grads),
        in_specs=[_ANY] * n, out_specs=tuple([_ANY] * n),
        scratch_shapes=[pltpu.SemaphoreType.DMA((n,)), pltpu.SemaphoreType.DMA((n,))],
    )(*grads)


def _gather_small(pack, name="gather_small"):
    def body(pk, pk_all, psend, precv, loc):
        x, y, c, chips = _mesh_pos()
        me_slot = 4 * x + 2 * y + c
        sib = (x, y, 1 - c)
        own = pltpu.make_async_copy(pk, pk_all.at[me_slot], loc)
        own.start()
        peers = [sib] + [(*chip, c) for chip in chips] + [(*chip, 1 - c) for chip in chips]
        small = [pltpu.make_async_remote_copy(src_ref=pk, dst_ref=pk_all.at[me_slot], send_sem=psend.at[k], recv_sem=precv.at[k],
                                              device_id=p, device_id_type=MESH) for k, p in enumerate(peers)]
        for d in small:
            d.start()
        for k, p in enumerate(peers):
            pltpu.make_async_remote_copy(src_ref=pk, dst_ref=pk_all.at[4 * p[0] + 2 * p[1] + p[2]], send_sem=psend.at[k],
                                         recv_sem=precv.at[k], device_id=p, device_id_type=MESH).wait_recv()
        for d in small:
            d.wait_send()
        own.wait()

    return pl.pallas_call(
        body, name=name, out_shape=jax.ShapeDtypeStruct((NDEV,) + pack.shape, pack.dtype), in_specs=[_ANY], out_specs=_ANY,
        scratch_shapes=[pltpu.SemaphoreType.DMA((7,)), pltpu.SemaphoreType.DMA((7,)), pltpu.SemaphoreType.DMA(())],
    )(pack)


def _main_row(g):
    return g if g < C_LR else g - RANK


def _window_pieces(lo, hi):
    out = []
    for a, b, where in ((lo, min(hi, C_LR), "main"), (max(lo, C_LR), min(hi, C_LR + RANK), "lr"), (max(lo, C_LR + RANK), hi, "main")):
        if a < b:
            out.append((a, b, where, _main_row(a) if where == "main" else a - C_LR))
    return out


def _assemble_w_in(windows, own, name="assemble_w_in"):
    edges = NDEV - 1

    def body(b_ref, own_ref, main_ref, lr_ref, buf, ebuf, in_sems, out_sems, esems):
        dev = 4 * lax.axis_index("x") + 2 * lax.axis_index("y") + lax.axis_index("c")

        def load(k):
            return pltpu.make_async_copy(b_ref.at[k], buf.at[k % 2], in_sems.at[k % 2])

        def start_load(k):
            pl.when(dev == k)(pltpu.make_async_copy(own_ref, buf.at[k % 2], in_sems.at[k % 2]).start)
            pl.when(dev != k)(load(k).start)

        lr_ref[RANK:, :] = jnp.zeros((LANE - RANK, D), BF16)
        start_load(0)
        pending, edge_out = [], []
        for k in range(NDEV):
            s = k % 2
            load(k).wait()
            if k:
                ebuf[k - 1] = buf[1 - s, WSTEP:WWIN, :] + buf[s, 0:16, :]
                edge_out.append(pltpu.make_async_copy(ebuf.at[k - 1], main_ref.at[pl.ds(_main_row(WSTEP * k), 16)], esems.at[k - 1]))
                edge_out[-1].start()
                for d in pending:
                    d.wait()
            if k + 1 < NDEV:
                start_load(k + 1)
            pending = []
            lo = WSTEP * k + (16 if k else 0)
            hi = WSTEP * k + (WWIN if k == NDEV - 1 else WSTEP)
            for a, b, where, dst in _window_pieces(lo, hi):
                if where == "lr":
                    lr_ref[dst:dst + b - a, :] = buf[s, a - WSTEP * k:b - WSTEP * k, :]
                else:
                    pending.append(pltpu.make_async_copy(buf.at[s, pl.ds(a - WSTEP * k, b - a)], main_ref.at[pl.ds(dst, b - a)],
                                                         out_sems.at[2 * s + len(pending)]))
                    pending[-1].start()
        for d in pending + edge_out:
            d.wait()

    return pl.pallas_call(
        body, name=name,
        out_shape=(jax.ShapeDtypeStruct((NMAIN, D), BF16), jax.ShapeDtypeStruct((LANE, D), BF16)),
        in_specs=[_ANY, _ANY], out_specs=(_ANY, pl.BlockSpec(memory_space=pltpu.VMEM)),
        scratch_shapes=[pltpu.VMEM((2, WWIN, D), BF16), pltpu.VMEM((edges, 16, D), BF16), pltpu.SemaphoreType.DMA((2,)),
                        pltpu.SemaphoreType.DMA((4,)), pltpu.SemaphoreType.DMA((edges,))],
        compiler_params=pltpu.CompilerParams(vmem_limit_bytes=VMEM_LIMIT),
    )(windows, own)


def _disassemble_w_in(d_main, d_lr, name="disassemble_w_in"):
    def body(main_ref, lr_ref, g_ref, buf, in_sems, out_sems):
        def loads(k):
            s, out = k % 2, []
            for a, b, where, src0 in _window_pieces(WSTEP * k, WSTEP * k + WWIN):
                if where == "main":
                    out.append(pltpu.make_async_copy(main_ref.at[pl.ds(src0, b - a)], buf.at[s, pl.ds(a - WSTEP * k, b - a)],
                                                     in_sems.at[2 * s + len(out)]))
            return out

        def store(k):
            return pltpu.make_async_copy(buf.at[k % 2], g_ref.at[k], out_sems.at[k % 2])

        for d in loads(0):
            d.start()
        for k in range(NDEV):
            for d in loads(k):
                d.wait()
            for a, b, where, src0 in _window_pieces(WSTEP * k, WSTEP * k + WWIN):
                if where == "lr":
                    buf[k % 2, a - WSTEP * k:b - WSTEP * k, :] = lr_ref[src0:src0 + b - a, :]
            if k:
                store(k - 1).wait()
            if k + 1 < NDEV:
                for d in loads(k + 1):
                    d.start()
            store(k).start()
        store(NDEV - 1).wait()

    return pl.pallas_call(
        body, name=name, out_shape=jax.ShapeDtypeStruct((NDEV, WWIN, D), BF16),
        in_specs=[_ANY, pl.BlockSpec(memory_space=pltpu.VMEM)], out_specs=_ANY,
        scratch_shapes=[pltpu.VMEM((2, WWIN, D), BF16), pltpu.SemaphoreType.DMA((4,)), pltpu.SemaphoreType.DMA((2,))],
        compiler_params=pltpu.CompilerParams(vmem_limit_bytes=VMEM_LIMIT),
    )(d_main, d_lr)


_HBM = pl.BlockSpec(memory_space=pltpu.HBM)
_SEM = pl.BlockSpec(memory_space=pltpu.SEMAPHORE)
_VMEM = pl.BlockSpec(memory_space=pltpu.VMEM)
_SIDE = pltpu.CompilerParams(has_side_effects=pltpu.SideEffectType.DATAFLOW_SIDE_EFFECTING)
_TOKEN = jax.ShapeDtypeStruct((8, LANE), F32)


def _hbm(a):
    return pltpu.with_memory_space_constraint(a, pltpu.HBM)


def _hbm_like(arrs):
    return tuple(pltpu.HBM(a.shape, a.dtype) for a in arrs)


def _tie(x, token):
    return x + token[0, 0].astype(x.dtype)


def _chip_copies(ins, lands, send, recv, nrel):
    x, y, c, chips = _mesh_pos()
    return [pltpu.make_async_remote_copy(src_ref=ins[a].at[2 * chip[0] + chip[1]], dst_ref=lands[a].at[j], send_sem=send.at[nrel * a + j],
                                         recv_sem=recv.at[nrel * a + j], device_id=(*chip, c), device_id_type=MESH)
            for a in range(len(ins)) for j, chip in enumerate(chips[:nrel])]


def _chip_start(psums, name, nrel=3):
    n = len(psums)
    lands = [lax.empty((nrel,) + p.shape[1:], p.dtype) for p in psums]

    def body(*refs):
        for d in _chip_copies(refs[:n], refs[n:2 * n], refs[2 * n], refs[2 * n + 1], nrel):
            d.start()
        refs[-1][...] = jnp.zeros_like(refs[-1])

    sems = pltpu.SemaphoreType.DMA((nrel * n,))
    out = pl.pallas_call(
        body, name=name, out_shape=(sems, sems) + _hbm_like(psums) + _hbm_like(lands) + (_TOKEN,),
        in_specs=[_HBM] * (2 * n), out_specs=(_SEM, _SEM) + (_HBM,) * (2 * n) + (_VMEM,),
        input_output_aliases={i: 2 + i for i in range(2 * n)}, compiler_params=_SIDE,
    )(*[_hbm(a) for a in list(psums) + lands])
    return out[0], out[1], list(out[2:2 + n]), list(out[2 + n:2 + 2 * n]), out[-1]


def _chip_wait(send, recv, psums, lands, after, name):
    n = len(psums)
    nrel = lands[0].shape[0]

    def body(*refs):
        for d in _chip_copies(refs[:n], refs[n:2 * n], refs[2 * n], refs[2 * n + 1], nrel):
            d.wait_send()
            d.wait_recv()

    out = pl.pallas_call(
        body, name=name, out_shape=_hbm_like(psums) + _hbm_like(lands),
        in_specs=[_HBM] * (2 * n) + [_SEM, _SEM, _ANY], out_specs=(_HBM,) * (2 * n),
        input_output_aliases={i: i for i in range(2 * n)}, compiler_params=_SIDE,
    )(*psums, *lands, send, recv, after)
    return list(out[:n]), list(out[n:])


def _hop_pos():
    x, y, c, _ = _mesh_pos()
    north = c == 1
    via = (jnp.where(north, 1 - x, x), jnp.where(north, y, 1 - y))
    return (*via, c), 2 * (1 - x) + (1 - y), jnp.where(north, 2 * x + (1 - y), 2 * (1 - x) + y)


def _hop_copies(ins, lands, send, recv):
    to, mine, _ = _hop_pos()
    return [pltpu.make_async_remote_copy(src_ref=ins[a].at[mine], dst_ref=lands[a], send_sem=send.at[a], recv_sem=recv.at[a],
                                         device_id=to, device_id_type=MESH) for a in range(len(ins))]


def _hop_start(psums, name):
    n = len(psums)
    lands = [lax.empty(p.shape[1:], p.dtype) for p in psums]

    def body(*refs):
        for d in _hop_copies(refs[:n], refs[n:2 * n], refs[2 * n], refs[2 * n + 1]):
            d.start()
        refs[-1][...] = jnp.zeros_like(refs[-1])

    sems = pltpu.SemaphoreType.DMA((n,))
    out = pl.pallas_call(
        body, name=name, out_shape=(sems, sems) + _hbm_like(psums) + _hbm_like(lands) + (_TOKEN,),
        in_specs=[_HBM] * (2 * n), out_specs=(_SEM, _SEM) + (_HBM,) * (2 * n) + (_VMEM,),
        input_output_aliases={i: 2 + i for i in range(2 * n)}, compiler_params=_SIDE,
    )(*[_hbm(a) for a in list(psums) + lands])
    return out[0], out[1], list(out[2:2 + n]), list(out[2 + n:2 + 2 * n]), out[-1]


def _hop_wait(send, recv, psums, lands, after, name):
    n = len(psums)

    def body(*refs):
        for d in _hop_copies(refs[:n], refs[n:2 * n], refs[2 * n], refs[2 * n + 1]):
            d.wait_send()
            d.wait_recv()

    out = pl.pallas_call(
        body, name=name, out_shape=_hbm_like(psums) + _hbm_like(lands),
        in_specs=[_HBM] * (2 * n) + [_SEM, _SEM, _ANY], out_specs=(_HBM,) * (2 * n),
        input_output_aliases={i: i for i in range(2 * n)}, compiler_params=_SIDE,
    )(*psums, *lands, send, recv, after)
    return list(out[:n]), list(out[n:])


def _hop_add(psums, land, idx, name, tr):
    _, R, C = psums.shape

    def body(s_ref, p_ref, l_ref, o_ref):
        o_ref[...] = (p_ref[...].astype(F32) + l_ref[...].astype(F32)).astype(BF16)

    blk = pl.BlockSpec((None, tr, C), lambda i, s: (s[0], i, 0))
    grid_spec = pltpu.PrefetchScalarGridSpec(num_scalar_prefetch=1, grid=(R // tr,),
                                             in_specs=[blk, pl.BlockSpec((tr, C), lambda i, s: (i, 0))], out_specs=blk)
    return pl.pallas_call(body, name=name, out_shape=jax.ShapeDtypeStruct(psums.shape, BF16), grid_spec=grid_spec,
                          input_output_aliases={1: 0}, compiler_params=_cp("parallel"))(idx, psums, land)


def _pair_copies(ins, lands, send, recv):
    x, y, c, _ = _mesh_pos()
    return [pltpu.make_async_remote_copy(src_ref=ins[a].at[:, 1 - c], dst_ref=lands[a], send_sem=send.at[a], recv_sem=recv.at[a],
                                         device_id=(x, y, 1 - c), device_id_type=MESH) for a in range(len(ins))]


def _pair_start(grads, name):
    n = len(grads)
    lands = [lax.empty((4,) + g.shape[2:], g.dtype) for g in grads]

    def body(*refs):
        for d in _pair_copies(refs[:n], refs[n:2 * n], refs[2 * n], refs[2 * n + 1]):
            d.start()
        refs[-1][...] = jnp.zeros_like(refs[-1])

    sems = pltpu.SemaphoreType.DMA((n,))
    out = pl.pallas_call(
        body, name=name, out_shape=(sems, sems) + _hbm_like(grads) + _hbm_like(lands) + (_TOKEN,),
        in_specs=[_HBM] * (2 * n), out_specs=(_SEM, _SEM) + (_HBM,) * (2 * n) + (_VMEM,),
        input_output_aliases={i: 2 + i for i in range(2 * n)}, compiler_params=_SIDE,
    )(*[_hbm(a) for a in list(grads) + lands])
    return out[0], out[1], list(out[2:2 + n]), list(out[2 + n:2 + 2 * n]), out[-1]


def _pair_wait(send, recv, grads, lands, after, name):
    n = len(grads)

    def body(*refs):
        for d in _pair_copies(refs[:n], refs[n:2 * n], refs[2 * n], refs[2 * n + 1]):
            d.wait_send()
            d.wait_recv()

    out = pl.pallas_call(
        body, name=name, out_shape=_hbm_like(grads) + _hbm_like(lands),
        in_specs=[_HBM] * (2 * n) + [_SEM, _SEM, _ANY], out_specs=(_HBM,) * (2 * n),
        input_output_aliases={i: i for i in range(2 * n)}, compiler_params=_SIDE,
    )(*grads, *lands, send, recv, after)
    return list(out[:n]), list(out[n:])


def _slot(chip, c):
    return 4 * chip[0] + 2 * chip[1] + c


def _gather_start(shards, lands, after, name):
    n = len(shards)

    def body(*refs):
        src, land, send, recv = refs[:n], refs[n:2 * n], refs[2 * n + 1], refs[2 * n + 2]
        x, y, c, chips = _mesh_pos()
        for a in range(n):
            for k, to in enumerate([(x, y, 1 - c)] + [(*chip, c) for chip in chips]):
                pltpu.make_async_remote_copy(src_ref=src[a], dst_ref=land[a].at[_slot((x, y), c)], send_sem=send.at[4 * a + k],
                                             recv_sem=recv.at[4 * a + k], device_id=to, device_id_type=MESH).start()
        refs[-1][...] = jnp.zeros_like(refs[-1])

    sems = pltpu.SemaphoreType.DMA((4 * n,))
    out = pl.pallas_call(
        body, name=name, out_shape=(sems, sems) + _hbm_like(shards) + _hbm_like(lands) + (_TOKEN,),
        in_specs=[_HBM] * (2 * n) + [_ANY], out_specs=(_SEM, _SEM) + (_HBM,) * (2 * n) + (_VMEM,),
        input_output_aliases={i: 2 + i for i in range(2 * n)}, compiler_params=_SIDE,
    )(*[_hbm(a) for a in list(shards) + list(lands)], after)
    return out[0], out[1], list(out[2:2 + n]), list(out[2 + n:2 + 2 * n]), out[-1]


def _gather_pass(lands, recv, after, name, first=0):
    n = len(lands)

    def body(*refs):
        land, recv1 = refs[:n], refs[n]
        send2, recv2 = refs[n + 2], refs[n + 3]
        x, y, c, chips = _mesh_pos()
        for a in range(n):
            for j, chip in enumerate(chips):
                blk = land[a].at[_slot(chip, c)]
                pltpu.make_async_remote_copy(src_ref=blk, dst_ref=blk, send_sem=send2.at[3 * a + j], recv_sem=recv1.at[4 * (first + a) + 1 + j],
                                             device_id=(*chip, c), device_id_type=MESH).wait_recv()
                pltpu.make_async_remote_copy(src_ref=blk, dst_ref=blk, send_sem=send2.at[3 * a + j], recv_sem=recv2.at[3 * a + j],
                                             device_id=(x, y, 1 - c), device_id_type=MESH).start()
        refs[-1][...] = jnp.zeros_like(refs[-1])

    sems = pltpu.SemaphoreType.DMA((3 * n,))
    out = pl.pallas_call(
        body, name=name, out_shape=(sems, sems) + _hbm_like(lands) + (_TOKEN,),
        in_specs=[_HBM] * n + [_SEM, _ANY], out_specs=(_SEM, _SEM) + (_HBM,) * n + (_VMEM,),
        input_output_aliases={i: 2 + i for i in range(n)}, compiler_params=_SIDE,
    )(*lands, recv, after)
    return out[0], out[1], list(out[2:2 + n]), out[-1]


def _gather_wait(shards, lands, send, recv, send2, recv2, after, name, first=0):
    n = len(lands)

    def body(*refs):
        src, land = refs[:n], refs[n:2 * n]
        send1, recv1, snd2, rcv2 = refs[2 * n:2 * n + 4]
        x, y, c, chips = _mesh_pos()
        sib = (x, y, 1 - c)
        for a in range(n):
            for k in range(4):
                pltpu.make_async_remote_copy(src_ref=src[a], dst_ref=land[a].at[_slot((x, y), c)], send_sem=send1.at[4 * (first + a) + k],
                                             recv_sem=recv1.at[4 * (first + a) + k], device_id=sib, device_id_type=MESH).wait_send()
            blk = land[a].at[_slot((x, y), 1 - c)]
            pltpu.make_async_remote_copy(src_ref=blk, dst_ref=blk, send_sem=send1.at[4 * (first + a)], recv_sem=recv1.at[4 * (first + a)],
                                         device_id=sib, device_id_type=MESH).wait_recv()
            for j, chip in enumerate(chips):
                mine, theirs = land[a].at[_slot(chip, c)], land[a].at[_slot(chip, 1 - c)]
                pltpu.make_async_remote_copy(src_ref=mine, dst_ref=mine, send_sem=snd2.at[3 * a + j], recv_sem=rcv2.at[3 * a + j],
                                             device_id=sib, device_id_type=MESH).wait_send()
                pltpu.make_async_remote_copy(src_ref=theirs, dst_ref=theirs, send_sem=snd2.at[3 * a + j], recv_sem=rcv2.at[3 * a + j],
                                             device_id=sib, device_id_type=MESH).wait_recv()

    out = pl.pallas_call(
        body, name=name, out_shape=_hbm_like(shards) + _hbm_like(lands),
        in_specs=[_HBM] * (2 * n) + [_SEM] * 4 + [_ANY], out_specs=(_HBM,) * (2 * n),
        input_output_aliases={i: i for i in range(2 * n)}, compiler_params=_SIDE,
    )(*shards, *lands, send, recv, send2, recv2, after)
    return list(out[n:])


def _win_tree():
    x, y, c, chips = _mesh_pos()
    north = c == 1
    handed = (jnp.where(north, 1 - x, x), jnp.where(north, y, 1 - y))
    hand_to = (jnp.where(north, x, 1 - x), jnp.where(north, 1 - y, y))
    return x, y, c, chips, handed, hand_to


def _blk(land, chip, c):
    return land.at[_slot(chip, c)]


def _rcopy(src, dst, send, recv, to):
    return pltpu.make_async_remote_copy(src_ref=src, dst_ref=dst, send_sem=send, recv_sem=recv, device_id=to, device_id_type=MESH)


def _win_start(shards, lands, name):
    n = len(shards)

    def body(*refs):
        src, land, send, recv = refs[:n], refs[n:2 * n], refs[2 * n], refs[2 * n + 1]
        x, y, c, chips, _, _ = _win_tree()
        for a in range(n):
            for k, to in enumerate([(x, y, 1 - c), (*chips[0], c), (*chips[1], c)]):
                _rcopy(src[a], _blk(land[a], (x, y), c), send.at[3 * a + k], recv.at[3 * a + k], to).start()
        refs[-1][...] = jnp.zeros_like(refs[-1])

    sems = pltpu.SemaphoreType.DMA((3 * n,))
    out = pl.pallas_call(
        body, name=name, out_shape=(sems, sems) + _hbm_like(shards) + _hbm_like(lands) + (_TOKEN,),
        in_specs=[_HBM] * (2 * n), out_specs=(_SEM, _SEM) + (_HBM,) * (2 * n) + (_VMEM,),
        input_output_aliases={i: 2 + i for i in range(2 * n)}, compiler_params=_SIDE,
    )(*[_hbm(a) for a in list(shards) + list(lands)])
    return out[0], out[1], list(out[2:2 + n]), list(out[2 + n:2 + 2 * n]), out[-1]


def _win_hand_on(lands, recv1, after, name):
    n, m = len(lands), len(after)

    def body(*refs):
        land, rcv1 = refs[:n], refs[n]
        send2, recv2 = refs[n + 1 + m], refs[n + 2 + m]
        x, y, c, chips, handed, hand_to = _win_tree()
        for a in range(n):
            for j in range(2):
                blk = _blk(land[a], chips[j], c)
                _rcopy(blk, blk, send2.at[3 * a], rcv1.at[3 * a + 1 + j], (*chips[j], c)).wait_recv()
            blk = _blk(land[a], handed, c)
            _rcopy(blk, blk, send2.at[3 * a], recv2.at[3 * a], (*hand_to, c)).start()
            for j in range(2):
                blk = _blk(land[a], chips[j], c)
                _rcopy(blk, blk, send2.at[3 * a + 1 + j], recv2.at[3 * a + 1 + j], (x, y, 1 - c)).start()
        refs[-1][...] = jnp.zeros_like(refs[-1])

    sems = pltpu.SemaphoreType.DMA((3 * n,))
    out = pl.pallas_call(
        body, name=name, out_shape=(sems, sems) + _hbm_like(lands) + (_TOKEN,),
        in_specs=[_HBM] * n + [_SEM] + [_ANY] * m, out_specs=(_SEM, _SEM) + (_HBM,) * n + (_VMEM,),
        input_output_aliases={i: 2 + i for i in range(n)}, compiler_params=_SIDE,
    )(*lands, recv1, *after)
    return out[0], out[1], list(out[2:2 + n]), out[-1]


def _win_last(lands, recv2, after, name):
    n, m = len(lands), len(after)

    def body(*refs):
        land, rcv2 = refs[:n], refs[n]
        send3, recv3 = refs[n + 1 + m], refs[n + 2 + m]
        x, y, c, chips, _, hand_to = _win_tree()
        for a in range(n):
            blk = _blk(land[a], chips[2], c)
            _rcopy(blk, blk, send3.at[a], rcv2.at[3 * a], (*hand_to, c)).wait_recv()
            _rcopy(blk, blk, send3.at[a], recv3.at[a], (x, y, 1 - c)).start()
        refs[-1][...] = jnp.zeros_like(refs[-1])

    sems = pltpu.SemaphoreType.DMA((n,))
    out = pl.pallas_call(
        body, name=name, out_shape=(sems, sems) + _hbm_like(lands) + (_TOKEN,),
        in_specs=[_HBM] * n + [_SEM] + [_ANY] * m, out_specs=(_SEM, _SEM) + (_HBM,) * n + (_VMEM,),
        input_output_aliases={i: 2 + i for i in range(n)}, compiler_params=_SIDE,
    )(*lands, recv2, *after)
    return out[0], out[1], list(out[2:2 + n]), out[-1]


def _win_wait(shards, lands, sems1, sems2, sems3, after, name):
    n = len(lands)

    def body(*refs):
        src, land = refs[:n], refs[n:2 * n]
        send1, recv1, send2, recv2, send3, recv3 = refs[2 * n:2 * n + 6]
        x, y, c, chips, handed, hand_to = _win_tree()
        sib = (x, y, 1 - c)
        for a in range(n):
            own = _blk(land[a], (x, y), c)
            for k in range(3):
                _rcopy(src[a], own, send1.at[3 * a + k], recv1.at[3 * a + k], sib).wait_send()
            blk = _blk(land[a], (x, y), 1 - c)
            _rcopy(blk, blk, send1.at[3 * a], recv1.at[3 * a], sib).wait_recv()
            blk = _blk(land[a], handed, c)
            _rcopy(blk, blk, send2.at[3 * a], recv2.at[3 * a], sib).wait_send()
            for j in range(2):
                mine, theirs = _blk(land[a], chips[j], c), _blk(land[a], chips[j], 1 - c)
                _rcopy(mine, mine, send2.at[3 * a + 1 + j], recv2.at[3 * a + 1 + j], sib).wait_send()
                _rcopy(theirs, theirs, send2.at[3 * a + 1 + j], recv2.at[3 * a + 1 + j], sib).wait_recv()
            mine, theirs = _blk(land[a], chips[2], c), _blk(land[a], chips[2], 1 - c)
            _rcopy(mine, mine, send3.at[a], recv3.at[a], sib).wait_send()
            _rcopy(theirs, theirs, send3.at[a], recv3.at[a], sib).wait_recv()

    out = pl.pallas_call(
        body, name=name, out_shape=_hbm_like(shards) + _hbm_like(lands),
        in_specs=[_HBM] * (2 * n) + [_SEM] * 6 + [_ANY], out_specs=(_HBM,) * (2 * n),
        input_output_aliases={i: i for i in range(2 * n)}, compiler_params=_SIDE,
    )(*shards, *lands, *sems1, *sems2, *sems3, after)
    return list(out[n:])


def _pad_to(v, n):
    return jnp.pad(v, [(0, 0)] * (v.ndim - 1) + [(0, n - v.shape[-1])])


def _pack_small(n1, gb, sk, gn, n2, fn, extra=None):
    parts = [n1.reshape(-1), gb.reshape(-1), sk.reshape(-1), gn.reshape(-1), n2.reshape(-1), fn.reshape(-1)]
    flat = jnp.concatenate(parts + ([extra.reshape(-1)] if extra is not None else []))
    return _pad_to(flat, SMALL_N).reshape(SMALL_ROWS, LANE)


def _unpack_small(p):
    f = p.reshape(-1)
    return (f[S_N1:S_GB].reshape(1, D), f[S_GB:S_SK].reshape(1, GH * DK), f[S_SK:S_GN].reshape(1, NQ), f[S_GN:S_N2].reshape(1, DV),
            f[S_N2:S_FN].reshape(1, D), f[S_FN:S_LOSS].reshape(D))


class _NoComm:
    def __init__(self, wo, wg_all, wu_all, wd_all):
        self.rest = (wo, wg_all, wu_all, wd_all)

    def mixed(self, gla_o, gla_norm_w):
        return gla_norm_w

    def w_out(self, merged, norm2_w):
        return self.rest[0], norm2_w

    def w_up(self, v2):
        return self.rest[1], self.rest[2]

    def w_down(self, ff):
        return self.rest[3]

    def ffn_grads(self, d_wg, d_wu, d_wd):
        self.ffn = (d_wg, d_wu, d_wd)

    def ffn_reduce(self, dv2, norm2_w):
        return norm2_w

    def in_grads(self, d_wmain, d_wlr, d_wo, w_lr):
        self.inw = (d_wmain, d_wlr, d_wo)
        return w_lr

    def in_reduce(self, du, norm1_w):
        return norm1_w


class _Comm:
    def __init__(self, rest_shards, rest_lands, after, c_idx):
        self.c_idx = c_idx
        self.send, self.recv, self.shards, self.lands, self.token = _gather_start(rest_shards, rest_lands, after, "gather_rest_start")

    def _pass(self, lo, hi, after, tag):
        send2, recv2, lands, token = _gather_pass(self.lands[lo:hi], self.recv, after, "gather_pass_" + tag, first=lo)
        self.passed = (lo, hi, send2, recv2, lands)
        return token

    def _wait(self, after, tag):
        lo, hi, send2, recv2, lands = self.passed
        return _gather_wait(self.shards[lo:hi], lands, self.send, self.recv, send2, recv2, after, "gather_wait_" + tag, first=lo)

    def mixed(self, gla_o, gla_norm_w):
        return _tie(gla_norm_w, self._pass(0, 1, gla_o, "out"))

    def w_out(self, merged, norm2_w):
        (wo_all,) = self._wait(merged, "out")
        return wo_all.reshape(D, D), _tie(norm2_w, self._pass(1, 3, merged, "up"))

    def w_up(self, v2):
        wg_all, wu_all = self._wait(v2, "up")
        self._pass(3, 4, v2, "down")
        return wg_all.reshape(FH, D), wu_all.reshape(FH, D)

    def w_down(self, ff):
        return self._wait(ff, "down")[0].reshape(FH, D)

    def _reduce(self, tag, names, grads, recv1, rows):
        psums = [_pair_add(g, r, self.c_idx, "pair_add_" + nm, tr) for g, r, nm, tr in zip(grads, recv1, names, rows)]
        *flight, token = _chip_start(psums, "reduce_chips_start_" + tag)
        return dict(tag=tag, names=names, rows=rows, flight=flight), token

    def ffn_grads(self, d_wg, d_wu, d_wd):
        self.ffn_pair = _pair_start([d.reshape(4, 2, FS, D) for d in (d_wg, d_wu, d_wd)], "reduce_pair_start_ffn")
        return self.ffn_pair[-1]

    def ffn_reduce(self, dv2, norm2_w):
        send, recv, grads, lands, _ = self.ffn_pair
        grads, recv1 = _pair_wait(send, recv, grads, lands, dv2, "reduce_pair_wait_ffn")
        self.ffn, token = self._reduce("ffn", ["w_ffn_gate", "w_ffn_up", "w_ffn_down"], grads, recv1, [176, 176, 176])
        return _tie(norm2_w, token)

    def in_grads(self, d_wmain, d_wlr, d_wo, w_lr):
        grads = [_disassemble_w_in(d_wmain, d_wlr).reshape(4, 2, WWIN, D), d_wo.reshape(4, 2, D // NDEV, D)]
        self.in_names, self.in_rows = ["w_in", "w_out"], [808, 256]
        psums = [_pair_add(g, r, self.c_idx, "pair_add_" + nm, tr)
                 for g, r, nm, tr in zip(grads, _pair_exchange(grads, "reduce_pair_in"), self.in_names, self.in_rows)]
        *self.in_hop, token = _hop_start(psums, "reduce_hop_start_in")
        return _tie(w_lr, token)

    def in_reduce(self, du, norm1_w):
        psums, lands = _hop_wait(*self.in_hop, du, "reduce_hop_wait_in")
        idx = _hop_pos()[2].astype(jnp.int32).reshape(1)
        psums = [_hop_add(p, l, idx, "hop_add_" + nm, tr) for p, l, nm, tr in zip(psums, lands, self.in_names, self.in_rows)]
        *flight, token = _chip_start(psums, "reduce_chips_start_in", nrel=2)
        self.inw = dict(tag="in", names=self.in_names, rows=self.in_rows, flight=flight)
        return _tie(norm1_w, token)


def _local_step(xs, tgt, u, norm1_w, gla_gate_b, attn_sinks, gla_norm_w, norm2_w, fnw, w_main, w_lr, w2p, comm):
    proj =_mm(u, w_main, tb=True, tm=1024, tn=1280, tk=D, name="in_proj")
    plr = _mm(u, w_lr, tb=True, tm=1024, tn=LANE, tk=D, name="in_proj_lr")
    attn_o = _attn_fwd(proj, attn_sinks)
    gla_o, states = _gla_fwd(proj, plr, w2p, gla_gate_b)
    merged = _merge_fwd(attn_o, gla_o, proj, comm.mixed(gla_o, gla_norm_w))
    wo, norm2_w = comm.w_out(merged, norm2_w)
    h1 = _mm(merged, wo, tm=1024, tn=512, tk=D, res=xs, name="out_proj")
    v2 = _rmsnorm_fwd(h1, norm2_w, "norm2_fwd")
    wg_all, wu_all = comm.w_up(v2)
    fa, fb, ff = _ffn_up(v2, wg_all, wu_all)
    wd_all = comm.w_down(ff)
    h2 = _mm(ff, wd_all, tm=1024, tn=1024, tk=FH // 2, res=h1, name="ffn_down")
    dh2, dh2b, d_fnw, loss_part = _loss_head(h2, fnw, tgt)

    da, db = _ffn_dact(dh2b, wd_all, fa, fb)
    Tn = xs.shape[0]
    d_wd = _mm(ff, dh2b, ta=True, tm=512, tn=D, tk=Tn, out_dtype=BF16, name="ffn_dwd")
    d_wg = _mm(da, v2, ta=True, tm=512, tn=D, tk=Tn, out_dtype=BF16, name="ffn_dwg")
    d_wu = _mm(db, v2, ta=True, tm=512, tn=D, tk=Tn, out_dtype=BF16, name="ffn_dwu")
    dv2 = _mm(da, wg_all, tm=1024, tn=1024, tk=FH // 2, after=comm.ffn_grads(d_wg, d_wu, d_wd), name="ffn_dv2_gate")
    dv2 = _mm(db, wu_all, tm=1024, tn=1024, tk=FH // 2, res=dv2, name="ffn_dv2_up")
    norm2_w = comm.ffn_reduce(dv2, norm2_w)
    dh1, dh1b, d_n2 = _rmsnorm_bwd(dv2, h1, norm2_w, dh2, "norm2_bwd")
    dmerged = _mm(dh1b, wo, tb=True, tm=1024, tn=512, tk=D, name="out_proj_dx")
    d_wo = _mm(merged, dh1b, ta=True, tm=1024, tn=512, tk=xs.shape[0], out_dtype=BF16, name="out_proj_dw")
    d_attn, d_gla, d_gates, d_gnw = _merge_bwd(dmerged, attn_o, gla_o, proj, gla_norm_w)
    d_q, d_kv, d_sinks = _attn_bwd(proj, attn_sinks, attn_o, d_attn)
    d_gqk, d_gv, d_plr, d_w2p, d_gb = _gla_bwd(proj, plr, w2p, gla_gate_b, states, d_gla)
    dproj = jnp.concatenate([d_q, d_kv, d_gqk, d_gv, d_gates], axis=1)
    d_wmain = _mm(dproj, u, ta=True, tm=640, tn=D, tk=xs.shape[0], out_dtype=BF16, name="in_proj_dw")
    d_wlr = _mm(d_plr, u, ta=True, tm=LANE, tn=1024, tk=xs.shape[0], out_dtype=BF16, name="in_proj_lr_dw")
    du_lr = _mm(d_plr, comm.in_grads(d_wmain, d_wlr, d_wo, w_lr), tm=1024, tn=1024, tk=LANE, name="in_proj_lr_dx")
    du = _mm(dproj, w_main, tm=1024, tn=1024, tk=2560, res=du_lr, name="in_proj_dx")
    dx, _, d_n1 = _rmsnorm_bwd(du, xs, comm.in_reduce(du, norm1_w), dh1, "norm1_bwd")
    return dx, loss_part, d_w2p, d_gb, d_sinks, d_gnw, d_n1, d_n2, d_fnw


def kernel(x, norm1_w, w_in, gla_gate_w2, gla_gate_b, attn_sinks, gla_norm_w, w_out, norm2_w, w_ffn_gate, w_ffn_up, w_ffn_down, final_norm_w, loss_target, m_norm1_w, m_w_in, m_gla_gate_w2, m_gla_gate_b, m_attn_sinks, m_gla_norm_w, m_w_out, m_norm2_w, m_w_ffn_gate, m_w_ffn_up, m_w_ffn_down, m_final_norm_w, v_norm1_w, v_w_in, v_gla_gate_w2, v_gla_gate_b, v_attn_sinks, v_gla_norm_w, v_w_out, v_norm2_w, v_w_ffn_gate, v_w_ffn_up, v_w_ffn_down, v_final_norm_w):
    xs, tgt = x[0], loss_target[0]
    fnw = final_norm_w.reshape(1, D)
    c_idx = lax.axis_index("c").astype(jnp.int32).reshape(1)
    dev = 4 * lax.axis_index("x") + 2 * lax.axis_index("y") + lax.axis_index("c")

    chip_idx = (2 * lax.axis_index("x") + lax.axis_index("y")).astype(jnp.int32).reshape(1)

    shift = (WS - WSTEP) * dev
    window = lax.dynamic_update_slice(jnp.zeros((WWIN, D), BF16), jnp.transpose(w_in[0]).astype(BF16), (shift, 0))
    w2_land = lax.dynamic_update_slice(lax.empty((NDEV, RANK, LANE), F32), gla_gate_w2, (dev, 0, 0))
    *sems1, win_srcs, win_lands, tok = _win_start([window, gla_gate_w2[0]], [lax.empty((NDEV, WWIN, D), BF16), w2_land], "gather_in_start")
    tr2 = lambda t: jnp.transpose(t[0])
    rows3 = lambda t: jnp.transpose(t[0] + tok[0, 0]).reshape(WS, D // LANE, LANE)
    rest = [(w + tok[0, 0]).astype(BF16) for w in (w_out[0], tr2(w_ffn_gate), tr2(w_ffn_up), w_ffn_down[0])]
    rest_lands = [lax.dynamic_update_slice(lax.empty((NDEV,) + s.shape, s.dtype), s[None], (dev, 0, 0)) for s in rest]
    win3 = [rows3(t) for t in (w_in, m_w_in, v_w_in)]
    *sems2, win_lands, tok = _win_hand_on(win_lands, sems1[1], rest + rest_lands + win3, "gather_in_hand_on")
    comm = _Comm(rest, rest_lands, tok, c_idx)
    u = _rmsnorm_fwd(xs, _tie(norm1_w, comm.token), "norm1_fwd")
    *sems3, win_lands, tok = _win_last(win_lands, sems2[1], [u], "gather_in_last")
    win_all, w2_all = _win_wait(win_srcs, win_lands, sems1, sems2, sems3, tok, "gather_in_wait")
    w_main, w_lr = _assemble_w_in(win_all, window)
    w2p = jnp.pad(jnp.transpose(w2_all, (1, 0, 2)).reshape(RANK, GH * DK), ((0, LANE - RANK), (0, 0)))

    dx, loss_part, d_w2p, d_gb, d_sinks, d_gnw, d_n1, d_n2, d_fnw = _local_step(
        xs, tgt, u, norm1_w, gla_gate_b, attn_sinks, gla_norm_w, norm2_w, fnw, w_main, w_lr, w2p, comm)

    pack = jnp.concatenate([_pack_small(d_n1, d_gb, d_sinks, d_gnw, d_n2, d_fnw, loss_part),
                            d_w2p[:RANK].reshape(GW2_ROWS, LANE)], axis=0)
    small = _sum_devices(_gather_small(pack))

    big = {}
    after = dx
    for grp in (comm.ffn, comm.inw):
        psums, parts = _chip_wait(*grp["flight"], after, "reduce_chips_wait_" + grp["tag"])
        for nm, ps, pt, tr in zip(grp["names"], psums, parts, grp["rows"]):
            w, m, v = {"w_in": (w_in, m_w_in, v_w_in), "w_out": (w_out, m_w_out, v_w_out), "w_ffn_gate": (w_ffn_gate, m_w_ffn_gate, v_w_ffn_gate),
                       "w_ffn_up": (w_ffn_up, m_w_ffn_up, v_w_ffn_up), "w_ffn_down": (w_ffn_down, m_w_ffn_down, v_w_ffn_down)}[nm]
            if nm == "w_in":
                g_win = _sum_parts(ps, pt, chip_idx, "sum_w_in", tr, 1024)
                g3 = lax.dynamic_slice(g_win, (shift, 0), (WS, D)).reshape(WS, D // LANE, LANE)
                out3 = (g3,) + tuple(_adamw_rows(*win3, g3, "adamw_w_in", 178))
                big[nm] = [jnp.transpose(t.reshape(WS, D))[None] for t in out3]
            elif nm in ("w_ffn_gate", "w_ffn_up"):
                big[nm] = [jnp.transpose(t)[None] for t in _adamw(tr2(w), tr2(m), tr2(v), ps, pt, chip_idx, "adamw_" + nm, tr)]
            else:
                big[nm] = [t[None] for t in _adamw(w[0], m[0], v[0], ps, pt, chip_idx, "adamw_" + nm, tr)]
            after = big[nm][0]
    g_small = small[:SMALL_ROWS]
    sm = _adamw_plain(_pack_small(norm1_w, gla_gate_b, attn_sinks, gla_norm_w, norm2_w, final_norm_w),
                      _pack_small(m_norm1_w, m_gla_gate_b, m_attn_sinks, m_gla_norm_w, m_norm2_w, m_final_norm_w),
                      _pack_small(v_norm1_w, v_gla_gate_b, v_attn_sinks, v_gla_norm_w, v_norm2_w, v_final_norm_w), g_small, "adamw_small")
    g_w2 = lax.dynamic_slice_in_dim(small[SMALL_ROWS:].reshape(RANK, GH * DK), dev * LANE, LANE, axis=1)
    w2 = [g_w2[None]] + [t[None] for t in _adamw_plain(gla_gate_w2[0], m_gla_gate_w2[0], v_gla_gate_w2[0], g_w2, "adamw_w2")]
    loss = g_small.reshape(-1)[S_LOSS]

    sg, sd, sm2, sv2 = [_unpack_small(t) for t in (g_small,) + tuple(sm)]

    def group(i, s):
        return (s[0], big["w_in"][i], w2[i], s[1], s[2], s[3], big["w_out"][i], s[4], big["w_ffn_gate"][i], big["w_ffn_up"][i],
                big["w_ffn_down"][i], s[5])

    return (loss, dx[None], *group(0, sg), *group(1, sd), *group(2, sm2), *group(3, sv2))
```

```python
import functools

import jax
import jax.numpy as jnp
from jax import lax
from jax.experimental import pallas as pl
from jax.experimental.pallas import tpu as pltpu

F32, BF16 = jnp.float32, jnp.bfloat16
HIGHEST = lax.Precision.HIGHEST

D = 2048
HD, NQ, NKV, GRP, WIN = 64, 32, 4, 8, 128
GH, DK, DV, RANK, GC = 4, 256, 512, 16, 64
FH, NDEV = 5632, 8
FS = FH // NDEV
DIN = 12816
WS = DIN // NDEV
EPS = 1e-6
MASKV = -1e30
LANE = 128

C_AQ, C_AK, C_AV, C_GQ, C_GK, C_GV, C_GR, C_GA, C_GB, NMAIN = 0, 2048, 2304, 2560, 3584, 4608, 6656, 8704, 10752, 12800
C_LR = 6656
WSTEP, WWIN = 1600, 1616

LR, B1, B2, AEPS, WD, STEP = 0.001, 0.9, 0.999, 1e-08, 0.01, 10

S_N1, S_GB, S_SK, S_GN, S_N2, S_FN, S_LOSS, SMALL_N = 0, 2048, 3072, 3104, 3616, 5664, 7712, 8192
SMALL_ROWS = SMALL_N // LANE
GW2_ROWS = RANK * GH * DK // LANE
PACK_ROWS = SMALL_ROWS + GW2_ROWS

MESH = pl.DeviceIdType.MESH


def _dot(a, b, ta=False, tb=False, prec=None):
    dn = (((0,) if ta else (1,), (1,) if tb else (0,)), ((), ()))
    return lax.dot_general(a, b, dn, preferred_element_type=F32, precision=prec)


def _sigmoid(x):
    return 1.0 / (1.0 + jnp.exp(-x))


VMEM_LIMIT = 56 * 1024 * 1024


def _cp(*sem):
    return pltpu.CompilerParams(dimension_semantics=sem, vmem_limit_bytes=VMEM_LIMIT)


def _mm(a, b, *, ta=False, tb=False, tm, tn, tk, out_dtype=F32, res=None, after=None, name):
    M, K = (a.shape[1], a.shape[0]) if ta else a.shape
    N = b.shape[0] if tb else b.shape[1]
    tm, tn, tk = min(tm, M), min(tn, N), min(tk, K)
    nk = K // tk
    assert M % tm == 0 and N % tn == 0 and K % tk == 0
    a_spec = pl.BlockSpec((tk, tm), lambda i, j, k: (k, i)) if ta else pl.BlockSpec((tm, tk), lambda i, j, k: (i, k))
    b_spec = pl.BlockSpec((tn, tk), lambda i, j, k: (j, k)) if tb else pl.BlockSpec((tk, tn), lambda i, j, k: (k, j))
    o_spec = pl.BlockSpec((tm, tn), lambda i, j, k: (i, j))
    has_res = res is not None

    def body(*refs):
        a_ref, b_ref = refs[0], refs[1]
        r_ref = refs[2] if has_res else None
        o_ref = refs[2 + has_res + (after is not None)]
        p = _dot(a_ref[...].astype(BF16), b_ref[...].astype(BF16), ta, tb)
        if nk == 1:
            if has_res:
                p = p + r_ref[...]
            o_ref[...] = p.astype(out_dtype)
        else:
            acc = refs[-1]
            k = pl.program_id(2)

            @pl.when(k == 0)
            def _():
                acc[...] = (p + r_ref[...]) if has_res else p

            @pl.when(k > 0)
            def _():
                acc[...] += p

            @pl.when(k == nk - 1)
            def _():
                o_ref[...] = acc[...].astype(out_dtype)

    return pl.pallas_call(
        body, name=name,
        out_shape=jax.ShapeDtypeStruct((M, N), out_dtype),
        grid=(M // tm, N // tn, nk),
        in_specs=[a_spec, b_spec] + ([o_spec] if has_res else []) + ([pl.BlockSpec(memory_space=pl.ANY)] if after is not None else []),
        out_specs=o_spec,
        scratch_shapes=[pltpu.VMEM((tm, tn), F32)] if nk > 1 else [],
        compiler_params=_cp("parallel", "parallel", "arbitrary"),
    )(*((a, b) + ((res,) if has_res else ()) + ((after,) if after is not None else ())))


def _rmsnorm_fwd(x, w, name, tm=256):
    Tn = x.shape[0]

    def body(x_ref, w_ref, o_ref):
        xv = x_ref[...]
        r = lax.rsqrt(jnp.mean(xv * xv, axis=1, keepdims=True) + EPS)
        o_ref[...] = (xv * r * w_ref[...]).astype(BF16)

    return pl.pallas_call(
        body, name=name, out_shape=jax.ShapeDtypeStruct((Tn, D), BF16), grid=(Tn // tm,),
        in_specs=[pl.BlockSpec((tm, D), lambda i: (i, 0)), pl.BlockSpec((1, D), lambda i: (0, 0))],
        out_specs=pl.BlockSpec((tm, D), lambda i: (i, 0)), compiler_params=_cp("parallel"),
    )(x, w)


def _rmsnorm_bwd(dy, h, w, res, name, tm=256):
    Tn = h.shape[0]

    def body(dy_ref, h_ref, w_ref, res_ref, dh_ref, dhb_ref, dw_ref):
        hv, dyv = h_ref[...], dy_ref[...]
        r = lax.rsqrt(jnp.mean(hv * hv, axis=1, keepdims=True) + EPS)
        g = dyv * w_ref[...]
        dh = res_ref[...] + r * g - hv * (r * r * r * jnp.mean(g * hv, axis=1, keepdims=True))
        dh_ref[...] = dh
        dhb_ref[...] = dh.astype(BF16)
        part = jnp.sum(dyv * hv * r, axis=0, keepdims=True)

        @pl.when(pl.program_id(0) == 0)
        def _():
            dw_ref[...] = part

        @pl.when(pl.program_id(0) > 0)
        def _():
            dw_ref[...] += part

    row = pl.BlockSpec((tm, D), lambda i: (i, 0))
    vec = pl.BlockSpec((1, D), lambda i: (0, 0))
    return pl.pallas_call(
        body, name=name,
        out_shape=(jax.ShapeDtypeStruct((Tn, D), F32), jax.ShapeDtypeStruct((Tn, D), BF16), jax.ShapeDtypeStruct((1, D), F32)),
        grid=(Tn // tm,), in_specs=[row, row, vec, row], out_specs=(row, row, vec), compiler_params=_cp("arbitrary"),
    )(dy, h, w, res)


def _loss_head(h2, wf, tgt, name="loss_head", tm=256):
    Tn = h2.shape[0]

    def body(h_ref, w_ref, t_ref, dh_ref, dhb_ref, dw_ref, loss_ref):
        hv, wv = h_ref[...], w_ref[...]
        r = lax.rsqrt(jnp.mean(hv * hv, axis=1, keepdims=True) + EPS)
        hn = hv * r
        e = hn * wv - t_ref[...]
        dy = e * (1.0 / D)
        g = dy * wv
        dh = r * g - hv * (r * r * r * jnp.mean(g * hv, axis=1, keepdims=True))
        dh_ref[...] = dh
        dhb_ref[...] = dh.astype(BF16)
        part = jnp.sum(dy * hn, axis=0, keepdims=True)
        lpart = (0.5 / D) * jnp.sum(jnp.sum(e * e, axis=1, keepdims=True), axis=0, keepdims=True)

        @pl.when(pl.program_id(0) == 0)
        def _():
            dw_ref[...] = part
            loss_ref[...] = lpart

        @pl.when(pl.program_id(0) > 0)
        def _():
            dw_ref[...] += part
            loss_ref[...] += lpart

    row = pl.BlockSpec((tm, D), lambda i: (i, 0))
    vec = pl.BlockSpec((1, D), lambda i: (0, 0))
    one = pl.BlockSpec((1, 1), lambda i: (0, 0))
    return pl.pallas_call(
        body, name=name,
        out_shape=(jax.ShapeDtypeStruct((Tn, D), F32), jax.ShapeDtypeStruct((Tn, D), BF16), jax.ShapeDtypeStruct((1, D), F32),
                   jax.ShapeDtypeStruct((1, 1), F32)),
        grid=(Tn // tm,), in_specs=[row, vec, row], out_specs=(row, row, vec, one), compiler_params=_cp("arbitrary"),
    )(h2, wf, tgt)


def _attn_mask(n):
    qi = lax.broadcasted_iota(jnp.int32, (NKV, GRP * WIN, 2 * WIN), 1) % WIN
    ki = lax.broadcasted_iota(jnp.int32, (NKV, GRP * WIN, 2 * WIN), 2)
    rel = qi + WIN - ki
    return (rel >= 0) & (rel < WIN) & ((n > 0) | (ki >= WIN))


def _kv_heads(prev_ref, cur_ref):
    return jnp.stack([jnp.concatenate([prev_ref[:, h * HD:(h + 1) * HD], cur_ref[:, h * HD:(h + 1) * HD]], axis=0) for h in range(NKV)])


def _q_heads(ref):
    return jnp.stack([jnp.concatenate([ref[:, (h * GRP + g) * HD:(h * GRP + g + 1) * HD] for g in range(GRP)], axis=0) for h in range(NKV)])


def _attn_probs(q_ref, kc_ref, kp_ref, sink_ref, mask):
    kk = _kv_heads(kp_ref, kc_ref).astype(BF16)
    qs = _q_heads(q_ref).astype(BF16)
    s = jnp.einsum('hqd,hkd->hqk', qs, kk, preferred_element_type=F32) * (HD ** -0.5)
    s = jnp.where(mask, s, MASKV)
    sink = jnp.stack([jnp.concatenate([jnp.full((WIN, 1), sink_ref[0, h * GRP + g], F32) for g in range(GRP)], axis=0) for h in range(NKV)])
    m = jnp.maximum(jnp.max(s, axis=2, keepdims=True), sink)
    e = jnp.exp(s - m)
    es = jnp.exp(sink - m)
    inv = 1.0 / (jnp.sum(e, axis=2, keepdims=True) + es)
    return e * inv, es * inv, qs, kk


def _attn_specs(nb, last):
    cur = lambda n: jnp.minimum(n, last)
    prev = lambda n: jnp.maximum(jnp.minimum(n, last) - 1, 0)
    return [
        pl.BlockSpec((WIN, NQ * HD), lambda n: (cur(n), C_AQ // (NQ * HD))),
        pl.BlockSpec((WIN, NKV * HD), lambda n: (cur(n), C_AK // (NKV * HD))),
        pl.BlockSpec((WIN, NKV * HD), lambda n: (prev(n), C_AK // (NKV * HD))),
        pl.BlockSpec((WIN, NKV * HD), lambda n: (cur(n), C_AV // (NKV * HD))),
        pl.BlockSpec((WIN, NKV * HD), lambda n: (prev(n), C_AV // (NKV * HD))),
    ]


def _attn_fwd(proj, sinks, name="attn_fwd"):
    Tn = proj.shape[0]
    nb = Tn // WIN

    def body(q_ref, kc_ref, kp_ref, vc_ref, vp_ref, sink_ref, o_ref):
        p, _, _, _ = _attn_probs(q_ref, kc_ref, kp_ref, sink_ref, _attn_mask(pl.program_id(0)))
        o = jnp.einsum('hqk,hkd->hqd', p.astype(BF16), _kv_heads(vp_ref, vc_ref).astype(BF16), preferred_element_type=F32)
        for h in range(NKV):
            for g in range(GRP):
                o_ref[:, (h * GRP + g) * HD:(h * GRP + g + 1) * HD] = o[h, g * WIN:(g + 1) * WIN, :]

    return pl.pallas_call(
        body, name=name, out_shape=jax.ShapeDtypeStruct((Tn, D), F32), grid=(nb,),
        in_specs=_attn_specs(nb, nb - 1) + [pl.BlockSpec(memory_space=pltpu.SMEM)],
        out_specs=pl.BlockSpec((WIN, D), lambda n: (n, 0)), compiler_params=_cp("parallel"),
    )(proj, proj, proj, proj, proj, sinks)


def _attn_bwd(proj, sinks, o, do, name="attn_bwd"):
    Tn = proj.shape[0]
    nb = Tn // WIN
    KW = NKV * HD

    def body(q_ref, kc_ref, kp_ref, vc_ref, vp_ref, o_ref, do_ref, sink_ref, dq_ref, dkv_ref, dsk_ref, carry, cur):
        n = pl.program_id(0)

        @pl.when(n == 0)
        def _():
            carry[...] = jnp.zeros_like(carry)
            dsk_ref[...] = jnp.zeros_like(dsk_ref)

        @pl.when(n < nb)
        def _():
            p, ps, qs, kk = _attn_probs(q_ref, kc_ref, kp_ref, sink_ref, _attn_mask(n))
            vv = _kv_heads(vp_ref, vc_ref).astype(BF16)
            dos = _q_heads(do_ref)
            delta = jnp.sum(dos * _q_heads(o_ref), axis=2, keepdims=True)
            dosb = dos.astype(BF16)
            dp = jnp.einsum('hqd,hkd->hqk', dosb, vv, preferred_element_type=F32)
            ds = (p * (dp - delta) * (HD ** -0.5)).astype(BF16)
            dq = jnp.einsum('hqk,hkd->hqd', ds, kk, preferred_element_type=F32)
            dkk = jnp.einsum('hqk,hqd->hkd', ds, qs, preferred_element_type=F32)
            dvv = jnp.einsum('hqk,hqd->hkd', p.astype(BF16), dosb, preferred_element_type=F32)
            dsk = ps * delta
            for h in range(NKV):
                for g in range(GRP):
                    i = h * GRP + g
                    dq_ref[:, i * HD:(i + 1) * HD] = dq[h, g * WIN:(g + 1) * WIN, :].astype(BF16)
                    dsk_ref[:, i:i + 1] -= jnp.sum(dsk[h, g * WIN:(g + 1) * WIN, :], axis=0, keepdims=True)
                dkv_ref[:, h * HD:(h + 1) * HD] = (carry[:, h * HD:(h + 1) * HD] + dkk[h, :WIN, :]).astype(BF16)
                dkv_ref[:, KW + h * HD:KW + (h + 1) * HD] = (carry[:, KW + h * HD:KW + (h + 1) * HD] + dvv[h, :WIN, :]).astype(BF16)
                cur[:, h * HD:(h + 1) * HD] = dkk[h, WIN:, :]
                cur[:, KW + h * HD:KW + (h + 1) * HD] = dvv[h, WIN:, :]
            carry[...] = cur[...]

        @pl.when(n == nb)
        def _():
            dkv_ref[...] = carry[...].astype(BF16)

    last = nb - 1
    row = pl.BlockSpec((WIN, D), lambda n: (jnp.minimum(n, last), 0))
    return pl.pallas_call(
        body, name=name,
        out_shape=(jax.ShapeDtypeStruct((Tn, D), BF16), jax.ShapeDtypeStruct((Tn, 2 * KW), BF16), jax.ShapeDtypeStruct((1, NQ), F32)),
        grid=(nb + 1,),
        in_specs=_attn_specs(nb, last) + [row, row, pl.BlockSpec(memory_space=pltpu.SMEM)],
        out_specs=(row, pl.BlockSpec((WIN, 2 * KW), lambda n: (jnp.maximum(n - 1, 0), 0)), pl.BlockSpec((1, NQ), lambda n: (0, 0))),
        scratch_shapes=[pltpu.VMEM((WIN, 2 * KW), F32), pltpu.VMEM((WIN, 2 * KW), F32)],
        compiler_params=_cp("arbitrary"),
    )(proj, proj, proj, proj, proj, o, do, sinks)


def _tri(lower):
    r = lax.broadcasted_iota(jnp.int32, (GC, GC), 0)
    c = lax.broadcasted_iota(jnp.int32, (GC, GC), 1)
    return r >= c if lower else r <= c


def _per_head(a):
    return jnp.stack([a[:, h * DK:(h + 1) * DK] for h in range(GH)])


def _all_heads(a):
    return jnp.concatenate([a[h] for h in range(GH)], axis=1)


def _gla_gates(lr, w2_ref, gb_ref):
    logit = _dot(lr, w2_ref[...].astype(BF16)) + gb_ref[...]
    la = (jnp.minimum(logit, 0.0) - jnp.log(1.0 + jnp.exp(-jnp.abs(logit)))) * (1.0 / 16.0)
    g = _dot(_tri(True).astype(F32), la, prec=HIGHEST)
    return logit, g


def _bmm(spec, a, b):
    return jnp.einsum(spec, a, b, preferred_element_type=F32)


def _gla_specs(nc, rev):
    idx = (lambda n: nc - 1 - n) if rev else (lambda n: n)
    half = 2 * DK
    return (
        [pl.BlockSpec((GC, half), lambda n, j=j: (idx(n), C_GQ // half + j)) for j in range(2)]
        + [pl.BlockSpec((GC, half), lambda n, j=j: (idx(n), C_GK // half + j)) for j in range(2)]
        + [pl.BlockSpec((GC, DV), lambda n, h=h: (idx(n), C_GV // DV + h)) for h in range(GH)]
        + [pl.BlockSpec((GC, LANE), lambda n: (idx(n), 0)), pl.BlockSpec((LANE, GH * DK), lambda n: (0, 0)),
           pl.BlockSpec((1, GH * DK), lambda n: (0, 0))])


def _gla_heads(refs):
    return (lambda h: refs[h // 2][:, (h % 2) * DK:(h % 2 + 1) * DK], lambda h: refs[2 + h // 2][:, (h % 2) * DK:(h % 2 + 1) * DK],
            lambda h: refs[4 + h][...])


def _gla_fwd(proj, plr, w2p, gb, name="gla_fwd"):
    Tn = proj.shape[0]
    nc = Tn // GC

    def body(*refs):
        qh, kh, vh = _gla_heads(refs)
        lr_ref, w2_ref, gb_ref, o_ref, st_ref, S = refs[8:]

        @pl.when(pl.program_id(0) == 0)
        def _():
            S[...] = jnp.zeros_like(S)

        heads = lambda f: jnp.stack([f(h) for h in range(GH)])
        _, g_all = _gla_gates(lr_ref[...].astype(BF16), w2_ref, gb_ref)
        g = _per_head(g_all)
        gl = g[:, GC - 1:GC, :]
        k = heads(kh)
        v = heads(vh).astype(BF16)
        qd = (heads(qh) * (DK ** -0.5) * jnp.exp(g)).astype(BF16)
        ki = (k * jnp.exp(-g)).astype(BF16)
        ke = (k * jnp.exp(gl - g)).astype(BF16)
        att = jnp.where(_tri(True)[None], _bmm('hid,hjd->hij', qd, ki), 0.0).astype(BF16)
        sp = S[...]
        st_ref[0] = sp
        o = _bmm('hij,hjv->hiv', att, v) + _bmm('hid,hvd->hiv', qd, sp.astype(BF16))
        for h in range(GH):
            o_ref[:, h * DV:(h + 1) * DV] = o[h]
        S[...] = sp * jnp.exp(gl) + _bmm('hjv,hjd->hvd', v, ke)

    return pl.pallas_call(
        body, name=name,
        out_shape=(jax.ShapeDtypeStruct((Tn, GH * DV), F32), jax.ShapeDtypeStruct((nc, GH, DV, DK), F32)),
        grid=(nc,), in_specs=_gla_specs(nc, False),
        out_specs=(pl.BlockSpec((GC, GH * DV), lambda n: (n, 0)), pl.BlockSpec((1, GH, DV, DK), lambda n: (n, 0, 0, 0))),
        scratch_shapes=[pltpu.VMEM((GH, DV, DK), F32)], compiler_params=_cp("arbitrary"),
    )(*([proj] * 8), plr, w2p, gb)


def _gla_bwd(proj, plr, w2p, gb, states, do, name="gla_bwd"):
    Tn = proj.shape[0]
    nc = Tn // GC

    def body(*refs):
        qh, kh, vh = _gla_heads(refs)
        lr_ref, w2_ref, gb_ref, st_ref, do_ref, dqk_ref, dv_ref, dlr_ref, dw2_ref, dgb_ref, dS = refs[8:]

        @pl.when(pl.program_id(0) == 0)
        def _():
            dS[...] = jnp.zeros_like(dS)
            dw2_ref[...] = jnp.zeros_like(dw2_ref)
            dgb_ref[...] = jnp.zeros_like(dgb_ref)

        heads = lambda f: jnp.stack([f(h) for h in range(GH)])
        lr = lr_ref[...].astype(BF16)
        causal = _tri(True)[None]
        last_row = lax.broadcasted_iota(jnp.int32, (GH, GC, DK), 1) == GC - 1
        logit, g_all = _gla_gates(lr, w2_ref, gb_ref)
        g = _per_head(g_all)
        gl = g[:, GC - 1:GC, :]
        egl = jnp.exp(gl)
        eg, eng, ege = jnp.exp(g), jnp.exp(-g), jnp.exp(gl - g)
        k = heads(kh)
        v = heads(vh).astype(BF16)
        dob = heads(lambda h: do_ref[:, h * DV:(h + 1) * DV]).astype(BF16)
        qd = heads(qh) * (DK ** -0.5) * eg
        ki = k * eng
        ke = k * ege
        qdb, kib, keb = qd.astype(BF16), ki.astype(BF16), ke.astype(BF16)
        att = jnp.where(causal, _bmm('hid,hjd->hij', qdb, kib), 0.0).astype(BF16)
        datt = jnp.where(causal, _bmm('hiv,hjv->hij', dob, v), 0.0).astype(BF16)
        sp = st_ref[0]
        dsn = dS[...]
        dsnb = dsn.astype(BF16)
        dv = (_bmm('hij,hiv->hjv', att, dob) + _bmm('hjd,hvd->hjv', keb, dsnb)).astype(BF16)
        dqd = _bmm('hij,hjd->hid', datt, kib) + _bmm('hiv,hvd->hid', dob, sp.astype(BF16))
        dki = _bmm('hij,hid->hjd', datt, qdb)
        dke = _bmm('hjv,hvd->hjd', v, dsnb)
        ddec = jnp.sum(dsn * sp, axis=1, keepdims=True)
        dS[...] = dsn * egl + _bmm('hiv,hid->hvd', dob, qdb)
        dke_ke = dke * ke
        dgl = jnp.sum(dke_ke, axis=1, keepdims=True) + ddec * egl
        dg = dqd * qd - dki * ki - dke_ke + jnp.where(last_row, dgl, 0.0)
        dq = (dqd * ((DK ** -0.5) * eg)).astype(BF16)
        dk = (dki * eng + dke * ege).astype(BF16)
        for h in range(GH):
            dv_ref[:, h * DV:(h + 1) * DV] = dv[h]
            dqk_ref[:, h * DK:(h + 1) * DK] = dq[h]
            dqk_ref[:, GH * DK + h * DK:GH * DK + (h + 1) * DK] = dk[h]
        dla = _dot(_tri(False).astype(F32), _all_heads(dg), prec=HIGHEST)
        dlogit = dla * (1.0 / 16.0) * _sigmoid(-logit)
        dlb = dlogit.astype(BF16)
        dlr_ref[...] = _dot(dlb, w2_ref[...].astype(BF16), tb=True).astype(BF16)
        dw2_ref[...] += _dot(lr, dlb, ta=True)
        dgb_ref[...] += jnp.sum(dlogit, axis=0, keepdims=True)

    rev = lambda n: nc - 1 - n
    row = pl.BlockSpec((GC, GH * DV), lambda n: (rev(n), 0))
    return pl.pallas_call(
        body, name=name,
        out_shape=(jax.ShapeDtypeStruct((Tn, 2 * GH * DK), BF16), jax.ShapeDtypeStruct((Tn, GH * DV), BF16),
                   jax.ShapeDtypeStruct((Tn, LANE), BF16), jax.ShapeDtypeStruct((LANE, GH * DK), F32),
                   jax.ShapeDtypeStruct((1, GH * DK), F32)),
        grid=(nc,),
        in_specs=_gla_specs(nc, True) + [pl.BlockSpec((1, GH, DV, DK), lambda n: (rev(n), 0, 0, 0)), row],
        out_specs=(row, row, pl.BlockSpec((GC, LANE), lambda n: (rev(n), 0)), pl.BlockSpec((LANE, GH * DK), lambda n: (0, 0)),
                   pl.BlockSpec((1, GH * DK), lambda n: (0, 0))),
        scratch_shapes=[pltpu.VMEM((GH, DV, DK), F32)], compiler_params=_cp("arbitrary"),
    )(*([proj] * 8), plr, w2p, gb, states, do)


def _merge_specs(tm):
    row = pl.BlockSpec((tm, D), lambda i: (i, 0))
    gates = [pl.BlockSpec((tm, DV), lambda i, j=c // DV + h: (i, j)) for c in (C_GR, C_GA, C_GB) for h in range(GH)]
    return row, gates, pl.BlockSpec((1, DV), lambda i: (0, 0))


def _merge_fwd(a, go, proj, gnw, name="merge_fwd", tm=256):
    Tn = a.shape[0]

    def body(a_ref, go_ref, *rest):
        gates, w_ref, m_ref = rest[:3 * GH], rest[3 * GH], rest[3 * GH + 1]
        for h in range(GH):
            sl = slice(h * DV, (h + 1) * DV)
            gov = go_ref[:, sl]
            r = lax.rsqrt(jnp.mean(gov * gov, axis=1, keepdims=True) + EPS)
            gr = gates[h][...]
            g2 = gov * r * w_ref[...] * (gr * _sigmoid(gr))
            m_ref[:, sl] = (_sigmoid(gates[GH + h][...]) * a_ref[:, sl] + _sigmoid(gates[2 * GH + h][...]) * g2).astype(BF16)

    row, gates, vec = _merge_specs(tm)
    return pl.pallas_call(
        body, name=name, out_shape=jax.ShapeDtypeStruct((Tn, D), BF16), grid=(Tn // tm,),
        in_specs=[row, row] + gates + [vec], out_specs=row, compiler_params=_cp("parallel"),
    )(a, go, *([proj] * (3 * GH)), gnw)


def _merge_bwd(dm, a, go, proj, gnw, name="merge_bwd", tm=256):
    Tn = a.shape[0]

    def body(dm_ref, a_ref, go_ref, *rest):
        gates = rest[:3 * GH]
        w_ref, da_ref, dgo_ref, dg_ref, dw_ref = rest[3 * GH:]
        wv = w_ref[...]
        dw = jnp.zeros((1, DV), F32)
        for h in range(GH):
            sl = slice(h * DV, (h + 1) * DV)
            dmv, av, gov, gr = dm_ref[:, sl], a_ref[:, sl], go_ref[:, sl], gates[h][...]
            sa, sb, sg = _sigmoid(gates[GH + h][...]), _sigmoid(gates[2 * GH + h][...]), _sigmoid(gr)
            r = lax.rsqrt(jnp.mean(gov * gov, axis=1, keepdims=True) + EPS)
            gn0 = gov * r
            gn = gn0 * wv
            silu = gr * sg
            dg2 = dmv * sb
            da_ref[:, sl] = dmv * sa
            dg_ref[:, D + h * DV:D + (h + 1) * DV] = (dmv * av * sa * (1.0 - sa)).astype(BF16)
            dg_ref[:, 2 * D + h * DV:2 * D + (h + 1) * DV] = (dg2 * gn * silu * (1.0 - sb)).astype(BF16)
            dg_ref[:, sl] = (dg2 * gn * (sg * (1.0 + gr * (1.0 - sg)))).astype(BF16)
            dgn = dg2 * silu
            dw = dw + jnp.sum(dgn * gn0, axis=0, keepdims=True)
            gg = dgn * wv
            dgo_ref[:, sl] = r * gg - gov * (r * r * r * jnp.mean(gg * gov, axis=1, keepdims=True))

        @pl.when(pl.program_id(0) == 0)
        def _():
            dw_ref[...] = dw

        @pl.when(pl.program_id(0) > 0)
        def _():
            dw_ref[...] += dw

    row, gates, vec = _merge_specs(tm)
    return pl.pallas_call(
        body, name=name,
        out_shape=(jax.ShapeDtypeStruct((Tn, D), F32), jax.ShapeDtypeStruct((Tn, D), F32), jax.ShapeDtypeStruct((Tn, 3 * D), BF16),
                   jax.ShapeDtypeStruct((1, DV), F32)),
        grid=(Tn // tm,), in_specs=[row, row, row] + gates + [vec],
        out_specs=(row, row, pl.BlockSpec((tm, 3 * D), lambda i: (i, 0)), vec), compiler_params=_cp("arbitrary"),
    )(dm, a, go, *([proj] * (3 * GH)), gnw)


def _ffn_up(v2, wgt, wut, name="ffn_up", tm=1024, tn=512):
    Tn = v2.shape[0]
    tm = min(tm, Tn)

    def body(v_ref, wg_ref, wu_ref, a_ref, b_ref, ff_ref):
        vv = v_ref[...]
        a = _dot(vv, wg_ref[...], tb=True)
        b = _dot(vv, wu_ref[...], tb=True)
        a_ref[...] = a.astype(BF16)
        b_ref[...] = b.astype(BF16)
        ff_ref[...] = (a * _sigmoid(a) * b).astype(BF16)

    w = pl.BlockSpec((tn, D), lambda j, i: (j, 0))
    act = pl.BlockSpec((tm, tn), lambda j, i: (i, j))
    return pl.pallas_call(
        body, name=name,
        out_shape=(jax.ShapeDtypeStruct((Tn, FH), BF16), jax.ShapeDtypeStruct((Tn, FH), BF16), jax.ShapeDtypeStruct((Tn, FH), BF16)),
        grid=(FH // tn, Tn // tm), in_specs=[pl.BlockSpec((tm, D), lambda j, i: (i, 0)), w, w], out_specs=(act, act, act),
        compiler_params=_cp("parallel", "parallel"),
    )(v2, wgt, wut)


def _ffn_dact(dh2b, wd, a, b, name="ffn_dact", tm=1024, tn=512):
    Tn = dh2b.shape[0]
    tm = min(tm, Tn)

    def body(d_ref, w_ref, a_ref, b_ref, da_ref, db_ref):
        dff = _dot(d_ref[...], w_ref[...], tb=True)
        av = a_ref[...].astype(F32)
        sg = _sigmoid(av)
        da_ref[...] = (dff * b_ref[...].astype(F32) * (sg * (1.0 + av * (1.0 - sg)))).astype(BF16)
        db_ref[...] = (dff * (av * sg)).astype(BF16)

    act = pl.BlockSpec((tm, tn), lambda j, i: (i, j))
    return pl.pallas_call(
        body, name=name,
        out_shape=(jax.ShapeDtypeStruct((Tn, FH), BF16), jax.ShapeDtypeStruct((Tn, FH), BF16)),
        grid=(FH // tn, Tn // tm),
        in_specs=[pl.BlockSpec((tm, D), lambda j, i: (i, 0)), pl.BlockSpec((tn, D), lambda j, i: (j, 0)), act, act],
        out_specs=(act, act), compiler_params=_cp("parallel", "parallel"),
    )(dh2b, wd, a, b)


def _adam_math(w, g, m, v):
    m2 = B1 * m + (1.0 - B1) * g
    v2 = B2 * v + (1.0 - B2) * (g * g)
    mh = m2 / (1.0 - B1 ** STEP)
    vh = v2 / (1.0 - B2 ** STEP)
    return -LR * (mh / (jnp.sqrt(vh) + AEPS) + WD * w), m2, v2


def _sum_blocks(o_ref, p_ref):
    g = o_ref[...].astype(F32)
    for j in range(p_ref.shape[0]):
        g = g + p_ref[j].astype(F32)
    return g


def _adamw(w, m, v, psums, parts, chip_idx, name, tr):
    R, C = w.shape

    def body(s_ref, w_ref, m_ref, v_ref, o_ref, p_ref, g_ref, d_ref, m2_ref, v2_ref):
        g = _sum_blocks(o_ref, p_ref)
        d, m2, v2 = _adam_math(w_ref[...], g, m_ref[...], v_ref[...])
        g_ref[...] = g
        d_ref[...] = d
        m2_ref[...] = m2
        v2_ref[...] = v2

    blk = pl.BlockSpec((tr, C), lambda i, s: (i, 0))
    out = jax.ShapeDtypeStruct((R, C), F32)
    grid_spec = pltpu.PrefetchScalarGridSpec(
        num_scalar_prefetch=1, grid=(R // tr,),
        in_specs=[blk, blk, blk, pl.BlockSpec((None, tr, C), lambda i, s: (s[0], i, 0)),
                  pl.BlockSpec((parts.shape[0], tr, C), lambda i, s: (0, i, 0))],
        out_specs=(blk, blk, blk, blk),
    )
    return pl.pallas_call(body, name=name, out_shape=(out, out, out, out), grid_spec=grid_spec, compiler_params=_cp("parallel"),
                          )(chip_idx, w, m, v, psums, parts)


def _adamw_rows(w, m, v, g, name, tr):
    R = w.shape[0]

    def body(w_ref, m_ref, v_ref, g_ref, d_ref, m2_ref, v2_ref):
        d, m2, v2 = _adam_math(w_ref[...], g_ref[...], m_ref[...], v_ref[...])
        d_ref[...] = d
        m2_ref[...] = m2
        v2_ref[...] = v2

    blk = pl.BlockSpec((tr,) + w.shape[1:], lambda i: (i, 0, 0))
    out = jax.ShapeDtypeStruct(w.shape, F32)
    return pl.pallas_call(body, name=name, out_shape=(out, out, out), grid=(R // tr,), in_specs=[blk] * 4, out_specs=(blk, blk, blk),
                          compiler_params=_cp("parallel"))(w, m, v, g)


def _sum_parts(psums, parts, chip_idx, name, tr, tc):
    _, R, C = psums.shape

    def body(s_ref, o_ref, p_ref, g_ref):
        g_ref[...] = _sum_blocks(o_ref, p_ref)

    grid_spec = pltpu.PrefetchScalarGridSpec(
        num_scalar_prefetch=1, grid=(R // tr, C // tc),
        in_specs=[pl.BlockSpec((None, tr, tc), lambda i, j, s: (s[0], i, j)),
                  pl.BlockSpec((parts.shape[0], tr, tc), lambda i, j, s: (0, i, j))],
        out_specs=pl.BlockSpec((tr, tc), lambda i, j, s: (i, j)),
    )
    return pl.pallas_call(body, name=name, out_shape=jax.ShapeDtypeStruct((R, C), F32), grid_spec=grid_spec,
                          compiler_params=_cp("parallel", "parallel"))(chip_idx, psums, parts)


def _adamw_plain(w, m, v, g, name):
    def body(w_ref, m_ref, v_ref, g_ref, d_ref, m2_ref, v2_ref):
        d, m2, v2 = _adam_math(w_ref[...], g_ref[...], m_ref[...], v_ref[...])
        d_ref[...] = d
        m2_ref[...] = m2
        v2_ref[...] = v2

    out = jax.ShapeDtypeStruct(w.shape, F32)
    return pl.pallas_call(body, name=name, out_shape=(out, out, out))(w, m, v, g)


def _sum_devices(pack_all, name="sum_small"):
    def body(p_ref, o_ref):
        s = p_ref[0]
        for k in range(1, NDEV):
            s = s + p_ref[k]
        o_ref[...] = s

    return pl.pallas_call(body, name=name, out_shape=jax.ShapeDtypeStruct(pack_all.shape[1:], F32))(pack_all)


def _pair_add(g5, recv, c_idx, name, tr):
    _, _, R, C = g5.shape

    def body(c_ref, g_ref, r_ref, o_ref):
        o_ref[...] = (g_ref[...].astype(F32) + r_ref[...].astype(F32)).astype(BF16)

    grid_spec = pltpu.PrefetchScalarGridSpec(
        num_scalar_prefetch=1, grid=(4, R // tr),
        in_specs=[pl.BlockSpec((None, None, tr, C), lambda q, i, c: (q, c[0], i, 0)), pl.BlockSpec((None, tr, C), lambda q, i, c: (q, i, 0))],
        out_specs=pl.BlockSpec((None, tr, C), lambda q, i, c: (q, i, 0)),
    )
    return pl.pallas_call(
        body, name=name, out_shape=jax.ShapeDtypeStruct((4, R, C), BF16), grid_spec=grid_spec,
        compiler_params=_cp("parallel", "parallel"),
    )(c_idx, g5, recv)


_ANY = pl.BlockSpec(memory_space=pl.ANY)


def _mesh_pos():
    x, y, c = lax.axis_index("x"), lax.axis_index("y"), lax.axis_index("c")
    return x, y, c, [(1 - x, y), (x, 1 - y), (1 - x, 1 - y)]


def _pair_exchange(grads, name):
    n = len(grads)

    def body(*refs):
        ins, outs = refs[:n], refs[n:2 * n]
        send, recv = refs[2 * n:]
        x, y, c, _ = _mesh_pos()
        big = [pltpu.make_async_remote_copy(src_ref=ins[a].at[:, 1 - c], dst_ref=outs[a], send_sem=send.at[a], recv_sem=recv.at[a],
                                            device_id=(x, y, 1 - c), device_id_type=MESH) for a in range(n)]
        for d in big:
            d.start()
        for d in big:
            d.wait_recv()
        for d in big:
            d.wait_send()

    return pl.pallas_call(
        body, name=name, out_shape=tuple(jax.ShapeDtypeStruct((4,) + g.shape[2:], g.dtype) for g in grads),
        in_specs=[_ANY] * n, out_specs=tuple([_ANY] * n),
        scratch_shapes=[pltpu.SemaphoreType.DMA((n,)), pltpu.SemaphoreType.DMA((n,))],
    )(*grads)


def _gather_small(pack, name="gather_small"):
    def body(pk, pk_all, psend, precv, loc):
        x, y, c, chips = _mesh_pos()
        me_slot = 4 * x + 2 * y + c
        sib = (x, y, 1 - c)
        own = pltpu.make_async_copy(pk, pk_all.at[me_slot], loc)
        own.start()
        peers = [sib] + [(*chip, c) for chip in chips] + [(*chip, 1 - c) for chip in chips]
        small = [pltpu.make_async_remote_copy(src_ref=pk, dst_ref=pk_all.at[me_slot], send_sem=psend.at[k], recv_sem=precv.at[k],
                                              device_id=p, device_id_type=MESH) for k, p in enumerate(peers)]
        for d in small:
            d.start()
        for k, p in enumerate(peers):
            pltpu.make_async_remote_copy(src_ref=pk, dst_ref=pk_all.at[4 * p[0] + 2 * p[1] + p[2]], send_sem=psend.at[k],
                                         recv_sem=precv.at[k], device_id=p, device_id_type=MESH).wait_recv()
        for d in small:
            d.wait_send()
        own.wait()

    return pl.pallas_call(
        body, name=name, out_shape=jax.ShapeDtypeStruct((NDEV,) + pack.shape, pack.dtype), in_specs=[_ANY], out_specs=_ANY,
        scratch_shapes=[pltpu.SemaphoreType.DMA((7,)), pltpu.SemaphoreType.DMA((7,)), pltpu.SemaphoreType.DMA(())],
    )(pack)


def _main_row(g):
    return g if g < C_LR else g - RANK


def _window_pieces(lo, hi):
    out = []
    for a, b, where in ((lo, min(hi, C_LR), "main"), (max(lo, C_LR), min(hi, C_LR + RANK), "lr"), (max(lo, C_LR + RANK), hi, "main")):
        if a < b:
            out.append((a, b, where, _main_row(a) if where == "main" else a - C_LR))
    return out


def _assemble_w_in(windows, own, name="assemble_w_in"):
    edges = NDEV - 1

    def body(b_ref, own_ref, main_ref, lr_ref, buf, ebuf, in_sems, out_sems, esems):
        dev = 4 * lax.axis_index("x") + 2 * lax.axis_index("y") + lax.axis_index("c")

        def load(k):
            return pltpu.make_async_copy(b_ref.at[k], buf.at[k % 2], in_sems.at[k % 2])

        def start_load(k):
            pl.when(dev == k)(pltpu.make_async_copy(own_ref, buf.at[k % 2], in_sems.at[k % 2]).start)
            pl.when(dev != k)(load(k).start)

        lr_ref[RANK:, :] = jnp.zeros((LANE - RANK, D), BF16)
        start_load(0)
        pending, edge_out = [], []
        for k in range(NDEV):
            s = k % 2
            load(k).wait()
            if k:
                ebuf[k - 1] = buf[1 - s, WSTEP:WWIN, :] + buf[s, 0:16, :]
                edge_out.append(pltpu.make_async_copy(ebuf.at[k - 1], main_ref.at[pl.ds(_main_row(WSTEP * k), 16)], esems.at[k - 1]))
                edge_out[-1].start()
                for d in pending:
                    d.wait()
            if k + 1 < NDEV:
                start_load(k + 1)
            pending = []
            lo = WSTEP * k + (16 if k else 0)
            hi = WSTEP * k + (WWIN if k == NDEV - 1 else WSTEP)
            for a, b, where, dst in _window_pieces(lo, hi):
                if where == "lr":
                    lr_ref[dst:dst + b - a, :] = buf[s, a - WSTEP * k:b - WSTEP * k, :]
                else:
                    pending.append(pltpu.make_async_copy(buf.at[s, pl.ds(a - WSTEP * k, b - a)], main_ref.at[pl.ds(dst, b - a)],
                                                         out_sems.at[2 * s + len(pending)]))
                    pending[-1].start()
        for d in pending + edge_out:
            d.wait()

    return pl.pallas_call(
        body, name=name,
        out_shape=(jax.ShapeDtypeStruct((NMAIN, D), BF16), jax.ShapeDtypeStruct((LANE, D), BF16)),
        in_specs=[_ANY, _ANY], out_specs=(_ANY, pl.BlockSpec(memory_space=pltpu.VMEM)),
        scratch_shapes=[pltpu.VMEM((2, WWIN, D), BF16), pltpu.VMEM((edges, 16, D), BF16), pltpu.SemaphoreType.DMA((2,)),
                        pltpu.SemaphoreType.DMA((4,)), pltpu.SemaphoreType.DMA((edges,))],
        compiler_params=pltpu.CompilerParams(vmem_limit_bytes=VMEM_LIMIT),
    )(windows, own)


def _disassemble_w_in(d_main, d_lr, name="disassemble_w_in"):
    def body(main_ref, lr_ref, g_ref, buf, in_sems, out_sems):
        def loads(k):
            s, out = k % 2, []
            for a, b, where, src0 in _window_pieces(WSTEP * k, WSTEP * k + WWIN):
                if where == "main":
                    out.append(pltpu.make_async_copy(main_ref.at[pl.ds(src0, b - a)], buf.at[s, pl.ds(a - WSTEP * k, b - a)],
                                                     in_sems.at[2 * s + len(out)]))
            return out

        def store(k):
            return pltpu.make_async_copy(buf.at[k % 2], g_ref.at[k], out_sems.at[k % 2])

        for d in loads(0):
            d.start()
        for k in range(NDEV):
            for d in loads(k):
                d.wait()
            for a, b, where, src0 in _window_pieces(WSTEP * k, WSTEP * k + WWIN):
                if where == "lr":
                    buf[k % 2, a - WSTEP * k:b - WSTEP * k, :] = lr_ref[src0:src0 + b - a, :]
            if k:
                store(k - 1).wait()
            if k + 1 < NDEV:
                for d in loads(k + 1):
                    d.start()
            store(k).start()
        store(NDEV - 1).wait()

    return pl.pallas_call(
        body, name=name, out_shape=jax.ShapeDtypeStruct((NDEV, WWIN, D), BF16),
        in_specs=[_ANY, pl.BlockSpec(memory_space=pltpu.VMEM)], out_specs=_ANY,
        scratch_shapes=[pltpu.VMEM((2, WWIN, D), BF16), pltpu.SemaphoreType.DMA((4,)), pltpu.SemaphoreType.DMA((2,))],
        compiler_params=pltpu.CompilerParams(vmem_limit_bytes=VMEM_LIMIT),
    )(d_main, d_lr)


_HBM = pl.BlockSpec(memory_space=pltpu.HBM)
_SEM = pl.BlockSpec(memory_space=pltpu.SEMAPHORE)
_VMEM = pl.BlockSpec(memory_space=pltpu.VMEM)
_SIDE = pltpu.CompilerParams(has_side_effects=pltpu.SideEffectType.DATAFLOW_SIDE_EFFECTING)
_TOKEN = jax.ShapeDtypeStruct((8, LANE), F32)


def _hbm(a):
    return pltpu.with_memory_space_constraint(a, pltpu.HBM)


def _hbm_like(arrs):
    return tuple(pltpu.HBM(a.shape, a.dtype) for a in arrs)


def _tie(x, token):
    return x + token[0, 0].astype(x.dtype)


def _chip_copies(ins, lands, send, recv, nrel):
    x, y, c, chips = _mesh_pos()
    return [pltpu.make_async_remote_copy(src_ref=ins[a].at[2 * chip[0] + chip[1]], dst_ref=lands[a].at[j], send_sem=send.at[nrel * a + j],
                                         recv_sem=recv.at[nrel * a + j], device_id=(*chip, c), device_id_type=MESH)
            for a in range(len(ins)) for j, chip in enumerate(chips[:nrel])]


def _chip_start(psums, name, nrel=3):
    n = len(psums)
    lands = [lax.empty((nrel,) + p.shape[1:], p.dtype) for p in psums]

    def body(*refs):
        for d in _chip_copies(refs[:n], refs[n:2 * n], refs[2 * n], refs[2 * n + 1], nrel):
            d.start()
        refs[-1][...] = jnp.zeros_like(refs[-1])

    sems = pltpu.SemaphoreType.DMA((nrel * n,))
    out = pl.pallas_call(
        body, name=name, out_shape=(sems, sems) + _hbm_like(psums) + _hbm_like(lands) + (_TOKEN,),
        in_specs=[_HBM] * (2 * n), out_specs=(_SEM, _SEM) + (_HBM,) * (2 * n) + (_VMEM,),
        input_output_aliases={i: 2 + i for i in range(2 * n)}, compiler_params=_SIDE,
    )(*[_hbm(a) for a in list(psums) + lands])
    return out[0], out[1], list(out[2:2 + n]), list(out[2 + n:2 + 2 * n]), out[-1]


def _chip_wait(send, recv, psums, lands, after, name):
    n = len(psums)
    nrel = lands[0].shape[0]

    def body(*refs):
        for d in _chip_copies(refs[:n], refs[n:2 * n], refs[2 * n], refs[2 * n + 1], nrel):
            d.wait_send()
            d.wait_recv()

    out = pl.pallas_call(
        body, name=name, out_shape=_hbm_like(psums) + _hbm_like(lands),
        in_specs=[_HBM] * (2 * n) + [_SEM, _SEM, _ANY], out_specs=(_HBM,) * (2 * n),
        input_output_aliases={i: i for i in range(2 * n)}, compiler_params=_SIDE,
    )(*psums, *lands, send, recv, after)
    return list(out[:n]), list(out[n:])


def _hop_pos():
    x, y, c, _ = _mesh_pos()
    north = c == 1
    via = (jnp.where(north, 1 - x, x), jnp.where(north, y, 1 - y))
    return (*via, c), 2 * (1 - x) + (1 - y), jnp.where(north, 2 * x + (1 - y), 2 * (1 - x) + y)


def _hop_copies(ins, lands, send, recv):
    to, mine, _ = _hop_pos()
    return [pltpu.make_async_remote_copy(src_ref=ins[a].at[mine], dst_ref=lands[a], send_sem=send.at[a], recv_sem=recv.at[a],
                                         device_id=to, device_id_type=MESH) for a in range(len(ins))]


def _hop_start(psums, name):
    n = len(psums)
    lands = [lax.empty(p.shape[1:], p.dtype) for p in psums]

    def body(*refs):
        for d in _hop_copies(refs[:n], refs[n:2 * n], refs[2 * n], refs[2 * n + 1]):
            d.start()
        refs[-1][...] = jnp.zeros_like(refs[-1])

    sems = pltpu.SemaphoreType.DMA((n,))
    out = pl.pallas_call(
        body, name=name, out_shape=(sems, sems) + _hbm_like(psums) + _hbm_like(lands) + (_TOKEN,),
        in_specs=[_HBM] * (2 * n), out_specs=(_SEM, _SEM) + (_HBM,) * (2 * n) + (_VMEM,),
        input_output_aliases={i: 2 + i for i in range(2 * n)}, compiler_params=_SIDE,
    )(*[_hbm(a) for a in list(psums) + lands])
    return out[0], out[1], list(out[2:2 + n]), list(out[2 + n:2 + 2 * n]), out[-1]


def _hop_wait(send, recv, psums, lands, after, name):
    n = len(psums)

    def body(*refs):
        for d in _hop_copies(refs[:n], refs[n:2 * n], refs[2 * n], refs[2 * n + 1]):
            d.wait_send()
            d.wait_recv()

    out = pl.pallas_call(
        body, name=name, out_shape=_hbm_like(psums) + _hbm_like(lands),
        in_specs=[_HBM] * (2 * n) + [_SEM, _SEM, _ANY], out_specs=(_HBM,) * (2 * n),
        input_output_aliases={i: i for i in range(2 * n)}, compiler_params=_SIDE,
    )(*psums, *lands, send, recv, after)
    return list(out[:n]), list(out[n:])


def _hop_add(psums, land, idx, name, tr):
    _, R, C = psums.shape

    def body(s_ref, p_ref, l_ref, o_ref):
        o_ref[...] = (p_ref[...].astype(F32) + l_ref[...].astype(F32)).astype(BF16)

    blk = pl.BlockSpec((None, tr, C), lambda i, s: (s[0], i, 0))
    grid_spec = pltpu.PrefetchScalarGridSpec(num_scalar_prefetch=1, grid=(R // tr,),
                                             in_specs=[blk, pl.BlockSpec((tr, C), lambda i, s: (i, 0))], out_specs=blk)
    return pl.pallas_call(body, name=name, out_shape=jax.ShapeDtypeStruct(psums.shape, BF16), grid_spec=grid_spec,
                          input_output_aliases={1: 0}, compiler_params=_cp("parallel"))(idx, psums, land)


def _pair_copies(ins, lands, send, recv):
    x, y, c, _ = _mesh_pos()
    return [pltpu.make_async_remote_copy(src_ref=ins[a].at[:, 1 - c], dst_ref=lands[a], send_sem=send.at[a], recv_sem=recv.at[a],
                                         device_id=(x, y, 1 - c), device_id_type=MESH) for a in range(len(ins))]


def _pair_start(grads, name):
    n = len(grads)
    lands = [lax.empty((4,) + g.shape[2:], g.dtype) for g in grads]

    def body(*refs):
        for d in _pair_copies(refs[:n], refs[n:2 * n], refs[2 * n], refs[2 * n + 1]):
            d.start()
        refs[-1][...] = jnp.zeros_like(refs[-1])

    sems = pltpu.SemaphoreType.DMA((n,))
    out = pl.pallas_call(
        body, name=name, out_shape=(sems, sems) + _hbm_like(grads) + _hbm_like(lands) + (_TOKEN,),
        in_specs=[_HBM] * (2 * n), out_specs=(_SEM, _SEM) + (_HBM,) * (2 * n) + (_VMEM,),
        input_output_aliases={i: 2 + i for i in range(2 * n)}, compiler_params=_SIDE,
    )(*[_hbm(a) for a in list(grads) + lands])
    return out[0], out[1], list(out[2:2 + n]), list(out[2 + n:2 + 2 * n]), out[-1]


def _pair_wait(send, recv, grads, lands, after, name):
    n = len(grads)

    def body(*refs):
        for d in _pair_copies(refs[:n], refs[n:2 * n], refs[2 * n], refs[2 * n + 1]):
            d.wait_send()
            d.wait_recv()

    out = pl.pallas_call(
        body, name=name, out_shape=_hbm_like(grads) + _hbm_like(lands),
        in_specs=[_HBM] * (2 * n) + [_SEM, _SEM, _ANY], out_specs=(_HBM,) * (2 * n),
        input_output_aliases={i: i for i in range(2 * n)}, compiler_params=_SIDE,
    )(*grads, *lands, send, recv, after)
    return list(out[:n]), list(out[n:])


def _slot(chip, c):
    return 4 * chip[0] + 2 * chip[1] + c


def _gather_start(shards, lands, after, name):
    n = len(shards)

    def body(*refs):
        src, land, send, recv = refs[:n], refs[n:2 * n], refs[2 * n + 1], refs[2 * n + 2]
        x, y, c, chips = _mesh_pos()
        for a in range(n):
            for k, to in enumerate([(x, y, 1 - c)] + [(*chip, c) for chip in chips]):
                pltpu.make_async_remote_copy(src_ref=src[a], dst_ref=land[a].at[_slot((x, y), c)], send_sem=send.at[4 * a + k],
                                             recv_sem=recv.at[4 * a + k], device_id=to, device_id_type=MESH).start()
        refs[-1][...] = jnp.zeros_like(refs[-1])

    sems = pltpu.SemaphoreType.DMA((4 * n,))
    out = pl.pallas_call(
        body, name=name, out_shape=(sems, sems) + _hbm_like(shards) + _hbm_like(lands) + (_TOKEN,),
        in_specs=[_HBM] * (2 * n) + [_ANY], out_specs=(_SEM, _SEM) + (_HBM,) * (2 * n) + (_VMEM,),
        input_output_aliases={i: 2 + i for i in range(2 * n)}, compiler_params=_SIDE,
    )(*[_hbm(a) for a in list(shards) + list(lands)], after)
    return out[0], out[1], list(out[2:2 + n]), list(out[2 + n:2 + 2 * n]), out[-1]


def _gather_pass(lands, recv, after, name, first=0):
    n = len(lands)

    def body(*refs):
        land, recv1 = refs[:n], refs[n]
        send2, recv2 = refs[n + 2], refs[n + 3]
        x, y, c, chips = _mesh_pos()
        for a in range(n):
            for j, chip in enumerate(chips):
                blk = land[a].at[_slot(chip, c)]
                pltpu.make_async_remote_copy(src_ref=blk, dst_ref=blk, send_sem=send2.at[3 * a + j], recv_sem=recv1.at[4 * (first + a) + 1 + j],
                                             device_id=(*chip, c), device_id_type=MESH).wait_recv()
                pltpu.make_async_remote_copy(src_ref=blk, dst_ref=blk, send_sem=send2.at[3 * a + j], recv_sem=recv2.at[3 * a + j],
                                             device_id=(x, y, 1 - c), device_id_type=MESH).start()
        refs[-1][...] = jnp.zeros_like(refs[-1])

    sems = pltpu.SemaphoreType.DMA((3 * n,))
    out = pl.pallas_call(
        body, name=name, out_shape=(sems, sems) + _hbm_like(lands) + (_TOKEN,),
        in_specs=[_HBM] * n + [_SEM, _ANY], out_specs=(_SEM, _SEM) + (_HBM,) * n + (_VMEM,),
        input_output_aliases={i: 2 + i for i in range(n)}, compiler_params=_SIDE,
    )(*lands, recv, after)
    return out[0], out[1], list(out[2:2 + n]), out[-1]


def _gather_wait(shards, lands, send, recv, send2, recv2, after, name, first=0):
    n = len(lands)

    def body(*refs):
        src, land = refs[:n], refs[n:2 * n]
        send1, recv1, snd2, rcv2 = refs[2 * n:2 * n + 4]
        x, y, c, chips = _mesh_pos()
        sib = (x, y, 1 - c)
        for a in range(n):
            for k in range(4):
                pltpu.make_async_remote_copy(src_ref=src[a], dst_ref=land[a].at[_slot((x, y), c)], send_sem=send1.at[4 * (first + a) + k],
                                             recv_sem=recv1.at[4 * (first + a) + k], device_id=sib, device_id_type=MESH).wait_send()
            blk = land[a].at[_slot((x, y), 1 - c)]
            pltpu.make_async_remote_copy(src_ref=blk, dst_ref=blk, send_sem=send1.at[4 * (first + a)], recv_sem=recv1.at[4 * (first + a)],
                                         device_id=sib, device_id_type=MESH).wait_recv()
            for j, chip in enumerate(chips):
                mine, theirs = land[a].at[_slot(chip, c)], land[a].at[_slot(chip, 1 - c)]
                pltpu.make_async_remote_copy(src_ref=mine, dst_ref=mine, send_sem=snd2.at[3 * a + j], recv_sem=rcv2.at[3 * a + j],
                                             device_id=sib, device_id_type=MESH).wait_send()
                pltpu.make_async_remote_copy(src_ref=theirs, dst_ref=theirs, send_sem=snd2.at[3 * a + j], recv_sem=rcv2.at[3 * a + j],
                                             device_id=sib, device_id_type=MESH).wait_recv()

    out = pl.pallas_call(
        body, name=name, out_shape=_hbm_like(shards) + _hbm_like(lands),
        in_specs=[_HBM] * (2 * n) + [_SEM] * 4 + [_ANY], out_specs=(_HBM,) * (2 * n),
        input_output_aliases={i: i for i in range(2 * n)}, compiler_params=_SIDE,
    )(*shards, *lands, send, recv, send2, recv2, after)
    return list(out[n:])


def _win_tree():
    x, y, c, chips = _mesh_pos()
    north = c == 1
    handed = (jnp.where(north, 1 - x, x), jnp.where(north, y, 1 - y))
    hand_to = (jnp.where(north, x, 1 - x), jnp.where(north, 1 - y, y))
    return x, y, c, chips, handed, hand_to


def _blk(land, chip, c):
    return land.at[_slot(chip, c)]


def _rcopy(src, dst, send, recv, to):
    return pltpu.make_async_remote_copy(src_ref=src, dst_ref=dst, send_sem=send, recv_sem=recv, device_id=to, device_id_type=MESH)


def _win_start(shards, lands, name):
    n = len(shards)

    def body(*refs):
        src, land, send, recv = refs[:n], refs[n:2 * n], refs[2 * n], refs[2 * n + 1]
        x, y, c, chips, _, _ = _win_tree()
        for a in range(n):
            for k, to in enumerate([(x, y, 1 - c), (*chips[0], c), (*chips[1], c)]):
                _rcopy(src[a], _blk(land[a], (x, y), c), send.at[3 * a + k], recv.at[3 * a + k], to).start()
        refs[-1][...] = jnp.zeros_like(refs[-1])

    sems = pltpu.SemaphoreType.DMA((3 * n,))
    out = pl.pallas_call(
        body, name=name, out_shape=(sems, sems) + _hbm_like(shards) + _hbm_like(lands) + (_TOKEN,),
        in_specs=[_HBM] * (2 * n), out_specs=(_SEM, _SEM) + (_HBM,) * (2 * n) + (_VMEM,),
        input_output_aliases={i: 2 + i for i in range(2 * n)}, compiler_params=_SIDE,
    )(*[_hbm(a) for a in list(shards) + list(lands)])
    return out[0], out[1], list(out[2:2 + n]), list(out[2 + n:2 + 2 * n]), out[-1]


def _win_hand_on(lands, recv1, after, name):
    n, m = len(lands), len(after)

    def body(*refs):
        land, rcv1 = refs[:n], refs[n]
        send2, recv2 = refs[n + 1 + m], refs[n + 2 + m]
        x, y, c, chips, handed, hand_to = _win_tree()
        for a in range(n):
            for j in range(2):
                blk = _blk(land[a], chips[j], c)
                _rcopy(blk, blk, send2.at[3 * a], rcv1.at[3 * a + 1 + j], (*chips[j], c)).wait_recv()
            blk = _blk(land[a], handed, c)
            _rcopy(blk, blk, send2.at[3 * a], recv2.at[3 * a], (*hand_to, c)).start()
            for j in range(2):
                blk = _blk(land[a], chips[j], c)
                _rcopy(blk, blk, send2.at[3 * a + 1 + j], recv2.at[3 * a + 1 + j], (x, y, 1 - c)).start()
        refs[-1][...] = jnp.zeros_like(refs[-1])

    sems = pltpu.SemaphoreType.DMA((3 * n,))
    out = pl.pallas_call(
        body, name=name, out_shape=(sems, sems) + _hbm_like(lands) + (_TOKEN,),
        in_specs=[_HBM] * n + [_SEM] + [_ANY] * m, out_specs=(_SEM, _SEM) + (_HBM,) * n + (_VMEM,),
        input_output_aliases={i: 2 + i for i in range(n)}, compiler_params=_SIDE,
    )(*lands, recv1, *after)
    return out[0], out[1], list(out[2:2 + n]), out[-1]


def _win_last(lands, recv2, after, name):
    n, m = len(lands), len(after)

    def body(*refs):
        land, rcv2 = refs[:n], refs[n]
        send3, recv3 = refs[n + 1 + m], refs[n + 2 + m]
        x, y, c, chips, _, hand_to = _win_tree()
        for a in range(n):
            blk = _blk(land[a], chips[2], c)
            _rcopy(blk, blk, send3.at[a], rcv2.at[3 * a], (*hand_to, c)).wait_recv()
            _rcopy(blk, blk, send3.at[a], recv3.at[a], (x, y, 1 - c)).start()
        refs[-1][...] = jnp.zeros_like(refs[-1])

    sems = pltpu.SemaphoreType.DMA((n,))
    out = pl.pallas_call(
        body, name=name, out_shape=(sems, sems) + _hbm_like(lands) + (_TOKEN,),
        in_specs=[_HBM] * n + [_SEM] + [_ANY] * m, out_specs=(_SEM, _SEM) + (_HBM,) * n + (_VMEM,),
        input_output_aliases={i: 2 + i for i in range(n)}, compiler_params=_SIDE,
    )(*lands, recv2, *after)
    return out[0], out[1], list(out[2:2 + n]), out[-1]


def _win_wait(shards, lands, sems1, sems2, sems3, after, name):
    n = len(lands)

    def body(*refs):
        src, land = refs[:n], refs[n:2 * n]
        send1, recv1, send2, recv2, send3, recv3 = refs[2 * n:2 * n + 6]
        x, y, c, chips, handed, hand_to = _win_tree()
        sib = (x, y, 1 - c)
        for a in range(n):
            own = _blk(land[a], (x, y), c)
            for k in range(3):
                _rcopy(src[a], own, send1.at[3 * a + k], recv1.at[3 * a + k], sib).wait_send()
            blk = _blk(land[a], (x, y), 1 - c)
            _rcopy(blk, blk, send1.at[3 * a], recv1.at[3 * a], sib).wait_recv()
            blk = _blk(land[a], handed, c)
            _rcopy(blk, blk, send2.at[3 * a], recv2.at[3 * a], sib).wait_send()
            for j in range(2):
                mine, theirs = _blk(land[a], chips[j], c), _blk(land[a], chips[j], 1 - c)
                _rcopy(mine, mine, send2.at[3 * a + 1 + j], recv2.at[3 * a + 1 + j], sib).wait_send()
                _rcopy(theirs, theirs, send2.at[3 * a + 1 + j], recv2.at[3 * a + 1 + j], sib).wait_recv()
            mine, theirs = _blk(land[a], chips[2], c), _blk(land[a], chips[2], 1 - c)
            _rcopy(mine, mine, send3.at[a], recv3.at[a], sib).wait_send()
            _rcopy(theirs, theirs, send3.at[a], recv3.at[a], sib).wait_recv()

    out = pl.pallas_call(
        body, name=name, out_shape=_hbm_like(shards) + _hbm_like(lands),
        in_specs=[_HBM] * (2 * n) + [_SEM] * 6 + [_ANY], out_specs=(_HBM,) * (2 * n),
        input_output_aliases={i: i for i in range(2 * n)}, compiler_params=_SIDE,
    )(*shards, *lands, *sems1, *sems2, *sems3, after)
    return list(out[n:])


def _pad_to(v, n):
    return jnp.pad(v, [(0, 0)] * (v.ndim - 1) + [(0, n - v.shape[-1])])


def _pack_small(n1, gb, sk, gn, n2, fn, extra=None):
    parts = [n1.reshape(-1), gb.reshape(-1), sk.reshape(-1), gn.reshape(-1), n2.reshape(-1), fn.reshape(-1)]
    flat = jnp.concatenate(parts + ([extra.reshape(-1)] if extra is not None else []))
    return _pad_to(flat, SMALL_N).reshape(SMALL_ROWS, LANE)


def _unpack_small(p):
    f = p.reshape(-1)
    return (f[S_N1:S_GB].reshape(1, D), f[S_GB:S_SK].reshape(1, GH * DK), f[S_SK:S_GN].reshape(1, NQ), f[S_GN:S_N2].reshape(1, DV),
            f[S_N2:S_FN].reshape(1, D), f[S_FN:S_LOSS].reshape(D))


class _NoComm:
    def __init__(self, wo, wg_all, wu_all, wd_all):
        self.rest = (wo, wg_all, wu_all, wd_all)

    def mixed(self, gla_o, gla_norm_w):
        return gla_norm_w

    def w_out(self, merged, norm2_w):
        return self.rest[0], norm2_w

    def w_up(self, v2):
        return self.rest[1], self.rest[2]

    def w_down(self, ff):
        return self.rest[3]

    def ffn_grads(self, d_wg, d_wu, d_wd):
        self.ffn = (d_wg, d_wu, d_wd)

    def ffn_reduce(self, dv2, norm2_w):
        return norm2_w

    def in_grads(self, d_wmain, d_wlr, d_wo, w_lr):
        self.inw = (d_wmain, d_wlr, d_wo)
        return w_lr

    def in_reduce(self, du, norm1_w):
        return norm1_w


class _Comm:
    def __init__(self, rest_shards, rest_lands, after, c_idx):
        self.c_idx = c_idx
        self.send, self.recv, self.shards, self.lands, self.token = _gather_start(rest_shards, rest_lands, after, "gather_rest_start")

    def _pass(self, lo, hi, after, tag):
        send2, recv2, lands, token = _gather_pass(self.lands[lo:hi], self.recv, after, "gather_pass_" + tag, first=lo)
        self.passed = (lo, hi, send2, recv2, lands)
        return token

    def _wait(self, after, tag):
        lo, hi, send2, recv2, lands = self.passed
        return _gather_wait(self.shards[lo:hi], lands, self.send, self.recv, send2, recv2, after, "gather_wait_" + tag, first=lo)

    def mixed(self, gla_o, gla_norm_w):
        return _tie(gla_norm_w, self._pass(0, 1, gla_o, "out"))

    def w_out(self, merged, norm2_w):
        (wo_all,) = self._wait(merged, "out")
        return wo_all.reshape(D, D), _tie(norm2_w, self._pass(1, 3, merged, "up"))

    def w_up(self, v2):
        wg_all, wu_all = self._wait(v2, "up")
        self._pass(3, 4, v2, "down")
        return wg_all.reshape(FH, D), wu_all.reshape(FH, D)

    def w_down(self, ff):
        return self._wait(ff, "down")[0].reshape(FH, D)

    def _reduce(self, tag, names, grads, recv1, rows):
        psums = [_pair_add(g, r, self.c_idx, "pair_add_" + nm, tr) for g, r, nm, tr in zip(grads, recv1, names, rows)]
        *flight, token = _chip_start(psums, "reduce_chips_start_" + tag)
        return dict(tag=tag, names=names, rows=rows, flight=flight), token

    def ffn_grads(self, d_wg, d_wu, d_wd):
        self.ffn_pair = _pair_start([d.reshape(4, 2, FS, D) for d in (d_wg, d_wu, d_wd)], "reduce_pair_start_ffn")
        return self.ffn_pair[-1]

    def ffn_reduce(self, dv2, norm2_w):
        send, recv, grads, lands, _ = self.ffn_pair
        grads, recv1 = _pair_wait(send, recv, grads, lands, dv2, "reduce_pair_wait_ffn")
        self.ffn, token = self._reduce("ffn", ["w_ffn_gate", "w_ffn_up", "w_ffn_down"], grads, recv1, [176, 176, 176])
        return _tie(norm2_w, token)

    def in_grads(self, d_wmain, d_wlr, d_wo, w_lr):
        grads = [_disassemble_w_in(d_wmain, d_wlr).reshape(4, 2, WWIN, D), d_wo.reshape(4, 2, D // NDEV, D)]
        self.in_names, self.in_rows = ["w_in", "w_out"], [808, 256]
        psums = [_pair_add(g, r, self.c_idx, "pair_add_" + nm, tr)
                 for g, r, nm, tr in zip(grads, _pair_exchange(grads, "reduce_pair_in"), self.in_names, self.in_rows)]
        *self.in_hop, token = _hop_start(psums, "reduce_hop_start_in")
        return _tie(w_lr, token)

    def in_reduce(self, du, norm1_w):
        psums, lands = _hop_wait(*self.in_hop, du, "reduce_hop_wait_in")
        idx = _hop_pos()[2].astype(jnp.int32).reshape(1)
        psums = [_hop_add(p, l, idx, "hop_add_" + nm, tr) for p, l, nm, tr in zip(psums, lands, self.in_names, self.in_rows)]
        *flight, token = _chip_start(psums, "reduce_chips_start_in", nrel=2)
        self.inw = dict(tag="in", names=self.in_names, rows=self.in_rows, flight=flight)
        return _tie(norm1_w, token)


def _local_step(xs, tgt, u, norm1_w, gla_gate_b, attn_sinks, gla_norm_w, norm2_w, fnw, w_main, w_lr, w2p, comm):
    proj =_mm(u, w_main, tb=True, tm=1024, tn=1280, tk=D, name="in_proj")
    plr = _mm(u, w_lr, tb=True, tm=1024, tn=LANE, tk=D, name="in_proj_lr")
    attn_o = _attn_fwd(proj, attn_sinks)
    gla_o, states = _gla_fwd(proj, plr, w2p, gla_gate_b)
    merged = _merge_fwd(attn_o, gla_o, proj, comm.mixed(gla_o, gla_norm_w))
    wo, norm2_w = comm.w_out(merged, norm2_w)
    h1 = _mm(merged, wo, tm=1024, tn=512, tk=D, res=xs, name="out_proj")
    v2 = _rmsnorm_fwd(h1, norm2_w, "norm2_fwd")
    wg_all, wu_all = comm.w_up(v2)
    fa, fb, ff = _ffn_up(v2, wg_all, wu_all)
    wd_all = comm.w_down(ff)
    h2 = _mm(ff, wd_all, tm=1024, tn=1024, tk=FH // 2, res=h1, name="ffn_down")
    dh2, dh2b, d_fnw, loss_part = _loss_head(h2, fnw, tgt)

    da, db = _ffn_dact(dh2b, wd_all, fa, fb)
    Tn = xs.shape[0]
    d_wd = _mm(ff, dh2b, ta=True, tm=512, tn=D, tk=Tn, out_dtype=BF16, name="ffn_dwd")
    d_wg = _mm(da, v2, ta=True, tm=512, tn=D, tk=Tn, out_dtype=BF16, name="ffn_dwg")
    d_wu = _mm(db, v2, ta=True, tm=512, tn=D, tk=Tn, out_dtype=BF16, name="ffn_dwu")
    dv2 = _mm(da, wg_all, tm=1024, tn=1024, tk=FH // 2, after=comm.ffn_grads(d_wg, d_wu, d_wd), name="ffn_dv2_gate")
    dv2 = _mm(db, wu_all, tm=1024, tn=1024, tk=FH // 2, res=dv2, name="ffn_dv2_up")
    norm2_w = comm.ffn_reduce(dv2, norm2_w)
    dh1, dh1b, d_n2 = _rmsnorm_bwd(dv2, h1, norm2_w, dh2, "norm2_bwd")
    dmerged = _mm(dh1b, wo, tb=True, tm=1024, tn=512, tk=D, name="out_proj_dx")
    d_wo = _mm(merged, dh1b, ta=True, tm=1024, tn=512, tk=xs.shape[0], out_dtype=BF16, name="out_proj_dw")
    d_attn, d_gla, d_gates, d_gnw = _merge_bwd(dmerged, attn_o, gla_o, proj, gla_norm_w)
    d_q, d_kv, d_sinks = _attn_bwd(proj, attn_sinks, attn_o, d_attn)
    d_gqk, d_gv, d_plr, d_w2p, d_gb = _gla_bwd(proj, plr, w2p, gla_gate_b, states, d_gla)
    dproj = jnp.concatenate([d_q, d_kv, d_gqk, d_gv, d_gates], axis=1)
    d_wmain = _mm(dproj, u, ta=True, tm=640, tn=D, tk=xs.shape[0], out_dtype=BF16, name="in_proj_dw")
    d_wlr = _mm(d_plr, u, ta=True, tm=LANE, tn=1024, tk=xs.shape[0], out_dtype=BF16, name="in_proj_lr_dw")
    du_lr = _mm(d_plr, comm.in_grads(d_wmain, d_wlr, d_wo, w_lr), tm=1024, tn=1024, tk=LANE, name="in_proj_lr_dx")
    du = _mm(dproj, w_main, tm=1024, tn=1024, tk=2560, res=du_lr, name="in_proj_dx")
    dx, _, d_n1 = _rmsnorm_bwd(du, xs, comm.in_reduce(du, norm1_w), dh1, "norm1_bwd")
    return dx, loss_part, d_w2p, d_gb, d_sinks, d_gnw, d_n1, d_n2, d_fnw


def kernel(x, norm1_w, w_in, gla_gate_w2, gla_gate_b, attn_sinks, gla_norm_w, w_out, norm2_w, w_ffn_gate, w_ffn_up, w_ffn_down, final_norm_w, loss_target, m_norm1_w, m_w_in, m_gla_gate_w2, m_gla_gate_b, m_attn_sinks, m_gla_norm_w, m_w_out, m_norm2_w, m_w_ffn_gate, m_w_ffn_up, m_w_ffn_down, m_final_norm_w, v_norm1_w, v_w_in, v_gla_gate_w2, v_gla_gate_b, v_attn_sinks, v_gla_norm_w, v_w_out, v_norm2_w, v_w_ffn_gate, v_w_ffn_up, v_w_ffn_down, v_final_norm_w):
    xs, tgt = x[0], loss_target[0]
    fnw = final_norm_w.reshape(1, D)
    c_idx = lax.axis_index("c").astype(jnp.int32).reshape(1)
    dev = 4 * lax.axis_index("x") + 2 * lax.axis_index("y") + lax.axis_index("c")

    chip_idx = (2 * lax.axis_index("x") + lax.axis_index("y")).astype(jnp.int32).reshape(1)

    shift = (WS - WSTEP) * dev
    window = lax.dynamic_update_slice(jnp.zeros((WWIN, D), BF16), jnp.transpose(w_in[0]).astype(BF16), (shift, 0))
    w2_land = lax.dynamic_update_slice(lax.empty((NDEV, RANK, LANE), F32), gla_gate_w2, (dev, 0, 0))
    *sems1, win_srcs, win_lands, tok = _win_start([window, gla_gate_w2[0]], [lax.empty((NDEV, WWIN, D), BF16), w2_land], "gather_in_start")
    tr2 = lambda t: jnp.transpose(t[0])
    rows3 = lambda t: jnp.transpose(t[0] + tok[0, 0]).reshape(WS, D // LANE, LANE)
    rest = [(w + tok[0, 0]).astype(BF16) for w in (w_out[0], tr2(w_ffn_gate), tr2(w_ffn_up), w_ffn_down[0])]
    rest_lands = [lax.dynamic_update_slice(lax.empty((NDEV,) + s.shape, s.dtype), s[None], (dev, 0, 0)) for s in rest]
    win3 = [rows3(t) for t in (w_in, m_w_in, v_w_in)]
    *sems2, win_lands, tok = _win_hand_on(win_lands, sems1[1], rest + rest_lands + win3, "gather_in_hand_on")
    u = _rmsnorm_fwd(xs, _tie(norm1_w, tok), "norm1_fwd")
    *sems3, win_lands, tok = _win_last(win_lands, sems2[1], [u], "gather_in_last")
    comm = _Comm(rest, rest_lands, tok, c_idx)
    win_all, w2_all = _win_wait(win_srcs, win_lands, sems1, sems2, sems3, comm.token, "gather_in_wait")
    w_main, w_lr = _assemble_w_in(win_all, window)
    w2p = jnp.pad(jnp.transpose(w2_all, (1, 0, 2)).reshape(RANK, GH * DK), ((0, LANE - RANK), (0, 0)))

    dx, loss_part, d_w2p, d_gb, d_sinks, d_gnw, d_n1, d_n2, d_fnw = _local_step(
        xs, tgt, u, norm1_w, gla_gate_b, attn_sinks, gla_norm_w, norm2_w, fnw, w_main, w_lr, w2p, comm)

    pack = jnp.concatenate([_pack_small(d_n1, d_gb, d_sinks, d_gnw, d_n2, d_fnw, loss_part),
                            d_w2p[:RANK].reshape(GW2_ROWS, LANE)], axis=0)
    small = _sum_devices(_gather_small(pack))

    big = {}
    after = dx
    for grp in (comm.ffn, comm.inw):
        psums, parts = _chip_wait(*grp["flight"], after, "reduce_chips_wait_" + grp["tag"])
        for nm, ps, pt, tr in zip(grp["names"], psums, parts, grp["rows"]):
            w, m, v = {"w_in": (w_in, m_w_in, v_w_in), "w_out": (w_out, m_w_out, v_w_out), "w_ffn_gate": (w_ffn_gate, m_w_ffn_gate, v_w_ffn_gate),
                       "w_ffn_up": (w_ffn_up, m_w_ffn_up, v_w_ffn_up), "w_ffn_down": (w_ffn_down, m_w_ffn_down, v_w_ffn_down)}[nm]
            if nm == "w_in":
                g_win = _sum_parts(ps, pt, chip_idx, "sum_w_in", tr, 1024)
                g3 = lax.dynamic_slice(g_win, (shift, 0), (WS, D)).reshape(WS, D // LANE, LANE)
                out3 = (g3,) + tuple(_adamw_rows(*win3, g3, "adamw_w_in", 178))
                big[nm] = [jnp.transpose(t.reshape(WS, D))[None] for t in out3]
            elif nm in ("w_ffn_gate", "w_ffn_up"):
                big[nm] = [jnp.transpose(t)[None] for t in _adamw(tr2(w), tr2(m), tr2(v), ps, pt, chip_idx, "adamw_" + nm, tr)]
            else:
                big[nm] = [t[None] for t in _adamw(w[0], m[0], v[0], ps, pt, chip_idx, "adamw_" + nm, tr)]
            after = big[nm][0]
    g_small = small[:SMALL_ROWS]
    sm = _adamw_plain(_pack_small(norm1_w, gla_gate_b, attn_sinks, gla_norm_w, norm2_w, final_norm_w),
                      _pack_small(m_norm1_w, m_gla_gate_b, m_attn_sinks, m_gla_norm_w, m_norm2_w, m_final_norm_w),
                      _pack_small(v_norm1_w, v_gla_gate_b, v_attn_sinks, v_gla_norm_w, v_norm2_w, v_final_norm_w), g_small, "adamw_small")
    g_w2 = lax.dynamic_slice_in_dim(small[SMALL_ROWS:].reshape(RANK, GH * DK), dev * LANE, LANE, axis=1)
    w2 = [g_w2[None]] + [t[None] for t in _adamw_plain(gla_gate_w2[0], m_gla_gate_w2[0], v_gla_gate_w2[0], g_w2, "adamw_w2")]
    loss = g_small.reshape(-1)[S_LOSS]

    sg, sd, sm2, sv2 = [_unpack_small(t) for t in (g_small,) + tuple(sm)]

    def group(i, s):
        return (s[0], big["w_in"][i], w2[i], s[1], s[2], s[3], big["w_out"][i], s[4], big["w_ffn_gate"][i], big["w_ffn_up"][i],
                big["w_ffn_down"][i], s[5])

    return (loss, dx[None], *group(0, sg), *group(1, sd), *group(2, sm2), *group(3, sv2))
```

```python
import functools

import jax
import jax.numpy as jnp
from jax import lax
from jax.experimental import pallas as pl
from jax.experimental.pallas import tpu as pltpu

F32, BF16 = jnp.float32, jnp.bfloat16
HIGHEST = lax.Precision.HIGHEST

D = 2048
HD, NQ, NKV, GRP, WIN = 64, 32, 4, 8, 128
GH, DK, DV, RANK, GC = 4, 256, 512, 16, 64
FH, NDEV = 5632, 8
FS = FH // NDEV
DIN = 12816
WS = DIN // NDEV
EPS = 1e-6
MASKV = -1e30
LANE = 128

C_AQ, C_AK, C_AV, C_GQ, C_GK, C_GV, C_GR, C_GA, C_GB, NMAIN = 0, 2048, 2304, 2560, 3584, 4608, 6656, 8704, 10752, 12800
C_LR = 6656
WSTEP, WWIN = 1600, 1616

LR, B1, B2, AEPS, WD, STEP = 0.001, 0.9, 0.999, 1e-08, 0.01, 10

S_N1, S_GB, S_SK, S_GN, S_N2, S_FN, S_LOSS, SMALL_N = 0, 2048, 3072, 3104, 3616, 5664, 7712, 8192
SMALL_ROWS = SMALL_N // LANE
GW2_ROWS = RANK * GH * DK // LANE
PACK_ROWS = SMALL_ROWS + GW2_ROWS

MESH = pl.DeviceIdType.MESH


def _dot(a, b, ta=False, tb=False, prec=None):
    dn = (((0,) if ta else (1,), (1,) if tb else (0,)), ((), ()))
    return lax.dot_general(a, b, dn, preferred_element_type=F32, precision=prec)


def _sigmoid(x):
    return 1.0 / (1.0 + jnp.exp(-x))


VMEM_LIMIT = 56 * 1024 * 1024


def _cp(*sem):
    return pltpu.CompilerParams(dimension_semantics=sem, vmem_limit_bytes=VMEM_LIMIT)


def _mm(a, b, *, ta=False, tb=False, tm, tn, tk, out_dtype=F32, res=None, after=None, name):
    M, K = (a.shape[1], a.shape[0]) if ta else a.shape
    N = b.shape[0] if tb else b.shape[1]
    tm, tn, tk = min(tm, M), min(tn, N), min(tk, K)
    nk = K // tk
    assert M % tm == 0 and N % tn == 0 and K % tk == 0
    a_spec = pl.BlockSpec((tk, tm), lambda i, j, k: (k, i)) if ta else pl.BlockSpec((tm, tk), lambda i, j, k: (i, k))
    b_spec = pl.BlockSpec((tn, tk), lambda i, j, k: (j, k)) if tb else pl.BlockSpec((tk, tn), lambda i, j, k: (k, j))
    o_spec = pl.BlockSpec((tm, tn), lambda i, j, k: (i, j))
    has_res = res is not None

    def body(*refs):
        a_ref, b_ref = refs[0], refs[1]
        r_ref = refs[2] if has_res else None
        o_ref = refs[2 + has_res + (after is not None)]
        p = _dot(a_ref[...].astype(BF16), b_ref[...].astype(BF16), ta, tb)
        if nk == 1:
            if has_res:
                p = p + r_ref[...]
            o_ref[...] = p.astype(out_dtype)
        else:
            acc = refs[-1]
            k = pl.program_id(2)

            @pl.when(k == 0)
            def _():
                acc[...] = (p + r_ref[...]) if has_res else p

            @pl.when(k > 0)
            def _():
                acc[...] += p

            @pl.when(k == nk - 1)
            def _():
                o_ref[...] = acc[...].astype(out_dtype)

    return pl.pallas_call(
        body, name=name,
        out_shape=jax.ShapeDtypeStruct((M, N), out_dtype),
        grid=(M // tm, N // tn, nk),
        in_specs=[a_spec, b_spec] + ([o_spec] if has_res else []) + ([pl.BlockSpec(memory_space=pl.ANY)] if after is not None else []),
        out_specs=o_spec,
        scratch_shapes=[pltpu.VMEM((tm, tn), F32)] if nk > 1 else [],
        compiler_params=_cp("parallel", "parallel", "arbitrary"),
    )(*((a, b) + ((res,) if has_res else ()) + ((after,) if after is not None else ())))


def _rmsnorm_fwd(x, w, name, tm=256):
    Tn = x.shape[0]

    def body(x_ref, w_ref, o_ref):
        xv = x_ref[...]
        r = lax.rsqrt(jnp.mean(xv * xv, axis=1, keepdims=True) + EPS)
        o_ref[...] = (xv * r * w_ref[...]).astype(BF16)

    return pl.pallas_call(
        body, name=name, out_shape=jax.ShapeDtypeStruct((Tn, D), BF16), grid=(Tn // tm,),
        in_specs=[pl.BlockSpec((tm, D), lambda i: (i, 0)), pl.BlockSpec((1, D), lambda i: (0, 0))],
        out_specs=pl.BlockSpec((tm, D), lambda i: (i, 0)), compiler_params=_cp("parallel"),
    )(x, w)


def _rmsnorm_bwd(dy, h, w, res, name, tm=256):
    Tn = h.shape[0]

    def body(dy_ref, h_ref, w_ref, res_ref, dh_ref, dhb_ref, dw_ref):
        hv, dyv = h_ref[...], dy_ref[...]
        r = lax.rsqrt(jnp.mean(hv * hv, axis=1, keepdims=True) + EPS)
        g = dyv * w_ref[...]
        dh = res_ref[...] + r * g - hv * (r * r * r * jnp.mean(g * hv, axis=1, keepdims=True))
        dh_ref[...] = dh
        dhb_ref[...] = dh.astype(BF16)
        part = jnp.sum(dyv * hv * r, axis=0, keepdims=True)

        @pl.when(pl.program_id(0) == 0)
        def _():
            dw_ref[...] = part

        @pl.when(pl.program_id(0) > 0)
        def _():
            dw_ref[...] += part

    row = pl.BlockSpec((tm, D), lambda i: (i, 0))
    vec = pl.BlockSpec((1, D), lambda i: (0, 0))
    return pl.pallas_call(
        body, name=name,
        out_shape=(jax.ShapeDtypeStruct((Tn, D), F32), jax.ShapeDtypeStruct((Tn, D), BF16), jax.ShapeDtypeStruct((1, D), F32)),
        grid=(Tn // tm,), in_specs=[row, row, vec, row], out_specs=(row, row, vec), compiler_params=_cp("arbitrary"),
    )(dy, h, w, res)


def _loss_head(h2, wf, tgt, name="loss_head", tm=256):
    Tn = h2.shape[0]

    def body(h_ref, w_ref, t_ref, dh_ref, dhb_ref, dw_ref, loss_ref):
        hv, wv = h_ref[...], w_ref[...]
        r = lax.rsqrt(jnp.mean(hv * hv, axis=1, keepdims=True) + EPS)
        hn = hv * r
        e = hn * wv - t_ref[...]
        dy = e * (1.0 / D)
        g = dy * wv
        dh = r * g - hv * (r * r * r * jnp.mean(g * hv, axis=1, keepdims=True))
        dh_ref[...] = dh
        dhb_ref[...] = dh.astype(BF16)
        part = jnp.sum(dy * hn, axis=0, keepdims=True)
        lpart = (0.5 / D) * jnp.sum(jnp.sum(e * e, axis=1, keepdims=True), axis=0, keepdims=True)

        @pl.when(pl.program_id(0) == 0)
        def _():
            dw_ref[...] = part
            loss_ref[...] = lpart

        @pl.when(pl.program_id(0) > 0)
        def _():
            dw_ref[...] += part
            loss_ref[...] += lpart

    row = pl.BlockSpec((tm, D), lambda i: (i, 0))
    vec = pl.BlockSpec((1, D), lambda i: (0, 0))
    one = pl.BlockSpec((1, 1), lambda i: (0, 0))
    return pl.pallas_call(
        body, name=name,
        out_shape=(jax.ShapeDtypeStruct((Tn, D), F32), jax.ShapeDtypeStruct((Tn, D), BF16), jax.ShapeDtypeStruct((1, D), F32),
                   jax.ShapeDtypeStruct((1, 1), F32)),
        grid=(Tn // tm,), in_specs=[row, vec, row], out_specs=(row, row, vec, one), compiler_params=_cp("arbitrary"),
    )(h2, wf, tgt)


def _attn_mask(n):
    qi = lax.broadcasted_iota(jnp.int32, (NKV, GRP * WIN, 2 * WIN), 1) % WIN
    ki = lax.broadcasted_iota(jnp.int32, (NKV, GRP * WIN, 2 * WIN), 2)
    rel = qi + WIN - ki
    return (rel >= 0) & (rel < WIN) & ((n > 0) | (ki >= WIN))


def _kv_heads(prev_ref, cur_ref):
    return jnp.stack([jnp.concatenate([prev_ref[:, h * HD:(h + 1) * HD], cur_ref[:, h * HD:(h + 1) * HD]], axis=0) for h in range(NKV)])


def _q_heads(ref):
    return jnp.stack([jnp.concatenate([ref[:, (h * GRP + g) * HD:(h * GRP + g + 1) * HD] for g in range(GRP)], axis=0) for h in range(NKV)])


def _attn_probs(q_ref, kc_ref, kp_ref, sink_ref, mask):
    kk = _kv_heads(kp_ref, kc_ref).astype(BF16)
    qs = _q_heads(q_ref).astype(BF16)
    s = jnp.einsum('hqd,hkd->hqk', qs, kk, preferred_element_type=F32) * (HD ** -0.5)
    s = jnp.where(mask, s, MASKV)
    sink = jnp.stack([jnp.concatenate([jnp.full((WIN, 1), sink_ref[0, h * GRP + g], F32) for g in range(GRP)], axis=0) for h in range(NKV)])
    m = jnp.maximum(jnp.max(s, axis=2, keepdims=True), sink)
    e = jnp.exp(s - m)
    es = jnp.exp(sink - m)
    inv = 1.0 / (jnp.sum(e, axis=2, keepdims=True) + es)
    return e * inv, es * inv, qs, kk


def _attn_specs(nb, last):
    cur = lambda n: jnp.minimum(n, last)
    prev = lambda n: jnp.maximum(jnp.minimum(n, last) - 1, 0)
    return [
        pl.BlockSpec((WIN, NQ * HD), lambda n: (cur(n), C_AQ // (NQ * HD))),
        pl.BlockSpec((WIN, NKV * HD), lambda n: (cur(n), C_AK // (NKV * HD))),
        pl.BlockSpec((WIN, NKV * HD), lambda n: (prev(n), C_AK // (NKV * HD))),
        pl.BlockSpec((WIN, NKV * HD), lambda n: (cur(n), C_AV // (NKV * HD))),
        pl.BlockSpec((WIN, NKV * HD), lambda n: (prev(n), C_AV // (NKV * HD))),
    ]


def _attn_fwd(proj, sinks, name="attn_fwd"):
    Tn = proj.shape[0]
    nb = Tn // WIN

    def body(q_ref, kc_ref, kp_ref, vc_ref, vp_ref, sink_ref, o_ref):
        p, _, _, _ = _attn_probs(q_ref, kc_ref, kp_ref, sink_ref, _attn_mask(pl.program_id(0)))
        o = jnp.einsum('hqk,hkd->hqd', p.astype(BF16), _kv_heads(vp_ref, vc_ref).astype(BF16), preferred_element_type=F32)
        for h in range(NKV):
            for g in range(GRP):
                o_ref[:, (h * GRP + g) * HD:(h * GRP + g + 1) * HD] = o[h, g * WIN:(g + 1) * WIN, :]

    return pl.pallas_call(
        body, name=name, out_shape=jax.ShapeDtypeStruct((Tn, D), F32), grid=(nb,),
        in_specs=_attn_specs(nb, nb - 1) + [pl.BlockSpec(memory_space=pltpu.SMEM)],
        out_specs=pl.BlockSpec((WIN, D), lambda n: (n, 0)), compiler_params=_cp("parallel"),
    )(proj, proj, proj, proj, proj, sinks)


def _attn_bwd(proj, sinks, o, do, name="attn_bwd"):
    Tn = proj.shape[0]
    nb = Tn // WIN
    KW = NKV * HD

    def body(q_ref, kc_ref, kp_ref, vc_ref, vp_ref, o_ref, do_ref, sink_ref, dq_ref, dkv_ref, dsk_ref, carry, cur):
        n = pl.program_id(0)

        @pl.when(n == 0)
        def _():
            carry[...] = jnp.zeros_like(carry)
            dsk_ref[...] = jnp.zeros_like(dsk_ref)

        @pl.when(n < nb)
        def _():
            p, ps, qs, kk = _attn_probs(q_ref, kc_ref, kp_ref, sink_ref, _attn_mask(n))
            vv = _kv_heads(vp_ref, vc_ref).astype(BF16)
            dos = _q_heads(do_ref)
            delta = jnp.sum(dos * _q_heads(o_ref), axis=2, keepdims=True)
            dosb = dos.astype(BF16)
            dp = jnp.einsum('hqd,hkd->hqk', dosb, vv, preferred_element_type=F32)
            ds = (p * (dp - delta) * (HD ** -0.5)).astype(BF16)
            dq = jnp.einsum('hqk,hkd->hqd', ds, kk, preferred_element_type=F32)
            dkk = jnp.einsum('hqk,hqd->hkd', ds, qs, preferred_element_type=F32)
            dvv = jnp.einsum('hqk,hqd->hkd', p.astype(BF16), dosb, preferred_element_type=F32)
            dsk = ps * delta
            for h in range(NKV):
                for g in range(GRP):
                    i = h * GRP + g
                    dq_ref[:, i * HD:(i + 1) * HD] = dq[h, g * WIN:(g + 1) * WIN, :].astype(BF16)
                    dsk_ref[:, i:i + 1] -= jnp.sum(dsk[h, g * WIN:(g + 1) * WIN, :], axis=0, keepdims=True)
                dkv_ref[:, h * HD:(h + 1) * HD] = (carry[:, h * HD:(h + 1) * HD] + dkk[h, :WIN, :]).astype(BF16)
                dkv_ref[:, KW + h * HD:KW + (h + 1) * HD] = (carry[:, KW + h * HD:KW + (h + 1) * HD] + dvv[h, :WIN, :]).astype(BF16)
                cur[:, h * HD:(h + 1) * HD] = dkk[h, WIN:, :]
                cur[:, KW + h * HD:KW + (h + 1) * HD] = dvv[h, WIN:, :]
            carry[...] = cur[...]

        @pl.when(n == nb)
        def _():
            dkv_ref[...] = carry[...].astype(BF16)

    last = nb - 1
    row = pl.BlockSpec((WIN, D), lambda n: (jnp.minimum(n, last), 0))
    return pl.pallas_call(
        body, name=name,
        out_shape=(jax.ShapeDtypeStruct((Tn, D), BF16), jax.ShapeDtypeStruct((Tn, 2 * KW), BF16), jax.ShapeDtypeStruct((1, NQ), F32)),
        grid=(nb + 1,),
        in_specs=_attn_specs(nb, last) + [row, row, pl.BlockSpec(memory_space=pltpu.SMEM)],
        out_specs=(row, pl.BlockSpec((WIN, 2 * KW), lambda n: (jnp.maximum(n - 1, 0), 0)), pl.BlockSpec((1, NQ), lambda n: (0, 0))),
        scratch_shapes=[pltpu.VMEM((WIN, 2 * KW), F32), pltpu.VMEM((WIN, 2 * KW), F32)],
        compiler_params=_cp("arbitrary"),
    )(proj, proj, proj, proj, proj, o, do, sinks)


def _tri(lower):
    r = lax.broadcasted_iota(jnp.int32, (GC, GC), 0)
    c = lax.broadcasted_iota(jnp.int32, (GC, GC), 1)
    return r >= c if lower else r <= c


def _per_head(a):
    return jnp.stack([a[:, h * DK:(h + 1) * DK] for h in range(GH)])


def _all_heads(a):
    return jnp.concatenate([a[h] for h in range(GH)], axis=1)


def _gla_gates(lr, w2_ref, gb_ref):
    logit = _dot(lr, w2_ref[...].astype(BF16)) + gb_ref[...]
    la = (jnp.minimum(logit, 0.0) - jnp.log(1.0 + jnp.exp(-jnp.abs(logit)))) * (1.0 / 16.0)
    g = _dot(_tri(True).astype(F32), la, prec=HIGHEST)
    return logit, g


def _bmm(spec, a, b):
    return jnp.einsum(spec, a, b, preferred_element_type=F32)


def _gla_specs(nc, rev):
    idx = (lambda n: nc - 1 - n) if rev else (lambda n: n)
    half = 2 * DK
    return (
        [pl.BlockSpec((GC, half), lambda n, j=j: (idx(n), C_GQ // half + j)) for j in range(2)]
        + [pl.BlockSpec((GC, half), lambda n, j=j: (idx(n), C_GK // half + j)) for j in range(2)]
        + [pl.BlockSpec((GC, DV), lambda n, h=h: (idx(n), C_GV // DV + h)) for h in range(GH)]
        + [pl.BlockSpec((GC, LANE), lambda n: (idx(n), 0)), pl.BlockSpec((LANE, GH * DK), lambda n: (0, 0)),
           pl.BlockSpec((1, GH * DK), lambda n: (0, 0))])


def _gla_heads(refs):
    return (lambda h: refs[h // 2][:, (h % 2) * DK:(h % 2 + 1) * DK], lambda h: refs[2 + h // 2][:, (h % 2) * DK:(h % 2 + 1) * DK],
            lambda h: refs[4 + h][...])


def _gla_fwd(proj, plr, w2p, gb, name="gla_fwd"):
    Tn = proj.shape[0]
    nc = Tn // GC

    def body(*refs):
        qh, kh, vh = _gla_heads(refs)
        lr_ref, w2_ref, gb_ref, o_ref, st_ref, S = refs[8:]

        @pl.when(pl.program_id(0) == 0)
        def _():
            S[...] = jnp.zeros_like(S)

        heads = lambda f: jnp.stack([f(h) for h in range(GH)])
        _, g_all = _gla_gates(lr_ref[...].astype(BF16), w2_ref, gb_ref)
        g = _per_head(g_all)
        gl = g[:, GC - 1:GC, :]
        k = heads(kh)
        v = heads(vh).astype(BF16)
        qd = (heads(qh) * (DK ** -0.5) * jnp.exp(g)).astype(BF16)
        ki = (k * jnp.exp(-g)).astype(BF16)
        ke = (k * jnp.exp(gl - g)).astype(BF16)
        att = jnp.where(_tri(True)[None], _bmm('hid,hjd->hij', qd, ki), 0.0).astype(BF16)
        sp = S[...]
        st_ref[0] = sp
        o = _bmm('hij,hjv->hiv', att, v) + _bmm('hid,hvd->hiv', qd, sp.astype(BF16))
        for h in range(GH):
            o_ref[:, h * DV:(h + 1) * DV] = o[h]
        S[...] = sp * jnp.exp(gl) + _bmm('hjv,hjd->hvd', v, ke)

    return pl.pallas_call(
        body, name=name,
        out_shape=(jax.ShapeDtypeStruct((Tn, GH * DV), F32), jax.ShapeDtypeStruct((nc, GH, DV, DK), F32)),
        grid=(nc,), in_specs=_gla_specs(nc, False),
        out_specs=(pl.BlockSpec((GC, GH * DV), lambda n: (n, 0)), pl.BlockSpec((1, GH, DV, DK), lambda n: (n, 0, 0, 0))),
        scratch_shapes=[pltpu.VMEM((GH, DV, DK), F32)], compiler_params=_cp("arbitrary"),
    )(*([proj] * 8), plr, w2p, gb)


def _gla_bwd(proj, plr, w2p, gb, states, do, name="gla_bwd"):
    Tn = proj.shape[0]
    nc = Tn // GC

    def body(*refs):
        qh, kh, vh = _gla_heads(refs)
        lr_ref, w2_ref, gb_ref, st_ref, do_ref, dqk_ref, dv_ref, dlr_ref, dw2_ref, dgb_ref, dS = refs[8:]

        @pl.when(pl.program_id(0) == 0)
        def _():
            dS[...] = jnp.zeros_like(dS)
            dw2_ref[...] = jnp.zeros_like(dw2_ref)
            dgb_ref[...] = jnp.zeros_like(dgb_ref)

        heads = lambda f: jnp.stack([f(h) for h in range(GH)])
        lr = lr_ref[...].astype(BF16)
        causal = _tri(True)[None]
        last_row = lax.broadcasted_iota(jnp.int32, (GH, GC, DK), 1) == GC - 1
        logit, g_all = _gla_gates(lr, w2_ref, gb_ref)
        g = _per_head(g_all)
        gl = g[:, GC - 1:GC, :]
        egl = jnp.exp(gl)
        eg, eng, ege = jnp.exp(g), jnp.exp(-g), jnp.exp(gl - g)
        k = heads(kh)
        v = heads(vh).astype(BF16)
        dob = heads(lambda h: do_ref[:, h * DV:(h + 1) * DV]).astype(BF16)
        qd = heads(qh) * (DK ** -0.5) * eg
        ki = k * eng
        ke = k * ege
        qdb, kib, keb = qd.astype(BF16), ki.astype(BF16), ke.astype(BF16)
        att = jnp.where(causal, _bmm('hid,hjd->hij', qdb, kib), 0.0).astype(BF16)
        datt = jnp.where(causal, _bmm('hiv,hjv->hij', dob, v), 0.0).astype(BF16)
        sp = st_ref[0]
        dsn = dS[...]
        dsnb = dsn.astype(BF16)
        dv = (_bmm('hij,hiv->hjv', att, dob) + _bmm('hjd,hvd->hjv', keb, dsnb)).astype(BF16)
        dqd = _bmm('hij,hjd->hid', datt, kib) + _bmm('hiv,hvd->hid', dob, sp.astype(BF16))
        dki = _bmm('hij,hid->hjd', datt, qdb)
        dke = _bmm('hjv,hvd->hjd', v, dsnb)
        ddec = jnp.sum(dsn * sp, axis=1, keepdims=True)
        dS[...] = dsn * egl + _bmm('hiv,hid->hvd', dob, qdb)
        dke_ke = dke * ke
        dgl = jnp.sum(dke_ke, axis=1, keepdims=True) + ddec * egl
        dg = dqd * qd - dki * ki - dke_ke + jnp.where(last_row, dgl, 0.0)
        dq = (dqd * ((DK ** -0.5) * eg)).astype(BF16)
        dk = (dki * eng + dke * ege).astype(BF16)
        for h in range(GH):
            dv_ref[:, h * DV:(h + 1) * DV] = dv[h]
            dqk_ref[:, h * DK:(h + 1) * DK] = dq[h]
            dqk_ref[:, GH * DK + h * DK:GH * DK + (h + 1) * DK] = dk[h]
        dla = _dot(_tri(False).astype(F32), _all_heads(dg), prec=HIGHEST)
        dlogit = dla * (1.0 / 16.0) * _sigmoid(-logit)
        dlb = dlogit.astype(BF16)
        dlr_ref[...] = _dot(dlb, w2_ref[...].astype(BF16), tb=True).astype(BF16)
        dw2_ref[...] += _dot(lr, dlb, ta=True)
        dgb_ref[...] += jnp.sum(dlogit, axis=0, keepdims=True)

    rev = lambda n: nc - 1 - n
    row = pl.BlockSpec((GC, GH * DV), lambda n: (rev(n), 0))
    return pl.pallas_call(
        body, name=name,
        out_shape=(jax.ShapeDtypeStruct((Tn, 2 * GH * DK), BF16), jax.ShapeDtypeStruct((Tn, GH * DV), BF16),
                   jax.ShapeDtypeStruct((Tn, LANE), BF16), jax.ShapeDtypeStruct((LANE, GH * DK), F32),
                   jax.ShapeDtypeStruct((1, GH * DK), F32)),
        grid=(nc,),
        in_specs=_gla_specs(nc, True) + [pl.BlockSpec((1, GH, DV, DK), lambda n: (rev(n), 0, 0, 0)), row],
        out_specs=(row, row, pl.BlockSpec((GC, LANE), lambda n: (rev(n), 0)), pl.BlockSpec((LANE, GH * DK), lambda n: (0, 0)),
                   pl.BlockSpec((1, GH * DK), lambda n: (0, 0))),
        scratch_shapes=[pltpu.VMEM((GH, DV, DK), F32)], compiler_params=_cp("arbitrary"),
    )(*([proj] * 8), plr, w2p, gb, states, do)


def _merge_specs(tm):
    row = pl.BlockSpec((tm, D), lambda i: (i, 0))
    gates = [pl.BlockSpec((tm, DV), lambda i, j=c // DV + h: (i, j)) for c in (C_GR, C_GA, C_GB) for h in range(GH)]
    return row, gates, pl.BlockSpec((1, DV), lambda i: (0, 0))


def _merge_fwd(a, go, proj, gnw, name="merge_fwd", tm=256):
    Tn = a.shape[0]

    def body(a_ref, go_ref, *rest):
        gates, w_ref, m_ref = rest[:3 * GH], rest[3 * GH], rest[3 * GH + 1]
        for h in range(GH):
            sl = slice(h * DV, (h + 1) * DV)
            gov = go_ref[:, sl]
            r = lax.rsqrt(jnp.mean(gov * gov, axis=1, keepdims=True) + EPS)
            gr = gates[h][...]
            g2 = gov * r * w_ref[...] * (gr * _sigmoid(gr))
            m_ref[:, sl] = (_sigmoid(gates[GH + h][...]) * a_ref[:, sl] + _sigmoid(gates[2 * GH + h][...]) * g2).astype(BF16)

    row, gates, vec = _merge_specs(tm)
    return pl.pallas_call(
        body, name=name, out_shape=jax.ShapeDtypeStruct((Tn, D), BF16), grid=(Tn // tm,),
        in_specs=[row, row] + gates + [vec], out_specs=row, compiler_params=_cp("parallel"),
    )(a, go, *([proj] * (3 * GH)), gnw)


def _merge_bwd(dm, a, go, proj, gnw, name="merge_bwd", tm=256):
    Tn = a.shape[0]

    def body(dm_ref, a_ref, go_ref, *rest):
        gates = rest[:3 * GH]
        w_ref, da_ref, dgo_ref, dg_ref, dw_ref = rest[3 * GH:]
        wv = w_ref[...]
        dw = jnp.zeros((1, DV), F32)
        for h in range(GH):
            sl = slice(h * DV, (h + 1) * DV)
            dmv, av, gov, gr = dm_ref[:, sl], a_ref[:, sl], go_ref[:, sl], gates[h][...]
            sa, sb, sg = _sigmoid(gates[GH + h][...]), _sigmoid(gates[2 * GH + h][...]), _sigmoid(gr)
            r = lax.rsqrt(jnp.mean(gov * gov, axis=1, keepdims=True) + EPS)
            gn0 = gov * r
            gn = gn0 * wv
            silu = gr * sg
            dg2 = dmv * sb
            da_ref[:, sl] = dmv * sa
            dg_ref[:, D + h * DV:D + (h + 1) * DV] = (dmv * av * sa * (1.0 - sa)).astype(BF16)
            dg_ref[:, 2 * D + h * DV:2 * D + (h + 1) * DV] = (dg2 * gn * silu * (1.0 - sb)).astype(BF16)
            dg_ref[:, sl] = (dg2 * gn * (sg * (1.0 + gr * (1.0 - sg)))).astype(BF16)
            dgn = dg2 * silu
            dw = dw + jnp.sum(dgn * gn0, axis=0, keepdims=True)
            gg = dgn * wv
            dgo_ref[:, sl] = r * gg - gov * (r * r * r * jnp.mean(gg * gov, axis=1, keepdims=True))

        @pl.when(pl.program_id(0) == 0)
        def _():
            dw_ref[...] = dw

        @pl.when(pl.program_id(0) > 0)
        def _():
            dw_ref[...] += dw

    row, gates, vec = _merge_specs(tm)
    return pl.pallas_call(
        body, name=name,
        out_shape=(jax.ShapeDtypeStruct((Tn, D), F32), jax.ShapeDtypeStruct((Tn, D), F32), jax.ShapeDtypeStruct((Tn, 3 * D), BF16),
                   jax.ShapeDtypeStruct((1, DV), F32)),
        grid=(Tn // tm,), in_specs=[row, row, row] + gates + [vec],
        out_specs=(row, row, pl.BlockSpec((tm, 3 * D), lambda i: (i, 0)), vec), compiler_params=_cp("arbitrary"),
    )(dm, a, go, *([proj] * (3 * GH)), gnw)


def _ffn_up(v2, wgt, wut, name="ffn_up", tm=1024, tn=512):
    Tn = v2.shape[0]
    tm = min(tm, Tn)

    def body(v_ref, wg_ref, wu_ref, a_ref, b_ref, ff_ref):
        vv = v_ref[...]
        a = _dot(vv, wg_ref[...], tb=True)
        b = _dot(vv, wu_ref[...], tb=True)
        a_ref[...] = a.astype(BF16)
        b_ref[...] = b.astype(BF16)
        ff_ref[...] = (a * _sigmoid(a) * b).astype(BF16)

    w = pl.BlockSpec((tn, D), lambda j, i: (j, 0))
    act = pl.BlockSpec((tm, tn), lambda j, i: (i, j))
    return pl.pallas_call(
        body, name=name,
        out_shape=(jax.ShapeDtypeStruct((Tn, FH), BF16), jax.ShapeDtypeStruct((Tn, FH), BF16), jax.ShapeDtypeStruct((Tn, FH), BF16)),
        grid=(FH // tn, Tn // tm), in_specs=[pl.BlockSpec((tm, D), lambda j, i: (i, 0)), w, w], out_specs=(act, act, act),
        compiler_params=_cp("parallel", "parallel"),
    )(v2, wgt, wut)


def _ffn_dact(dh2b, wd, a, b, name="ffn_dact", tm=1024, tn=512):
    Tn = dh2b.shape[0]
    tm = min(tm, Tn)

    def body(d_ref, w_ref, a_ref, b_ref, da_ref, db_ref):
        dff = _dot(d_ref[...], w_ref[...], tb=True)
        av = a_ref[...].astype(F32)
        sg = _sigmoid(av)
        da_ref[...] = (dff * b_ref[...].astype(F32) * (sg * (1.0 + av * (1.0 - sg)))).astype(BF16)
        db_ref[...] = (dff * (av * sg)).astype(BF16)

    act = pl.BlockSpec((tm, tn), lambda j, i: (i, j))
    return pl.pallas_call(
        body, name=name,
        out_shape=(jax.ShapeDtypeStruct((Tn, FH), BF16), jax.ShapeDtypeStruct((Tn, FH), BF16)),
        grid=(FH // tn, Tn // tm),
        in_specs=[pl.BlockSpec((tm, D), lambda j, i: (i, 0)), pl.BlockSpec((tn, D), lambda j, i: (j, 0)), act, act],
        out_specs=(act, act), compiler_params=_cp("parallel", "parallel"),
    )(dh2b, wd, a, b)


def _adam_math(w, g, m, v):
    m2 = B1 * m + (1.0 - B1) * g
    v2 = B2 * v + (1.0 - B2) * (g * g)
    mh = m2 / (1.0 - B1 ** STEP)
    vh = v2 / (1.0 - B2 ** STEP)
    return -LR * (mh / (jnp.sqrt(vh) + AEPS) + WD * w), m2, v2


def _sum_blocks(o_ref, p_ref):
    g = o_ref[...].astype(F32)
    for j in range(p_ref.shape[0]):
        g = g + p_ref[j].astype(F32)
    return g


def _adamw(w, m, v, psums, parts, chip_idx, name, tr):
    R, C = w.shape

    def body(s_ref, w_ref, m_ref, v_ref, o_ref, p_ref, g_ref, d_ref, m2_ref, v2_ref):
        g = _sum_blocks(o_ref, p_ref)
        d, m2, v2 = _adam_math(w_ref[...], g, m_ref[...], v_ref[...])
        g_ref[...] = g
        d_ref[...] = d
        m2_ref[...] = m2
        v2_ref[...] = v2

    blk = pl.BlockSpec((tr, C), lambda i, s: (i, 0))
    out = jax.ShapeDtypeStruct((R, C), F32)
    grid_spec = pltpu.PrefetchScalarGridSpec(
        num_scalar_prefetch=1, grid=(R // tr,),
        in_specs=[blk, blk, blk, pl.BlockSpec((None, tr, C), lambda i, s: (s[0], i, 0)),
                  pl.BlockSpec((parts.shape[0], tr, C), lambda i, s: (0, i, 0))],
        out_specs=(blk, blk, blk, blk),
    )
    return pl.pallas_call(body, name=name, out_shape=(out, out, out, out), grid_spec=grid_spec, compiler_params=_cp("parallel"),
                          )(chip_idx, w, m, v, psums, parts)


def _adamw_rows(w, m, v, g, name, tr):
    R = w.shape[0]

    def body(w_ref, m_ref, v_ref, g_ref, d_ref, m2_ref, v2_ref):
        d, m2, v2 = _adam_math(w_ref[...], g_ref[...], m_ref[...], v_ref[...])
        d_ref[...] = d
        m2_ref[...] = m2
        v2_ref[...] = v2

    blk = pl.BlockSpec((tr,) + w.shape[1:], lambda i: (i, 0, 0))
    out = jax.ShapeDtypeStruct(w.shape, F32)
    return pl.pallas_call(body, name=name, out_shape=(out, out, out), grid=(R // tr,), in_specs=[blk] * 4, out_specs=(blk, blk, blk),
                          compiler_params=_cp("parallel"))(w, m, v, g)


def _sum_parts(psums, parts, chip_idx, name, tr, tc):
    _, R, C = psums.shape

    def body(s_ref, o_ref, p_ref, g_ref):
        g_ref[...] = _sum_blocks(o_ref, p_ref)

    grid_spec = pltpu.PrefetchScalarGridSpec(
        num_scalar_prefetch=1, grid=(R // tr, C // tc),
        in_specs=[pl.BlockSpec((None, tr, tc), lambda i, j, s: (s[0], i, j)),
                  pl.BlockSpec((parts.shape[0], tr, tc), lambda i, j, s: (0, i, j))],
        out_specs=pl.BlockSpec((tr, tc), lambda i, j, s: (i, j)),
    )
    return pl.pallas_call(body, name=name, out_shape=jax.ShapeDtypeStruct((R, C), F32), grid_spec=grid_spec,
                          compiler_params=_cp("parallel", "parallel"))(chip_idx, psums, parts)


def _adamw_plain(w, m, v, g, name):
    def body(w_ref, m_ref, v_ref, g_ref, d_ref, m2_ref, v2_ref):
        d, m2, v2 = _adam_math(w_ref[...], g_ref[...], m_ref[...], v_ref[...])
        d_ref[...] = d
        m2_ref[...] = m2
        v2_ref[...] = v2

    out = jax.ShapeDtypeStruct(w.shape, F32)
    return pl.pallas_call(body, name=name, out_shape=(out, out, out))(w, m, v, g)


def _sum_devices(pack_all, name="sum_small"):
    def body(p_ref, o_ref):
        s = p_ref[0]
        for k in range(1, NDEV):
            s = s + p_ref[k]
        o_ref[...] = s

    return pl.pallas_call(body, name=name, out_shape=jax.ShapeDtypeStruct(pack_all.shape[1:], F32))(pack_all)


def _pair_add(g5, recv, c_idx, name, tr):
    _, _, R, C = g5.shape

    def body(c_ref, g_ref, r_ref, o_ref):
        o_ref[...] = (g_ref[...].astype(F32) + r_ref[...].astype(F32)).astype(BF16)

    grid_spec = pltpu.PrefetchScalarGridSpec(
        num_scalar_prefetch=1, grid=(4, R // tr),
        in_specs=[pl.BlockSpec((None, None, tr, C), lambda q, i, c: (q, c[0], i, 0)), pl.BlockSpec((None, tr, C), lambda q, i, c: (q, i, 0))],
        out_specs=pl.BlockSpec((None, tr, C), lambda q, i, c: (q, i, 0)),
    )
    return pl.pallas_call(
        body, name=name, out_shape=jax.ShapeDtypeStruct((4, R, C), BF16), grid_spec=grid_spec,
        compiler_params=_cp("parallel", "parallel"),
    )(c_idx, g5, recv)


_ANY = pl.BlockSpec(memory_space=pl.ANY)


def _mesh_pos():
    x, y, c = lax.axis_index("x"), lax.axis_index("y"), lax.axis_index("c")
    return x, y, c, [(1 - x, y), (x, 1 - y), (1 - x, 1 - y)]


def _pair_exchange(grads, name):
    n = len(grads)

    def body(*refs):
        ins, outs = refs[:n], refs[n:2 * n]
        send, recv = refs[2 * n:]
        x, y, c, _ = _mesh_pos()
        big = [pltpu.make_async_remote_copy(src_ref=ins[a].at[:, 1 - c], dst_ref=outs[a], send_sem=send.at[a], recv_sem=recv.at[a],
                                            device_id=(x, y, 1 - c), device_id_type=MESH) for a in range(n)]
        for d in big:
            d.start()
        for d in big:
            d.wait_recv()
        for d in big:
            d.wait_send()

    return pl.pallas_call(
        body, name=name, out_shape=tuple(jax.ShapeDtypeStruct((4,) + g.shape[2:], g.dtype) for g in grads),
        in_specs=[_ANY] * n, out_specs=tuple([_ANY] * n),
        scratch_shapes=[pltpu.SemaphoreType.DMA((n,)), pltpu.SemaphoreType.DMA((n,))],
    )(*grads)


def _gather_small(pack, name="gather_small"):
    def body(pk, pk_all, psend, precv, loc):
        x, y, c, chips = _mesh_pos()
        me_slot = 4 * x + 2 * y + c
        sib = (x, y, 1 - c)
        own = pltpu.make_async_copy(pk, pk_all.at[me_slot], loc)
        own.start()
        peers = [sib] + [(*chip, c) for chip in chips] + [(*chip, 1 - c) for chip in chips]
        small = [pltpu.make_async_remote_copy(src_ref=pk, dst_ref=pk_all.at[me_slot], send_sem=psend.at[k], recv_sem=precv.at[k],
                                              device_id=p, device_id_type=MESH) for k, p in enumerate(peers)]
        for d in small:
            d.start()
        for k, p in enumerate(peers):
            pltpu.make_async_remote_copy(src_ref=pk, dst_ref=pk_all.at[4 * p[0] + 2 * p[1] + p[2]], send_sem=psend.at[k],
                                         recv_sem=precv.at[k], device_id=p, device_id_type=MESH).wait_recv()
        for d in small:
            d.wait_send()
        own.wait()

    return pl.pallas_call(
        body, name=name, out_shape=jax.ShapeDtypeStruct((NDEV,) + pack.shape, pack.dtype), in_specs=[_ANY], out_specs=_ANY,
        scratch_shapes=[pltpu.SemaphoreType.DMA((7,)), pltpu.SemaphoreType.DMA((7,)), pltpu.SemaphoreType.DMA(())],
    )(pack)


def _main_row(g):
    return g if g < C_LR else g - RANK


def _window_pieces(lo, hi):
    out = []
    for a, b, where in ((lo, min(hi, C_LR), "main"), (max(lo, C_LR), min(hi, C_LR + RANK), "lr"), (max(lo, C_LR + RANK), hi, "main")):
        if a < b:
            out.append((a, b, where, _main_row(a) if where == "main" else a - C_LR))
    return out


def _assemble_w_in(windows, own, name="assemble_w_in"):
    edges = NDEV - 1

    def body(b_ref, own_ref, main_ref, lr_ref, buf, ebuf, in_sems, out_sems, esems):
        dev = 4 * lax.axis_index("x") + 2 * lax.axis_index("y") + lax.axis_index("c")

        def load(k):
            return pltpu.make_async_copy(b_ref.at[k], buf.at[k % 2], in_sems.at[k % 2])

        def start_load(k):
            pl.when(dev == k)(pltpu.make_async_copy(own_ref, buf.at[k % 2], in_sems.at[k % 2]).start)
            pl.when(dev != k)(load(k).start)

        lr_ref[RANK:, :] = jnp.zeros((LANE - RANK, D), BF16)
        start_load(0)
        pending, edge_out = [], []
        for k in range(NDEV):
            s = k % 2
            load(k).wait()
            if k:
                ebuf[k - 1] = buf[1 - s, WSTEP:WWIN, :] + buf[s, 0:16, :]
                edge_out.append(pltpu.make_async_copy(ebuf.at[k - 1], main_ref.at[pl.ds(_main_row(WSTEP * k), 16)], esems.at[k - 1]))
                edge_out[-1].start()
                for d in pending:
                    d.wait()
            if k + 1 < NDEV:
                start_load(k + 1)
            pending = []
            lo = WSTEP * k + (16 if k else 0)
            hi = WSTEP * k + (WWIN if k == NDEV - 1 else WSTEP)
            for a, b, where, dst in _window_pieces(lo, hi):
                if where == "lr":
                    lr_ref[dst:dst + b - a, :] = buf[s, a - WSTEP * k:b - WSTEP * k, :]
                else:
                    pending.append(pltpu.make_async_copy(buf.at[s, pl.ds(a - WSTEP * k, b - a)], main_ref.at[pl.ds(dst, b - a)],
                                                         out_sems.at[2 * s + len(pending)]))
                    pending[-1].start()
        for d in pending + edge_out:
            d.wait()

    return pl.pallas_call(
        body, name=name,
        out_shape=(jax.ShapeDtypeStruct((NMAIN, D), BF16), jax.ShapeDtypeStruct((LANE, D), BF16)),
        in_specs=[_ANY, _ANY], out_specs=(_ANY, pl.BlockSpec(memory_space=pltpu.VMEM)),
        scratch_shapes=[pltpu.VMEM((2, WWIN, D), BF16), pltpu.VMEM((edges, 16, D), BF16), pltpu.SemaphoreType.DMA((2,)),
                        pltpu.SemaphoreType.DMA((4,)), pltpu.SemaphoreType.DMA((edges,))],
        compiler_params=pltpu.CompilerParams(vmem_limit_bytes=VMEM_LIMIT),
    )(windows, own)


def _disassemble_w_in(d_main, d_lr, name="disassemble_w_in"):
    def body(main_ref, lr_ref, g_ref, buf, in_sems, out_sems):
        def loads(k):
            s, out = k % 2, []
            for a, b, where, src0 in _window_pieces(WSTEP * k, WSTEP * k + WWIN):
                if where == "main":
                    out.append(pltpu.make_async_copy(main_ref.at[pl.ds(src0, b - a)], buf.at[s, pl.ds(a - WSTEP * k, b - a)],
                                                     in_sems.at[2 * s + len(out)]))
            return out

        def store(k):
            return pltpu.make_async_copy(buf.at[k % 2], g_ref.at[k], out_sems.at[k % 2])

        for d in loads(0):
            d.start()
        for k in range(NDEV):
            for d in loads(k):
                d.wait()
            for a, b, where, src0 in _window_pieces(WSTEP * k, WSTEP * k + WWIN):
                if where == "lr":
                    buf[k % 2, a - WSTEP * k:b - WSTEP * k, :] = lr_ref[src0:src0 + b - a, :]
            if k:
                store(k - 1).wait()
            if k + 1 < NDEV:
                for d in loads(k + 1):
                    d.start()
            store(k).start()
        store(NDEV - 1).wait()

    return pl.pallas_call(
        body, name=name, out_shape=jax.ShapeDtypeStruct((NDEV, WWIN, D), BF16),
        in_specs=[_ANY, pl.BlockSpec(memory_space=pltpu.VMEM)], out_specs=_ANY,
        scratch_shapes=[pltpu.VMEM((2, WWIN, D), BF16), pltpu.SemaphoreType.DMA((4,)), pltpu.SemaphoreType.DMA((2,))],
        compiler_params=pltpu.CompilerParams(vmem_limit_bytes=VMEM_LIMIT),
    )(d_main, d_lr)


_HBM = pl.BlockSpec(memory_space=pltpu.HBM)
_SEM = pl.BlockSpec(memory_space=pltpu.SEMAPHORE)
_VMEM = pl.BlockSpec(memory_space=pltpu.VMEM)
_SIDE = pltpu.CompilerParams(has_side_effects=pltpu.SideEffectType.DATAFLOW_SIDE_EFFECTING)
_TOKEN = jax.ShapeDtypeStruct((8, LANE), F32)


def _hbm(a):
    return pltpu.with_memory_space_constraint(a, pltpu.HBM)


def _hbm_like(arrs):
    return tuple(pltpu.HBM(a.shape, a.dtype) for a in arrs)


def _tie(x, token):
    return x + token[0, 0].astype(x.dtype)


def _chip_copies(ins, lands, send, recv, nrel):
    x, y, c, chips = _mesh_pos()
    return [pltpu.make_async_remote_copy(src_ref=ins[a].at[2 * chip[0] + chip[1]], dst_ref=lands[a].at[j], send_sem=send.at[nrel * a + j],
                                         recv_sem=recv.at[nrel * a + j], device_id=(*chip, c), device_id_type=MESH)
            for a in range(len(ins)) for j, chip in enumerate(chips[:nrel])]


def _chip_start(psums, name, nrel=3):
    n = len(psums)
    lands = [lax.empty((nrel,) + p.shape[1:], p.dtype) for p in psums]

    def body(*refs):
        for d in _chip_copies(refs[:n], refs[n:2 * n], refs[2 * n], refs[2 * n + 1], nrel):
            d.start()
        refs[-1][...] = jnp.zeros_like(refs[-1])

    sems = pltpu.SemaphoreType.DMA((nrel * n,))
    out = pl.pallas_call(
        body, name=name, out_shape=(sems, sems) + _hbm_like(psums) + _hbm_like(lands) + (_TOKEN,),
        in_specs=[_HBM] * (2 * n), out_specs=(_SEM, _SEM) + (_HBM,) * (2 * n) + (_VMEM,),
        input_output_aliases={i: 2 + i for i in range(2 * n)}, compiler_params=_SIDE,
    )(*[_hbm(a) for a in list(psums) + lands])
    return out[0], out[1], list(out[2:2 + n]), list(out[2 + n:2 + 2 * n]), out[-1]


def _chip_wait(send, recv, psums, lands, after, name):
    n = len(psums)
    nrel = lands[0].shape[0]

    def body(*refs):
        for d in _chip_copies(refs[:n], refs[n:2 * n], refs[2 * n], refs[2 * n + 1], nrel):
            d.wait_send()
            d.wait_recv()

    out = pl.pallas_call(
        body, name=name, out_shape=_hbm_like(psums) + _hbm_like(lands),
        in_specs=[_HBM] * (2 * n) + [_SEM, _SEM, _ANY], out_specs=(_HBM,) * (2 * n),
        input_output_aliases={i: i for i in range(2 * n)}, compiler_params=_SIDE,
    )(*psums, *lands, send, recv, after)
    return list(out[:n]), list(out[n:])


def _hop_pos():
    x, y, c, _ = _mesh_pos()
    north = c == 1
    via = (jnp.where(north, 1 - x, x), jnp.where(north, y, 1 - y))
    return (*via, c), 2 * (1 - x) + (1 - y), jnp.where(north, 2 * x + (1 - y), 2 * (1 - x) + y)


def _hop_copies(ins, lands, send, recv):
    to, mine, _ = _hop_pos()
    return [pltpu.make_async_remote_copy(src_ref=ins[a].at[mine], dst_ref=lands[a], send_sem=send.at[a], recv_sem=recv.at[a],
                                         device_id=to, device_id_type=MESH) for a in range(len(ins))]


def _hop_start(psums, name):
    n = len(psums)
    lands = [lax.empty(p.shape[1:], p.dtype) for p in psums]

    def body(*refs):
        for d in _hop_copies(refs[:n], refs[n:2 * n], refs[2 * n], refs[2 * n + 1]):
            d.start()
        refs[-1][...] = jnp.zeros_like(refs[-1])

    sems = pltpu.SemaphoreType.DMA((n,))
    out = pl.pallas_call(
        body, name=name, out_shape=(sems, sems) + _hbm_like(psums) + _hbm_like(lands) + (_TOKEN,),
        in_specs=[_HBM] * (2 * n), out_specs=(_SEM, _SEM) + (_HBM,) * (2 * n) + (_VMEM,),
        input_output_aliases={i: 2 + i for i in range(2 * n)}, compiler_params=_SIDE,
    )(*[_hbm(a) for a in list(psums) + lands])
    return out[0], out[1], list(out[2:2 + n]), list(out[2 + n:2 + 2 * n]), out[-1]


def _hop_wait(send, recv, psums, lands, after, name):
    n = len(psums)

    def body(*refs):
        for d in _hop_copies(refs[:n], refs[n:2 * n], refs[2 * n], refs[2 * n + 1]):
            d.wait_send()
            d.wait_recv()

    out = pl.pallas_call(
        body, name=name, out_shape=_hbm_like(psums) + _hbm_like(lands),
        in_specs=[_HBM] * (2 * n) + [_SEM, _SEM, _ANY], out_specs=(_HBM,) * (2 * n),
        input_output_aliases={i: i for i in range(2 * n)}, compiler_params=_SIDE,
    )(*psums, *lands, send, recv, after)
    return list(out[:n]), list(out[n:])


def _hop_add(psums, land, idx, name, tr):
    _, R, C = psums.shape

    def body(s_ref, p_ref, l_ref, o_ref):
        o_ref[...] = (p_ref[...].astype(F32) + l_ref[...].astype(F32)).astype(BF16)

    blk = pl.BlockSpec((None, tr, C), lambda i, s: (s[0], i, 0))
    grid_spec = pltpu.PrefetchScalarGridSpec(num_scalar_prefetch=1, grid=(R // tr,),
                                             in_specs=[blk, pl.BlockSpec((tr, C), lambda i, s: (i, 0))], out_specs=blk)
    return pl.pallas_call(body, name=name, out_shape=jax.ShapeDtypeStruct(psums.shape, BF16), grid_spec=grid_spec,
                          input_output_aliases={1: 0}, compiler_params=_cp("parallel"))(idx, psums, land)


def _pair_copies(ins, lands, send, recv):
    x, y, c, _ = _mesh_pos()
    return [pltpu.make_async_remote_copy(src_ref=ins[a].at[:, 1 - c], dst_ref=lands[a], send_sem=send.at[a], recv_sem=recv.at[a],
                                         device_id=(x, y, 1 - c), device_id_type=MESH) for a in range(len(ins))]


def _pair_start(grads, name):
    n = len(grads)
    lands = [lax.empty((4,) + g.shape[2:], g.dtype) for g in grads]

    def body(*refs):
        for d in _pair_copies(refs[:n], refs[n:2 * n], refs[2 * n], refs[2 * n + 1]):
            d.start()
        refs[-1][...] = jnp.zeros_like(refs[-1])

    sems = pltpu.SemaphoreType.DMA((n,))
    out = pl.pallas_call(
        body, name=name, out_shape=(sems, sems) + _hbm_like(grads) + _hbm_like(lands) + (_TOKEN,),
        in_specs=[_HBM] * (2 * n), out_specs=(_SEM, _SEM) + (_HBM,) * (2 * n) + (_VMEM,),
        input_output_aliases={i: 2 + i for i in range(2 * n)}, compiler_params=_SIDE,
    )(*[_hbm(a) for a in list(grads) + lands])
    return out[0], out[1], list(out[2:2 + n]), list(out[2 + n:2 + 2 * n]), out[-1]


def _pair_wait(send, recv, grads, lands, after, name):
    n = len(grads)

    def body(*refs):
        for d in _pair_copies(refs[:n], refs[n:2 * n], refs[2 * n], refs[2 * n + 1]):
            d.wait_send()
            d.wait_recv()

    out = pl.pallas_call(
        body, name=name, out_shape=_hbm_like(grads) + _hbm_like(lands),
        in_specs=[_HBM] * (2 * n) + [_SEM, _SEM, _ANY], out_specs=(_HBM,) * (2 * n),
        input_output_aliases={i: i for i in range(2 * n)}, compiler_params=_SIDE,
    )(*grads, *lands, send, recv, after)
    return list(out[:n]), list(out[n:])


def _slot(chip, c):
    return 4 * chip[0] + 2 * chip[1] + c


def _gather_start(shards, lands, after, name):
    n = len(shards)

    def body(*refs):
        src, land, send, recv = refs[:n], refs[n:2 * n], refs[2 * n + 1], refs[2 * n + 2]
        x, y, c, chips = _mesh_pos()
        for a in range(n):
            for k, to in enumerate([(x, y, 1 - c)] + [(*chip, c) for chip in chips]):
                pltpu.make_async_remote_copy(src_ref=src[a], dst_ref=land[a].at[_slot((x, y), c)], send_sem=send.at[4 * a + k],
                                             recv_sem=recv.at[4 * a + k], device_id=to, device_id_type=MESH).start()
        refs[-1][...] = jnp.zeros_like(refs[-1])

    sems = pltpu.SemaphoreType.DMA((4 * n,))
    out = pl.pallas_call(
        body, name=name, out_shape=(sems, sems) + _hbm_like(shards) + _hbm_like(lands) + (_TOKEN,),
        in_specs=[_HBM] * (2 * n) + [_ANY], out_specs=(_SEM, _SEM) + (_HBM,) * (2 * n) + (_VMEM,),
        input_output_aliases={i: 2 + i for i in range(2 * n)}, compiler_params=_SIDE,
    )(*[_hbm(a) for a in list(shards) + list(lands)], after)
    return out[0], out[1], list(out[2:2 + n]), list(out[2 + n:2 + 2 * n]), out[-1]


def _gather_pass(lands, recv, after, name, first=0):
    n = len(lands)

    def body(*refs):
        land, recv1 = refs[:n], refs[n]
        send2, recv2 = refs[n + 2], refs[n + 3]
        x, y, c, chips = _mesh_pos()
        for a in range(n):
            for j, chip in enumerate(chips):
                blk = land[a].at[_slot(chip, c)]
                pltpu.make_async_remote_copy(src_ref=blk, dst_ref=blk, send_sem=send2.at[3 * a + j], recv_sem=recv1.at[4 * (first + a) + 1 + j],
                                             device_id=(*chip, c), device_id_type=MESH).wait_recv()
                pltpu.make_async_remote_copy(src_ref=blk, dst_ref=blk, send_sem=send2.at[3 * a + j], recv_sem=recv2.at[3 * a + j],
                                             device_id=(x, y, 1 - c), device_id_type=MESH).start()
        refs[-1][...] = jnp.zeros_like(refs[-1])

    sems = pltpu.SemaphoreType.DMA((3 * n,))
    out = pl.pallas_call(
        body, name=name, out_shape=(sems, sems) + _hbm_like(lands) + (_TOKEN,),
        in_specs=[_HBM] * n + [_SEM, _ANY], out_specs=(_SEM, _SEM) + (_HBM,) * n + (_VMEM,),
        input_output_aliases={i: 2 + i for i in range(n)}, compiler_params=_SIDE,
    )(*lands, recv, after)
    return out[0], out[1], list(out[2:2 + n]), out[-1]


def _gather_wait(shards, lands, send, recv, send2, recv2, after, name, first=0):
    n = len(lands)

    def body(*refs):
        src, land = refs[:n], refs[n:2 * n]
        send1, recv1, snd2, rcv2 = refs[2 * n:2 * n + 4]
        x, y, c, chips = _mesh_pos()
        sib = (x, y, 1 - c)
        for a in range(n):
            for k in range(4):
                pltpu.make_async_remote_copy(src_ref=src[a], dst_ref=land[a].at[_slot((x, y), c)], send_sem=send1.at[4 * (first + a) + k],
                                             recv_sem=recv1.at[4 * (first + a) + k], device_id=sib, device_id_type=MESH).wait_send()
            blk = land[a].at[_slot((x, y), 1 - c)]
            pltpu.make_async_remote_copy(src_ref=blk, dst_ref=blk, send_sem=send1.at[4 * (first + a)], recv_sem=recv1.at[4 * (first + a)],
                                         device_id=sib, device_id_type=MESH).wait_recv()
            for j, chip in enumerate(chips):
                mine, theirs = land[a].at[_slot(chip, c)], land[a].at[_slot(chip, 1 - c)]
                pltpu.make_async_remote_copy(src_ref=mine, dst_ref=mine, send_sem=snd2.at[3 * a + j], recv_sem=rcv2.at[3 * a + j],
                                             device_id=sib, device_id_type=MESH).wait_send()
                pltpu.make_async_remote_copy(src_ref=theirs, dst_ref=theirs, send_sem=snd2.at[3 * a + j], recv_sem=rcv2.at[3 * a + j],
                                             device_id=sib, device_id_type=MESH).wait_recv()

    out = pl.pallas_call(
        body, name=name, out_shape=_hbm_like(shards) + _hbm_like(lands),
        in_specs=[_HBM] * (2 * n) + [_SEM] * 4 + [_ANY], out_specs=(_HBM,) * (2 * n),
        input_output_aliases={i: i for i in range(2 * n)}, compiler_params=_SIDE,
    )(*shards, *lands, send, recv, send2, recv2, after)
    return list(out[n:])


def _win_tree():
    x, y, c, chips = _mesh_pos()
    north = c == 1
    handed = (jnp.where(north, 1 - x, x), jnp.where(north, y, 1 - y))
    hand_to = (jnp.where(north, x, 1 - x), jnp.where(north, 1 - y, y))
    return x, y, c, chips, handed, hand_to


def _blk(land, chip, c):
    return land.at[_slot(chip, c)]


def _rcopy(src, dst, send, recv, to):
    return pltpu.make_async_remote_copy(src_ref=src, dst_ref=dst, send_sem=send, recv_sem=recv, device_id=to, device_id_type=MESH)


def _win_start(shards, lands, name):
    n = len(shards)

    def body(*refs):
        src, land, send, recv = refs[:n], refs[n:2 * n], refs[2 * n], refs[2 * n + 1]
        x, y, c, chips, _, _ = _win_tree()
        for a in range(n):
            for k, to in enumerate([(x, y, 1 - c), (*chips[0], c), (*chips[1], c)]):
                _rcopy(src[a], _blk(land[a], (x, y), c), send.at[3 * a + k], recv.at[3 * a + k], to).start()
        refs[-1][...] = jnp.zeros_like(refs[-1])

    sems = pltpu.SemaphoreType.DMA((3 * n,))
    out = pl.pallas_call(
        body, name=name, out_shape=(sems, sems) + _hbm_like(shards) + _hbm_like(lands) + (_TOKEN,),
        in_specs=[_HBM] * (2 * n), out_specs=(_SEM, _SEM) + (_HBM,) * (2 * n) + (_VMEM,),
        input_output_aliases={i: 2 + i for i in range(2 * n)}, compiler_params=_SIDE,
    )(*[_hbm(a) for a in list(shards) + list(lands)])
    return out[0], out[1], list(out[2:2 + n]), list(out[2 + n:2 + 2 * n]), out[-1]


def _win_hand_on(lands, recv1, after, name):
    n, m = len(lands), len(after)

    def body(*refs):
        land, rcv1 = refs[:n], refs[n]
        send2, recv2 = refs[n + 1 + m], refs[n + 2 + m]
        x, y, c, chips, handed, hand_to = _win_tree()
        for a in range(n):
            for j in range(2):
                blk = _blk(land[a], chips[j], c)
                _rcopy(blk, blk, send2.at[3 * a], rcv1.at[3 * a + 1 + j], (*chips[j], c)).wait_recv()
            blk = _blk(land[a], handed, c)
            _rcopy(blk, blk, send2.at[3 * a], recv2.at[3 * a], (*hand_to, c)).start()
            for j in range(2):
                blk = _blk(land[a], chips[j], c)
                _rcopy(blk, blk, send2.at[3 * a + 1 + j], recv2.at[3 * a + 1 + j], (x, y, 1 - c)).start()
        refs[-1][...] = jnp.zeros_like(refs[-1])

    sems = pltpu.SemaphoreType.DMA((3 * n,))
    out = pl.pallas_call(
        body, name=name, out_shape=(sems, sems) + _hbm_like(lands) + (_TOKEN,),
        in_specs=[_HBM] * n + [_SEM] + [_ANY] * m, out_specs=(_SEM, _SEM) + (_HBM,) * n + (_VMEM,),
        input_output_aliases={i: 2 + i for i in range(n)}, compiler_params=_SIDE,
    )(*lands, recv1, *after)
    return out[0], out[1], list(out[2:2 + n]), out[-1]


def _win_last(lands, recv2, after, name):
    n, m = len(lands), len(after)

    def body(*refs):
        land, rcv2 = refs[:n], refs[n]
        send3, recv3 = refs[n + 1 + m], refs[n + 2 + m]
        x, y, c, chips, _, hand_to = _win_tree()
        for a in range(n):
            blk = _blk(land[a], chips[2], c)
            _rcopy(blk, blk, send3.at[a], rcv2.at[3 * a], (*hand_to, c)).wait_recv()
            _rcopy(blk, blk, send3.at[a], recv3.at[a], (x, y, 1 - c)).start()
        refs[-1][...] = jnp.zeros_like(refs[-1])

    sems = pltpu.SemaphoreType.DMA((n,))
    out = pl.pallas_call(
        body, name=name, out_shape=(sems, sems) + _hbm_like(lands) + (_TOKEN,),
        in_specs=[_HBM] * n + [_SEM] + [_ANY] * m, out_specs=(_SEM, _SEM) + (_HBM,) * n + (_VMEM,),
        input_output_aliases={i: 2 + i for i in range(n)}, compiler_params=_SIDE,
    )(*lands, recv2, *after)
    return out[0], out[1], list(out[2:2 + n]), out[-1]


def _win_wait(shards, lands, sems1, sems2, sems3, after, name):
    n = len(lands)

    def body(*refs):
        src, land = refs[:n], refs[n:2 * n]
        send1, recv1, send2, recv2, send3, recv3 = refs[2 * n:2 * n + 6]
        x, y, c, chips, handed, hand_to = _win_tree()
        sib = (x, y, 1 - c)
        for a in range(n):
            own = _blk(land[a], (x, y), c)
            for k in range(3):
                _rcopy(src[a], own, send1.at[3 * a + k], recv1.at[3 * a + k], sib).wait_send()
            blk = _blk(land[a], (x, y), 1 - c)
            _rcopy(blk, blk, send1.at[3 * a], recv1.at[3 * a], sib).wait_recv()
            blk = _blk(land[a], handed, c)
            _rcopy(blk, blk, send2.at[3 * a], recv2.at[3 * a], sib).wait_send()
            for j in range(2):
                mine, theirs = _blk(land[a], chips[j], c), _blk(land[a], chips[j], 1 - c)
                _rcopy(mine, mine, send2.at[3 * a + 1 + j], recv2.at[3 * a + 1 + j], sib).wait_send()
                _rcopy(theirs, theirs, send2.at[3 * a + 1 + j], recv2.at[3 * a + 1 + j], sib).wait_recv()
            mine, theirs = _blk(land[a], chips[2], c), _blk(land[a], chips[2], 1 - c)
            _rcopy(mine, mine, send3.at[a], recv3.at[a], sib).wait_send()
            _rcopy(theirs, theirs, send3.at[a], recv3.at[a], sib).wait_recv()

    out = pl.pallas_call(
        body, name=name, out_shape=_hbm_like(shards) + _hbm_like(lands),
        in_specs=[_HBM] * (2 * n) + [_SEM] * 6 + [_ANY], out_specs=(_HBM,) * (2 * n),
        input_output_aliases={i: i for i in range(2 * n)}, compiler_params=_SIDE,
    )(*shards, *lands, *sems1, *sems2, *sems3, after)
    return list(out[n:])


def _pad_to(v, n):
    return jnp.pad(v, [(0, 0)] * (v.ndim - 1) + [(0, n - v.shape[-1])])


def _pack_small(n1, gb, sk, gn, n2, fn, extra=None):
    parts = [n1.reshape(-1), gb.reshape(-1), sk.reshape(-1), gn.reshape(-1), n2.reshape(-1), fn.reshape(-1)]
    flat = jnp.concatenate(parts + ([extra.reshape(-1)] if extra is not None else []))
    return _pad_to(flat, SMALL_N).reshape(SMALL_ROWS, LANE)


def _unpack_small(p):
    f = p.reshape(-1)
    return (f[S_N1:S_GB].reshape(1, D), f[S_GB:S_SK].reshape(1, GH * DK), f[S_SK:S_GN].reshape(1, NQ), f[S_GN:S_N2].reshape(1, DV),
            f[S_N2:S_FN].reshape(1, D), f[S_FN:S_LOSS].reshape(D))


class _NoComm:
    def __init__(self, wo, wg_all, wu_all, wd_all):
        self.rest = (wo, wg_all, wu_all, wd_all)

    def mixed(self, gla_o, gla_norm_w):
        return gla_norm_w

    def w_out(self, merged, norm2_w):
        return self.rest[0], norm2_w

    def w_up(self, v2):
        return self.rest[1], self.rest[2]

    def w_down(self, ff):
        return self.rest[3]

    def ffn_grads(self, d_wg, d_wu, d_wd):
        self.ffn = (d_wg, d_wu, d_wd)

    def ffn_reduce(self, dv2, norm2_w):
        return norm2_w

    def in_grads(self, d_wmain, d_wlr, d_wo, w_lr):
        self.inw = (d_wmain, d_wlr, d_wo)
        return w_lr

    def in_reduce(self, du_lr):
        return None


class _Comm:
    def __init__(self, rest_shards, rest_lands, after, c_idx):
        self.c_idx = c_idx
        self.send, self.recv, self.shards, self.lands, self.token = _gather_start(rest_shards, rest_lands, after, "gather_rest_start")

    def _pass(self, lo, hi, after, tag):
        send2, recv2, lands, token = _gather_pass(self.lands[lo:hi], self.recv, after, "gather_pass_" + tag, first=lo)
        self.passed = (lo, hi, send2, recv2, lands)
        return token

    def _wait(self, after, tag):
        lo, hi, send2, recv2, lands = self.passed
        return _gather_wait(self.shards[lo:hi], lands, self.send, self.recv, send2, recv2, after, "gather_wait_" + tag, first=lo)

    def mixed(self, gla_o, gla_norm_w):
        return _tie(gla_norm_w, self._pass(0, 1, gla_o, "out"))

    def w_out(self, merged, norm2_w):
        (wo_all,) = self._wait(merged, "out")
        return wo_all.reshape(D, D), _tie(norm2_w, self._pass(1, 3, merged, "up"))

    def w_up(self, v2):
        wg_all, wu_all = self._wait(v2, "up")
        self._pass(3, 4, v2, "down")
        return wg_all.reshape(FH, D), wu_all.reshape(FH, D)

    def w_down(self, ff):
        return self._wait(ff, "down")[0].reshape(FH, D)

    def _reduce(self, tag, names, grads, recv1, rows):
        psums = [_pair_add(g, r, self.c_idx, "pair_add_" + nm, tr) for g, r, nm, tr in zip(grads, recv1, names, rows)]
        *flight, token = _chip_start(psums, "reduce_chips_start_" + tag)
        return dict(tag=tag, names=names, rows=rows, flight=flight), token

    def ffn_grads(self, d_wg, d_wu, d_wd):
        self.ffn_pair = _pair_start([d.reshape(4, 2, FS, D) for d in (d_wg, d_wu, d_wd)], "reduce_pair_start_ffn")
        return self.ffn_pair[-1]

    def ffn_reduce(self, dv2, norm2_w):
        send, recv, grads, lands, _ = self.ffn_pair
        grads, recv1 = _pair_wait(send, recv, grads, lands, dv2, "reduce_pair_wait_ffn")
        self.ffn, token = self._reduce("ffn", ["w_ffn_gate", "w_ffn_up", "w_ffn_down"], grads, recv1, [176, 176, 176])
        return _tie(norm2_w, token)

    def in_grads(self, d_wmain, d_wlr, d_wo, w_lr):
        grads = [_disassemble_w_in(d_wmain, d_wlr).reshape(4, 2, WWIN, D), d_wo.reshape(4, 2, D // NDEV, D)]
        self.in_names, self.in_rows = ["w_in", "w_out"], [808, 256]
        psums = [_pair_add(g, r, self.c_idx, "pair_add_" + nm, tr)
                 for g, r, nm, tr in zip(grads, _pair_exchange(grads, "reduce_pair_in"), self.in_names, self.in_rows)]
        *self.in_hop, token = _hop_start(psums, "reduce_hop_start_in")
        return _tie(w_lr, token)

    def in_reduce(self, du_lr):
        psums, lands = _hop_wait(*self.in_hop, self.update(self.ffn, du_lr), "reduce_hop_wait_in")
        idx = _hop_pos()[2].astype(jnp.int32).reshape(1)
        psums = [_hop_add(p, l, idx, "hop_add_" + nm, tr) for p, l, nm, tr in zip(psums, lands, self.in_names, self.in_rows)]
        *flight, token = _chip_start(psums, "reduce_chips_start_in", nrel=2)
        self.inw = dict(tag="in", names=self.in_names, rows=self.in_rows, flight=flight)
        return token


def _local_step(xs, tgt, u, norm1_w, gla_gate_b, attn_sinks, gla_norm_w, norm2_w, fnw, w_main, w_lr, w2p, comm):
    proj =_mm(u, w_main, tb=True, tm=1024, tn=1280, tk=D, name="in_proj")
    plr = _mm(u, w_lr, tb=True, tm=1024, tn=LANE, tk=D, name="in_proj_lr")
    attn_o = _attn_fwd(proj, attn_sinks)
    gla_o, states = _gla_fwd(proj, plr, w2p, gla_gate_b)
    merged = _merge_fwd(attn_o, gla_o, proj, comm.mixed(gla_o, gla_norm_w))
    wo, norm2_w = comm.w_out(merged, norm2_w)
    h1 = _mm(merged, wo, tm=1024, tn=512, tk=D, res=xs, name="out_proj")
    v2 = _rmsnorm_fwd(h1, norm2_w, "norm2_fwd")
    wg_all, wu_all = comm.w_up(v2)
    fa, fb, ff = _ffn_up(v2, wg_all, wu_all)
    wd_all = comm.w_down(ff)
    h2 = _mm(ff, wd_all, tm=1024, tn=1024, tk=FH // 2, res=h1, name="ffn_down")
    dh2, dh2b, d_fnw, loss_part = _loss_head(h2, fnw, tgt)

    da, db = _ffn_dact(dh2b, wd_all, fa, fb)
    Tn = xs.shape[0]
    d_wd = _mm(ff, dh2b, ta=True, tm=512, tn=D, tk=Tn, out_dtype=BF16, name="ffn_dwd")
    d_wg = _mm(da, v2, ta=True, tm=512, tn=D, tk=Tn, out_dtype=BF16, name="ffn_dwg")
    d_wu = _mm(db, v2, ta=True, tm=512, tn=D, tk=Tn, out_dtype=BF16, name="ffn_dwu")
    dv2 = _mm(da, wg_all, tm=1024, tn=1024, tk=FH // 2, after=comm.ffn_grads(d_wg, d_wu, d_wd), name="ffn_dv2_gate")
    dv2 = _mm(db, wu_all, tm=1024, tn=1024, tk=FH // 2, res=dv2, name="ffn_dv2_up")
    norm2_w = comm.ffn_reduce(dv2, norm2_w)
    dh1, dh1b, d_n2 = _rmsnorm_bwd(dv2, h1, norm2_w, dh2, "norm2_bwd")
    dmerged = _mm(dh1b, wo, tb=True, tm=1024, tn=512, tk=D, name="out_proj_dx")
    d_wo = _mm(merged, dh1b, ta=True, tm=1024, tn=512, tk=xs.shape[0], out_dtype=BF16, name="out_proj_dw")
    d_attn, d_gla, d_gates, d_gnw = _merge_bwd(dmerged, attn_o, gla_o, proj, gla_norm_w)
    d_q, d_kv, d_sinks = _attn_bwd(proj, attn_sinks, attn_o, d_attn)
    d_gqk, d_gv, d_plr, d_w2p, d_gb = _gla_bwd(proj, plr, w2p, gla_gate_b, states, d_gla)
    dproj = jnp.concatenate([d_q, d_kv, d_gqk, d_gv, d_gates], axis=1)
    d_wmain = _mm(dproj, u, ta=True, tm=640, tn=D, tk=xs.shape[0], out_dtype=BF16, name="in_proj_dw")
    d_wlr = _mm(d_plr, u, ta=True, tm=LANE, tn=1024, tk=xs.shape[0], out_dtype=BF16, name="in_proj_lr_dw")
    du_lr = _mm(d_plr, comm.in_grads(d_wmain, d_wlr, d_wo, w_lr), tm=1024, tn=1024, tk=LANE, name="in_proj_lr_dx")
    du = _mm(dproj, w_main, tm=1024, tn=1024, tk=2560, res=du_lr, after=comm.in_reduce(du_lr), name="in_proj_dx")
    dx, _, d_n1 = _rmsnorm_bwd(du, xs, norm1_w, dh1, "norm1_bwd")
    return dx, loss_part, d_w2p, d_gb, d_sinks, d_gnw, d_n1, d_n2, d_fnw


def kernel(x, norm1_w, w_in, gla_gate_w2, gla_gate_b, attn_sinks, gla_norm_w, w_out, norm2_w, w_ffn_gate, w_ffn_up, w_ffn_down, final_norm_w, loss_target, m_norm1_w, m_w_in, m_gla_gate_w2, m_gla_gate_b, m_attn_sinks, m_gla_norm_w, m_w_out, m_norm2_w, m_w_ffn_gate, m_w_ffn_up, m_w_ffn_down, m_final_norm_w, v_norm1_w, v_w_in, v_gla_gate_w2, v_gla_gate_b, v_attn_sinks, v_gla_norm_w, v_w_out, v_norm2_w, v_w_ffn_gate, v_w_ffn_up, v_w_ffn_down, v_final_norm_w):
    xs, tgt = x[0], loss_target[0]
    fnw = final_norm_w.reshape(1, D)
    c_idx = lax.axis_index("c").astype(jnp.int32).reshape(1)
    dev = 4 * lax.axis_index("x") + 2 * lax.axis_index("y") + lax.axis_index("c")

    chip_idx = (2 * lax.axis_index("x") + lax.axis_index("y")).astype(jnp.int32).reshape(1)

    shift = (WS - WSTEP) * dev
    window = lax.dynamic_update_slice(jnp.zeros((WWIN, D), BF16), jnp.transpose(w_in[0]).astype(BF16), (shift, 0))
    w2_land = lax.dynamic_update_slice(lax.empty((NDEV, RANK, LANE), F32), gla_gate_w2, (dev, 0, 0))
    *sems1, win_srcs, win_lands, tok = _win_start([window, gla_gate_w2[0]], [lax.empty((NDEV, WWIN, D), BF16), w2_land], "gather_in_start")
    tr2 = lambda t: jnp.transpose(t[0])
    rows3 = lambda t: jnp.transpose(t[0] + tok[0, 0]).reshape(WS, D // LANE, LANE)
    rest = [(w + tok[0, 0]).astype(BF16) for w in (w_out[0], tr2(w_ffn_gate), tr2(w_ffn_up), w_ffn_down[0])]
    rest_lands = [lax.dynamic_update_slice(lax.empty((NDEV,) + s.shape, s.dtype), s[None], (dev, 0, 0)) for s in rest]
    win3 = [rows3(t) for t in (w_in, m_w_in, v_w_in)]
    *sems2, win_lands, tok = _win_hand_on(win_lands, sems1[1], rest + rest_lands + win3, "gather_in_hand_on")
    u = _rmsnorm_fwd(xs, _tie(norm1_w, tok), "norm1_fwd")
    *sems3, win_lands, tok = _win_last(win_lands, sems2[1], [u], "gather_in_last")
    comm = _Comm(rest, rest_lands, tok, c_idx)
    win_all, w2_all = _win_wait(win_srcs, win_lands, sems1, sems2, sems3, comm.token, "gather_in_wait")
    w_main, w_lr = _assemble_w_in(win_all, window)
    w2p = jnp.pad(jnp.transpose(w2_all, (1, 0, 2)).reshape(RANK, GH * DK), ((0, LANE - RANK), (0, 0)))

    big = {}

    def update(grp, after):
        psums, parts = _chip_wait(*grp["flight"], after, "reduce_chips_wait_" + grp["tag"])
        for nm, ps, pt, tr in zip(grp["names"], psums, parts, grp["rows"]):
            w, m, v = {"w_in": (w_in, m_w_in, v_w_in), "w_out": (w_out, m_w_out, v_w_out), "w_ffn_gate": (w_ffn_gate, m_w_ffn_gate, v_w_ffn_gate),
                       "w_ffn_up": (w_ffn_up, m_w_ffn_up, v_w_ffn_up), "w_ffn_down": (w_ffn_down, m_w_ffn_down, v_w_ffn_down)}[nm]
            if nm == "w_in":
                g_win = _sum_parts(ps, pt, chip_idx, "sum_w_in", tr, 1024)
                g3 = lax.dynamic_slice(g_win, (shift, 0), (WS, D)).reshape(WS, D // LANE, LANE)
                out3 = (g3,) + tuple(_adamw_rows(*win3, g3, "adamw_w_in", 178))
                big[nm] = [jnp.transpose(t.reshape(WS, D))[None] for t in out3]
            elif nm in ("w_ffn_gate", "w_ffn_up"):
                big[nm] = [jnp.transpose(t)[None] for t in _adamw(tr2(w), tr2(m), tr2(v), ps, pt, chip_idx, "adamw_" + nm, tr)]
            else:
                big[nm] = [t[None] for t in _adamw(w[0], m[0], v[0], ps, pt, chip_idx, "adamw_" + nm, tr)]
            after = big[nm][0]
        return after

    comm.update = update
    dx, loss_part, d_w2p, d_gb, d_sinks, d_gnw, d_n1, d_n2, d_fnw = _local_step(
        xs, tgt, u, norm1_w, gla_gate_b, attn_sinks, gla_norm_w, norm2_w, fnw, w_main, w_lr, w2p, comm)

    pack = jnp.concatenate([_pack_small(d_n1, d_gb, d_sinks, d_gnw, d_n2, d_fnw, loss_part),
                            d_w2p[:RANK].reshape(GW2_ROWS, LANE)], axis=0)
    small = _sum_devices(_gather_small(pack))

    update(comm.inw, dx)
    g_small = small[:SMALL_ROWS]
    sm = _adamw_plain(_pack_small(norm1_w, gla_gate_b, attn_sinks, gla_norm_w, norm2_w, final_norm_w),
                      _pack_small(m_norm1_w, m_gla_gate_b, m_attn_sinks, m_gla_norm_w, m_norm2_w, m_final_norm_w),
                      _pack_small(v_norm1_w, v_gla_gate_b, v_attn_sinks, v_gla_norm_w, v_norm2_w, v_final_norm_w), g_small, "adamw_small")
    g_w2 = lax.dynamic_slice_in_dim(small[SMALL_ROWS:].reshape(RANK, GH * DK), dev * LANE, LANE, axis=1)
    w2 = [g_w2[None]] + [t[None] for t in _adamw_plain(gla_gate_w2[0], m_gla_gate_w2[0], v_gla_gate_w2[0], g_w2, "adamw_w2")]
    loss = g_small.reshape(-1)[S_LOSS]

    sg, sd, sm2, sv2 = [_unpack_small(t) for t in (g_small,) + tuple(sm)]

    def group(i, s):
        return (s[0], big["w_in"][i], w2[i], s[1], s[2], s[3], big["w_out"][i], s[4], big["w_ffn_gate"][i], big["w_ffn_up"][i],
                big["w_ffn_down"][i], s[5])

    return (loss, dx[None], *group(0, sg), *group(1, sd), *group(2, sm2), *group(3, sv2))
```

```python
import functools

import jax
import jax.numpy as jnp
from jax import lax
from jax.experimental import pallas as pl
from jax.experimental.pallas import tpu as pltpu

F32, BF16 = jnp.float32, jnp.bfloat16
HIGHEST = lax.Precision.HIGHEST

D = 2048
HD, NQ, NKV, GRP, WIN = 64, 32, 4, 8, 128
GH, DK, DV, RANK, GC = 4, 256, 512, 16, 64
FH, NDEV = 5632, 8
FS = FH // NDEV
DIN = 12816
WS = DIN // NDEV
EPS = 1e-6
MASKV = -1e30
LANE = 128

C_AQ, C_AK, C_AV, C_GQ, C_GK, C_GV, C_GR, C_GA, C_GB, NMAIN = 0, 2048, 2304, 2560, 3584, 4608, 6656, 8704, 10752, 12800
C_LR = 6656
WSTEP, WWIN = 1600, 1616

LR, B1, B2, AEPS, WD, STEP = 0.001, 0.9, 0.999, 1e-08, 0.01, 10

S_N1, S_GB, S_SK, S_GN, S_N2, S_FN, S_LOSS, SMALL_N = 0, 2048, 3072, 3104, 3616, 5664, 7712, 8192
SMALL_ROWS = SMALL_N // LANE
GW2_ROWS = RANK * GH * DK // LANE
PACK_ROWS = SMALL_ROWS + GW2_ROWS

MESH = pl.DeviceIdType.MESH


def _dot(a, b, ta=False, tb=False, prec=None):
    dn = (((0,) if ta else (1,), (1,) if tb else (0,)), ((), ()))
    return lax.dot_general(a, b, dn, preferred_element_type=F32, precision=prec)


def _sigmoid(x):
    return 1.0 / (1.0 + jnp.exp(-x))


VMEM_LIMIT = 56 * 1024 * 1024


def _cp(*sem):
    return pltpu.CompilerParams(dimension_semantics=sem, vmem_limit_bytes=VMEM_LIMIT)


def _mm(a, b, *, ta=False, tb=False, tm, tn, tk, out_dtype=F32, res=None, after=None, name):
    M, K = (a.shape[1], a.shape[0]) if ta else a.shape
    N = b.shape[0] if tb else b.shape[1]
    tm, tn, tk = min(tm, M), min(tn, N), min(tk, K)
    nk = K // tk
    assert M % tm == 0 and N % tn == 0 and K % tk == 0
    a_spec = pl.BlockSpec((tk, tm), lambda i, j, k: (k, i)) if ta else pl.BlockSpec((tm, tk), lambda i, j, k: (i, k))
    b_spec = pl.BlockSpec((tn, tk), lambda i, j, k: (j, k)) if tb else pl.BlockSpec((tk, tn), lambda i, j, k: (k, j))
    o_spec = pl.BlockSpec((tm, tn), lambda i, j, k: (i, j))
    has_res = res is not None

    def body(*refs):
        a_ref, b_ref = refs[0], refs[1]
        r_ref = refs[2] if has_res else None
        o_ref = refs[2 + has_res + (after is not None)]
        p = _dot(a_ref[...].astype(BF16), b_ref[...].astype(BF16), ta, tb)
        if nk == 1:
            if has_res:
                p = p + r_ref[...]
            o_ref[...] = p.astype(out_dtype)
        else:
            acc = refs[-1]
            k = pl.program_id(2)

            @pl.when(k == 0)
            def _():
                acc[...] = (p + r_ref[...]) if has_res else p

            @pl.when(k > 0)
            def _():
                acc[...] += p

            @pl.when(k == nk - 1)
            def _():
                o_ref[...] = acc[...].astype(out_dtype)

    return pl.pallas_call(
        body, name=name,
        out_shape=jax.ShapeDtypeStruct((M, N), out_dtype),
        grid=(M // tm, N // tn, nk),
        in_specs=[a_spec, b_spec] + ([o_spec] if has_res else []) + ([pl.BlockSpec(memory_space=pl.ANY)] if after is not None else []),
        out_specs=o_spec,
        scratch_shapes=[pltpu.VMEM((tm, tn), F32)] if nk > 1 else [],
        compiler_params=_cp("parallel", "parallel", "arbitrary"),
    )(*((a, b) + ((res,) if has_res else ()) + ((after,) if after is not None else ())))


def _rmsnorm_fwd(x, w, name, tm=256):
    Tn = x.shape[0]

    def body(x_ref, w_ref, o_ref):
        xv = x_ref[...]
        r = lax.rsqrt(jnp.mean(xv * xv, axis=1, keepdims=True) + EPS)
        o_ref[...] = (xv * r * w_ref[...]).astype(BF16)

    return pl.pallas_call(
        body, name=name, out_shape=jax.ShapeDtypeStruct((Tn, D), BF16), grid=(Tn // tm,),
        in_specs=[pl.BlockSpec((tm, D), lambda i: (i, 0)), pl.BlockSpec((1, D), lambda i: (0, 0))],
        out_specs=pl.BlockSpec((tm, D), lambda i: (i, 0)), compiler_params=_cp("parallel"),
    )(x, w)


def _rmsnorm_bwd(dy, h, w, res, name, tm=256):
    Tn = h.shape[0]

    def body(dy_ref, h_ref, w_ref, res_ref, dh_ref, dhb_ref, dw_ref):
        hv, dyv = h_ref[...], dy_ref[...]
        r = lax.rsqrt(jnp.mean(hv * hv, axis=1, keepdims=True) + EPS)
        g = dyv * w_ref[...]
        dh = res_ref[...] + r * g - hv * (r * r * r * jnp.mean(g * hv, axis=1, keepdims=True))
        dh_ref[...] = dh
        dhb_ref[...] = dh.astype(BF16)
        part = jnp.sum(dyv * hv * r, axis=0, keepdims=True)

        @pl.when(pl.program_id(0) == 0)
        def _():
            dw_ref[...] = part

        @pl.when(pl.program_id(0) > 0)
        def _():
            dw_ref[...] += part

    row = pl.BlockSpec((tm, D), lambda i: (i, 0))
    vec = pl.BlockSpec((1, D), lambda i: (0, 0))
    return pl.pallas_call(
        body, name=name,
        out_shape=(jax.ShapeDtypeStruct((Tn, D), F32), jax.ShapeDtypeStruct((Tn, D), BF16), jax.ShapeDtypeStruct((1, D), F32)),
        grid=(Tn // tm,), in_specs=[row, row, vec, row], out_specs=(row, row, vec), compiler_params=_cp("arbitrary"),
    )(dy, h, w, res)


def _loss_head(h2, wf, tgt, name="loss_head", tm=256):
    Tn = h2.shape[0]

    def body(h_ref, w_ref, t_ref, dh_ref, dhb_ref, dw_ref, loss_ref):
        hv, wv = h_ref[...], w_ref[...]
        r = lax.rsqrt(jnp.mean(hv * hv, axis=1, keepdims=True) + EPS)
        hn = hv * r
        e = hn * wv - t_ref[...]
        dy = e * (1.0 / D)
        g = dy * wv
        dh = r * g - hv * (r * r * r * jnp.mean(g * hv, axis=1, keepdims=True))
        dh_ref[...] = dh
        dhb_ref[...] = dh.astype(BF16)
        part = jnp.sum(dy * hn, axis=0, keepdims=True)
        lpart = (0.5 / D) * jnp.sum(jnp.sum(e * e, axis=1, keepdims=True), axis=0, keepdims=True)

        @pl.when(pl.program_id(0) == 0)
        def _():
            dw_ref[...] = part
            loss_ref[...] = lpart

        @pl.when(pl.program_id(0) > 0)
        def _():
            dw_ref[...] += part
            loss_ref[...] += lpart

    row = pl.BlockSpec((tm, D), lambda i: (i, 0))
    vec = pl.BlockSpec((1, D), lambda i: (0, 0))
    one = pl.BlockSpec((1, 1), lambda i: (0, 0))
    return pl.pallas_call(
        body, name=name,
        out_shape=(jax.ShapeDtypeStruct((Tn, D), F32), jax.ShapeDtypeStruct((Tn, D), BF16), jax.ShapeDtypeStruct((1, D), F32),
                   jax.ShapeDtypeStruct((1, 1), F32)),
        grid=(Tn // tm,), in_specs=[row, vec, row], out_specs=(row, row, vec, one), compiler_params=_cp("arbitrary"),
    )(h2, wf, tgt)


def _attn_mask(n):
    qi = lax.broadcasted_iota(jnp.int32, (NKV, GRP * WIN, 2 * WIN), 1) % WIN
    ki = lax.broadcasted_iota(jnp.int32, (NKV, GRP * WIN, 2 * WIN), 2)
    rel = qi + WIN - ki
    return (rel >= 0) & (rel < WIN) & ((n > 0) | (ki >= WIN))


def _kv_heads(prev_ref, cur_ref):
    return jnp.stack([jnp.concatenate([prev_ref[:, h * HD:(h + 1) * HD], cur_ref[:, h * HD:(h + 1) * HD]], axis=0) for h in range(NKV)])


def _q_heads(ref):
    return jnp.stack([jnp.concatenate([ref[:, (h * GRP + g) * HD:(h * GRP + g + 1) * HD] for g in range(GRP)], axis=0) for h in range(NKV)])


def _attn_probs(q_ref, kc_ref, kp_ref, sink_ref, mask):
    kk = _kv_heads(kp_ref, kc_ref).astype(BF16)
    qs = _q_heads(q_ref).astype(BF16)
    s = jnp.einsum('hqd,hkd->hqk', qs, kk, preferred_element_type=F32) * (HD ** -0.5)
    s = jnp.where(mask, s, MASKV)
    sink = jnp.stack([jnp.concatenate([jnp.full((WIN, 1), sink_ref[0, h * GRP + g], F32) for g in range(GRP)], axis=0) for h in range(NKV)])
    m = jnp.maximum(jnp.max(s, axis=2, keepdims=True), sink)
    e = jnp.exp(s - m)
    es = jnp.exp(sink - m)
    inv = 1.0 / (jnp.sum(e, axis=2, keepdims=True) + es)
    return e * inv, es * inv, qs, kk


def _attn_specs(nb, last):
    cur = lambda n: jnp.minimum(n, last)
    prev = lambda n: jnp.maximum(jnp.minimum(n, last) - 1, 0)
    return [
        pl.BlockSpec((WIN, NQ * HD), lambda n: (cur(n), C_AQ // (NQ * HD))),
        pl.BlockSpec((WIN, NKV * HD), lambda n: (cur(n), C_AK // (NKV * HD))),
        pl.BlockSpec((WIN, NKV * HD), lambda n: (prev(n), C_AK // (NKV * HD))),
        pl.BlockSpec((WIN, NKV * HD), lambda n: (cur(n), C_AV // (NKV * HD))),
        pl.BlockSpec((WIN, NKV * HD), lambda n: (prev(n), C_AV // (NKV * HD))),
    ]


def _attn_fwd(proj, sinks, name="attn_fwd"):
    Tn = proj.shape[0]
    nb = Tn // WIN

    def body(q_ref, kc_ref, kp_ref, vc_ref, vp_ref, sink_ref, o_ref):
        p, _, _, _ = _attn_probs(q_ref, kc_ref, kp_ref, sink_ref, _attn_mask(pl.program_id(0)))
        o = jnp.einsum('hqk,hkd->hqd', p.astype(BF16), _kv_heads(vp_ref, vc_ref).astype(BF16), preferred_element_type=F32)
        for h in range(NKV):
            for g in range(GRP):
                o_ref[:, (h * GRP + g) * HD:(h * GRP + g + 1) * HD] = o[h, g * WIN:(g + 1) * WIN, :]

    return pl.pallas_call(
        body, name=name, out_shape=jax.ShapeDtypeStruct((Tn, D), F32), grid=(nb,),
        in_specs=_attn_specs(nb, nb - 1) + [pl.BlockSpec(memory_space=pltpu.SMEM)],
        out_specs=pl.BlockSpec((WIN, D), lambda n: (n, 0)), compiler_params=_cp("parallel"),
    )(proj, proj, proj, proj, proj, sinks)


def _attn_bwd(proj, sinks, o, do, name="attn_bwd"):
    Tn = proj.shape[0]
    nb = Tn // WIN
    KW = NKV * HD

    def body(q_ref, kc_ref, kp_ref, vc_ref, vp_ref, o_ref, do_ref, sink_ref, dq_ref, dkv_ref, dsk_ref, carry, cur):
        n = pl.program_id(0)

        @pl.when(n == 0)
        def _():
            carry[...] = jnp.zeros_like(carry)
            dsk_ref[...] = jnp.zeros_like(dsk_ref)

        @pl.when(n < nb)
        def _():
            p, ps, qs, kk = _attn_probs(q_ref, kc_ref, kp_ref, sink_ref, _attn_mask(n))
            vv = _kv_heads(vp_ref, vc_ref).astype(BF16)
            dos = _q_heads(do_ref)
            delta = jnp.sum(dos * _q_heads(o_ref), axis=2, keepdims=True)
            dosb = dos.astype(BF16)
            dp = jnp.einsum('hqd,hkd->hqk', dosb, vv, preferred_element_type=F32)
            ds = (p * (dp - delta) * (HD ** -0.5)).astype(BF16)
            dq = jnp.einsum('hqk,hkd->hqd', ds, kk, preferred_element_type=F32)
            dkk = jnp.einsum('hqk,hqd->hkd', ds, qs, preferred_element_type=F32)
            dvv = jnp.einsum('hqk,hqd->hkd', p.astype(BF16), dosb, preferred_element_type=F32)
            dsk = ps * delta
            for h in range(NKV):
                for g in range(GRP):
                    i = h * GRP + g
                    dq_ref[:, i * HD:(i + 1) * HD] = dq[h, g * WIN:(g + 1) * WIN, :].astype(BF16)
                    dsk_ref[:, i:i + 1] -= jnp.sum(dsk[h, g * WIN:(g + 1) * WIN, :], axis=0, keepdims=True)
                dkv_ref[:, h * HD:(h + 1) * HD] = (carry[:, h * HD:(h + 1) * HD] + dkk[h, :WIN, :]).astype(BF16)
                dkv_ref[:, KW + h * HD:KW + (h + 1) * HD] = (carry[:, KW + h * HD:KW + (h + 1) * HD] + dvv[h, :WIN, :]).astype(BF16)
                cur[:, h * HD:(h + 1) * HD] = dkk[h, WIN:, :]
                cur[:, KW + h * HD:KW + (h + 1) * HD] = dvv[h, WIN:, :]
            carry[...] = cur[...]

        @pl.when(n == nb)
        def _():
            dkv_ref[...] = carry[...].astype(BF16)

    last = nb - 1
    row = pl.BlockSpec((WIN, D), lambda n: (jnp.minimum(n, last), 0))
    return pl.pallas_call(
        body, name=name,
        out_shape=(jax.ShapeDtypeStruct((Tn, D), BF16), jax.ShapeDtypeStruct((Tn, 2 * KW), BF16), jax.ShapeDtypeStruct((1, NQ), F32)),
        grid=(nb + 1,),
        in_specs=_attn_specs(nb, last) + [row, row, pl.BlockSpec(memory_space=pltpu.SMEM)],
        out_specs=(row, pl.BlockSpec((WIN, 2 * KW), lambda n: (jnp.maximum(n - 1, 0), 0)), pl.BlockSpec((1, NQ), lambda n: (0, 0))),
        scratch_shapes=[pltpu.VMEM((WIN, 2 * KW), F32), pltpu.VMEM((WIN, 2 * KW), F32)],
        compiler_params=_cp("arbitrary"),
    )(proj, proj, proj, proj, proj, o, do, sinks)


def _tri(lower):
    r = lax.broadcasted_iota(jnp.int32, (GC, GC), 0)
    c = lax.broadcasted_iota(jnp.int32, (GC, GC), 1)
    return r >= c if lower else r <= c


def _per_head(a):
    return jnp.stack([a[:, h * DK:(h + 1) * DK] for h in range(GH)])


def _all_heads(a):
    return jnp.concatenate([a[h] for h in range(GH)], axis=1)


def _gla_gates(lr, w2_ref, gb_ref):
    logit = _dot(lr, w2_ref[...].astype(BF16)) + gb_ref[...]
    la = (jnp.minimum(logit, 0.0) - jnp.log(1.0 + jnp.exp(-jnp.abs(logit)))) * (1.0 / 16.0)
    g = _dot(_tri(True).astype(F32), la, prec=HIGHEST)
    return logit, g


def _bmm(spec, a, b):
    return jnp.einsum(spec, a, b, preferred_element_type=F32)


def _gla_specs(nc, rev):
    idx = (lambda n: nc - 1 - n) if rev else (lambda n: n)
    half = 2 * DK
    return (
        [pl.BlockSpec((GC, half), lambda n, j=j: (idx(n), C_GQ // half + j)) for j in range(2)]
        + [pl.BlockSpec((GC, half), lambda n, j=j: (idx(n), C_GK // half + j)) for j in range(2)]
        + [pl.BlockSpec((GC, DV), lambda n, h=h: (idx(n), C_GV // DV + h)) for h in range(GH)]
        + [pl.BlockSpec((GC, LANE), lambda n: (idx(n), 0)), pl.BlockSpec((LANE, GH * DK), lambda n: (0, 0)),
           pl.BlockSpec((1, GH * DK), lambda n: (0, 0))])


def _gla_heads(refs):
    return (lambda h: refs[h // 2][:, (h % 2) * DK:(h % 2 + 1) * DK], lambda h: refs[2 + h // 2][:, (h % 2) * DK:(h % 2 + 1) * DK],
            lambda h: refs[4 + h][...])


def _gla_fwd(proj, plr, w2p, gb, name="gla_fwd"):
    Tn = proj.shape[0]
    nc = Tn // GC

    def body(*refs):
        qh, kh, vh = _gla_heads(refs)
        lr_ref, w2_ref, gb_ref, o_ref, st_ref, S = refs[8:]

        @pl.when(pl.program_id(0) == 0)
        def _():
            S[...] = jnp.zeros_like(S)

        heads = lambda f: jnp.stack([f(h) for h in range(GH)])
        _, g_all = _gla_gates(lr_ref[...].astype(BF16), w2_ref, gb_ref)
        g = _per_head(g_all)
        gl = g[:, GC - 1:GC, :]
        k = heads(kh)
        v = heads(vh).astype(BF16)
        qd = (heads(qh) * (DK ** -0.5) * jnp.exp(g)).astype(BF16)
        ki = (k * jnp.exp(-g)).astype(BF16)
        ke = (k * jnp.exp(gl - g)).astype(BF16)
        att = jnp.where(_tri(True)[None], _bmm('hid,hjd->hij', qd, ki), 0.0).astype(BF16)
        sp = S[...]
        st_ref[0] = sp
        o = _bmm('hij,hjv->hiv', att, v) + _bmm('hid,hvd->hiv', qd, sp.astype(BF16))
        for h in range(GH):
            o_ref[:, h * DV:(h + 1) * DV] = o[h]
        S[...] = sp * jnp.exp(gl) + _bmm('hjv,hjd->hvd', v, ke)

    return pl.pallas_call(
        body, name=name,
        out_shape=(jax.ShapeDtypeStruct((Tn, GH * DV), F32), jax.ShapeDtypeStruct((nc, GH, DV, DK), F32)),
        grid=(nc,), in_specs=_gla_specs(nc, False),
        out_specs=(pl.BlockSpec((GC, GH * DV), lambda n: (n, 0)), pl.BlockSpec((1, GH, DV, DK), lambda n: (n, 0, 0, 0))),
        scratch_shapes=[pltpu.VMEM((GH, DV, DK), F32)], compiler_params=_cp("arbitrary"),
    )(*([proj] * 8), plr, w2p, gb)


def _gla_bwd(proj, plr, w2p, gb, states, do, name="gla_bwd"):
    Tn = proj.shape[0]
    nc = Tn // GC

    def body(*refs):
        qh, kh, vh = _gla_heads(refs)
        lr_ref, w2_ref, gb_ref, st_ref, do_ref, dqk_ref, dv_ref, dlr_ref, dw2_ref, dgb_ref, dS = refs[8:]

        @pl.when(pl.program_id(0) == 0)
        def _():
            dS[...] = jnp.zeros_like(dS)
            dw2_ref[...] = jnp.zeros_like(dw2_ref)
            dgb_ref[...] = jnp.zeros_like(dgb_ref)

        heads = lambda f: jnp.stack([f(h) for h in range(GH)])
        lr = lr_ref[...].astype(BF16)
        causal = _tri(True)[None]
        last_row = lax.broadcasted_iota(jnp.int32, (GH, GC, DK), 1) == GC - 1
        logit, g_all = _gla_gates(lr, w2_ref, gb_ref)
        g = _per_head(g_all)
        gl = g[:, GC - 1:GC, :]
        egl = jnp.exp(gl)
        eg, eng, ege = jnp.exp(g), jnp.exp(-g), jnp.exp(gl - g)
        k = heads(kh)
        v = heads(vh).astype(BF16)
        dob = heads(lambda h: do_ref[:, h * DV:(h + 1) * DV]).astype(BF16)
        qd = heads(qh) * (DK ** -0.5) * eg
        ki = k * eng
        ke = k * ege
        qdb, kib, keb = qd.astype(BF16), ki.astype(BF16), ke.astype(BF16)
        att = jnp.where(causal, _bmm('hid,hjd->hij', qdb, kib), 0.0).astype(BF16)
        datt = jnp.where(causal, _bmm('hiv,hjv->hij', dob, v), 0.0).astype(BF16)
        sp = st_ref[0]
        dsn = dS[...]
        dsnb = dsn.astype(BF16)
        dv = (_bmm('hij,hiv->hjv', att, dob) + _bmm('hjd,hvd->hjv', keb, dsnb)).astype(BF16)
        dqd = _bmm('hij,hjd->hid', datt, kib) + _bmm('hiv,hvd->hid', dob, sp.astype(BF16))
        dki = _bmm('hij,hid->hjd', datt, qdb)
        dke = _bmm('hjv,hvd->hjd', v, dsnb)
        ddec = jnp.sum(dsn * sp, axis=1, keepdims=True)
        dS[...] = dsn * egl + _bmm('hiv,hid->hvd', dob, qdb)
        dke_ke = dke * ke
        dgl = jnp.sum(dke_ke, axis=1, keepdims=True) + ddec * egl
        dg = dqd * qd - dki * ki - dke_ke + jnp.where(last_row, dgl, 0.0)
        dq = (dqd * ((DK ** -0.5) * eg)).astype(BF16)
        dk = (dki * eng + dke * ege).astype(BF16)
        for h in range(GH):
            dv_ref[:, h * DV:(h + 1) * DV] = dv[h]
            dqk_ref[:, h * DK:(h + 1) * DK] = dq[h]
            dqk_ref[:, GH * DK + h * DK:GH * DK + (h + 1) * DK] = dk[h]
        dla = _dot(_tri(False).astype(F32), _all_heads(dg), prec=HIGHEST)
        dlogit = dla * (1.0 / 16.0) * _sigmoid(-logit)
        dlb = dlogit.astype(BF16)
        dlr_ref[...] = _dot(dlb, w2_ref[...].astype(BF16), tb=True).astype(BF16)
        dw2_ref[...] += _dot(lr, dlb, ta=True)
        dgb_ref[...] += jnp.sum(dlogit, axis=0, keepdims=True)

    rev = lambda n: nc - 1 - n
    row = pl.BlockSpec((GC, GH * DV), lambda n: (rev(n), 0))
    return pl.pallas_call(
        body, name=name,
        out_shape=(jax.ShapeDtypeStruct((Tn, 2 * GH * DK), BF16), jax.ShapeDtypeStruct((Tn, GH * DV), BF16),
                   jax.ShapeDtypeStruct((Tn, LANE), BF16), jax.ShapeDtypeStruct((LANE, GH * DK), F32),
                   jax.ShapeDtypeStruct((1, GH * DK), F32)),
        grid=(nc,),
        in_specs=_gla_specs(nc, True) + [pl.BlockSpec((1, GH, DV, DK), lambda n: (rev(n), 0, 0, 0)), row],
        out_specs=(row, row, pl.BlockSpec((GC, LANE), lambda n: (rev(n), 0)), pl.BlockSpec((LANE, GH * DK), lambda n: (0, 0)),
                   pl.BlockSpec((1, GH * DK), lambda n: (0, 0))),
        scratch_shapes=[pltpu.VMEM((GH, DV, DK), F32)], compiler_params=_cp("arbitrary"),
    )(*([proj] * 8), plr, w2p, gb, states, do)


def _merge_specs(tm):
    row = pl.BlockSpec((tm, D), lambda i: (i, 0))
    gates = [pl.BlockSpec((tm, DV), lambda i, j=c // DV + h: (i, j)) for c in (C_GR, C_GA, C_GB) for h in range(GH)]
    return row, gates, pl.BlockSpec((1, DV), lambda i: (0, 0))


def _merge_fwd(a, go, proj, gnw, name="merge_fwd", tm=256):
    Tn = a.shape[0]

    def body(a_ref, go_ref, *rest):
        gates, w_ref, m_ref = rest[:3 * GH], rest[3 * GH], rest[3 * GH + 1]
        for h in range(GH):
            sl = slice(h * DV, (h + 1) * DV)
            gov = go_ref[:, sl]
            r = lax.rsqrt(jnp.mean(gov * gov, axis=1, keepdims=True) + EPS)
            gr = gates[h][...]
            g2 = gov * r * w_ref[...] * (gr * _sigmoid(gr))
            m_ref[:, sl] = (_sigmoid(gates[GH + h][...]) * a_ref[:, sl] + _sigmoid(gates[2 * GH + h][...]) * g2).astype(BF16)

    row, gates, vec = _merge_specs(tm)
    return pl.pallas_call(
        body, name=name, out_shape=jax.ShapeDtypeStruct((Tn, D), BF16), grid=(Tn // tm,),
        in_specs=[row, row] + gates + [vec], out_specs=row, compiler_params=_cp("parallel"),
    )(a, go, *([proj] * (3 * GH)), gnw)


def _merge_bwd(dm, a, go, proj, gnw, name="merge_bwd", tm=256):
    Tn = a.shape[0]

    def body(dm_ref, a_ref, go_ref, *rest):
        gates = rest[:3 * GH]
        w_ref, da_ref, dgo_ref, dg_ref, dw_ref = rest[3 * GH:]
        wv = w_ref[...]
        dw = jnp.zeros((1, DV), F32)
        for h in range(GH):
            sl = slice(h * DV, (h + 1) * DV)
            dmv, av, gov, gr = dm_ref[:, sl], a_ref[:, sl], go_ref[:, sl], gates[h][...]
            sa, sb, sg = _sigmoid(gates[GH + h][...]), _sigmoid(gates[2 * GH + h][...]), _sigmoid(gr)
            r = lax.rsqrt(jnp.mean(gov * gov, axis=1, keepdims=True) + EPS)
            gn0 = gov * r
            gn = gn0 * wv
            silu = gr * sg
            dg2 = dmv * sb
            da_ref[:, sl] = dmv * sa
            dg_ref[:, D + h * DV:D + (h + 1) * DV] = (dmv * av * sa * (1.0 - sa)).astype(BF16)
            dg_ref[:, 2 * D + h * DV:2 * D + (h + 1) * DV] = (dg2 * gn * silu * (1.0 - sb)).astype(BF16)
            dg_ref[:, sl] = (dg2 * gn * (sg * (1.0 + gr * (1.0 - sg)))).astype(BF16)
            dgn = dg2 * silu
            dw = dw + jnp.sum(dgn * gn0, axis=0, keepdims=True)
            gg = dgn * wv
            dgo_ref[:, sl] = r * gg - gov * (r * r * r * jnp.mean(gg * gov, axis=1, keepdims=True))

        @pl.when(pl.program_id(0) == 0)
        def _():
            dw_ref[...] = dw

        @pl.when(pl.program_id(0) > 0)
        def _():
            dw_ref[...] += dw

    row, gates, vec = _merge_specs(tm)
    return pl.pallas_call(
        body, name=name,
        out_shape=(jax.ShapeDtypeStruct((Tn, D), F32), jax.ShapeDtypeStruct((Tn, D), F32), jax.ShapeDtypeStruct((Tn, 3 * D), BF16),
                   jax.ShapeDtypeStruct((1, DV), F32)),
        grid=(Tn // tm,), in_specs=[row, row, row] + gates + [vec],
        out_specs=(row, row, pl.BlockSpec((tm, 3 * D), lambda i: (i, 0)), vec), compiler_params=_cp("arbitrary"),
    )(dm, a, go, *([proj] * (3 * GH)), gnw)


def _ffn_up(v2, wgt, wut, name="ffn_up", tm=1024, tn=512):
    Tn = v2.shape[0]
    tm = min(tm, Tn)

    def body(v_ref, wg_ref, wu_ref, a_ref, b_ref, ff_ref):
        vv = v_ref[...]
        a = _dot(vv, wg_ref[...], tb=True)
        b = _dot(vv, wu_ref[...], tb=True)
        a_ref[...] = a.astype(BF16)
        b_ref[...] = b.astype(BF16)
        ff_ref[...] = (a * _sigmoid(a) * b).astype(BF16)

    w = pl.BlockSpec((tn, D), lambda j, i: (j, 0))
    act = pl.BlockSpec((tm, tn), lambda j, i: (i, j))
    return pl.pallas_call(
        body, name=name,
        out_shape=(jax.ShapeDtypeStruct((Tn, FH), BF16), jax.ShapeDtypeStruct((Tn, FH), BF16), jax.ShapeDtypeStruct((Tn, FH), BF16)),
        grid=(FH // tn, Tn // tm), in_specs=[pl.BlockSpec((tm, D), lambda j, i: (i, 0)), w, w], out_specs=(act, act, act),
        compiler_params=_cp("parallel", "parallel"),
    )(v2, wgt, wut)


def _ffn_dact(dh2b, wd, a, b, name="ffn_dact", tm=1024, tn=512):
    Tn = dh2b.shape[0]
    tm = min(tm, Tn)

    def body(d_ref, w_ref, a_ref, b_ref, da_ref, db_ref):
        dff = _dot(d_ref[...], w_ref[...], tb=True)
        av = a_ref[...].astype(F32)
        sg = _sigmoid(av)
        da_ref[...] = (dff * b_ref[...].astype(F32) * (sg * (1.0 + av * (1.0 - sg)))).astype(BF16)
        db_ref[...] = (dff * (av * sg)).astype(BF16)

    act = pl.BlockSpec((tm, tn), lambda j, i: (i, j))
    return pl.pallas_call(
        body, name=name,
        out_shape=(jax.ShapeDtypeStruct((Tn, FH), BF16), jax.ShapeDtypeStruct((Tn, FH), BF16)),
        grid=(FH // tn, Tn // tm),
        in_specs=[pl.BlockSpec((tm, D), lambda j, i: (i, 0)), pl.BlockSpec((tn, D), lambda j, i: (j, 0)), act, act],
        out_specs=(act, act), compiler_params=_cp("parallel", "parallel"),
    )(dh2b, wd, a, b)


def _adam_math(w, g, m, v):
    m2 = B1 * m + (1.0 - B1) * g
    v2 = B2 * v + (1.0 - B2) * (g * g)
    mh = m2 / (1.0 - B1 ** STEP)
    vh = v2 / (1.0 - B2 ** STEP)
    return -LR * (mh / (jnp.sqrt(vh) + AEPS) + WD * w), m2, v2


def _sum_blocks(o_ref, p_ref):
    g = o_ref[...].astype(F32)
    for j in range(p_ref.shape[0]):
        g = g + p_ref[j].astype(F32)
    return g


def _adamw(w, m, v, psums, parts, chip_idx, name, tr):
    R, C = w.shape

    def body(s_ref, w_ref, m_ref, v_ref, o_ref, p_ref, g_ref, d_ref, m2_ref, v2_ref):
        g = _sum_blocks(o_ref, p_ref)
        d, m2, v2 = _adam_math(w_ref[...], g, m_ref[...], v_ref[...])
        g_ref[...] = g
        d_ref[...] = d
        m2_ref[...] = m2
        v2_ref[...] = v2

    blk = pl.BlockSpec((tr, C), lambda i, s: (i, 0))
    out = jax.ShapeDtypeStruct((R, C), F32)
    grid_spec = pltpu.PrefetchScalarGridSpec(
        num_scalar_prefetch=1, grid=(R // tr,),
        in_specs=[blk, blk, blk, pl.BlockSpec((None, tr, C), lambda i, s: (s[0], i, 0)),
                  pl.BlockSpec((parts.shape[0], tr, C), lambda i, s: (0, i, 0))],
        out_specs=(blk, blk, blk, blk),
    )
    return pl.pallas_call(body, name=name, out_shape=(out, out, out, out), grid_spec=grid_spec, compiler_params=_cp("parallel"),
                          )(chip_idx, w, m, v, psums, parts)


def _adamw_rows(w, m, v, g, name, tr):
    R = w.shape[0]

    def body(w_ref, m_ref, v_ref, g_ref, d_ref, m2_ref, v2_ref):
        d, m2, v2 = _adam_math(w_ref[...], g_ref[...], m_ref[...], v_ref[...])
        d_ref[...] = d
        m2_ref[...] = m2
        v2_ref[...] = v2

    blk = pl.BlockSpec((tr,) + w.shape[1:], lambda i: (i, 0, 0))
    out = jax.ShapeDtypeStruct(w.shape, F32)
    return pl.pallas_call(body, name=name, out_shape=(out, out, out), grid=(R // tr,), in_specs=[blk] * 4, out_specs=(blk, blk, blk),
                          compiler_params=_cp("parallel"))(w, m, v, g)


def _sum_parts(psums, parts, chip_idx, name, tr, tc):
    _, R, C = psums.shape

    def body(s_ref, o_ref, p_ref, g_ref):
        g_ref[...] = _sum_blocks(o_ref, p_ref)

    grid_spec = pltpu.PrefetchScalarGridSpec(
        num_scalar_prefetch=1, grid=(R // tr, C // tc),
        in_specs=[pl.BlockSpec((None, tr, tc), lambda i, j, s: (s[0], i, j)),
                  pl.BlockSpec((parts.shape[0], tr, tc), lambda i, j, s: (0, i, j))],
        out_specs=pl.BlockSpec((tr, tc), lambda i, j, s: (i, j)),
    )
    return pl.pallas_call(body, name=name, out_shape=jax.ShapeDtypeStruct((R, C), F32), grid_spec=grid_spec,
                          compiler_params=_cp("parallel", "parallel"))(chip_idx, psums, parts)


def _adamw_plain(w, m, v, g, name):
    def body(w_ref, m_ref, v_ref, g_ref, d_ref, m2_ref, v2_ref):
        d, m2, v2 = _adam_math(w_ref[...], g_ref[...], m_ref[...], v_ref[...])
        d_ref[...] = d
        m2_ref[...] = m2
        v2_ref[...] = v2

    out = jax.ShapeDtypeStruct(w.shape, F32)
    return pl.pallas_call(body, name=name, out_shape=(out, out, out))(w, m, v, g)


def _sum_devices(pack_all, name="sum_small"):
    def body(p_ref, o_ref):
        s = p_ref[0]
        for k in range(1, NDEV):
            s = s + p_ref[k]
        o_ref[...] = s

    return pl.pallas_call(body, name=name, out_shape=jax.ShapeDtypeStruct(pack_all.shape[1:], F32))(pack_all)


def _pair_add(g5, recv, c_idx, name, tr):
    _, _, R, C = g5.shape

    def body(c_ref, g_ref, r_ref, o_ref):
        o_ref[...] = (g_ref[...].astype(F32) + r_ref[...].astype(F32)).astype(BF16)

    grid_spec = pltpu.PrefetchScalarGridSpec(
        num_scalar_prefetch=1, grid=(4, R // tr),
        in_specs=[pl.BlockSpec((None, None, tr, C), lambda q, i, c: (q, c[0], i, 0)), pl.BlockSpec((None, tr, C), lambda q, i, c: (q, i, 0))],
        out_specs=pl.BlockSpec((None, tr, C), lambda q, i, c: (q, i, 0)),
    )
    return pl.pallas_call(
        body, name=name, out_shape=jax.ShapeDtypeStruct((4, R, C), BF16), grid_spec=grid_spec,
        compiler_params=_cp("parallel", "parallel"),
    )(c_idx, g5, recv)


_ANY = pl.BlockSpec(memory_space=pl.ANY)


def _mesh_pos():
    x, y, c = lax.axis_index("x"), lax.axis_index("y"), lax.axis_index("c")
    return x, y, c, [(1 - x, y), (x, 1 - y), (1 - x, 1 - y)]


def _gather_small(pack, name="gather_small"):
    def body(pk, pk_all, psend, precv, loc):
        x, y, c, chips = _mesh_pos()
        me_slot = 4 * x + 2 * y + c
        sib = (x, y, 1 - c)
        own = pltpu.make_async_copy(pk, pk_all.at[me_slot], loc)
        own.start()
        peers = [sib] + [(*chip, c) for chip in chips] + [(*chip, 1 - c) for chip in chips]
        small = [pltpu.make_async_remote_copy(src_ref=pk, dst_ref=pk_all.at[me_slot], send_sem=psend.at[k], recv_sem=precv.at[k],
                                              device_id=p, device_id_type=MESH) for k, p in enumerate(peers)]
        for d in small:
            d.start()
        for k, p in enumerate(peers):
            pltpu.make_async_remote_copy(src_ref=pk, dst_ref=pk_all.at[4 * p[0] + 2 * p[1] + p[2]], send_sem=psend.at[k],
                                         recv_sem=precv.at[k], device_id=p, device_id_type=MESH).wait_recv()
        for d in small:
            d.wait_send()
        own.wait()

    return pl.pallas_call(
        body, name=name, out_shape=jax.ShapeDtypeStruct((NDEV,) + pack.shape, pack.dtype), in_specs=[_ANY], out_specs=_ANY,
        scratch_shapes=[pltpu.SemaphoreType.DMA((7,)), pltpu.SemaphoreType.DMA((7,)), pltpu.SemaphoreType.DMA(())],
    )(pack)


def _main_row(g):
    return g if g < C_LR else g - RANK


def _window_pieces(lo, hi):
    out = []
    for a, b, where in ((lo, min(hi, C_LR), "main"), (max(lo, C_LR), min(hi, C_LR + RANK), "lr"), (max(lo, C_LR + RANK), hi, "main")):
        if a < b:
            out.append((a, b, where, _main_row(a) if where == "main" else a - C_LR))
    return out


def _assemble_w_in(windows, own, name="assemble_w_in"):
    edges = NDEV - 1

    def body(b_ref, own_ref, main_ref, lr_ref, buf, ebuf, in_sems, out_sems, esems):
        dev = 4 * lax.axis_index("x") + 2 * lax.axis_index("y") + lax.axis_index("c")

        def load(k):
            return pltpu.make_async_copy(b_ref.at[k], buf.at[k % 2], in_sems.at[k % 2])

        def start_load(k):
            pl.when(dev == k)(pltpu.make_async_copy(own_ref, buf.at[k % 2], in_sems.at[k % 2]).start)
            pl.when(dev != k)(load(k).start)

        lr_ref[RANK:, :] = jnp.zeros((LANE - RANK, D), BF16)
        start_load(0)
        pending, edge_out = [], []
        for k in range(NDEV):
            s = k % 2
            load(k).wait()
            if k:
                ebuf[k - 1] = buf[1 - s, WSTEP:WWIN, :] + buf[s, 0:16, :]
                edge_out.append(pltpu.make_async_copy(ebuf.at[k - 1], main_ref.at[pl.ds(_main_row(WSTEP * k), 16)], esems.at[k - 1]))
                edge_out[-1].start()
                for d in pending:
                    d.wait()
            if k + 1 < NDEV:
                start_load(k + 1)
            pending = []
            lo = WSTEP * k + (16 if k else 0)
            hi = WSTEP * k + (WWIN if k == NDEV - 1 else WSTEP)
            for a, b, where, dst in _window_pieces(lo, hi):
                if where == "lr":
                    lr_ref[dst:dst + b - a, :] = buf[s, a - WSTEP * k:b - WSTEP * k, :]
                else:
                    pending.append(pltpu.make_async_copy(buf.at[s, pl.ds(a - WSTEP * k, b - a)], main_ref.at[pl.ds(dst, b - a)],
                                                         out_sems.at[2 * s + len(pending)]))
                    pending[-1].start()
        for d in pending + edge_out:
            d.wait()

    return pl.pallas_call(
        body, name=name,
        out_shape=(jax.ShapeDtypeStruct((NMAIN, D), BF16), jax.ShapeDtypeStruct((LANE, D), BF16)),
        in_specs=[_ANY, _ANY], out_specs=(_ANY, pl.BlockSpec(memory_space=pltpu.VMEM)),
        scratch_shapes=[pltpu.VMEM((2, WWIN, D), BF16), pltpu.VMEM((edges, 16, D), BF16), pltpu.SemaphoreType.DMA((2,)),
                        pltpu.SemaphoreType.DMA((4,)), pltpu.SemaphoreType.DMA((edges,))],
        compiler_params=pltpu.CompilerParams(vmem_limit_bytes=VMEM_LIMIT),
    )(windows, own)


def _disassemble_exchange(d_main, d_lr, d_wo, name="disassemble_exchange"):
    def body(main_ref, lr_ref, wo_ref, mine_ref, recv_ref, worecv_ref, buf, in_sems, keep_sems, send_sems, recv_sems, wo_sems):
        x, y, c, _ = _mesh_pos()
        sib = (x, y, 1 - c)
        wo = _rcopy(wo_ref.at[:, 1 - c], worecv_ref, wo_sems.at[0], wo_sems.at[1], sib)
        wo.start()

        def loads(k):
            s, out = k % 2, []
            for a, b, where, src0 in _window_pieces(WSTEP * k, WSTEP * k + WWIN):
                if where == "main":
                    out.append(pltpu.make_async_copy(main_ref.at[pl.ds(src0, b - a)], buf.at[s, pl.ds(a - WSTEP * k, b - a)],
                                                     in_sems.at[2 * s + len(out)]))
            return out

        def keep(k):
            return pltpu.make_async_copy(buf.at[k % 2], mine_ref.at[k // 2], keep_sems.at[k % 2])

        def send(k):
            return _rcopy(buf.at[k % 2], recv_ref.at[k // 2], send_sems.at[k % 2], recv_sems.at[k // 2], sib)

        def store_start(k):
            pl.when(c == k % 2)(keep(k).start)
            pl.when(c != k % 2)(send(k).start)

        def store_wait(k):
            pl.when(c == k % 2)(keep(k).wait)
            pl.when(c != k % 2)(send(k).wait_send)

        for d in loads(0):
            d.start()
        for k in range(NDEV):
            for d in loads(k):
                d.wait()
            for a, b, where, src0 in _window_pieces(WSTEP * k, WSTEP * k + WWIN):
                if where == "lr":
                    buf[k % 2, a - WSTEP * k:b - WSTEP * k, :] = lr_ref[src0:src0 + b - a, :]
            if k:
                store_wait(k - 1)
            if k + 1 < NDEV:
                for d in loads(k + 1):
                    d.start()
            store_start(k)
        store_wait(NDEV - 1)
        for chip in range(NDEV // 2):
            _rcopy(buf.at[0], recv_ref.at[chip], send_sems.at[0], recv_sems.at[chip], sib).wait_recv()
        wo.wait_send()
        wo.wait_recv()

    half = jax.ShapeDtypeStruct((NDEV // 2, WWIN, D), BF16)
    return pl.pallas_call(
        body, name=name, out_shape=(half, half, jax.ShapeDtypeStruct((NDEV // 2,) + d_wo.shape[2:], BF16)),
        in_specs=[_ANY, pl.BlockSpec(memory_space=pltpu.VMEM), _ANY], out_specs=(_ANY, _ANY, _ANY),
        scratch_shapes=[pltpu.VMEM((2, WWIN, D), BF16), pltpu.SemaphoreType.DMA((4,)), pltpu.SemaphoreType.DMA((2,)),
                        pltpu.SemaphoreType.DMA((2,)), pltpu.SemaphoreType.DMA((NDEV // 2,)), pltpu.SemaphoreType.DMA((2,))],
        compiler_params=pltpu.CompilerParams(vmem_limit_bytes=VMEM_LIMIT),
    )(d_main, d_lr, d_wo)


def _add_blocks(a, b, name, tr):
    _, R, C = a.shape

    def body(a_ref, b_ref, o_ref):
        o_ref[...] = (a_ref[...].astype(F32) + b_ref[...].astype(F32)).astype(BF16)

    blk = pl.BlockSpec((None, tr, C), lambda q, i: (q, i, 0))
    return pl.pallas_call(body, name=name, out_shape=jax.ShapeDtypeStruct(a.shape, BF16), grid=(a.shape[0], R // tr),
                          in_specs=[blk, blk], out_specs=blk, compiler_params=_cp("parallel", "parallel"))(a, b)


_HBM = pl.BlockSpec(memory_space=pltpu.HBM)
_SEM = pl.BlockSpec(memory_space=pltpu.SEMAPHORE)
_VMEM = pl.BlockSpec(memory_space=pltpu.VMEM)
_SIDE = pltpu.CompilerParams(has_side_effects=pltpu.SideEffectType.DATAFLOW_SIDE_EFFECTING)
_TOKEN = jax.ShapeDtypeStruct((8, LANE), F32)


def _hbm(a):
    return pltpu.with_memory_space_constraint(a, pltpu.HBM)


def _hbm_like(arrs):
    return tuple(pltpu.HBM(a.shape, a.dtype) for a in arrs)


def _tie(x, token):
    return x + token[0, 0].astype(x.dtype)


def _chip_copies(ins, lands, send, recv, nrel):
    x, y, c, chips = _mesh_pos()
    return [pltpu.make_async_remote_copy(src_ref=ins[a].at[2 * chip[0] + chip[1]], dst_ref=lands[a].at[j], send_sem=send.at[nrel * a + j],
                                         recv_sem=recv.at[nrel * a + j], device_id=(*chip, c), device_id_type=MESH)
            for a in range(len(ins)) for j, chip in enumerate(chips[:nrel])]


def _chip_start(psums, name, nrel=3):
    n = len(psums)
    lands = [lax.empty((nrel,) + p.shape[1:], p.dtype) for p in psums]

    def body(*refs):
        for d in _chip_copies(refs[:n], refs[n:2 * n], refs[2 * n], refs[2 * n + 1], nrel):
            d.start()
        refs[-1][...] = jnp.zeros_like(refs[-1])

    sems = pltpu.SemaphoreType.DMA((nrel * n,))
    out = pl.pallas_call(
        body, name=name, out_shape=(sems, sems) + _hbm_like(psums) + _hbm_like(lands) + (_TOKEN,),
        in_specs=[_HBM] * (2 * n), out_specs=(_SEM, _SEM) + (_HBM,) * (2 * n) + (_VMEM,),
        input_output_aliases={i: 2 + i for i in range(2 * n)}, compiler_params=_SIDE,
    )(*[_hbm(a) for a in list(psums) + lands])
    return out[0], out[1], list(out[2:2 + n]), list(out[2 + n:2 + 2 * n]), out[-1]


def _chip_wait(send, recv, psums, lands, after, name):
    n = len(psums)
    nrel = lands[0].shape[0]

    def body(*refs):
        for d in _chip_copies(refs[:n], refs[n:2 * n], refs[2 * n], refs[2 * n + 1], nrel):
            d.wait_send()
            d.wait_recv()

    out = pl.pallas_call(
        body, name=name, out_shape=_hbm_like(psums) + _hbm_like(lands),
        in_specs=[_HBM] * (2 * n) + [_SEM, _SEM, _ANY], out_specs=(_HBM,) * (2 * n),
        input_output_aliases={i: i for i in range(2 * n)}, compiler_params=_SIDE,
    )(*psums, *lands, send, recv, after)
    return list(out[:n]), list(out[n:])


def _hop_pos():
    x, y, c, _ = _mesh_pos()
    north = c == 1
    via = (jnp.where(north, 1 - x, x), jnp.where(north, y, 1 - y))
    return (*via, c), 2 * (1 - x) + (1 - y), jnp.where(north, 2 * x + (1 - y), 2 * (1 - x) + y)


def _hop_copies(ins, lands, send, recv):
    to, mine, _ = _hop_pos()
    return [pltpu.make_async_remote_copy(src_ref=ins[a].at[mine], dst_ref=lands[a], send_sem=send.at[a], recv_sem=recv.at[a],
                                         device_id=to, device_id_type=MESH) for a in range(len(ins))]


def _hop_start(psums, name):
    n = len(psums)
    lands = [lax.empty(p.shape[1:], p.dtype) for p in psums]

    def body(*refs):
        for d in _hop_copies(refs[:n], refs[n:2 * n], refs[2 * n], refs[2 * n + 1]):
            d.start()
        refs[-1][...] = jnp.zeros_like(refs[-1])

    sems = pltpu.SemaphoreType.DMA((n,))
    out = pl.pallas_call(
        body, name=name, out_shape=(sems, sems) + _hbm_like(psums) + _hbm_like(lands) + (_TOKEN,),
        in_specs=[_HBM] * (2 * n), out_specs=(_SEM, _SEM) + (_HBM,) * (2 * n) + (_VMEM,),
        input_output_aliases={i: 2 + i for i in range(2 * n)}, compiler_params=_SIDE,
    )(*[_hbm(a) for a in list(psums) + lands])
    return out[0], out[1], list(out[2:2 + n]), list(out[2 + n:2 + 2 * n]), out[-1]


def _hop_wait(send, recv, psums, lands, after, name):
    n = len(psums)

    def body(*refs):
        for d in _hop_copies(refs[:n], refs[n:2 * n], refs[2 * n], refs[2 * n + 1]):
            d.wait_send()
            d.wait_recv()

    out = pl.pallas_call(
        body, name=name, out_shape=_hbm_like(psums) + _hbm_like(lands),
        in_specs=[_HBM] * (2 * n) + [_SEM, _SEM, _ANY], out_specs=(_HBM,) * (2 * n),
        input_output_aliases={i: i for i in range(2 * n)}, compiler_params=_SIDE,
    )(*psums, *lands, send, recv, after)
    return list(out[:n]), list(out[n:])


def _hop_add(psums, land, idx, name, tr):
    _, R, C = psums.shape

    def body(s_ref, p_ref, l_ref, o_ref):
        o_ref[...] = (p_ref[...].astype(F32) + l_ref[...].astype(F32)).astype(BF16)

    blk = pl.BlockSpec((None, tr, C), lambda i, s: (s[0], i, 0))
    grid_spec = pltpu.PrefetchScalarGridSpec(num_scalar_prefetch=1, grid=(R // tr,),
                                             in_specs=[blk, pl.BlockSpec((tr, C), lambda i, s: (i, 0))], out_specs=blk)
    return pl.pallas_call(body, name=name, out_shape=jax.ShapeDtypeStruct(psums.shape, BF16), grid_spec=grid_spec,
                          input_output_aliases={1: 0}, compiler_params=_cp("parallel"))(idx, psums, land)


def _pair_copies(ins, lands, send, recv):
    x, y, c, _ = _mesh_pos()
    return [pltpu.make_async_remote_copy(src_ref=ins[a].at[:, 1 - c], dst_ref=lands[a], send_sem=send.at[a], recv_sem=recv.at[a],
                                         device_id=(x, y, 1 - c), device_id_type=MESH) for a in range(len(ins))]


def _pair_start(grads, name):
    n = len(grads)
    lands = [lax.empty((4,) + g.shape[2:], g.dtype) for g in grads]

    def body(*refs):
        for d in _pair_copies(refs[:n], refs[n:2 * n], refs[2 * n], refs[2 * n + 1]):
            d.start()
        refs[-1][...] = jnp.zeros_like(refs[-1])

    sems = pltpu.SemaphoreType.DMA((n,))
    out = pl.pallas_call(
        body, name=name, out_shape=(sems, sems) + _hbm_like(grads) + _hbm_like(lands) + (_TOKEN,),
        in_specs=[_HBM] * (2 * n), out_specs=(_SEM, _SEM) + (_HBM,) * (2 * n) + (_VMEM,),
        input_output_aliases={i: 2 + i for i in range(2 * n)}, compiler_params=_SIDE,
    )(*[_hbm(a) for a in list(grads) + lands])
    return out[0], out[1], list(out[2:2 + n]), list(out[2 + n:2 + 2 * n]), out[-1]


def _pair_wait(send, recv, grads, lands, after, name):
    n = len(grads)

    def body(*refs):
        for d in _pair_copies(refs[:n], refs[n:2 * n], refs[2 * n], refs[2 * n + 1]):
            d.wait_send()
            d.wait_recv()

    out = pl.pallas_call(
        body, name=name, out_shape=_hbm_like(grads) + _hbm_like(lands),
        in_specs=[_HBM] * (2 * n) + [_SEM, _SEM, _ANY], out_specs=(_HBM,) * (2 * n),
        input_output_aliases={i: i for i in range(2 * n)}, compiler_params=_SIDE,
    )(*grads, *lands, send, recv, after)
    return list(out[:n]), list(out[n:])


def _slot(chip, c):
    return 4 * chip[0] + 2 * chip[1] + c


def _gather_start(shards, lands, after, name):
    n = len(shards)

    def body(*refs):
        src, land, send, recv = refs[:n], refs[n:2 * n], refs[2 * n + 1], refs[2 * n + 2]
        x, y, c, chips = _mesh_pos()
        for a in range(n):
            for k, to in enumerate([(x, y, 1 - c)] + [(*chip, c) for chip in chips]):
                pltpu.make_async_remote_copy(src_ref=src[a], dst_ref=land[a].at[_slot((x, y), c)], send_sem=send.at[4 * a + k],
                                             recv_sem=recv.at[4 * a + k], device_id=to, device_id_type=MESH).start()
        refs[-1][...] = jnp.zeros_like(refs[-1])

    sems = pltpu.SemaphoreType.DMA((4 * n,))
    out = pl.pallas_call(
        body, name=name, out_shape=(sems, sems) + _hbm_like(shards) + _hbm_like(lands) + (_TOKEN,),
        in_specs=[_HBM] * (2 * n) + [_ANY], out_specs=(_SEM, _SEM) + (_HBM,) * (2 * n) + (_VMEM,),
        input_output_aliases={i: 2 + i for i in range(2 * n)}, compiler_params=_SIDE,
    )(*[_hbm(a) for a in list(shards) + list(lands)], after)
    return out[0], out[1], list(out[2:2 + n]), list(out[2 + n:2 + 2 * n]), out[-1]


def _gather_pass(lands, recv, after, name, first=0):
    n = len(lands)

    def body(*refs):
        land, recv1 = refs[:n], refs[n]
        send2, recv2 = refs[n + 2], refs[n + 3]
        x, y, c, chips = _mesh_pos()
        for a in range(n):
            for j, chip in enumerate(chips):
                blk = land[a].at[_slot(chip, c)]
                pltpu.make_async_remote_copy(src_ref=blk, dst_ref=blk, send_sem=send2.at[3 * a + j], recv_sem=recv1.at[4 * (first + a) + 1 + j],
                                             device_id=(*chip, c), device_id_type=MESH).wait_recv()
                pltpu.make_async_remote_copy(src_ref=blk, dst_ref=blk, send_sem=send2.at[3 * a + j], recv_sem=recv2.at[3 * a + j],
                                             device_id=(x, y, 1 - c), device_id_type=MESH).start()
        refs[-1][...] = jnp.zeros_like(refs[-1])

    sems = pltpu.SemaphoreType.DMA((3 * n,))
    out = pl.pallas_call(
        body, name=name, out_shape=(sems, sems) + _hbm_like(lands) + (_TOKEN,),
        in_specs=[_HBM] * n + [_SEM, _ANY], out_specs=(_SEM, _SEM) + (_HBM,) * n + (_VMEM,),
        input_output_aliases={i: 2 + i for i in range(n)}, compiler_params=_SIDE,
    )(*lands, recv, after)
    return out[0], out[1], list(out[2:2 + n]), out[-1]


def _gather_wait(shards, lands, send, recv, send2, recv2, after, name, first=0):
    n = len(lands)

    def body(*refs):
        src, land = refs[:n], refs[n:2 * n]
        send1, recv1, snd2, rcv2 = refs[2 * n:2 * n + 4]
        x, y, c, chips = _mesh_pos()
        sib = (x, y, 1 - c)
        for a in range(n):
            for k in range(4):
                pltpu.make_async_remote_copy(src_ref=src[a], dst_ref=land[a].at[_slot((x, y), c)], send_sem=send1.at[4 * (first + a) + k],
                                             recv_sem=recv1.at[4 * (first + a) + k], device_id=sib, device_id_type=MESH).wait_send()
            blk = land[a].at[_slot((x, y), 1 - c)]
            pltpu.make_async_remote_copy(src_ref=blk, dst_ref=blk, send_sem=send1.at[4 * (first + a)], recv_sem=recv1.at[4 * (first + a)],
                                         device_id=sib, device_id_type=MESH).wait_recv()
            for j, chip in enumerate(chips):
                mine, theirs = land[a].at[_slot(chip, c)], land[a].at[_slot(chip, 1 - c)]
                pltpu.make_async_remote_copy(src_ref=mine, dst_ref=mine, send_sem=snd2.at[3 * a + j], recv_sem=rcv2.at[3 * a + j],
                                             device_id=sib, device_id_type=MESH).wait_send()
                pltpu.make_async_remote_copy(src_ref=theirs, dst_ref=theirs, send_sem=snd2.at[3 * a + j], recv_sem=rcv2.at[3 * a + j],
                                             device_id=sib, device_id_type=MESH).wait_recv()

    out = pl.pallas_call(
        body, name=name, out_shape=_hbm_like(shards) + _hbm_like(lands),
        in_specs=[_HBM] * (2 * n) + [_SEM] * 4 + [_ANY], out_specs=(_HBM,) * (2 * n),
        input_output_aliases={i: i for i in range(2 * n)}, compiler_params=_SIDE,
    )(*shards, *lands, send, recv, send2, recv2, after)
    return list(out[n:])


def _win_tree():
    x, y, c, chips = _mesh_pos()
    north = c == 1
    handed = (jnp.where(north, 1 - x, x), jnp.where(north, y, 1 - y))
    hand_to = (jnp.where(north, x, 1 - x), jnp.where(north, 1 - y, y))
    return x, y, c, chips, handed, hand_to


def _blk(land, chip, c):
    return land.at[_slot(chip, c)]


def _rcopy(src, dst, send, recv, to):
    return pltpu.make_async_remote_copy(src_ref=src, dst_ref=dst, send_sem=send, recv_sem=recv, device_id=to, device_id_type=MESH)


def _win_start(shards, lands, name):
    n = len(shards)

    def body(*refs):
        src, land, send, recv = refs[:n], refs[n:2 * n], refs[2 * n], refs[2 * n + 1]
        x, y, c, chips, _, _ = _win_tree()
        for a in range(n):
            for k, to in enumerate([(x, y, 1 - c), (*chips[0], c), (*chips[1], c)]):
                _rcopy(src[a], _blk(land[a], (x, y), c), send.at[3 * a + k], recv.at[3 * a + k], to).start()
        refs[-1][...] = jnp.zeros_like(refs[-1])

    sems = pltpu.SemaphoreType.DMA((3 * n,))
    out = pl.pallas_call(
        body, name=name, out_shape=(sems, sems) + _hbm_like(shards) + _hbm_like(lands) + (_TOKEN,),
        in_specs=[_HBM] * (2 * n), out_specs=(_SEM, _SEM) + (_HBM,) * (2 * n) + (_VMEM,),
        input_output_aliases={i: 2 + i for i in range(2 * n)}, compiler_params=_SIDE,
    )(*[_hbm(a) for a in list(shards) + list(lands)])
    return out[0], out[1], list(out[2:2 + n]), list(out[2 + n:2 + 2 * n]), out[-1]


def _win_hand_on(lands, recv1, after, name):
    n, m = len(lands), len(after)

    def body(*refs):
        land, rcv1 = refs[:n], refs[n]
        send2, recv2 = refs[n + 1 + m], refs[n + 2 + m]
        x, y, c, chips, handed, hand_to = _win_tree()
        for a in range(n):
            for j in range(2):
                blk = _blk(land[a], chips[j], c)
                _rcopy(blk, blk, send2.at[3 * a], rcv1.at[3 * a + 1 + j], (*chips[j], c)).wait_recv()
            blk = _blk(land[a], handed, c)
            _rcopy(blk, blk, send2.at[3 * a], recv2.at[3 * a], (*hand_to, c)).start()
            for j in range(2):
                blk = _blk(land[a], chips[j], c)
                _rcopy(blk, blk, send2.at[3 * a + 1 + j], recv2.at[3 * a + 1 + j], (x, y, 1 - c)).start()
        refs[-1][...] = jnp.zeros_like(refs[-1])

    sems = pltpu.SemaphoreType.DMA((3 * n,))
    out = pl.pallas_call(
        body, name=name, out_shape=(sems, sems) + _hbm_like(lands) + (_TOKEN,),
        in_specs=[_HBM] * n + [_SEM] + [_ANY] * m, out_specs=(_SEM, _SEM) + (_HBM,) * n + (_VMEM,),
        input_output_aliases={i: 2 + i for i in range(n)}, compiler_params=_SIDE,
    )(*lands, recv1, *after)
    return out[0], out[1], list(out[2:2 + n]), out[-1]


def _win_last(lands, recv2, after, name):
    n, m = len(lands), len(after)

    def body(*refs):
        land, rcv2 = refs[:n], refs[n]
        send3, recv3 = refs[n + 1 + m], refs[n + 2 + m]
        x, y, c, chips, _, hand_to = _win_tree()
        for a in range(n):
            blk = _blk(land[a], chips[2], c)
            _rcopy(blk, blk, send3.at[a], rcv2.at[3 * a], (*hand_to, c)).wait_recv()
            _rcopy(blk, blk, send3.at[a], recv3.at[a], (x, y, 1 - c)).start()
        refs[-1][...] = jnp.zeros_like(refs[-1])

    sems = pltpu.SemaphoreType.DMA((n,))
    out = pl.pallas_call(
        body, name=name, out_shape=(sems, sems) + _hbm_like(lands) + (_TOKEN,),
        in_specs=[_HBM] * n + [_SEM] + [_ANY] * m, out_specs=(_SEM, _SEM) + (_HBM,) * n + (_VMEM,),
        input_output_aliases={i: 2 + i for i in range(n)}, compiler_params=_SIDE,
    )(*lands, recv2, *after)
    return out[0], out[1], list(out[2:2 + n]), out[-1]


def _win_wait(shards, lands, sems1, sems2, sems3, after, name):
    n = len(lands)

    def body(*refs):
        src, land = refs[:n], refs[n:2 * n]
        send1, recv1, send2, recv2, send3, recv3 = refs[2 * n:2 * n + 6]
        x, y, c, chips, handed, hand_to = _win_tree()
        sib = (x, y, 1 - c)
        for a in range(n):
            own = _blk(land[a], (x, y), c)
            for k in range(3):
                _rcopy(src[a], own, send1.at[3 * a + k], recv1.at[3 * a + k], sib).wait_send()
            blk = _blk(land[a], (x, y), 1 - c)
            _rcopy(blk, blk, send1.at[3 * a], recv1.at[3 * a], sib).wait_recv()
            blk = _blk(land[a], handed, c)
            _rcopy(blk, blk, send2.at[3 * a], recv2.at[3 * a], sib).wait_send()
            for j in range(2):
                mine, theirs = _blk(land[a], chips[j], c), _blk(land[a], chips[j], 1 - c)
                _rcopy(mine, mine, send2.at[3 * a + 1 + j], recv2.at[3 * a + 1 + j], sib).wait_send()
                _rcopy(theirs, theirs, send2.at[3 * a + 1 + j], recv2.at[3 * a + 1 + j], sib).wait_recv()
            mine, theirs = _blk(land[a], chips[2], c), _blk(land[a], chips[2], 1 - c)
            _rcopy(mine, mine, send3.at[a], recv3.at[a], sib).wait_send()
            _rcopy(theirs, theirs, send3.at[a], recv3.at[a], sib).wait_recv()

    out = pl.pallas_call(
        body, name=name, out_shape=_hbm_like(shards) + _hbm_like(lands),
        in_specs=[_HBM] * (2 * n) + [_SEM] * 6 + [_ANY], out_specs=(_HBM,) * (2 * n),
        input_output_aliases={i: i for i in range(2 * n)}, compiler_params=_SIDE,
    )(*shards, *lands, *sems1, *sems2, *sems3, after)
    return list(out[n:])


def _pad_to(v, n):
    return jnp.pad(v, [(0, 0)] * (v.ndim - 1) + [(0, n - v.shape[-1])])


def _pack_small(n1, gb, sk, gn, n2, fn, extra=None):
    parts = [n1.reshape(-1), gb.reshape(-1), sk.reshape(-1), gn.reshape(-1), n2.reshape(-1), fn.reshape(-1)]
    flat = jnp.concatenate(parts + ([extra.reshape(-1)] if extra is not None else []))
    return _pad_to(flat, SMALL_N).reshape(SMALL_ROWS, LANE)


def _unpack_small(p):
    f = p.reshape(-1)
    return (f[S_N1:S_GB].reshape(1, D), f[S_GB:S_SK].reshape(1, GH * DK), f[S_SK:S_GN].reshape(1, NQ), f[S_GN:S_N2].reshape(1, DV),
            f[S_N2:S_FN].reshape(1, D), f[S_FN:S_LOSS].reshape(D))


class _NoComm:
    def __init__(self, wo, wg_all, wu_all, wd_all):
        self.rest = (wo, wg_all, wu_all, wd_all)

    def mixed(self, gla_o, gla_norm_w):
        return gla_norm_w

    def w_out(self, merged, norm2_w):
        return self.rest[0], norm2_w

    def w_up(self, v2):
        return self.rest[1], self.rest[2]

    def w_down(self, ff):
        return self.rest[3]

    def ffn_grads(self, d_wg, d_wu, d_wd):
        self.ffn = (d_wg, d_wu, d_wd)

    def ffn_reduce(self, dv2, norm2_w):
        return norm2_w

    def in_grads(self, d_wmain, d_wlr, d_wo, w_lr):
        self.inw = (d_wmain, d_wlr, d_wo)
        return w_lr

    def in_reduce(self, du_lr):
        return None


class _Comm:
    def __init__(self, rest_shards, rest_lands, after, c_idx):
        self.c_idx = c_idx
        self.send, self.recv, self.shards, self.lands, self.token = _gather_start(rest_shards, rest_lands, after, "gather_rest_start")

    def _pass(self, lo, hi, after, tag):
        send2, recv2, lands, token = _gather_pass(self.lands[lo:hi], self.recv, after, "gather_pass_" + tag, first=lo)
        self.passed = (lo, hi, send2, recv2, lands)
        return token

    def _wait(self, after, tag):
        lo, hi, send2, recv2, lands = self.passed
        return _gather_wait(self.shards[lo:hi], lands, self.send, self.recv, send2, recv2, after, "gather_wait_" + tag, first=lo)

    def mixed(self, gla_o, gla_norm_w):
        return _tie(gla_norm_w, self._pass(0, 1, gla_o, "out"))

    def w_out(self, merged, norm2_w):
        (wo_all,) = self._wait(merged, "out")
        return wo_all.reshape(D, D), _tie(norm2_w, self._pass(1, 3, merged, "up"))

    def w_up(self, v2):
        wg_all, wu_all = self._wait(v2, "up")
        self._pass(3, 4, v2, "down")
        return wg_all.reshape(FH, D), wu_all.reshape(FH, D)

    def w_down(self, ff):
        return self._wait(ff, "down")[0].reshape(FH, D)

    def _reduce(self, tag, names, grads, recv1, rows):
        psums = [_pair_add(g, r, self.c_idx, "pair_add_" + nm, tr) for g, r, nm, tr in zip(grads, recv1, names, rows)]
        *flight, token = _chip_start(psums, "reduce_chips_start_" + tag)
        return dict(tag=tag, names=names, rows=rows, flight=flight), token

    def ffn_grads(self, d_wg, d_wu, d_wd):
        self.ffn_pair = _pair_start([d.reshape(4, 2, FS, D) for d in (d_wg, d_wu, d_wd)], "reduce_pair_start_ffn")
        return self.ffn_pair[-1]

    def ffn_reduce(self, dv2, norm2_w):
        send, recv, grads, lands, _ = self.ffn_pair
        grads, recv1 = _pair_wait(send, recv, grads, lands, dv2, "reduce_pair_wait_ffn")
        self.ffn, token = self._reduce("ffn", ["w_ffn_gate", "w_ffn_up", "w_ffn_down"], grads, recv1, [176, 176, 176])
        return _tie(norm2_w, token)

    def in_grads(self, d_wmain, d_wlr, d_wo, w_lr):
        d_wo4 = d_wo.reshape(4, 2, D // NDEV, D)
        mine, recv, wo_recv = _disassemble_exchange(d_wmain, d_wlr, d_wo4)
        self.in_names, self.in_rows = ["w_in", "w_out"], [808, 256]
        psums = [_add_blocks(mine, recv, "pair_add_w_in", 808), _pair_add(d_wo4, wo_recv, self.c_idx, "pair_add_w_out", 256)]
        *self.in_hop, token = _hop_start(psums, "reduce_hop_start_in")
        return _tie(w_lr, token)

    def in_reduce(self, du_lr):
        psums, lands = _hop_wait(*self.in_hop, self.update(self.ffn, du_lr), "reduce_hop_wait_in")
        idx = _hop_pos()[2].astype(jnp.int32).reshape(1)
        psums = [_hop_add(p, l, idx, "hop_add_" + nm, tr) for p, l, nm, tr in zip(psums, lands, self.in_names, self.in_rows)]
        *flight, token = _chip_start(psums, "reduce_chips_start_in", nrel=2)
        self.inw = dict(tag="in", names=self.in_names, rows=self.in_rows, flight=flight)
        return token


def _local_step(xs, tgt, u, norm1_w, gla_gate_b, attn_sinks, gla_norm_w, norm2_w, fnw, w_main, w_lr, w2p, comm):
    proj =_mm(u, w_main, tb=True, tm=1024, tn=1280, tk=D, name="in_proj")
    plr = _mm(u, w_lr, tb=True, tm=1024, tn=LANE, tk=D, name="in_proj_lr")
    attn_o = _attn_fwd(proj, attn_sinks)
    gla_o, states = _gla_fwd(proj, plr, w2p, gla_gate_b)
    merged = _merge_fwd(attn_o, gla_o, proj, comm.mixed(gla_o, gla_norm_w))
    wo, norm2_w = comm.w_out(merged, norm2_w)
    h1 = _mm(merged, wo, tm=1024, tn=512, tk=D, res=xs, name="out_proj")
    v2 = _rmsnorm_fwd(h1, norm2_w, "norm2_fwd")
    wg_all, wu_all = comm.w_up(v2)
    fa, fb, ff = _ffn_up(v2, wg_all, wu_all)
    wd_all = comm.w_down(ff)
    h2 = _mm(ff, wd_all, tm=1024, tn=1024, tk=FH // 2, res=h1, name="ffn_down")
    dh2, dh2b, d_fnw, loss_part = _loss_head(h2, fnw, tgt)

    da, db = _ffn_dact(dh2b, wd_all, fa, fb)
    Tn = xs.shape[0]
    d_wd = _mm(ff, dh2b, ta=True, tm=512, tn=D, tk=Tn, out_dtype=BF16, name="ffn_dwd")
    d_wg = _mm(da, v2, ta=True, tm=512, tn=D, tk=Tn, out_dtype=BF16, name="ffn_dwg")
    d_wu = _mm(db, v2, ta=True, tm=512, tn=D, tk=Tn, out_dtype=BF16, name="ffn_dwu")
    dv2 = _mm(da, wg_all, tm=1024, tn=1024, tk=FH // 2, after=comm.ffn_grads(d_wg, d_wu, d_wd), name="ffn_dv2_gate")
    dv2 = _mm(db, wu_all, tm=1024, tn=1024, tk=FH // 2, res=dv2, name="ffn_dv2_up")
    norm2_w = comm.ffn_reduce(dv2, norm2_w)
    dh1, dh1b, d_n2 = _rmsnorm_bwd(dv2, h1, norm2_w, dh2, "norm2_bwd")
    dmerged = _mm(dh1b, wo, tb=True, tm=1024, tn=512, tk=D, name="out_proj_dx")
    d_wo = _mm(merged, dh1b, ta=True, tm=1024, tn=512, tk=xs.shape[0], out_dtype=BF16, name="out_proj_dw")
    d_attn, d_gla, d_gates, d_gnw = _merge_bwd(dmerged, attn_o, gla_o, proj, gla_norm_w)
    d_q, d_kv, d_sinks = _attn_bwd(proj, attn_sinks, attn_o, d_attn)
    d_gqk, d_gv, d_plr, d_w2p, d_gb = _gla_bwd(proj, plr, w2p, gla_gate_b, states, d_gla)
    dproj = jnp.concatenate([d_q, d_kv, d_gqk, d_gv, d_gates], axis=1)
    d_wmain = _mm(dproj, u, ta=True, tm=640, tn=D, tk=xs.shape[0], out_dtype=BF16, name="in_proj_dw")
    d_wlr = _mm(d_plr, u, ta=True, tm=LANE, tn=1024, tk=xs.shape[0], out_dtype=BF16, name="in_proj_lr_dw")
    du_lr = _mm(d_plr, comm.in_grads(d_wmain, d_wlr, d_wo, w_lr), tm=1024, tn=1024, tk=LANE, name="in_proj_lr_dx")
    du = _mm(dproj, w_main, tm=1024, tn=1024, tk=2560, res=du_lr, after=comm.in_reduce(du_lr), name="in_proj_dx")
    dx, _, d_n1 = _rmsnorm_bwd(du, xs, norm1_w, dh1, "norm1_bwd")
    return dx, loss_part, d_w2p, d_gb, d_sinks, d_gnw, d_n1, d_n2, d_fnw


def kernel(x, norm1_w, w_in, gla_gate_w2, gla_gate_b, attn_sinks, gla_norm_w, w_out, norm2_w, w_ffn_gate, w_ffn_up, w_ffn_down, final_norm_w, loss_target, m_norm1_w, m_w_in, m_gla_gate_w2, m_gla_gate_b, m_attn_sinks, m_gla_norm_w, m_w_out, m_norm2_w, m_w_ffn_gate, m_w_ffn_up, m_w_ffn_down, m_final_norm_w, v_norm1_w, v_w_in, v_gla_gate_w2, v_gla_gate_b, v_attn_sinks, v_gla_norm_w, v_w_out, v_norm2_w, v_w_ffn_gate, v_w_ffn_up, v_w_ffn_down, v_final_norm_w):
    xs, tgt = x[0], loss_target[0]
    fnw = final_norm_w.reshape(1, D)
    c_idx = lax.axis_index("c").astype(jnp.int32).reshape(1)
    dev = 4 * lax.axis_index("x") + 2 * lax.axis_index("y") + lax.axis_index("c")

    chip_idx = (2 * lax.axis_index("x") + lax.axis_index("y")).astype(jnp.int32).reshape(1)

    shift = (WS - WSTEP) * dev
    window = lax.dynamic_update_slice(jnp.zeros((WWIN, D), BF16), jnp.transpose(w_in[0]).astype(BF16), (shift, 0))
    w2_land = lax.dynamic_update_slice(lax.empty((NDEV, RANK, LANE), F32), gla_gate_w2, (dev, 0, 0))
    *sems1, win_srcs, win_lands, tok = _win_start([window, gla_gate_w2[0]], [lax.empty((NDEV, WWIN, D), BF16), w2_land], "gather_in_start")
    tr2 = lambda t: jnp.transpose(t[0])
    rows3 = lambda t: jnp.transpose(t[0] + tok[0, 0]).reshape(WS, D // LANE, LANE)
    rest = [(w + tok[0, 0]).astype(BF16) for w in (w_out[0], tr2(w_ffn_gate), tr2(w_ffn_up), w_ffn_down[0])]
    rest_lands = [lax.dynamic_update_slice(lax.empty((NDEV,) + s.shape, s.dtype), s[None], (dev, 0, 0)) for s in rest]
    win3 = [rows3(t) for t in (w_in, m_w_in, v_w_in)]
    *sems2, win_lands, tok = _win_hand_on(win_lands, sems1[1], rest + rest_lands + win3, "gather_in_hand_on")
    u = _rmsnorm_fwd(xs, _tie(norm1_w, tok), "norm1_fwd")
    *sems3, win_lands, tok = _win_last(win_lands, sems2[1], [u], "gather_in_last")
    comm = _Comm(rest, rest_lands, tok, c_idx)
    win_all, w2_all = _win_wait(win_srcs, win_lands, sems1, sems2, sems3, comm.token, "gather_in_wait")
    w_main, w_lr = _assemble_w_in(win_all, window)
    w2p = jnp.pad(jnp.transpose(w2_all, (1, 0, 2)).reshape(RANK, GH * DK), ((0, LANE - RANK), (0, 0)))

    big = {}

    def update(grp, after):
        psums, parts = _chip_wait(*grp["flight"], after, "reduce_chips_wait_" + grp["tag"])
        for nm, ps, pt, tr in zip(grp["names"], psums, parts, grp["rows"]):
            w, m, v = {"w_in": (w_in, m_w_in, v_w_in), "w_out": (w_out, m_w_out, v_w_out), "w_ffn_gate": (w_ffn_gate, m_w_ffn_gate, v_w_ffn_gate),
                       "w_ffn_up": (w_ffn_up, m_w_ffn_up, v_w_ffn_up), "w_ffn_down": (w_ffn_down, m_w_ffn_down, v_w_ffn_down)}[nm]
            if nm == "w_in":
                g_win = _sum_parts(ps, pt, chip_idx, "sum_w_in", tr, 1024)
                g3 = lax.dynamic_slice(g_win, (shift, 0), (WS, D)).reshape(WS, D // LANE, LANE)
                out3 = (g3,) + tuple(_adamw_rows(*win3, g3, "adamw_w_in", 178))
                big[nm] = [jnp.transpose(t.reshape(WS, D))[None] for t in out3]
            elif nm in ("w_ffn_gate", "w_ffn_up"):
                big[nm] = [jnp.transpose(t)[None] for t in _adamw(tr2(w), tr2(m), tr2(v), ps, pt, chip_idx, "adamw_" + nm, tr)]
            else:
                big[nm] = [t[None] for t in _adamw(w[0], m[0], v[0], ps, pt, chip_idx, "adamw_" + nm, tr)]
            after = big[nm][0]
        return after

    comm.update = update
    dx, loss_part, d_w2p, d_gb, d_sinks, d_gnw, d_n1, d_n2, d_fnw = _local_step(
        xs, tgt, u, norm1_w, gla_gate_b, attn_sinks, gla_norm_w, norm2_w, fnw, w_main, w_lr, w2p, comm)

    pack = jnp.concatenate([_pack_small(d_n1, d_gb, d_sinks, d_gnw, d_n2, d_fnw, loss_part),
                            d_w2p[:RANK].reshape(GW2_ROWS, LANE)], axis=0)
    small = _sum_devices(_gather_small(pack))

    update(comm.inw, dx)
    g_small = small[:SMALL_ROWS]
    sm = _adamw_plain(_pack_small(norm1_w, gla_gate_b, attn_sinks, gla_norm_w, norm2_w, final_norm_w),
                      _pack_small(m_norm1_w, m_gla_gate_b, m_attn_sinks, m_gla_norm_w, m_norm2_w, m_final_norm_w),
                      _pack_small(v_norm1_w, v_gla_gate_b, v_attn_sinks, v_gla_norm_w, v_norm2_w, v_final_norm_w), g_small, "adamw_small")
    g_w2 = lax.dynamic_slice_in_dim(small[SMALL_ROWS:].reshape(RANK, GH * DK), dev * LANE, LANE, axis=1)
    w2 = [g_w2[None]] + [t[None] for t in _adamw_plain(gla_gate_w2[0], m_gla_gate_w2[0], v_gla_gate_w2[0], g_w2, "adamw_w2")]
    loss = g_small.reshape(-1)[S_LOSS]

    sg, sd, sm2, sv2 = [_unpack_small(t) for t in (g_small,) + tuple(sm)]

    def group(i, s):
        return (s[0], big["w_in"][i], w2[i], s[1], s[2], s[3], big["w_out"][i], s[4], big["w_ffn_gate"][i], big["w_ffn_up"][i],
                big["w_ffn_down"][i], s[5])

    return (loss, dx[None], *group(0, sg), *group(1, sd), *group(2, sm2), *group(3, sv2))
```

```python
import functools

import jax
import jax.numpy as jnp
from jax import lax
from jax.experimental import pallas as pl
from jax.experimental.pallas import tpu as pltpu

F32, BF16 = jnp.float32, jnp.bfloat16
HIGHEST = lax.Precision.HIGHEST

D = 2048
HD, NQ, NKV, GRP, WIN = 64, 32, 4, 8, 128
GH, DK, DV, RANK, GC = 4, 256, 512, 16, 64
FH, NDEV = 5632, 8
FS = FH // NDEV
DIN = 12816
WS = DIN // NDEV
EPS = 1e-6
MASKV = -1e30
LANE = 128

C_AQ, C_AK, C_AV, C_GQ, C_GK, C_GV, C_GR, C_GA, C_GB, NMAIN = 0, 2048, 2304, 2560, 3584, 4608, 6656, 8704, 10752, 12800
C_LR = 6656
WSTEP, WWIN = 1600, 1616

LR, B1, B2, AEPS, WD, STEP = 0.001, 0.9, 0.999, 1e-08, 0.01, 10

S_N1, S_GB, S_SK, S_GN, S_N2, S_FN, S_LOSS, SMALL_N = 0, 2048, 3072, 3104, 3616, 5664, 7712, 8192
SMALL_ROWS = SMALL_N // LANE
GW2_ROWS = RANK * GH * DK // LANE
PACK_ROWS = SMALL_ROWS + GW2_ROWS

MESH = pl.DeviceIdType.MESH


def _dot(a, b, ta=False, tb=False, prec=None):
    dn = (((0,) if ta else (1,), (1,) if tb else (0,)), ((), ()))
    return lax.dot_general(a, b, dn, preferred_element_type=F32, precision=prec)


def _sigmoid(x):
    return 1.0 / (1.0 + jnp.exp(-x))


VMEM_LIMIT = 56 * 1024 * 1024


def _cp(*sem):
    return pltpu.CompilerParams(dimension_semantics=sem, vmem_limit_bytes=VMEM_LIMIT)


def _mm(a, b, *, ta=False, tb=False, tm, tn, tk, out_dtype=F32, res=None, after=None, name):
    M, K = (a.shape[1], a.shape[0]) if ta else a.shape
    N = b.shape[0] if tb else b.shape[1]
    tm, tn, tk = min(tm, M), min(tn, N), min(tk, K)
    nk = K // tk
    assert M % tm == 0 and N % tn == 0 and K % tk == 0
    a_spec = pl.BlockSpec((tk, tm), lambda i, j, k: (k, i)) if ta else pl.BlockSpec((tm, tk), lambda i, j, k: (i, k))
    b_spec = pl.BlockSpec((tn, tk), lambda i, j, k: (j, k)) if tb else pl.BlockSpec((tk, tn), lambda i, j, k: (k, j))
    o_spec = pl.BlockSpec((tm, tn), lambda i, j, k: (i, j))
    has_res = res is not None

    def body(*refs):
        a_ref, b_ref = refs[0], refs[1]
        r_ref = refs[2] if has_res else None
        o_ref = refs[2 + has_res + (after is not None)]
        p = _dot(a_ref[...].astype(BF16), b_ref[...].astype(BF16), ta, tb)
        if nk == 1:
            if has_res:
                p = p + r_ref[...]
            o_ref[...] = p.astype(out_dtype)
        else:
            acc = refs[-1]
            k = pl.program_id(2)

            @pl.when(k == 0)
            def _():
                acc[...] = (p + r_ref[...]) if has_res else p

            @pl.when(k > 0)
            def _():
                acc[...] += p

            @pl.when(k == nk - 1)
            def _():
                o_ref[...] = acc[...].astype(out_dtype)

    return pl.pallas_call(
        body, name=name,
        out_shape=jax.ShapeDtypeStruct((M, N), out_dtype),
        grid=(M // tm, N // tn, nk),
        in_specs=[a_spec, b_spec] + ([o_spec] if has_res else []) + ([pl.BlockSpec(memory_space=pl.ANY)] if after is not None else []),
        out_specs=o_spec,
        scratch_shapes=[pltpu.VMEM((tm, tn), F32)] if nk > 1 else [],
        compiler_params=_cp("parallel", "parallel", "arbitrary"),
    )(*((a, b) + ((res,) if has_res else ()) + ((after,) if after is not None else ())))


def _rmsnorm_fwd(x, w, name, tm=256):
    Tn = x.shape[0]

    def body(x_ref, w_ref, o_ref):
        xv = x_ref[...]
        r = lax.rsqrt(jnp.mean(xv * xv, axis=1, keepdims=True) + EPS)
        o_ref[...] = (xv * r * w_ref[...]).astype(BF16)

    return pl.pallas_call(
        body, name=name, out_shape=jax.ShapeDtypeStruct((Tn, D), BF16), grid=(Tn // tm,),
        in_specs=[pl.BlockSpec((tm, D), lambda i: (i, 0)), pl.BlockSpec((1, D), lambda i: (0, 0))],
        out_specs=pl.BlockSpec((tm, D), lambda i: (i, 0)), compiler_params=_cp("parallel"),
    )(x, w)


def _rmsnorm_bwd(dy, h, w, res, name, tm=256):
    Tn = h.shape[0]

    def body(dy_ref, h_ref, w_ref, res_ref, dh_ref, dhb_ref, dw_ref):
        hv, dyv = h_ref[...], dy_ref[...]
        r = lax.rsqrt(jnp.mean(hv * hv, axis=1, keepdims=True) + EPS)
        g = dyv * w_ref[...]
        dh = res_ref[...] + r * g - hv * (r * r * r * jnp.mean(g * hv, axis=1, keepdims=True))
        dh_ref[...] = dh
        dhb_ref[...] = dh.astype(BF16)
        part = jnp.sum(dyv * hv * r, axis=0, keepdims=True)

        @pl.when(pl.program_id(0) == 0)
        def _():
            dw_ref[...] = part

        @pl.when(pl.program_id(0) > 0)
        def _():
            dw_ref[...] += part

    row = pl.BlockSpec((tm, D), lambda i: (i, 0))
    vec = pl.BlockSpec((1, D), lambda i: (0, 0))
    return pl.pallas_call(
        body, name=name,
        out_shape=(jax.ShapeDtypeStruct((Tn, D), F32), jax.ShapeDtypeStruct((Tn, D), BF16), jax.ShapeDtypeStruct((1, D), F32)),
        grid=(Tn // tm,), in_specs=[row, row, vec, row], out_specs=(row, row, vec), compiler_params=_cp("arbitrary"),
    )(dy, h, w, res)


def _loss_head(h2, wf, tgt, name="loss_head", tm=256):
    Tn = h2.shape[0]

    def body(h_ref, w_ref, t_ref, dh_ref, dhb_ref, dw_ref, loss_ref):
        hv, wv = h_ref[...], w_ref[...]
        r = lax.rsqrt(jnp.mean(hv * hv, axis=1, keepdims=True) + EPS)
        hn = hv * r
        e = hn * wv - t_ref[...]
        dy = e * (1.0 / D)
        g = dy * wv
        dh = r * g - hv * (r * r * r * jnp.mean(g * hv, axis=1, keepdims=True))
        dh_ref[...] = dh
        dhb_ref[...] = dh.astype(BF16)
        part = jnp.sum(dy * hn, axis=0, keepdims=True)
        lpart = (0.5 / D) * jnp.sum(jnp.sum(e * e, axis=1, keepdims=True), axis=0, keepdims=True)

        @pl.when(pl.program_id(0) == 0)
        def _():
            dw_ref[...] = part
            loss_ref[...] = lpart

        @pl.when(pl.program_id(0) > 0)
        def _():
            dw_ref[...] += part
            loss_ref[...] += lpart

    row = pl.BlockSpec((tm, D), lambda i: (i, 0))
    vec = pl.BlockSpec((1, D), lambda i: (0, 0))
    one = pl.BlockSpec((1, 1), lambda i: (0, 0))
    return pl.pallas_call(
        body, name=name,
        out_shape=(jax.ShapeDtypeStruct((Tn, D), F32), jax.ShapeDtypeStruct((Tn, D), BF16), jax.ShapeDtypeStruct((1, D), F32),
                   jax.ShapeDtypeStruct((1, 1), F32)),
        grid=(Tn // tm,), in_specs=[row, vec, row], out_specs=(row, row, vec, one), compiler_params=_cp("arbitrary"),
    )(h2, wf, tgt)


def _attn_mask(n):
    qi = lax.broadcasted_iota(jnp.int32, (NKV, GRP * WIN, 2 * WIN), 1) % WIN
    ki = lax.broadcasted_iota(jnp.int32, (NKV, GRP * WIN, 2 * WIN), 2)
    rel = qi + WIN - ki
    return (rel >= 0) & (rel < WIN) & ((n > 0) | (ki >= WIN))


def _kv_heads(prev_ref, cur_ref):
    return jnp.stack([jnp.concatenate([prev_ref[:, h * HD:(h + 1) * HD], cur_ref[:, h * HD:(h + 1) * HD]], axis=0) for h in range(NKV)])


def _q_heads(ref):
    return jnp.stack([jnp.concatenate([ref[:, (h * GRP + g) * HD:(h * GRP + g + 1) * HD] for g in range(GRP)], axis=0) for h in range(NKV)])


def _attn_probs(q_ref, kc_ref, kp_ref, sink_ref, mask):
    kk = _kv_heads(kp_ref, kc_ref).astype(BF16)
    qs = _q_heads(q_ref).astype(BF16)
    s = jnp.einsum('hqd,hkd->hqk', qs, kk, preferred_element_type=F32) * (HD ** -0.5)
    s = jnp.where(mask, s, MASKV)
    sink = jnp.stack([jnp.concatenate([jnp.full((WIN, 1), sink_ref[0, h * GRP + g], F32) for g in range(GRP)], axis=0) for h in range(NKV)])
    m = jnp.maximum(jnp.max(s, axis=2, keepdims=True), sink)
    e = jnp.exp(s - m)
    es = jnp.exp(sink - m)
    inv = 1.0 / (jnp.sum(e, axis=2, keepdims=True) + es)
    return e * inv, es * inv, qs, kk


def _attn_specs(nb, last):
    cur = lambda n: jnp.minimum(n, last)
    prev = lambda n: jnp.maximum(jnp.minimum(n, last) - 1, 0)
    return [
        pl.BlockSpec((WIN, NQ * HD), lambda n: (cur(n), C_AQ // (NQ * HD))),
        pl.BlockSpec((WIN, NKV * HD), lambda n: (cur(n), C_AK // (NKV * HD))),
        pl.BlockSpec((WIN, NKV * HD), lambda n: (prev(n), C_AK // (NKV * HD))),
        pl.BlockSpec((WIN, NKV * HD), lambda n: (cur(n), C_AV // (NKV * HD))),
        pl.BlockSpec((WIN, NKV * HD), lambda n: (prev(n), C_AV // (NKV * HD))),
    ]


def _attn_fwd(proj, sinks, name="attn_fwd"):
    Tn = proj.shape[0]
    nb = Tn // WIN

    def body(q_ref, kc_ref, kp_ref, vc_ref, vp_ref, sink_ref, o_ref):
        p, _, _, _ = _attn_probs(q_ref, kc_ref, kp_ref, sink_ref, _attn_mask(pl.program_id(0)))
        o = jnp.einsum('hqk,hkd->hqd', p.astype(BF16), _kv_heads(vp_ref, vc_ref).astype(BF16), preferred_element_type=F32)
        for h in range(NKV):
            for g in range(GRP):
                o_ref[:, (h * GRP + g) * HD:(h * GRP + g + 1) * HD] = o[h, g * WIN:(g + 1) * WIN, :]

    return pl.pallas_call(
        body, name=name, out_shape=jax.ShapeDtypeStruct((Tn, D), F32), grid=(nb,),
        in_specs=_attn_specs(nb, nb - 1) + [pl.BlockSpec(memory_space=pltpu.SMEM)],
        out_specs=pl.BlockSpec((WIN, D), lambda n: (n, 0)), compiler_params=_cp("parallel"),
    )(proj, proj, proj, proj, proj, sinks)


def _attn_bwd(proj, sinks, o, do, name="attn_bwd"):
    Tn = proj.shape[0]
    nb = Tn // WIN
    KW = NKV * HD

    def body(q_ref, kc_ref, kp_ref, vc_ref, vp_ref, o_ref, do_ref, sink_ref, dq_ref, dkv_ref, dsk_ref, carry, cur):
        n = pl.program_id(0)

        @pl.when(n == 0)
        def _():
            carry[...] = jnp.zeros_like(carry)
            dsk_ref[...] = jnp.zeros_like(dsk_ref)

        @pl.when(n < nb)
        def _():
            p, ps, qs, kk = _attn_probs(q_ref, kc_ref, kp_ref, sink_ref, _attn_mask(n))
            vv = _kv_heads(vp_ref, vc_ref).astype(BF16)
            dos = _q_heads(do_ref)
            delta = jnp.sum(dos * _q_heads(o_ref), axis=2, keepdims=True)
            dosb = dos.astype(BF16)
            dp = jnp.einsum('hqd,hkd->hqk', dosb, vv, preferred_element_type=F32)
            ds = (p * (dp - delta) * (HD ** -0.5)).astype(BF16)
            dq = jnp.einsum('hqk,hkd->hqd', ds, kk, preferred_element_type=F32)
            dkk = jnp.einsum('hqk,hqd->hkd', ds, qs, preferred_element_type=F32)
            dvv = jnp.einsum('hqk,hqd->hkd', p.astype(BF16), dosb, preferred_element_type=F32)
            dsk = ps * delta
            for h in range(NKV):
                for g in range(GRP):
                    i = h * GRP + g
                    dq_ref[:, i * HD:(i + 1) * HD] = dq[h, g * WIN:(g + 1) * WIN, :].astype(BF16)
                    dsk_ref[:, i:i + 1] -= jnp.sum(dsk[h, g * WIN:(g + 1) * WIN, :], axis=0, keepdims=True)
                dkv_ref[:, h * HD:(h + 1) * HD] = (carry[:, h * HD:(h + 1) * HD] + dkk[h, :WIN, :]).astype(BF16)
                dkv_ref[:, KW + h * HD:KW + (h + 1) * HD] = (carry[:, KW + h * HD:KW + (h + 1) * HD] + dvv[h, :WIN, :]).astype(BF16)
                cur[:, h * HD:(h + 1) * HD] = dkk[h, WIN:, :]
                cur[:, KW + h * HD:KW + (h + 1) * HD] = dvv[h, WIN:, :]
            carry[...] = cur[...]

        @pl.when(n == nb)
        def _():
            dkv_ref[...] = carry[...].astype(BF16)

    last = nb - 1
    row = pl.BlockSpec((WIN, D), lambda n: (jnp.minimum(n, last), 0))
    return pl.pallas_call(
        body, name=name,
        out_shape=(jax.ShapeDtypeStruct((Tn, D), BF16), jax.ShapeDtypeStruct((Tn, 2 * KW), BF16), jax.ShapeDtypeStruct((1, NQ), F32)),
        grid=(nb + 1,),
        in_specs=_attn_specs(nb, last) + [row, row, pl.BlockSpec(memory_space=pltpu.SMEM)],
        out_specs=(row, pl.BlockSpec((WIN, 2 * KW), lambda n: (jnp.maximum(n - 1, 0), 0)), pl.BlockSpec((1, NQ), lambda n: (0, 0))),
        scratch_shapes=[pltpu.VMEM((WIN, 2 * KW), F32), pltpu.VMEM((WIN, 2 * KW), F32)],
        compiler_params=_cp("arbitrary"),
    )(proj, proj, proj, proj, proj, o, do, sinks)


def _tri(lower):
    r = lax.broadcasted_iota(jnp.int32, (GC, GC), 0)
    c = lax.broadcasted_iota(jnp.int32, (GC, GC), 1)
    return r >= c if lower else r <= c


def _per_head(a):
    return jnp.stack([a[:, h * DK:(h + 1) * DK] for h in range(GH)])


def _all_heads(a):
    return jnp.concatenate([a[h] for h in range(GH)], axis=1)


def _gla_gates(lr, w2_ref, gb_ref):
    logit = _dot(lr, w2_ref[...].astype(BF16)) + gb_ref[...]
    la = (jnp.minimum(logit, 0.0) - jnp.log(1.0 + jnp.exp(-jnp.abs(logit)))) * (1.0 / 16.0)
    g = _dot(_tri(True).astype(F32), la, prec=HIGHEST)
    return logit, g


def _bmm(spec, a, b):
    return jnp.einsum(spec, a, b, preferred_element_type=F32)


def _gla_specs(nc, rev):
    idx = (lambda n: nc - 1 - n) if rev else (lambda n: n)
    half = 2 * DK
    return (
        [pl.BlockSpec((GC, half), lambda n, j=j: (idx(n), C_GQ // half + j)) for j in range(2)]
        + [pl.BlockSpec((GC, half), lambda n, j=j: (idx(n), C_GK // half + j)) for j in range(2)]
        + [pl.BlockSpec((GC, DV), lambda n, h=h: (idx(n), C_GV // DV + h)) for h in range(GH)]
        + [pl.BlockSpec((GC, LANE), lambda n: (idx(n), 0)), pl.BlockSpec((LANE, GH * DK), lambda n: (0, 0)),
           pl.BlockSpec((1, GH * DK), lambda n: (0, 0))])


def _gla_heads(refs):
    return (lambda h: refs[h // 2][:, (h % 2) * DK:(h % 2 + 1) * DK], lambda h: refs[2 + h // 2][:, (h % 2) * DK:(h % 2 + 1) * DK],
            lambda h: refs[4 + h][...])


def _gla_fwd(proj, plr, w2p, gb, name="gla_fwd"):
    Tn = proj.shape[0]
    nc = Tn // GC

    def body(*refs):
        qh, kh, vh = _gla_heads(refs)
        lr_ref, w2_ref, gb_ref, o_ref, st_ref, S = refs[8:]

        @pl.when(pl.program_id(0) == 0)
        def _():
            S[...] = jnp.zeros_like(S)

        heads = lambda f: jnp.stack([f(h) for h in range(GH)])
        _, g_all = _gla_gates(lr_ref[...].astype(BF16), w2_ref, gb_ref)
        g = _per_head(g_all)
        gl = g[:, GC - 1:GC, :]
        k = heads(kh)
        v = heads(vh).astype(BF16)
        qd = (heads(qh) * (DK ** -0.5) * jnp.exp(g)).astype(BF16)
        ki = (k * jnp.exp(-g)).astype(BF16)
        ke = (k * jnp.exp(gl - g)).astype(BF16)
        att = jnp.where(_tri(True)[None], _bmm('hid,hjd->hij', qd, ki), 0.0).astype(BF16)
        sp = S[...]
        st_ref[0] = sp
        o = _bmm('hij,hjv->hiv', att, v) + _bmm('hid,hvd->hiv', qd, sp.astype(BF16))
        for h in range(GH):
            o_ref[:, h * DV:(h + 1) * DV] = o[h]
        S[...] = sp * jnp.exp(gl) + _bmm('hjv,hjd->hvd', v, ke)

    return pl.pallas_call(
        body, name=name,
        out_shape=(jax.ShapeDtypeStruct((Tn, GH * DV), F32), jax.ShapeDtypeStruct((nc, GH, DV, DK), F32)),
        grid=(nc,), in_specs=_gla_specs(nc, False),
        out_specs=(pl.BlockSpec((GC, GH * DV), lambda n: (n, 0)), pl.BlockSpec((1, GH, DV, DK), lambda n: (n, 0, 0, 0))),
        scratch_shapes=[pltpu.VMEM((GH, DV, DK), F32)], compiler_params=_cp("arbitrary"),
    )(*([proj] * 8), plr, w2p, gb)


def _gla_bwd(proj, plr, w2p, gb, states, do, name="gla_bwd"):
    Tn = proj.shape[0]
    nc = Tn // GC

    def body(*refs):
        qh, kh, vh = _gla_heads(refs)
        lr_ref, w2_ref, gb_ref, st_ref, do_ref, dqk_ref, dv_ref, dlr_ref, dw2_ref, dgb_ref, dS = refs[8:]

        @pl.when(pl.program_id(0) == 0)
        def _():
            dS[...] = jnp.zeros_like(dS)
            dw2_ref[...] = jnp.zeros_like(dw2_ref)
            dgb_ref[...] = jnp.zeros_like(dgb_ref)

        heads = lambda f: jnp.stack([f(h) for h in range(GH)])
        lr = lr_ref[...].astype(BF16)
        causal = _tri(True)[None]
        last_row = lax.broadcasted_iota(jnp.int32, (GH, GC, DK), 1) == GC - 1
        logit, g_all = _gla_gates(lr, w2_ref, gb_ref)
        g = _per_head(g_all)
        gl = g[:, GC - 1:GC, :]
        egl = jnp.exp(gl)
        eg, eng, ege = jnp.exp(g), jnp.exp(-g), jnp.exp(gl - g)
        k = heads(kh)
        v = heads(vh).astype(BF16)
        dob = heads(lambda h: do_ref[:, h * DV:(h + 1) * DV]).astype(BF16)
        qd = heads(qh) * (DK ** -0.5) * eg
        ki = k * eng
        ke = k * ege
        qdb, kib, keb = qd.astype(BF16), ki.astype(BF16), ke.astype(BF16)
        att = jnp.where(causal, _bmm('hid,hjd->hij', qdb, kib), 0.0).astype(BF16)
        datt = jnp.where(causal, _bmm('hiv,hjv->hij', dob, v), 0.0).astype(BF16)
        sp = st_ref[0]
        dsn = dS[...]
        dsnb = dsn.astype(BF16)
        dv = (_bmm('hij,hiv->hjv', att, dob) + _bmm('hjd,hvd->hjv', keb, dsnb)).astype(BF16)
        dqd = _bmm('hij,hjd->hid', datt, kib) + _bmm('hiv,hvd->hid', dob, sp.astype(BF16))
        dki = _bmm('hij,hid->hjd', datt, qdb)
        dke = _bmm('hjv,hvd->hjd', v, dsnb)
        ddec = jnp.sum(dsn * sp, axis=1, keepdims=True)
        dS[...] = dsn * egl + _bmm('hiv,hid->hvd', dob, qdb)
        dke_ke = dke * ke
        dgl = jnp.sum(dke_ke, axis=1, keepdims=True) + ddec * egl
        dg = dqd * qd - dki * ki - dke_ke + jnp.where(last_row, dgl, 0.0)
        dq = (dqd * ((DK ** -0.5) * eg)).astype(BF16)
        dk = (dki * eng + dke * ege).astype(BF16)
        for h in range(GH):
            dv_ref[:, h * DV:(h + 1) * DV] = dv[h]
            dqk_ref[:, h * DK:(h + 1) * DK] = dq[h]
            dqk_ref[:, GH * DK + h * DK:GH * DK + (h + 1) * DK] = dk[h]
        dla = _dot(_tri(False).astype(F32), _all_heads(dg), prec=HIGHEST)
        dlogit = dla * (1.0 / 16.0) * _sigmoid(-logit)
        dlb = dlogit.astype(BF16)
        dlr_ref[...] = _dot(dlb, w2_ref[...].astype(BF16), tb=True).astype(BF16)
        dw2_ref[...] += _dot(lr, dlb, ta=True)
        dgb_ref[...] += jnp.sum(dlogit, axis=0, keepdims=True)

    rev = lambda n: nc - 1 - n
    row = pl.BlockSpec((GC, GH * DV), lambda n: (rev(n), 0))
    return pl.pallas_call(
        body, name=name,
        out_shape=(jax.ShapeDtypeStruct((Tn, 2 * GH * DK), BF16), jax.ShapeDtypeStruct((Tn, GH * DV), BF16),
                   jax.ShapeDtypeStruct((Tn, LANE), BF16), jax.ShapeDtypeStruct((LANE, GH * DK), F32),
                   jax.ShapeDtypeStruct((1, GH * DK), F32)),
        grid=(nc,),
        in_specs=_gla_specs(nc, True) + [pl.BlockSpec((1, GH, DV, DK), lambda n: (rev(n), 0, 0, 0)), row],
        out_specs=(row, row, pl.BlockSpec((GC, LANE), lambda n: (rev(n), 0)), pl.BlockSpec((LANE, GH * DK), lambda n: (0, 0)),
                   pl.BlockSpec((1, GH * DK), lambda n: (0, 0))),
        scratch_shapes=[pltpu.VMEM((GH, DV, DK), F32)], compiler_params=_cp("arbitrary"),
    )(*([proj] * 8), plr, w2p, gb, states, do)


def _merge_specs(tm):
    row = pl.BlockSpec((tm, D), lambda i: (i, 0))
    gates = [pl.BlockSpec((tm, DV), lambda i, j=c // DV + h: (i, j)) for c in (C_GR, C_GA, C_GB) for h in range(GH)]
    return row, gates, pl.BlockSpec((1, DV), lambda i: (0, 0))


def _merge_fwd(a, go, proj, gnw, name="merge_fwd", tm=256):
    Tn = a.shape[0]

    def body(a_ref, go_ref, *rest):
        gates, w_ref, m_ref = rest[:3 * GH], rest[3 * GH], rest[3 * GH + 1]
        for h in range(GH):
            sl = slice(h * DV, (h + 1) * DV)
            gov = go_ref[:, sl]
            r = lax.rsqrt(jnp.mean(gov * gov, axis=1, keepdims=True) + EPS)
            gr = gates[h][...]
            g2 = gov * r * w_ref[...] * (gr * _sigmoid(gr))
            m_ref[:, sl] = (_sigmoid(gates[GH + h][...]) * a_ref[:, sl] + _sigmoid(gates[2 * GH + h][...]) * g2).astype(BF16)

    row, gates, vec = _merge_specs(tm)
    return pl.pallas_call(
        body, name=name, out_shape=jax.ShapeDtypeStruct((Tn, D), BF16), grid=(Tn // tm,),
        in_specs=[row, row] + gates + [vec], out_specs=row, compiler_params=_cp("parallel"),
    )(a, go, *([proj] * (3 * GH)), gnw)


def _merge_bwd(dm, a, go, proj, gnw, name="merge_bwd", tm=256):
    Tn = a.shape[0]

    def body(dm_ref, a_ref, go_ref, *rest):
        gates = rest[:3 * GH]
        w_ref, da_ref, dgo_ref, dg_ref, dw_ref = rest[3 * GH:]
        wv = w_ref[...]
        dw = jnp.zeros((1, DV), F32)
        for h in range(GH):
            sl = slice(h * DV, (h + 1) * DV)
            dmv, av, gov, gr = dm_ref[:, sl], a_ref[:, sl], go_ref[:, sl], gates[h][...]
            sa, sb, sg = _sigmoid(gates[GH + h][...]), _sigmoid(gates[2 * GH + h][...]), _sigmoid(gr)
            r = lax.rsqrt(jnp.mean(gov * gov, axis=1, keepdims=True) + EPS)
            gn0 = gov * r
            gn = gn0 * wv
            silu = gr * sg
            dg2 = dmv * sb
            da_ref[:, sl] = dmv * sa
            dg_ref[:, D + h * DV:D + (h + 1) * DV] = (dmv * av * sa * (1.0 - sa)).astype(BF16)
            dg_ref[:, 2 * D + h * DV:2 * D + (h + 1) * DV] = (dg2 * gn * silu * (1.0 - sb)).astype(BF16)
            dg_ref[:, sl] = (dg2 * gn * (sg * (1.0 + gr * (1.0 - sg)))).astype(BF16)
            dgn = dg2 * silu
            dw = dw + jnp.sum(dgn * gn0, axis=0, keepdims=True)
            gg = dgn * wv
            dgo_ref[:, sl] = r * gg - gov * (r * r * r * jnp.mean(gg * gov, axis=1, keepdims=True))

        @pl.when(pl.program_id(0) == 0)
        def _():
            dw_ref[...] = dw

        @pl.when(pl.program_id(0) > 0)
        def _():
            dw_ref[...] += dw

    row, gates, vec = _merge_specs(tm)
    return pl.pallas_call(
        body, name=name,
        out_shape=(jax.ShapeDtypeStruct((Tn, D), F32), jax.ShapeDtypeStruct((Tn, D), F32), jax.ShapeDtypeStruct((Tn, 3 * D), BF16),
                   jax.ShapeDtypeStruct((1, DV), F32)),
        grid=(Tn // tm,), in_specs=[row, row, row] + gates + [vec],
        out_specs=(row, row, pl.BlockSpec((tm, 3 * D), lambda i: (i, 0)), vec), compiler_params=_cp("arbitrary"),
    )(dm, a, go, *([proj] * (3 * GH)), gnw)


def _ffn_up(v2, wgt, wut, name="ffn_up", tm=1024, tn=512):
    Tn = v2.shape[0]
    tm = min(tm, Tn)

    def body(v_ref, wg_ref, wu_ref, a_ref, b_ref, ff_ref):
        vv = v_ref[...]
        a = _dot(vv, wg_ref[...], tb=True)
        b = _dot(vv, wu_ref[...], tb=True)
        a_ref[...] = a.astype(BF16)
        b_ref[...] = b.astype(BF16)
        ff_ref[...] = (a * _sigmoid(a) * b).astype(BF16)

    w = pl.BlockSpec((tn, D), lambda j, i: (j, 0))
    act = pl.BlockSpec((tm, tn), lambda j, i: (i, j))
    return pl.pallas_call(
        body, name=name,
        out_shape=(jax.ShapeDtypeStruct((Tn, FH), BF16), jax.ShapeDtypeStruct((Tn, FH), BF16), jax.ShapeDtypeStruct((Tn, FH), BF16)),
        grid=(FH // tn, Tn // tm), in_specs=[pl.BlockSpec((tm, D), lambda j, i: (i, 0)), w, w], out_specs=(act, act, act),
        compiler_params=_cp("parallel", "parallel"),
    )(v2, wgt, wut)


def _ffn_dact(dh2b, wd, a, b, name="ffn_dact", tm=1024, tn=512):
    Tn = dh2b.shape[0]
    tm = min(tm, Tn)

    def body(d_ref, w_ref, a_ref, b_ref, da_ref, db_ref):
        dff = _dot(d_ref[...], w_ref[...], tb=True)
        av = a_ref[...].astype(F32)
        sg = _sigmoid(av)
        da_ref[...] = (dff * b_ref[...].astype(F32) * (sg * (1.0 + av * (1.0 - sg)))).astype(BF16)
        db_ref[...] = (dff * (av * sg)).astype(BF16)

    act = pl.BlockSpec((tm, tn), lambda j, i: (i, j))
    return pl.pallas_call(
        body, name=name,
        out_shape=(jax.ShapeDtypeStruct((Tn, FH), BF16), jax.ShapeDtypeStruct((Tn, FH), BF16)),
        grid=(FH // tn, Tn // tm),
        in_specs=[pl.BlockSpec((tm, D), lambda j, i: (i, 0)), pl.BlockSpec((tn, D), lambda j, i: (j, 0)), act, act],
        out_specs=(act, act), compiler_params=_cp("parallel", "parallel"),
    )(dh2b, wd, a, b)


def _adam_math(w, g, m, v):
    m2 = B1 * m + (1.0 - B1) * g
    v2 = B2 * v + (1.0 - B2) * (g * g)
    mh = m2 / (1.0 - B1 ** STEP)
    vh = v2 / (1.0 - B2 ** STEP)
    return -LR * (mh / (jnp.sqrt(vh) + AEPS) + WD * w), m2, v2


def _sum_blocks(o_ref, p_ref):
    g = o_ref[...].astype(F32)
    for j in range(p_ref.shape[0]):
        g = g + p_ref[j].astype(F32)
    return g


def _adamw(w, m, v, psums, parts, chip_idx, name, tr):
    R, C = w.shape

    def body(s_ref, w_ref, m_ref, v_ref, o_ref, p_ref, g_ref, d_ref, m2_ref, v2_ref):
        g = _sum_blocks(o_ref, p_ref)
        d, m2, v2 = _adam_math(w_ref[...], g, m_ref[...], v_ref[...])
        g_ref[...] = g
        d_ref[...] = d
        m2_ref[...] = m2
        v2_ref[...] = v2

    blk = pl.BlockSpec((tr, C), lambda i, s: (i, 0))
    out = jax.ShapeDtypeStruct((R, C), F32)
    grid_spec = pltpu.PrefetchScalarGridSpec(
        num_scalar_prefetch=1, grid=(R // tr,),
        in_specs=[blk, blk, blk, pl.BlockSpec((None, tr, C), lambda i, s: (s[0], i, 0)),
                  pl.BlockSpec((parts.shape[0], tr, C), lambda i, s: (0, i, 0))],
        out_specs=(blk, blk, blk, blk),
    )
    return pl.pallas_call(body, name=name, out_shape=(out, out, out, out), grid_spec=grid_spec, compiler_params=_cp("parallel"),
                          )(chip_idx, w, m, v, psums, parts)


def _adamw_rows(w, m, v, g, name, tr):
    R = w.shape[0]

    def body(w_ref, m_ref, v_ref, g_ref, d_ref, m2_ref, v2_ref):
        d, m2, v2 = _adam_math(w_ref[...], g_ref[...], m_ref[...], v_ref[...])
        d_ref[...] = d
        m2_ref[...] = m2
        v2_ref[...] = v2

    blk = pl.BlockSpec((tr,) + w.shape[1:], lambda i: (i, 0, 0))
    out = jax.ShapeDtypeStruct(w.shape, F32)
    return pl.pallas_call(body, name=name, out_shape=(out, out, out), grid=(R // tr,), in_specs=[blk] * 4, out_specs=(blk, blk, blk),
                          compiler_params=_cp("parallel"))(w, m, v, g)


def _sum_parts(psums, parts, chip_idx, name, tr, tc):
    _, R, C = psums.shape

    def body(s_ref, o_ref, p_ref, g_ref):
        g_ref[...] = _sum_blocks(o_ref, p_ref)

    grid_spec = pltpu.PrefetchScalarGridSpec(
        num_scalar_prefetch=1, grid=(R // tr, C // tc),
        in_specs=[pl.BlockSpec((None, tr, tc), lambda i, j, s: (s[0], i, j)),
                  pl.BlockSpec((parts.shape[0], tr, tc), lambda i, j, s: (0, i, j))],
        out_specs=pl.BlockSpec((tr, tc), lambda i, j, s: (i, j)),
    )
    return pl.pallas_call(body, name=name, out_shape=jax.ShapeDtypeStruct((R, C), F32), grid_spec=grid_spec,
                          compiler_params=_cp("parallel", "parallel"))(chip_idx, psums, parts)


def _adamw_plain(w, m, v, g, name):
    def body(w_ref, m_ref, v_ref, g_ref, d_ref, m2_ref, v2_ref):
        d, m2, v2 = _adam_math(w_ref[...], g_ref[...], m_ref[...], v_ref[...])
        d_ref[...] = d
        m2_ref[...] = m2
        v2_ref[...] = v2

    out = jax.ShapeDtypeStruct(w.shape, F32)
    return pl.pallas_call(body, name=name, out_shape=(out, out, out))(w, m, v, g)


def _sum_devices(pack_all, name="sum_small"):
    def body(p_ref, o_ref):
        s = p_ref[0]
        for k in range(1, NDEV):
            s = s + p_ref[k]
        o_ref[...] = s

    return pl.pallas_call(body, name=name, out_shape=jax.ShapeDtypeStruct(pack_all.shape[1:], F32))(pack_all)


def _pair_add(g5, recv, c_idx, name, tr):
    _, _, R, C = g5.shape

    def body(c_ref, g_ref, r_ref, o_ref):
        o_ref[...] = (g_ref[...].astype(F32) + r_ref[...].astype(F32)).astype(BF16)

    grid_spec = pltpu.PrefetchScalarGridSpec(
        num_scalar_prefetch=1, grid=(4, R // tr),
        in_specs=[pl.BlockSpec((None, None, tr, C), lambda q, i, c: (q, c[0], i, 0)), pl.BlockSpec((None, tr, C), lambda q, i, c: (q, i, 0))],
        out_specs=pl.BlockSpec((None, tr, C), lambda q, i, c: (q, i, 0)),
    )
    return pl.pallas_call(
        body, name=name, out_shape=jax.ShapeDtypeStruct((4, R, C), BF16), grid_spec=grid_spec,
        compiler_params=_cp("parallel", "parallel"),
    )(c_idx, g5, recv)


_ANY = pl.BlockSpec(memory_space=pl.ANY)


def _mesh_pos():
    x, y, c = lax.axis_index("x"), lax.axis_index("y"), lax.axis_index("c")
    return x, y, c, [(1 - x, y), (x, 1 - y), (1 - x, 1 - y)]


def _gather_small(pack, name="gather_small"):
    def body(pk, pk_all, psend, precv, loc):
        x, y, c, chips = _mesh_pos()
        me_slot = 4 * x + 2 * y + c
        sib = (x, y, 1 - c)
        own = pltpu.make_async_copy(pk, pk_all.at[me_slot], loc)
        own.start()
        peers = [sib] + [(*chip, c) for chip in chips] + [(*chip, 1 - c) for chip in chips]
        small = [pltpu.make_async_remote_copy(src_ref=pk, dst_ref=pk_all.at[me_slot], send_sem=psend.at[k], recv_sem=precv.at[k],
                                              device_id=p, device_id_type=MESH) for k, p in enumerate(peers)]
        for d in small:
            d.start()
        for k, p in enumerate(peers):
            pltpu.make_async_remote_copy(src_ref=pk, dst_ref=pk_all.at[4 * p[0] + 2 * p[1] + p[2]], send_sem=psend.at[k],
                                         recv_sem=precv.at[k], device_id=p, device_id_type=MESH).wait_recv()
        for d in small:
            d.wait_send()
        own.wait()

    return pl.pallas_call(
        body, name=name, out_shape=jax.ShapeDtypeStruct((NDEV,) + pack.shape, pack.dtype), in_specs=[_ANY], out_specs=_ANY,
        scratch_shapes=[pltpu.SemaphoreType.DMA((7,)), pltpu.SemaphoreType.DMA((7,)), pltpu.SemaphoreType.DMA(())],
    )(pack)


def _main_row(g):
    return g if g < C_LR else g - RANK


def _window_pieces(lo, hi):
    out = []
    for a, b, where in ((lo, min(hi, C_LR), "main"), (max(lo, C_LR), min(hi, C_LR + RANK), "lr"), (max(lo, C_LR + RANK), hi, "main")):
        if a < b:
            out.append((a, b, where, _main_row(a) if where == "main" else a - C_LR))
    return out


def _assemble_w_in(windows, own, name="assemble_w_in"):
    edges = NDEV - 1

    def body(b_ref, own_ref, main_ref, lr_ref, buf, ebuf, in_sems, out_sems, esems):
        dev = 4 * lax.axis_index("x") + 2 * lax.axis_index("y") + lax.axis_index("c")

        def load(k):
            return pltpu.make_async_copy(b_ref.at[k], buf.at[k % 2], in_sems.at[k % 2])

        def start_load(k):
            pl.when(dev == k)(pltpu.make_async_copy(own_ref, buf.at[k % 2], in_sems.at[k % 2]).start)
            pl.when(dev != k)(load(k).start)

        lr_ref[RANK:, :] = jnp.zeros((LANE - RANK, D), BF16)
        start_load(0)
        pending, edge_out = [], []
        for k in range(NDEV):
            s = k % 2
            load(k).wait()
            if k:
                ebuf[k - 1] = buf[1 - s, WSTEP:WWIN, :] + buf[s, 0:16, :]
                edge_out.append(pltpu.make_async_copy(ebuf.at[k - 1], main_ref.at[pl.ds(_main_row(WSTEP * k), 16)], esems.at[k - 1]))
                edge_out[-1].start()
                for d in pending:
                    d.wait()
            if k + 1 < NDEV:
                start_load(k + 1)
            pending = []
            lo = WSTEP * k + (16 if k else 0)
            hi = WSTEP * k + (WWIN if k == NDEV - 1 else WSTEP)
            for a, b, where, dst in _window_pieces(lo, hi):
                if where == "lr":
                    lr_ref[dst:dst + b - a, :] = buf[s, a - WSTEP * k:b - WSTEP * k, :]
                else:
                    pending.append(pltpu.make_async_copy(buf.at[s, pl.ds(a - WSTEP * k, b - a)], main_ref.at[pl.ds(dst, b - a)],
                                                         out_sems.at[2 * s + len(pending)]))
                    pending[-1].start()
        for d in pending + edge_out:
            d.wait()

    return pl.pallas_call(
        body, name=name,
        out_shape=(jax.ShapeDtypeStruct((NMAIN, D), BF16), jax.ShapeDtypeStruct((LANE, D), BF16)),
        in_specs=[_ANY, _ANY], out_specs=(_ANY, pl.BlockSpec(memory_space=pltpu.VMEM)),
        scratch_shapes=[pltpu.VMEM((2, WWIN, D), BF16), pltpu.VMEM((edges, 16, D), BF16), pltpu.SemaphoreType.DMA((2,)),
                        pltpu.SemaphoreType.DMA((4,)), pltpu.SemaphoreType.DMA((edges,))],
        compiler_params=pltpu.CompilerParams(vmem_limit_bytes=VMEM_LIMIT),
    )(windows, own)


def _disassemble_exchange(d_main, d_lr, d_wo, name="disassemble_exchange"):
    def body(main_ref, lr_ref, wo_ref, mine_ref, recv_ref, worecv_ref, buf, in_sems, keep_sems, send_sems, recv_sems, wo_sems):
        x, y, c, _ = _mesh_pos()
        sib = (x, y, 1 - c)
        wo = _rcopy(wo_ref.at[:, 1 - c], worecv_ref, wo_sems.at[0], wo_sems.at[1], sib)
        wo.start()

        def loads(k):
            s, out = k % 2, []
            for a, b, where, src0 in _window_pieces(WSTEP * k, WSTEP * k + WWIN):
                if where == "main":
                    out.append(pltpu.make_async_copy(main_ref.at[pl.ds(src0, b - a)], buf.at[s, pl.ds(a - WSTEP * k, b - a)],
                                                     in_sems.at[2 * s + len(out)]))
            return out

        def keep(k):
            return pltpu.make_async_copy(buf.at[k % 2], mine_ref.at[k // 2], keep_sems.at[k % 2])

        def send(k):
            return _rcopy(buf.at[k % 2], recv_ref.at[k // 2], send_sems.at[k % 2], recv_sems.at[k // 2], sib)

        def store_start(k):
            pl.when(c == k % 2)(keep(k).start)
            pl.when(c != k % 2)(send(k).start)

        def store_wait(k):
            pl.when(c == k % 2)(keep(k).wait)
            pl.when(c != k % 2)(send(k).wait_send)

        for d in loads(0):
            d.start()
        for k in range(NDEV):
            for d in loads(k):
                d.wait()
            for a, b, where, src0 in _window_pieces(WSTEP * k, WSTEP * k + WWIN):
                if where == "lr":
                    buf[k % 2, a - WSTEP * k:b - WSTEP * k, :] = lr_ref[src0:src0 + b - a, :]
            if k:
                store_wait(k - 1)
            if k + 1 < NDEV:
                for d in loads(k + 1):
                    d.start()
            store_start(k)
        store_wait(NDEV - 1)
        for chip in range(NDEV // 2):
            _rcopy(buf.at[0], recv_ref.at[chip], send_sems.at[0], recv_sems.at[chip], sib).wait_recv()
        wo.wait_send()
        wo.wait_recv()

    half = jax.ShapeDtypeStruct((NDEV // 2, WWIN, D), BF16)
    return pl.pallas_call(
        body, name=name, out_shape=(half, half, jax.ShapeDtypeStruct((NDEV // 2,) + d_wo.shape[2:], BF16)),
        in_specs=[_ANY, pl.BlockSpec(memory_space=pltpu.VMEM), _ANY], out_specs=(_ANY, _ANY, _ANY),
        scratch_shapes=[pltpu.VMEM((2, WWIN, D), BF16), pltpu.SemaphoreType.DMA((4,)), pltpu.SemaphoreType.DMA((2,)),
                        pltpu.SemaphoreType.DMA((2,)), pltpu.SemaphoreType.DMA((NDEV // 2,)), pltpu.SemaphoreType.DMA((2,))],
        compiler_params=pltpu.CompilerParams(vmem_limit_bytes=VMEM_LIMIT),
    )(d_main, d_lr, d_wo)


def _add_blocks(a, b, name, tr):
    _, R, C = a.shape

    def body(a_ref, b_ref, o_ref):
        o_ref[...] = (a_ref[...].astype(F32) + b_ref[...].astype(F32)).astype(BF16)

    blk = pl.BlockSpec((None, tr, C), lambda q, i: (q, i, 0))
    return pl.pallas_call(body, name=name, out_shape=jax.ShapeDtypeStruct(a.shape, BF16), grid=(a.shape[0], R // tr),
                          in_specs=[blk, blk], out_specs=blk, compiler_params=_cp("parallel", "parallel"))(a, b)


_HBM = pl.BlockSpec(memory_space=pltpu.HBM)
_SEM = pl.BlockSpec(memory_space=pltpu.SEMAPHORE)
_VMEM = pl.BlockSpec(memory_space=pltpu.VMEM)
_SIDE = pltpu.CompilerParams(has_side_effects=pltpu.SideEffectType.DATAFLOW_SIDE_EFFECTING)
_TOKEN = jax.ShapeDtypeStruct((8, LANE), F32)


def _hbm(a):
    return pltpu.with_memory_space_constraint(a, pltpu.HBM)


def _hbm_like(arrs):
    return tuple(pltpu.HBM(a.shape, a.dtype) for a in arrs)


def _tie(x, token):
    return x + token[0, 0].astype(x.dtype)


def _chip_copies(ins, lands, send, recv, nrel):
    x, y, c, chips = _mesh_pos()
    first = [sum(nrel[:a]) for a in range(len(ins))]
    return [pltpu.make_async_remote_copy(src_ref=ins[a].at[2 * chip[0] + chip[1]], dst_ref=lands[a].at[j], send_sem=send.at[first[a] + j],
                                         recv_sem=recv.at[first[a] + j], device_id=(*chip, c), device_id_type=MESH)
            for a in range(len(ins)) for j, chip in enumerate(chips[:nrel[a]])]


def _chip_start(psums, name, nrel=None):
    n = len(psums)
    nrel = nrel or [3] * n
    lands = [lax.empty((r,) + p.shape[1:], p.dtype) for r, p in zip(nrel, psums)]

    def body(*refs):
        for d in _chip_copies(refs[:n], refs[n:2 * n], refs[2 * n], refs[2 * n + 1], nrel):
            d.start()
        refs[-1][...] = jnp.zeros_like(refs[-1])

    sems = pltpu.SemaphoreType.DMA((sum(nrel),))
    out = pl.pallas_call(
        body, name=name, out_shape=(sems, sems) + _hbm_like(psums) + _hbm_like(lands) + (_TOKEN,),
        in_specs=[_HBM] * (2 * n), out_specs=(_SEM, _SEM) + (_HBM,) * (2 * n) + (_VMEM,),
        input_output_aliases={i: 2 + i for i in range(2 * n)}, compiler_params=_SIDE,
    )(*[_hbm(a) for a in list(psums) + lands])
    return out[0], out[1], list(out[2:2 + n]), list(out[2 + n:2 + 2 * n]), out[-1]


def _chip_wait(send, recv, psums, lands, after, name):
    n = len(psums)
    nrel = [l.shape[0] for l in lands]

    def body(*refs):
        for d in _chip_copies(refs[:n], refs[n:2 * n], refs[2 * n], refs[2 * n + 1], nrel):
            d.wait_send()
            d.wait_recv()

    out = pl.pallas_call(
        body, name=name, out_shape=_hbm_like(psums) + _hbm_like(lands),
        in_specs=[_HBM] * (2 * n) + [_SEM, _SEM, _ANY], out_specs=(_HBM,) * (2 * n),
        input_output_aliases={i: i for i in range(2 * n)}, compiler_params=_SIDE,
    )(*psums, *lands, send, recv, after)
    return list(out[:n]), list(out[n:])


def _hop_pos():
    x, y, c, _ = _mesh_pos()
    north = c == 1
    via = (jnp.where(north, 1 - x, x), jnp.where(north, y, 1 - y))
    return (*via, c), 2 * (1 - x) + (1 - y), jnp.where(north, 2 * x + (1 - y), 2 * (1 - x) + y)


def _hop_copies(ins, lands, send, recv):
    to, mine, _ = _hop_pos()
    return [pltpu.make_async_remote_copy(src_ref=ins[a].at[mine], dst_ref=lands[a], send_sem=send.at[a], recv_sem=recv.at[a],
                                         device_id=to, device_id_type=MESH) for a in range(len(ins))]


def _hop_start(psums, name):
    n = len(psums)
    lands = [lax.empty(p.shape[1:], p.dtype) for p in psums]

    def body(*refs):
        for d in _hop_copies(refs[:n], refs[n:2 * n], refs[2 * n], refs[2 * n + 1]):
            d.start()
        refs[-1][...] = jnp.zeros_like(refs[-1])

    sems = pltpu.SemaphoreType.DMA((n,))
    out = pl.pallas_call(
        body, name=name, out_shape=(sems, sems) + _hbm_like(psums) + _hbm_like(lands) + (_TOKEN,),
        in_specs=[_HBM] * (2 * n), out_specs=(_SEM, _SEM) + (_HBM,) * (2 * n) + (_VMEM,),
        input_output_aliases={i: 2 + i for i in range(2 * n)}, compiler_params=_SIDE,
    )(*[_hbm(a) for a in list(psums) + lands])
    return out[0], out[1], list(out[2:2 + n]), list(out[2 + n:2 + 2 * n]), out[-1]


def _hop_wait(send, recv, psums, lands, after, name):
    n = len(psums)

    def body(*refs):
        for d in _hop_copies(refs[:n], refs[n:2 * n], refs[2 * n], refs[2 * n + 1]):
            d.wait_send()
            d.wait_recv()

    out = pl.pallas_call(
        body, name=name, out_shape=_hbm_like(psums) + _hbm_like(lands),
        in_specs=[_HBM] * (2 * n) + [_SEM, _SEM, _ANY], out_specs=(_HBM,) * (2 * n),
        input_output_aliases={i: i for i in range(2 * n)}, compiler_params=_SIDE,
    )(*psums, *lands, send, recv, after)
    return list(out[:n]), list(out[n:])


def _hop_add(psums, land, idx, name, tr):
    _, R, C = psums.shape

    def body(s_ref, p_ref, l_ref, o_ref):
        o_ref[...] = (p_ref[...].astype(F32) + l_ref[...].astype(F32)).astype(BF16)

    blk = pl.BlockSpec((None, tr, C), lambda i, s: (s[0], i, 0))
    grid_spec = pltpu.PrefetchScalarGridSpec(num_scalar_prefetch=1, grid=(R // tr,),
                                             in_specs=[blk, pl.BlockSpec((tr, C), lambda i, s: (i, 0))], out_specs=blk)
    return pl.pallas_call(body, name=name, out_shape=jax.ShapeDtypeStruct(psums.shape, BF16), grid_spec=grid_spec,
                          input_output_aliases={1: 0}, compiler_params=_cp("parallel"))(idx, psums, land)


def _pair_copies(ins, lands, send, recv):
    x, y, c, _ = _mesh_pos()
    return [pltpu.make_async_remote_copy(src_ref=ins[a].at[:, 1 - c], dst_ref=lands[a], send_sem=send.at[a], recv_sem=recv.at[a],
                                         device_id=(x, y, 1 - c), device_id_type=MESH) for a in range(len(ins))]


def _pair_start(grads, name):
    n = len(grads)
    lands = [lax.empty((4,) + g.shape[2:], g.dtype) for g in grads]

    def body(*refs):
        for d in _pair_copies(refs[:n], refs[n:2 * n], refs[2 * n], refs[2 * n + 1]):
            d.start()
        refs[-1][...] = jnp.zeros_like(refs[-1])

    sems = pltpu.SemaphoreType.DMA((n,))
    out = pl.pallas_call(
        body, name=name, out_shape=(sems, sems) + _hbm_like(grads) + _hbm_like(lands) + (_TOKEN,),
        in_specs=[_HBM] * (2 * n), out_specs=(_SEM, _SEM) + (_HBM,) * (2 * n) + (_VMEM,),
        input_output_aliases={i: 2 + i for i in range(2 * n)}, compiler_params=_SIDE,
    )(*[_hbm(a) for a in list(grads) + lands])
    return out[0], out[1], list(out[2:2 + n]), list(out[2 + n:2 + 2 * n]), out[-1]


def _pair_wait(send, recv, grads, lands, after, name):
    n = len(grads)

    def body(*refs):
        for d in _pair_copies(refs[:n], refs[n:2 * n], refs[2 * n], refs[2 * n + 1]):
            d.wait_send()
            d.wait_recv()

    out = pl.pallas_call(
        body, name=name, out_shape=_hbm_like(grads) + _hbm_like(lands),
        in_specs=[_HBM] * (2 * n) + [_SEM, _SEM, _ANY], out_specs=(_HBM,) * (2 * n),
        input_output_aliases={i: i for i in range(2 * n)}, compiler_params=_SIDE,
    )(*grads, *lands, send, recv, after)
    return list(out[:n]), list(out[n:])


def _slot(chip, c):
    return 4 * chip[0] + 2 * chip[1] + c


def _gather_start(shards, lands, after, name):
    n = len(shards)

    def body(*refs):
        src, land, send, recv = refs[:n], refs[n:2 * n], refs[2 * n + 1], refs[2 * n + 2]
        x, y, c, chips = _mesh_pos()
        for a in range(n):
            for k, to in enumerate([(x, y, 1 - c)] + [(*chip, c) for chip in chips]):
                pltpu.make_async_remote_copy(src_ref=src[a], dst_ref=land[a].at[_slot((x, y), c)], send_sem=send.at[4 * a + k],
                                             recv_sem=recv.at[4 * a + k], device_id=to, device_id_type=MESH).start()
        refs[-1][...] = jnp.zeros_like(refs[-1])

    sems = pltpu.SemaphoreType.DMA((4 * n,))
    out = pl.pallas_call(
        body, name=name, out_shape=(sems, sems) + _hbm_like(shards) + _hbm_like(lands) + (_TOKEN,),
        in_specs=[_HBM] * (2 * n) + [_ANY], out_specs=(_SEM, _SEM) + (_HBM,) * (2 * n) + (_VMEM,),
        input_output_aliases={i: 2 + i for i in range(2 * n)}, compiler_params=_SIDE,
    )(*[_hbm(a) for a in list(shards) + list(lands)], after)
    return out[0], out[1], list(out[2:2 + n]), list(out[2 + n:2 + 2 * n]), out[-1]


def _gather_pass(lands, recv, after, name, first=0):
    n = len(lands)

    def body(*refs):
        land, recv1 = refs[:n], refs[n]
        send2, recv2 = refs[n + 2], refs[n + 3]
        x, y, c, chips = _mesh_pos()
        for a in range(n):
            for j, chip in enumerate(chips):
                blk = land[a].at[_slot(chip, c)]
                pltpu.make_async_remote_copy(src_ref=blk, dst_ref=blk, send_sem=send2.at[3 * a + j], recv_sem=recv1.at[4 * (first + a) + 1 + j],
                                             device_id=(*chip, c), device_id_type=MESH).wait_recv()
                pltpu.make_async_remote_copy(src_ref=blk, dst_ref=blk, send_sem=send2.at[3 * a + j], recv_sem=recv2.at[3 * a + j],
                                             device_id=(x, y, 1 - c), device_id_type=MESH).start()
        refs[-1][...] = jnp.zeros_like(refs[-1])

    sems = pltpu.SemaphoreType.DMA((3 * n,))
    out = pl.pallas_call(
        body, name=name, out_shape=(sems, sems) + _hbm_like(lands) + (_TOKEN,),
        in_specs=[_HBM] * n + [_SEM, _ANY], out_specs=(_SEM, _SEM) + (_HBM,) * n + (_VMEM,),
        input_output_aliases={i: 2 + i for i in range(n)}, compiler_params=_SIDE,
    )(*lands, recv, after)
    return out[0], out[1], list(out[2:2 + n]), out[-1]


def _gather_wait(shards, lands, send, recv, send2, recv2, after, name, first=0):
    n = len(lands)

    def body(*refs):
        src, land = refs[:n], refs[n:2 * n]
        send1, recv1, snd2, rcv2 = refs[2 * n:2 * n + 4]
        x, y, c, chips = _mesh_pos()
        sib = (x, y, 1 - c)
        for a in range(n):
            for k in range(4):
                pltpu.make_async_remote_copy(src_ref=src[a], dst_ref=land[a].at[_slot((x, y), c)], send_sem=send1.at[4 * (first + a) + k],
                                             recv_sem=recv1.at[4 * (first + a) + k], device_id=sib, device_id_type=MESH).wait_send()
            blk = land[a].at[_slot((x, y), 1 - c)]
            pltpu.make_async_remote_copy(src_ref=blk, dst_ref=blk, send_sem=send1.at[4 * (first + a)], recv_sem=recv1.at[4 * (first + a)],
                                         device_id=sib, device_id_type=MESH).wait_recv()
            for j, chip in enumerate(chips):
                mine, theirs = land[a].at[_slot(chip, c)], land[a].at[_slot(chip, 1 - c)]
                pltpu.make_async_remote_copy(src_ref=mine, dst_ref=mine, send_sem=snd2.at[3 * a + j], recv_sem=rcv2.at[3 * a + j],
                                             device_id=sib, device_id_type=MESH).wait_send()
                pltpu.make_async_remote_copy(src_ref=theirs, dst_ref=theirs, send_sem=snd2.at[3 * a + j], recv_sem=rcv2.at[3 * a + j],
                                             device_id=sib, device_id_type=MESH).wait_recv()

    out = pl.pallas_call(
        body, name=name, out_shape=_hbm_like(shards) + _hbm_like(lands),
        in_specs=[_HBM] * (2 * n) + [_SEM] * 4 + [_ANY], out_specs=(_HBM,) * (2 * n),
        input_output_aliases={i: i for i in range(2 * n)}, compiler_params=_SIDE,
    )(*shards, *lands, send, recv, send2, recv2, after)
    return list(out[n:])


def _win_tree():
    x, y, c, chips = _mesh_pos()
    north = c == 1
    handed = (jnp.where(north, 1 - x, x), jnp.where(north, y, 1 - y))
    hand_to = (jnp.where(north, x, 1 - x), jnp.where(north, 1 - y, y))
    return x, y, c, chips, handed, hand_to


def _blk(land, chip, c):
    return land.at[_slot(chip, c)]


def _rcopy(src, dst, send, recv, to):
    return pltpu.make_async_remote_copy(src_ref=src, dst_ref=dst, send_sem=send, recv_sem=recv, device_id=to, device_id_type=MESH)


def _win_start(shards, lands, name):
    n = len(shards)

    def body(*refs):
        src, land, send, recv = refs[:n], refs[n:2 * n], refs[2 * n], refs[2 * n + 1]
        x, y, c, chips, _, _ = _win_tree()
        for a in range(n):
            for k, to in enumerate([(x, y, 1 - c), (*chips[0], c), (*chips[1], c)]):
                _rcopy(src[a], _blk(land[a], (x, y), c), send.at[3 * a + k], recv.at[3 * a + k], to).start()
        refs[-1][...] = jnp.zeros_like(refs[-1])

    sems = pltpu.SemaphoreType.DMA((3 * n,))
    out = pl.pallas_call(
        body, name=name, out_shape=(sems, sems) + _hbm_like(shards) + _hbm_like(lands) + (_TOKEN,),
        in_specs=[_HBM] * (2 * n), out_specs=(_SEM, _SEM) + (_HBM,) * (2 * n) + (_VMEM,),
        input_output_aliases={i: 2 + i for i in range(2 * n)}, compiler_params=_SIDE,
    )(*[_hbm(a) for a in list(shards) + list(lands)])
    return out[0], out[1], list(out[2:2 + n]), list(out[2 + n:2 + 2 * n]), out[-1]


def _win_hand_on(lands, recv1, after, name):
    n, m = len(lands), len(after)

    def body(*refs):
        land, rcv1 = refs[:n], refs[n]
        send2, recv2 = refs[n + 1 + m], refs[n + 2 + m]
        x, y, c, chips, handed, hand_to = _win_tree()
        for a in range(n):
            for j in range(2):
                blk = _blk(land[a], chips[j], c)
                _rcopy(blk, blk, send2.at[3 * a], rcv1.at[3 * a + 1 + j], (*chips[j], c)).wait_recv()
            blk = _blk(land[a], handed, c)
            _rcopy(blk, blk, send2.at[3 * a], recv2.at[3 * a], (*hand_to, c)).start()
            for j in range(2):
                blk = _blk(land[a], chips[j], c)
                _rcopy(blk, blk, send2.at[3 * a + 1 + j], recv2.at[3 * a + 1 + j], (x, y, 1 - c)).start()
        refs[-1][...] = jnp.zeros_like(refs[-1])

    sems = pltpu.SemaphoreType.DMA((3 * n,))
    out = pl.pallas_call(
        body, name=name, out_shape=(sems, sems) + _hbm_like(lands) + (_TOKEN,),
        in_specs=[_HBM] * n + [_SEM] + [_ANY] * m, out_specs=(_SEM, _SEM) + (_HBM,) * n + (_VMEM,),
        input_output_aliases={i: 2 + i for i in range(n)}, compiler_params=_SIDE,
    )(*lands, recv1, *after)
    return out[0], out[1], list(out[2:2 + n]), out[-1]


def _win_last(lands, recv2, after, name):
    n, m = len(lands), len(after)

    def body(*refs):
        land, rcv2 = refs[:n], refs[n]
        send3, recv3 = refs[n + 1 + m], refs[n + 2 + m]
        x, y, c, chips, _, hand_to = _win_tree()
        for a in range(n):
            blk = _blk(land[a], chips[2], c)
            _rcopy(blk, blk, send3.at[a], rcv2.at[3 * a], (*hand_to, c)).wait_recv()
            _rcopy(blk, blk, send3.at[a], recv3.at[a], (x, y, 1 - c)).start()
        refs[-1][...] = jnp.zeros_like(refs[-1])

    sems = pltpu.SemaphoreType.DMA((n,))
    out = pl.pallas_call(
        body, name=name, out_shape=(sems, sems) + _hbm_like(lands) + (_TOKEN,),
        in_specs=[_HBM] * n + [_SEM] + [_ANY] * m, out_specs=(_SEM, _SEM) + (_HBM,) * n + (_VMEM,),
        input_output_aliases={i: 2 + i for i in range(n)}, compiler_params=_SIDE,
    )(*lands, recv2, *after)
    return out[0], out[1], list(out[2:2 + n]), out[-1]


def _win_wait(shards, lands, sems1, sems2, sems3, after, name):
    n = len(lands)

    def body(*refs):
        src, land = refs[:n], refs[n:2 * n]
        send1, recv1, send2, recv2, send3, recv3 = refs[2 * n:2 * n + 6]
        x, y, c, chips, handed, hand_to = _win_tree()
        sib = (x, y, 1 - c)
        for a in range(n):
            own = _blk(land[a], (x, y), c)
            for k in range(3):
                _rcopy(src[a], own, send1.at[3 * a + k], recv1.at[3 * a + k], sib).wait_send()
            blk = _blk(land[a], (x, y), 1 - c)
            _rcopy(blk, blk, send1.at[3 * a], recv1.at[3 * a], sib).wait_recv()
            blk = _blk(land[a], handed, c)
            _rcopy(blk, blk, send2.at[3 * a], recv2.at[3 * a], sib).wait_send()
            for j in range(2):
                mine, theirs = _blk(land[a], chips[j], c), _blk(land[a], chips[j], 1 - c)
                _rcopy(mine, mine, send2.at[3 * a + 1 + j], recv2.at[3 * a + 1 + j], sib).wait_send()
                _rcopy(theirs, theirs, send2.at[3 * a + 1 + j], recv2.at[3 * a + 1 + j], sib).wait_recv()
            mine, theirs = _blk(land[a], chips[2], c), _blk(land[a], chips[2], 1 - c)
            _rcopy(mine, mine, send3.at[a], recv3.at[a], sib).wait_send()
            _rcopy(theirs, theirs, send3.at[a], recv3.at[a], sib).wait_recv()

    out = pl.pallas_call(
        body, name=name, out_shape=_hbm_like(shards) + _hbm_like(lands),
        in_specs=[_HBM] * (2 * n) + [_SEM] * 6 + [_ANY], out_specs=(_HBM,) * (2 * n),
        input_output_aliases={i: i for i in range(2 * n)}, compiler_params=_SIDE,
    )(*shards, *lands, *sems1, *sems2, *sems3, after)
    return list(out[n:])


def _pad_to(v, n):
    return jnp.pad(v, [(0, 0)] * (v.ndim - 1) + [(0, n - v.shape[-1])])


def _pack_small(n1, gb, sk, gn, n2, fn, extra=None):
    parts = [n1.reshape(-1), gb.reshape(-1), sk.reshape(-1), gn.reshape(-1), n2.reshape(-1), fn.reshape(-1)]
    flat = jnp.concatenate(parts + ([extra.reshape(-1)] if extra is not None else []))
    return _pad_to(flat, SMALL_N).reshape(SMALL_ROWS, LANE)


def _unpack_small(p):
    f = p.reshape(-1)
    return (f[S_N1:S_GB].reshape(1, D), f[S_GB:S_SK].reshape(1, GH * DK), f[S_SK:S_GN].reshape(1, NQ), f[S_GN:S_N2].reshape(1, DV),
            f[S_N2:S_FN].reshape(1, D), f[S_FN:S_LOSS].reshape(D))


class _NoComm:
    def __init__(self, wo, wg_all, wu_all, wd_all):
        self.rest = (wo, wg_all, wu_all, wd_all)

    def mixed(self, gla_o, gla_norm_w):
        return gla_norm_w

    def w_out(self, merged, norm2_w):
        return self.rest[0], norm2_w

    def w_up(self, v2):
        return self.rest[1], self.rest[2]

    def w_down(self, ff):
        return self.rest[3]

    def ffn_grads(self, d_wg, d_wu, d_wd):
        self.ffn = (d_wg, d_wu, d_wd)

    def ffn_reduce(self, dv2, norm2_w):
        return norm2_w

    def in_grads(self, d_wmain, d_wlr, d_wo, w_lr):
        self.inw = (d_wmain, d_wlr, d_wo)
        return w_lr

    def in_reduce(self, du_lr):
        return None


class _Comm:
    def __init__(self, rest_shards, rest_lands, after, c_idx):
        self.c_idx = c_idx
        self.send, self.recv, self.shards, self.lands, self.token = _gather_start(rest_shards, rest_lands, after, "gather_rest_start")

    def _pass(self, lo, hi, after, tag):
        send2, recv2, lands, token = _gather_pass(self.lands[lo:hi], self.recv, after, "gather_pass_" + tag, first=lo)
        self.passed = (lo, hi, send2, recv2, lands)
        return token

    def _wait(self, after, tag):
        lo, hi, send2, recv2, lands = self.passed
        return _gather_wait(self.shards[lo:hi], lands, self.send, self.recv, send2, recv2, after, "gather_wait_" + tag, first=lo)

    def mixed(self, gla_o, gla_norm_w):
        return _tie(gla_norm_w, self._pass(0, 1, gla_o, "out"))

    def w_out(self, merged, norm2_w):
        (wo_all,) = self._wait(merged, "out")
        return wo_all.reshape(D, D), _tie(norm2_w, self._pass(1, 3, merged, "up"))

    def w_up(self, v2):
        wg_all, wu_all = self._wait(v2, "up")
        self._pass(3, 4, v2, "down")
        return wg_all.reshape(FH, D), wu_all.reshape(FH, D)

    def w_down(self, ff):
        return self._wait(ff, "down")[0].reshape(FH, D)

    def _reduce(self, tag, names, grads, recv1, rows):
        psums = [_pair_add(g, r, self.c_idx, "pair_add_" + nm, tr) for g, r, nm, tr in zip(grads, recv1, names, rows)]
        *flight, token = _chip_start(psums, "reduce_chips_start_" + tag)
        return dict(tag=tag, names=names, rows=rows, flight=flight), token

    def ffn_grads(self, d_wg, d_wu, d_wd):
        self.ffn_pair = _pair_start([d.reshape(4, 2, FS, D) for d in (d_wg, d_wu, d_wd)], "reduce_pair_start_ffn")
        return self.ffn_pair[-1]

    def ffn_reduce(self, dv2, norm2_w):
        send, recv, grads, lands, _ = self.ffn_pair
        grads, recv1 = _pair_wait(send, recv, grads, lands, dv2, "reduce_pair_wait_ffn")
        self.ffn, token = self._reduce("ffn", ["w_ffn_gate", "w_ffn_up", "w_ffn_down"], grads, recv1, [176, 176, 176])
        return _tie(norm2_w, token)

    def in_grads(self, d_wmain, d_wlr, d_wo, w_lr):
        d_wo4 = d_wo.reshape(4, 2, D // NDEV, D)
        mine, recv, wo_recv = _disassemble_exchange(d_wmain, d_wlr, d_wo4)
        self.in_names, self.in_rows = ["w_in", "w_out"], [808, 256]
        self.wo_psum = _pair_add(d_wo4, wo_recv, self.c_idx, "pair_add_w_out", 256)
        *self.in_hop, token = _hop_start([_add_blocks(mine, recv, "pair_add_w_in", 808)], "reduce_hop_start_in")
        return _tie(w_lr, token)

    def in_reduce(self, du_lr):
        (psum,), (land,) = _hop_wait(*self.in_hop, self.update(self.ffn, du_lr), "reduce_hop_wait_in")
        psum = _hop_add(psum, land, _hop_pos()[2].astype(jnp.int32).reshape(1), "hop_add_w_in", 808)
        *flight, token = _chip_start([psum, self.wo_psum], "reduce_chips_start_in", nrel=[2, 3])
        self.inw = dict(tag="in", names=self.in_names, rows=self.in_rows, flight=flight)
        return token


def _local_step(xs, tgt, u, norm1_w, gla_gate_b, attn_sinks, gla_norm_w, norm2_w, fnw, w_main, w_lr, w2p, comm):
    proj =_mm(u, w_main, tb=True, tm=1024, tn=1280, tk=D, name="in_proj")
    plr = _mm(u, w_lr, tb=True, tm=1024, tn=LANE, tk=D, name="in_proj_lr")
    attn_o = _attn_fwd(proj, attn_sinks)
    gla_o, states = _gla_fwd(proj, plr, w2p, gla_gate_b)
    merged = _merge_fwd(attn_o, gla_o, proj, comm.mixed(gla_o, gla_norm_w))
    wo, norm2_w = comm.w_out(merged, norm2_w)
    h1 = _mm(merged, wo, tm=1024, tn=512, tk=D, res=xs, name="out_proj")
    v2 = _rmsnorm_fwd(h1, norm2_w, "norm2_fwd")
    wg_all, wu_all = comm.w_up(v2)
    fa, fb, ff = _ffn_up(v2, wg_all, wu_all)
    wd_all = comm.w_down(ff)
    h2 = _mm(ff, wd_all, tm=1024, tn=1024, tk=FH // 2, res=h1, name="ffn_down")
    dh2, dh2b, d_fnw, loss_part = _loss_head(h2, fnw, tgt)

    da, db = _ffn_dact(dh2b, wd_all, fa, fb)
    Tn = xs.shape[0]
    d_wd = _mm(ff, dh2b, ta=True, tm=512, tn=D, tk=Tn, out_dtype=BF16, name="ffn_dwd")
    d_wg = _mm(da, v2, ta=True, tm=512, tn=D, tk=Tn, out_dtype=BF16, name="ffn_dwg")
    d_wu = _mm(db, v2, ta=True, tm=512, tn=D, tk=Tn, out_dtype=BF16, name="ffn_dwu")
    dv2 = _mm(da, wg_all, tm=1024, tn=1024, tk=FH // 2, after=comm.ffn_grads(d_wg, d_wu, d_wd), name="ffn_dv2_gate")
    dv2 = _mm(db, wu_all, tm=1024, tn=1024, tk=FH // 2, res=dv2, name="ffn_dv2_up")
    norm2_w = comm.ffn_reduce(dv2, norm2_w)
    dh1, dh1b, d_n2 = _rmsnorm_bwd(dv2, h1, norm2_w, dh2, "norm2_bwd")
    dmerged = _mm(dh1b, wo, tb=True, tm=1024, tn=512, tk=D, name="out_proj_dx")
    d_wo = _mm(merged, dh1b, ta=True, tm=1024, tn=512, tk=xs.shape[0], out_dtype=BF16, name="out_proj_dw")
    d_attn, d_gla, d_gates, d_gnw = _merge_bwd(dmerged, attn_o, gla_o, proj, gla_norm_w)
    d_q, d_kv, d_sinks = _attn_bwd(proj, attn_sinks, attn_o, d_attn)
    d_gqk, d_gv, d_plr, d_w2p, d_gb = _gla_bwd(proj, plr, w2p, gla_gate_b, states, d_gla)
    dproj = jnp.concatenate([d_q, d_kv, d_gqk, d_gv, d_gates], axis=1)
    d_wmain = _mm(dproj, u, ta=True, tm=640, tn=D, tk=xs.shape[0], out_dtype=BF16, name="in_proj_dw")
    d_wlr = _mm(d_plr, u, ta=True, tm=LANE, tn=1024, tk=xs.shape[0], out_dtype=BF16, name="in_proj_lr_dw")
    du_lr = _mm(d_plr, comm.in_grads(d_wmain, d_wlr, d_wo, w_lr), tm=1024, tn=1024, tk=LANE, name="in_proj_lr_dx")
    du = _mm(dproj, w_main, tm=1024, tn=1024, tk=2560, res=du_lr, after=comm.in_reduce(du_lr), name="in_proj_dx")
    dx, _, d_n1 = _rmsnorm_bwd(du, xs, norm1_w, dh1, "norm1_bwd")
    return dx, loss_part, d_w2p, d_gb, d_sinks, d_gnw, d_n1, d_n2, d_fnw


def kernel(x, norm1_w, w_in, gla_gate_w2, gla_gate_b, attn_sinks, gla_norm_w, w_out, norm2_w, w_ffn_gate, w_ffn_up, w_ffn_down, final_norm_w, loss_target, m_norm1_w, m_w_in, m_gla_gate_w2, m_gla_gate_b, m_attn_sinks, m_gla_norm_w, m_w_out, m_norm2_w, m_w_ffn_gate, m_w_ffn_up, m_w_ffn_down, m_final_norm_w, v_norm1_w, v_w_in, v_gla_gate_w2, v_gla_gate_b, v_attn_sinks, v_gla_norm_w, v_w_out, v_norm2_w, v_w_ffn_gate, v_w_ffn_up, v_w_ffn_down, v_final_norm_w):
    xs, tgt = x[0], loss_target[0]
    fnw = final_norm_w.reshape(1, D)
    c_idx = lax.axis_index("c").astype(jnp.int32).reshape(1)
    dev = 4 * lax.axis_index("x") + 2 * lax.axis_index("y") + lax.axis_index("c")

    chip_idx = (2 * lax.axis_index("x") + lax.axis_index("y")).astype(jnp.int32).reshape(1)

    shift = (WS - WSTEP) * dev
    window = lax.dynamic_update_slice(jnp.zeros((WWIN, D), BF16), jnp.transpose(w_in[0]).astype(BF16), (shift, 0))
    w2_land = lax.dynamic_update_slice(lax.empty((NDEV, RANK, LANE), F32), gla_gate_w2, (dev, 0, 0))
    *sems1, win_srcs, win_lands, tok = _win_start([window, gla_gate_w2[0]], [lax.empty((NDEV, WWIN, D), BF16), w2_land], "gather_in_start")
    tr2 = lambda t: jnp.transpose(t[0])
    rows3 = lambda t: jnp.transpose(t[0] + tok[0, 0]).reshape(WS, D // LANE, LANE)
    rest = [(w + tok[0, 0]).astype(BF16) for w in (w_out[0], tr2(w_ffn_gate), tr2(w_ffn_up), w_ffn_down[0])]
    rest_lands = [lax.dynamic_update_slice(lax.empty((NDEV,) + s.shape, s.dtype), s[None], (dev, 0, 0)) for s in rest]
    win3 = [rows3(t) for t in (w_in, m_w_in, v_w_in)]
    *sems2, win_lands, tok = _win_hand_on(win_lands, sems1[1], rest + rest_lands + win3, "gather_in_hand_on")
    u = _rmsnorm_fwd(xs, _tie(norm1_w, tok), "norm1_fwd")
    *sems3, win_lands, tok = _win_last(win_lands, sems2[1], [u], "gather_in_last")
    comm = _Comm(rest, rest_lands, tok, c_idx)
    win_all, w2_all = _win_wait(win_srcs, win_lands, sems1, sems2, sems3, comm.token, "gather_in_wait")
    w_main, w_lr = _assemble_w_in(win_all, window)
    w2p = jnp.pad(jnp.transpose(w2_all, (1, 0, 2)).reshape(RANK, GH * DK), ((0, LANE - RANK), (0, 0)))

    big = {}

    def update(grp, after):
        psums, parts = _chip_wait(*grp["flight"], after, "reduce_chips_wait_" + grp["tag"])
        for nm, ps, pt, tr in zip(grp["names"], psums, parts, grp["rows"]):
            w, m, v = {"w_in": (w_in, m_w_in, v_w_in), "w_out": (w_out, m_w_out, v_w_out), "w_ffn_gate": (w_ffn_gate, m_w_ffn_gate, v_w_ffn_gate),
                       "w_ffn_up": (w_ffn_up, m_w_ffn_up, v_w_ffn_up), "w_ffn_down": (w_ffn_down, m_w_ffn_down, v_w_ffn_down)}[nm]
            if nm == "w_in":
                g_win = _sum_parts(ps, pt, chip_idx, "sum_w_in", tr, 1024)
                g3 = lax.dynamic_slice(g_win, (shift, 0), (WS, D)).reshape(WS, D // LANE, LANE)
                out3 = (g3,) + tuple(_adamw_rows(*win3, g3, "adamw_w_in", 178))
                big[nm] = [jnp.transpose(t.reshape(WS, D))[None] for t in out3]
            elif nm in ("w_ffn_gate", "w_ffn_up"):
                big[nm] = [jnp.transpose(t)[None] for t in _adamw(tr2(w), tr2(m), tr2(v), ps, pt, chip_idx, "adamw_" + nm, tr)]
            else:
                big[nm] = [t[None] for t in _adamw(w[0], m[0], v[0], ps, pt, chip_idx, "adamw_" + nm, tr)]
            after = big[nm][0]
        return after

    comm.update = update
    dx, loss_part, d_w2p, d_gb, d_sinks, d_gnw, d_n1, d_n2, d_fnw = _local_step(
        xs, tgt, u, norm1_w, gla_gate_b, attn_sinks, gla_norm_w, norm2_w, fnw, w_main, w_lr, w2p, comm)

    pack = jnp.concatenate([_pack_small(d_n1, d_gb, d_sinks, d_gnw, d_n2, d_fnw, loss_part),
                            d_w2p[:RANK].reshape(GW2_ROWS, LANE)], axis=0)
    small = _sum_devices(_gather_small(pack))

    update(comm.inw, dx)
    g_small = small[:SMALL_ROWS]
    sm = _adamw_plain(_pack_small(norm1_w, gla_gate_b, attn_sinks, gla_norm_w, norm2_w, final_norm_w),
                      _pack_small(m_norm1_w, m_gla_gate_b, m_attn_sinks, m_gla_norm_w, m_norm2_w, m_final_norm_w),
                      _pack_small(v_norm1_w, v_gla_gate_b, v_attn_sinks, v_gla_norm_w, v_norm2_w, v_final_norm_w), g_small, "adamw_small")
    g_w2 = lax.dynamic_slice_in_dim(small[SMALL_ROWS:].reshape(RANK, GH * DK), dev * LANE, LANE, axis=1)
    w2 = [g_w2[None]] + [t[None] for t in _adamw_plain(gla_gate_w2[0], m_gla_gate_w2[0], v_gla_gate_w2[0], g_w2, "adamw_w2")]
    loss = g_small.reshape(-1)[S_LOSS]

    sg, sd, sm2, sv2 = [_unpack_small(t) for t in (g_small,) + tuple(sm)]

    def group(i, s):
        return (s[0], big["w_in"][i], w2[i], s[1], s[2], s[3], big["w_out"][i], s[4], big["w_ffn_gate"][i], big["w_ffn_up"][i],
                big["w_ffn_down"][i], s[5])

    return (loss, dx[None], *group(0, sg), *group(1, sd), *group(2, sm2), *group(3, sv2))
```

```python
import functools

import jax
import jax.numpy as jnp
from jax import lax
from jax.experimental import pallas as pl
from jax.experimental.pallas import tpu as pltpu

F32, BF16 = jnp.float32, jnp.bfloat16
HIGHEST = lax.Precision.HIGHEST

D = 2048
HD, NQ, NKV, GRP, WIN = 64, 32, 4, 8, 128
GH, DK, DV, RANK, GC = 4, 256, 512, 16, 64
FH, NDEV = 5632, 8
FS = FH // NDEV
DIN = 12816
WS = DIN // NDEV
EPS = 1e-6
MASKV = -1e30
LANE = 128

C_AQ, C_AK, C_AV, C_GQ, C_GK, C_GV, C_GR, C_GA, C_GB, NMAIN = 0, 2048, 2304, 2560, 3584, 4608, 6656, 8704, 10752, 12800
C_LR = 6656
WSTEP, WWIN = 1600, 1616

LR, B1, B2, AEPS, WD, STEP = 0.001, 0.9, 0.999, 1e-08, 0.01, 10

S_N1, S_GB, S_SK, S_GN, S_N2, S_FN, S_LOSS, SMALL_N = 0, 2048, 3072, 3104, 3616, 5664, 7712, 8192
SMALL_ROWS = SMALL_N // LANE
GW2_ROWS = RANK * GH * DK // LANE
PACK_ROWS = SMALL_ROWS + GW2_ROWS

MESH = pl.DeviceIdType.MESH


def _dot(a, b, ta=False, tb=False, prec=None):
    dn = (((0,) if ta else (1,), (1,) if tb else (0,)), ((), ()))
    return lax.dot_general(a, b, dn, preferred_element_type=F32, precision=prec)


def _sigmoid(x):
    return 1.0 / (1.0 + jnp.exp(-x))


VMEM_LIMIT = 56 * 1024 * 1024


def _cp(*sem):
    return pltpu.CompilerParams(dimension_semantics=sem, vmem_limit_bytes=VMEM_LIMIT)


def _mm(a, b, *, ta=False, tb=False, tm, tn, tk, out_dtype=F32, res=None, after=None, name):
    M, K = (a.shape[1], a.shape[0]) if ta else a.shape
    N = b.shape[0] if tb else b.shape[1]
    tm, tn, tk = min(tm, M), min(tn, N), min(tk, K)
    nk = K // tk
    assert M % tm == 0 and N % tn == 0 and K % tk == 0
    a_spec = pl.BlockSpec((tk, tm), lambda i, j, k: (k, i)) if ta else pl.BlockSpec((tm, tk), lambda i, j, k: (i, k))
    b_spec = pl.BlockSpec((tn, tk), lambda i, j, k: (j, k)) if tb else pl.BlockSpec((tk, tn), lambda i, j, k: (k, j))
    o_spec = pl.BlockSpec((tm, tn), lambda i, j, k: (i, j))
    has_res = res is not None

    def body(*refs):
        a_ref, b_ref = refs[0], refs[1]
        r_ref = refs[2] if has_res else None
        o_ref = refs[2 + has_res + (after is not None)]
        p = _dot(a_ref[...].astype(BF16), b_ref[...].astype(BF16), ta, tb)
        if nk == 1:
            if has_res:
                p = p + r_ref[...]
            o_ref[...] = p.astype(out_dtype)
        else:
            acc = refs[-1]
            k = pl.program_id(2)

            @pl.when(k == 0)
            def _():
                acc[...] = (p + r_ref[...]) if has_res else p

            @pl.when(k > 0)
            def _():
                acc[...] += p

            @pl.when(k == nk - 1)
            def _():
                o_ref[...] = acc[...].astype(out_dtype)

    return pl.pallas_call(
        body, name=name,
        out_shape=jax.ShapeDtypeStruct((M, N), out_dtype),
        grid=(M // tm, N // tn, nk),
        in_specs=[a_spec, b_spec] + ([o_spec] if has_res else []) + ([pl.BlockSpec(memory_space=pl.ANY)] if after is not None else []),
        out_specs=o_spec,
        scratch_shapes=[pltpu.VMEM((tm, tn), F32)] if nk > 1 else [],
        compiler_params=_cp("parallel", "parallel", "arbitrary"),
    )(*((a, b) + ((res,) if has_res else ()) + ((after,) if after is not None else ())))


def _rmsnorm_fwd(x, w, name, tm=256):
    Tn = x.shape[0]

    def body(x_ref, w_ref, o_ref):
        xv = x_ref[...]
        r = lax.rsqrt(jnp.mean(xv * xv, axis=1, keepdims=True) + EPS)
        o_ref[...] = (xv * r * w_ref[...]).astype(BF16)

    return pl.pallas_call(
        body, name=name, out_shape=jax.ShapeDtypeStruct((Tn, D), BF16), grid=(Tn // tm,),
        in_specs=[pl.BlockSpec((tm, D), lambda i: (i, 0)), pl.BlockSpec((1, D), lambda i: (0, 0))],
        out_specs=pl.BlockSpec((tm, D), lambda i: (i, 0)), compiler_params=_cp("parallel"),
    )(x, w)


def _rmsnorm_bwd(dy, h, w, res, name, tm=256):
    Tn = h.shape[0]

    def body(dy_ref, h_ref, w_ref, res_ref, dh_ref, dhb_ref, dw_ref):
        hv, dyv = h_ref[...], dy_ref[...]
        r = lax.rsqrt(jnp.mean(hv * hv, axis=1, keepdims=True) + EPS)
        g = dyv * w_ref[...]
        dh = res_ref[...] + r * g - hv * (r * r * r * jnp.mean(g * hv, axis=1, keepdims=True))
        dh_ref[...] = dh
        dhb_ref[...] = dh.astype(BF16)
        part = jnp.sum(dyv * hv * r, axis=0, keepdims=True)

        @pl.when(pl.program_id(0) == 0)
        def _():
            dw_ref[...] = part

        @pl.when(pl.program_id(0) > 0)
        def _():
            dw_ref[...] += part

    row = pl.BlockSpec((tm, D), lambda i: (i, 0))
    vec = pl.BlockSpec((1, D), lambda i: (0, 0))
    return pl.pallas_call(
        body, name=name,
        out_shape=(jax.ShapeDtypeStruct((Tn, D), F32), jax.ShapeDtypeStruct((Tn, D), BF16), jax.ShapeDtypeStruct((1, D), F32)),
        grid=(Tn // tm,), in_specs=[row, row, vec, row], out_specs=(row, row, vec), compiler_params=_cp("arbitrary"),
    )(dy, h, w, res)


def _loss_head(h2, wf, tgt, name="loss_head", tm=256):
    Tn = h2.shape[0]

    def body(h_ref, w_ref, t_ref, dh_ref, dhb_ref, dw_ref, loss_ref):
        hv, wv = h_ref[...], w_ref[...]
        r = lax.rsqrt(jnp.mean(hv * hv, axis=1, keepdims=True) + EPS)
        hn = hv * r
        e = hn * wv - t_ref[...]
        dy = e * (1.0 / D)
        g = dy * wv
        dh = r * g - hv * (r * r * r * jnp.mean(g * hv, axis=1, keepdims=True))
        dh_ref[...] = dh
        dhb_ref[...] = dh.astype(BF16)
        part = jnp.sum(dy * hn, axis=0, keepdims=True)
        lpart = (0.5 / D) * jnp.sum(jnp.sum(e * e, axis=1, keepdims=True), axis=0, keepdims=True)

        @pl.when(pl.program_id(0) == 0)
        def _():
            dw_ref[...] = part
            loss_ref[...] = lpart

        @pl.when(pl.program_id(0) > 0)
        def _():
            dw_ref[...] += part
            loss_ref[...] += lpart

    row = pl.BlockSpec((tm, D), lambda i: (i, 0))
    vec = pl.BlockSpec((1, D), lambda i: (0, 0))
    one = pl.BlockSpec((1, 1), lambda i: (0, 0))
    return pl.pallas_call(
        body, name=name,
        out_shape=(jax.ShapeDtypeStruct((Tn, D), F32), jax.ShapeDtypeStruct((Tn, D), BF16), jax.ShapeDtypeStruct((1, D), F32),
                   jax.ShapeDtypeStruct((1, 1), F32)),
        grid=(Tn // tm,), in_specs=[row, vec, row], out_specs=(row, row, vec, one), compiler_params=_cp("arbitrary"),
    )(h2, wf, tgt)


def _attn_mask(n):
    qi = lax.broadcasted_iota(jnp.int32, (NKV, GRP * WIN, 2 * WIN), 1) % WIN
    ki = lax.broadcasted_iota(jnp.int32, (NKV, GRP * WIN, 2 * WIN), 2)
    rel = qi + WIN - ki
    return (rel >= 0) & (rel < WIN) & ((n > 0) | (ki >= WIN))


def _kv_heads(prev_ref, cur_ref):
    return jnp.stack([jnp.concatenate([prev_ref[:, h * HD:(h + 1) * HD], cur_ref[:, h * HD:(h + 1) * HD]], axis=0) for h in range(NKV)])


def _q_heads(ref):
    return jnp.stack([jnp.concatenate([ref[:, (h * GRP + g) * HD:(h * GRP + g + 1) * HD] for g in range(GRP)], axis=0) for h in range(NKV)])


def _attn_probs(q_ref, kc_ref, kp_ref, sink_ref, mask):
    kk = _kv_heads(kp_ref, kc_ref).astype(BF16)
    qs = _q_heads(q_ref).astype(BF16)
    s = jnp.einsum('hqd,hkd->hqk', qs, kk, preferred_element_type=F32) * (HD ** -0.5)
    s = jnp.where(mask, s, MASKV)
    sink = jnp.stack([jnp.concatenate([jnp.full((WIN, 1), sink_ref[0, h * GRP + g], F32) for g in range(GRP)], axis=0) for h in range(NKV)])
    m = jnp.maximum(jnp.max(s, axis=2, keepdims=True), sink)
    e = jnp.exp(s - m)
    es = jnp.exp(sink - m)
    inv = 1.0 / (jnp.sum(e, axis=2, keepdims=True) + es)
    return e * inv, es * inv, qs, kk


def _attn_specs(nb, last):
    cur = lambda n: jnp.minimum(n, last)
    prev = lambda n: jnp.maximum(jnp.minimum(n, last) - 1, 0)
    return [
        pl.BlockSpec((WIN, NQ * HD), lambda n: (cur(n), C_AQ // (NQ * HD))),
        pl.BlockSpec((WIN, NKV * HD), lambda n: (cur(n), C_AK // (NKV * HD))),
        pl.BlockSpec((WIN, NKV * HD), lambda n: (prev(n), C_AK // (NKV * HD))),
        pl.BlockSpec((WIN, NKV * HD), lambda n: (cur(n), C_AV // (NKV * HD))),
        pl.BlockSpec((WIN, NKV * HD), lambda n: (prev(n), C_AV // (NKV * HD))),
    ]


def _attn_fwd(proj, sinks, name="attn_fwd"):
    Tn = proj.shape[0]
    nb = Tn // WIN

    def body(q_ref, kc_ref, kp_ref, vc_ref, vp_ref, sink_ref, o_ref):
        p, _, _, _ = _attn_probs(q_ref, kc_ref, kp_ref, sink_ref, _attn_mask(pl.program_id(0)))
        o = jnp.einsum('hqk,hkd->hqd', p.astype(BF16), _kv_heads(vp_ref, vc_ref).astype(BF16), preferred_element_type=F32)
        for h in range(NKV):
            for g in range(GRP):
                o_ref[:, (h * GRP + g) * HD:(h * GRP + g + 1) * HD] = o[h, g * WIN:(g + 1) * WIN, :]

    return pl.pallas_call(
        body, name=name, out_shape=jax.ShapeDtypeStruct((Tn, D), F32), grid=(nb,),
        in_specs=_attn_specs(nb, nb - 1) + [pl.BlockSpec(memory_space=pltpu.SMEM)],
        out_specs=pl.BlockSpec((WIN, D), lambda n: (n, 0)), compiler_params=_cp("parallel"),
    )(proj, proj, proj, proj, proj, sinks)


def _attn_bwd(proj, sinks, o, do, name="attn_bwd"):
    Tn = proj.shape[0]
    nb = Tn // WIN
    KW = NKV * HD

    def body(q_ref, kc_ref, kp_ref, vc_ref, vp_ref, o_ref, do_ref, sink_ref, dq_ref, dkv_ref, dsk_ref, carry, cur):
        n = pl.program_id(0)

        @pl.when(n == 0)
        def _():
            carry[...] = jnp.zeros_like(carry)
            dsk_ref[...] = jnp.zeros_like(dsk_ref)

        @pl.when(n < nb)
        def _():
            p, ps, qs, kk = _attn_probs(q_ref, kc_ref, kp_ref, sink_ref, _attn_mask(n))
            vv = _kv_heads(vp_ref, vc_ref).astype(BF16)
            dos = _q_heads(do_ref)
            delta = jnp.sum(dos * _q_heads(o_ref), axis=2, keepdims=True)
            dosb = dos.astype(BF16)
            dp = jnp.einsum('hqd,hkd->hqk', dosb, vv, preferred_element_type=F32)
            ds = (p * (dp - delta) * (HD ** -0.5)).astype(BF16)
            dq = jnp.einsum('hqk,hkd->hqd', ds, kk, preferred_element_type=F32)
            dkk = jnp.einsum('hqk,hqd->hkd', ds, qs, preferred_element_type=F32)
            dvv = jnp.einsum('hqk,hqd->hkd', p.astype(BF16), dosb, preferred_element_type=F32)
            dsk = ps * delta
            for h in range(NKV):
                for g in range(GRP):
                    i = h * GRP + g
                    dq_ref[:, i * HD:(i + 1) * HD] = dq[h, g * WIN:(g + 1) * WIN, :].astype(BF16)
                    dsk_ref[:, i:i + 1] -= jnp.sum(dsk[h, g * WIN:(g + 1) * WIN, :], axis=0, keepdims=True)
                dkv_ref[:, h * HD:(h + 1) * HD] = (carry[:, h * HD:(h + 1) * HD] + dkk[h, :WIN, :]).astype(BF16)
                dkv_ref[:, KW + h * HD:KW + (h + 1) * HD] = (carry[:, KW + h * HD:KW + (h + 1) * HD] + dvv[h, :WIN, :]).astype(BF16)
                cur[:, h * HD:(h + 1) * HD] = dkk[h, WIN:, :]
                cur[:, KW + h * HD:KW + (h + 1) * HD] = dvv[h, WIN:, :]
            carry[...] = cur[...]

        @pl.when(n == nb)
        def _():
            dkv_ref[...] = carry[...].astype(BF16)

    last = nb - 1
    row = pl.BlockSpec((WIN, D), lambda n: (jnp.minimum(n, last), 0))
    return pl.pallas_call(
        body, name=name,
        out_shape=(jax.ShapeDtypeStruct((Tn, D), BF16), jax.ShapeDtypeStruct((Tn, 2 * KW), BF16), jax.ShapeDtypeStruct((1, NQ), F32)),
        grid=(nb + 1,),
        in_specs=_attn_specs(nb, last) + [row, row, pl.BlockSpec(memory_space=pltpu.SMEM)],
        out_specs=(row, pl.BlockSpec((WIN, 2 * KW), lambda n: (jnp.maximum(n - 1, 0), 0)), pl.BlockSpec((1, NQ), lambda n: (0, 0))),
        scratch_shapes=[pltpu.VMEM((WIN, 2 * KW), F32), pltpu.VMEM((WIN, 2 * KW), F32)],
        compiler_params=_cp("arbitrary"),
    )(proj, proj, proj, proj, proj, o, do, sinks)


def _tri(lower):
    r = lax.broadcasted_iota(jnp.int32, (GC, GC), 0)
    c = lax.broadcasted_iota(jnp.int32, (GC, GC), 1)
    return r >= c if lower else r <= c


def _per_head(a):
    return jnp.stack([a[:, h * DK:(h + 1) * DK] for h in range(GH)])


def _all_heads(a):
    return jnp.concatenate([a[h] for h in range(GH)], axis=1)


def _gla_gates(lr, w2_ref, gb_ref):
    logit = _dot(lr, w2_ref[...].astype(BF16)) + gb_ref[...]
    la = (jnp.minimum(logit, 0.0) - jnp.log(1.0 + jnp.exp(-jnp.abs(logit)))) * (1.0 / 16.0)
    g = _dot(_tri(True).astype(F32), la, prec=HIGHEST)
    return logit, g


def _bmm(spec, a, b):
    return jnp.einsum(spec, a, b, preferred_element_type=F32)


def _gla_specs(nc, rev):
    idx = (lambda n: nc - 1 - n) if rev else (lambda n: n)
    half = 2 * DK
    return (
        [pl.BlockSpec((GC, half), lambda n, j=j: (idx(n), C_GQ // half + j)) for j in range(2)]
        + [pl.BlockSpec((GC, half), lambda n, j=j: (idx(n), C_GK // half + j)) for j in range(2)]
        + [pl.BlockSpec((GC, DV), lambda n, h=h: (idx(n), C_GV // DV + h)) for h in range(GH)]
        + [pl.BlockSpec((GC, LANE), lambda n: (idx(n), 0)), pl.BlockSpec((LANE, GH * DK), lambda n: (0, 0)),
           pl.BlockSpec((1, GH * DK), lambda n: (0, 0))])


def _gla_heads(refs):
    return (lambda h: refs[h // 2][:, (h % 2) * DK:(h % 2 + 1) * DK], lambda h: refs[2 + h // 2][:, (h % 2) * DK:(h % 2 + 1) * DK],
            lambda h: refs[4 + h][...])


def _gla_fwd(proj, plr, w2p, gb, name="gla_fwd"):
    Tn = proj.shape[0]
    nc = Tn // GC

    def body(*refs):
        qh, kh, vh = _gla_heads(refs)
        lr_ref, w2_ref, gb_ref, o_ref, st_ref, S = refs[8:]

        @pl.when(pl.program_id(0) == 0)
        def _():
            S[...] = jnp.zeros_like(S)

        heads = lambda f: jnp.stack([f(h) for h in range(GH)])
        _, g_all = _gla_gates(lr_ref[...].astype(BF16), w2_ref, gb_ref)
        g = _per_head(g_all)
        gl = g[:, GC - 1:GC, :]
        k = heads(kh)
        v = heads(vh).astype(BF16)
        qd = (heads(qh) * (DK ** -0.5) * jnp.exp(g)).astype(BF16)
        ki = (k * jnp.exp(-g)).astype(BF16)
        ke = (k * jnp.exp(gl - g)).astype(BF16)
        att = jnp.where(_tri(True)[None], _bmm('hid,hjd->hij', qd, ki), 0.0).astype(BF16)
        sp = S[...]
        st_ref[0] = sp
        o = _bmm('hij,hjv->hiv', att, v) + _bmm('hid,hvd->hiv', qd, sp.astype(BF16))
        for h in range(GH):
            o_ref[:, h * DV:(h + 1) * DV] = o[h]
        S[...] = sp * jnp.exp(gl) + _bmm('hjv,hjd->hvd', v, ke)

    return pl.pallas_call(
        body, name=name,
        out_shape=(jax.ShapeDtypeStruct((Tn, GH * DV), F32), jax.ShapeDtypeStruct((nc, GH, DV, DK), F32)),
        grid=(nc,), in_specs=_gla_specs(nc, False),
        out_specs=(pl.BlockSpec((GC, GH * DV), lambda n: (n, 0)), pl.BlockSpec((1, GH, DV, DK), lambda n: (n, 0, 0, 0))),
        scratch_shapes=[pltpu.VMEM((GH, DV, DK), F32)], compiler_params=_cp("arbitrary"),
    )(*([proj] * 8), plr, w2p, gb)


def _gla_bwd(proj, plr, w2p, gb, states, do, name="gla_bwd"):
    Tn = proj.shape[0]
    nc = Tn // GC

    def body(*refs):
        qh, kh, vh = _gla_heads(refs)
        lr_ref, w2_ref, gb_ref, st_ref, do_ref, dqk_ref, dv_ref, dlr_ref, dw2_ref, dgb_ref, dS = refs[8:]

        @pl.when(pl.program_id(0) == 0)
        def _():
            dS[...] = jnp.zeros_like(dS)
            dw2_ref[...] = jnp.zeros_like(dw2_ref)
            dgb_ref[...] = jnp.zeros_like(dgb_ref)

        heads = lambda f: jnp.stack([f(h) for h in range(GH)])
        lr = lr_ref[...].astype(BF16)
        causal = _tri(True)[None]
        last_row = lax.broadcasted_iota(jnp.int32, (GH, GC, DK), 1) == GC - 1
        logit, g_all = _gla_gates(lr, w2_ref, gb_ref)
        g = _per_head(g_all)
        gl = g[:, GC - 1:GC, :]
        egl = jnp.exp(gl)
        eg, eng, ege = jnp.exp(g), jnp.exp(-g), jnp.exp(gl - g)
        k = heads(kh)
        v = heads(vh).astype(BF16)
        dob = heads(lambda h: do_ref[:, h * DV:(h + 1) * DV]).astype(BF16)
        qd = heads(qh) * (DK ** -0.5) * eg
        ki = k * eng
        ke = k * ege
        qdb, kib, keb = qd.astype(BF16), ki.astype(BF16), ke.astype(BF16)
        att = jnp.where(causal, _bmm('hid,hjd->hij', qdb, kib), 0.0).astype(BF16)
        datt = jnp.where(causal, _bmm('hiv,hjv->hij', dob, v), 0.0).astype(BF16)
        sp = st_ref[0]
        dsn = dS[...]
        dsnb = dsn.astype(BF16)
        dv = (_bmm('hij,hiv->hjv', att, dob) + _bmm('hjd,hvd->hjv', keb, dsnb)).astype(BF16)
        dqd = _bmm('hij,hjd->hid', datt, kib) + _bmm('hiv,hvd->hid', dob, sp.astype(BF16))
        dki = _bmm('hij,hid->hjd', datt, qdb)
        dke = _bmm('hjv,hvd->hjd', v, dsnb)
        ddec = jnp.sum(dsn * sp, axis=1, keepdims=True)
        dS[...] = dsn * egl + _bmm('hiv,hid->hvd', dob, qdb)
        dke_ke = dke * ke
        dgl = jnp.sum(dke_ke, axis=1, keepdims=True) + ddec * egl
        dg = dqd * qd - dki * ki - dke_ke + jnp.where(last_row, dgl, 0.0)
        dq = (dqd * ((DK ** -0.5) * eg)).astype(BF16)
        dk = (dki * eng + dke * ege).astype(BF16)
        for h in range(GH):
            dv_ref[:, h * DV:(h + 1) * DV] = dv[h]
            dqk_ref[:, h * DK:(h + 1) * DK] = dq[h]
            dqk_ref[:, GH * DK + h * DK:GH * DK + (h + 1) * DK] = dk[h]
        dla = _dot(_tri(False).astype(F32), _all_heads(dg), prec=HIGHEST)
        dlogit = dla * (1.0 / 16.0) * _sigmoid(-logit)
        dlb = dlogit.astype(BF16)
        dlr_ref[...] = _dot(dlb, w2_ref[...].astype(BF16), tb=True).astype(BF16)
        dw2_ref[...] += _dot(lr, dlb, ta=True)
        dgb_ref[...] += jnp.sum(dlogit, axis=0, keepdims=True)

    rev = lambda n: nc - 1 - n
    row = pl.BlockSpec((GC, GH * DV), lambda n: (rev(n), 0))
    return pl.pallas_call(
        body, name=name,
        out_shape=(jax.ShapeDtypeStruct((Tn, 2 * GH * DK), BF16), jax.ShapeDtypeStruct((Tn, GH * DV), BF16),
                   jax.ShapeDtypeStruct((Tn, LANE), BF16), jax.ShapeDtypeStruct((LANE, GH * DK), F32),
                   jax.ShapeDtypeStruct((1, GH * DK), F32)),
        grid=(nc,),
        in_specs=_gla_specs(nc, True) + [pl.BlockSpec((1, GH, DV, DK), lambda n: (rev(n), 0, 0, 0)), row],
        out_specs=(row, row, pl.BlockSpec((GC, LANE), lambda n: (rev(n), 0)), pl.BlockSpec((LANE, GH * DK), lambda n: (0, 0)),
                   pl.BlockSpec((1, GH * DK), lambda n: (0, 0))),
        scratch_shapes=[pltpu.VMEM((GH, DV, DK), F32)], compiler_params=_cp("arbitrary"),
    )(*([proj] * 8), plr, w2p, gb, states, do)


def _merge_specs(tm):
    row = pl.BlockSpec((tm, D), lambda i: (i, 0))
    gates = [pl.BlockSpec((tm, DV), lambda i, j=c // DV + h: (i, j)) for c in (C_GR, C_GA, C_GB) for h in range(GH)]
    return row, gates, pl.BlockSpec((1, DV), lambda i: (0, 0))


def _merge_fwd(a, go, proj, gnw, name="merge_fwd", tm=256):
    Tn = a.shape[0]

    def body(a_ref, go_ref, *rest):
        gates, w_ref, m_ref = rest[:3 * GH], rest[3 * GH], rest[3 * GH + 1]
        for h in range(GH):
            sl = slice(h * DV, (h + 1) * DV)
            gov = go_ref[:, sl]
            r = lax.rsqrt(jnp.mean(gov * gov, axis=1, keepdims=True) + EPS)
            gr = gates[h][...]
            g2 = gov * r * w_ref[...] * (gr * _sigmoid(gr))
            m_ref[:, sl] = (_sigmoid(gates[GH + h][...]) * a_ref[:, sl] + _sigmoid(gates[2 * GH + h][...]) * g2).astype(BF16)

    row, gates, vec = _merge_specs(tm)
    return pl.pallas_call(
        body, name=name, out_shape=jax.ShapeDtypeStruct((Tn, D), BF16), grid=(Tn // tm,),
        in_specs=[row, row] + gates + [vec], out_specs=row, compiler_params=_cp("parallel"),
    )(a, go, *([proj] * (3 * GH)), gnw)


def _merge_bwd(dm, a, go, proj, gnw, name="merge_bwd", tm=256):
    Tn = a.shape[0]

    def body(dm_ref, a_ref, go_ref, *rest):
        gates = rest[:3 * GH]
        w_ref, da_ref, dgo_ref, dg_ref, dw_ref = rest[3 * GH:]
        wv = w_ref[...]
        dw = jnp.zeros((1, DV), F32)
        for h in range(GH):
            sl = slice(h * DV, (h + 1) * DV)
            dmv, av, gov, gr = dm_ref[:, sl], a_ref[:, sl], go_ref[:, sl], gates[h][...]
            sa, sb, sg = _sigmoid(gates[GH + h][...]), _sigmoid(gates[2 * GH + h][...]), _sigmoid(gr)
            r = lax.rsqrt(jnp.mean(gov * gov, axis=1, keepdims=True) + EPS)
            gn0 = gov * r
            gn = gn0 * wv
            silu = gr * sg
            dg2 = dmv * sb
            da_ref[:, sl] = dmv * sa
            dg_ref[:, D + h * DV:D + (h + 1) * DV] = (dmv * av * sa * (1.0 - sa)).astype(BF16)
            dg_ref[:, 2 * D + h * DV:2 * D + (h + 1) * DV] = (dg2 * gn * silu * (1.0 - sb)).astype(BF16)
            dg_ref[:, sl] = (dg2 * gn * (sg * (1.0 + gr * (1.0 - sg)))).astype(BF16)
            dgn = dg2 * silu
            dw = dw + jnp.sum(dgn * gn0, axis=0, keepdims=True)
            gg = dgn * wv
            dgo_ref[:, sl] = r * gg - gov * (r * r * r * jnp.mean(gg * gov, axis=1, keepdims=True))

        @pl.when(pl.program_id(0) == 0)
        def _():
            dw_ref[...] = dw

        @pl.when(pl.program_id(0) > 0)
        def _():
            dw_ref[...] += dw

    row, gates, vec = _merge_specs(tm)
    return pl.pallas_call(
        body, name=name,
        out_shape=(jax.ShapeDtypeStruct((Tn, D), F32), jax.ShapeDtypeStruct((Tn, D), F32), jax.ShapeDtypeStruct((Tn, 3 * D), BF16),
                   jax.ShapeDtypeStruct((1, DV), F32)),
        grid=(Tn // tm,), in_specs=[row, row, row] + gates + [vec],
        out_specs=(row, row, pl.BlockSpec((tm, 3 * D), lambda i: (i, 0)), vec), compiler_params=_cp("arbitrary"),
    )(dm, a, go, *([proj] * (3 * GH)), gnw)


def _ffn_up(v2, wgt, wut, name="ffn_up", tm=1024, tn=512):
    Tn = v2.shape[0]
    tm = min(tm, Tn)

    def body(v_ref, wg_ref, wu_ref, a_ref, b_ref, ff_ref):
        vv = v_ref[...]
        a = _dot(vv, wg_ref[...], tb=True)
        b = _dot(vv, wu_ref[...], tb=True)
        a_ref[...] = a.astype(BF16)
        b_ref[...] = b.astype(BF16)
        ff_ref[...] = (a * _sigmoid(a) * b).astype(BF16)

    w = pl.BlockSpec((tn, D), lambda j, i: (j, 0))
    act = pl.BlockSpec((tm, tn), lambda j, i: (i, j))
    return pl.pallas_call(
        body, name=name,
        out_shape=(jax.ShapeDtypeStruct((Tn, FH), BF16), jax.ShapeDtypeStruct((Tn, FH), BF16), jax.ShapeDtypeStruct((Tn, FH), BF16)),
        grid=(FH // tn, Tn // tm), in_specs=[pl.BlockSpec((tm, D), lambda j, i: (i, 0)), w, w], out_specs=(act, act, act),
        compiler_params=_cp("parallel", "parallel"),
    )(v2, wgt, wut)


def _ffn_dact(dh2b, wd, a, b, name="ffn_dact", tm=1024, tn=512):
    Tn = dh2b.shape[0]
    tm = min(tm, Tn)

    def body(d_ref, w_ref, a_ref, b_ref, da_ref, db_ref):
        dff = _dot(d_ref[...], w_ref[...], tb=True)
        av = a_ref[...].astype(F32)
        sg = _sigmoid(av)
        da_ref[...] = (dff * b_ref[...].astype(F32) * (sg * (1.0 + av * (1.0 - sg)))).astype(BF16)
        db_ref[...] = (dff * (av * sg)).astype(BF16)

    act = pl.BlockSpec((tm, tn), lambda j, i: (i, j))
    return pl.pallas_call(
        body, name=name,
        out_shape=(jax.ShapeDtypeStruct((Tn, FH), BF16), jax.ShapeDtypeStruct((Tn, FH), BF16)),
        grid=(FH // tn, Tn // tm),
        in_specs=[pl.BlockSpec((tm, D), lambda j, i: (i, 0)), pl.BlockSpec((tn, D), lambda j, i: (j, 0)), act, act],
        out_specs=(act, act), compiler_params=_cp("parallel", "parallel"),
    )(dh2b, wd, a, b)


def _adam_math(w, g, m, v):
    m2 = B1 * m + (1.0 - B1) * g
    v2 = B2 * v + (1.0 - B2) * (g * g)
    mh = m2 / (1.0 - B1 ** STEP)
    vh = v2 / (1.0 - B2 ** STEP)
    return -LR * (mh / (jnp.sqrt(vh) + AEPS) + WD * w), m2, v2


def _sum_blocks(o_ref, p_ref):
    g = o_ref[...].astype(F32)
    for j in range(p_ref.shape[0]):
        g = g + p_ref[j].astype(F32)
    return g


def _adamw(w, m, v, psums, parts, chip_idx, name, tr):
    R, C = w.shape

    def body(s_ref, w_ref, m_ref, v_ref, o_ref, p_ref, g_ref, d_ref, m2_ref, v2_ref):
        g = _sum_blocks(o_ref, p_ref)
        d, m2, v2 = _adam_math(w_ref[...], g, m_ref[...], v_ref[...])
        g_ref[...] = g
        d_ref[...] = d
        m2_ref[...] = m2
        v2_ref[...] = v2

    blk = pl.BlockSpec((tr, C), lambda i, s: (i, 0))
    out = jax.ShapeDtypeStruct((R, C), F32)
    grid_spec = pltpu.PrefetchScalarGridSpec(
        num_scalar_prefetch=1, grid=(R // tr,),
        in_specs=[blk, blk, blk, pl.BlockSpec((None, tr, C), lambda i, s: (s[0], i, 0)),
                  pl.BlockSpec((parts.shape[0], tr, C), lambda i, s: (0, i, 0))],
        out_specs=(blk, blk, blk, blk),
    )
    return pl.pallas_call(body, name=name, out_shape=(out, out, out, out), grid_spec=grid_spec, compiler_params=_cp("parallel"),
                          )(chip_idx, w, m, v, psums, parts)


def _adamw_rows(w, m, v, g, name, tr):
    R = w.shape[0]

    def body(w_ref, m_ref, v_ref, g_ref, d_ref, m2_ref, v2_ref):
        d, m2, v2 = _adam_math(w_ref[...], g_ref[...], m_ref[...], v_ref[...])
        d_ref[...] = d
        m2_ref[...] = m2
        v2_ref[...] = v2

    blk = pl.BlockSpec((tr,) + w.shape[1:], lambda i: (i, 0, 0))
    out = jax.ShapeDtypeStruct(w.shape, F32)
    return pl.pallas_call(body, name=name, out_shape=(out, out, out), grid=(R // tr,), in_specs=[blk] * 4, out_specs=(blk, blk, blk),
                          compiler_params=_cp("parallel"))(w, m, v, g)


def _sum_parts(psums, parts, chip_idx, name, tr, tc):
    _, R, C = psums.shape

    def body(s_ref, o_ref, p_ref, g_ref):
        g_ref[...] = _sum_blocks(o_ref, p_ref)

    grid_spec = pltpu.PrefetchScalarGridSpec(
        num_scalar_prefetch=1, grid=(R // tr, C // tc),
        in_specs=[pl.BlockSpec((None, tr, tc), lambda i, j, s: (s[0], i, j)),
                  pl.BlockSpec((parts.shape[0], tr, tc), lambda i, j, s: (0, i, j))],
        out_specs=pl.BlockSpec((tr, tc), lambda i, j, s: (i, j)),
    )
    return pl.pallas_call(body, name=name, out_shape=jax.ShapeDtypeStruct((R, C), F32), grid_spec=grid_spec,
                          compiler_params=_cp("parallel", "parallel"))(chip_idx, psums, parts)


def _adamw_plain(w, m, v, g, name):
    def body(w_ref, m_ref, v_ref, g_ref, d_ref, m2_ref, v2_ref):
        d, m2, v2 = _adam_math(w_ref[...], g_ref[...], m_ref[...], v_ref[...])
        d_ref[...] = d
        m2_ref[...] = m2
        v2_ref[...] = v2

    out = jax.ShapeDtypeStruct(w.shape, F32)
    return pl.pallas_call(body, name=name, out_shape=(out, out, out))(w, m, v, g)


def _sum_devices(pack_all, name="sum_small"):
    def body(p_ref, o_ref):
        s = p_ref[0]
        for k in range(1, NDEV):
            s = s + p_ref[k]
        o_ref[...] = s

    return pl.pallas_call(body, name=name, out_shape=jax.ShapeDtypeStruct(pack_all.shape[1:], F32))(pack_all)


def _pair_add(g5, recv, c_idx, name, tr):
    _, _, R, C = g5.shape

    def body(c_ref, g_ref, r_ref, o_ref):
        o_ref[...] = (g_ref[...].astype(F32) + r_ref[...].astype(F32)).astype(BF16)

    grid_spec = pltpu.PrefetchScalarGridSpec(
        num_scalar_prefetch=1, grid=(4, R // tr),
        in_specs=[pl.BlockSpec((None, None, tr, C), lambda q, i, c: (q, c[0], i, 0)), pl.BlockSpec((None, tr, C), lambda q, i, c: (q, i, 0))],
        out_specs=pl.BlockSpec((None, tr, C), lambda q, i, c: (q, i, 0)),
    )
    return pl.pallas_call(
        body, name=name, out_shape=jax.ShapeDtypeStruct((4, R, C), BF16), grid_spec=grid_spec,
        compiler_params=_cp("parallel", "parallel"),
    )(c_idx, g5, recv)


_ANY = pl.BlockSpec(memory_space=pl.ANY)


def _mesh_pos():
    x, y, c = lax.axis_index("x"), lax.axis_index("y"), lax.axis_index("c")
    return x, y, c, [(1 - x, y), (x, 1 - y), (1 - x, 1 - y)]


def _gather_small(pack, name="gather_small"):
    def body(pk, pk_all, psend, precv, loc):
        x, y, c, chips = _mesh_pos()
        me_slot = 4 * x + 2 * y + c
        sib = (x, y, 1 - c)
        own = pltpu.make_async_copy(pk, pk_all.at[me_slot], loc)
        own.start()
        peers = [sib] + [(*chip, c) for chip in chips] + [(*chip, 1 - c) for chip in chips]
        small = [pltpu.make_async_remote_copy(src_ref=pk, dst_ref=pk_all.at[me_slot], send_sem=psend.at[k], recv_sem=precv.at[k],
                                              device_id=p, device_id_type=MESH) for k, p in enumerate(peers)]
        for d in small:
            d.start()
        for k, p in enumerate(peers):
            pltpu.make_async_remote_copy(src_ref=pk, dst_ref=pk_all.at[4 * p[0] + 2 * p[1] + p[2]], send_sem=psend.at[k],
                                         recv_sem=precv.at[k], device_id=p, device_id_type=MESH).wait_recv()
        for d in small:
            d.wait_send()
        own.wait()

    return pl.pallas_call(
        body, name=name, out_shape=jax.ShapeDtypeStruct((NDEV,) + pack.shape, pack.dtype), in_specs=[_ANY], out_specs=_ANY,
        scratch_shapes=[pltpu.SemaphoreType.DMA((7,)), pltpu.SemaphoreType.DMA((7,)), pltpu.SemaphoreType.DMA(())],
    )(pack)


def _main_row(g):
    return g if g < C_LR else g - RANK


def _window_pieces(lo, hi):
    out = []
    for a, b, where in ((lo, min(hi, C_LR), "main"), (max(lo, C_LR), min(hi, C_LR + RANK), "lr"), (max(lo, C_LR + RANK), hi, "main")):
        if a < b:
            out.append((a, b, where, _main_row(a) if where == "main" else a - C_LR))
    return out


def _assemble_w_in(windows, own, name="assemble_w_in"):
    edges = NDEV - 1

    def body(b_ref, own_ref, main_ref, lr_ref, buf, ebuf, in_sems, out_sems, esems):
        dev = 4 * lax.axis_index("x") + 2 * lax.axis_index("y") + lax.axis_index("c")

        def load(k):
            return pltpu.make_async_copy(b_ref.at[k], buf.at[k % 2], in_sems.at[k % 2])

        def start_load(k):
            pl.when(dev == k)(pltpu.make_async_copy(own_ref, buf.at[k % 2], in_sems.at[k % 2]).start)
            pl.when(dev != k)(load(k).start)

        lr_ref[RANK:, :] = jnp.zeros((LANE - RANK, D), BF16)
        start_load(0)
        pending, edge_out = [], []
        for k in range(NDEV):
            s = k % 2
            load(k).wait()
            if k:
                ebuf[k - 1] = buf[1 - s, WSTEP:WWIN, :] + buf[s, 0:16, :]
                edge_out.append(pltpu.make_async_copy(ebuf.at[k - 1], main_ref.at[pl.ds(_main_row(WSTEP * k), 16)], esems.at[k - 1]))
                edge_out[-1].start()
                for d in pending:
                    d.wait()
            if k + 1 < NDEV:
                start_load(k + 1)
            pending = []
            lo = WSTEP * k + (16 if k else 0)
            hi = WSTEP * k + (WWIN if k == NDEV - 1 else WSTEP)
            for a, b, where, dst in _window_pieces(lo, hi):
                if where == "lr":
                    lr_ref[dst:dst + b - a, :] = buf[s, a - WSTEP * k:b - WSTEP * k, :]
                else:
                    pending.append(pltpu.make_async_copy(buf.at[s, pl.ds(a - WSTEP * k, b - a)], main_ref.at[pl.ds(dst, b - a)],
                                                         out_sems.at[2 * s + len(pending)]))
                    pending[-1].start()
        for d in pending + edge_out:
            d.wait()

    return pl.pallas_call(
        body, name=name,
        out_shape=(jax.ShapeDtypeStruct((NMAIN, D), BF16), jax.ShapeDtypeStruct((LANE, D), BF16)),
        in_specs=[_ANY, _ANY], out_specs=(_ANY, pl.BlockSpec(memory_space=pltpu.VMEM)),
        scratch_shapes=[pltpu.VMEM((2, WWIN, D), BF16), pltpu.VMEM((edges, 16, D), BF16), pltpu.SemaphoreType.DMA((2,)),
                        pltpu.SemaphoreType.DMA((4,)), pltpu.SemaphoreType.DMA((edges,))],
        compiler_params=pltpu.CompilerParams(vmem_limit_bytes=VMEM_LIMIT),
    )(windows, own)


def _disassemble_exchange(d_main, d_lr, name="disassemble_exchange"):
    def body(main_ref, lr_ref, mine_ref, recv_ref, buf, in_sems, keep_sems, send_sems, recv_sems):
        x, y, c, _ = _mesh_pos()
        sib = (x, y, 1 - c)

        def loads(k):
            s, out = k % 2, []
            for a, b, where, src0 in _window_pieces(WSTEP * k, WSTEP * k + WWIN):
                if where == "main":
                    out.append(pltpu.make_async_copy(main_ref.at[pl.ds(src0, b - a)], buf.at[s, pl.ds(a - WSTEP * k, b - a)],
                                                     in_sems.at[2 * s + len(out)]))
            return out

        def keep(k):
            return pltpu.make_async_copy(buf.at[k % 2], mine_ref.at[k // 2], keep_sems.at[k % 2])

        def send(k):
            return _rcopy(buf.at[k % 2], recv_ref.at[k // 2], send_sems.at[k % 2], recv_sems.at[k // 2], sib)

        def store_start(k):
            pl.when(c == k % 2)(keep(k).start)
            pl.when(c != k % 2)(send(k).start)

        def store_wait(k):
            pl.when(c == k % 2)(keep(k).wait)
            pl.when(c != k % 2)(send(k).wait_send)

        for d in loads(0):
            d.start()
        for k in range(NDEV):
            for d in loads(k):
                d.wait()
            for a, b, where, src0 in _window_pieces(WSTEP * k, WSTEP * k + WWIN):
                if where == "lr":
                    buf[k % 2, a - WSTEP * k:b - WSTEP * k, :] = lr_ref[src0:src0 + b - a, :]
            if k:
                store_wait(k - 1)
            if k + 1 < NDEV:
                for d in loads(k + 1):
                    d.start()
            store_start(k)
        store_wait(NDEV - 1)
        for chip in range(NDEV // 2):
            _rcopy(buf.at[0], recv_ref.at[chip], send_sems.at[0], recv_sems.at[chip], sib).wait_recv()

    half = jax.ShapeDtypeStruct((NDEV // 2, WWIN, D), BF16)
    return pl.pallas_call(
        body, name=name, out_shape=(half, half),
        in_specs=[_ANY, pl.BlockSpec(memory_space=pltpu.VMEM)], out_specs=(_ANY, _ANY),
        scratch_shapes=[pltpu.VMEM((2, WWIN, D), BF16), pltpu.SemaphoreType.DMA((4,)), pltpu.SemaphoreType.DMA((2,)),
                        pltpu.SemaphoreType.DMA((2,)), pltpu.SemaphoreType.DMA((NDEV // 2,))],
        compiler_params=pltpu.CompilerParams(vmem_limit_bytes=VMEM_LIMIT),
    )(d_main, d_lr)


def _add_blocks(a, b, name, tr):
    _, R, C = a.shape

    def body(a_ref, b_ref, o_ref):
        o_ref[...] = (a_ref[...].astype(F32) + b_ref[...].astype(F32)).astype(BF16)

    blk = pl.BlockSpec((None, tr, C), lambda q, i: (q, i, 0))
    return pl.pallas_call(body, name=name, out_shape=jax.ShapeDtypeStruct(a.shape, BF16), grid=(a.shape[0], R // tr),
                          in_specs=[blk, blk], out_specs=blk, compiler_params=_cp("parallel", "parallel"))(a, b)


_HBM = pl.BlockSpec(memory_space=pltpu.HBM)
_SEM = pl.BlockSpec(memory_space=pltpu.SEMAPHORE)
_VMEM = pl.BlockSpec(memory_space=pltpu.VMEM)
_SIDE = pltpu.CompilerParams(has_side_effects=pltpu.SideEffectType.DATAFLOW_SIDE_EFFECTING)
_TOKEN = jax.ShapeDtypeStruct((8, LANE), F32)


def _hbm(a):
    return pltpu.with_memory_space_constraint(a, pltpu.HBM)


def _hbm_like(arrs):
    return tuple(pltpu.HBM(a.shape, a.dtype) for a in arrs)


def _tie(x, token):
    return x + token[0, 0].astype(x.dtype)


def _chip_copies(ins, lands, send, recv, nrel):
    x, y, c, chips = _mesh_pos()
    first = [sum(nrel[:a]) for a in range(len(ins))]
    return [pltpu.make_async_remote_copy(src_ref=ins[a].at[2 * chip[0] + chip[1]], dst_ref=lands[a].at[j], send_sem=send.at[first[a] + j],
                                         recv_sem=recv.at[first[a] + j], device_id=(*chip, c), device_id_type=MESH)
            for a in range(len(ins)) for j, chip in enumerate(chips[:nrel[a]])]


def _chip_start(psums, name, nrel=None):
    n = len(psums)
    nrel = nrel or [3] * n
    lands = [lax.empty((r,) + p.shape[1:], p.dtype) for r, p in zip(nrel, psums)]

    def body(*refs):
        for d in _chip_copies(refs[:n], refs[n:2 * n], refs[2 * n], refs[2 * n + 1], nrel):
            d.start()
        refs[-1][...] = jnp.zeros_like(refs[-1])

    sems = pltpu.SemaphoreType.DMA((sum(nrel),))
    out = pl.pallas_call(
        body, name=name, out_shape=(sems, sems) + _hbm_like(psums) + _hbm_like(lands) + (_TOKEN,),
        in_specs=[_HBM] * (2 * n), out_specs=(_SEM, _SEM) + (_HBM,) * (2 * n) + (_VMEM,),
        input_output_aliases={i: 2 + i for i in range(2 * n)}, compiler_params=_SIDE,
    )(*[_hbm(a) for a in list(psums) + lands])
    return out[0], out[1], list(out[2:2 + n]), list(out[2 + n:2 + 2 * n]), out[-1]


def _chip_wait(send, recv, psums, lands, after, name):
    n = len(psums)
    nrel = [l.shape[0] for l in lands]

    def body(*refs):
        for d in _chip_copies(refs[:n], refs[n:2 * n], refs[2 * n], refs[2 * n + 1], nrel):
            d.wait_send()
            d.wait_recv()

    out = pl.pallas_call(
        body, name=name, out_shape=_hbm_like(psums) + _hbm_like(lands),
        in_specs=[_HBM] * (2 * n) + [_SEM, _SEM, _ANY], out_specs=(_HBM,) * (2 * n),
        input_output_aliases={i: i for i in range(2 * n)}, compiler_params=_SIDE,
    )(*psums, *lands, send, recv, after)
    return list(out[:n]), list(out[n:])


def _hop_pos():
    x, y, c, _ = _mesh_pos()
    north = c == 1
    via = (jnp.where(north, 1 - x, x), jnp.where(north, y, 1 - y))
    return (*via, c), 2 * (1 - x) + (1 - y), jnp.where(north, 2 * x + (1 - y), 2 * (1 - x) + y)


def _hop_copies(ins, lands, send, recv):
    to, mine, _ = _hop_pos()
    return [pltpu.make_async_remote_copy(src_ref=ins[a].at[mine], dst_ref=lands[a], send_sem=send.at[a], recv_sem=recv.at[a],
                                         device_id=to, device_id_type=MESH) for a in range(len(ins))]


def _hop_start(psums, name):
    n = len(psums)
    lands = [lax.empty(p.shape[1:], p.dtype) for p in psums]

    def body(*refs):
        for d in _hop_copies(refs[:n], refs[n:2 * n], refs[2 * n], refs[2 * n + 1]):
            d.start()
        refs[-1][...] = jnp.zeros_like(refs[-1])

    sems = pltpu.SemaphoreType.DMA((n,))
    out = pl.pallas_call(
        body, name=name, out_shape=(sems, sems) + _hbm_like(psums) + _hbm_like(lands) + (_TOKEN,),
        in_specs=[_HBM] * (2 * n), out_specs=(_SEM, _SEM) + (_HBM,) * (2 * n) + (_VMEM,),
        input_output_aliases={i: 2 + i for i in range(2 * n)}, compiler_params=_SIDE,
    )(*[_hbm(a) for a in list(psums) + lands])
    return out[0], out[1], list(out[2:2 + n]), list(out[2 + n:2 + 2 * n]), out[-1]


def _hop_wait(send, recv, psums, lands, after, name):
    n = len(psums)

    def body(*refs):
        for d in _hop_copies(refs[:n], refs[n:2 * n], refs[2 * n], refs[2 * n + 1]):
            d.wait_send()
            d.wait_recv()

    out = pl.pallas_call(
        body, name=name, out_shape=_hbm_like(psums) + _hbm_like(lands),
        in_specs=[_HBM] * (2 * n) + [_SEM, _SEM, _ANY], out_specs=(_HBM,) * (2 * n),
        input_output_aliases={i: i for i in range(2 * n)}, compiler_params=_SIDE,
    )(*psums, *lands, send, recv, after)
    return list(out[:n]), list(out[n:])


def _hop_add(psums, land, idx, name, tr):
    _, R, C = psums.shape

    def body(s_ref, p_ref, l_ref, o_ref):
        o_ref[...] = (p_ref[...].astype(F32) + l_ref[...].astype(F32)).astype(BF16)

    blk = pl.BlockSpec((None, tr, C), lambda i, s: (s[0], i, 0))
    grid_spec = pltpu.PrefetchScalarGridSpec(num_scalar_prefetch=1, grid=(R // tr,),
                                             in_specs=[blk, pl.BlockSpec((tr, C), lambda i, s: (i, 0))], out_specs=blk)
    return pl.pallas_call(body, name=name, out_shape=jax.ShapeDtypeStruct(psums.shape, BF16), grid_spec=grid_spec,
                          input_output_aliases={1: 0}, compiler_params=_cp("parallel"))(idx, psums, land)


def _pair_copies(ins, lands, send, recv):
    x, y, c, _ = _mesh_pos()
    return [pltpu.make_async_remote_copy(src_ref=ins[a].at[:, 1 - c], dst_ref=lands[a], send_sem=send.at[a], recv_sem=recv.at[a],
                                         device_id=(x, y, 1 - c), device_id_type=MESH) for a in range(len(ins))]


def _pair_start(grads, name):
    n = len(grads)
    lands = [lax.empty((4,) + g.shape[2:], g.dtype) for g in grads]

    def body(*refs):
        for d in _pair_copies(refs[:n], refs[n:2 * n], refs[2 * n], refs[2 * n + 1]):
            d.start()
        refs[-1][...] = jnp.zeros_like(refs[-1])

    sems = pltpu.SemaphoreType.DMA((n,))
    out = pl.pallas_call(
        body, name=name, out_shape=(sems, sems) + _hbm_like(grads) + _hbm_like(lands) + (_TOKEN,),
        in_specs=[_HBM] * (2 * n), out_specs=(_SEM, _SEM) + (_HBM,) * (2 * n) + (_VMEM,),
        input_output_aliases={i: 2 + i for i in range(2 * n)}, compiler_params=_SIDE,
    )(*[_hbm(a) for a in list(grads) + lands])
    return out[0], out[1], list(out[2:2 + n]), list(out[2 + n:2 + 2 * n]), out[-1]


def _pair_wait(send, recv, grads, lands, after, name):
    n = len(grads)

    def body(*refs):
        for d in _pair_copies(refs[:n], refs[n:2 * n], refs[2 * n], refs[2 * n + 1]):
            d.wait_send()
            d.wait_recv()

    out = pl.pallas_call(
        body, name=name, out_shape=_hbm_like(grads) + _hbm_like(lands),
        in_specs=[_HBM] * (2 * n) + [_SEM, _SEM, _ANY], out_specs=(_HBM,) * (2 * n),
        input_output_aliases={i: i for i in range(2 * n)}, compiler_params=_SIDE,
    )(*grads, *lands, send, recv, after)
    return list(out[:n]), list(out[n:])


def _slot(chip, c):
    return 4 * chip[0] + 2 * chip[1] + c


def _gather_start(shards, lands, after, name):
    n = len(shards)

    def body(*refs):
        src, land, send, recv = refs[:n], refs[n:2 * n], refs[2 * n + 1], refs[2 * n + 2]
        x, y, c, chips = _mesh_pos()
        for a in range(n):
            for k, to in enumerate([(x, y, 1 - c)] + [(*chip, c) for chip in chips]):
                pltpu.make_async_remote_copy(src_ref=src[a], dst_ref=land[a].at[_slot((x, y), c)], send_sem=send.at[4 * a + k],
                                             recv_sem=recv.at[4 * a + k], device_id=to, device_id_type=MESH).start()
        refs[-1][...] = jnp.zeros_like(refs[-1])

    sems = pltpu.SemaphoreType.DMA((4 * n,))
    out = pl.pallas_call(
        body, name=name, out_shape=(sems, sems) + _hbm_like(shards) + _hbm_like(lands) + (_TOKEN,),
        in_specs=[_HBM] * (2 * n) + [_ANY], out_specs=(_SEM, _SEM) + (_HBM,) * (2 * n) + (_VMEM,),
        input_output_aliases={i: 2 + i for i in range(2 * n)}, compiler_params=_SIDE,
    )(*[_hbm(a) for a in list(shards) + list(lands)], after)
    return out[0], out[1], list(out[2:2 + n]), list(out[2 + n:2 + 2 * n]), out[-1]


def _gather_pass(lands, recv, after, name, first=0):
    n = len(lands)

    def body(*refs):
        land, recv1 = refs[:n], refs[n]
        send2, recv2 = refs[n + 2], refs[n + 3]
        x, y, c, chips = _mesh_pos()
        for a in range(n):
            for j, chip in enumerate(chips):
                blk = land[a].at[_slot(chip, c)]
                pltpu.make_async_remote_copy(src_ref=blk, dst_ref=blk, send_sem=send2.at[3 * a + j], recv_sem=recv1.at[4 * (first + a) + 1 + j],
                                             device_id=(*chip, c), device_id_type=MESH).wait_recv()
                pltpu.make_async_remote_copy(src_ref=blk, dst_ref=blk, send_sem=send2.at[3 * a + j], recv_sem=recv2.at[3 * a + j],
                                             device_id=(x, y, 1 - c), device_id_type=MESH).start()
        refs[-1][...] = jnp.zeros_like(refs[-1])

    sems = pltpu.SemaphoreType.DMA((3 * n,))
    out = pl.pallas_call(
        body, name=name, out_shape=(sems, sems) + _hbm_like(lands) + (_TOKEN,),
        in_specs=[_HBM] * n + [_SEM, _ANY], out_specs=(_SEM, _SEM) + (_HBM,) * n + (_VMEM,),
        input_output_aliases={i: 2 + i for i in range(n)}, compiler_params=_SIDE,
    )(*lands, recv, after)
    return out[0], out[1], list(out[2:2 + n]), out[-1]


def _gather_wait(shards, lands, send, recv, send2, recv2, after, name, first=0):
    n = len(lands)

    def body(*refs):
        src, land = refs[:n], refs[n:2 * n]
        send1, recv1, snd2, rcv2 = refs[2 * n:2 * n + 4]
        x, y, c, chips = _mesh_pos()
        sib = (x, y, 1 - c)
        for a in range(n):
            for k in range(4):
                pltpu.make_async_remote_copy(src_ref=src[a], dst_ref=land[a].at[_slot((x, y), c)], send_sem=send1.at[4 * (first + a) + k],
                                             recv_sem=recv1.at[4 * (first + a) + k], device_id=sib, device_id_type=MESH).wait_send()
            blk = land[a].at[_slot((x, y), 1 - c)]
            pltpu.make_async_remote_copy(src_ref=blk, dst_ref=blk, send_sem=send1.at[4 * (first + a)], recv_sem=recv1.at[4 * (first + a)],
                                         device_id=sib, device_id_type=MESH).wait_recv()
            for j, chip in enumerate(chips):
                mine, theirs = land[a].at[_slot(chip, c)], land[a].at[_slot(chip, 1 - c)]
                pltpu.make_async_remote_copy(src_ref=mine, dst_ref=mine, send_sem=snd2.at[3 * a + j], recv_sem=rcv2.at[3 * a + j],
                                             device_id=sib, device_id_type=MESH).wait_send()
                pltpu.make_async_remote_copy(src_ref=theirs, dst_ref=theirs, send_sem=snd2.at[3 * a + j], recv_sem=rcv2.at[3 * a + j],
                                             device_id=sib, device_id_type=MESH).wait_recv()

    out = pl.pallas_call(
        body, name=name, out_shape=_hbm_like(shards) + _hbm_like(lands),
        in_specs=[_HBM] * (2 * n) + [_SEM] * 4 + [_ANY], out_specs=(_HBM,) * (2 * n),
        input_output_aliases={i: i for i in range(2 * n)}, compiler_params=_SIDE,
    )(*shards, *lands, send, recv, send2, recv2, after)
    return list(out[n:])


def _win_tree():
    x, y, c, chips = _mesh_pos()
    north = c == 1
    handed = (jnp.where(north, 1 - x, x), jnp.where(north, y, 1 - y))
    hand_to = (jnp.where(north, x, 1 - x), jnp.where(north, 1 - y, y))
    return x, y, c, chips, handed, hand_to


def _blk(land, chip, c):
    return land.at[_slot(chip, c)]


def _rcopy(src, dst, send, recv, to):
    return pltpu.make_async_remote_copy(src_ref=src, dst_ref=dst, send_sem=send, recv_sem=recv, device_id=to, device_id_type=MESH)


def _win_start(shards, lands, name):
    n = len(shards)

    def body(*refs):
        src, land, send, recv = refs[:n], refs[n:2 * n], refs[2 * n], refs[2 * n + 1]
        x, y, c, chips, _, _ = _win_tree()
        for a in range(n):
            for k, to in enumerate([(x, y, 1 - c), (*chips[0], c), (*chips[1], c)]):
                _rcopy(src[a], _blk(land[a], (x, y), c), send.at[3 * a + k], recv.at[3 * a + k], to).start()
        refs[-1][...] = jnp.zeros_like(refs[-1])

    sems = pltpu.SemaphoreType.DMA((3 * n,))
    out = pl.pallas_call(
        body, name=name, out_shape=(sems, sems) + _hbm_like(shards) + _hbm_like(lands) + (_TOKEN,),
        in_specs=[_HBM] * (2 * n), out_specs=(_SEM, _SEM) + (_HBM,) * (2 * n) + (_VMEM,),
        input_output_aliases={i: 2 + i for i in range(2 * n)}, compiler_params=_SIDE,
    )(*[_hbm(a) for a in list(shards) + list(lands)])
    return out[0], out[1], list(out[2:2 + n]), list(out[2 + n:2 + 2 * n]), out[-1]


def _win_hand_on(lands, recv1, after, name):
    n, m = len(lands), len(after)

    def body(*refs):
        land, rcv1 = refs[:n], refs[n]
        send2, recv2 = refs[n + 1 + m], refs[n + 2 + m]
        x, y, c, chips, handed, hand_to = _win_tree()
        for a in range(n):
            for j in range(2):
                blk = _blk(land[a], chips[j], c)
                _rcopy(blk, blk, send2.at[3 * a], rcv1.at[3 * a + 1 + j], (*chips[j], c)).wait_recv()
            blk = _blk(land[a], handed, c)
            _rcopy(blk, blk, send2.at[3 * a], recv2.at[3 * a], (*hand_to, c)).start()
            for j in range(2):
                blk = _blk(land[a], chips[j], c)
                _rcopy(blk, blk, send2.at[3 * a + 1 + j], recv2.at[3 * a + 1 + j], (x, y, 1 - c)).start()
        refs[-1][...] = jnp.zeros_like(refs[-1])

    sems = pltpu.SemaphoreType.DMA((3 * n,))
    out = pl.pallas_call(
        body, name=name, out_shape=(sems, sems) + _hbm_like(lands) + (_TOKEN,),
        in_specs=[_HBM] * n + [_SEM] + [_ANY] * m, out_specs=(_SEM, _SEM) + (_HBM,) * n + (_VMEM,),
        input_output_aliases={i: 2 + i for i in range(n)}, compiler_params=_SIDE,
    )(*lands, recv1, *after)
    return out[0], out[1], list(out[2:2 + n]), out[-1]


def _win_last(lands, recv2, after, name):
    n, m = len(lands), len(after)

    def body(*refs):
        land, rcv2 = refs[:n], refs[n]
        send3, recv3 = refs[n + 1 + m], refs[n + 2 + m]
        x, y, c, chips, _, hand_to = _win_tree()
        for a in range(n):
            blk = _blk(land[a], chips[2], c)
            _rcopy(blk, blk, send3.at[a], rcv2.at[3 * a], (*hand_to, c)).wait_recv()
            _rcopy(blk, blk, send3.at[a], recv3.at[a], (x, y, 1 - c)).start()
        refs[-1][...] = jnp.zeros_like(refs[-1])

    sems = pltpu.SemaphoreType.DMA((n,))
    out = pl.pallas_call(
        body, name=name, out_shape=(sems, sems) + _hbm_like(lands) + (_TOKEN,),
        in_specs=[_HBM] * n + [_SEM] + [_ANY] * m, out_specs=(_SEM, _SEM) + (_HBM,) * n + (_VMEM,),
        input_output_aliases={i: 2 + i for i in range(n)}, compiler_params=_SIDE,
    )(*lands, recv2, *after)
    return out[0], out[1], list(out[2:2 + n]), out[-1]


def _win_wait(shards, lands, sems1, sems2, sems3, after, name):
    n = len(lands)

    def body(*refs):
        src, land = refs[:n], refs[n:2 * n]
        send1, recv1, send2, recv2, send3, recv3 = refs[2 * n:2 * n + 6]
        x, y, c, chips, handed, hand_to = _win_tree()
        sib = (x, y, 1 - c)
        for a in range(n):
            own = _blk(land[a], (x, y), c)
            for k in range(3):
                _rcopy(src[a], own, send1.at[3 * a + k], recv1.at[3 * a + k], sib).wait_send()
            blk = _blk(land[a], (x, y), 1 - c)
            _rcopy(blk, blk, send1.at[3 * a], recv1.at[3 * a], sib).wait_recv()
            blk = _blk(land[a], handed, c)
            _rcopy(blk, blk, send2.at[3 * a], recv2.at[3 * a], sib).wait_send()
            for j in range(2):
                mine, theirs = _blk(land[a], chips[j], c), _blk(land[a], chips[j], 1 - c)
                _rcopy(mine, mine, send2.at[3 * a + 1 + j], recv2.at[3 * a + 1 + j], sib).wait_send()
                _rcopy(theirs, theirs, send2.at[3 * a + 1 + j], recv2.at[3 * a + 1 + j], sib).wait_recv()
            mine, theirs = _blk(land[a], chips[2], c), _blk(land[a], chips[2], 1 - c)
            _rcopy(mine, mine, send3.at[a], recv3.at[a], sib).wait_send()
            _rcopy(theirs, theirs, send3.at[a], recv3.at[a], sib).wait_recv()

    out = pl.pallas_call(
        body, name=name, out_shape=_hbm_like(shards) + _hbm_like(lands),
        in_specs=[_HBM] * (2 * n) + [_SEM] * 6 + [_ANY], out_specs=(_HBM,) * (2 * n),
        input_output_aliases={i: i for i in range(2 * n)}, compiler_params=_SIDE,
    )(*shards, *lands, *sems1, *sems2, *sems3, after)
    return list(out[n:])


def _pad_to(v, n):
    return jnp.pad(v, [(0, 0)] * (v.ndim - 1) + [(0, n - v.shape[-1])])


def _pack_small(n1, gb, sk, gn, n2, fn, extra=None):
    parts = [n1.reshape(-1), gb.reshape(-1), sk.reshape(-1), gn.reshape(-1), n2.reshape(-1), fn.reshape(-1)]
    flat = jnp.concatenate(parts + ([extra.reshape(-1)] if extra is not None else []))
    return _pad_to(flat, SMALL_N).reshape(SMALL_ROWS, LANE)


def _unpack_small(p):
    f = p.reshape(-1)
    return (f[S_N1:S_GB].reshape(1, D), f[S_GB:S_SK].reshape(1, GH * DK), f[S_SK:S_GN].reshape(1, NQ), f[S_GN:S_N2].reshape(1, DV),
            f[S_N2:S_FN].reshape(1, D), f[S_FN:S_LOSS].reshape(D))


class _NoComm:
    def __init__(self, wo, wg_all, wu_all, wd_all):
        self.rest = (wo, wg_all, wu_all, wd_all)

    def mixed(self, gla_o, gla_norm_w):
        return gla_norm_w

    def w_out(self, merged, norm2_w):
        return self.rest[0], norm2_w

    def w_up(self, v2):
        return self.rest[1], self.rest[2]

    def w_down(self, ff):
        return self.rest[3]

    def ffn_grads(self, d_wg, d_wu, d_wd):
        self.ffn = (d_wg, d_wu, d_wd)

    def ffn_reduce(self, dv2, norm2_w):
        return norm2_w

    def in_grads(self, d_wmain, d_wlr, w_lr):
        self.inw = (d_wmain, d_wlr)
        return w_lr

    def in_reduce(self, d_wo):
        self.inw += (d_wo,)
        return None


class _Comm:
    def __init__(self, rest_shards, rest_lands, after, c_idx):
        self.c_idx = c_idx
        self.send, self.recv, self.shards, self.lands, self.token = _gather_start(rest_shards, rest_lands, after, "gather_rest_start")

    def _pass(self, lo, hi, after, tag):
        send2, recv2, lands, token = _gather_pass(self.lands[lo:hi], self.recv, after, "gather_pass_" + tag, first=lo)
        self.passed = (lo, hi, send2, recv2, lands)
        return token

    def _wait(self, after, tag):
        lo, hi, send2, recv2, lands = self.passed
        return _gather_wait(self.shards[lo:hi], lands, self.send, self.recv, send2, recv2, after, "gather_wait_" + tag, first=lo)

    def mixed(self, gla_o, gla_norm_w):
        return _tie(gla_norm_w, self._pass(0, 1, gla_o, "out"))

    def w_out(self, merged, norm2_w):
        (wo_all,) = self._wait(merged, "out")
        return wo_all.reshape(D, D), _tie(norm2_w, self._pass(1, 3, merged, "up"))

    def w_up(self, v2):
        wg_all, wu_all = self._wait(v2, "up")
        self._pass(3, 4, v2, "down")
        return wg_all.reshape(FH, D), wu_all.reshape(FH, D)

    def w_down(self, ff):
        return self._wait(ff, "down")[0].reshape(FH, D)

    def _reduce(self, tag, names, grads, recv1, rows):
        psums = [_pair_add(g, r, self.c_idx, "pair_add_" + nm, tr) for g, r, nm, tr in zip(grads, recv1, names, rows)]
        *flight, token = _chip_start(psums, "reduce_chips_start_" + tag)
        return dict(tag=tag, names=names, rows=rows, flight=flight), token

    def ffn_grads(self, d_wg, d_wu, d_wd):
        self.ffn_pair = _pair_start([d.reshape(4, 2, FS, D) for d in (d_wg, d_wu, d_wd)], "reduce_pair_start_ffn")
        return self.ffn_pair[-1]

    def ffn_reduce(self, dv2, norm2_w):
        send, recv, grads, lands, _ = self.ffn_pair
        grads, recv1 = _pair_wait(send, recv, grads, lands, dv2, "reduce_pair_wait_ffn")
        self.ffn, token = self._reduce("ffn", ["w_ffn_gate", "w_ffn_up", "w_ffn_down"], grads, recv1, [176, 176, 176])
        return _tie(norm2_w, token)

    def in_grads(self, d_wmain, d_wlr, w_lr):
        mine, recv = _disassemble_exchange(d_wmain, d_wlr)
        self.in_names, self.in_rows = ["w_in", "w_out"], [808, 256]
        *self.in_hop, token = _hop_start([_add_blocks(mine, recv, "pair_add_w_in", 808)], "reduce_hop_start_in")
        return _tie(w_lr, token)

    def in_reduce(self, d_wo):
        d_wo4 = d_wo.reshape(4, 2, D // NDEV, D)
        send, recv, (d_wo4,), lands, _ = _pair_start([d_wo4], "reduce_pair_start_out")
        (d_wo4,), (wo_recv,) = _pair_wait(send, recv, [d_wo4], lands, self.update(self.ffn, d_wo), "reduce_pair_wait_out")
        wo_psum = _pair_add(d_wo4, wo_recv, self.c_idx, "pair_add_w_out", 256)
        (psum,), (land,) = _hop_wait(*self.in_hop, wo_psum, "reduce_hop_wait_in")
        psum = _hop_add(psum, land, _hop_pos()[2].astype(jnp.int32).reshape(1), "hop_add_w_in", 808)
        *flight, token = _chip_start([psum, wo_psum], "reduce_chips_start_in", nrel=[2, 3])
        self.inw = dict(tag="in", names=self.in_names, rows=self.in_rows, flight=flight)
        return token


def _local_step(xs, tgt, u, norm1_w, gla_gate_b, attn_sinks, gla_norm_w, norm2_w, fnw, w_main, w_lr, w2p, comm):
    proj =_mm(u, w_main, tb=True, tm=1024, tn=1280, tk=D, name="in_proj")
    plr = _mm(u, w_lr, tb=True, tm=1024, tn=LANE, tk=D, name="in_proj_lr")
    attn_o = _attn_fwd(proj, attn_sinks)
    gla_o, states = _gla_fwd(proj, plr, w2p, gla_gate_b)
    merged = _merge_fwd(attn_o, gla_o, proj, comm.mixed(gla_o, gla_norm_w))
    wo, norm2_w = comm.w_out(merged, norm2_w)
    h1 = _mm(merged, wo, tm=1024, tn=512, tk=D, res=xs, name="out_proj")
    v2 = _rmsnorm_fwd(h1, norm2_w, "norm2_fwd")
    wg_all, wu_all = comm.w_up(v2)
    fa, fb, ff = _ffn_up(v2, wg_all, wu_all)
    wd_all = comm.w_down(ff)
    h2 = _mm(ff, wd_all, tm=1024, tn=1024, tk=FH // 2, res=h1, name="ffn_down")
    dh2, dh2b, d_fnw, loss_part = _loss_head(h2, fnw, tgt)

    da, db = _ffn_dact(dh2b, wd_all, fa, fb)
    Tn = xs.shape[0]
    d_wd = _mm(ff, dh2b, ta=True, tm=512, tn=D, tk=Tn, out_dtype=BF16, name="ffn_dwd")
    d_wg = _mm(da, v2, ta=True, tm=512, tn=D, tk=Tn, out_dtype=BF16, name="ffn_dwg")
    d_wu = _mm(db, v2, ta=True, tm=512, tn=D, tk=Tn, out_dtype=BF16, name="ffn_dwu")
    dv2 = _mm(da, wg_all, tm=1024, tn=1024, tk=FH // 2, after=comm.ffn_grads(d_wg, d_wu, d_wd), name="ffn_dv2_gate")
    dv2 = _mm(db, wu_all, tm=1024, tn=1024, tk=FH // 2, res=dv2, name="ffn_dv2_up")
    norm2_w = comm.ffn_reduce(dv2, norm2_w)
    dh1, dh1b, d_n2 = _rmsnorm_bwd(dv2, h1, norm2_w, dh2, "norm2_bwd")
    dmerged = _mm(dh1b, wo, tb=True, tm=1024, tn=512, tk=D, name="out_proj_dx")
    d_attn, d_gla, d_gates, d_gnw = _merge_bwd(dmerged, attn_o, gla_o, proj, gla_norm_w)
    d_q, d_kv, d_sinks = _attn_bwd(proj, attn_sinks, attn_o, d_attn)
    d_gqk, d_gv, d_plr, d_w2p, d_gb = _gla_bwd(proj, plr, w2p, gla_gate_b, states, d_gla)
    dproj = jnp.concatenate([d_q, d_kv, d_gqk, d_gv, d_gates], axis=1)
    d_wmain = _mm(dproj, u, ta=True, tm=640, tn=D, tk=xs.shape[0], out_dtype=BF16, name="in_proj_dw")
    d_wlr = _mm(d_plr, u, ta=True, tm=LANE, tn=1024, tk=xs.shape[0], out_dtype=BF16, name="in_proj_lr_dw")
    du_lr = _mm(d_plr, comm.in_grads(d_wmain, d_wlr, w_lr), tm=1024, tn=1024, tk=LANE, name="in_proj_lr_dx")
    d_wo = _mm(merged, dh1b, ta=True, tm=1024, tn=512, tk=xs.shape[0], out_dtype=BF16, after=du_lr, name="out_proj_dw")
    du = _mm(dproj, w_main, tm=1024, tn=1024, tk=2560, res=du_lr, after=comm.in_reduce(d_wo), name="in_proj_dx")
    dx, _, d_n1 = _rmsnorm_bwd(du, xs, norm1_w, dh1, "norm1_bwd")
    return dx, loss_part, d_w2p, d_gb, d_sinks, d_gnw, d_n1, d_n2, d_fnw


def kernel(x, norm1_w, w_in, gla_gate_w2, gla_gate_b, attn_sinks, gla_norm_w, w_out, norm2_w, w_ffn_gate, w_ffn_up, w_ffn_down, final_norm_w, loss_target, m_norm1_w, m_w_in, m_gla_gate_w2, m_gla_gate_b, m_attn_sinks, m_gla_norm_w, m_w_out, m_norm2_w, m_w_ffn_gate, m_w_ffn_up, m_w_ffn_down, m_final_norm_w, v_norm1_w, v_w_in, v_gla_gate_w2, v_gla_gate_b, v_attn_sinks, v_gla_norm_w, v_w_out, v_norm2_w, v_w_ffn_gate, v_w_ffn_up, v_w_ffn_down, v_final_norm_w):
    xs, tgt = x[0], loss_target[0]
    fnw = final_norm_w.reshape(1, D)
    c_idx = lax.axis_index("c").astype(jnp.int32).reshape(1)
    dev = 4 * lax.axis_index("x") + 2 * lax.axis_index("y") + lax.axis_index("c")

    chip_idx = (2 * lax.axis_index("x") + lax.axis_index("y")).astype(jnp.int32).reshape(1)

    shift = (WS - WSTEP) * dev
    window = lax.dynamic_update_slice(jnp.zeros((WWIN, D), BF16), jnp.transpose(w_in[0]).astype(BF16), (shift, 0))
    w2_land = lax.dynamic_update_slice(lax.empty((NDEV, RANK, LANE), F32), gla_gate_w2, (dev, 0, 0))
    *sems1, win_srcs, win_lands, tok = _win_start([window, gla_gate_w2[0]], [lax.empty((NDEV, WWIN, D), BF16), w2_land], "gather_in_start")
    tr2 = lambda t: jnp.transpose(t[0])
    rows3 = lambda t: jnp.transpose(t[0] + tok[0, 0]).reshape(WS, D // LANE, LANE)
    rest = [(w + tok[0, 0]).astype(BF16) for w in (w_out[0], tr2(w_ffn_gate), tr2(w_ffn_up), w_ffn_down[0])]
    rest_lands = [lax.dynamic_update_slice(lax.empty((NDEV,) + s.shape, s.dtype), s[None], (dev, 0, 0)) for s in rest]
    win3 = [rows3(t) for t in (w_in, m_w_in, v_w_in)]
    *sems2, win_lands, tok = _win_hand_on(win_lands, sems1[1], rest + rest_lands + win3, "gather_in_hand_on")
    u = _rmsnorm_fwd(xs, _tie(norm1_w, tok), "norm1_fwd")
    *sems3, win_lands, tok = _win_last(win_lands, sems2[1], [u], "gather_in_last")
    comm = _Comm(rest, rest_lands, tok, c_idx)
    win_all, w2_all = _win_wait(win_srcs, win_lands, sems1, sems2, sems3, comm.token, "gather_in_wait")
    w_main, w_lr = _assemble_w_in(win_all, window)
    w2p = jnp.pad(jnp.transpose(w2_all, (1, 0, 2)).reshape(RANK, GH * DK), ((0, LANE - RANK), (0, 0)))

    big = {}

    def update(grp, after):
        psums, parts = _chip_wait(*grp["flight"], after, "reduce_chips_wait_" + grp["tag"])
        for nm, ps, pt, tr in zip(grp["names"], psums, parts, grp["rows"]):
            w, m, v = {"w_in": (w_in, m_w_in, v_w_in), "w_out": (w_out, m_w_out, v_w_out), "w_ffn_gate": (w_ffn_gate, m_w_ffn_gate, v_w_ffn_gate),
                       "w_ffn_up": (w_ffn_up, m_w_ffn_up, v_w_ffn_up), "w_ffn_down": (w_ffn_down, m_w_ffn_down, v_w_ffn_down)}[nm]
            if nm == "w_in":
                g_win = _sum_parts(ps, pt, chip_idx, "sum_w_in", tr, 1024)
                g3 = lax.dynamic_slice(g_win, (shift, 0), (WS, D)).reshape(WS, D // LANE, LANE)
                out3 = (g3,) + tuple(_adamw_rows(*win3, g3, "adamw_w_in", 178))
                big[nm] = [jnp.transpose(t.reshape(WS, D))[None] for t in out3]
            elif nm in ("w_ffn_gate", "w_ffn_up"):
                big[nm] = [jnp.transpose(t)[None] for t in _adamw(tr2(w), tr2(m), tr2(v), ps, pt, chip_idx, "adamw_" + nm, tr)]
            else:
                big[nm] = [t[None] for t in _adamw(w[0], m[0], v[0], ps, pt, chip_idx, "adamw_" + nm, tr)]
            after = big[nm][0]
        return after

    comm.update = update
    dx, loss_part, d_w2p, d_gb, d_sinks, d_gnw, d_n1, d_n2, d_fnw = _local_step(
        xs, tgt, u, norm1_w, gla_gate_b, attn_sinks, gla_norm_w, norm2_w, fnw, w_main, w_lr, w2p, comm)

    pack = jnp.concatenate([_pack_small(d_n1, d_gb, d_sinks, d_gnw, d_n2, d_fnw, loss_part),
                            d_w2p[:RANK].reshape(GW2_ROWS, LANE)], axis=0)
    small = _sum_devices(_gather_small(pack))

    update(comm.inw, dx)
    g_small = small[:SMALL_ROWS]
    sm = _adamw_plain(_pack_small(norm1_w, gla_gate_b, attn_sinks, gla_norm_w, norm2_w, final_norm_w),
                      _pack_small(m_norm1_w, m_gla_gate_b, m_attn_sinks, m_gla_norm_w, m_norm2_w, m_final_norm_w),
                      _pack_small(v_norm1_w, v_gla_gate_b, v_attn_sinks, v_gla_norm_w, v_norm2_w, v_final_norm_w), g_small, "adamw_small")
    g_w2 = lax.dynamic_slice_in_dim(small[SMALL_ROWS:].reshape(RANK, GH * DK), dev * LANE, LANE, axis=1)
    w2 = [g_w2[None]] + [t[None] for t in _adamw_plain(gla_gate_w2[0], m_gla_gate_w2[0], v_gla_gate_w2[0], g_w2, "adamw_w2")]
    loss = g_small.reshape(-1)[S_LOSS]

    sg, sd, sm2, sv2 = [_unpack_small(t) for t in (g_small,) + tuple(sm)]

    def group(i, s):
        return (s[0], big["w_in"][i], w2[i], s[1], s[2], s[3], big["w_out"][i], s[4], big["w_ffn_gate"][i], big["w_ffn_up"][i],
                big["w_ffn_down"][i], s[5])

    return (loss, dx[None], *group(0, sg), *group(1, sd), *group(2, sm2), *group(3, sv2))
```

```python
import functools

import jax
import jax.numpy as jnp
from jax import lax
from jax.experimental import pallas as pl
from jax.experimental.pallas import tpu as pltpu

F32, BF16 = jnp.float32, jnp.bfloat16
HIGHEST = lax.Precision.HIGHEST

D = 2048
HD, NQ, NKV, GRP, WIN = 64, 32, 4, 8, 128
GH, DK, DV, RANK, GC = 4, 256, 512, 16, 64
FH, NDEV = 5632, 8
FS = FH // NDEV
DIN = 12816
WS = DIN // NDEV
EPS = 1e-6
MASKV = -1e30
LANE = 128

C_AQ, C_AK, C_AV, C_GQ, C_GK, C_GV, C_GR, C_GA, C_GB, NMAIN = 0, 2048, 2304, 2560, 3584, 4608, 6656, 8704, 10752, 12800
C_LR = 6656
WSTEP, WWIN = 1600, 1616

LR, B1, B2, AEPS, WD, STEP = 0.001, 0.9, 0.999, 1e-08, 0.01, 10

S_N1, S_GB, S_SK, S_GN, S_N2, S_FN, S_LOSS, SMALL_N = 0, 2048, 3072, 3104, 3616, 5664, 7712, 8192
SMALL_ROWS = SMALL_N // LANE
GW2_ROWS = RANK * GH * DK // LANE
PACK_ROWS = SMALL_ROWS + GW2_ROWS

MESH = pl.DeviceIdType.MESH


def _dot(a, b, ta=False, tb=False, prec=None):
    dn = (((0,) if ta else (1,), (1,) if tb else (0,)), ((), ()))
    return lax.dot_general(a, b, dn, preferred_element_type=F32, precision=prec)


def _sigmoid(x):
    return 1.0 / (1.0 + jnp.exp(-x))


VMEM_LIMIT = 56 * 1024 * 1024


def _cp(*sem):
    return pltpu.CompilerParams(dimension_semantics=sem, vmem_limit_bytes=VMEM_LIMIT)


def _mm(a, b, *, ta=False, tb=False, tm, tn, tk, out_dtype=F32, res=None, after=None, name):
    M, K = (a.shape[1], a.shape[0]) if ta else a.shape
    N = b.shape[0] if tb else b.shape[1]
    tm, tn, tk = min(tm, M), min(tn, N), min(tk, K)
    nk = K // tk
    assert M % tm == 0 and N % tn == 0 and K % tk == 0
    a_spec = pl.BlockSpec((tk, tm), lambda i, j, k: (k, i)) if ta else pl.BlockSpec((tm, tk), lambda i, j, k: (i, k))
    b_spec = pl.BlockSpec((tn, tk), lambda i, j, k: (j, k)) if tb else pl.BlockSpec((tk, tn), lambda i, j, k: (k, j))
    o_spec = pl.BlockSpec((tm, tn), lambda i, j, k: (i, j))
    has_res = res is not None

    def body(*refs):
        a_ref, b_ref = refs[0], refs[1]
        r_ref = refs[2] if has_res else None
        o_ref = refs[2 + has_res + (after is not None)]
        p = _dot(a_ref[...].astype(BF16), b_ref[...].astype(BF16), ta, tb)
        if nk == 1:
            if has_res:
                p = p + r_ref[...]
            o_ref[...] = p.astype(out_dtype)
        else:
            acc = refs[-1]
            k = pl.program_id(2)

            @pl.when(k == 0)
            def _():
                acc[...] = (p + r_ref[...]) if has_res else p

            @pl.when(k > 0)
            def _():
                acc[...] += p

            @pl.when(k == nk - 1)
            def _():
                o_ref[...] = acc[...].astype(out_dtype)

    return pl.pallas_call(
        body, name=name,
        out_shape=jax.ShapeDtypeStruct((M, N), out_dtype),
        grid=(M // tm, N // tn, nk),
        in_specs=[a_spec, b_spec] + ([o_spec] if has_res else []) + ([pl.BlockSpec(memory_space=pl.ANY)] if after is not None else []),
        out_specs=o_spec,
        scratch_shapes=[pltpu.VMEM((tm, tn), F32)] if nk > 1 else [],
        compiler_params=_cp("parallel", "parallel", "arbitrary"),
    )(*((a, b) + ((res,) if has_res else ()) + ((after,) if after is not None else ())))


def _rmsnorm_fwd(x, w, name, tm=256):
    Tn = x.shape[0]

    def body(x_ref, w_ref, o_ref):
        xv = x_ref[...]
        r = lax.rsqrt(jnp.mean(xv * xv, axis=1, keepdims=True) + EPS)
        o_ref[...] = (xv * r * w_ref[...]).astype(BF16)

    return pl.pallas_call(
        body, name=name, out_shape=jax.ShapeDtypeStruct((Tn, D), BF16), grid=(Tn // tm,),
        in_specs=[pl.BlockSpec((tm, D), lambda i: (i, 0)), pl.BlockSpec((1, D), lambda i: (0, 0))],
        out_specs=pl.BlockSpec((tm, D), lambda i: (i, 0)), compiler_params=_cp("parallel"),
    )(x, w)


def _rmsnorm_bwd(dy, h, w, res, name, tm=256):
    Tn = h.shape[0]

    def body(dy_ref, h_ref, w_ref, res_ref, dh_ref, dhb_ref, dw_ref):
        hv, dyv = h_ref[...], dy_ref[...]
        r = lax.rsqrt(jnp.mean(hv * hv, axis=1, keepdims=True) + EPS)
        g = dyv * w_ref[...]
        dh = res_ref[...] + r * g - hv * (r * r * r * jnp.mean(g * hv, axis=1, keepdims=True))
        dh_ref[...] = dh
        dhb_ref[...] = dh.astype(BF16)
        part = jnp.sum(dyv * hv * r, axis=0, keepdims=True)

        @pl.when(pl.program_id(0) == 0)
        def _():
            dw_ref[...] = part

        @pl.when(pl.program_id(0) > 0)
        def _():
            dw_ref[...] += part

    row = pl.BlockSpec((tm, D), lambda i: (i, 0))
    vec = pl.BlockSpec((1, D), lambda i: (0, 0))
    return pl.pallas_call(
        body, name=name,
        out_shape=(jax.ShapeDtypeStruct((Tn, D), F32), jax.ShapeDtypeStruct((Tn, D), BF16), jax.ShapeDtypeStruct((1, D), F32)),
        grid=(Tn // tm,), in_specs=[row, row, vec, row], out_specs=(row, row, vec), compiler_params=_cp("arbitrary"),
    )(dy, h, w, res)


def _loss_head(h2, wf, tgt, name="loss_head", tm=256):
    Tn = h2.shape[0]

    def body(h_ref, w_ref, t_ref, dh_ref, dhb_ref, dw_ref, loss_ref):
        hv, wv = h_ref[...], w_ref[...]
        r = lax.rsqrt(jnp.mean(hv * hv, axis=1, keepdims=True) + EPS)
        hn = hv * r
        e = hn * wv - t_ref[...]
        dy = e * (1.0 / D)
        g = dy * wv
        dh = r * g - hv * (r * r * r * jnp.mean(g * hv, axis=1, keepdims=True))
        dh_ref[...] = dh
        dhb_ref[...] = dh.astype(BF16)
        part = jnp.sum(dy * hn, axis=0, keepdims=True)
        lpart = (0.5 / D) * jnp.sum(jnp.sum(e * e, axis=1, keepdims=True), axis=0, keepdims=True)

        @pl.when(pl.program_id(0) == 0)
        def _():
            dw_ref[...] = part
            loss_ref[...] = lpart

        @pl.when(pl.program_id(0) > 0)
        def _():
            dw_ref[...] += part
            loss_ref[...] += lpart

    row = pl.BlockSpec((tm, D), lambda i: (i, 0))
    vec = pl.BlockSpec((1, D), lambda i: (0, 0))
    one = pl.BlockSpec((1, 1), lambda i: (0, 0))
    return pl.pallas_call(
        body, name=name,
        out_shape=(jax.ShapeDtypeStruct((Tn, D), F32), jax.ShapeDtypeStruct((Tn, D), BF16), jax.ShapeDtypeStruct((1, D), F32),
                   jax.ShapeDtypeStruct((1, 1), F32)),
        grid=(Tn // tm,), in_specs=[row, vec, row], out_specs=(row, row, vec, one), compiler_params=_cp("arbitrary"),
    )(h2, wf, tgt)


def _attn_mask(n):
    qi = lax.broadcasted_iota(jnp.int32, (NKV, GRP * WIN, 2 * WIN), 1) % WIN
    ki = lax.broadcasted_iota(jnp.int32, (NKV, GRP * WIN, 2 * WIN), 2)
    rel = qi + WIN - ki
    return (rel >= 0) & (rel < WIN) & ((n > 0) | (ki >= WIN))


def _kv_heads(prev_ref, cur_ref):
    return jnp.stack([jnp.concatenate([prev_ref[:, h * HD:(h + 1) * HD], cur_ref[:, h * HD:(h + 1) * HD]], axis=0) for h in range(NKV)])


def _q_heads(ref):
    return jnp.stack([jnp.concatenate([ref[:, (h * GRP + g) * HD:(h * GRP + g + 1) * HD] for g in range(GRP)], axis=0) for h in range(NKV)])


def _attn_probs(q_ref, kc_ref, kp_ref, sink_ref, mask):
    kk = _kv_heads(kp_ref, kc_ref).astype(BF16)
    qs = _q_heads(q_ref).astype(BF16)
    s = jnp.einsum('hqd,hkd->hqk', qs, kk, preferred_element_type=F32) * (HD ** -0.5)
    s = jnp.where(mask, s, MASKV)
    sink = jnp.stack([jnp.concatenate([jnp.full((WIN, 1), sink_ref[0, h * GRP + g], F32) for g in range(GRP)], axis=0) for h in range(NKV)])
    m = jnp.maximum(jnp.max(s, axis=2, keepdims=True), sink)
    e = jnp.exp(s - m)
    es = jnp.exp(sink - m)
    inv = 1.0 / (jnp.sum(e, axis=2, keepdims=True) + es)
    return e * inv, es * inv, qs, kk


def _attn_specs(nb, last):
    cur = lambda n: jnp.minimum(n, last)
    prev = lambda n: jnp.maximum(jnp.minimum(n, last) - 1, 0)
    return [
        pl.BlockSpec((WIN, NQ * HD), lambda n: (cur(n), C_AQ // (NQ * HD))),
        pl.BlockSpec((WIN, NKV * HD), lambda n: (cur(n), C_AK // (NKV * HD))),
        pl.BlockSpec((WIN, NKV * HD), lambda n: (prev(n), C_AK // (NKV * HD))),
        pl.BlockSpec((WIN, NKV * HD), lambda n: (cur(n), C_AV // (NKV * HD))),
        pl.BlockSpec((WIN, NKV * HD), lambda n: (prev(n), C_AV // (NKV * HD))),
    ]


def _attn_fwd(proj, sinks, name="attn_fwd"):
    Tn = proj.shape[0]
    nb = Tn // WIN

    def body(q_ref, kc_ref, kp_ref, vc_ref, vp_ref, sink_ref, o_ref):
        p, _, _, _ = _attn_probs(q_ref, kc_ref, kp_ref, sink_ref, _attn_mask(pl.program_id(0)))
        o = jnp.einsum('hqk,hkd->hqd', p.astype(BF16), _kv_heads(vp_ref, vc_ref).astype(BF16), preferred_element_type=F32)
        for h in range(NKV):
            for g in range(GRP):
                o_ref[:, (h * GRP + g) * HD:(h * GRP + g + 1) * HD] = o[h, g * WIN:(g + 1) * WIN, :]

    return pl.pallas_call(
        body, name=name, out_shape=jax.ShapeDtypeStruct((Tn, D), F32), grid=(nb,),
        in_specs=_attn_specs(nb, nb - 1) + [pl.BlockSpec(memory_space=pltpu.SMEM)],
        out_specs=pl.BlockSpec((WIN, D), lambda n: (n, 0)), compiler_params=_cp("parallel"),
    )(proj, proj, proj, proj, proj, sinks)


def _attn_bwd(proj, sinks, o, do, name="attn_bwd"):
    Tn = proj.shape[0]
    nb = Tn // WIN
    KW = NKV * HD

    def body(q_ref, kc_ref, kp_ref, vc_ref, vp_ref, o_ref, do_ref, sink_ref, dq_ref, dkv_ref, dsk_ref, carry, cur):
        n = pl.program_id(0)

        @pl.when(n == 0)
        def _():
            carry[...] = jnp.zeros_like(carry)
            dsk_ref[...] = jnp.zeros_like(dsk_ref)

        @pl.when(n < nb)
        def _():
            p, ps, qs, kk = _attn_probs(q_ref, kc_ref, kp_ref, sink_ref, _attn_mask(n))
            vv = _kv_heads(vp_ref, vc_ref).astype(BF16)
            dos = _q_heads(do_ref)
            delta = jnp.sum(dos * _q_heads(o_ref), axis=2, keepdims=True)
            dosb = dos.astype(BF16)
            dp = jnp.einsum('hqd,hkd->hqk', dosb, vv, preferred_element_type=F32)
            ds = (p * (dp - delta) * (HD ** -0.5)).astype(BF16)
            dq = jnp.einsum('hqk,hkd->hqd', ds, kk, preferred_element_type=F32)
            dkk = jnp.einsum('hqk,hqd->hkd', ds, qs, preferred_element_type=F32)
            dvv = jnp.einsum('hqk,hqd->hkd', p.astype(BF16), dosb, preferred_element_type=F32)
            dsk = ps * delta
            for h in range(NKV):
                for g in range(GRP):
                    i = h * GRP + g
                    dq_ref[:, i * HD:(i + 1) * HD] = dq[h, g * WIN:(g + 1) * WIN, :].astype(BF16)
                    dsk_ref[:, i:i + 1] -= jnp.sum(dsk[h, g * WIN:(g + 1) * WIN, :], axis=0, keepdims=True)
                dkv_ref[:, h * HD:(h + 1) * HD] = (carry[:, h * HD:(h + 1) * HD] + dkk[h, :WIN, :]).astype(BF16)
                dkv_ref[:, KW + h * HD:KW + (h + 1) * HD] = (carry[:, KW + h * HD:KW + (h + 1) * HD] + dvv[h, :WIN, :]).astype(BF16)
                cur[:, h * HD:(h + 1) * HD] = dkk[h, WIN:, :]
                cur[:, KW + h * HD:KW + (h + 1) * HD] = dvv[h, WIN:, :]
            carry[...] = cur[...]

        @pl.when(n == nb)
        def _():
            dkv_ref[...] = carry[...].astype(BF16)

    last = nb - 1
    row = pl.BlockSpec((WIN, D), lambda n: (jnp.minimum(n, last), 0))
    return pl.pallas_call(
        body, name=name,
        out_shape=(jax.ShapeDtypeStruct((Tn, D), BF16), jax.ShapeDtypeStruct((Tn, 2 * KW), BF16), jax.ShapeDtypeStruct((1, NQ), F32)),
        grid=(nb + 1,),
        in_specs=_attn_specs(nb, last) + [row, row, pl.BlockSpec(memory_space=pltpu.SMEM)],
        out_specs=(row, pl.BlockSpec((WIN, 2 * KW), lambda n: (jnp.maximum(n - 1, 0), 0)), pl.BlockSpec((1, NQ), lambda n: (0, 0))),
        scratch_shapes=[pltpu.VMEM((WIN, 2 * KW), F32), pltpu.VMEM((WIN, 2 * KW), F32)],
        compiler_params=_cp("arbitrary"),
    )(proj, proj, proj, proj, proj, o, do, sinks)


def _tri(lower):
    r = lax.broadcasted_iota(jnp.int32, (GC, GC), 0)
    c = lax.broadcasted_iota(jnp.int32, (GC, GC), 1)
    return r >= c if lower else r <= c


def _per_head(a):
    return jnp.stack([a[:, h * DK:(h + 1) * DK] for h in range(GH)])


def _all_heads(a):
    return jnp.concatenate([a[h] for h in range(GH)], axis=1)


def _gla_gates(lr, w2_ref, gb_ref):
    logit = _dot(lr, w2_ref[...].astype(BF16)) + gb_ref[...]
    la = (jnp.minimum(logit, 0.0) - jnp.log(1.0 + jnp.exp(-jnp.abs(logit)))) * (1.0 / 16.0)
    g = _dot(_tri(True).astype(F32), la, prec=HIGHEST)
    return logit, g


def _bmm(spec, a, b):
    return jnp.einsum(spec, a, b, preferred_element_type=F32)


def _gla_specs(nc, rev):
    idx = (lambda n: nc - 1 - n) if rev else (lambda n: n)
    half = 2 * DK
    return (
        [pl.BlockSpec((GC, half), lambda n, j=j: (idx(n), C_GQ // half + j)) for j in range(2)]
        + [pl.BlockSpec((GC, half), lambda n, j=j: (idx(n), C_GK // half + j)) for j in range(2)]
        + [pl.BlockSpec((GC, DV), lambda n, h=h: (idx(n), C_GV // DV + h)) for h in range(GH)]
        + [pl.BlockSpec((GC, LANE), lambda n: (idx(n), 0)), pl.BlockSpec((LANE, GH * DK), lambda n: (0, 0)),
           pl.BlockSpec((1, GH * DK), lambda n: (0, 0))])


def _gla_heads(refs):
    return (lambda h: refs[h // 2][:, (h % 2) * DK:(h % 2 + 1) * DK], lambda h: refs[2 + h // 2][:, (h % 2) * DK:(h % 2 + 1) * DK],
            lambda h: refs[4 + h][...])


def _gla_fwd(proj, plr, w2p, gb, name="gla_fwd"):
    Tn = proj.shape[0]
    nc = Tn // GC

    def body(*refs):
        qh, kh, vh = _gla_heads(refs)
        lr_ref, w2_ref, gb_ref, o_ref, st_ref, S = refs[8:]

        @pl.when(pl.program_id(0) == 0)
        def _():
            S[...] = jnp.zeros_like(S)

        heads = lambda f: jnp.stack([f(h) for h in range(GH)])
        _, g_all = _gla_gates(lr_ref[...].astype(BF16), w2_ref, gb_ref)
        g = _per_head(g_all)
        gl = g[:, GC - 1:GC, :]
        k = heads(kh)
        v = heads(vh).astype(BF16)
        qd = (heads(qh) * (DK ** -0.5) * jnp.exp(g)).astype(BF16)
        ki = (k * jnp.exp(-g)).astype(BF16)
        ke = (k * jnp.exp(gl - g)).astype(BF16)
        att = jnp.where(_tri(True)[None], _bmm('hid,hjd->hij', qd, ki), 0.0).astype(BF16)
        sp = S[...]
        st_ref[0] = sp
        o = _bmm('hij,hjv->hiv', att, v) + _bmm('hid,hvd->hiv', qd, sp.astype(BF16))
        for h in range(GH):
            o_ref[:, h * DV:(h + 1) * DV] = o[h]
        S[...] = sp * jnp.exp(gl) + _bmm('hjv,hjd->hvd', v, ke)

    return pl.pallas_call(
        body, name=name,
        out_shape=(jax.ShapeDtypeStruct((Tn, GH * DV), F32), jax.ShapeDtypeStruct((nc, GH, DV, DK), F32)),
        grid=(nc,), in_specs=_gla_specs(nc, False),
        out_specs=(pl.BlockSpec((GC, GH * DV), lambda n: (n, 0)), pl.BlockSpec((1, GH, DV, DK), lambda n: (n, 0, 0, 0))),
        scratch_shapes=[pltpu.VMEM((GH, DV, DK), F32)], compiler_params=_cp("arbitrary"),
    )(*([proj] * 8), plr, w2p, gb)


def _gla_bwd(proj, plr, w2p, gb, states, do, name="gla_bwd"):
    Tn = proj.shape[0]
    nc = Tn // GC

    def body(*refs):
        qh, kh, vh = _gla_heads(refs)
        lr_ref, w2_ref, gb_ref, st_ref, do_ref, dqk_ref, dv_ref, dlr_ref, dw2_ref, dgb_ref, dS = refs[8:]

        @pl.when(pl.program_id(0) == 0)
        def _():
            dS[...] = jnp.zeros_like(dS)
            dw2_ref[...] = jnp.zeros_like(dw2_ref)
            dgb_ref[...] = jnp.zeros_like(dgb_ref)

        heads = lambda f: jnp.stack([f(h) for h in range(GH)])
        lr = lr_ref[...].astype(BF16)
        causal = _tri(True)[None]
        last_row = lax.broadcasted_iota(jnp.int32, (GH, GC, DK), 1) == GC - 1
        logit, g_all = _gla_gates(lr, w2_ref, gb_ref)
        g = _per_head(g_all)
        gl = g[:, GC - 1:GC, :]
        egl = jnp.exp(gl)
        eg, eng, ege = jnp.exp(g), jnp.exp(-g), jnp.exp(gl - g)
        k = heads(kh)
        v = heads(vh).astype(BF16)
        dob = heads(lambda h: do_ref[:, h * DV:(h + 1) * DV]).astype(BF16)
        qd = heads(qh) * (DK ** -0.5) * eg
        ki = k * eng
        ke = k * ege
        qdb, kib, keb = qd.astype(BF16), ki.astype(BF16), ke.astype(BF16)
        att = jnp.where(causal, _bmm('hid,hjd->hij', qdb, kib), 0.0).astype(BF16)
        datt = jnp.where(causal, _bmm('hiv,hjv->hij', dob, v), 0.0).astype(BF16)
        sp = st_ref[0]
        dsn = dS[...]
        dsnb = dsn.astype(BF16)
        dv = (_bmm('hij,hiv->hjv', att, dob) + _bmm('hjd,hvd->hjv', keb, dsnb)).astype(BF16)
        dqd = _bmm('hij,hjd->hid', datt, kib) + _bmm('hiv,hvd->hid', dob, sp.astype(BF16))
        dki = _bmm('hij,hid->hjd', datt, qdb)
        dke = _bmm('hjv,hvd->hjd', v, dsnb)
        ddec = jnp.sum(dsn * sp, axis=1, keepdims=True)
        dS[...] = dsn * egl + _bmm('hiv,hid->hvd', dob, qdb)
        dke_ke = dke * ke
        dgl = jnp.sum(dke_ke, axis=1, keepdims=True) + ddec * egl
        dg = dqd * qd - dki * ki - dke_ke + jnp.where(last_row, dgl, 0.0)
        dq = (dqd * ((DK ** -0.5) * eg)).astype(BF16)
        dk = (dki * eng + dke * ege).astype(BF16)
        for h in range(GH):
            dv_ref[:, h * DV:(h + 1) * DV] = dv[h]
            dqk_ref[:, h * DK:(h + 1) * DK] = dq[h]
            dqk_ref[:, GH * DK + h * DK:GH * DK + (h + 1) * DK] = dk[h]
        dla = _dot(_tri(False).astype(F32), _all_heads(dg), prec=HIGHEST)
        dlogit = dla * (1.0 / 16.0) * _sigmoid(-logit)
        dlb = dlogit.astype(BF16)
        dlr_ref[...] = _dot(dlb, w2_ref[...].astype(BF16), tb=True).astype(BF16)
        dw2_ref[...] += _dot(lr, dlb, ta=True)
        dgb_ref[...] += jnp.sum(dlogit, axis=0, keepdims=True)

    rev = lambda n: nc - 1 - n
    row = pl.BlockSpec((GC, GH * DV), lambda n: (rev(n), 0))
    return pl.pallas_call(
        body, name=name,
        out_shape=(jax.ShapeDtypeStruct((Tn, 2 * GH * DK), BF16), jax.ShapeDtypeStruct((Tn, GH * DV), BF16),
                   jax.ShapeDtypeStruct((Tn, LANE), BF16), jax.ShapeDtypeStruct((LANE, GH * DK), F32),
                   jax.ShapeDtypeStruct((1, GH * DK), F32)),
        grid=(nc,),
        in_specs=_gla_specs(nc, True) + [pl.BlockSpec((1, GH, DV, DK), lambda n: (rev(n), 0, 0, 0)), row],
        out_specs=(row, row, pl.BlockSpec((GC, LANE), lambda n: (rev(n), 0)), pl.BlockSpec((LANE, GH * DK), lambda n: (0, 0)),
                   pl.BlockSpec((1, GH * DK), lambda n: (0, 0))),
        scratch_shapes=[pltpu.VMEM((GH, DV, DK), F32)], compiler_params=_cp("arbitrary"),
    )(*([proj] * 8), plr, w2p, gb, states, do)


def _merge_specs(tm):
    row = pl.BlockSpec((tm, D), lambda i: (i, 0))
    gates = [pl.BlockSpec((tm, DV), lambda i, j=c // DV + h: (i, j)) for c in (C_GR, C_GA, C_GB) for h in range(GH)]
    return row, gates, pl.BlockSpec((1, DV), lambda i: (0, 0))


def _merge_fwd(a, go, proj, gnw, name="merge_fwd", tm=256):
    Tn = a.shape[0]

    def body(a_ref, go_ref, *rest):
        gates, w_ref, m_ref = rest[:3 * GH], rest[3 * GH], rest[3 * GH + 1]
        for h in range(GH):
            sl = slice(h * DV, (h + 1) * DV)
            gov = go_ref[:, sl]
            r = lax.rsqrt(jnp.mean(gov * gov, axis=1, keepdims=True) + EPS)
            gr = gates[h][...]
            g2 = gov * r * w_ref[...] * (gr * _sigmoid(gr))
            m_ref[:, sl] = (_sigmoid(gates[GH + h][...]) * a_ref[:, sl] + _sigmoid(gates[2 * GH + h][...]) * g2).astype(BF16)

    row, gates, vec = _merge_specs(tm)
    return pl.pallas_call(
        body, name=name, out_shape=jax.ShapeDtypeStruct((Tn, D), BF16), grid=(Tn // tm,),
        in_specs=[row, row] + gates + [vec], out_specs=row, compiler_params=_cp("parallel"),
    )(a, go, *([proj] * (3 * GH)), gnw)


def _merge_bwd(dm, a, go, proj, gnw, name="merge_bwd", tm=256):
    Tn = a.shape[0]

    def body(dm_ref, a_ref, go_ref, *rest):
        gates = rest[:3 * GH]
        w_ref, da_ref, dgo_ref, dg_ref, dw_ref = rest[3 * GH:]
        wv = w_ref[...]
        dw = jnp.zeros((1, DV), F32)
        for h in range(GH):
            sl = slice(h * DV, (h + 1) * DV)
            dmv, av, gov, gr = dm_ref[:, sl], a_ref[:, sl], go_ref[:, sl], gates[h][...]
            sa, sb, sg = _sigmoid(gates[GH + h][...]), _sigmoid(gates[2 * GH + h][...]), _sigmoid(gr)
            r = lax.rsqrt(jnp.mean(gov * gov, axis=1, keepdims=True) + EPS)
            gn0 = gov * r
            gn = gn0 * wv
            silu = gr * sg
            dg2 = dmv * sb
            da_ref[:, sl] = dmv * sa
            dg_ref[:, D + h * DV:D + (h + 1) * DV] = (dmv * av * sa * (1.0 - sa)).astype(BF16)
            dg_ref[:, 2 * D + h * DV:2 * D + (h + 1) * DV] = (dg2 * gn * silu * (1.0 - sb)).astype(BF16)
            dg_ref[:, sl] = (dg2 * gn * (sg * (1.0 + gr * (1.0 - sg)))).astype(BF16)
            dgn = dg2 * silu
            dw = dw + jnp.sum(dgn * gn0, axis=0, keepdims=True)
            gg = dgn * wv
            dgo_ref[:, sl] = r * gg - gov * (r * r * r * jnp.mean(gg * gov, axis=1, keepdims=True))

        @pl.when(pl.program_id(0) == 0)
        def _():
            dw_ref[...] = dw

        @pl.when(pl.program_id(0) > 0)
        def _():
            dw_ref[...] += dw

    row, gates, vec = _merge_specs(tm)
    return pl.pallas_call(
        body, name=name,
        out_shape=(jax.ShapeDtypeStruct((Tn, D), F32), jax.ShapeDtypeStruct((Tn, D), F32), jax.ShapeDtypeStruct((Tn, 3 * D), BF16),
                   jax.ShapeDtypeStruct((1, DV), F32)),
        grid=(Tn // tm,), in_specs=[row, row, row] + gates + [vec],
        out_specs=(row, row, pl.BlockSpec((tm, 3 * D), lambda i: (i, 0)), vec), compiler_params=_cp("arbitrary"),
    )(dm, a, go, *([proj] * (3 * GH)), gnw)


def _ffn_up(v2, wgt, wut, name="ffn_up", tm=1024, tn=512):
    Tn = v2.shape[0]
    tm = min(tm, Tn)

    def body(v_ref, wg_ref, wu_ref, a_ref, b_ref, ff_ref):
        vv = v_ref[...]
        a = _dot(vv, wg_ref[...], tb=True)
        b = _dot(vv, wu_ref[...], tb=True)
        a_ref[...] = a.astype(BF16)
        b_ref[...] = b.astype(BF16)
        ff_ref[...] = (a * _sigmoid(a) * b).astype(BF16)

    w = pl.BlockSpec((tn, D), lambda j, i: (j, 0))
    act = pl.BlockSpec((tm, tn), lambda j, i: (i, j))
    return pl.pallas_call(
        body, name=name,
        out_shape=(jax.ShapeDtypeStruct((Tn, FH), BF16), jax.ShapeDtypeStruct((Tn, FH), BF16), jax.ShapeDtypeStruct((Tn, FH), BF16)),
        grid=(FH // tn, Tn // tm), in_specs=[pl.BlockSpec((tm, D), lambda j, i: (i, 0)), w, w], out_specs=(act, act, act),
        compiler_params=_cp("parallel", "parallel"),
    )(v2, wgt, wut)


def _ffn_dact(dh2b, wd, a, b, name="ffn_dact", tm=1024, tn=512):
    Tn = dh2b.shape[0]
    tm = min(tm, Tn)

    def body(d_ref, w_ref, a_ref, b_ref, da_ref, db_ref):
        dff = _dot(d_ref[...], w_ref[...], tb=True)
        av = a_ref[...].astype(F32)
        sg = _sigmoid(av)
        da_ref[...] = (dff * b_ref[...].astype(F32) * (sg * (1.0 + av * (1.0 - sg)))).astype(BF16)
        db_ref[...] = (dff * (av * sg)).astype(BF16)

    act = pl.BlockSpec((tm, tn), lambda j, i: (i, j))
    return pl.pallas_call(
        body, name=name,
        out_shape=(jax.ShapeDtypeStruct((Tn, FH), BF16), jax.ShapeDtypeStruct((Tn, FH), BF16)),
        grid=(FH // tn, Tn // tm),
        in_specs=[pl.BlockSpec((tm, D), lambda j, i: (i, 0)), pl.BlockSpec((tn, D), lambda j, i: (j, 0)), act, act],
        out_specs=(act, act), compiler_params=_cp("parallel", "parallel"),
    )(dh2b, wd, a, b)


def _adam_math(w, g, m, v):
    m2 = B1 * m + (1.0 - B1) * g
    v2 = B2 * v + (1.0 - B2) * (g * g)
    mh = m2 / (1.0 - B1 ** STEP)
    vh = v2 / (1.0 - B2 ** STEP)
    return -LR * (mh / (jnp.sqrt(vh) + AEPS) + WD * w), m2, v2


def _sum_blocks(o_ref, p_ref):
    g = o_ref[...].astype(F32)
    for j in range(p_ref.shape[0]):
        g = g + p_ref[j].astype(F32)
    return g


def _adamw(w, m, v, psums, parts, chip_idx, name, tr):
    R, C = w.shape

    def body(s_ref, w_ref, m_ref, v_ref, o_ref, p_ref, g_ref, d_ref, m2_ref, v2_ref):
        g = _sum_blocks(o_ref, p_ref)
        d, m2, v2 = _adam_math(w_ref[...], g, m_ref[...], v_ref[...])
        g_ref[...] = g
        d_ref[...] = d
        m2_ref[...] = m2
        v2_ref[...] = v2

    blk = pl.BlockSpec((tr, C), lambda i, s: (i, 0))
    out = jax.ShapeDtypeStruct((R, C), F32)
    grid_spec = pltpu.PrefetchScalarGridSpec(
        num_scalar_prefetch=1, grid=(R // tr,),
        in_specs=[blk, blk, blk, pl.BlockSpec((None, tr, C), lambda i, s: (s[0], i, 0)),
                  pl.BlockSpec((parts.shape[0], tr, C), lambda i, s: (0, i, 0))],
        out_specs=(blk, blk, blk, blk),
    )
    return pl.pallas_call(body, name=name, out_shape=(out, out, out, out), grid_spec=grid_spec, compiler_params=_cp("parallel"),
                          )(chip_idx, w, m, v, psums, parts)


def _adamw_given(w, m, v, g, name, tr, tc):
    R, C = w.shape

    def body(w_ref, m_ref, v_ref, g_ref, d_ref, m2_ref, v2_ref):
        d, m2, v2 = _adam_math(w_ref[...], g_ref[...], m_ref[...], v_ref[...])
        d_ref[...] = d
        m2_ref[...] = m2
        v2_ref[...] = v2

    blk = pl.BlockSpec((tr, tc), lambda i, j: (i, j))
    out = jax.ShapeDtypeStruct(w.shape, F32)
    return pl.pallas_call(body, name=name, out_shape=(out, out, out), grid=(pl.cdiv(R, tr), C // tc), in_specs=[blk] * 4,
                          out_specs=(blk, blk, blk), compiler_params=_cp("parallel", "parallel"))(w, m, v, g)


def _sum_parts(psums, parts, chip_idx, name, tr, tc):
    _, R, C = psums.shape

    def body(s_ref, o_ref, p_ref, g_ref):
        g_ref[...] = _sum_blocks(o_ref, p_ref)

    grid_spec = pltpu.PrefetchScalarGridSpec(
        num_scalar_prefetch=1, grid=(R // tr, C // tc),
        in_specs=[pl.BlockSpec((None, tr, tc), lambda i, j, s: (s[0], i, j)),
                  pl.BlockSpec((parts.shape[0], tr, tc), lambda i, j, s: (0, i, j))],
        out_specs=pl.BlockSpec((tr, tc), lambda i, j, s: (i, j)),
    )
    return pl.pallas_call(body, name=name, out_shape=jax.ShapeDtypeStruct((R, C), F32), grid_spec=grid_spec,
                          compiler_params=_cp("parallel", "parallel"))(chip_idx, psums, parts)


def _adamw_plain(w, m, v, g, name):
    def body(w_ref, m_ref, v_ref, g_ref, d_ref, m2_ref, v2_ref):
        d, m2, v2 = _adam_math(w_ref[...], g_ref[...], m_ref[...], v_ref[...])
        d_ref[...] = d
        m2_ref[...] = m2
        v2_ref[...] = v2

    out = jax.ShapeDtypeStruct(w.shape, F32)
    return pl.pallas_call(body, name=name, out_shape=(out, out, out))(w, m, v, g)


def _sum_devices(pack_all, name="sum_small"):
    def body(p_ref, o_ref):
        s = p_ref[0]
        for k in range(1, NDEV):
            s = s + p_ref[k]
        o_ref[...] = s

    return pl.pallas_call(body, name=name, out_shape=jax.ShapeDtypeStruct(pack_all.shape[1:], F32))(pack_all)


def _pair_add(g5, recv, c_idx, name, tr):
    _, _, R, C = g5.shape

    def body(c_ref, g_ref, r_ref, o_ref):
        o_ref[...] = (g_ref[...].astype(F32) + r_ref[...].astype(F32)).astype(BF16)

    grid_spec = pltpu.PrefetchScalarGridSpec(
        num_scalar_prefetch=1, grid=(4, R // tr),
        in_specs=[pl.BlockSpec((None, None, tr, C), lambda q, i, c: (q, c[0], i, 0)), pl.BlockSpec((None, tr, C), lambda q, i, c: (q, i, 0))],
        out_specs=pl.BlockSpec((None, tr, C), lambda q, i, c: (q, i, 0)),
    )
    return pl.pallas_call(
        body, name=name, out_shape=jax.ShapeDtypeStruct((4, R, C), BF16), grid_spec=grid_spec,
        compiler_params=_cp("parallel", "parallel"),
    )(c_idx, g5, recv)


_ANY = pl.BlockSpec(memory_space=pl.ANY)


def _mesh_pos():
    x, y, c = lax.axis_index("x"), lax.axis_index("y"), lax.axis_index("c")
    return x, y, c, [(1 - x, y), (x, 1 - y), (1 - x, 1 - y)]


def _gather_small(pack, name="gather_small"):
    def body(pk, pk_all, psend, precv, loc):
        x, y, c, chips = _mesh_pos()
        me_slot = 4 * x + 2 * y + c
        sib = (x, y, 1 - c)
        own = pltpu.make_async_copy(pk, pk_all.at[me_slot], loc)
        own.start()
        peers = [sib] + [(*chip, c) for chip in chips] + [(*chip, 1 - c) for chip in chips]
        small = [pltpu.make_async_remote_copy(src_ref=pk, dst_ref=pk_all.at[me_slot], send_sem=psend.at[k], recv_sem=precv.at[k],
                                              device_id=p, device_id_type=MESH) for k, p in enumerate(peers)]
        for d in small:
            d.start()
        for k, p in enumerate(peers):
            pltpu.make_async_remote_copy(src_ref=pk, dst_ref=pk_all.at[4 * p[0] + 2 * p[1] + p[2]], send_sem=psend.at[k],
                                         recv_sem=precv.at[k], device_id=p, device_id_type=MESH).wait_recv()
        for d in small:
            d.wait_send()
        own.wait()

    return pl.pallas_call(
        body, name=name, out_shape=jax.ShapeDtypeStruct((NDEV,) + pack.shape, pack.dtype), in_specs=[_ANY], out_specs=_ANY,
        scratch_shapes=[pltpu.SemaphoreType.DMA((7,)), pltpu.SemaphoreType.DMA((7,)), pltpu.SemaphoreType.DMA(())],
    )(pack)


def _main_row(g):
    return g if g < C_LR else g - RANK


def _window_pieces(lo, hi):
    out = []
    for a, b, where in ((lo, min(hi, C_LR), "main"), (max(lo, C_LR), min(hi, C_LR + RANK), "lr"), (max(lo, C_LR + RANK), hi, "main")):
        if a < b:
            out.append((a, b, where, _main_row(a) if where == "main" else a - C_LR))
    return out


def _assemble_w_in(windows, own, name="assemble_w_in"):
    edges = NDEV - 1

    def body(b_ref, own_ref, main_ref, lr_ref, buf, ebuf, in_sems, out_sems, esems):
        dev = 4 * lax.axis_index("x") + 2 * lax.axis_index("y") + lax.axis_index("c")

        def load(k):
            return pltpu.make_async_copy(b_ref.at[k], buf.at[k % 2], in_sems.at[k % 2])

        def start_load(k):
            pl.when(dev == k)(pltpu.make_async_copy(own_ref, buf.at[k % 2], in_sems.at[k % 2]).start)
            pl.when(dev != k)(load(k).start)

        lr_ref[RANK:, :] = jnp.zeros((LANE - RANK, D), BF16)
        start_load(0)
        pending, edge_out = [], []
        for k in range(NDEV):
            s = k % 2
            load(k).wait()
            if k:
                ebuf[k - 1] = buf[1 - s, WSTEP:WWIN, :] + buf[s, 0:16, :]
                edge_out.append(pltpu.make_async_copy(ebuf.at[k - 1], main_ref.at[pl.ds(_main_row(WSTEP * k), 16)], esems.at[k - 1]))
                edge_out[-1].start()
                for d in pending:
                    d.wait()
            if k + 1 < NDEV:
                start_load(k + 1)
            pending = []
            lo = WSTEP * k + (16 if k else 0)
            hi = WSTEP * k + (WWIN if k == NDEV - 1 else WSTEP)
            for a, b, where, dst in _window_pieces(lo, hi):
                if where == "lr":
                    lr_ref[dst:dst + b - a, :] = buf[s, a - WSTEP * k:b - WSTEP * k, :]
                else:
                    pending.append(pltpu.make_async_copy(buf.at[s, pl.ds(a - WSTEP * k, b - a)], main_ref.at[pl.ds(dst, b - a)],
                                                         out_sems.at[2 * s + len(pending)]))
                    pending[-1].start()
        for d in pending + edge_out:
            d.wait()

    return pl.pallas_call(
        body, name=name,
        out_shape=(jax.ShapeDtypeStruct((NMAIN, D), BF16), jax.ShapeDtypeStruct((LANE, D), BF16)),
        in_specs=[_ANY, _ANY], out_specs=(_ANY, pl.BlockSpec(memory_space=pltpu.VMEM)),
        scratch_shapes=[pltpu.VMEM((2, WWIN, D), BF16), pltpu.VMEM((edges, 16, D), BF16), pltpu.SemaphoreType.DMA((2,)),
                        pltpu.SemaphoreType.DMA((4,)), pltpu.SemaphoreType.DMA((edges,))],
        compiler_params=pltpu.CompilerParams(vmem_limit_bytes=VMEM_LIMIT),
    )(windows, own)


def _disassemble_exchange(d_main, d_lr, name="disassemble_exchange"):
    def body(main_ref, lr_ref, mine_ref, recv_ref, buf, in_sems, keep_sems, send_sems, recv_sems):
        x, y, c, _ = _mesh_pos()
        sib = (x, y, 1 - c)

        def loads(k):
            s, out = k % 2, []
            for a, b, where, src0 in _window_pieces(WSTEP * k, WSTEP * k + WWIN):
                if where == "main":
                    out.append(pltpu.make_async_copy(main_ref.at[pl.ds(src0, b - a)], buf.at[s, pl.ds(a - WSTEP * k, b - a)],
                                                     in_sems.at[2 * s + len(out)]))
            return out

        def keep(k):
            return pltpu.make_async_copy(buf.at[k % 2], mine_ref.at[k // 2], keep_sems.at[k % 2])

        def send(k):
            return _rcopy(buf.at[k % 2], recv_ref.at[k // 2], send_sems.at[k % 2], recv_sems.at[k // 2], sib)

        def store_start(k):
            pl.when(c == k % 2)(keep(k).start)
            pl.when(c != k % 2)(send(k).start)

        def store_wait(k):
            pl.when(c == k % 2)(keep(k).wait)
            pl.when(c != k % 2)(send(k).wait_send)

        for d in loads(0):
            d.start()
        for k in range(NDEV):
            for d in loads(k):
                d.wait()
            for a, b, where, src0 in _window_pieces(WSTEP * k, WSTEP * k + WWIN):
                if where == "lr":
                    buf[k % 2, a - WSTEP * k:b - WSTEP * k, :] = lr_ref[src0:src0 + b - a, :]
            if k:
                store_wait(k - 1)
            if k + 1 < NDEV:
                for d in loads(k + 1):
                    d.start()
            store_start(k)
        store_wait(NDEV - 1)
        for chip in range(NDEV // 2):
            _rcopy(buf.at[0], recv_ref.at[chip], send_sems.at[0], recv_sems.at[chip], sib).wait_recv()

    half = jax.ShapeDtypeStruct((NDEV // 2, WWIN, D), BF16)
    return pl.pallas_call(
        body, name=name, out_shape=(half, half),
        in_specs=[_ANY, pl.BlockSpec(memory_space=pltpu.VMEM)], out_specs=(_ANY, _ANY),
        scratch_shapes=[pltpu.VMEM((2, WWIN, D), BF16), pltpu.SemaphoreType.DMA((4,)), pltpu.SemaphoreType.DMA((2,)),
                        pltpu.SemaphoreType.DMA((2,)), pltpu.SemaphoreType.DMA((NDEV // 2,))],
        compiler_params=pltpu.CompilerParams(vmem_limit_bytes=VMEM_LIMIT),
    )(d_main, d_lr)


def _add_blocks(a, b, name, tr):
    _, R, C = a.shape

    def body(a_ref, b_ref, o_ref):
        o_ref[...] = (a_ref[...].astype(F32) + b_ref[...].astype(F32)).astype(BF16)

    blk = pl.BlockSpec((None, tr, C), lambda q, i: (q, i, 0))
    return pl.pallas_call(body, name=name, out_shape=jax.ShapeDtypeStruct(a.shape, BF16), grid=(a.shape[0], R // tr),
                          in_specs=[blk, blk], out_specs=blk, compiler_params=_cp("parallel", "parallel"))(a, b)


_HBM = pl.BlockSpec(memory_space=pltpu.HBM)
_SEM = pl.BlockSpec(memory_space=pltpu.SEMAPHORE)
_VMEM = pl.BlockSpec(memory_space=pltpu.VMEM)
_SIDE = pltpu.CompilerParams(has_side_effects=pltpu.SideEffectType.DATAFLOW_SIDE_EFFECTING)
_TOKEN = jax.ShapeDtypeStruct((8, LANE), F32)


def _hbm(a):
    return pltpu.with_memory_space_constraint(a, pltpu.HBM)


def _hbm_like(arrs):
    return tuple(pltpu.HBM(a.shape, a.dtype) for a in arrs)


def _tie(x, token):
    return x + token[0, 0].astype(x.dtype)


def _chip_copies(ins, lands, send, recv, nrel):
    x, y, c, chips = _mesh_pos()
    first = [sum(nrel[:a]) for a in range(len(ins))]
    return [pltpu.make_async_remote_copy(src_ref=ins[a].at[2 * chip[0] + chip[1]], dst_ref=lands[a].at[j], send_sem=send.at[first[a] + j],
                                         recv_sem=recv.at[first[a] + j], device_id=(*chip, c), device_id_type=MESH)
            for a in range(len(ins)) for j, chip in enumerate(chips[:nrel[a]])]


def _chip_start(psums, name, nrel=None):
    n = len(psums)
    nrel = nrel or [3] * n
    lands = [lax.empty((r,) + p.shape[1:], p.dtype) for r, p in zip(nrel, psums)]

    def body(*refs):
        for d in _chip_copies(refs[:n], refs[n:2 * n], refs[2 * n], refs[2 * n + 1], nrel):
            d.start()
        refs[-1][...] = jnp.zeros_like(refs[-1])

    sems = pltpu.SemaphoreType.DMA((sum(nrel),))
    out = pl.pallas_call(
        body, name=name, out_shape=(sems, sems) + _hbm_like(psums) + _hbm_like(lands) + (_TOKEN,),
        in_specs=[_HBM] * (2 * n), out_specs=(_SEM, _SEM) + (_HBM,) * (2 * n) + (_VMEM,),
        input_output_aliases={i: 2 + i for i in range(2 * n)}, compiler_params=_SIDE,
    )(*[_hbm(a) for a in list(psums) + lands])
    return out[0], out[1], list(out[2:2 + n]), list(out[2 + n:2 + 2 * n]), out[-1]


def _chip_wait(send, recv, psums, lands, after, name):
    n = len(psums)
    nrel = [l.shape[0] for l in lands]

    def body(*refs):
        for d in _chip_copies(refs[:n], refs[n:2 * n], refs[2 * n], refs[2 * n + 1], nrel):
            d.wait_send()
            d.wait_recv()

    out = pl.pallas_call(
        body, name=name, out_shape=_hbm_like(psums) + _hbm_like(lands),
        in_specs=[_HBM] * (2 * n) + [_SEM, _SEM, _ANY], out_specs=(_HBM,) * (2 * n),
        input_output_aliases={i: i for i in range(2 * n)}, compiler_params=_SIDE,
    )(*psums, *lands, send, recv, after)
    return list(out[:n]), list(out[n:])


def _hop_pos():
    x, y, c, _ = _mesh_pos()
    north = c == 1
    via = (jnp.where(north, 1 - x, x), jnp.where(north, y, 1 - y))
    return (*via, c), 2 * (1 - x) + (1 - y), jnp.where(north, 2 * x + (1 - y), 2 * (1 - x) + y)


def _hop_copies(ins, lands, send, recv):
    to, mine, _ = _hop_pos()
    return [pltpu.make_async_remote_copy(src_ref=ins[a].at[mine], dst_ref=lands[a], send_sem=send.at[a], recv_sem=recv.at[a],
                                         device_id=to, device_id_type=MESH) for a in range(len(ins))]


def _hop_start(psums, name):
    n = len(psums)
    lands = [lax.empty(p.shape[1:], p.dtype) for p in psums]

    def body(*refs):
        for d in _hop_copies(refs[:n], refs[n:2 * n], refs[2 * n], refs[2 * n + 1]):
            d.start()
        refs[-1][...] = jnp.zeros_like(refs[-1])

    sems = pltpu.SemaphoreType.DMA((n,))
    out = pl.pallas_call(
        body, name=name, out_shape=(sems, sems) + _hbm_like(psums) + _hbm_like(lands) + (_TOKEN,),
        in_specs=[_HBM] * (2 * n), out_specs=(_SEM, _SEM) + (_HBM,) * (2 * n) + (_VMEM,),
        input_output_aliases={i: 2 + i for i in range(2 * n)}, compiler_params=_SIDE,
    )(*[_hbm(a) for a in list(psums) + lands])
    return out[0], out[1], list(out[2:2 + n]), list(out[2 + n:2 + 2 * n]), out[-1]


def _hop_wait(send, recv, psums, lands, after, name):
    n = len(psums)

    def body(*refs):
        for d in _hop_copies(refs[:n], refs[n:2 * n], refs[2 * n], refs[2 * n + 1]):
            d.wait_send()
            d.wait_recv()

    out = pl.pallas_call(
        body, name=name, out_shape=_hbm_like(psums) + _hbm_like(lands),
        in_specs=[_HBM] * (2 * n) + [_SEM, _SEM, _ANY], out_specs=(_HBM,) * (2 * n),
        input_output_aliases={i: i for i in range(2 * n)}, compiler_params=_SIDE,
    )(*psums, *lands, send, recv, after)
    return list(out[:n]), list(out[n:])


def _hop_add(psums, land, idx, name, tr):
    _, R, C = psums.shape

    def body(s_ref, p_ref, l_ref, o_ref):
        o_ref[...] = (p_ref[...].astype(F32) + l_ref[...].astype(F32)).astype(BF16)

    blk = pl.BlockSpec((None, tr, C), lambda i, s: (s[0], i, 0))
    grid_spec = pltpu.PrefetchScalarGridSpec(num_scalar_prefetch=1, grid=(R // tr,),
                                             in_specs=[blk, pl.BlockSpec((tr, C), lambda i, s: (i, 0))], out_specs=blk)
    return pl.pallas_call(body, name=name, out_shape=jax.ShapeDtypeStruct(psums.shape, BF16), grid_spec=grid_spec,
                          input_output_aliases={1: 0}, compiler_params=_cp("parallel"))(idx, psums, land)


def _pair_copies(ins, lands, send, recv):
    x, y, c, _ = _mesh_pos()
    return [pltpu.make_async_remote_copy(src_ref=ins[a].at[:, 1 - c], dst_ref=lands[a], send_sem=send.at[a], recv_sem=recv.at[a],
                                         device_id=(x, y, 1 - c), device_id_type=MESH) for a in range(len(ins))]


def _pair_start(grads, name):
    n = len(grads)
    lands = [lax.empty((4,) + g.shape[2:], g.dtype) for g in grads]

    def body(*refs):
        for d in _pair_copies(refs[:n], refs[n:2 * n], refs[2 * n], refs[2 * n + 1]):
            d.start()
        refs[-1][...] = jnp.zeros_like(refs[-1])

    sems = pltpu.SemaphoreType.DMA((n,))
    out = pl.pallas_call(
        body, name=name, out_shape=(sems, sems) + _hbm_like(grads) + _hbm_like(lands) + (_TOKEN,),
        in_specs=[_HBM] * (2 * n), out_specs=(_SEM, _SEM) + (_HBM,) * (2 * n) + (_VMEM,),
        input_output_aliases={i: 2 + i for i in range(2 * n)}, compiler_params=_SIDE,
    )(*[_hbm(a) for a in list(grads) + lands])
    return out[0], out[1], list(out[2:2 + n]), list(out[2 + n:2 + 2 * n]), out[-1]


def _pair_wait(send, recv, grads, lands, after, name):
    n = len(grads)

    def body(*refs):
        for d in _pair_copies(refs[:n], refs[n:2 * n], refs[2 * n], refs[2 * n + 1]):
            d.wait_send()
            d.wait_recv()

    out = pl.pallas_call(
        body, name=name, out_shape=_hbm_like(grads) + _hbm_like(lands),
        in_specs=[_HBM] * (2 * n) + [_SEM, _SEM, _ANY], out_specs=(_HBM,) * (2 * n),
        input_output_aliases={i: i for i in range(2 * n)}, compiler_params=_SIDE,
    )(*grads, *lands, send, recv, after)
    return list(out[:n]), list(out[n:])


def _slot(chip, c):
    return 4 * chip[0] + 2 * chip[1] + c


def _gather_start(shards, lands, after, name):
    n = len(shards)

    def body(*refs):
        src, land, send, recv = refs[:n], refs[n:2 * n], refs[2 * n + 1], refs[2 * n + 2]
        x, y, c, chips = _mesh_pos()
        for a in range(n):
            for k, to in enumerate([(x, y, 1 - c)] + [(*chip, c) for chip in chips]):
                pltpu.make_async_remote_copy(src_ref=src[a], dst_ref=land[a].at[_slot((x, y), c)], send_sem=send.at[4 * a + k],
                                             recv_sem=recv.at[4 * a + k], device_id=to, device_id_type=MESH).start()
        refs[-1][...] = jnp.zeros_like(refs[-1])

    sems = pltpu.SemaphoreType.DMA((4 * n,))
    out = pl.pallas_call(
        body, name=name, out_shape=(sems, sems) + _hbm_like(shards) + _hbm_like(lands) + (_TOKEN,),
        in_specs=[_HBM] * (2 * n) + [_ANY], out_specs=(_SEM, _SEM) + (_HBM,) * (2 * n) + (_VMEM,),
        input_output_aliases={i: 2 + i for i in range(2 * n)}, compiler_params=_SIDE,
    )(*[_hbm(a) for a in list(shards) + list(lands)], after)
    return out[0], out[1], list(out[2:2 + n]), list(out[2 + n:2 + 2 * n]), out[-1]


def _gather_pass(lands, recv, after, name, first=0):
    n = len(lands)

    def body(*refs):
        land, recv1 = refs[:n], refs[n]
        send2, recv2 = refs[n + 2], refs[n + 3]
        x, y, c, chips = _mesh_pos()
        for a in range(n):
            for j, chip in enumerate(chips):
                blk = land[a].at[_slot(chip, c)]
                pltpu.make_async_remote_copy(src_ref=blk, dst_ref=blk, send_sem=send2.at[3 * a + j], recv_sem=recv1.at[4 * (first + a) + 1 + j],
                                             device_id=(*chip, c), device_id_type=MESH).wait_recv()
                pltpu.make_async_remote_copy(src_ref=blk, dst_ref=blk, send_sem=send2.at[3 * a + j], recv_sem=recv2.at[3 * a + j],
                                             device_id=(x, y, 1 - c), device_id_type=MESH).start()
        refs[-1][...] = jnp.zeros_like(refs[-1])

    sems = pltpu.SemaphoreType.DMA((3 * n,))
    out = pl.pallas_call(
        body, name=name, out_shape=(sems, sems) + _hbm_like(lands) + (_TOKEN,),
        in_specs=[_HBM] * n + [_SEM, _ANY], out_specs=(_SEM, _SEM) + (_HBM,) * n + (_VMEM,),
        input_output_aliases={i: 2 + i for i in range(n)}, compiler_params=_SIDE,
    )(*lands, recv, after)
    return out[0], out[1], list(out[2:2 + n]), out[-1]


def _gather_wait(shards, lands, send, recv, send2, recv2, after, name, first=0):
    n = len(lands)

    def body(*refs):
        src, land = refs[:n], refs[n:2 * n]
        send1, recv1, snd2, rcv2 = refs[2 * n:2 * n + 4]
        x, y, c, chips = _mesh_pos()
        sib = (x, y, 1 - c)
        for a in range(n):
            for k in range(4):
                pltpu.make_async_remote_copy(src_ref=src[a], dst_ref=land[a].at[_slot((x, y), c)], send_sem=send1.at[4 * (first + a) + k],
                                             recv_sem=recv1.at[4 * (first + a) + k], device_id=sib, device_id_type=MESH).wait_send()
            blk = land[a].at[_slot((x, y), 1 - c)]
            pltpu.make_async_remote_copy(src_ref=blk, dst_ref=blk, send_sem=send1.at[4 * (first + a)], recv_sem=recv1.at[4 * (first + a)],
                                         device_id=sib, device_id_type=MESH).wait_recv()
            for j, chip in enumerate(chips):
                mine, theirs = land[a].at[_slot(chip, c)], land[a].at[_slot(chip, 1 - c)]
                pltpu.make_async_remote_copy(src_ref=mine, dst_ref=mine, send_sem=snd2.at[3 * a + j], recv_sem=rcv2.at[3 * a + j],
                                             device_id=sib, device_id_type=MESH).wait_send()
                pltpu.make_async_remote_copy(src_ref=theirs, dst_ref=theirs, send_sem=snd2.at[3 * a + j], recv_sem=rcv2.at[3 * a + j],
                                             device_id=sib, device_id_type=MESH).wait_recv()

    out = pl.pallas_call(
        body, name=name, out_shape=_hbm_like(shards) + _hbm_like(lands),
        in_specs=[_HBM] * (2 * n) + [_SEM] * 4 + [_ANY], out_specs=(_HBM,) * (2 * n),
        input_output_aliases={i: i for i in range(2 * n)}, compiler_params=_SIDE,
    )(*shards, *lands, send, recv, send2, recv2, after)
    return list(out[n:])


def _win_tree():
    x, y, c, chips = _mesh_pos()
    north = c == 1
    handed = (jnp.where(north, 1 - x, x), jnp.where(north, y, 1 - y))
    hand_to = (jnp.where(north, x, 1 - x), jnp.where(north, 1 - y, y))
    return x, y, c, chips, handed, hand_to


def _blk(land, chip, c):
    return land.at[_slot(chip, c)]


def _rcopy(src, dst, send, recv, to):
    return pltpu.make_async_remote_copy(src_ref=src, dst_ref=dst, send_sem=send, recv_sem=recv, device_id=to, device_id_type=MESH)


def _win_start(shards, lands, name):
    n = len(shards)

    def body(*refs):
        src, land, send, recv = refs[:n], refs[n:2 * n], refs[2 * n], refs[2 * n + 1]
        x, y, c, chips, _, _ = _win_tree()
        for a in range(n):
            for k, to in enumerate([(x, y, 1 - c), (*chips[0], c), (*chips[1], c)]):
                _rcopy(src[a], _blk(land[a], (x, y), c), send.at[3 * a + k], recv.at[3 * a + k], to).start()
        refs[-1][...] = jnp.zeros_like(refs[-1])

    sems = pltpu.SemaphoreType.DMA((3 * n,))
    out = pl.pallas_call(
        body, name=name, out_shape=(sems, sems) + _hbm_like(shards) + _hbm_like(lands) + (_TOKEN,),
        in_specs=[_HBM] * (2 * n), out_specs=(_SEM, _SEM) + (_HBM,) * (2 * n) + (_VMEM,),
        input_output_aliases={i: 2 + i for i in range(2 * n)}, compiler_params=_SIDE,
    )(*[_hbm(a) for a in list(shards) + list(lands)])
    return out[0], out[1], list(out[2:2 + n]), list(out[2 + n:2 + 2 * n]), out[-1]


def _win_hand_on(lands, recv1, after, name):
    n, m = len(lands), len(after)

    def body(*refs):
        land, rcv1 = refs[:n], refs[n]
        send2, recv2 = refs[n + 1 + m], refs[n + 2 + m]
        x, y, c, chips, handed, hand_to = _win_tree()
        for a in range(n):
            for j in range(2):
                blk = _blk(land[a], chips[j], c)
                _rcopy(blk, blk, send2.at[3 * a], rcv1.at[3 * a + 1 + j], (*chips[j], c)).wait_recv()
            blk = _blk(land[a], handed, c)
            _rcopy(blk, blk, send2.at[3 * a], recv2.at[3 * a], (*hand_to, c)).start()
            for j in range(2):
                blk = _blk(land[a], chips[j], c)
                _rcopy(blk, blk, send2.at[3 * a + 1 + j], recv2.at[3 * a + 1 + j], (x, y, 1 - c)).start()
        refs[-1][...] = jnp.zeros_like(refs[-1])

    sems = pltpu.SemaphoreType.DMA((3 * n,))
    out = pl.pallas_call(
        body, name=name, out_shape=(sems, sems) + _hbm_like(lands) + (_TOKEN,),
        in_specs=[_HBM] * n + [_SEM] + [_ANY] * m, out_specs=(_SEM, _SEM) + (_HBM,) * n + (_VMEM,),
        input_output_aliases={i: 2 + i for i in range(n)}, compiler_params=_SIDE,
    )(*lands, recv1, *after)
    return out[0], out[1], list(out[2:2 + n]), out[-1]


def _win_last(lands, recv2, after, name):
    n, m = len(lands), len(after)

    def body(*refs):
        land, rcv2 = refs[:n], refs[n]
        send3, recv3 = refs[n + 1 + m], refs[n + 2 + m]
        x, y, c, chips, _, hand_to = _win_tree()
        for a in range(n):
            blk = _blk(land[a], chips[2], c)
            _rcopy(blk, blk, send3.at[a], rcv2.at[3 * a], (*hand_to, c)).wait_recv()
            _rcopy(blk, blk, send3.at[a], recv3.at[a], (x, y, 1 - c)).start()
        refs[-1][...] = jnp.zeros_like(refs[-1])

    sems = pltpu.SemaphoreType.DMA((n,))
    out = pl.pallas_call(
        body, name=name, out_shape=(sems, sems) + _hbm_like(lands) + (_TOKEN,),
        in_specs=[_HBM] * n + [_SEM] + [_ANY] * m, out_specs=(_SEM, _SEM) + (_HBM,) * n + (_VMEM,),
        input_output_aliases={i: 2 + i for i in range(n)}, compiler_params=_SIDE,
    )(*lands, recv2, *after)
    return out[0], out[1], list(out[2:2 + n]), out[-1]


def _win_wait(shards, lands, sems1, sems2, sems3, after, name):
    n = len(lands)

    def body(*refs):
        src, land = refs[:n], refs[n:2 * n]
        send1, recv1, send2, recv2, send3, recv3 = refs[2 * n:2 * n + 6]
        x, y, c, chips, handed, hand_to = _win_tree()
        sib = (x, y, 1 - c)
        for a in range(n):
            own = _blk(land[a], (x, y), c)
            for k in range(3):
                _rcopy(src[a], own, send1.at[3 * a + k], recv1.at[3 * a + k], sib).wait_send()
            blk = _blk(land[a], (x, y), 1 - c)
            _rcopy(blk, blk, send1.at[3 * a], recv1.at[3 * a], sib).wait_recv()
            blk = _blk(land[a], handed, c)
            _rcopy(blk, blk, send2.at[3 * a], recv2.at[3 * a], sib).wait_send()
            for j in range(2):
                mine, theirs = _blk(land[a], chips[j], c), _blk(land[a], chips[j], 1 - c)
                _rcopy(mine, mine, send2.at[3 * a + 1 + j], recv2.at[3 * a + 1 + j], sib).wait_send()
                _rcopy(theirs, theirs, send2.at[3 * a + 1 + j], recv2.at[3 * a + 1 + j], sib).wait_recv()
            mine, theirs = _blk(land[a], chips[2], c), _blk(land[a], chips[2], 1 - c)
            _rcopy(mine, mine, send3.at[a], recv3.at[a], sib).wait_send()
            _rcopy(theirs, theirs, send3.at[a], recv3.at[a], sib).wait_recv()

    out = pl.pallas_call(
        body, name=name, out_shape=_hbm_like(shards) + _hbm_like(lands),
        in_specs=[_HBM] * (2 * n) + [_SEM] * 6 + [_ANY], out_specs=(_HBM,) * (2 * n),
        input_output_aliases={i: i for i in range(2 * n)}, compiler_params=_SIDE,
    )(*shards, *lands, *sems1, *sems2, *sems3, after)
    return list(out[n:])


def _pad_to(v, n):
    return jnp.pad(v, [(0, 0)] * (v.ndim - 1) + [(0, n - v.shape[-1])])


def _pack_small(n1, gb, sk, gn, n2, fn, extra=None):
    parts = [n1.reshape(-1), gb.reshape(-1), sk.reshape(-1), gn.reshape(-1), n2.reshape(-1), fn.reshape(-1)]
    flat = jnp.concatenate(parts + ([extra.reshape(-1)] if extra is not None else []))
    return _pad_to(flat, SMALL_N).reshape(SMALL_ROWS, LANE)


def _unpack_small(p):
    f = p.reshape(-1)
    return (f[S_N1:S_GB].reshape(1, D), f[S_GB:S_SK].reshape(1, GH * DK), f[S_SK:S_GN].reshape(1, NQ), f[S_GN:S_N2].reshape(1, DV),
            f[S_N2:S_FN].reshape(1, D), f[S_FN:S_LOSS].reshape(D))


class _NoComm:
    def __init__(self, wo, wg_all, wu_all, wd_all):
        self.rest = (wo, wg_all, wu_all, wd_all)

    def mixed(self, gla_o, gla_norm_w):
        return gla_norm_w

    def w_out(self, merged, norm2_w):
        return self.rest[0], norm2_w

    def w_up(self, v2):
        return self.rest[1], self.rest[2]

    def w_down(self, ff):
        return self.rest[3]

    def ffn_grads(self, d_wg, d_wu, d_wd):
        self.ffn = (d_wg, d_wu, d_wd)

    def ffn_reduce(self, dv2, norm2_w):
        return norm2_w

    def in_grads(self, d_wmain, d_wlr, w_lr):
        self.inw = (d_wmain, d_wlr)
        return w_lr

    def in_reduce(self, d_wo):
        self.inw += (d_wo,)
        return None


class _Comm:
    def __init__(self, rest_shards, rest_lands, after, c_idx):
        self.c_idx = c_idx
        self.send, self.recv, self.shards, self.lands, self.token = _gather_start(rest_shards, rest_lands, after, "gather_rest_start")

    def _pass(self, lo, hi, after, tag):
        send2, recv2, lands, token = _gather_pass(self.lands[lo:hi], self.recv, after, "gather_pass_" + tag, first=lo)
        self.passed = (lo, hi, send2, recv2, lands)
        return token

    def _wait(self, after, tag):
        lo, hi, send2, recv2, lands = self.passed
        return _gather_wait(self.shards[lo:hi], lands, self.send, self.recv, send2, recv2, after, "gather_wait_" + tag, first=lo)

    def mixed(self, gla_o, gla_norm_w):
        return _tie(gla_norm_w, self._pass(0, 1, gla_o, "out"))

    def w_out(self, merged, norm2_w):
        (wo_all,) = self._wait(merged, "out")
        return wo_all.reshape(D, D), _tie(norm2_w, self._pass(1, 3, merged, "up"))

    def w_up(self, v2):
        wg_all, wu_all = self._wait(v2, "up")
        self._pass(3, 4, v2, "down")
        return wg_all.reshape(FH, D), wu_all.reshape(FH, D)

    def w_down(self, ff):
        return self._wait(ff, "down")[0].reshape(FH, D)

    def _reduce(self, tag, names, grads, recv1, rows):
        psums = [_pair_add(g, r, self.c_idx, "pair_add_" + nm, tr) for g, r, nm, tr in zip(grads, recv1, names, rows)]
        *flight, token = _chip_start(psums, "reduce_chips_start_" + tag)
        return dict(tag=tag, names=names, rows=rows, flight=flight), token

    def ffn_grads(self, d_wg, d_wu, d_wd):
        self.ffn_pair = _pair_start([d.reshape(4, 2, FS, D) for d in (d_wg, d_wu, d_wd)], "reduce_pair_start_ffn")
        return self.ffn_pair[-1]

    def ffn_reduce(self, dv2, norm2_w):
        send, recv, grads, lands, _ = self.ffn_pair
        grads, recv1 = _pair_wait(send, recv, grads, lands, dv2, "reduce_pair_wait_ffn")
        self.ffn, token = self._reduce("ffn", ["w_ffn_gate", "w_ffn_up", "w_ffn_down"], grads, recv1, [176, 176, 176])
        return _tie(norm2_w, token)

    def in_grads(self, d_wmain, d_wlr, w_lr):
        mine, recv = _disassemble_exchange(d_wmain, d_wlr)
        self.in_names, self.in_rows = ["w_in", "w_out"], [808, 256]
        *self.in_hop, token = _hop_start([_add_blocks(mine, recv, "pair_add_w_in", 808)], "reduce_hop_start_in")
        return _tie(w_lr, token)

    def in_reduce(self, d_wo):
        d_wo4 = d_wo.reshape(4, 2, D // NDEV, D)
        send, recv, (d_wo4,), lands, _ = _pair_start([d_wo4], "reduce_pair_start_out")
        (d_wo4,), (wo_recv,) = _pair_wait(send, recv, [d_wo4], lands, self.update(self.ffn, d_wo), "reduce_pair_wait_out")
        wo_psum = _pair_add(d_wo4, wo_recv, self.c_idx, "pair_add_w_out", 256)
        (psum,), (land,) = _hop_wait(*self.in_hop, wo_psum, "reduce_hop_wait_in")
        psum = _hop_add(psum, land, _hop_pos()[2].astype(jnp.int32).reshape(1), "hop_add_w_in", 808)
        *flight, token = _chip_start([psum, wo_psum], "reduce_chips_start_in", nrel=[2, 3])
        self.inw = dict(tag="in", names=self.in_names, rows=self.in_rows, flight=flight)
        return token


def _local_step(xs, tgt, u, norm1_w, gla_gate_b, attn_sinks, gla_norm_w, norm2_w, fnw, w_main, w_lr, w2p, comm):
    proj =_mm(u, w_main, tb=True, tm=1024, tn=1280, tk=D, name="in_proj")
    plr = _mm(u, w_lr, tb=True, tm=1024, tn=LANE, tk=D, name="in_proj_lr")
    attn_o = _attn_fwd(proj, attn_sinks)
    gla_o, states = _gla_fwd(proj, plr, w2p, gla_gate_b)
    merged = _merge_fwd(attn_o, gla_o, proj, comm.mixed(gla_o, gla_norm_w))
    wo, norm2_w = comm.w_out(merged, norm2_w)
    h1 = _mm(merged, wo, tm=1024, tn=512, tk=D, res=xs, name="out_proj")
    v2 = _rmsnorm_fwd(h1, norm2_w, "norm2_fwd")
    wg_all, wu_all = comm.w_up(v2)
    fa, fb, ff = _ffn_up(v2, wg_all, wu_all)
    wd_all = comm.w_down(ff)
    h2 = _mm(ff, wd_all, tm=1024, tn=1024, tk=FH // 2, res=h1, name="ffn_down")
    dh2, dh2b, d_fnw, loss_part = _loss_head(h2, fnw, tgt)

    da, db = _ffn_dact(dh2b, wd_all, fa, fb)
    Tn = xs.shape[0]
    d_wd = _mm(ff, dh2b, ta=True, tm=512, tn=D, tk=Tn, out_dtype=BF16, name="ffn_dwd")
    d_wg = _mm(da, v2, ta=True, tm=512, tn=D, tk=Tn, out_dtype=BF16, name="ffn_dwg")
    d_wu = _mm(db, v2, ta=True, tm=512, tn=D, tk=Tn, out_dtype=BF16, name="ffn_dwu")
    dv2 = _mm(da, wg_all, tm=1024, tn=1024, tk=FH // 2, after=comm.ffn_grads(d_wg, d_wu, d_wd), name="ffn_dv2_gate")
    dv2 = _mm(db, wu_all, tm=1024, tn=1024, tk=FH // 2, res=dv2, name="ffn_dv2_up")
    norm2_w = comm.ffn_reduce(dv2, norm2_w)
    dh1, dh1b, d_n2 = _rmsnorm_bwd(dv2, h1, norm2_w, dh2, "norm2_bwd")
    dmerged = _mm(dh1b, wo, tb=True, tm=1024, tn=512, tk=D, name="out_proj_dx")
    d_attn, d_gla, d_gates, d_gnw = _merge_bwd(dmerged, attn_o, gla_o, proj, gla_norm_w)
    d_q, d_kv, d_sinks = _attn_bwd(proj, attn_sinks, attn_o, d_attn)
    d_gqk, d_gv, d_plr, d_w2p, d_gb = _gla_bwd(proj, plr, w2p, gla_gate_b, states, d_gla)
    dproj = jnp.concatenate([d_q, d_kv, d_gqk, d_gv, d_gates], axis=1)
    d_wmain = _mm(dproj, u, ta=True, tm=640, tn=D, tk=xs.shape[0], out_dtype=BF16, name="in_proj_dw")
    d_wlr = _mm(d_plr, u, ta=True, tm=LANE, tn=1024, tk=xs.shape[0], out_dtype=BF16, name="in_proj_lr_dw")
    du_lr = _mm(d_plr, comm.in_grads(d_wmain, d_wlr, w_lr), tm=1024, tn=1024, tk=LANE, name="in_proj_lr_dx")
    d_wo = _mm(merged, dh1b, ta=True, tm=1024, tn=512, tk=xs.shape[0], out_dtype=BF16, after=du_lr, name="out_proj_dw")
    du = _mm(dproj, w_main, tm=1024, tn=1024, tk=2560, res=du_lr, after=comm.in_reduce(d_wo), name="in_proj_dx")
    dx, _, d_n1 = _rmsnorm_bwd(du, xs, norm1_w, dh1, "norm1_bwd")
    return dx, loss_part, d_w2p, d_gb, d_sinks, d_gnw, d_n1, d_n2, d_fnw


def kernel(x, norm1_w, w_in, gla_gate_w2, gla_gate_b, attn_sinks, gla_norm_w, w_out, norm2_w, w_ffn_gate, w_ffn_up, w_ffn_down, final_norm_w, loss_target, m_norm1_w, m_w_in, m_gla_gate_w2, m_gla_gate_b, m_attn_sinks, m_gla_norm_w, m_w_out, m_norm2_w, m_w_ffn_gate, m_w_ffn_up, m_w_ffn_down, m_final_norm_w, v_norm1_w, v_w_in, v_gla_gate_w2, v_gla_gate_b, v_attn_sinks, v_gla_norm_w, v_w_out, v_norm2_w, v_w_ffn_gate, v_w_ffn_up, v_w_ffn_down, v_final_norm_w):
    xs, tgt = x[0], loss_target[0]
    fnw = final_norm_w.reshape(1, D)
    c_idx = lax.axis_index("c").astype(jnp.int32).reshape(1)
    dev = 4 * lax.axis_index("x") + 2 * lax.axis_index("y") + lax.axis_index("c")

    chip_idx = (2 * lax.axis_index("x") + lax.axis_index("y")).astype(jnp.int32).reshape(1)

    shift = (WS - WSTEP) * dev
    window = lax.dynamic_update_slice(jnp.zeros((WWIN, D), BF16), jnp.transpose(w_in[0]).astype(BF16), (shift, 0))
    w2_land = lax.dynamic_update_slice(lax.empty((NDEV, RANK, LANE), F32), gla_gate_w2, (dev, 0, 0))
    *sems1, win_srcs, win_lands, tok = _win_start([window, gla_gate_w2[0]], [lax.empty((NDEV, WWIN, D), BF16), w2_land], "gather_in_start")
    tr2 = lambda t: jnp.transpose(t[0])
    rows3 = lambda t: jnp.transpose(t[0] + tok[0, 0])
    rest = [(w + tok[0, 0]).astype(BF16) for w in (w_out[0], tr2(w_ffn_gate), tr2(w_ffn_up), w_ffn_down[0])]
    rest_lands = [lax.dynamic_update_slice(lax.empty((NDEV,) + s.shape, s.dtype), s[None], (dev, 0, 0)) for s in rest]
    win3 = [rows3(t) for t in (w_in, m_w_in, v_w_in)]
    *sems2, win_lands, tok = _win_hand_on(win_lands, sems1[1], rest + rest_lands + win3, "gather_in_hand_on")
    u = _rmsnorm_fwd(xs, _tie(norm1_w, tok), "norm1_fwd")
    *sems3, win_lands, tok = _win_last(win_lands, sems2[1], [u], "gather_in_last")
    comm = _Comm(rest, rest_lands, tok, c_idx)
    win_all, w2_all = _win_wait(win_srcs, win_lands, sems1, sems2, sems3, comm.token, "gather_in_wait")
    w_main, w_lr = _assemble_w_in(win_all, window)
    w2p = jnp.pad(jnp.transpose(w2_all, (1, 0, 2)).reshape(RANK, GH * DK), ((0, LANE - RANK), (0, 0)))

    big = {}

    def update(grp, after):
        psums, parts = _chip_wait(*grp["flight"], after, "reduce_chips_wait_" + grp["tag"])
        for nm, ps, pt, tr in zip(grp["names"], psums, parts, grp["rows"]):
            w, m, v = {"w_in": (w_in, m_w_in, v_w_in), "w_out": (w_out, m_w_out, v_w_out), "w_ffn_gate": (w_ffn_gate, m_w_ffn_gate, v_w_ffn_gate),
                       "w_ffn_up": (w_ffn_up, m_w_ffn_up, v_w_ffn_up), "w_ffn_down": (w_ffn_down, m_w_ffn_down, v_w_ffn_down)}[nm]
            if nm == "w_in":
                g_win = _sum_parts(ps, pt, chip_idx, "sum_w_in", tr, 1024)
                g3 = lax.dynamic_slice(g_win, (shift, 0), (WS, D))
                out3 = (g3,) + tuple(_adamw_given(*win3, g3, "adamw_w_in", 536, 512))
                big[nm] = [jnp.transpose(t)[None] for t in out3]
            elif nm in ("w_ffn_gate", "w_ffn_up"):
                big[nm] = [jnp.transpose(t)[None] for t in _adamw(tr2(w), tr2(m), tr2(v), ps, pt, chip_idx, "adamw_" + nm, tr)]
            else:
                big[nm] = [t[None] for t in _adamw(w[0], m[0], v[0], ps, pt, chip_idx, "adamw_" + nm, tr)]
            after = big[nm][0]
        return after

    comm.update = update
    dx, loss_part, d_w2p, d_gb, d_sinks, d_gnw, d_n1, d_n2, d_fnw = _local_step(
        xs, tgt, u, norm1_w, gla_gate_b, attn_sinks, gla_norm_w, norm2_w, fnw, w_main, w_lr, w2p, comm)

    pack = jnp.concatenate([_pack_small(d_n1, d_gb, d_sinks, d_gnw, d_n2, d_fnw, loss_part),
                            d_w2p[:RANK].reshape(GW2_ROWS, LANE)], axis=0)
    small = _sum_devices(_gather_small(pack))

    update(comm.inw, dx)
    g_small = small[:SMALL_ROWS]
    sm = _adamw_plain(_pack_small(norm1_w, gla_gate_b, attn_sinks, gla_norm_w, norm2_w, final_norm_w),
                      _pack_small(m_norm1_w, m_gla_gate_b, m_attn_sinks, m_gla_norm_w, m_norm2_w, m_final_norm_w),
                      _pack_small(v_norm1_w, v_gla_gate_b, v_attn_sinks, v_gla_norm_w, v_norm2_w, v_final_norm_w), g_small, "adamw_small")
    g_w2 = lax.dynamic_slice_in_dim(small[SMALL_ROWS:].reshape(RANK, GH * DK), dev * LANE, LANE, axis=1)
    w2 = [g_w2[None]] + [t[None] for t in _adamw_plain(gla_gate_w2[0], m_gla_gate_w2[0], v_gla_gate_w2[0], g_w2, "adamw_w2")]
    loss = g_small.reshape(-1)[S_LOSS]

    sg, sd, sm2, sv2 = [_unpack_small(t) for t in (g_small,) + tuple(sm)]

    def group(i, s):
        return (s[0], big["w_in"][i], w2[i], s[1], s[2], s[3], big["w_out"][i], s[4], big["w_ffn_gate"][i], big["w_ffn_up"][i],
                big["w_ffn_down"][i], s[5])

    return (loss, dx[None], *group(0, sg), *group(1, sd), *group(2, sm2), *group(3, sv2))
```

```python
import jax
import jax.numpy as jnp
from jax import lax
from jax.experimental import pallas as pl
from jax.experimental.pallas import tpu as pltpu

F32, BF16 = jnp.float32, jnp.bfloat16
HIGHEST = lax.Precision.HIGHEST

D = 2048
HD, NQ, NKV, GRP, WIN = 64, 32, 4, 8, 128
GH, DK, DV, RANK, GC = 4, 256, 512, 16, 64
FH, NDEV = 5632, 8
FS = FH // NDEV
DIN = 12816
WS = DIN // NDEV
EPS = 1e-6
MASKV = -1e30
LANE = 128

C_AQ, C_AK, C_AV, C_GQ, C_GK, C_GV, C_GR, C_GA, C_GB, NMAIN = 0, 2048, 2304, 2560, 3584, 4608, 6656, 8704, 10752, 12800
C_LR = 6656
WSTEP, WWIN = 1600, 1616

LR, B1, B2, AEPS, WD, STEP = 0.001, 0.9, 0.999, 1e-08, 0.01, 10

S_N1, S_GB, S_SK, S_GN, S_N2, S_FN, S_LOSS, SMALL_N = 0, 2048, 3072, 3104, 3616, 5664, 7712, 8192
SMALL_ROWS = SMALL_N // LANE
GW2_ROWS = RANK * GH * DK // LANE

MESH = pl.DeviceIdType.MESH


def _dot(a, b, ta=False, tb=False, prec=None):
    dn = (((0,) if ta else (1,), (1,) if tb else (0,)), ((), ()))
    return lax.dot_general(a, b, dn, preferred_element_type=F32, precision=prec)


def _sigmoid(x):
    return 1.0 / (1.0 + jnp.exp(-x))


VMEM_LIMIT = 56 * 1024 * 1024


def _cp(*sem):
    return pltpu.CompilerParams(dimension_semantics=sem, vmem_limit_bytes=VMEM_LIMIT)


def _mm(a, b, *, ta=False, tb=False, tm, tn, tk, out_dtype=F32, res=None, after=None, name):
    M, K = (a.shape[1], a.shape[0]) if ta else a.shape
    N = b.shape[0] if tb else b.shape[1]
    tm, tn, tk = min(tm, M), min(tn, N), min(tk, K)
    nk = K // tk
    assert M % tm == 0 and N % tn == 0 and K % tk == 0
    a_spec = pl.BlockSpec((tk, tm), lambda i, j, k: (k, i)) if ta else pl.BlockSpec((tm, tk), lambda i, j, k: (i, k))
    b_spec = pl.BlockSpec((tn, tk), lambda i, j, k: (j, k)) if tb else pl.BlockSpec((tk, tn), lambda i, j, k: (k, j))
    o_spec = pl.BlockSpec((tm, tn), lambda i, j, k: (i, j))
    has_res = res is not None

    def body(*refs):
        a_ref, b_ref = refs[0], refs[1]
        r_ref = refs[2] if has_res else None
        o_ref = refs[2 + has_res + (after is not None)]
        p = _dot(a_ref[...].astype(BF16), b_ref[...].astype(BF16), ta, tb)
        if nk == 1:
            if has_res:
                p = p + r_ref[...]
            o_ref[...] = p.astype(out_dtype)
        else:
            acc = refs[-1]
            k = pl.program_id(2)

            @pl.when(k == 0)
            def _():
                acc[...] = (p + r_ref[...]) if has_res else p

            @pl.when(k > 0)
            def _():
                acc[...] += p

            @pl.when(k == nk - 1)
            def _():
                o_ref[...] = acc[...].astype(out_dtype)

    return pl.pallas_call(
        body, name=name,
        out_shape=jax.ShapeDtypeStruct((M, N), out_dtype),
        grid=(M // tm, N // tn, nk),
        in_specs=[a_spec, b_spec] + ([o_spec] if has_res else []) + ([pl.BlockSpec(memory_space=pl.ANY)] if after is not None else []),
        out_specs=o_spec,
        scratch_shapes=[pltpu.VMEM((tm, tn), F32)] if nk > 1 else [],
        compiler_params=_cp("parallel", "parallel", "arbitrary"),
    )(*((a, b) + ((res,) if has_res else ()) + ((after,) if after is not None else ())))


def _rmsnorm_fwd(x, w, name, tm=256):
    Tn = x.shape[0]

    def body(x_ref, w_ref, o_ref):
        xv = x_ref[...]
        r = lax.rsqrt(jnp.mean(xv * xv, axis=1, keepdims=True) + EPS)
        o_ref[...] = (xv * r * w_ref[...]).astype(BF16)

    return pl.pallas_call(
        body, name=name, out_shape=jax.ShapeDtypeStruct((Tn, D), BF16), grid=(Tn // tm,),
        in_specs=[pl.BlockSpec((tm, D), lambda i: (i, 0)), pl.BlockSpec((1, D), lambda i: (0, 0))],
        out_specs=pl.BlockSpec((tm, D), lambda i: (i, 0)), compiler_params=_cp("parallel"),
    )(x, w)


def _rmsnorm_bwd(dy, h, w, res, name, tm=256):
    Tn = h.shape[0]

    def body(dy_ref, h_ref, w_ref, res_ref, dh_ref, dhb_ref, dw_ref):
        hv, dyv = h_ref[...], dy_ref[...]
        r = lax.rsqrt(jnp.mean(hv * hv, axis=1, keepdims=True) + EPS)
        g = dyv * w_ref[...]
        dh = res_ref[...] + r * g - hv * (r * r * r * jnp.mean(g * hv, axis=1, keepdims=True))
        dh_ref[...] = dh
        dhb_ref[...] = dh.astype(BF16)
        part = jnp.sum(dyv * hv * r, axis=0, keepdims=True)

        @pl.when(pl.program_id(0) == 0)
        def _():
            dw_ref[...] = part

        @pl.when(pl.program_id(0) > 0)
        def _():
            dw_ref[...] += part

    row = pl.BlockSpec((tm, D), lambda i: (i, 0))
    vec = pl.BlockSpec((1, D), lambda i: (0, 0))
    return pl.pallas_call(
        body, name=name,
        out_shape=(jax.ShapeDtypeStruct((Tn, D), F32), jax.ShapeDtypeStruct((Tn, D), BF16), jax.ShapeDtypeStruct((1, D), F32)),
        grid=(Tn // tm,), in_specs=[row, row, vec, row], out_specs=(row, row, vec), compiler_params=_cp("arbitrary"),
    )(dy, h, w, res)


def _loss_head(h2, wf, tgt, name="loss_head", tm=256):
    Tn = h2.shape[0]

    def body(h_ref, w_ref, t_ref, dh_ref, dhb_ref, dw_ref, loss_ref):
        hv, wv = h_ref[...], w_ref[...]
        r = lax.rsqrt(jnp.mean(hv * hv, axis=1, keepdims=True) + EPS)
        hn = hv * r
        e = hn * wv - t_ref[...]
        dy = e * (1.0 / D)
        g = dy * wv
        dh = r * g - hv * (r * r * r * jnp.mean(g * hv, axis=1, keepdims=True))
        dh_ref[...] = dh
        dhb_ref[...] = dh.astype(BF16)
        part = jnp.sum(dy * hn, axis=0, keepdims=True)
        lpart = (0.5 / D) * jnp.sum(jnp.sum(e * e, axis=1, keepdims=True), axis=0, keepdims=True)

        @pl.when(pl.program_id(0) == 0)
        def _():
            dw_ref[...] = part
            loss_ref[...] = lpart

        @pl.when(pl.program_id(0) > 0)
        def _():
            dw_ref[...] += part
            loss_ref[...] += lpart

    row = pl.BlockSpec((tm, D), lambda i: (i, 0))
    vec = pl.BlockSpec((1, D), lambda i: (0, 0))
    one = pl.BlockSpec((1, 1), lambda i: (0, 0))
    return pl.pallas_call(
        body, name=name,
        out_shape=(jax.ShapeDtypeStruct((Tn, D), F32), jax.ShapeDtypeStruct((Tn, D), BF16), jax.ShapeDtypeStruct((1, D), F32),
                   jax.ShapeDtypeStruct((1, 1), F32)),
        grid=(Tn // tm,), in_specs=[row, vec, row], out_specs=(row, row, vec, one), compiler_params=_cp("arbitrary"),
    )(h2, wf, tgt)


def _attn_mask(n):
    qi = lax.broadcasted_iota(jnp.int32, (NKV, GRP * WIN, 2 * WIN), 1) % WIN
    ki = lax.broadcasted_iota(jnp.int32, (NKV, GRP * WIN, 2 * WIN), 2)
    rel = qi + WIN - ki
    return (rel >= 0) & (rel < WIN) & ((n > 0) | (ki >= WIN))


def _kv_heads(prev_ref, cur_ref):
    return jnp.stack([jnp.concatenate([prev_ref[:, h * HD:(h + 1) * HD], cur_ref[:, h * HD:(h + 1) * HD]], axis=0) for h in range(NKV)])


def _q_heads(ref):
    return jnp.stack([jnp.concatenate([ref[:, (h * GRP + g) * HD:(h * GRP + g + 1) * HD] for g in range(GRP)], axis=0) for h in range(NKV)])


def _attn_probs(q_ref, kc_ref, kp_ref, sink_ref, mask):
    kk = _kv_heads(kp_ref, kc_ref).astype(BF16)
    qs = _q_heads(q_ref).astype(BF16)
    s = jnp.einsum('hqd,hkd->hqk', qs, kk, preferred_element_type=F32) * (HD ** -0.5)
    s = jnp.where(mask, s, MASKV)
    sink = jnp.stack([jnp.concatenate([jnp.full((WIN, 1), sink_ref[0, h * GRP + g], F32) for g in range(GRP)], axis=0) for h in range(NKV)])
    m = jnp.maximum(jnp.max(s, axis=2, keepdims=True), sink)
    e = jnp.exp(s - m)
    es = jnp.exp(sink - m)
    inv = 1.0 / (jnp.sum(e, axis=2, keepdims=True) + es)
    return e * inv, es * inv, qs, kk


def _attn_specs(nb, last):
    cur = lambda n: jnp.minimum(n, last)
    prev = lambda n: jnp.maximum(jnp.minimum(n, last) - 1, 0)
    return [
        pl.BlockSpec((WIN, NQ * HD), lambda n: (cur(n), C_AQ // (NQ * HD))),
        pl.BlockSpec((WIN, NKV * HD), lambda n: (cur(n), C_AK // (NKV * HD))),
        pl.BlockSpec((WIN, NKV * HD), lambda n: (prev(n), C_AK // (NKV * HD))),
        pl.BlockSpec((WIN, NKV * HD), lambda n: (cur(n), C_AV // (NKV * HD))),
        pl.BlockSpec((WIN, NKV * HD), lambda n: (prev(n), C_AV // (NKV * HD))),
    ]


def _attn_fwd(proj, sinks, name="attn_fwd"):
    Tn = proj.shape[0]
    nb = Tn // WIN

    def body(q_ref, kc_ref, kp_ref, vc_ref, vp_ref, sink_ref, o_ref):
        p, _, _, _ = _attn_probs(q_ref, kc_ref, kp_ref, sink_ref, _attn_mask(pl.program_id(0)))
        o = jnp.einsum('hqk,hkd->hqd', p.astype(BF16), _kv_heads(vp_ref, vc_ref).astype(BF16), preferred_element_type=F32)
        for h in range(NKV):
            for g in range(GRP):
                o_ref[:, (h * GRP + g) * HD:(h * GRP + g + 1) * HD] = o[h, g * WIN:(g + 1) * WIN, :]

    return pl.pallas_call(
        body, name=name, out_shape=jax.ShapeDtypeStruct((Tn, D), F32), grid=(nb,),
        in_specs=_attn_specs(nb, nb - 1) + [pl.BlockSpec(memory_space=pltpu.SMEM)],
        out_specs=pl.BlockSpec((WIN, D), lambda n: (n, 0)), compiler_params=_cp("parallel"),
    )(proj, proj, proj, proj, proj, sinks)


def _attn_bwd(proj, sinks, o, do, name="attn_bwd"):
    Tn = proj.shape[0]
    nb = Tn // WIN
    KW = NKV * HD

    def body(q_ref, kc_ref, kp_ref, vc_ref, vp_ref, o_ref, do_ref, sink_ref, dq_ref, dkv_ref, dsk_ref, carry, cur):
        n = pl.program_id(0)

        @pl.when(n == 0)
        def _():
            carry[...] = jnp.zeros_like(carry)
            dsk_ref[...] = jnp.zeros_like(dsk_ref)

        @pl.when(n < nb)
        def _():
            p, ps, qs, kk = _attn_probs(q_ref, kc_ref, kp_ref, sink_ref, _attn_mask(n))
            vv = _kv_heads(vp_ref, vc_ref).astype(BF16)
            dos = _q_heads(do_ref)
            delta = jnp.sum(dos * _q_heads(o_ref), axis=2, keepdims=True)
            dosb = dos.astype(BF16)
            dp = jnp.einsum('hqd,hkd->hqk', dosb, vv, preferred_element_type=F32)
            ds = (p * (dp - delta) * (HD ** -0.5)).astype(BF16)
            dq = jnp.einsum('hqk,hkd->hqd', ds, kk, preferred_element_type=F32)
            dkk = jnp.einsum('hqk,hqd->hkd', ds, qs, preferred_element_type=F32)
            dvv = jnp.einsum('hqk,hqd->hkd', p.astype(BF16), dosb, preferred_element_type=F32)
            dsk = ps * delta
            for h in range(NKV):
                for g in range(GRP):
                    i = h * GRP + g
                    dq_ref[:, i * HD:(i + 1) * HD] = dq[h, g * WIN:(g + 1) * WIN, :].astype(BF16)
                    dsk_ref[:, i:i + 1] -= jnp.sum(dsk[h, g * WIN:(g + 1) * WIN, :], axis=0, keepdims=True)
                dkv_ref[:, h * HD:(h + 1) * HD] = (carry[:, h * HD:(h + 1) * HD] + dkk[h, :WIN, :]).astype(BF16)
                dkv_ref[:, KW + h * HD:KW + (h + 1) * HD] = (carry[:, KW + h * HD:KW + (h + 1) * HD] + dvv[h, :WIN, :]).astype(BF16)
                cur[:, h * HD:(h + 1) * HD] = dkk[h, WIN:, :]
                cur[:, KW + h * HD:KW + (h + 1) * HD] = dvv[h, WIN:, :]
            carry[...] = cur[...]

        @pl.when(n == nb)
        def _():
            dkv_ref[...] = carry[...].astype(BF16)

    last = nb - 1
    row = pl.BlockSpec((WIN, D), lambda n: (jnp.minimum(n, last), 0))
    return pl.pallas_call(
        body, name=name,
        out_shape=(jax.ShapeDtypeStruct((Tn, D), BF16), jax.ShapeDtypeStruct((Tn, 2 * KW), BF16), jax.ShapeDtypeStruct((1, NQ), F32)),
        grid=(nb + 1,),
        in_specs=_attn_specs(nb, last) + [row, row, pl.BlockSpec(memory_space=pltpu.SMEM)],
        out_specs=(row, pl.BlockSpec((WIN, 2 * KW), lambda n: (jnp.maximum(n - 1, 0), 0)), pl.BlockSpec((1, NQ), lambda n: (0, 0))),
        scratch_shapes=[pltpu.VMEM((WIN, 2 * KW), F32), pltpu.VMEM((WIN, 2 * KW), F32)],
        compiler_params=_cp("arbitrary"),
    )(proj, proj, proj, proj, proj, o, do, sinks)


def _tri(lower):
    r = lax.broadcasted_iota(jnp.int32, (GC, GC), 0)
    c = lax.broadcasted_iota(jnp.int32, (GC, GC), 1)
    return r >= c if lower else r <= c


def _per_head(a):
    return jnp.stack([a[:, h * DK:(h + 1) * DK] for h in range(GH)])


def _all_heads(a):
    return jnp.concatenate([a[h] for h in range(GH)], axis=1)


def _gla_gates(lr, w2_ref, gb_ref):
    logit = _dot(lr, w2_ref[...].astype(BF16)) + gb_ref[...]
    la = (jnp.minimum(logit, 0.0) - jnp.log(1.0 + jnp.exp(-jnp.abs(logit)))) * (1.0 / 16.0)
    g = _dot(_tri(True).astype(F32), la, prec=HIGHEST)
    return logit, g


def _bmm(spec, a, b):
    return jnp.einsum(spec, a, b, preferred_element_type=F32)


def _gla_specs(nc, rev):
    idx = (lambda n: nc - 1 - n) if rev else (lambda n: n)
    half = 2 * DK
    return (
        [pl.BlockSpec((GC, half), lambda n, j=j: (idx(n), C_GQ // half + j)) for j in range(2)]
        + [pl.BlockSpec((GC, half), lambda n, j=j: (idx(n), C_GK // half + j)) for j in range(2)]
        + [pl.BlockSpec((GC, DV), lambda n, h=h: (idx(n), C_GV // DV + h)) for h in range(GH)]
        + [pl.BlockSpec((GC, LANE), lambda n: (idx(n), 0)), pl.BlockSpec((LANE, GH * DK), lambda n: (0, 0)),
           pl.BlockSpec((1, GH * DK), lambda n: (0, 0))])


def _gla_heads(refs):
    return (lambda h: refs[h // 2][:, (h % 2) * DK:(h % 2 + 1) * DK], lambda h: refs[2 + h // 2][:, (h % 2) * DK:(h % 2 + 1) * DK],
            lambda h: refs[4 + h][...])


def _gla_fwd(proj, plr, w2p, gb, name="gla_fwd"):
    Tn = proj.shape[0]
    nc = Tn // GC

    def body(*refs):
        qh, kh, vh = _gla_heads(refs)
        lr_ref, w2_ref, gb_ref, o_ref, st_ref, S = refs[8:]

        @pl.when(pl.program_id(0) == 0)
        def _():
            S[...] = jnp.zeros_like(S)

        heads = lambda f: jnp.stack([f(h) for h in range(GH)])
        _, g_all = _gla_gates(lr_ref[...].astype(BF16), w2_ref, gb_ref)
        g = _per_head(g_all)
        gl = g[:, GC - 1:GC, :]
        k = heads(kh)
        v = heads(vh).astype(BF16)
        qd = (heads(qh) * (DK ** -0.5) * jnp.exp(g)).astype(BF16)
        ki = (k * jnp.exp(-g)).astype(BF16)
        ke = (k * jnp.exp(gl - g)).astype(BF16)
        att = jnp.where(_tri(True)[None], _bmm('hid,hjd->hij', qd, ki), 0.0).astype(BF16)
        sp = S[...]
        st_ref[0] = sp
        o = _bmm('hij,hjv->hiv', att, v) + _bmm('hid,hvd->hiv', qd, sp.astype(BF16))
        for h in range(GH):
            o_ref[:, h * DV:(h + 1) * DV] = o[h]
        S[...] = sp * jnp.exp(gl) + _bmm('hjv,hjd->hvd', v, ke)

    return pl.pallas_call(
        body, name=name,
        out_shape=(jax.ShapeDtypeStruct((Tn, GH * DV), F32), jax.ShapeDtypeStruct((nc, GH, DV, DK), F32)),
        grid=(nc,), in_specs=_gla_specs(nc, False),
        out_specs=(pl.BlockSpec((GC, GH * DV), lambda n: (n, 0)), pl.BlockSpec((1, GH, DV, DK), lambda n: (n, 0, 0, 0))),
        scratch_shapes=[pltpu.VMEM((GH, DV, DK), F32)], compiler_params=_cp("arbitrary"),
    )(*([proj] * 8), plr, w2p, gb)


def _gla_bwd(proj, plr, w2p, gb, states, do, name="gla_bwd"):
    Tn = proj.shape[0]
    nc = Tn // GC

    def body(*refs):
        qh, kh, vh = _gla_heads(refs)
        lr_ref, w2_ref, gb_ref, st_ref, do_ref, dqk_ref, dv_ref, dlr_ref, dw2_ref, dgb_ref, dS = refs[8:]

        @pl.when(pl.program_id(0) == 0)
        def _():
            dS[...] = jnp.zeros_like(dS)
            dw2_ref[...] = jnp.zeros_like(dw2_ref)
            dgb_ref[...] = jnp.zeros_like(dgb_ref)

        heads = lambda f: jnp.stack([f(h) for h in range(GH)])
        lr = lr_ref[...].astype(BF16)
        causal = _tri(True)[None]
        last_row = lax.broadcasted_iota(jnp.int32, (GH, GC, DK), 1) == GC - 1
        logit, g_all = _gla_gates(lr, w2_ref, gb_ref)
        g = _per_head(g_all)
        gl = g[:, GC - 1:GC, :]
        egl = jnp.exp(gl)
        eg, eng, ege = jnp.exp(g), jnp.exp(-g), jnp.exp(gl - g)
        k = heads(kh)
        v = heads(vh).astype(BF16)
        dob = heads(lambda h: do_ref[:, h * DV:(h + 1) * DV]).astype(BF16)
        qd = heads(qh) * (DK ** -0.5) * eg
        ki = k * eng
        ke = k * ege
        qdb, kib, keb = qd.astype(BF16), ki.astype(BF16), ke.astype(BF16)
        att = jnp.where(causal, _bmm('hid,hjd->hij', qdb, kib), 0.0).astype(BF16)
        datt = jnp.where(causal, _bmm('hiv,hjv->hij', dob, v), 0.0).astype(BF16)
        sp = st_ref[0]
        dsn = dS[...]
        dsnb = dsn.astype(BF16)
        dv = (_bmm('hij,hiv->hjv', att, dob) + _bmm('hjd,hvd->hjv', keb, dsnb)).astype(BF16)
        dqd = _bmm('hij,hjd->hid', datt, kib) + _bmm('hiv,hvd->hid', dob, sp.astype(BF16))
        dki = _bmm('hij,hid->hjd', datt, qdb)
        dke = _bmm('hjv,hvd->hjd', v, dsnb)
        ddec = jnp.sum(dsn * sp, axis=1, keepdims=True)
        dS[...] = dsn * egl + _bmm('hiv,hid->hvd', dob, qdb)
        dke_ke = dke * ke
        dgl = jnp.sum(dke_ke, axis=1, keepdims=True) + ddec * egl
        dg = dqd * qd - dki * ki - dke_ke + jnp.where(last_row, dgl, 0.0)
        dq = (dqd * ((DK ** -0.5) * eg)).astype(BF16)
        dk = (dki * eng + dke * ege).astype(BF16)
        for h in range(GH):
            dv_ref[:, h * DV:(h + 1) * DV] = dv[h]
            dqk_ref[:, h * DK:(h + 1) * DK] = dq[h]
            dqk_ref[:, GH * DK + h * DK:GH * DK + (h + 1) * DK] = dk[h]
        dla = _dot(_tri(False).astype(F32), _all_heads(dg), prec=HIGHEST)
        dlogit = dla * (1.0 / 16.0) * _sigmoid(-logit)
        dlb = dlogit.astype(BF16)
        dlr_ref[...] = _dot(dlb, w2_ref[...].astype(BF16), tb=True).astype(BF16)
        dw2_ref[...] += _dot(lr, dlb, ta=True)
        dgb_ref[...] += jnp.sum(dlogit, axis=0, keepdims=True)

    rev = lambda n: nc - 1 - n
    row = pl.BlockSpec((GC, GH * DV), lambda n: (rev(n), 0))
    return pl.pallas_call(
        body, name=name,
        out_shape=(jax.ShapeDtypeStruct((Tn, 2 * GH * DK), BF16), jax.ShapeDtypeStruct((Tn, GH * DV), BF16),
                   jax.ShapeDtypeStruct((Tn, LANE), BF16), jax.ShapeDtypeStruct((LANE, GH * DK), F32),
                   jax.ShapeDtypeStruct((1, GH * DK), F32)),
        grid=(nc,),
        in_specs=_gla_specs(nc, True) + [pl.BlockSpec((1, GH, DV, DK), lambda n: (rev(n), 0, 0, 0)), row],
        out_specs=(row, row, pl.BlockSpec((GC, LANE), lambda n: (rev(n), 0)), pl.BlockSpec((LANE, GH * DK), lambda n: (0, 0)),
                   pl.BlockSpec((1, GH * DK), lambda n: (0, 0))),
        scratch_shapes=[pltpu.VMEM((GH, DV, DK), F32)], compiler_params=_cp("arbitrary"),
    )(*([proj] * 8), plr, w2p, gb, states, do)


def _merge_specs(tm):
    row = pl.BlockSpec((tm, D), lambda i: (i, 0))
    gates = [pl.BlockSpec((tm, DV), lambda i, j=c // DV + h: (i, j)) for c in (C_GR, C_GA, C_GB) for h in range(GH)]
    return row, gates, pl.BlockSpec((1, DV), lambda i: (0, 0))


def _merge_fwd(a, go, proj, gnw, name="merge_fwd", tm=256):
    Tn = a.shape[0]

    def body(a_ref, go_ref, *rest):
        gates, w_ref, m_ref = rest[:3 * GH], rest[3 * GH], rest[3 * GH + 1]
        for h in range(GH):
            sl = slice(h * DV, (h + 1) * DV)
            gov = go_ref[:, sl]
            r = lax.rsqrt(jnp.mean(gov * gov, axis=1, keepdims=True) + EPS)
            gr = gates[h][...]
            g2 = gov * r * w_ref[...] * (gr * _sigmoid(gr))
            m_ref[:, sl] = (_sigmoid(gates[GH + h][...]) * a_ref[:, sl] + _sigmoid(gates[2 * GH + h][...]) * g2).astype(BF16)

    row, gates, vec = _merge_specs(tm)
    return pl.pallas_call(
        body, name=name, out_shape=jax.ShapeDtypeStruct((Tn, D), BF16), grid=(Tn // tm,),
        in_specs=[row, row] + gates + [vec], out_specs=row, compiler_params=_cp("parallel"),
    )(a, go, *([proj] * (3 * GH)), gnw)


def _merge_bwd(dm, a, go, proj, gnw, name="merge_bwd", tm=256):
    Tn = a.shape[0]

    def body(dm_ref, a_ref, go_ref, *rest):
        gates = rest[:3 * GH]
        w_ref, da_ref, dgo_ref, dg_ref, dw_ref = rest[3 * GH:]
        wv = w_ref[...]
        dw = jnp.zeros((1, DV), F32)
        for h in range(GH):
            sl = slice(h * DV, (h + 1) * DV)
            dmv, av, gov, gr = dm_ref[:, sl], a_ref[:, sl], go_ref[:, sl], gates[h][...]
            sa, sb, sg = _sigmoid(gates[GH + h][...]), _sigmoid(gates[2 * GH + h][...]), _sigmoid(gr)
            r = lax.rsqrt(jnp.mean(gov * gov, axis=1, keepdims=True) + EPS)
            gn0 = gov * r
            gn = gn0 * wv
            silu = gr * sg
            dg2 = dmv * sb
            da_ref[:, sl] = dmv * sa
            dg_ref[:, D + h * DV:D + (h + 1) * DV] = (dmv * av * sa * (1.0 - sa)).astype(BF16)
            dg_ref[:, 2 * D + h * DV:2 * D + (h + 1) * DV] = (dg2 * gn * silu * (1.0 - sb)).astype(BF16)
            dg_ref[:, sl] = (dg2 * gn * (sg * (1.0 + gr * (1.0 - sg)))).astype(BF16)
            dgn = dg2 * silu
            dw = dw + jnp.sum(dgn * gn0, axis=0, keepdims=True)
            gg = dgn * wv
            dgo_ref[:, sl] = r * gg - gov * (r * r * r * jnp.mean(gg * gov, axis=1, keepdims=True))

        @pl.when(pl.program_id(0) == 0)
        def _():
            dw_ref[...] = dw

        @pl.when(pl.program_id(0) > 0)
        def _():
            dw_ref[...] += dw

    row, gates, vec = _merge_specs(tm)
    return pl.pallas_call(
        body, name=name,
        out_shape=(jax.ShapeDtypeStruct((Tn, D), F32), jax.ShapeDtypeStruct((Tn, D), F32), jax.ShapeDtypeStruct((Tn, 3 * D), BF16),
                   jax.ShapeDtypeStruct((1, DV), F32)),
        grid=(Tn // tm,), in_specs=[row, row, row] + gates + [vec],
        out_specs=(row, row, pl.BlockSpec((tm, 3 * D), lambda i: (i, 0)), vec), compiler_params=_cp("arbitrary"),
    )(dm, a, go, *([proj] * (3 * GH)), gnw)


def _ffn_up(v2, wgt, wut, name="ffn_up", tm=1024, tn=512):
    Tn = v2.shape[0]
    tm = min(tm, Tn)

    def body(v_ref, wg_ref, wu_ref, a_ref, b_ref, ff_ref):
        vv = v_ref[...]
        a = _dot(vv, wg_ref[...], tb=True)
        b = _dot(vv, wu_ref[...], tb=True)
        a_ref[...] = a.astype(BF16)
        b_ref[...] = b.astype(BF16)
        ff_ref[...] = (a * _sigmoid(a) * b).astype(BF16)

    w = pl.BlockSpec((tn, D), lambda j, i: (j, 0))
    act = pl.BlockSpec((tm, tn), lambda j, i: (i, j))
    return pl.pallas_call(
        body, name=name,
        out_shape=(jax.ShapeDtypeStruct((Tn, FH), BF16), jax.ShapeDtypeStruct((Tn, FH), BF16), jax.ShapeDtypeStruct((Tn, FH), BF16)),
        grid=(FH // tn, Tn // tm), in_specs=[pl.BlockSpec((tm, D), lambda j, i: (i, 0)), w, w], out_specs=(act, act, act),
        compiler_params=_cp("parallel", "parallel"),
    )(v2, wgt, wut)


def _ffn_dact(dh2b, wd, a, b, name="ffn_dact", tm=1024, tn=512):
    Tn = dh2b.shape[0]
    tm = min(tm, Tn)

    def body(d_ref, w_ref, a_ref, b_ref, da_ref, db_ref):
        dff = _dot(d_ref[...], w_ref[...], tb=True)
        av = a_ref[...].astype(F32)
        sg = _sigmoid(av)
        da_ref[...] = (dff * b_ref[...].astype(F32) * (sg * (1.0 + av * (1.0 - sg)))).astype(BF16)
        db_ref[...] = (dff * (av * sg)).astype(BF16)

    act = pl.BlockSpec((tm, tn), lambda j, i: (i, j))
    return pl.pallas_call(
        body, name=name,
        out_shape=(jax.ShapeDtypeStruct((Tn, FH), BF16), jax.ShapeDtypeStruct((Tn, FH), BF16)),
        grid=(FH // tn, Tn // tm),
        in_specs=[pl.BlockSpec((tm, D), lambda j, i: (i, 0)), pl.BlockSpec((tn, D), lambda j, i: (j, 0)), act, act],
        out_specs=(act, act), compiler_params=_cp("parallel", "parallel"),
    )(dh2b, wd, a, b)


def _adam_math(w, g, m, v):
    m2 = B1 * m + (1.0 - B1) * g
    v2 = B2 * v + (1.0 - B2) * (g * g)
    mh = m2 / (1.0 - B1 ** STEP)
    vh = v2 / (1.0 - B2 ** STEP)
    return -LR * (mh / (jnp.sqrt(vh) + AEPS) + WD * w), m2, v2


def _sum_blocks(o_ref, p_ref):
    g = o_ref[...].astype(F32)
    for j in range(p_ref.shape[0]):
        g = g + p_ref[j].astype(F32)
    return g


def _adamw(w, m, v, psums, parts, chip_idx, name, tr):
    R, C = w.shape

    def body(s_ref, w_ref, m_ref, v_ref, o_ref, p_ref, g_ref, d_ref, m2_ref, v2_ref):
        g = _sum_blocks(o_ref, p_ref)
        d, m2, v2 = _adam_math(w_ref[...], g, m_ref[...], v_ref[...])
        g_ref[...] = g
        d_ref[...] = d
        m2_ref[...] = m2
        v2_ref[...] = v2

    blk = pl.BlockSpec((tr, C), lambda i, s: (i, 0))
    out = jax.ShapeDtypeStruct((R, C), F32)
    grid_spec = pltpu.PrefetchScalarGridSpec(
        num_scalar_prefetch=1, grid=(R // tr,),
        in_specs=[blk, blk, blk, pl.BlockSpec((None, tr, C), lambda i, s: (s[0], i, 0)),
                  pl.BlockSpec((parts.shape[0], tr, C), lambda i, s: (0, i, 0))],
        out_specs=(blk, blk, blk, blk),
    )
    return pl.pallas_call(body, name=name, out_shape=(out, out, out, out), grid_spec=grid_spec, compiler_params=_cp("parallel"),
                          )(chip_idx, w, m, v, psums, parts)


def _adamw_given(w, m, v, g, name, tr, tc):
    R, C = w.shape

    def body(w_ref, m_ref, v_ref, g_ref, d_ref, m2_ref, v2_ref):
        d, m2, v2 = _adam_math(w_ref[...], g_ref[...], m_ref[...], v_ref[...])
        d_ref[...] = d
        m2_ref[...] = m2
        v2_ref[...] = v2

    blk = pl.BlockSpec((tr, tc), lambda i, j: (i, j))
    out = jax.ShapeDtypeStruct(w.shape, F32)
    return pl.pallas_call(body, name=name, out_shape=(out, out, out), grid=(pl.cdiv(R, tr), C // tc), in_specs=[blk] * 4,
                          out_specs=(blk, blk, blk), compiler_params=_cp("parallel", "parallel"))(w, m, v, g)


def _sum_parts(psums, parts, chip_idx, name, tr, tc):
    _, R, C = psums.shape

    def body(s_ref, o_ref, p_ref, g_ref):
        g_ref[...] = _sum_blocks(o_ref, p_ref)

    grid_spec = pltpu.PrefetchScalarGridSpec(
        num_scalar_prefetch=1, grid=(R // tr, C // tc),
        in_specs=[pl.BlockSpec((None, tr, tc), lambda i, j, s: (s[0], i, j)),
                  pl.BlockSpec((parts.shape[0], tr, tc), lambda i, j, s: (0, i, j))],
        out_specs=pl.BlockSpec((tr, tc), lambda i, j, s: (i, j)),
    )
    return pl.pallas_call(body, name=name, out_shape=jax.ShapeDtypeStruct((R, C), F32), grid_spec=grid_spec,
                          compiler_params=_cp("parallel", "parallel"))(chip_idx, psums, parts)


def _adamw_plain(w, m, v, g, name):
    def body(w_ref, m_ref, v_ref, g_ref, d_ref, m2_ref, v2_ref):
        d, m2, v2 = _adam_math(w_ref[...], g_ref[...], m_ref[...], v_ref[...])
        d_ref[...] = d
        m2_ref[...] = m2
        v2_ref[...] = v2

    out = jax.ShapeDtypeStruct(w.shape, F32)
    return pl.pallas_call(body, name=name, out_shape=(out, out, out))(w, m, v, g)


def _sum_devices(pack_all, name="sum_small"):
    def body(p_ref, o_ref):
        s = p_ref[0]
        for k in range(1, NDEV):
            s = s + p_ref[k]
        o_ref[...] = s

    return pl.pallas_call(body, name=name, out_shape=jax.ShapeDtypeStruct(pack_all.shape[1:], F32))(pack_all)


def _pair_add(g5, recv, c_idx, name, tr):
    _, _, R, C = g5.shape

    def body(c_ref, g_ref, r_ref, o_ref):
        o_ref[...] = (g_ref[...].astype(F32) + r_ref[...].astype(F32)).astype(BF16)

    grid_spec = pltpu.PrefetchScalarGridSpec(
        num_scalar_prefetch=1, grid=(4, R // tr),
        in_specs=[pl.BlockSpec((None, None, tr, C), lambda q, i, c: (q, c[0], i, 0)), pl.BlockSpec((None, tr, C), lambda q, i, c: (q, i, 0))],
        out_specs=pl.BlockSpec((None, tr, C), lambda q, i, c: (q, i, 0)),
    )
    return pl.pallas_call(
        body, name=name, out_shape=jax.ShapeDtypeStruct((4, R, C), BF16), grid_spec=grid_spec,
        compiler_params=_cp("parallel", "parallel"),
    )(c_idx, g5, recv)


_ANY = pl.BlockSpec(memory_space=pl.ANY)


def _mesh_pos():
    x, y, c = lax.axis_index("x"), lax.axis_index("y"), lax.axis_index("c")
    return x, y, c, [(1 - x, y), (x, 1 - y), (1 - x, 1 - y)]


def _gather_small(pack, name="gather_small"):
    def body(pk, pk_all, psend, precv, loc):
        x, y, c, chips = _mesh_pos()
        me_slot = 4 * x + 2 * y + c
        sib = (x, y, 1 - c)
        own = pltpu.make_async_copy(pk, pk_all.at[me_slot], loc)
        own.start()
        peers = [sib] + [(*chip, c) for chip in chips] + [(*chip, 1 - c) for chip in chips]
        small = [pltpu.make_async_remote_copy(src_ref=pk, dst_ref=pk_all.at[me_slot], send_sem=psend.at[k], recv_sem=precv.at[k],
                                              device_id=p, device_id_type=MESH) for k, p in enumerate(peers)]
        for d in small:
            d.start()
        for k, p in enumerate(peers):
            pltpu.make_async_remote_copy(src_ref=pk, dst_ref=pk_all.at[4 * p[0] + 2 * p[1] + p[2]], send_sem=psend.at[k],
                                         recv_sem=precv.at[k], device_id=p, device_id_type=MESH).wait_recv()
        for d in small:
            d.wait_send()
        own.wait()

    return pl.pallas_call(
        body, name=name, out_shape=jax.ShapeDtypeStruct((NDEV,) + pack.shape, pack.dtype), in_specs=[_ANY], out_specs=_ANY,
        scratch_shapes=[pltpu.SemaphoreType.DMA((7,)), pltpu.SemaphoreType.DMA((7,)), pltpu.SemaphoreType.DMA(())],
    )(pack)


def _main_row(g):
    return g if g < C_LR else g - RANK


def _window_pieces(lo, hi):
    out = []
    for a, b, where in ((lo, min(hi, C_LR), "main"), (max(lo, C_LR), min(hi, C_LR + RANK), "lr"), (max(lo, C_LR + RANK), hi, "main")):
        if a < b:
            out.append((a, b, where, _main_row(a) if where == "main" else a - C_LR))
    return out


def _assemble_w_in(windows, own, name="assemble_w_in"):
    edges = NDEV - 1

    def body(b_ref, own_ref, main_ref, lr_ref, buf, ebuf, in_sems, out_sems, esems):
        dev = 4 * lax.axis_index("x") + 2 * lax.axis_index("y") + lax.axis_index("c")

        def load(k):
            return pltpu.make_async_copy(b_ref.at[k], buf.at[k % 2], in_sems.at[k % 2])

        def start_load(k):
            pl.when(dev == k)(pltpu.make_async_copy(own_ref, buf.at[k % 2], in_sems.at[k % 2]).start)
            pl.when(dev != k)(load(k).start)

        lr_ref[RANK:, :] = jnp.zeros((LANE - RANK, D), BF16)
        start_load(0)
        pending, edge_out = [], []
        for k in range(NDEV):
            s = k % 2
            load(k).wait()
            if k:
                ebuf[k - 1] = buf[1 - s, WSTEP:WWIN, :] + buf[s, 0:16, :]
                edge_out.append(pltpu.make_async_copy(ebuf.at[k - 1], main_ref.at[pl.ds(_main_row(WSTEP * k), 16)], esems.at[k - 1]))
                edge_out[-1].start()
                for d in pending:
                    d.wait()
            if k + 1 < NDEV:
                start_load(k + 1)
            pending = []
            lo = WSTEP * k + (16 if k else 0)
            hi = WSTEP * k + (WWIN if k == NDEV - 1 else WSTEP)
            for a, b, where, dst in _window_pieces(lo, hi):
                if where == "lr":
                    lr_ref[dst:dst + b - a, :] = buf[s, a - WSTEP * k:b - WSTEP * k, :]
                else:
                    pending.append(pltpu.make_async_copy(buf.at[s, pl.ds(a - WSTEP * k, b - a)], main_ref.at[pl.ds(dst, b - a)],
                                                         out_sems.at[2 * s + len(pending)]))
                    pending[-1].start()
        for d in pending + edge_out:
            d.wait()

    return pl.pallas_call(
        body, name=name,
        out_shape=(jax.ShapeDtypeStruct((NMAIN, D), BF16), jax.ShapeDtypeStruct((LANE, D), BF16)),
        in_specs=[_ANY, _ANY], out_specs=(_ANY, pl.BlockSpec(memory_space=pltpu.VMEM)),
        scratch_shapes=[pltpu.VMEM((2, WWIN, D), BF16), pltpu.VMEM((edges, 16, D), BF16), pltpu.SemaphoreType.DMA((2,)),
                        pltpu.SemaphoreType.DMA((4,)), pltpu.SemaphoreType.DMA((edges,))],
        compiler_params=pltpu.CompilerParams(vmem_limit_bytes=VMEM_LIMIT),
    )(windows, own)


def _disassemble_exchange(d_main, d_lr, name="disassemble_exchange"):
    def body(main_ref, lr_ref, mine_ref, recv_ref, buf, in_sems, keep_sems, send_sems, recv_sems):
        x, y, c, _ = _mesh_pos()
        sib = (x, y, 1 - c)

        def loads(k):
            s, out = k % 2, []
            for a, b, where, src0 in _window_pieces(WSTEP * k, WSTEP * k + WWIN):
                if where == "main":
                    out.append(pltpu.make_async_copy(main_ref.at[pl.ds(src0, b - a)], buf.at[s, pl.ds(a - WSTEP * k, b - a)],
                                                     in_sems.at[2 * s + len(out)]))
            return out

        def keep(k):
            return pltpu.make_async_copy(buf.at[k % 2], mine_ref.at[k // 2], keep_sems.at[k % 2])

        def send(k):
            return _rcopy(buf.at[k % 2], recv_ref.at[k // 2], send_sems.at[k % 2], recv_sems.at[k // 2], sib)

        def store_start(k):
            pl.when(c == k % 2)(keep(k).start)
            pl.when(c != k % 2)(send(k).start)

        def store_wait(k):
            pl.when(c == k % 2)(keep(k).wait)
            pl.when(c != k % 2)(send(k).wait_send)

        for d in loads(0):
            d.start()
        for k in range(NDEV):
            for d in loads(k):
                d.wait()
            for a, b, where, src0 in _window_pieces(WSTEP * k, WSTEP * k + WWIN):
                if where == "lr":
                    buf[k % 2, a - WSTEP * k:b - WSTEP * k, :] = lr_ref[src0:src0 + b - a, :]
            if k:
                store_wait(k - 1)
            if k + 1 < NDEV:
                for d in loads(k + 1):
                    d.start()
            store_start(k)
        store_wait(NDEV - 1)
        for chip in range(NDEV // 2):
            _rcopy(buf.at[0], recv_ref.at[chip], send_sems.at[0], recv_sems.at[chip], sib).wait_recv()

    half = jax.ShapeDtypeStruct((NDEV // 2, WWIN, D), BF16)
    return pl.pallas_call(
        body, name=name, out_shape=(half, half),
        in_specs=[_ANY, pl.BlockSpec(memory_space=pltpu.VMEM)], out_specs=(_ANY, _ANY),
        scratch_shapes=[pltpu.VMEM((2, WWIN, D), BF16), pltpu.SemaphoreType.DMA((4,)), pltpu.SemaphoreType.DMA((2,)),
                        pltpu.SemaphoreType.DMA((2,)), pltpu.SemaphoreType.DMA((NDEV // 2,))],
        compiler_params=pltpu.CompilerParams(vmem_limit_bytes=VMEM_LIMIT),
    )(d_main, d_lr)


def _add_blocks(a, b, name, tr):
    _, R, C = a.shape

    def body(a_ref, b_ref, o_ref):
        o_ref[...] = (a_ref[...].astype(F32) + b_ref[...].astype(F32)).astype(BF16)

    blk = pl.BlockSpec((None, tr, C), lambda q, i: (q, i, 0))
    return pl.pallas_call(body, name=name, out_shape=jax.ShapeDtypeStruct(a.shape, BF16), grid=(a.shape[0], R // tr),
                          in_specs=[blk, blk], out_specs=blk, compiler_params=_cp("parallel", "parallel"))(a, b)


_HBM = pl.BlockSpec(memory_space=pltpu.HBM)
_SEM = pl.BlockSpec(memory_space=pltpu.SEMAPHORE)
_VMEM = pl.BlockSpec(memory_space=pltpu.VMEM)
_SIDE = pltpu.CompilerParams(has_side_effects=pltpu.SideEffectType.DATAFLOW_SIDE_EFFECTING)
_TOKEN = jax.ShapeDtypeStruct((8, LANE), F32)


def _hbm(a):
    return pltpu.with_memory_space_constraint(a, pltpu.HBM)


def _hbm_like(arrs):
    return tuple(pltpu.HBM(a.shape, a.dtype) for a in arrs)


def _tie(x, token):
    return x + token[0, 0].astype(x.dtype)


def _chip_copies(ins, lands, send, recv, nrel):
    x, y, c, chips = _mesh_pos()
    first = [sum(nrel[:a]) for a in range(len(ins))]
    return [pltpu.make_async_remote_copy(src_ref=ins[a].at[2 * chip[0] + chip[1]], dst_ref=lands[a].at[j], send_sem=send.at[first[a] + j],
                                         recv_sem=recv.at[first[a] + j], device_id=(*chip, c), device_id_type=MESH)
            for a in range(len(ins)) for j, chip in enumerate(chips[:nrel[a]])]


def _chip_start(psums, name, nrel=None):
    n = len(psums)
    nrel = nrel or [3] * n
    lands = [lax.empty((r,) + p.shape[1:], p.dtype) for r, p in zip(nrel, psums)]

    def body(*refs):
        for d in _chip_copies(refs[:n], refs[n:2 * n], refs[2 * n], refs[2 * n + 1], nrel):
            d.start()
        refs[-1][...] = jnp.zeros_like(refs[-1])

    sems = pltpu.SemaphoreType.DMA((sum(nrel),))
    out = pl.pallas_call(
        body, name=name, out_shape=(sems, sems) + _hbm_like(psums) + _hbm_like(lands) + (_TOKEN,),
        in_specs=[_HBM] * (2 * n), out_specs=(_SEM, _SEM) + (_HBM,) * (2 * n) + (_VMEM,),
        input_output_aliases={i: 2 + i for i in range(2 * n)}, compiler_params=_SIDE,
    )(*[_hbm(a) for a in list(psums) + lands])
    return out[0], out[1], list(out[2:2 + n]), list(out[2 + n:2 + 2 * n]), out[-1]


def _chip_wait(send, recv, psums, lands, after, name):
    n = len(psums)
    nrel = [l.shape[0] for l in lands]

    def body(*refs):
        for d in _chip_copies(refs[:n], refs[n:2 * n], refs[2 * n], refs[2 * n + 1], nrel):
            d.wait_send()
            d.wait_recv()

    out = pl.pallas_call(
        body, name=name, out_shape=_hbm_like(psums) + _hbm_like(lands),
        in_specs=[_HBM] * (2 * n) + [_SEM, _SEM, _ANY], out_specs=(_HBM,) * (2 * n),
        input_output_aliases={i: i for i in range(2 * n)}, compiler_params=_SIDE,
    )(*psums, *lands, send, recv, after)
    return list(out[:n]), list(out[n:])


def _hop_pos():
    x, y, c, _ = _mesh_pos()
    north = c == 1
    via = (jnp.where(north, 1 - x, x), jnp.where(north, y, 1 - y))
    return (*via, c), 2 * (1 - x) + (1 - y), jnp.where(north, 2 * x + (1 - y), 2 * (1 - x) + y)


def _hop_copies(ins, lands, send, recv):
    to, mine, _ = _hop_pos()
    return [pltpu.make_async_remote_copy(src_ref=ins[a].at[mine], dst_ref=lands[a], send_sem=send.at[a], recv_sem=recv.at[a],
                                         device_id=to, device_id_type=MESH) for a in range(len(ins))]


def _hop_start(psums, name):
    n = len(psums)
    lands = [lax.empty(p.shape[1:], p.dtype) for p in psums]

    def body(*refs):
        for d in _hop_copies(refs[:n], refs[n:2 * n], refs[2 * n], refs[2 * n + 1]):
            d.start()
        refs[-1][...] = jnp.zeros_like(refs[-1])

    sems = pltpu.SemaphoreType.DMA((n,))
    out = pl.pallas_call(
        body, name=name, out_shape=(sems, sems) + _hbm_like(psums) + _hbm_like(lands) + (_TOKEN,),
        in_specs=[_HBM] * (2 * n), out_specs=(_SEM, _SEM) + (_HBM,) * (2 * n) + (_VMEM,),
        input_output_aliases={i: 2 + i for i in range(2 * n)}, compiler_params=_SIDE,
    )(*[_hbm(a) for a in list(psums) + lands])
    return out[0], out[1], list(out[2:2 + n]), list(out[2 + n:2 + 2 * n]), out[-1]


def _hop_wait(send, recv, psums, lands, after, name):
    n = len(psums)

    def body(*refs):
        for d in _hop_copies(refs[:n], refs[n:2 * n], refs[2 * n], refs[2 * n + 1]):
            d.wait_send()
            d.wait_recv()

    out = pl.pallas_call(
        body, name=name, out_shape=_hbm_like(psums) + _hbm_like(lands),
        in_specs=[_HBM] * (2 * n) + [_SEM, _SEM, _ANY], out_specs=(_HBM,) * (2 * n),
        input_output_aliases={i: i for i in range(2 * n)}, compiler_params=_SIDE,
    )(*psums, *lands, send, recv, after)
    return list(out[:n]), list(out[n:])


def _hop_add(psums, land, idx, name, tr):
    _, R, C = psums.shape

    def body(s_ref, p_ref, l_ref, o_ref):
        o_ref[...] = (p_ref[...].astype(F32) + l_ref[...].astype(F32)).astype(BF16)

    blk = pl.BlockSpec((None, tr, C), lambda i, s: (s[0], i, 0))
    grid_spec = pltpu.PrefetchScalarGridSpec(num_scalar_prefetch=1, grid=(R // tr,),
                                             in_specs=[blk, pl.BlockSpec((tr, C), lambda i, s: (i, 0))], out_specs=blk)
    return pl.pallas_call(body, name=name, out_shape=jax.ShapeDtypeStruct(psums.shape, BF16), grid_spec=grid_spec,
                          input_output_aliases={1: 0}, compiler_params=_cp("parallel"))(idx, psums, land)


def _pair_copies(ins, lands, send, recv):
    x, y, c, _ = _mesh_pos()
    return [pltpu.make_async_remote_copy(src_ref=ins[a].at[:, 1 - c], dst_ref=lands[a], send_sem=send.at[a], recv_sem=recv.at[a],
                                         device_id=(x, y, 1 - c), device_id_type=MESH) for a in range(len(ins))]


def _pair_start(grads, name):
    n = len(grads)
    lands = [lax.empty((4,) + g.shape[2:], g.dtype) for g in grads]

    def body(*refs):
        for d in _pair_copies(refs[:n], refs[n:2 * n], refs[2 * n], refs[2 * n + 1]):
            d.start()
        refs[-1][...] = jnp.zeros_like(refs[-1])

    sems = pltpu.SemaphoreType.DMA((n,))
    out = pl.pallas_call(
        body, name=name, out_shape=(sems, sems) + _hbm_like(grads) + _hbm_like(lands) + (_TOKEN,),
        in_specs=[_HBM] * (2 * n), out_specs=(_SEM, _SEM) + (_HBM,) * (2 * n) + (_VMEM,),
        input_output_aliases={i: 2 + i for i in range(2 * n)}, compiler_params=_SIDE,
    )(*[_hbm(a) for a in list(grads) + lands])
    return out[0], out[1], list(out[2:2 + n]), list(out[2 + n:2 + 2 * n]), out[-1]


def _pair_wait(send, recv, grads, lands, after, name):
    n = len(grads)

    def body(*refs):
        for d in _pair_copies(refs[:n], refs[n:2 * n], refs[2 * n], refs[2 * n + 1]):
            d.wait_send()
            d.wait_recv()

    out = pl.pallas_call(
        body, name=name, out_shape=_hbm_like(grads) + _hbm_like(lands),
        in_specs=[_HBM] * (2 * n) + [_SEM, _SEM, _ANY], out_specs=(_HBM,) * (2 * n),
        input_output_aliases={i: i for i in range(2 * n)}, compiler_params=_SIDE,
    )(*grads, *lands, send, recv, after)
    return list(out[:n]), list(out[n:])


def _slot(chip, c):
    return 4 * chip[0] + 2 * chip[1] + c


def _gather_start(shards, lands, after, name):
    n = len(shards)

    def body(*refs):
        src, land, send, recv = refs[:n], refs[n:2 * n], refs[2 * n + 1], refs[2 * n + 2]
        x, y, c, chips = _mesh_pos()
        for a in range(n):
            for k, to in enumerate([(x, y, 1 - c)] + [(*chip, c) for chip in chips]):
                pltpu.make_async_remote_copy(src_ref=src[a], dst_ref=land[a].at[_slot((x, y), c)], send_sem=send.at[4 * a + k],
                                             recv_sem=recv.at[4 * a + k], device_id=to, device_id_type=MESH).start()
        refs[-1][...] = jnp.zeros_like(refs[-1])

    sems = pltpu.SemaphoreType.DMA((4 * n,))
    out = pl.pallas_call(
        body, name=name, out_shape=(sems, sems) + _hbm_like(shards) + _hbm_like(lands) + (_TOKEN,),
        in_specs=[_HBM] * (2 * n) + [_ANY], out_specs=(_SEM, _SEM) + (_HBM,) * (2 * n) + (_VMEM,),
        input_output_aliases={i: 2 + i for i in range(2 * n)}, compiler_params=_SIDE,
    )(*[_hbm(a) for a in list(shards) + list(lands)], after)
    return out[0], out[1], list(out[2:2 + n]), list(out[2 + n:2 + 2 * n]), out[-1]


def _gather_pass(lands, recv, after, name, first=0):
    n = len(lands)

    def body(*refs):
        land, recv1 = refs[:n], refs[n]
        send2, recv2 = refs[n + 2], refs[n + 3]
        x, y, c, chips = _mesh_pos()
        for a in range(n):
            for j, chip in enumerate(chips):
                blk = land[a].at[_slot(chip, c)]
                pltpu.make_async_remote_copy(src_ref=blk, dst_ref=blk, send_sem=send2.at[3 * a + j], recv_sem=recv1.at[4 * (first + a) + 1 + j],
                                             device_id=(*chip, c), device_id_type=MESH).wait_recv()
                pltpu.make_async_remote_copy(src_ref=blk, dst_ref=blk, send_sem=send2.at[3 * a + j], recv_sem=recv2.at[3 * a + j],
                                             device_id=(x, y, 1 - c), device_id_type=MESH).start()
        refs[-1][...] = jnp.zeros_like(refs[-1])

    sems = pltpu.SemaphoreType.DMA((3 * n,))
    out = pl.pallas_call(
        body, name=name, out_shape=(sems, sems) + _hbm_like(lands) + (_TOKEN,),
        in_specs=[_HBM] * n + [_SEM, _ANY], out_specs=(_SEM, _SEM) + (_HBM,) * n + (_VMEM,),
        input_output_aliases={i: 2 + i for i in range(n)}, compiler_params=_SIDE,
    )(*lands, recv, after)
    return out[0], out[1], list(out[2:2 + n]), out[-1]


def _gather_wait(shards, lands, send, recv, send2, recv2, after, name, first=0):
    n = len(lands)

    def body(*refs):
        src, land = refs[:n], refs[n:2 * n]
        send1, recv1, snd2, rcv2 = refs[2 * n:2 * n + 4]
        x, y, c, chips = _mesh_pos()
        sib = (x, y, 1 - c)
        for a in range(n):
            for k in range(4):
                pltpu.make_async_remote_copy(src_ref=src[a], dst_ref=land[a].at[_slot((x, y), c)], send_sem=send1.at[4 * (first + a) + k],
                                             recv_sem=recv1.at[4 * (first + a) + k], device_id=sib, device_id_type=MESH).wait_send()
            blk = land[a].at[_slot((x, y), 1 - c)]
            pltpu.make_async_remote_copy(src_ref=blk, dst_ref=blk, send_sem=send1.at[4 * (first + a)], recv_sem=recv1.at[4 * (first + a)],
                                         device_id=sib, device_id_type=MESH).wait_recv()
            for j, chip in enumerate(chips):
                mine, theirs = land[a].at[_slot(chip, c)], land[a].at[_slot(chip, 1 - c)]
                pltpu.make_async_remote_copy(src_ref=mine, dst_ref=mine, send_sem=snd2.at[3 * a + j], recv_sem=rcv2.at[3 * a + j],
                                             device_id=sib, device_id_type=MESH).wait_send()
                pltpu.make_async_remote_copy(src_ref=theirs, dst_ref=theirs, send_sem=snd2.at[3 * a + j], recv_sem=rcv2.at[3 * a + j],
                                             device_id=sib, device_id_type=MESH).wait_recv()

    out = pl.pallas_call(
        body, name=name, out_shape=_hbm_like(shards) + _hbm_like(lands),
        in_specs=[_HBM] * (2 * n) + [_SEM] * 4 + [_ANY], out_specs=(_HBM,) * (2 * n),
        input_output_aliases={i: i for i in range(2 * n)}, compiler_params=_SIDE,
    )(*shards, *lands, send, recv, send2, recv2, after)
    return list(out[n:])


def _win_tree():
    x, y, c, chips = _mesh_pos()
    north = c == 1
    handed = (jnp.where(north, 1 - x, x), jnp.where(north, y, 1 - y))
    hand_to = (jnp.where(north, x, 1 - x), jnp.where(north, 1 - y, y))
    return x, y, c, chips, handed, hand_to


def _blk(land, chip, c):
    return land.at[_slot(chip, c)]


def _rcopy(src, dst, send, recv, to):
    return pltpu.make_async_remote_copy(src_ref=src, dst_ref=dst, send_sem=send, recv_sem=recv, device_id=to, device_id_type=MESH)


def _win_start(shards, lands, name):
    n = len(shards)

    def body(*refs):
        src, land, send, recv = refs[:n], refs[n:2 * n], refs[2 * n], refs[2 * n + 1]
        x, y, c, chips, _, _ = _win_tree()
        for a in range(n):
            for k, to in enumerate([(x, y, 1 - c), (*chips[0], c), (*chips[1], c)]):
                _rcopy(src[a], _blk(land[a], (x, y), c), send.at[3 * a + k], recv.at[3 * a + k], to).start()
        refs[-1][...] = jnp.zeros_like(refs[-1])

    sems = pltpu.SemaphoreType.DMA((3 * n,))
    out = pl.pallas_call(
        body, name=name, out_shape=(sems, sems) + _hbm_like(shards) + _hbm_like(lands) + (_TOKEN,),
        in_specs=[_HBM] * (2 * n), out_specs=(_SEM, _SEM) + (_HBM,) * (2 * n) + (_VMEM,),
        input_output_aliases={i: 2 + i for i in range(2 * n)}, compiler_params=_SIDE,
    )(*[_hbm(a) for a in list(shards) + list(lands)])
    return out[0], out[1], list(out[2:2 + n]), list(out[2 + n:2 + 2 * n]), out[-1]


def _win_hand_on(lands, recv1, after, name):
    n, m = len(lands), len(after)

    def body(*refs):
        land, rcv1 = refs[:n], refs[n]
        send2, recv2 = refs[n + 1 + m], refs[n + 2 + m]
        x, y, c, chips, handed, hand_to = _win_tree()
        for a in range(n):
            for j in range(2):
                blk = _blk(land[a], chips[j], c)
                _rcopy(blk, blk, send2.at[3 * a], rcv1.at[3 * a + 1 + j], (*chips[j], c)).wait_recv()
            blk = _blk(land[a], handed, c)
            _rcopy(blk, blk, send2.at[3 * a], recv2.at[3 * a], (*hand_to, c)).start()
            for j in range(2):
                blk = _blk(land[a], chips[j], c)
                _rcopy(blk, blk, send2.at[3 * a + 1 + j], recv2.at[3 * a + 1 + j], (x, y, 1 - c)).start()
        refs[-1][...] = jnp.zeros_like(refs[-1])

    sems = pltpu.SemaphoreType.DMA((3 * n,))
    out = pl.pallas_call(
        body, name=name, out_shape=(sems, sems) + _hbm_like(lands) + (_TOKEN,),
        in_specs=[_HBM] * n + [_SEM] + [_ANY] * m, out_specs=(_SEM, _SEM) + (_HBM,) * n + (_VMEM,),
        input_output_aliases={i: 2 + i for i in range(n)}, compiler_params=_SIDE,
    )(*lands, recv1, *after)
    return out[0], out[1], list(out[2:2 + n]), out[-1]


def _win_last(lands, recv2, after, name):
    n, m = len(lands), len(after)

    def body(*refs):
        land, rcv2 = refs[:n], refs[n]
        send3, recv3 = refs[n + 1 + m], refs[n + 2 + m]
        x, y, c, chips, _, hand_to = _win_tree()
        for a in range(n):
            blk = _blk(land[a], chips[2], c)
            _rcopy(blk, blk, send3.at[a], rcv2.at[3 * a], (*hand_to, c)).wait_recv()
            _rcopy(blk, blk, send3.at[a], recv3.at[a], (x, y, 1 - c)).start()
        refs[-1][...] = jnp.zeros_like(refs[-1])

    sems = pltpu.SemaphoreType.DMA((n,))
    out = pl.pallas_call(
        body, name=name, out_shape=(sems, sems) + _hbm_like(lands) + (_TOKEN,),
        in_specs=[_HBM] * n + [_SEM] + [_ANY] * m, out_specs=(_SEM, _SEM) + (_HBM,) * n + (_VMEM,),
        input_output_aliases={i: 2 + i for i in range(n)}, compiler_params=_SIDE,
    )(*lands, recv2, *after)
    return out[0], out[1], list(out[2:2 + n]), out[-1]


def _win_wait(shards, lands, sems1, sems2, sems3, after, name):
    n = len(lands)

    def body(*refs):
        src, land = refs[:n], refs[n:2 * n]
        send1, recv1, send2, recv2, send3, recv3 = refs[2 * n:2 * n + 6]
        x, y, c, chips, handed, hand_to = _win_tree()
        sib = (x, y, 1 - c)
        for a in range(n):
            own = _blk(land[a], (x, y), c)
            for k in range(3):
                _rcopy(src[a], own, send1.at[3 * a + k], recv1.at[3 * a + k], sib).wait_send()
            blk = _blk(land[a], (x, y), 1 - c)
            _rcopy(blk, blk, send1.at[3 * a], recv1.at[3 * a], sib).wait_recv()
            blk = _blk(land[a], handed, c)
            _rcopy(blk, blk, send2.at[3 * a], recv2.at[3 * a], sib).wait_send()
            for j in range(2):
                mine, theirs = _blk(land[a], chips[j], c), _blk(land[a], chips[j], 1 - c)
                _rcopy(mine, mine, send2.at[3 * a + 1 + j], recv2.at[3 * a + 1 + j], sib).wait_send()
                _rcopy(theirs, theirs, send2.at[3 * a + 1 + j], recv2.at[3 * a + 1 + j], sib).wait_recv()
            mine, theirs = _blk(land[a], chips[2], c), _blk(land[a], chips[2], 1 - c)
            _rcopy(mine, mine, send3.at[a], recv3.at[a], sib).wait_send()
            _rcopy(theirs, theirs, send3.at[a], recv3.at[a], sib).wait_recv()

    out = pl.pallas_call(
        body, name=name, out_shape=_hbm_like(shards) + _hbm_like(lands),
        in_specs=[_HBM] * (2 * n) + [_SEM] * 6 + [_ANY], out_specs=(_HBM,) * (2 * n),
        input_output_aliases={i: i for i in range(2 * n)}, compiler_params=_SIDE,
    )(*shards, *lands, *sems1, *sems2, *sems3, after)
    return list(out[n:])


def _pad_to(v, n):
    return jnp.pad(v, [(0, 0)] * (v.ndim - 1) + [(0, n - v.shape[-1])])


def _pack_small(n1, gb, sk, gn, n2, fn, extra=None):
    parts = [n1.reshape(-1), gb.reshape(-1), sk.reshape(-1), gn.reshape(-1), n2.reshape(-1), fn.reshape(-1)]
    flat = jnp.concatenate(parts + ([extra.reshape(-1)] if extra is not None else []))
    return _pad_to(flat, SMALL_N).reshape(SMALL_ROWS, LANE)


def _unpack_small(p):
    f = p.reshape(-1)
    return (f[S_N1:S_GB].reshape(1, D), f[S_GB:S_SK].reshape(1, GH * DK), f[S_SK:S_GN].reshape(1, NQ), f[S_GN:S_N2].reshape(1, DV),
            f[S_N2:S_FN].reshape(1, D), f[S_FN:S_LOSS].reshape(D))


class _Comm:
    def __init__(self, rest_shards, rest_lands, after, c_idx):
        self.c_idx = c_idx
        self.send, self.recv, self.shards, self.lands, self.token = _gather_start(rest_shards, rest_lands, after, "gather_rest_start")

    def _pass(self, lo, hi, after, tag):
        send2, recv2, lands, token = _gather_pass(self.lands[lo:hi], self.recv, after, "gather_pass_" + tag, first=lo)
        self.passed = (lo, hi, send2, recv2, lands)
        return token

    def _wait(self, after, tag):
        lo, hi, send2, recv2, lands = self.passed
        return _gather_wait(self.shards[lo:hi], lands, self.send, self.recv, send2, recv2, after, "gather_wait_" + tag, first=lo)

    def mixed(self, gla_o, gla_norm_w):
        return _tie(gla_norm_w, self._pass(0, 1, gla_o, "out"))

    def w_out(self, merged, norm2_w):
        (wo_all,) = self._wait(merged, "out")
        return wo_all.reshape(D, D), _tie(norm2_w, self._pass(1, 3, merged, "up"))

    def w_up(self, v2):
        wg_all, wu_all = self._wait(v2, "up")
        self._pass(3, 4, v2, "down")
        return wg_all.reshape(FH, D), wu_all.reshape(FH, D)

    def w_down(self, ff):
        return self._wait(ff, "down")[0].reshape(FH, D)

    def _reduce(self, tag, names, grads, recv1, rows):
        psums = [_pair_add(g, r, self.c_idx, "pair_add_" + nm, tr) for g, r, nm, tr in zip(grads, recv1, names, rows)]
        *flight, token = _chip_start(psums, "reduce_chips_start_" + tag)
        return dict(tag=tag, names=names, rows=rows, flight=flight), token

    def ffn_grads(self, d_wg, d_wu, d_wd):
        self.ffn_pair = _pair_start([d.reshape(4, 2, FS, D) for d in (d_wg, d_wu, d_wd)], "reduce_pair_start_ffn")
        return self.ffn_pair[-1]

    def ffn_reduce(self, dv2, norm2_w):
        send, recv, grads, lands, _ = self.ffn_pair
        grads, recv1 = _pair_wait(send, recv, grads, lands, dv2, "reduce_pair_wait_ffn")
        self.ffn, token = self._reduce("ffn", ["w_ffn_gate", "w_ffn_up", "w_ffn_down"], grads, recv1, [176, 176, 176])
        return _tie(norm2_w, token)

    def in_grads(self, d_wmain, d_wlr, w_lr):
        mine, recv = _disassemble_exchange(d_wmain, d_wlr)
        self.in_names, self.in_rows = ["w_in", "w_out"], [808, 256]
        *self.in_hop, token = _hop_start([_add_blocks(mine, recv, "pair_add_w_in", 808)], "reduce_hop_start_in")
        return _tie(w_lr, token)

    def in_reduce(self, d_wo):
        d_wo4 = d_wo.reshape(4, 2, D // NDEV, D)
        send, recv, (d_wo4,), lands, _ = _pair_start([d_wo4], "reduce_pair_start_out")
        (d_wo4,), (wo_recv,) = _pair_wait(send, recv, [d_wo4], lands, self.update(self.ffn, d_wo), "reduce_pair_wait_out")
        wo_psum = _pair_add(d_wo4, wo_recv, self.c_idx, "pair_add_w_out", 256)
        (psum,), (land,) = _hop_wait(*self.in_hop, wo_psum, "reduce_hop_wait_in")
        psum = _hop_add(psum, land, _hop_pos()[2].astype(jnp.int32).reshape(1), "hop_add_w_in", 808)
        *flight, token = _chip_start([psum, wo_psum], "reduce_chips_start_in", nrel=[2, 3])
        self.inw = dict(tag="in", names=self.in_names, rows=self.in_rows, flight=flight)
        return token


def _local_step(xs, tgt, u, norm1_w, gla_gate_b, attn_sinks, gla_norm_w, norm2_w, fnw, w_main, w_lr, w2p, comm):
    proj =_mm(u, w_main, tb=True, tm=1024, tn=1280, tk=D, name="in_proj")
    plr = _mm(u, w_lr, tb=True, tm=1024, tn=LANE, tk=D, name="in_proj_lr")
    attn_o = _attn_fwd(proj, attn_sinks)
    gla_o, states = _gla_fwd(proj, plr, w2p, gla_gate_b)
    merged = _merge_fwd(attn_o, gla_o, proj, comm.mixed(gla_o, gla_norm_w))
    wo, norm2_w = comm.w_out(merged, norm2_w)
    h1 = _mm(merged, wo, tm=1024, tn=512, tk=D, res=xs, name="out_proj")
    v2 = _rmsnorm_fwd(h1, norm2_w, "norm2_fwd")
    wg_all, wu_all = comm.w_up(v2)
    fa, fb, ff = _ffn_up(v2, wg_all, wu_all)
    wd_all = comm.w_down(ff)
    h2 = _mm(ff, wd_all, tm=1024, tn=1024, tk=FH // 2, res=h1, name="ffn_down")
    dh2, dh2b, d_fnw, loss_part = _loss_head(h2, fnw, tgt)

    da, db = _ffn_dact(dh2b, wd_all, fa, fb)
    Tn = xs.shape[0]
    d_wd = _mm(ff, dh2b, ta=True, tm=512, tn=D, tk=Tn, out_dtype=BF16, name="ffn_dwd")
    d_wg = _mm(da, v2, ta=True, tm=512, tn=D, tk=Tn, out_dtype=BF16, name="ffn_dwg")
    d_wu = _mm(db, v2, ta=True, tm=512, tn=D, tk=Tn, out_dtype=BF16, name="ffn_dwu")
    dv2 = _mm(da, wg_all, tm=1024, tn=1024, tk=FH // 2, after=comm.ffn_grads(d_wg, d_wu, d_wd), name="ffn_dv2_gate")
    dv2 = _mm(db, wu_all, tm=1024, tn=1024, tk=FH // 2, res=dv2, name="ffn_dv2_up")
    norm2_w = comm.ffn_reduce(dv2, norm2_w)
    dh1, dh1b, d_n2 = _rmsnorm_bwd(dv2, h1, norm2_w, dh2, "norm2_bwd")
    dmerged = _mm(dh1b, wo, tb=True, tm=1024, tn=512, tk=D, name="out_proj_dx")
    d_attn, d_gla, d_gates, d_gnw = _merge_bwd(dmerged, attn_o, gla_o, proj, gla_norm_w)
    d_q, d_kv, d_sinks = _attn_bwd(proj, attn_sinks, attn_o, d_attn)
    d_gqk, d_gv, d_plr, d_w2p, d_gb = _gla_bwd(proj, plr, w2p, gla_gate_b, states, d_gla)
    dproj = jnp.concatenate([d_q, d_kv, d_gqk, d_gv, d_gates], axis=1)
    d_wmain = _mm(dproj, u, ta=True, tm=640, tn=D, tk=xs.shape[0], out_dtype=BF16, name="in_proj_dw")
    d_wlr = _mm(d_plr, u, ta=True, tm=LANE, tn=1024, tk=xs.shape[0], out_dtype=BF16, name="in_proj_lr_dw")
    du_lr = _mm(d_plr, comm.in_grads(d_wmain, d_wlr, w_lr), tm=1024, tn=1024, tk=LANE, name="in_proj_lr_dx")
    d_wo = _mm(merged, dh1b, ta=True, tm=1024, tn=512, tk=xs.shape[0], out_dtype=BF16, after=du_lr, name="out_proj_dw")
    du = _mm(dproj, w_main, tm=1024, tn=1024, tk=2560, res=du_lr, after=comm.in_reduce(d_wo), name="in_proj_dx")
    dx, _, d_n1 = _rmsnorm_bwd(du, xs, norm1_w, dh1, "norm1_bwd")
    return dx, loss_part, d_w2p, d_gb, d_sinks, d_gnw, d_n1, d_n2, d_fnw


def kernel(x, norm1_w, w_in, gla_gate_w2, gla_gate_b, attn_sinks, gla_norm_w, w_out, norm2_w, w_ffn_gate, w_ffn_up, w_ffn_down, final_norm_w, loss_target, m_norm1_w, m_w_in, m_gla_gate_w2, m_gla_gate_b, m_attn_sinks, m_gla_norm_w, m_w_out, m_norm2_w, m_w_ffn_gate, m_w_ffn_up, m_w_ffn_down, m_final_norm_w, v_norm1_w, v_w_in, v_gla_gate_w2, v_gla_gate_b, v_attn_sinks, v_gla_norm_w, v_w_out, v_norm2_w, v_w_ffn_gate, v_w_ffn_up, v_w_ffn_down, v_final_norm_w):
    xs, tgt = x[0], loss_target[0]
    fnw = final_norm_w.reshape(1, D)
    c_idx = lax.axis_index("c").astype(jnp.int32).reshape(1)
    dev = 4 * lax.axis_index("x") + 2 * lax.axis_index("y") + lax.axis_index("c")

    chip_idx = (2 * lax.axis_index("x") + lax.axis_index("y")).astype(jnp.int32).reshape(1)

    shift = (WS - WSTEP) * dev
    edge = WWIN - WS
    window = lax.dynamic_slice(jnp.pad(jnp.transpose(w_in[0]).astype(BF16), ((edge, edge), (0, 0))), (edge - shift, 0), (WWIN, D))
    w2_land = lax.dynamic_update_slice(lax.empty((NDEV, RANK, LANE), F32), gla_gate_w2, (dev, 0, 0))
    *sems1, win_srcs, win_lands, tok = _win_start([window, gla_gate_w2[0]], [lax.empty((NDEV, WWIN, D), BF16), w2_land], "gather_in_start")
    tr2 = lambda t: jnp.transpose(t[0])
    rows3 = lambda t: jnp.transpose(t[0] + tok[0, 0])
    rest = [(w + tok[0, 0]).astype(BF16) for w in (w_out[0], tr2(w_ffn_gate), tr2(w_ffn_up), w_ffn_down[0])]
    rest_lands = [lax.dynamic_update_slice(lax.empty((NDEV,) + s.shape, s.dtype), s[None], (dev, 0, 0)) for s in rest]
    win3 = [rows3(t) for t in (w_in, m_w_in, v_w_in)]
    *sems2, win_lands, tok = _win_hand_on(win_lands, sems1[1], rest + rest_lands + win3, "gather_in_hand_on")
    u = _rmsnorm_fwd(xs, _tie(norm1_w, tok), "norm1_fwd")
    *sems3, win_lands, tok = _win_last(win_lands, sems2[1], [u], "gather_in_last")
    comm = _Comm(rest, rest_lands, tok, c_idx)
    win_all, w2_all = _win_wait(win_srcs, win_lands, sems1, sems2, sems3, comm.token, "gather_in_wait")
    w_main, w_lr = _assemble_w_in(win_all, window)
    w2p = jnp.pad(jnp.transpose(w2_all, (1, 0, 2)).reshape(RANK, GH * DK), ((0, LANE - RANK), (0, 0)))

    big = {}

    def update(grp, after):
        psums, parts = _chip_wait(*grp["flight"], after, "reduce_chips_wait_" + grp["tag"])
        for nm, ps, pt, tr in zip(grp["names"], psums, parts, grp["rows"]):
            w, m, v = {"w_in": (w_in, m_w_in, v_w_in), "w_out": (w_out, m_w_out, v_w_out), "w_ffn_gate": (w_ffn_gate, m_w_ffn_gate, v_w_ffn_gate),
                       "w_ffn_up": (w_ffn_up, m_w_ffn_up, v_w_ffn_up), "w_ffn_down": (w_ffn_down, m_w_ffn_down, v_w_ffn_down)}[nm]
            if nm == "w_in":
                g_win = _sum_parts(ps, pt, chip_idx, "sum_w_in", tr, 1024)
                g3 = lax.dynamic_slice(g_win, (shift, 0), (WS, D))
                out3 = (g3,) + tuple(_adamw_given(*win3, g3, "adamw_w_in", 536, 512))
                big[nm] = [jnp.transpose(t)[None] for t in out3]
            elif nm in ("w_ffn_gate", "w_ffn_up"):
                big[nm] = [jnp.transpose(t)[None] for t in _adamw(tr2(w), tr2(m), tr2(v), ps, pt, chip_idx, "adamw_" + nm, tr)]
            else:
                big[nm] = [t[None] for t in _adamw(w[0], m[0], v[0], ps, pt, chip_idx, "adamw_" + nm, tr)]
            after = big[nm][0]
        return after

    comm.update = update
    dx, loss_part, d_w2p, d_gb, d_sinks, d_gnw, d_n1, d_n2, d_fnw = _local_step(
        xs, tgt, u, norm1_w, gla_gate_b, attn_sinks, gla_norm_w, norm2_w, fnw, w_main, w_lr, w2p, comm)

    pack = jnp.concatenate([_pack_small(d_n1, d_gb, d_sinks, d_gnw, d_n2, d_fnw, loss_part),
                            d_w2p[:RANK].reshape(GW2_ROWS, LANE)], axis=0)
    small = _sum_devices(_gather_small(pack))

    update(comm.inw, dx)
    g_small = small[:SMALL_ROWS]
    sm = _adamw_plain(_pack_small(norm1_w, gla_gate_b, attn_sinks, gla_norm_w, norm2_w, final_norm_w),
                      _pack_small(m_norm1_w, m_gla_gate_b, m_attn_sinks, m_gla_norm_w, m_norm2_w, m_final_norm_w),
                      _pack_small(v_norm1_w, v_gla_gate_b, v_attn_sinks, v_gla_norm_w, v_norm2_w, v_final_norm_w), g_small, "adamw_small")
    g_w2 = lax.dynamic_slice_in_dim(small[SMALL_ROWS:].reshape(RANK, GH * DK), dev * LANE, LANE, axis=1)
    w2 = [g_w2[None]] + [t[None] for t in _adamw_plain(gla_gate_w2[0], m_gla_gate_w2[0], v_gla_gate_w2[0], g_w2, "adamw_w2")]
    loss = g_small.reshape(-1)[S_LOSS]

    sg, sd, sm2, sv2 = [_unpack_small(t) for t in (g_small,) + tuple(sm)]

    def group(i, s):
        return (s[0], big["w_in"][i], w2[i], s[1], s[2], s[3], big["w_out"][i], s[4], big["w_ffn_gate"][i], big["w_ffn_up"][i],
                big["w_ffn_down"][i], s[5])

    return (loss, dx[None], *group(0, sg), *group(1, sd), *group(2, sm2), *group(3, sv2))
```

```python
import jax
import jax.numpy as jnp
from jax import lax
from jax.experimental import pallas as pl
from jax.experimental.pallas import tpu as pltpu

F32, BF16 = jnp.float32, jnp.bfloat16
HIGHEST = lax.Precision.HIGHEST

D = 2048
HD, NQ, NKV, GRP, WIN = 64, 32, 4, 8, 128
GH, DK, DV, RANK, GC = 4, 256, 512, 16, 64
FH, NDEV = 5632, 8
FS = FH // NDEV
DIN = 12816
WS = DIN // NDEV
EPS = 1e-6
MASKV = -1e30
LANE = 128

C_AQ, C_AK, C_AV, C_GQ, C_GK, C_GV, C_GR, C_GA, C_GB, NMAIN = 0, 2048, 2304, 2560, 3584, 4608, 6656, 8704, 10752, 12800
C_LR = 6656
WSTEP, WWIN = 1600, 1616

LR, B1, B2, AEPS, WD, STEP = 0.001, 0.9, 0.999, 1e-08, 0.01, 10

S_N1, S_GB, S_SK, S_GN, S_N2, S_FN, S_LOSS, SMALL_N = 0, 2048, 3072, 3104, 3616, 5664, 7712, 8192
SMALL_ROWS = SMALL_N // LANE
GW2_ROWS = RANK * GH * DK // LANE

MESH = pl.DeviceIdType.MESH


def _dot(a, b, ta=False, tb=False, prec=None):
    dn = (((0,) if ta else (1,), (1,) if tb else (0,)), ((), ()))
    return lax.dot_general(a, b, dn, preferred_element_type=F32, precision=prec)


def _sigmoid(x):
    return 1.0 / (1.0 + jnp.exp(-x))


VMEM_LIMIT = 56 * 1024 * 1024


def _cp(*sem):
    return pltpu.CompilerParams(dimension_semantics=sem, vmem_limit_bytes=VMEM_LIMIT)


def _mm(a, b, *, ta=False, tb=False, tm, tn, tk, out_dtype=F32, res=None, after=None, name):
    M, K = (a.shape[1], a.shape[0]) if ta else a.shape
    N = b.shape[0] if tb else b.shape[1]
    tm, tn, tk = min(tm, M), min(tn, N), min(tk, K)
    nk = K // tk
    assert M % tm == 0 and N % tn == 0 and K % tk == 0
    a_spec = pl.BlockSpec((tk, tm), lambda i, j, k: (k, i)) if ta else pl.BlockSpec((tm, tk), lambda i, j, k: (i, k))
    b_spec = pl.BlockSpec((tn, tk), lambda i, j, k: (j, k)) if tb else pl.BlockSpec((tk, tn), lambda i, j, k: (k, j))
    o_spec = pl.BlockSpec((tm, tn), lambda i, j, k: (i, j))
    has_res = res is not None

    def body(*refs):
        a_ref, b_ref = refs[0], refs[1]
        r_ref = refs[2] if has_res else None
        o_ref = refs[2 + has_res + (after is not None)]
        p = _dot(a_ref[...].astype(BF16), b_ref[...].astype(BF16), ta, tb)
        if nk == 1:
            if has_res:
                p = p + r_ref[...]
            o_ref[...] = p.astype(out_dtype)
        else:
            acc = refs[-1]
            k = pl.program_id(2)

            @pl.when(k == 0)
            def _():
                acc[...] = (p + r_ref[...]) if has_res else p

            @pl.when(k > 0)
            def _():
                acc[...] += p

            @pl.when(k == nk - 1)
            def _():
                o_ref[...] = acc[...].astype(out_dtype)

    return pl.pallas_call(
        body, name=name,
        out_shape=jax.ShapeDtypeStruct((M, N), out_dtype),
        grid=(M // tm, N // tn, nk),
        in_specs=[a_spec, b_spec] + ([o_spec] if has_res else []) + ([pl.BlockSpec(memory_space=pl.ANY)] if after is not None else []),
        out_specs=o_spec,
        scratch_shapes=[pltpu.VMEM((tm, tn), F32)] if nk > 1 else [],
        compiler_params=_cp("parallel", "parallel", "arbitrary"),
    )(*((a, b) + ((res,) if has_res else ()) + ((after,) if after is not None else ())))


def _rmsnorm_fwd(x, w, name, tm=256):
    Tn = x.shape[0]

    def body(x_ref, w_ref, o_ref):
        xv = x_ref[...]
        r = lax.rsqrt(jnp.mean(xv * xv, axis=1, keepdims=True) + EPS)
        o_ref[...] = (xv * r * w_ref[...]).astype(BF16)

    return pl.pallas_call(
        body, name=name, out_shape=jax.ShapeDtypeStruct((Tn, D), BF16), grid=(Tn // tm,),
        in_specs=[pl.BlockSpec((tm, D), lambda i: (i, 0)), pl.BlockSpec((1, D), lambda i: (0, 0))],
        out_specs=pl.BlockSpec((tm, D), lambda i: (i, 0)), compiler_params=_cp("parallel"),
    )(x, w)


def _rmsnorm_bwd(dy, h, w, res, name, tm=256):
    Tn = h.shape[0]

    def body(dy_ref, h_ref, w_ref, res_ref, dh_ref, dhb_ref, dw_ref):
        hv, dyv = h_ref[...], dy_ref[...]
        r = lax.rsqrt(jnp.mean(hv * hv, axis=1, keepdims=True) + EPS)
        g = dyv * w_ref[...]
        dh = res_ref[...] + r * g - hv * (r * r * r * jnp.mean(g * hv, axis=1, keepdims=True))
        dh_ref[...] = dh
        dhb_ref[...] = dh.astype(BF16)
        part = jnp.sum(dyv * hv * r, axis=0, keepdims=True)

        @pl.when(pl.program_id(0) == 0)
        def _():
            dw_ref[...] = part

        @pl.when(pl.program_id(0) > 0)
        def _():
            dw_ref[...] += part

    row = pl.BlockSpec((tm, D), lambda i: (i, 0))
    vec = pl.BlockSpec((1, D), lambda i: (0, 0))
    return pl.pallas_call(
        body, name=name,
        out_shape=(jax.ShapeDtypeStruct((Tn, D), F32), jax.ShapeDtypeStruct((Tn, D), BF16), jax.ShapeDtypeStruct((1, D), F32)),
        grid=(Tn // tm,), in_specs=[row, row, vec, row], out_specs=(row, row, vec), compiler_params=_cp("arbitrary"),
    )(dy, h, w, res)


def _loss_head(h2, wf, tgt, name="loss_head", tm=256):
    Tn = h2.shape[0]

    def body(h_ref, w_ref, t_ref, dh_ref, dhb_ref, dw_ref, loss_ref):
        hv, wv = h_ref[...], w_ref[...]
        r = lax.rsqrt(jnp.mean(hv * hv, axis=1, keepdims=True) + EPS)
        hn = hv * r
        e = hn * wv - t_ref[...]
        dy = e * (1.0 / D)
        g = dy * wv
        dh = r * g - hv * (r * r * r * jnp.mean(g * hv, axis=1, keepdims=True))
        dh_ref[...] = dh
        dhb_ref[...] = dh.astype(BF16)
        part = jnp.sum(dy * hn, axis=0, keepdims=True)
        lpart = (0.5 / D) * jnp.sum(jnp.sum(e * e, axis=1, keepdims=True), axis=0, keepdims=True)

        @pl.when(pl.program_id(0) == 0)
        def _():
            dw_ref[...] = part
            loss_ref[...] = lpart

        @pl.when(pl.program_id(0) > 0)
        def _():
            dw_ref[...] += part
            loss_ref[...] += lpart

    row = pl.BlockSpec((tm, D), lambda i: (i, 0))
    vec = pl.BlockSpec((1, D), lambda i: (0, 0))
    one = pl.BlockSpec((1, 1), lambda i: (0, 0))
    return pl.pallas_call(
        body, name=name,
        out_shape=(jax.ShapeDtypeStruct((Tn, D), F32), jax.ShapeDtypeStruct((Tn, D), BF16), jax.ShapeDtypeStruct((1, D), F32),
                   jax.ShapeDtypeStruct((1, 1), F32)),
        grid=(Tn // tm,), in_specs=[row, vec, row], out_specs=(row, row, vec, one), compiler_params=_cp("arbitrary"),
    )(h2, wf, tgt)


def _attn_mask(n):
    qi = lax.broadcasted_iota(jnp.int32, (NKV, GRP * WIN, 2 * WIN), 1) % WIN
    ki = lax.broadcasted_iota(jnp.int32, (NKV, GRP * WIN, 2 * WIN), 2)
    rel = qi + WIN - ki
    return (rel >= 0) & (rel < WIN) & ((n > 0) | (ki >= WIN))


def _kv_heads(prev_ref, cur_ref):
    return jnp.stack([jnp.concatenate([prev_ref[:, h * HD:(h + 1) * HD], cur_ref[:, h * HD:(h + 1) * HD]], axis=0) for h in range(NKV)])


def _q_heads(ref):
    return jnp.stack([jnp.concatenate([ref[:, (h * GRP + g) * HD:(h * GRP + g + 1) * HD] for g in range(GRP)], axis=0) for h in range(NKV)])


def _attn_probs(q_ref, kc_ref, kp_ref, sink_ref, mask):
    kk = _kv_heads(kp_ref, kc_ref).astype(BF16)
    qs = _q_heads(q_ref).astype(BF16)
    s = jnp.einsum('hqd,hkd->hqk', qs, kk, preferred_element_type=F32) * (HD ** -0.5)
    s = jnp.where(mask, s, MASKV)
    sink = jnp.stack([jnp.concatenate([jnp.full((WIN, 1), sink_ref[0, h * GRP + g], F32) for g in range(GRP)], axis=0) for h in range(NKV)])
    m = jnp.maximum(jnp.max(s, axis=2, keepdims=True), sink)
    e = jnp.exp(s - m)
    es = jnp.exp(sink - m)
    inv = 1.0 / (jnp.sum(e, axis=2, keepdims=True) + es)
    return e * inv, es * inv, qs, kk


def _attn_specs(nb, last):
    cur = lambda n: jnp.minimum(n, last)
    prev = lambda n: jnp.maximum(jnp.minimum(n, last) - 1, 0)
    return [
        pl.BlockSpec((WIN, NQ * HD), lambda n: (cur(n), C_AQ // (NQ * HD))),
        pl.BlockSpec((WIN, NKV * HD), lambda n: (cur(n), C_AK // (NKV * HD))),
        pl.BlockSpec((WIN, NKV * HD), lambda n: (prev(n), C_AK // (NKV * HD))),
        pl.BlockSpec((WIN, NKV * HD), lambda n: (cur(n), C_AV // (NKV * HD))),
        pl.BlockSpec((WIN, NKV * HD), lambda n: (prev(n), C_AV // (NKV * HD))),
    ]


def _attn_fwd(proj, sinks, name="attn_fwd"):
    Tn = proj.shape[0]
    nb = Tn // WIN

    def body(q_ref, kc_ref, kp_ref, vc_ref, vp_ref, sink_ref, o_ref):
        p, _, _, _ = _attn_probs(q_ref, kc_ref, kp_ref, sink_ref, _attn_mask(pl.program_id(0)))
        o = jnp.einsum('hqk,hkd->hqd', p.astype(BF16), _kv_heads(vp_ref, vc_ref).astype(BF16), preferred_element_type=F32)
        for h in range(NKV):
            for g in range(GRP):
                o_ref[:, (h * GRP + g) * HD:(h * GRP + g + 1) * HD] = o[h, g * WIN:(g + 1) * WIN, :]

    return pl.pallas_call(
        body, name=name, out_shape=jax.ShapeDtypeStruct((Tn, D), F32), grid=(nb,),
        in_specs=_attn_specs(nb, nb - 1) + [pl.BlockSpec(memory_space=pltpu.SMEM)],
        out_specs=pl.BlockSpec((WIN, D), lambda n: (n, 0)), compiler_params=_cp("parallel"),
    )(proj, proj, proj, proj, proj, sinks)


def _attn_bwd(proj, sinks, o, do, name="attn_bwd"):
    Tn = proj.shape[0]
    nb = Tn // WIN
    KW = NKV * HD

    def body(q_ref, kc_ref, kp_ref, vc_ref, vp_ref, o_ref, do_ref, sink_ref, dq_ref, dkv_ref, dsk_ref, carry, cur):
        n = pl.program_id(0)

        @pl.when(n == 0)
        def _():
            carry[...] = jnp.zeros_like(carry)
            dsk_ref[...] = jnp.zeros_like(dsk_ref)

        @pl.when(n < nb)
        def _():
            p, ps, qs, kk = _attn_probs(q_ref, kc_ref, kp_ref, sink_ref, _attn_mask(n))
            vv = _kv_heads(vp_ref, vc_ref).astype(BF16)
            dos = _q_heads(do_ref)
            delta = jnp.sum(dos * _q_heads(o_ref), axis=2, keepdims=True)
            dosb = dos.astype(BF16)
            dp = jnp.einsum('hqd,hkd->hqk', dosb, vv, preferred_element_type=F32)
            ds = (p * (dp - delta) * (HD ** -0.5)).astype(BF16)
            dq = jnp.einsum('hqk,hkd->hqd', ds, kk, preferred_element_type=F32)
            dkk = jnp.einsum('hqk,hqd->hkd', ds, qs, preferred_element_type=F32)
            dvv = jnp.einsum('hqk,hqd->hkd', p.astype(BF16), dosb, preferred_element_type=F32)
            dsk = ps * delta
            for h in range(NKV):
                for g in range(GRP):
                    i = h * GRP + g
                    dq_ref[:, i * HD:(i + 1) * HD] = dq[h, g * WIN:(g + 1) * WIN, :].astype(BF16)
                    dsk_ref[:, i:i + 1] -= jnp.sum(dsk[h, g * WIN:(g + 1) * WIN, :], axis=0, keepdims=True)
                dkv_ref[:, h * HD:(h + 1) * HD] = (carry[:, h * HD:(h + 1) * HD] + dkk[h, :WIN, :]).astype(BF16)
                dkv_ref[:, KW + h * HD:KW + (h + 1) * HD] = (carry[:, KW + h * HD:KW + (h + 1) * HD] + dvv[h, :WIN, :]).astype(BF16)
                cur[:, h * HD:(h + 1) * HD] = dkk[h, WIN:, :]
                cur[:, KW + h * HD:KW + (h + 1) * HD] = dvv[h, WIN:, :]
            carry[...] = cur[...]

        @pl.when(n == nb)
        def _():
            dkv_ref[...] = carry[...].astype(BF16)

    last = nb - 1
    row = pl.BlockSpec((WIN, D), lambda n: (jnp.minimum(n, last), 0))
    return pl.pallas_call(
        body, name=name,
        out_shape=(jax.ShapeDtypeStruct((Tn, D), BF16), jax.ShapeDtypeStruct((Tn, 2 * KW), BF16), jax.ShapeDtypeStruct((1, NQ), F32)),
        grid=(nb + 1,),
        in_specs=_attn_specs(nb, last) + [row, row, pl.BlockSpec(memory_space=pltpu.SMEM)],
        out_specs=(row, pl.BlockSpec((WIN, 2 * KW), lambda n: (jnp.maximum(n - 1, 0), 0)), pl.BlockSpec((1, NQ), lambda n: (0, 0))),
        scratch_shapes=[pltpu.VMEM((WIN, 2 * KW), F32), pltpu.VMEM((WIN, 2 * KW), F32)],
        compiler_params=_cp("arbitrary"),
    )(proj, proj, proj, proj, proj, o, do, sinks)


def _tri(lower):
    r = lax.broadcasted_iota(jnp.int32, (GC, GC), 0)
    c = lax.broadcasted_iota(jnp.int32, (GC, GC), 1)
    return r >= c if lower else r <= c


def _per_head(a):
    return jnp.stack([a[:, h * DK:(h + 1) * DK] for h in range(GH)])


def _all_heads(a):
    return jnp.concatenate([a[h] for h in range(GH)], axis=1)


def _gla_gates(lr, w2_ref, gb_ref):
    logit = _dot(lr, w2_ref[...].astype(BF16)) + gb_ref[...]
    la = (jnp.minimum(logit, 0.0) - jnp.log(1.0 + jnp.exp(-jnp.abs(logit)))) * (1.0 / 16.0)
    g = _dot(_tri(True).astype(F32), la, prec=HIGHEST)
    return logit, g


def _bmm(spec, a, b):
    return jnp.einsum(spec, a, b, preferred_element_type=F32)


def _gla_specs(nc, rev):
    idx = (lambda n: nc - 1 - n) if rev else (lambda n: n)
    half = 2 * DK
    return (
        [pl.BlockSpec((GC, half), lambda n, j=j: (idx(n), C_GQ // half + j)) for j in range(2)]
        + [pl.BlockSpec((GC, half), lambda n, j=j: (idx(n), C_GK // half + j)) for j in range(2)]
        + [pl.BlockSpec((GC, DV), lambda n, h=h: (idx(n), C_GV // DV + h)) for h in range(GH)]
        + [pl.BlockSpec((GC, LANE), lambda n: (idx(n), 0)), pl.BlockSpec((LANE, GH * DK), lambda n: (0, 0)),
           pl.BlockSpec((1, GH * DK), lambda n: (0, 0))])


def _gla_heads(refs):
    return (lambda h: refs[h // 2][:, (h % 2) * DK:(h % 2 + 1) * DK], lambda h: refs[2 + h // 2][:, (h % 2) * DK:(h % 2 + 1) * DK],
            lambda h: refs[4 + h][...])


def _gla_fwd(proj, plr, w2p, gb, name="gla_fwd"):
    Tn = proj.shape[0]
    nc = Tn // GC

    def body(*refs):
        qh, kh, vh = _gla_heads(refs)
        lr_ref, w2_ref, gb_ref, o_ref, st_ref, S = refs[8:]

        @pl.when(pl.program_id(0) == 0)
        def _():
            S[...] = jnp.zeros_like(S)

        heads = lambda f: jnp.stack([f(h) for h in range(GH)])
        _, g_all = _gla_gates(lr_ref[...].astype(BF16), w2_ref, gb_ref)
        g = _per_head(g_all)
        gl = g[:, GC - 1:GC, :]
        k = heads(kh)
        v = heads(vh).astype(BF16)
        qd = (heads(qh) * (DK ** -0.5) * jnp.exp(g)).astype(BF16)
        ki = (k * jnp.exp(-g)).astype(BF16)
        ke = (k * jnp.exp(gl - g)).astype(BF16)
        att = jnp.where(_tri(True)[None], _bmm('hid,hjd->hij', qd, ki), 0.0).astype(BF16)
        sp = S[...]
        st_ref[0] = sp
        o = _bmm('hij,hjv->hiv', att, v) + _bmm('hid,hvd->hiv', qd, sp.astype(BF16))
        for h in range(GH):
            o_ref[:, h * DV:(h + 1) * DV] = o[h]
        S[...] = sp * jnp.exp(gl) + _bmm('hjv,hjd->hvd', v, ke)

    return pl.pallas_call(
        body, name=name,
        out_shape=(jax.ShapeDtypeStruct((Tn, GH * DV), F32), jax.ShapeDtypeStruct((nc, GH, DV, DK), F32)),
        grid=(nc,), in_specs=_gla_specs(nc, False),
        out_specs=(pl.BlockSpec((GC, GH * DV), lambda n: (n, 0)), pl.BlockSpec((1, GH, DV, DK), lambda n: (n, 0, 0, 0))),
        scratch_shapes=[pltpu.VMEM((GH, DV, DK), F32)], compiler_params=_cp("arbitrary"),
    )(*([proj] * 8), plr, w2p, gb)


def _gla_bwd(proj, plr, w2p, gb, states, do, name="gla_bwd"):
    Tn = proj.shape[0]
    nc = Tn // GC

    def body(*refs):
        qh, kh, vh = _gla_heads(refs)
        lr_ref, w2_ref, gb_ref, st_ref, do_ref, dqk_ref, dv_ref, dlr_ref, dw2_ref, dgb_ref, dS = refs[8:]

        @pl.when(pl.program_id(0) == 0)
        def _():
            dS[...] = jnp.zeros_like(dS)
            dw2_ref[...] = jnp.zeros_like(dw2_ref)
            dgb_ref[...] = jnp.zeros_like(dgb_ref)

        heads = lambda f: jnp.stack([f(h) for h in range(GH)])
        lr = lr_ref[...].astype(BF16)
        causal = _tri(True)[None]
        last_row = lax.broadcasted_iota(jnp.int32, (GH, GC, DK), 1) == GC - 1
        logit, g_all = _gla_gates(lr, w2_ref, gb_ref)
        g = _per_head(g_all)
        gl = g[:, GC - 1:GC, :]
        egl = jnp.exp(gl)
        eg, eng, ege = jnp.exp(g), jnp.exp(-g), jnp.exp(gl - g)
        k = heads(kh)
        v = heads(vh).astype(BF16)
        dob = heads(lambda h: do_ref[:, h * DV:(h + 1) * DV]).astype(BF16)
        qd = heads(qh) * (DK ** -0.5) * eg
        ki = k * eng
        ke = k * ege
        qdb, kib, keb = qd.astype(BF16), ki.astype(BF16), ke.astype(BF16)
        att = jnp.where(causal, _bmm('hid,hjd->hij', qdb, kib), 0.0).astype(BF16)
        datt = jnp.where(causal, _bmm('hiv,hjv->hij', dob, v), 0.0).astype(BF16)
        sp = st_ref[0]
        dsn = dS[...]
        dsnb = dsn.astype(BF16)
        dv = (_bmm('hij,hiv->hjv', att, dob) + _bmm('hjd,hvd->hjv', keb, dsnb)).astype(BF16)
        dqd = _bmm('hij,hjd->hid', datt, kib) + _bmm('hiv,hvd->hid', dob, sp.astype(BF16))
        dki = _bmm('hij,hid->hjd', datt, qdb)
        dke = _bmm('hjv,hvd->hjd', v, dsnb)
        ddec = jnp.sum(dsn * sp, axis=1, keepdims=True)
        dS[...] = dsn * egl + _bmm('hiv,hid->hvd', dob, qdb)
        dke_ke = dke * ke
        dgl = jnp.sum(dke_ke, axis=1, keepdims=True) + ddec * egl
        dg = dqd * qd - dki * ki - dke_ke + jnp.where(last_row, dgl, 0.0)
        dq = (dqd * ((DK ** -0.5) * eg)).astype(BF16)
        dk = (dki * eng + dke * ege).astype(BF16)
        for h in range(GH):
            dv_ref[:, h * DV:(h + 1) * DV] = dv[h]
            dqk_ref[:, h * DK:(h + 1) * DK] = dq[h]
            dqk_ref[:, GH * DK + h * DK:GH * DK + (h + 1) * DK] = dk[h]
        dla = _dot(_tri(False).astype(F32), _all_heads(dg), prec=HIGHEST)
        dlogit = dla * (1.0 / 16.0) * _sigmoid(-logit)
        dlb = dlogit.astype(BF16)
        dlr_ref[...] = _dot(dlb, w2_ref[...].astype(BF16), tb=True).astype(BF16)
        dw2_ref[...] += _dot(lr, dlb, ta=True)
        dgb_ref[...] += jnp.sum(dlogit, axis=0, keepdims=True)

    rev = lambda n: nc - 1 - n
    row = pl.BlockSpec((GC, GH * DV), lambda n: (rev(n), 0))
    return pl.pallas_call(
        body, name=name,
        out_shape=(jax.ShapeDtypeStruct((Tn, 2 * GH * DK), BF16), jax.ShapeDtypeStruct((Tn, GH * DV), BF16),
                   jax.ShapeDtypeStruct((Tn, LANE), BF16), jax.ShapeDtypeStruct((LANE, GH * DK), F32),
                   jax.ShapeDtypeStruct((1, GH * DK), F32)),
        grid=(nc,),
        in_specs=_gla_specs(nc, True) + [pl.BlockSpec((1, GH, DV, DK), lambda n: (rev(n), 0, 0, 0)), row],
        out_specs=(row, row, pl.BlockSpec((GC, LANE), lambda n: (rev(n), 0)), pl.BlockSpec((LANE, GH * DK), lambda n: (0, 0)),
                   pl.BlockSpec((1, GH * DK), lambda n: (0, 0))),
        scratch_shapes=[pltpu.VMEM((GH, DV, DK), F32)], compiler_params=_cp("arbitrary"),
    )(*([proj] * 8), plr, w2p, gb, states, do)


def _merge_specs(tm):
    row = pl.BlockSpec((tm, D), lambda i: (i, 0))
    gates = [pl.BlockSpec((tm, DV), lambda i, j=c // DV + h: (i, j)) for c in (C_GR, C_GA, C_GB) for h in range(GH)]
    return row, gates, pl.BlockSpec((1, DV), lambda i: (0, 0))


def _merge_fwd(a, go, proj, gnw, name="merge_fwd", tm=256):
    Tn = a.shape[0]

    def body(a_ref, go_ref, *rest):
        gates, w_ref, m_ref = rest[:3 * GH], rest[3 * GH], rest[3 * GH + 1]
        for h in range(GH):
            sl = slice(h * DV, (h + 1) * DV)
            gov = go_ref[:, sl]
            r = lax.rsqrt(jnp.mean(gov * gov, axis=1, keepdims=True) + EPS)
            gr = gates[h][...]
            g2 = gov * r * w_ref[...] * (gr * _sigmoid(gr))
            m_ref[:, sl] = (_sigmoid(gates[GH + h][...]) * a_ref[:, sl] + _sigmoid(gates[2 * GH + h][...]) * g2).astype(BF16)

    row, gates, vec = _merge_specs(tm)
    return pl.pallas_call(
        body, name=name, out_shape=jax.ShapeDtypeStruct((Tn, D), BF16), grid=(Tn // tm,),
        in_specs=[row, row] + gates + [vec], out_specs=row, compiler_params=_cp("parallel"),
    )(a, go, *([proj] * (3 * GH)), gnw)


def _merge_bwd(dm, a, go, proj, gnw, name="merge_bwd", tm=256):
    Tn = a.shape[0]

    def body(dm_ref, a_ref, go_ref, *rest):
        gates = rest[:3 * GH]
        w_ref, da_ref, dgo_ref, dg_ref, dw_ref = rest[3 * GH:]
        wv = w_ref[...]
        dw = jnp.zeros((1, DV), F32)
        for h in range(GH):
            sl = slice(h * DV, (h + 1) * DV)
            dmv, av, gov, gr = dm_ref[:, sl], a_ref[:, sl], go_ref[:, sl], gates[h][...]
            sa, sb, sg = _sigmoid(gates[GH + h][...]), _sigmoid(gates[2 * GH + h][...]), _sigmoid(gr)
            r = lax.rsqrt(jnp.mean(gov * gov, axis=1, keepdims=True) + EPS)
            gn0 = gov * r
            gn = gn0 * wv
            silu = gr * sg
            dg2 = dmv * sb
            da_ref[:, sl] = dmv * sa
            dg_ref[:, D + h * DV:D + (h + 1) * DV] = (dmv * av * sa * (1.0 - sa)).astype(BF16)
            dg_ref[:, 2 * D + h * DV:2 * D + (h + 1) * DV] = (dg2 * gn * silu * (1.0 - sb)).astype(BF16)
            dg_ref[:, sl] = (dg2 * gn * (sg * (1.0 + gr * (1.0 - sg)))).astype(BF16)
            dgn = dg2 * silu
            dw = dw + jnp.sum(dgn * gn0, axis=0, keepdims=True)
            gg = dgn * wv
            dgo_ref[:, sl] = r * gg - gov * (r * r * r * jnp.mean(gg * gov, axis=1, keepdims=True))

        @pl.when(pl.program_id(0) == 0)
        def _():
            dw_ref[...] = dw

        @pl.when(pl.program_id(0) > 0)
        def _():
            dw_ref[...] += dw

    row, gates, vec = _merge_specs(tm)
    return pl.pallas_call(
        body, name=name,
        out_shape=(jax.ShapeDtypeStruct((Tn, D), F32), jax.ShapeDtypeStruct((Tn, D), F32), jax.ShapeDtypeStruct((Tn, 3 * D), BF16),
                   jax.ShapeDtypeStruct((1, DV), F32)),
        grid=(Tn // tm,), in_specs=[row, row, row] + gates + [vec],
        out_specs=(row, row, pl.BlockSpec((tm, 3 * D), lambda i: (i, 0)), vec), compiler_params=_cp("arbitrary"),
    )(dm, a, go, *([proj] * (3 * GH)), gnw)


def _ffn_up(v2, wgt, wut, name="ffn_up", tm=1024, tn=512):
    Tn = v2.shape[0]
    tm = min(tm, Tn)

    def body(v_ref, wg_ref, wu_ref, a_ref, b_ref, ff_ref):
        vv = v_ref[...]
        a = _dot(vv, wg_ref[...], tb=True)
        b = _dot(vv, wu_ref[...], tb=True)
        a_ref[...] = a.astype(BF16)
        b_ref[...] = b.astype(BF16)
        ff_ref[...] = (a * _sigmoid(a) * b).astype(BF16)

    w = pl.BlockSpec((tn, D), lambda j, i: (j, 0))
    act = pl.BlockSpec((tm, tn), lambda j, i: (i, j))
    return pl.pallas_call(
        body, name=name,
        out_shape=(jax.ShapeDtypeStruct((Tn, FH), BF16), jax.ShapeDtypeStruct((Tn, FH), BF16), jax.ShapeDtypeStruct((Tn, FH), BF16)),
        grid=(FH // tn, Tn // tm), in_specs=[pl.BlockSpec((tm, D), lambda j, i: (i, 0)), w, w], out_specs=(act, act, act),
        compiler_params=_cp("parallel", "parallel"),
    )(v2, wgt, wut)


def _ffn_dact(dh2b, wd, a, b, name="ffn_dact", tm=1024, tn=512):
    Tn = dh2b.shape[0]
    tm = min(tm, Tn)

    def body(d_ref, w_ref, a_ref, b_ref, da_ref, db_ref):
        dff = _dot(d_ref[...], w_ref[...], tb=True)
        av = a_ref[...].astype(F32)
        sg = _sigmoid(av)
        da_ref[...] = (dff * b_ref[...].astype(F32) * (sg * (1.0 + av * (1.0 - sg)))).astype(BF16)
        db_ref[...] = (dff * (av * sg)).astype(BF16)

    act = pl.BlockSpec((tm, tn), lambda j, i: (i, j))
    return pl.pallas_call(
        body, name=name,
        out_shape=(jax.ShapeDtypeStruct((Tn, FH), BF16), jax.ShapeDtypeStruct((Tn, FH), BF16)),
        grid=(FH // tn, Tn // tm),
        in_specs=[pl.BlockSpec((tm, D), lambda j, i: (i, 0)), pl.BlockSpec((tn, D), lambda j, i: (j, 0)), act, act],
        out_specs=(act, act), compiler_params=_cp("parallel", "parallel"),
    )(dh2b, wd, a, b)


def _adam_math(w, g, m, v):
    m2 = B1 * m + (1.0 - B1) * g
    v2 = B2 * v + (1.0 - B2) * (g * g)
    mh = m2 / (1.0 - B1 ** STEP)
    vh = v2 / (1.0 - B2 ** STEP)
    return -LR * (mh / (jnp.sqrt(vh) + AEPS) + WD * w), m2, v2


def _sum_blocks(o_ref, p_ref):
    g = o_ref[...].astype(F32)
    for j in range(p_ref.shape[0]):
        g = g + p_ref[j].astype(F32)
    return g


def _adamw(w, m, v, psums, parts, chip_idx, name, tr):
    R, C = w.shape

    def body(s_ref, w_ref, m_ref, v_ref, o_ref, p_ref, g_ref, d_ref, m2_ref, v2_ref):
        g = _sum_blocks(o_ref, p_ref)
        d, m2, v2 = _adam_math(w_ref[...], g, m_ref[...], v_ref[...])
        g_ref[...] = g
        d_ref[...] = d
        m2_ref[...] = m2
        v2_ref[...] = v2

    blk = pl.BlockSpec((tr, C), lambda i, s: (i, 0))
    out = jax.ShapeDtypeStruct((R, C), F32)
    grid_spec = pltpu.PrefetchScalarGridSpec(
        num_scalar_prefetch=1, grid=(R // tr,),
        in_specs=[blk, blk, blk, pl.BlockSpec((None, tr, C), lambda i, s: (s[0], i, 0)),
                  pl.BlockSpec((parts.shape[0], tr, C), lambda i, s: (0, i, 0))],
        out_specs=(blk, blk, blk, blk),
    )
    return pl.pallas_call(body, name=name, out_shape=(out, out, out, out), grid_spec=grid_spec, compiler_params=_cp("parallel"),
                          )(chip_idx, w, m, v, psums, parts)


def _adamw_given(w, m, v, g, name, tr, tc):
    R, C = w.shape

    def body(w_ref, m_ref, v_ref, g_ref, d_ref, m2_ref, v2_ref):
        d, m2, v2 = _adam_math(w_ref[...], g_ref[...], m_ref[...], v_ref[...])
        d_ref[...] = d
        m2_ref[...] = m2
        v2_ref[...] = v2

    blk = pl.BlockSpec((tr, tc), lambda i, j: (i, j))
    out = jax.ShapeDtypeStruct(w.shape, F32)
    return pl.pallas_call(body, name=name, out_shape=(out, out, out), grid=(pl.cdiv(R, tr), C // tc), in_specs=[blk] * 4,
                          out_specs=(blk, blk, blk), compiler_params=_cp("parallel", "parallel"))(w, m, v, g)


def _sum_parts(psums, parts, chip_idx, name, tr, tc):
    _, R, C = psums.shape

    def body(s_ref, o_ref, p_ref, g_ref):
        g_ref[...] = _sum_blocks(o_ref, p_ref)

    grid_spec = pltpu.PrefetchScalarGridSpec(
        num_scalar_prefetch=1, grid=(R // tr, C // tc),
        in_specs=[pl.BlockSpec((None, tr, tc), lambda i, j, s: (s[0], i, j)),
                  pl.BlockSpec((parts.shape[0], tr, tc), lambda i, j, s: (0, i, j))],
        out_specs=pl.BlockSpec((tr, tc), lambda i, j, s: (i, j)),
    )
    return pl.pallas_call(body, name=name, out_shape=jax.ShapeDtypeStruct((R, C), F32), grid_spec=grid_spec,
                          compiler_params=_cp("parallel", "parallel"))(chip_idx, psums, parts)


def _adamw_plain(w, m, v, g, name):
    def body(w_ref, m_ref, v_ref, g_ref, d_ref, m2_ref, v2_ref):
        d, m2, v2 = _adam_math(w_ref[...], g_ref[...], m_ref[...], v_ref[...])
        d_ref[...] = d
        m2_ref[...] = m2
        v2_ref[...] = v2

    out = jax.ShapeDtypeStruct(w.shape, F32)
    return pl.pallas_call(body, name=name, out_shape=(out, out, out))(w, m, v, g)


def _sum_devices(pack_all, name="sum_small"):
    def body(p_ref, o_ref):
        s = p_ref[0]
        for k in range(1, NDEV):
            s = s + p_ref[k]
        o_ref[...] = s

    return pl.pallas_call(body, name=name, out_shape=jax.ShapeDtypeStruct(pack_all.shape[1:], F32))(pack_all)


def _pair_add(g5, recv, c_idx, name, tr):
    _, _, R, C = g5.shape

    def body(c_ref, g_ref, r_ref, o_ref):
        o_ref[...] = (g_ref[...].astype(F32) + r_ref[...].astype(F32)).astype(BF16)

    grid_spec = pltpu.PrefetchScalarGridSpec(
        num_scalar_prefetch=1, grid=(4, R // tr),
        in_specs=[pl.BlockSpec((None, None, tr, C), lambda q, i, c: (q, c[0], i, 0)), pl.BlockSpec((None, tr, C), lambda q, i, c: (q, i, 0))],
        out_specs=pl.BlockSpec((None, tr, C), lambda q, i, c: (q, i, 0)),
    )
    return pl.pallas_call(
        body, name=name, out_shape=jax.ShapeDtypeStruct((4, R, C), BF16), grid_spec=grid_spec,
        compiler_params=_cp("parallel", "parallel"),
    )(c_idx, g5, recv)


_ANY = pl.BlockSpec(memory_space=pl.ANY)


def _mesh_pos():
    x, y, c = lax.axis_index("x"), lax.axis_index("y"), lax.axis_index("c")
    return x, y, c, [(1 - x, y), (x, 1 - y), (1 - x, 1 - y)]


def _gather_small(pack, name="gather_small"):
    def body(pk, pk_all, psend, precv, loc):
        x, y, c, chips = _mesh_pos()
        me_slot = 4 * x + 2 * y + c
        sib = (x, y, 1 - c)
        own = pltpu.make_async_copy(pk, pk_all.at[me_slot], loc)
        own.start()
        peers = [sib] + [(*chip, c) for chip in chips] + [(*chip, 1 - c) for chip in chips]
        small = [pltpu.make_async_remote_copy(src_ref=pk, dst_ref=pk_all.at[me_slot], send_sem=psend.at[k], recv_sem=precv.at[k],
                                              device_id=p, device_id_type=MESH) for k, p in enumerate(peers)]
        for d in small:
            d.start()
        for k, p in enumerate(peers):
            pltpu.make_async_remote_copy(src_ref=pk, dst_ref=pk_all.at[4 * p[0] + 2 * p[1] + p[2]], send_sem=psend.at[k],
                                         recv_sem=precv.at[k], device_id=p, device_id_type=MESH).wait_recv()
        for d in small:
            d.wait_send()
        own.wait()

    return pl.pallas_call(
        body, name=name, out_shape=jax.ShapeDtypeStruct((NDEV,) + pack.shape, pack.dtype), in_specs=[_ANY], out_specs=_ANY,
        scratch_shapes=[pltpu.SemaphoreType.DMA((7,)), pltpu.SemaphoreType.DMA((7,)), pltpu.SemaphoreType.DMA(())],
    )(pack)


def _main_row(g):
    return g if g < C_LR else g - RANK


def _window_pieces(lo, hi):
    out = []
    for a, b, where in ((lo, min(hi, C_LR), "main"), (max(lo, C_LR), min(hi, C_LR + RANK), "lr"), (max(lo, C_LR + RANK), hi, "main")):
        if a < b:
            out.append((a, b, where, _main_row(a) if where == "main" else a - C_LR))
    return out


def _assemble_w_in(windows, own, name="assemble_w_in"):
    edges = NDEV - 1

    def body(b_ref, own_ref, main_ref, lr_ref, buf, ebuf, in_sems, out_sems, esems):
        dev = 4 * lax.axis_index("x") + 2 * lax.axis_index("y") + lax.axis_index("c")

        def load(k):
            return pltpu.make_async_copy(b_ref.at[k], buf.at[k % 2], in_sems.at[k % 2])

        def start_load(k):
            pl.when(dev == k)(pltpu.make_async_copy(own_ref, buf.at[k % 2], in_sems.at[k % 2]).start)
            pl.when(dev != k)(load(k).start)

        lr_ref[RANK:, :] = jnp.zeros((LANE - RANK, D), BF16)
        start_load(0)
        pending, edge_out = [], []
        for k in range(NDEV):
            s = k % 2
            load(k).wait()
            if k:
                ebuf[k - 1] = buf[1 - s, WSTEP:WWIN, :] + buf[s, 0:16, :]
                edge_out.append(pltpu.make_async_copy(ebuf.at[k - 1], main_ref.at[pl.ds(_main_row(WSTEP * k), 16)], esems.at[k - 1]))
                edge_out[-1].start()
                for d in pending:
                    d.wait()
            if k + 1 < NDEV:
                start_load(k + 1)
            pending = []
            lo = WSTEP * k + (16 if k else 0)
            hi = WSTEP * k + (WWIN if k == NDEV - 1 else WSTEP)
            for a, b, where, dst in _window_pieces(lo, hi):
                if where == "lr":
                    lr_ref[dst:dst + b - a, :] = buf[s, a - WSTEP * k:b - WSTEP * k, :]
                else:
                    pending.append(pltpu.make_async_copy(buf.at[s, pl.ds(a - WSTEP * k, b - a)], main_ref.at[pl.ds(dst, b - a)],
                                                         out_sems.at[2 * s + len(pending)]))
                    pending[-1].start()
        for d in pending + edge_out:
            d.wait()

    return pl.pallas_call(
        body, name=name,
        out_shape=(jax.ShapeDtypeStruct((NMAIN, D), BF16), jax.ShapeDtypeStruct((LANE, D), BF16)),
        in_specs=[_ANY, _ANY], out_specs=(_ANY, pl.BlockSpec(memory_space=pltpu.VMEM)),
        scratch_shapes=[pltpu.VMEM((2, WWIN, D), BF16), pltpu.VMEM((edges, 16, D), BF16), pltpu.SemaphoreType.DMA((2,)),
                        pltpu.SemaphoreType.DMA((4,)), pltpu.SemaphoreType.DMA((edges,))],
        compiler_params=pltpu.CompilerParams(vmem_limit_bytes=VMEM_LIMIT),
    )(windows, own)


def _disassemble_exchange(d_main, d_lr, name="disassemble_exchange"):
    def body(main_ref, lr_ref, mine_ref, recv_ref, buf, in_sems, keep_sems, send_sems, recv_sems):
        x, y, c, _ = _mesh_pos()
        sib = (x, y, 1 - c)

        def loads(k):
            s, out = k % 2, []
            for a, b, where, src0 in _window_pieces(WSTEP * k, WSTEP * k + WWIN):
                if where == "main":
                    out.append(pltpu.make_async_copy(main_ref.at[pl.ds(src0, b - a)], buf.at[s, pl.ds(a - WSTEP * k, b - a)],
                                                     in_sems.at[2 * s + len(out)]))
            return out

        def keep(k):
            return pltpu.make_async_copy(buf.at[k % 2], mine_ref.at[k // 2], keep_sems.at[k % 2])

        def send(k):
            return _rcopy(buf.at[k % 2], recv_ref.at[k // 2], send_sems.at[k % 2], recv_sems.at[k // 2], sib)

        def store_start(k):
            pl.when(c == k % 2)(keep(k).start)
            pl.when(c != k % 2)(send(k).start)

        def store_wait(k):
            pl.when(c == k % 2)(keep(k).wait)
            pl.when(c != k % 2)(send(k).wait_send)

        for d in loads(0):
            d.start()
        for k in range(NDEV):
            for d in loads(k):
                d.wait()
            for a, b, where, src0 in _window_pieces(WSTEP * k, WSTEP * k + WWIN):
                if where == "lr":
                    buf[k % 2, a - WSTEP * k:b - WSTEP * k, :] = lr_ref[src0:src0 + b - a, :]
            if k:
                store_wait(k - 1)
            if k + 1 < NDEV:
                for d in loads(k + 1):
                    d.start()
            store_start(k)
        store_wait(NDEV - 1)
        for chip in range(NDEV // 2):
            _rcopy(buf.at[0], recv_ref.at[chip], send_sems.at[0], recv_sems.at[chip], sib).wait_recv()

    half = jax.ShapeDtypeStruct((NDEV // 2, WWIN, D), BF16)
    return pl.pallas_call(
        body, name=name, out_shape=(half, half),
        in_specs=[_ANY, pl.BlockSpec(memory_space=pltpu.VMEM)], out_specs=(_ANY, _ANY),
        scratch_shapes=[pltpu.VMEM((2, WWIN, D), BF16), pltpu.SemaphoreType.DMA((4,)), pltpu.SemaphoreType.DMA((2,)),
                        pltpu.SemaphoreType.DMA((2,)), pltpu.SemaphoreType.DMA((NDEV // 2,))],
        compiler_params=pltpu.CompilerParams(vmem_limit_bytes=VMEM_LIMIT),
    )(d_main, d_lr)


def _add_blocks(a, b, name, tr):
    _, R, C = a.shape

    def body(a_ref, b_ref, o_ref):
        o_ref[...] = (a_ref[...].astype(F32) + b_ref[...].astype(F32)).astype(BF16)

    blk = pl.BlockSpec((None, tr, C), lambda q, i: (q, i, 0))
    return pl.pallas_call(body, name=name, out_shape=jax.ShapeDtypeStruct(a.shape, BF16), grid=(a.shape[0], R // tr),
                          in_specs=[blk, blk], out_specs=blk, compiler_params=_cp("parallel", "parallel"))(a, b)


_HBM = pl.BlockSpec(memory_space=pltpu.HBM)
_SEM = pl.BlockSpec(memory_space=pltpu.SEMAPHORE)
_VMEM = pl.BlockSpec(memory_space=pltpu.VMEM)
_SIDE = pltpu.CompilerParams(has_side_effects=pltpu.SideEffectType.DATAFLOW_SIDE_EFFECTING)
_TOKEN = jax.ShapeDtypeStruct((8, LANE), F32)


def _hbm(a):
    return pltpu.with_memory_space_constraint(a, pltpu.HBM)


def _hbm_like(arrs):
    return tuple(pltpu.HBM(a.shape, a.dtype) for a in arrs)


def _tie(x, token):
    return x + token[0, 0].astype(x.dtype)


def _chip_copies(ins, lands, send, recv, nrel):
    x, y, c, chips = _mesh_pos()
    first = [sum(nrel[:a]) for a in range(len(ins))]
    return [pltpu.make_async_remote_copy(src_ref=ins[a].at[2 * chip[0] + chip[1]], dst_ref=lands[a].at[j], send_sem=send.at[first[a] + j],
                                         recv_sem=recv.at[first[a] + j], device_id=(*chip, c), device_id_type=MESH)
            for a in range(len(ins)) for j, chip in enumerate(chips[:nrel[a]])]


def _chip_start(psums, name, nrel=None):
    n = len(psums)
    nrel = nrel or [3] * n
    lands = [lax.empty((r,) + p.shape[1:], p.dtype) for r, p in zip(nrel, psums)]

    def body(*refs):
        for d in _chip_copies(refs[:n], refs[n:2 * n], refs[2 * n], refs[2 * n + 1], nrel):
            d.start()
        refs[-1][...] = jnp.zeros_like(refs[-1])

    sems = pltpu.SemaphoreType.DMA((sum(nrel),))
    out = pl.pallas_call(
        body, name=name, out_shape=(sems, sems) + _hbm_like(psums) + _hbm_like(lands) + (_TOKEN,),
        in_specs=[_HBM] * (2 * n), out_specs=(_SEM, _SEM) + (_HBM,) * (2 * n) + (_VMEM,),
        input_output_aliases={i: 2 + i for i in range(2 * n)}, compiler_params=_SIDE,
    )(*[_hbm(a) for a in list(psums) + lands])
    return out[0], out[1], list(out[2:2 + n]), list(out[2 + n:2 + 2 * n]), out[-1]


def _chip_wait(send, recv, psums, lands, after, name):
    n = len(psums)
    nrel = [l.shape[0] for l in lands]

    def body(*refs):
        for d in _chip_copies(refs[:n], refs[n:2 * n], refs[2 * n], refs[2 * n + 1], nrel):
            d.wait_send()
            d.wait_recv()

    out = pl.pallas_call(
        body, name=name, out_shape=_hbm_like(psums) + _hbm_like(lands),
        in_specs=[_HBM] * (2 * n) + [_SEM, _SEM, _ANY], out_specs=(_HBM,) * (2 * n),
        input_output_aliases={i: i for i in range(2 * n)}, compiler_params=_SIDE,
    )(*psums, *lands, send, recv, after)
    return list(out[:n]), list(out[n:])


def _hop_pos():
    x, y, c, _ = _mesh_pos()
    north = c == 1
    via = (jnp.where(north, 1 - x, x), jnp.where(north, y, 1 - y))
    return (*via, c), 2 * (1 - x) + (1 - y), jnp.where(north, 2 * x + (1 - y), 2 * (1 - x) + y)


def _hop_copies(ins, lands, send, recv):
    to, mine, _ = _hop_pos()
    return [pltpu.make_async_remote_copy(src_ref=ins[a].at[mine], dst_ref=lands[a], send_sem=send.at[a], recv_sem=recv.at[a],
                                         device_id=to, device_id_type=MESH) for a in range(len(ins))]


def _hop_start(psums, name):
    n = len(psums)
    lands = [lax.empty(p.shape[1:], p.dtype) for p in psums]

    def body(*refs):
        for d in _hop_copies(refs[:n], refs[n:2 * n], refs[2 * n], refs[2 * n + 1]):
            d.start()
        refs[-1][...] = jnp.zeros_like(refs[-1])

    sems = pltpu.SemaphoreType.DMA((n,))
    out = pl.pallas_call(
        body, name=name, out_shape=(sems, sems) + _hbm_like(psums) + _hbm_like(lands) + (_TOKEN,),
        in_specs=[_HBM] * (2 * n), out_specs=(_SEM, _SEM) + (_HBM,) * (2 * n) + (_VMEM,),
        input_output_aliases={i: 2 + i for i in range(2 * n)}, compiler_params=_SIDE,
    )(*[_hbm(a) for a in list(psums) + lands])
    return out[0], out[1], list(out[2:2 + n]), list(out[2 + n:2 + 2 * n]), out[-1]


def _hop_wait(send, recv, psums, lands, after, name):
    n = len(psums)

    def body(*refs):
        for d in _hop_copies(refs[:n], refs[n:2 * n], refs[2 * n], refs[2 * n + 1]):
            d.wait_send()
            d.wait_recv()

    out = pl.pallas_call(
        body, name=name, out_shape=_hbm_like(psums) + _hbm_like(lands),
        in_specs=[_HBM] * (2 * n) + [_SEM, _SEM, _ANY], out_specs=(_HBM,) * (2 * n),
        input_output_aliases={i: i for i in range(2 * n)}, compiler_params=_SIDE,
    )(*psums, *lands, send, recv, after)
    return list(out[:n]), list(out[n:])


def _hop_add(psums, land, idx, name, tr):
    _, R, C = psums.shape

    def body(s_ref, p_ref, l_ref, o_ref):
        o_ref[...] = (p_ref[...].astype(F32) + l_ref[...].astype(F32)).astype(BF16)

    blk = pl.BlockSpec((None, tr, C), lambda i, s: (s[0], i, 0))
    grid_spec = pltpu.PrefetchScalarGridSpec(num_scalar_prefetch=1, grid=(R // tr,),
                                             in_specs=[blk, pl.BlockSpec((tr, C), lambda i, s: (i, 0))], out_specs=blk)
    return pl.pallas_call(body, name=name, out_shape=jax.ShapeDtypeStruct(psums.shape, BF16), grid_spec=grid_spec,
                          input_output_aliases={1: 0}, compiler_params=_cp("parallel"))(idx, psums, land)


def _pair_copies(ins, lands, send, recv):
    x, y, c, _ = _mesh_pos()
    return [pltpu.make_async_remote_copy(src_ref=ins[a].at[:, 1 - c], dst_ref=lands[a], send_sem=send.at[a], recv_sem=recv.at[a],
                                         device_id=(x, y, 1 - c), device_id_type=MESH) for a in range(len(ins))]


def _pair_start(grads, name):
    n = len(grads)
    lands = [lax.empty((4,) + g.shape[2:], g.dtype) for g in grads]

    def body(*refs):
        for d in _pair_copies(refs[:n], refs[n:2 * n], refs[2 * n], refs[2 * n + 1]):
            d.start()
        refs[-1][...] = jnp.zeros_like(refs[-1])

    sems = pltpu.SemaphoreType.DMA((n,))
    out = pl.pallas_call(
        body, name=name, out_shape=(sems, sems) + _hbm_like(grads) + _hbm_like(lands) + (_TOKEN,),
        in_specs=[_HBM] * (2 * n), out_specs=(_SEM, _SEM) + (_HBM,) * (2 * n) + (_VMEM,),
        input_output_aliases={i: 2 + i for i in range(2 * n)}, compiler_params=_SIDE,
    )(*[_hbm(a) for a in list(grads) + lands])
    return out[0], out[1], list(out[2:2 + n]), list(out[2 + n:2 + 2 * n]), out[-1]


def _pair_wait(send, recv, grads, lands, after, name):
    n = len(grads)

    def body(*refs):
        for d in _pair_copies(refs[:n], refs[n:2 * n], refs[2 * n], refs[2 * n + 1]):
            d.wait_send()
            d.wait_recv()

    out = pl.pallas_call(
        body, name=name, out_shape=_hbm_like(grads) + _hbm_like(lands),
        in_specs=[_HBM] * (2 * n) + [_SEM, _SEM, _ANY], out_specs=(_HBM,) * (2 * n),
        input_output_aliases={i: i for i in range(2 * n)}, compiler_params=_SIDE,
    )(*grads, *lands, send, recv, after)
    return list(out[:n]), list(out[n:])


def _slot(chip, c):
    return 4 * chip[0] + 2 * chip[1] + c


def _gather_start(shards, lands, after, name):
    n = len(shards)

    def body(*refs):
        src, land, send, recv = refs[:n], refs[n:2 * n], refs[2 * n + 1], refs[2 * n + 2]
        x, y, c, chips = _mesh_pos()
        for a in range(n):
            for k, to in enumerate([(x, y, 1 - c)] + [(*chip, c) for chip in chips]):
                pltpu.make_async_remote_copy(src_ref=src[a], dst_ref=land[a].at[_slot((x, y), c)], send_sem=send.at[4 * a + k],
                                             recv_sem=recv.at[4 * a + k], device_id=to, device_id_type=MESH).start()
        refs[-1][...] = jnp.zeros_like(refs[-1])

    sems = pltpu.SemaphoreType.DMA((4 * n,))
    out = pl.pallas_call(
        body, name=name, out_shape=(sems, sems) + _hbm_like(shards) + _hbm_like(lands) + (_TOKEN,),
        in_specs=[_HBM] * (2 * n) + [_ANY], out_specs=(_SEM, _SEM) + (_HBM,) * (2 * n) + (_VMEM,),
        input_output_aliases={i: 2 + i for i in range(2 * n)}, compiler_params=_SIDE,
    )(*[_hbm(a) for a in list(shards) + list(lands)], after)
    return out[0], out[1], list(out[2:2 + n]), list(out[2 + n:2 + 2 * n]), out[-1]


def _gather_pass(lands, recv, after, name, first=0):
    n = len(lands)

    def body(*refs):
        land, recv1 = refs[:n], refs[n]
        send2, recv2 = refs[n + 2], refs[n + 3]
        x, y, c, chips = _mesh_pos()
        for a in range(n):
            for j, chip in enumerate(chips):
                blk = land[a].at[_slot(chip, c)]
                pltpu.make_async_remote_copy(src_ref=blk, dst_ref=blk, send_sem=send2.at[3 * a + j], recv_sem=recv1.at[4 * (first + a) + 1 + j],
                                             device_id=(*chip, c), device_id_type=MESH).wait_recv()
                pltpu.make_async_remote_copy(src_ref=blk, dst_ref=blk, send_sem=send2.at[3 * a + j], recv_sem=recv2.at[3 * a + j],
                                             device_id=(x, y, 1 - c), device_id_type=MESH).start()
        refs[-1][...] = jnp.zeros_like(refs[-1])

    sems = pltpu.SemaphoreType.DMA((3 * n,))
    out = pl.pallas_call(
        body, name=name, out_shape=(sems, sems) + _hbm_like(lands) + (_TOKEN,),
        in_specs=[_HBM] * n + [_SEM, _ANY], out_specs=(_SEM, _SEM) + (_HBM,) * n + (_VMEM,),
        input_output_aliases={i: 2 + i for i in range(n)}, compiler_params=_SIDE,
    )(*lands, recv, after)
    return out[0], out[1], list(out[2:2 + n]), out[-1]


def _gather_wait(shards, lands, send, recv, send2, recv2, after, name, first=0):
    n = len(lands)

    def body(*refs):
        src, land = refs[:n], refs[n:2 * n]
        send1, recv1, snd2, rcv2 = refs[2 * n:2 * n + 4]
        x, y, c, chips = _mesh_pos()
        sib = (x, y, 1 - c)
        for a in range(n):
            for k in range(4):
                pltpu.make_async_remote_copy(src_ref=src[a], dst_ref=land[a].at[_slot((x, y), c)], send_sem=send1.at[4 * (first + a) + k],
                                             recv_sem=recv1.at[4 * (first + a) + k], device_id=sib, device_id_type=MESH).wait_send()
            blk = land[a].at[_slot((x, y), 1 - c)]
            pltpu.make_async_remote_copy(src_ref=blk, dst_ref=blk, send_sem=send1.at[4 * (first + a)], recv_sem=recv1.at[4 * (first + a)],
                                         device_id=sib, device_id_type=MESH).wait_recv()
            for j, chip in enumerate(chips):
                mine, theirs = land[a].at[_slot(chip, c)], land[a].at[_slot(chip, 1 - c)]
                pltpu.make_async_remote_copy(src_ref=mine, dst_ref=mine, send_sem=snd2.at[3 * a + j], recv_sem=rcv2.at[3 * a + j],
                                             device_id=sib, device_id_type=MESH).wait_send()
                pltpu.make_async_remote_copy(src_ref=theirs, dst_ref=theirs, send_sem=snd2.at[3 * a + j], recv_sem=rcv2.at[3 * a + j],
                                             device_id=sib, device_id_type=MESH).wait_recv()

    out = pl.pallas_call(
        body, name=name, out_shape=_hbm_like(shards) + _hbm_like(lands),
        in_specs=[_HBM] * (2 * n) + [_SEM] * 4 + [_ANY], out_specs=(_HBM,) * (2 * n),
        input_output_aliases={i: i for i in range(2 * n)}, compiler_params=_SIDE,
    )(*shards, *lands, send, recv, send2, recv2, after)
    return list(out[n:])


def _win_tree():
    x, y, c, chips = _mesh_pos()
    north = c == 1
    handed = (jnp.where(north, 1 - x, x), jnp.where(north, y, 1 - y))
    hand_to = (jnp.where(north, x, 1 - x), jnp.where(north, 1 - y, y))
    return x, y, c, chips, handed, hand_to


WIN_PARTS = ((0, 800), (800, 816))


def _units(n):
    return [(0, p) for p in WIN_PARTS] + [(a, None) for a in range(1, n)]


def _part(ref, rows):
    return ref if rows is None else ref.at[pl.ds(*rows)]


def _blk(land, chip, c, rows=None):
    return _part(land.at[_slot(chip, c)], rows)


def _rcopy(src, dst, send, recv, to):
    return pltpu.make_async_remote_copy(src_ref=src, dst_ref=dst, send_sem=send, recv_sem=recv, device_id=to, device_id_type=MESH)


def _win_start(shards, lands, name):
    n = len(shards)

    def body(*refs):
        src, land, send, recv = refs[:n], refs[n:2 * n], refs[2 * n], refs[2 * n + 1]
        x, y, c, chips, _, _ = _win_tree()
        for u, (a, rows) in enumerate(_units(n)):
            for k, to in enumerate([(x, y, 1 - c), (*chips[0], c), (*chips[1], c)]):
                _rcopy(_part(src[a], rows), _blk(land[a], (x, y), c, rows), send.at[3 * u + k], recv.at[3 * u + k], to).start()
        refs[-1][...] = jnp.zeros_like(refs[-1])

    sems = pltpu.SemaphoreType.DMA((3 * len(_units(n)),))
    out = pl.pallas_call(
        body, name=name, out_shape=(sems, sems) + _hbm_like(shards) + _hbm_like(lands) + (_TOKEN,),
        in_specs=[_HBM] * (2 * n), out_specs=(_SEM, _SEM) + (_HBM,) * (2 * n) + (_VMEM,),
        input_output_aliases={i: 2 + i for i in range(2 * n)}, compiler_params=_SIDE,
    )(*[_hbm(a) for a in list(shards) + list(lands)])
    return out[0], out[1], list(out[2:2 + n]), list(out[2 + n:2 + 2 * n]), out[-1]


def _win_hand_on(lands, recv1, after, name):
    n, m = len(lands), len(after)

    def body(*refs):
        land, rcv1 = refs[:n], refs[n]
        send2, recv2 = refs[n + 1 + m], refs[n + 2 + m]
        x, y, c, chips, handed, hand_to = _win_tree()
        for u, (a, rows) in enumerate(_units(n)):
            for j in range(2):
                blk = _blk(land[a], chips[j], c, rows)
                _rcopy(blk, blk, send2.at[3 * u], rcv1.at[3 * u + 1 + j], (*chips[j], c)).wait_recv()
            blk = _blk(land[a], handed, c, rows)
            _rcopy(blk, blk, send2.at[3 * u], recv2.at[3 * u], (*hand_to, c)).start()
            for j in range(2):
                blk = _blk(land[a], chips[j], c, rows)
                _rcopy(blk, blk, send2.at[3 * u + 1 + j], recv2.at[3 * u + 1 + j], (x, y, 1 - c)).start()
        refs[-1][...] = jnp.zeros_like(refs[-1])

    sems = pltpu.SemaphoreType.DMA((3 * len(_units(n)),))
    out = pl.pallas_call(
        body, name=name, out_shape=(sems, sems) + _hbm_like(lands) + (_TOKEN,),
        in_specs=[_HBM] * n + [_SEM] + [_ANY] * m, out_specs=(_SEM, _SEM) + (_HBM,) * n + (_VMEM,),
        input_output_aliases={i: 2 + i for i in range(n)}, compiler_params=_SIDE,
    )(*lands, recv1, *after)
    return out[0], out[1], list(out[2:2 + n]), out[-1]


def _win_last(lands, recv2, after, name):
    n, m = len(lands), len(after)

    def body(*refs):
        land, rcv2 = refs[:n], refs[n]
        send3, recv3 = refs[n + 1 + m], refs[n + 2 + m]
        x, y, c, chips, _, hand_to = _win_tree()
        for u, (a, rows) in enumerate(_units(n)):
            blk = _blk(land[a], chips[2], c, rows)
            _rcopy(blk, blk, send3.at[u], rcv2.at[3 * u], (*hand_to, c)).wait_recv()
            _rcopy(blk, blk, send3.at[u], recv3.at[u], (x, y, 1 - c)).start()
        refs[-1][...] = jnp.zeros_like(refs[-1])

    sems = pltpu.SemaphoreType.DMA((len(_units(n)),))
    out = pl.pallas_call(
        body, name=name, out_shape=(sems, sems) + _hbm_like(lands) + (_TOKEN,),
        in_specs=[_HBM] * n + [_SEM] + [_ANY] * m, out_specs=(_SEM, _SEM) + (_HBM,) * n + (_VMEM,),
        input_output_aliases={i: 2 + i for i in range(n)}, compiler_params=_SIDE,
    )(*lands, recv2, *after)
    return out[0], out[1], list(out[2:2 + n]), out[-1]


def _win_wait(shards, lands, sems1, sems2, sems3, after, name):
    n = len(lands)

    def body(*refs):
        src, land = refs[:n], refs[n:2 * n]
        send1, recv1, send2, recv2, send3, recv3 = refs[2 * n:2 * n + 6]
        x, y, c, chips, handed, hand_to = _win_tree()
        sib = (x, y, 1 - c)
        for u, (a, rows) in enumerate(_units(n)):
            own = _blk(land[a], (x, y), c, rows)
            for k in range(3):
                _rcopy(_part(src[a], rows), own, send1.at[3 * u + k], recv1.at[3 * u + k], sib).wait_send()
            blk = _blk(land[a], (x, y), 1 - c, rows)
            _rcopy(blk, blk, send1.at[3 * u], recv1.at[3 * u], sib).wait_recv()
            blk = _blk(land[a], handed, c, rows)
            _rcopy(blk, blk, send2.at[3 * u], recv2.at[3 * u], sib).wait_send()
            for j in range(2):
                mine, theirs = _blk(land[a], chips[j], c, rows), _blk(land[a], chips[j], 1 - c, rows)
                _rcopy(mine, mine, send2.at[3 * u + 1 + j], recv2.at[3 * u + 1 + j], sib).wait_send()
                _rcopy(theirs, theirs, send2.at[3 * u + 1 + j], recv2.at[3 * u + 1 + j], sib).wait_recv()
            mine, theirs = _blk(land[a], chips[2], c, rows), _blk(land[a], chips[2], 1 - c, rows)
            _rcopy(mine, mine, send3.at[u], recv3.at[u], sib).wait_send()
            _rcopy(theirs, theirs, send3.at[u], recv3.at[u], sib).wait_recv()

    out = pl.pallas_call(
        body, name=name, out_shape=_hbm_like(shards) + _hbm_like(lands),
        in_specs=[_HBM] * (2 * n) + [_SEM] * 6 + [_ANY], out_specs=(_HBM,) * (2 * n),
        input_output_aliases={i: i for i in range(2 * n)}, compiler_params=_SIDE,
    )(*shards, *lands, *sems1, *sems2, *sems3, after)
    return list(out[n:])


def _pad_to(v, n):
    return jnp.pad(v, [(0, 0)] * (v.ndim - 1) + [(0, n - v.shape[-1])])


def _pack_small(n1, gb, sk, gn, n2, fn, extra=None):
    parts = [n1.reshape(-1), gb.reshape(-1), sk.reshape(-1), gn.reshape(-1), n2.reshape(-1), fn.reshape(-1)]
    flat = jnp.concatenate(parts + ([extra.reshape(-1)] if extra is not None else []))
    return _pad_to(flat, SMALL_N).reshape(SMALL_ROWS, LANE)


def _unpack_small(p):
    f = p.reshape(-1)
    return (f[S_N1:S_GB].reshape(1, D), f[S_GB:S_SK].reshape(1, GH * DK), f[S_SK:S_GN].reshape(1, NQ), f[S_GN:S_N2].reshape(1, DV),
            f[S_N2:S_FN].reshape(1, D), f[S_FN:S_LOSS].reshape(D))


class _Comm:
    def __init__(self, rest_shards, rest_lands, after, c_idx):
        self.c_idx = c_idx
        self.send, self.recv, self.shards, self.lands, self.token = _gather_start(rest_shards, rest_lands, after, "gather_rest_start")

    def _pass(self, lo, hi, after, tag):
        send2, recv2, lands, token = _gather_pass(self.lands[lo:hi], self.recv, after, "gather_pass_" + tag, first=lo)
        self.passed = (lo, hi, send2, recv2, lands)
        return token

    def _wait(self, after, tag):
        lo, hi, send2, recv2, lands = self.passed
        return _gather_wait(self.shards[lo:hi], lands, self.send, self.recv, send2, recv2, after, "gather_wait_" + tag, first=lo)

    def mixed(self, gla_o, gla_norm_w):
        return _tie(gla_norm_w, self._pass(0, 1, gla_o, "out"))

    def w_out(self, merged, norm2_w):
        (wo_all,) = self._wait(merged, "out")
        return wo_all.reshape(D, D), _tie(norm2_w, self._pass(1, 3, merged, "up"))

    def w_up(self, v2):
        wg_all, wu_all = self._wait(v2, "up")
        self._pass(3, 4, v2, "down")
        return wg_all.reshape(FH, D), wu_all.reshape(FH, D)

    def w_down(self, ff):
        return self._wait(ff, "down")[0].reshape(FH, D)

    def _reduce(self, tag, names, grads, recv1, rows):
        psums = [_pair_add(g, r, self.c_idx, "pair_add_" + nm, tr) for g, r, nm, tr in zip(grads, recv1, names, rows)]
        *flight, token = _chip_start(psums, "reduce_chips_start_" + tag)
        return dict(tag=tag, names=names, rows=rows, flight=flight), token

    def ffn_grads(self, d_wg, d_wu, d_wd):
        self.ffn_pair = _pair_start([d.reshape(4, 2, FS, D) for d in (d_wg, d_wu, d_wd)], "reduce_pair_start_ffn")
        return self.ffn_pair[-1]

    def ffn_reduce(self, dv2, norm2_w):
        send, recv, grads, lands, _ = self.ffn_pair
        grads, recv1 = _pair_wait(send, recv, grads, lands, dv2, "reduce_pair_wait_ffn")
        self.ffn, token = self._reduce("ffn", ["w_ffn_gate", "w_ffn_up", "w_ffn_down"], grads, recv1, [176, 176, 176])
        return _tie(norm2_w, token)

    def in_grads(self, d_wmain, d_wlr, w_lr):
        mine, recv = _disassemble_exchange(d_wmain, d_wlr)
        self.in_names, self.in_rows = ["w_in", "w_out"], [808, 256]
        *self.in_hop, token = _hop_start([_add_blocks(mine, recv, "pair_add_w_in", 808)], "reduce_hop_start_in")
        return _tie(w_lr, token)

    def in_reduce(self, d_wo):
        d_wo4 = d_wo.reshape(4, 2, D // NDEV, D)
        send, recv, (d_wo4,), lands, _ = _pair_start([d_wo4], "reduce_pair_start_out")
        (d_wo4,), (wo_recv,) = _pair_wait(send, recv, [d_wo4], lands, self.update(self.ffn, d_wo), "reduce_pair_wait_out")
        wo_psum = _pair_add(d_wo4, wo_recv, self.c_idx, "pair_add_w_out", 256)
        (psum,), (land,) = _hop_wait(*self.in_hop, wo_psum, "reduce_hop_wait_in")
        psum = _hop_add(psum, land, _hop_pos()[2].astype(jnp.int32).reshape(1), "hop_add_w_in", 808)
        *flight, token = _chip_start([psum, wo_psum], "reduce_chips_start_in", nrel=[2, 3])
        self.inw = dict(tag="in", names=self.in_names, rows=self.in_rows, flight=flight)
        return token


def _local_step(xs, tgt, u, norm1_w, gla_gate_b, attn_sinks, gla_norm_w, norm2_w, fnw, w_main, w_lr, w2p, comm):
    proj =_mm(u, w_main, tb=True, tm=1024, tn=1280, tk=D, name="in_proj")
    plr = _mm(u, w_lr, tb=True, tm=1024, tn=LANE, tk=D, name="in_proj_lr")
    attn_o = _attn_fwd(proj, attn_sinks)
    gla_o, states = _gla_fwd(proj, plr, w2p, gla_gate_b)
    merged = _merge_fwd(attn_o, gla_o, proj, comm.mixed(gla_o, gla_norm_w))
    wo, norm2_w = comm.w_out(merged, norm2_w)
    h1 = _mm(merged, wo, tm=1024, tn=512, tk=D, res=xs, name="out_proj")
    v2 = _rmsnorm_fwd(h1, norm2_w, "norm2_fwd")
    wg_all, wu_all = comm.w_up(v2)
    fa, fb, ff = _ffn_up(v2, wg_all, wu_all)
    wd_all = comm.w_down(ff)
    h2 = _mm(ff, wd_all, tm=1024, tn=1024, tk=FH // 2, res=h1, name="ffn_down")
    dh2, dh2b, d_fnw, loss_part = _loss_head(h2, fnw, tgt)

    da, db = _ffn_dact(dh2b, wd_all, fa, fb)
    Tn = xs.shape[0]
    d_wd = _mm(ff, dh2b, ta=True, tm=512, tn=D, tk=Tn, out_dtype=BF16, name="ffn_dwd")
    d_wg = _mm(da, v2, ta=True, tm=512, tn=D, tk=Tn, out_dtype=BF16, name="ffn_dwg")
    d_wu = _mm(db, v2, ta=True, tm=512, tn=D, tk=Tn, out_dtype=BF16, name="ffn_dwu")
    dv2 = _mm(da, wg_all, tm=1024, tn=1024, tk=FH // 2, after=comm.ffn_grads(d_wg, d_wu, d_wd), name="ffn_dv2_gate")
    dv2 = _mm(db, wu_all, tm=1024, tn=1024, tk=FH // 2, res=dv2, name="ffn_dv2_up")
    norm2_w = comm.ffn_reduce(dv2, norm2_w)
    dh1, dh1b, d_n2 = _rmsnorm_bwd(dv2, h1, norm2_w, dh2, "norm2_bwd")
    dmerged = _mm(dh1b, wo, tb=True, tm=1024, tn=512, tk=D, name="out_proj_dx")
    d_attn, d_gla, d_gates, d_gnw = _merge_bwd(dmerged, attn_o, gla_o, proj, gla_norm_w)
    d_q, d_kv, d_sinks = _attn_bwd(proj, attn_sinks, attn_o, d_attn)
    d_gqk, d_gv, d_plr, d_w2p, d_gb = _gla_bwd(proj, plr, w2p, gla_gate_b, states, d_gla)
    dproj = jnp.concatenate([d_q, d_kv, d_gqk, d_gv, d_gates], axis=1)
    d_wmain = _mm(dproj, u, ta=True, tm=640, tn=D, tk=xs.shape[0], out_dtype=BF16, name="in_proj_dw")
    d_wlr = _mm(d_plr, u, ta=True, tm=LANE, tn=1024, tk=xs.shape[0], out_dtype=BF16, name="in_proj_lr_dw")
    du_lr = _mm(d_plr, comm.in_grads(d_wmain, d_wlr, w_lr), tm=1024, tn=1024, tk=LANE, name="in_proj_lr_dx")
    d_wo = _mm(merged, dh1b, ta=True, tm=1024, tn=512, tk=xs.shape[0], out_dtype=BF16, after=du_lr, name="out_proj_dw")
    du = _mm(dproj, w_main, tm=1024, tn=1024, tk=2560, res=du_lr, after=comm.in_reduce(d_wo), name="in_proj_dx")
    dx, _, d_n1 = _rmsnorm_bwd(du, xs, norm1_w, dh1, "norm1_bwd")
    return dx, loss_part, d_w2p, d_gb, d_sinks, d_gnw, d_n1, d_n2, d_fnw


def kernel(x, norm1_w, w_in, gla_gate_w2, gla_gate_b, attn_sinks, gla_norm_w, w_out, norm2_w, w_ffn_gate, w_ffn_up, w_ffn_down, final_norm_w, loss_target, m_norm1_w, m_w_in, m_gla_gate_w2, m_gla_gate_b, m_attn_sinks, m_gla_norm_w, m_w_out, m_norm2_w, m_w_ffn_gate, m_w_ffn_up, m_w_ffn_down, m_final_norm_w, v_norm1_w, v_w_in, v_gla_gate_w2, v_gla_gate_b, v_attn_sinks, v_gla_norm_w, v_w_out, v_norm2_w, v_w_ffn_gate, v_w_ffn_up, v_w_ffn_down, v_final_norm_w):
    xs, tgt = x[0], loss_target[0]
    fnw = final_norm_w.reshape(1, D)
    c_idx = lax.axis_index("c").astype(jnp.int32).reshape(1)
    dev = 4 * lax.axis_index("x") + 2 * lax.axis_index("y") + lax.axis_index("c")

    chip_idx = (2 * lax.axis_index("x") + lax.axis_index("y")).astype(jnp.int32).reshape(1)

    shift = (WS - WSTEP) * dev
    edge = WWIN - WS
    window = lax.dynamic_slice(jnp.pad(jnp.transpose(w_in[0]).astype(BF16), ((edge, edge), (0, 0))), (edge - shift, 0), (WWIN, D))
    w2_land = lax.dynamic_update_slice(lax.empty((NDEV, RANK, LANE), F32), gla_gate_w2, (dev, 0, 0))
    *sems1, win_srcs, win_lands, tok = _win_start([window, gla_gate_w2[0]], [lax.empty((NDEV, WWIN, D), BF16), w2_land], "gather_in_start")
    tr2 = lambda t: jnp.transpose(t[0])
    rows3 = lambda t: jnp.transpose(t[0] + tok[0, 0])
    rest = [(w + tok[0, 0]).astype(BF16) for w in (w_out[0], tr2(w_ffn_gate), tr2(w_ffn_up), w_ffn_down[0])]
    rest_lands = [lax.dynamic_update_slice(lax.empty((NDEV,) + s.shape, s.dtype), s[None], (dev, 0, 0)) for s in rest]
    win3 = [rows3(t) for t in (w_in, m_w_in, v_w_in)]
    *sems2, win_lands, tok = _win_hand_on(win_lands, sems1[1], rest + rest_lands + win3, "gather_in_hand_on")
    u = _rmsnorm_fwd(xs, _tie(norm1_w, tok), "norm1_fwd")
    *sems3, win_lands, tok = _win_last(win_lands, sems2[1], [u], "gather_in_last")
    comm = _Comm(rest, rest_lands, tok, c_idx)
    win_all, w2_all = _win_wait(win_srcs, win_lands, sems1, sems2, sems3, comm.token, "gather_in_wait")
    w_main, w_lr = _assemble_w_in(win_all, window)
    w2p = jnp.pad(jnp.transpose(w2_all, (1, 0, 2)).reshape(RANK, GH * DK), ((0, LANE - RANK), (0, 0)))

    big = {}

    def update(grp, after):
        psums, parts = _chip_wait(*grp["flight"], after, "reduce_chips_wait_" + grp["tag"])
        for nm, ps, pt, tr in zip(grp["names"], psums, parts, grp["rows"]):
            w, m, v = {"w_in": (w_in, m_w_in, v_w_in), "w_out": (w_out, m_w_out, v_w_out), "w_ffn_gate": (w_ffn_gate, m_w_ffn_gate, v_w_ffn_gate),
                       "w_ffn_up": (w_ffn_up, m_w_ffn_up, v_w_ffn_up), "w_ffn_down": (w_ffn_down, m_w_ffn_down, v_w_ffn_down)}[nm]
            if nm == "w_in":
                g_win = _sum_parts(ps, pt, chip_idx, "sum_w_in", tr, 1024)
                g3 = lax.dynamic_slice(g_win, (shift, 0), (WS, D))
                out3 = (g3,) + tuple(_adamw_given(*win3, g3, "adamw_w_in", 536, 512))
                big[nm] = [jnp.transpose(t)[None] for t in out3]
            elif nm in ("w_ffn_gate", "w_ffn_up"):
                big[nm] = [jnp.transpose(t)[None] for t in _adamw(tr2(w), tr2(m), tr2(v), ps, pt, chip_idx, "adamw_" + nm, tr)]
            else:
                big[nm] = [t[None] for t in _adamw(w[0], m[0], v[0], ps, pt, chip_idx, "adamw_" + nm, tr)]
            after = big[nm][0]
        return after

    comm.update = update
    dx, loss_part, d_w2p, d_gb, d_sinks, d_gnw, d_n1, d_n2, d_fnw = _local_step(
        xs, tgt, u, norm1_w, gla_gate_b, attn_sinks, gla_norm_w, norm2_w, fnw, w_main, w_lr, w2p, comm)

    pack = jnp.concatenate([_pack_small(d_n1, d_gb, d_sinks, d_gnw, d_n2, d_fnw, loss_part),
                            d_w2p[:RANK].reshape(GW2_ROWS, LANE)], axis=0)
    small = _sum_devices(_gather_small(pack))

    update(comm.inw, dx)
    g_small = small[:SMALL_ROWS]
    sm = _adamw_plain(_pack_small(norm1_w, gla_gate_b, attn_sinks, gla_norm_w, norm2_w, final_norm_w),
                      _pack_small(m_norm1_w, m_gla_gate_b, m_attn_sinks, m_gla_norm_w, m_norm2_w, m_final_norm_w),
                      _pack_small(v_norm1_w, v_gla_gate_b, v_attn_sinks, v_gla_norm_w, v_norm2_w, v_final_norm_w), g_small, "adamw_small")
    g_w2 = lax.dynamic_slice_in_dim(small[SMALL_ROWS:].reshape(RANK, GH * DK), dev * LANE, LANE, axis=1)
    w2 = [g_w2[None]] + [t[None] for t in _adamw_plain(gla_gate_w2[0], m_gla_gate_w2[0], v_gla_gate_w2[0], g_w2, "adamw_w2")]
    loss = g_small.reshape(-1)[S_LOSS]

    sg, sd, sm2, sv2 = [_unpack_small(t) for t in (g_small,) + tuple(sm)]

    def group(i, s):
        return (s[0], big["w_in"][i], w2[i], s[1], s[2], s[3], big["w_out"][i], s[4], big["w_ffn_gate"][i], big["w_ffn_up"][i],
                big["w_ffn_down"][i], s[5])

    return (loss, dx[None], *group(0, sg), *group(1, sd), *group(2, sm2), *group(3, sv2))
```

```python
import jax
import jax.numpy as jnp
from jax import lax
from jax.experimental import pallas as pl
from jax.experimental.pallas import tpu as pltpu

F32, BF16 = jnp.float32, jnp.bfloat16
HIGHEST = lax.Precision.HIGHEST

D = 2048
HD, NQ, NKV, GRP, WIN = 64, 32, 4, 8, 128
GH, DK, DV, RANK, GC = 4, 256, 512, 16, 64
FH, NDEV = 5632, 8
FS = FH // NDEV
DIN = 12816
WS = DIN // NDEV
EPS = 1e-6
MASKV = -1e30
LANE = 128

C_AQ, C_AK, C_AV, C_GQ, C_GK, C_GV, C_GR, C_GA, C_GB, NMAIN = 0, 2048, 2304, 2560, 3584, 4608, 6656, 8704, 10752, 12800
C_LR = 6656
WSTEP, WWIN = 1600, 1616

LR, B1, B2, AEPS, WD, STEP = 0.001, 0.9, 0.999, 1e-08, 0.01, 10

S_N1, S_GB, S_SK, S_GN, S_N2, S_FN, S_LOSS, SMALL_N = 0, 2048, 3072, 3104, 3616, 5664, 7712, 8192
SMALL_ROWS = SMALL_N // LANE
GW2_ROWS = RANK * GH * DK // LANE

MESH = pl.DeviceIdType.MESH


def _dot(a, b, ta=False, tb=False, prec=None):
    dn = (((0,) if ta else (1,), (1,) if tb else (0,)), ((), ()))
    return lax.dot_general(a, b, dn, preferred_element_type=F32, precision=prec)


def _sigmoid(x):
    return 1.0 / (1.0 + jnp.exp(-x))


VMEM_LIMIT = 56 * 1024 * 1024


def _cp(*sem):
    return pltpu.CompilerParams(dimension_semantics=sem, vmem_limit_bytes=VMEM_LIMIT)


def _mm(a, b, *, ta=False, tb=False, tm, tn, tk, out_dtype=F32, res=None, after=None, name):
    M, K = (a.shape[1], a.shape[0]) if ta else a.shape
    N = b.shape[0] if tb else b.shape[1]
    tm, tn, tk = min(tm, M), min(tn, N), min(tk, K)
    nk = K // tk
    assert M % tm == 0 and N % tn == 0 and K % tk == 0
    a_spec = pl.BlockSpec((tk, tm), lambda i, j, k: (k, i)) if ta else pl.BlockSpec((tm, tk), lambda i, j, k: (i, k))
    b_spec = pl.BlockSpec((tn, tk), lambda i, j, k: (j, k)) if tb else pl.BlockSpec((tk, tn), lambda i, j, k: (k, j))
    o_spec = pl.BlockSpec((tm, tn), lambda i, j, k: (i, j))
    has_res = res is not None

    def body(*refs):
        a_ref, b_ref = refs[0], refs[1]
        r_ref = refs[2] if has_res else None
        o_ref = refs[2 + has_res + (after is not None)]
        p = _dot(a_ref[...].astype(BF16), b_ref[...].astype(BF16), ta, tb)
        if nk == 1:
            if has_res:
                p = p + r_ref[...]
            o_ref[...] = p.astype(out_dtype)
        else:
            acc = refs[-1]
            k = pl.program_id(2)

            @pl.when(k == 0)
            def _():
                acc[...] = (p + r_ref[...]) if has_res else p

            @pl.when(k > 0)
            def _():
                acc[...] += p

            @pl.when(k == nk - 1)
            def _():
                o_ref[...] = acc[...].astype(out_dtype)

    return pl.pallas_call(
        body, name=name,
        out_shape=jax.ShapeDtypeStruct((M, N), out_dtype),
        grid=(M // tm, N // tn, nk),
        in_specs=[a_spec, b_spec] + ([o_spec] if has_res else []) + ([pl.BlockSpec(memory_space=pl.ANY)] if after is not None else []),
        out_specs=o_spec,
        scratch_shapes=[pltpu.VMEM((tm, tn), F32)] if nk > 1 else [],
        compiler_params=_cp("parallel", "parallel", "arbitrary"),
    )(*((a, b) + ((res,) if has_res else ()) + ((after,) if after is not None else ())))


def _rmsnorm_fwd(x, w, name, tm=256):
    Tn = x.shape[0]

    def body(x_ref, w_ref, o_ref):
        xv = x_ref[...]
        r = lax.rsqrt(jnp.mean(xv * xv, axis=1, keepdims=True) + EPS)
        o_ref[...] = (xv * r * w_ref[...]).astype(BF16)

    return pl.pallas_call(
        body, name=name, out_shape=jax.ShapeDtypeStruct((Tn, D), BF16), grid=(Tn // tm,),
        in_specs=[pl.BlockSpec((tm, D), lambda i: (i, 0)), pl.BlockSpec((1, D), lambda i: (0, 0))],
        out_specs=pl.BlockSpec((tm, D), lambda i: (i, 0)), compiler_params=_cp("parallel"),
    )(x, w)


def _rmsnorm_bwd(dy, h, w, res, name, tm=256):
    Tn = h.shape[0]

    def body(dy_ref, h_ref, w_ref, res_ref, dh_ref, dhb_ref, dw_ref):
        hv, dyv = h_ref[...], dy_ref[...]
        r = lax.rsqrt(jnp.mean(hv * hv, axis=1, keepdims=True) + EPS)
        g = dyv * w_ref[...]
        dh = res_ref[...] + r * g - hv * (r * r * r * jnp.mean(g * hv, axis=1, keepdims=True))
        dh_ref[...] = dh
        dhb_ref[...] = dh.astype(BF16)
        part = jnp.sum(dyv * hv * r, axis=0, keepdims=True)

        @pl.when(pl.program_id(0) == 0)
        def _():
            dw_ref[...] = part

        @pl.when(pl.program_id(0) > 0)
        def _():
            dw_ref[...] += part

    row = pl.BlockSpec((tm, D), lambda i: (i, 0))
    vec = pl.BlockSpec((1, D), lambda i: (0, 0))
    return pl.pallas_call(
        body, name=name,
        out_shape=(jax.ShapeDtypeStruct((Tn, D), F32), jax.ShapeDtypeStruct((Tn, D), BF16), jax.ShapeDtypeStruct((1, D), F32)),
        grid=(Tn // tm,), in_specs=[row, row, vec, row], out_specs=(row, row, vec), compiler_params=_cp("arbitrary"),
    )(dy, h, w, res)


def _loss_head(h2, wf, tgt, name="loss_head", tm=256):
    Tn = h2.shape[0]

    def body(h_ref, w_ref, t_ref, dh_ref, dhb_ref, dw_ref, loss_ref):
        hv, wv = h_ref[...], w_ref[...]
        r = lax.rsqrt(jnp.mean(hv * hv, axis=1, keepdims=True) + EPS)
        hn = hv * r
        e = hn * wv - t_ref[...]
        dy = e * (1.0 / D)
        g = dy * wv
        dh = r * g - hv * (r * r * r * jnp.mean(g * hv, axis=1, keepdims=True))
        dh_ref[...] = dh
        dhb_ref[...] = dh.astype(BF16)
        part = jnp.sum(dy * hn, axis=0, keepdims=True)
        lpart = (0.5 / D) * jnp.sum(jnp.sum(e * e, axis=1, keepdims=True), axis=0, keepdims=True)

        @pl.when(pl.program_id(0) == 0)
        def _():
            dw_ref[...] = part
            loss_ref[...] = lpart

        @pl.when(pl.program_id(0) > 0)
        def _():
            dw_ref[...] += part
            loss_ref[...] += lpart

    row = pl.BlockSpec((tm, D), lambda i: (i, 0))
    vec = pl.BlockSpec((1, D), lambda i: (0, 0))
    one = pl.BlockSpec((1, 1), lambda i: (0, 0))
    return pl.pallas_call(
        body, name=name,
        out_shape=(jax.ShapeDtypeStruct((Tn, D), F32), jax.ShapeDtypeStruct((Tn, D), BF16), jax.ShapeDtypeStruct((1, D), F32),
                   jax.ShapeDtypeStruct((1, 1), F32)),
        grid=(Tn // tm,), in_specs=[row, vec, row], out_specs=(row, row, vec, one), compiler_params=_cp("arbitrary"),
    )(h2, wf, tgt)


def _attn_mask(n):
    qi = lax.broadcasted_iota(jnp.int32, (NKV, GRP * WIN, 2 * WIN), 1) % WIN
    ki = lax.broadcasted_iota(jnp.int32, (NKV, GRP * WIN, 2 * WIN), 2)
    rel = qi + WIN - ki
    return (rel >= 0) & (rel < WIN) & ((n > 0) | (ki >= WIN))


def _kv_heads(prev_ref, cur_ref):
    return jnp.stack([jnp.concatenate([prev_ref[:, h * HD:(h + 1) * HD], cur_ref[:, h * HD:(h + 1) * HD]], axis=0) for h in range(NKV)])


def _q_heads(ref):
    return jnp.stack([jnp.concatenate([ref[:, (h * GRP + g) * HD:(h * GRP + g + 1) * HD] for g in range(GRP)], axis=0) for h in range(NKV)])


def _attn_probs(q_ref, kc_ref, kp_ref, sink_ref, mask):
    kk = _kv_heads(kp_ref, kc_ref).astype(BF16)
    qs = _q_heads(q_ref).astype(BF16)
    s = jnp.einsum('hqd,hkd->hqk', qs, kk, preferred_element_type=F32) * (HD ** -0.5)
    s = jnp.where(mask, s, MASKV)
    sink = jnp.stack([jnp.concatenate([jnp.full((WIN, 1), sink_ref[0, h * GRP + g], F32) for g in range(GRP)], axis=0) for h in range(NKV)])
    m = jnp.maximum(jnp.max(s, axis=2, keepdims=True), sink)
    e = jnp.exp(s - m)
    es = jnp.exp(sink - m)
    inv = 1.0 / (jnp.sum(e, axis=2, keepdims=True) + es)
    return e * inv, es * inv, qs, kk


def _attn_specs(nb, last):
    cur = lambda n: jnp.minimum(n, last)
    prev = lambda n: jnp.maximum(jnp.minimum(n, last) - 1, 0)
    return [
        pl.BlockSpec((WIN, NQ * HD), lambda n: (cur(n), C_AQ // (NQ * HD))),
        pl.BlockSpec((WIN, NKV * HD), lambda n: (cur(n), C_AK // (NKV * HD))),
        pl.BlockSpec((WIN, NKV * HD), lambda n: (prev(n), C_AK // (NKV * HD))),
        pl.BlockSpec((WIN, NKV * HD), lambda n: (cur(n), C_AV // (NKV * HD))),
        pl.BlockSpec((WIN, NKV * HD), lambda n: (prev(n), C_AV // (NKV * HD))),
    ]


def _attn_fwd(proj, sinks, name="attn_fwd"):
    Tn = proj.shape[0]
    nb = Tn // WIN

    def body(q_ref, kc_ref, kp_ref, vc_ref, vp_ref, sink_ref, o_ref):
        p, _, _, _ = _attn_probs(q_ref, kc_ref, kp_ref, sink_ref, _attn_mask(pl.program_id(0)))
        o = jnp.einsum('hqk,hkd->hqd', p.astype(BF16), _kv_heads(vp_ref, vc_ref).astype(BF16), preferred_element_type=F32)
        for h in range(NKV):
            for g in range(GRP):
                o_ref[:, (h * GRP + g) * HD:(h * GRP + g + 1) * HD] = o[h, g * WIN:(g + 1) * WIN, :]

    return pl.pallas_call(
        body, name=name, out_shape=jax.ShapeDtypeStruct((Tn, D), F32), grid=(nb,),
        in_specs=_attn_specs(nb, nb - 1) + [pl.BlockSpec(memory_space=pltpu.SMEM)],
        out_specs=pl.BlockSpec((WIN, D), lambda n: (n, 0)), compiler_params=_cp("parallel"),
    )(proj, proj, proj, proj, proj, sinks)


def _attn_bwd(proj, sinks, o, do, name="attn_bwd"):
    Tn = proj.shape[0]
    nb = Tn // WIN
    KW = NKV * HD

    def body(q_ref, kc_ref, kp_ref, vc_ref, vp_ref, o_ref, do_ref, sink_ref, dq_ref, dkv_ref, dsk_ref, carry, cur):
        n = pl.program_id(0)

        @pl.when(n == 0)
        def _():
            carry[...] = jnp.zeros_like(carry)
            dsk_ref[...] = jnp.zeros_like(dsk_ref)

        @pl.when(n < nb)
        def _():
            p, ps, qs, kk = _attn_probs(q_ref, kc_ref, kp_ref, sink_ref, _attn_mask(n))
            vv = _kv_heads(vp_ref, vc_ref).astype(BF16)
            dos = _q_heads(do_ref)
            delta = jnp.sum(dos * _q_heads(o_ref), axis=2, keepdims=True)
            dosb = dos.astype(BF16)
            dp = jnp.einsum('hqd,hkd->hqk', dosb, vv, preferred_element_type=F32)
            ds = (p * (dp - delta) * (HD ** -0.5)).astype(BF16)
            dq = jnp.einsum('hqk,hkd->hqd', ds, kk, preferred_element_type=F32)
            dkk = jnp.einsum('hqk,hqd->hkd', ds, qs, preferred_element_type=F32)
            dvv = jnp.einsum('hqk,hqd->hkd', p.astype(BF16), dosb, preferred_element_type=F32)
            dsk = ps * delta
            for h in range(NKV):
                for g in range(GRP):
                    i = h * GRP + g
                    dq_ref[:, i * HD:(i + 1) * HD] = dq[h, g * WIN:(g + 1) * WIN, :].astype(BF16)
                    dsk_ref[:, i:i + 1] -= jnp.sum(dsk[h, g * WIN:(g + 1) * WIN, :], axis=0, keepdims=True)
                dkv_ref[:, h * HD:(h + 1) * HD] = (carry[:, h * HD:(h + 1) * HD] + dkk[h, :WIN, :]).astype(BF16)
                dkv_ref[:, KW + h * HD:KW + (h + 1) * HD] = (carry[:, KW + h * HD:KW + (h + 1) * HD] + dvv[h, :WIN, :]).astype(BF16)
                cur[:, h * HD:(h + 1) * HD] = dkk[h, WIN:, :]
                cur[:, KW + h * HD:KW + (h + 1) * HD] = dvv[h, WIN:, :]
            carry[...] = cur[...]

        @pl.when(n == nb)
        def _():
            dkv_ref[...] = carry[...].astype(BF16)

    last = nb - 1
    row = pl.BlockSpec((WIN, D), lambda n: (jnp.minimum(n, last), 0))
    return pl.pallas_call(
        body, name=name,
        out_shape=(jax.ShapeDtypeStruct((Tn, D), BF16), jax.ShapeDtypeStruct((Tn, 2 * KW), BF16), jax.ShapeDtypeStruct((1, NQ), F32)),
        grid=(nb + 1,),
        in_specs=_attn_specs(nb, last) + [row, row, pl.BlockSpec(memory_space=pltpu.SMEM)],
        out_specs=(row, pl.BlockSpec((WIN, 2 * KW), lambda n: (jnp.maximum(n - 1, 0), 0)), pl.BlockSpec((1, NQ), lambda n: (0, 0))),
        scratch_shapes=[pltpu.VMEM((WIN, 2 * KW), F32), pltpu.VMEM((WIN, 2 * KW), F32)],
        compiler_params=_cp("arbitrary"),
    )(proj, proj, proj, proj, proj, o, do, sinks)


def _tri(lower):
    r = lax.broadcasted_iota(jnp.int32, (GC, GC), 0)
    c = lax.broadcasted_iota(jnp.int32, (GC, GC), 1)
    return r >= c if lower else r <= c


def _per_head(a):
    return jnp.stack([a[:, h * DK:(h + 1) * DK] for h in range(GH)])


def _all_heads(a):
    return jnp.concatenate([a[h] for h in range(GH)], axis=1)


def _gla_gates(lr, w2_ref, gb_ref):
    logit = _dot(lr, w2_ref[...].astype(BF16)) + gb_ref[...]
    la = (jnp.minimum(logit, 0.0) - jnp.log(1.0 + jnp.exp(-jnp.abs(logit)))) * (1.0 / 16.0)
    g = _dot(_tri(True).astype(F32), la, prec=HIGHEST)
    return logit, g


def _bmm(spec, a, b):
    return jnp.einsum(spec, a, b, preferred_element_type=F32)


def _gla_specs(nc, rev):
    idx = (lambda n: nc - 1 - n) if rev else (lambda n: n)
    half = 2 * DK
    return (
        [pl.BlockSpec((GC, half), lambda n, j=j: (idx(n), C_GQ // half + j)) for j in range(2)]
        + [pl.BlockSpec((GC, half), lambda n, j=j: (idx(n), C_GK // half + j)) for j in range(2)]
        + [pl.BlockSpec((GC, DV), lambda n, h=h: (idx(n), C_GV // DV + h)) for h in range(GH)]
        + [pl.BlockSpec((GC, LANE), lambda n: (idx(n), 0)), pl.BlockSpec((LANE, GH * DK), lambda n: (0, 0)),
           pl.BlockSpec((1, GH * DK), lambda n: (0, 0))])


def _gla_heads(refs):
    return (lambda h: refs[h // 2][:, (h % 2) * DK:(h % 2 + 1) * DK], lambda h: refs[2 + h // 2][:, (h % 2) * DK:(h % 2 + 1) * DK],
            lambda h: refs[4 + h][...])


def _gla_fwd(proj, plr, w2p, gb, name="gla_fwd"):
    Tn = proj.shape[0]
    nc = Tn // GC

    def body(*refs):
        qh, kh, vh = _gla_heads(refs)
        lr_ref, w2_ref, gb_ref, o_ref, st_ref, S = refs[8:]

        @pl.when(pl.program_id(0) == 0)
        def _():
            S[...] = jnp.zeros_like(S)

        heads = lambda f: jnp.stack([f(h) for h in range(GH)])
        _, g_all = _gla_gates(lr_ref[...].astype(BF16), w2_ref, gb_ref)
        g = _per_head(g_all)
        gl = g[:, GC - 1:GC, :]
        k = heads(kh)
        v = heads(vh).astype(BF16)
        qd = (heads(qh) * (DK ** -0.5) * jnp.exp(g)).astype(BF16)
        ki = (k * jnp.exp(-g)).astype(BF16)
        ke = (k * jnp.exp(gl - g)).astype(BF16)
        att = jnp.where(_tri(True)[None], _bmm('hid,hjd->hij', qd, ki), 0.0).astype(BF16)
        sp = S[...]
        st_ref[0] = sp
        o = _bmm('hij,hjv->hiv', att, v) + _bmm('hid,hvd->hiv', qd, sp.astype(BF16))
        for h in range(GH):
            o_ref[:, h * DV:(h + 1) * DV] = o[h]
        S[...] = sp * jnp.exp(gl) + _bmm('hjv,hjd->hvd', v, ke)

    return pl.pallas_call(
        body, name=name,
        out_shape=(jax.ShapeDtypeStruct((Tn, GH * DV), F32), jax.ShapeDtypeStruct((nc, GH, DV, DK), F32)),
        grid=(nc,), in_specs=_gla_specs(nc, False),
        out_specs=(pl.BlockSpec((GC, GH * DV), lambda n: (n, 0)), pl.BlockSpec((1, GH, DV, DK), lambda n: (n, 0, 0, 0))),
        scratch_shapes=[pltpu.VMEM((GH, DV, DK), F32)], compiler_params=_cp("arbitrary"),
    )(*([proj] * 8), plr, w2p, gb)


def _gla_bwd(proj, plr, w2p, gb, states, do, name="gla_bwd"):
    Tn = proj.shape[0]
    nc = Tn // GC

    def body(*refs):
        qh, kh, vh = _gla_heads(refs)
        lr_ref, w2_ref, gb_ref, st_ref, do_ref, dqk_ref, dv_ref, dlr_ref, dw2_ref, dgb_ref, dS = refs[8:]

        @pl.when(pl.program_id(0) == 0)
        def _():
            dS[...] = jnp.zeros_like(dS)
            dw2_ref[...] = jnp.zeros_like(dw2_ref)
            dgb_ref[...] = jnp.zeros_like(dgb_ref)

        heads = lambda f: jnp.stack([f(h) for h in range(GH)])
        lr = lr_ref[...].astype(BF16)
        causal = _tri(True)[None]
        last_row = lax.broadcasted_iota(jnp.int32, (GH, GC, DK), 1) == GC - 1
        logit, g_all = _gla_gates(lr, w2_ref, gb_ref)
        g = _per_head(g_all)
        gl = g[:, GC - 1:GC, :]
        egl = jnp.exp(gl)
        eg, eng, ege = jnp.exp(g), jnp.exp(-g), jnp.exp(gl - g)
        k = heads(kh)
        v = heads(vh).astype(BF16)
        dob = heads(lambda h: do_ref[:, h * DV:(h + 1) * DV]).astype(BF16)
        qd = heads(qh) * (DK ** -0.5) * eg
        ki = k * eng
        ke = k * ege
        qdb, kib, keb = qd.astype(BF16), ki.astype(BF16), ke.astype(BF16)
        att = jnp.where(causal, _bmm('hid,hjd->hij', qdb, kib), 0.0).astype(BF16)
        datt = jnp.where(causal, _bmm('hiv,hjv->hij', dob, v), 0.0).astype(BF16)
        sp = st_ref[0]
        dsn = dS[...]
        dsnb = dsn.astype(BF16)
        dv = (_bmm('hij,hiv->hjv', att, dob) + _bmm('hjd,hvd->hjv', keb, dsnb)).astype(BF16)
        dqd = _bmm('hij,hjd->hid', datt, kib) + _bmm('hiv,hvd->hid', dob, sp.astype(BF16))
        dki = _bmm('hij,hid->hjd', datt, qdb)
        dke = _bmm('hjv,hvd->hjd', v, dsnb)
        ddec = jnp.sum(dsn * sp, axis=1, keepdims=True)
        dS[...] = dsn * egl + _bmm('hiv,hid->hvd', dob, qdb)
        dke_ke = dke * ke
        dgl = jnp.sum(dke_ke, axis=1, keepdims=True) + ddec * egl
        dg = dqd * qd - dki * ki - dke_ke + jnp.where(last_row, dgl, 0.0)
        dq = (dqd * ((DK ** -0.5) * eg)).astype(BF16)
        dk = (dki * eng + dke * ege).astype(BF16)
        for h in range(GH):
            dv_ref[:, h * DV:(h + 1) * DV] = dv[h]
            dqk_ref[:, h * DK:(h + 1) * DK] = dq[h]
            dqk_ref[:, GH * DK + h * DK:GH * DK + (h + 1) * DK] = dk[h]
        dla = _dot(_tri(False).astype(F32), _all_heads(dg), prec=HIGHEST)
        dlogit = dla * (1.0 / 16.0) * _sigmoid(-logit)
        dlb = dlogit.astype(BF16)
        dlr_ref[...] = _dot(dlb, w2_ref[...].astype(BF16), tb=True).astype(BF16)
        dw2_ref[...] += _dot(lr, dlb, ta=True)
        dgb_ref[...] += jnp.sum(dlogit, axis=0, keepdims=True)

    rev = lambda n: nc - 1 - n
    row = pl.BlockSpec((GC, GH * DV), lambda n: (rev(n), 0))
    return pl.pallas_call(
        body, name=name,
        out_shape=(jax.ShapeDtypeStruct((Tn, 2 * GH * DK), BF16), jax.ShapeDtypeStruct((Tn, GH * DV), BF16),
                   jax.ShapeDtypeStruct((Tn, LANE), BF16), jax.ShapeDtypeStruct((LANE, GH * DK), F32),
                   jax.ShapeDtypeStruct((1, GH * DK), F32)),
        grid=(nc,),
        in_specs=_gla_specs(nc, True) + [pl.BlockSpec((1, GH, DV, DK), lambda n: (rev(n), 0, 0, 0)), row],
        out_specs=(row, row, pl.BlockSpec((GC, LANE), lambda n: (rev(n), 0)), pl.BlockSpec((LANE, GH * DK), lambda n: (0, 0)),
                   pl.BlockSpec((1, GH * DK), lambda n: (0, 0))),
        scratch_shapes=[pltpu.VMEM((GH, DV, DK), F32)], compiler_params=_cp("arbitrary"),
    )(*([proj] * 8), plr, w2p, gb, states, do)


def _merge_specs(tm):
    row = pl.BlockSpec((tm, D), lambda i: (i, 0))
    gates = [pl.BlockSpec((tm, DV), lambda i, j=c // DV + h: (i, j)) for c in (C_GR, C_GA, C_GB) for h in range(GH)]
    return row, gates, pl.BlockSpec((1, DV), lambda i: (0, 0))


def _merge_fwd(a, go, proj, gnw, name="merge_fwd", tm=256):
    Tn = a.shape[0]

    def body(a_ref, go_ref, *rest):
        gates, w_ref, m_ref = rest[:3 * GH], rest[3 * GH], rest[3 * GH + 1]
        for h in range(GH):
            sl = slice(h * DV, (h + 1) * DV)
            gov = go_ref[:, sl]
            r = lax.rsqrt(jnp.mean(gov * gov, axis=1, keepdims=True) + EPS)
            gr = gates[h][...]
            g2 = gov * r * w_ref[...] * (gr * _sigmoid(gr))
            m_ref[:, sl] = (_sigmoid(gates[GH + h][...]) * a_ref[:, sl] + _sigmoid(gates[2 * GH + h][...]) * g2).astype(BF16)

    row, gates, vec = _merge_specs(tm)
    return pl.pallas_call(
        body, name=name, out_shape=jax.ShapeDtypeStruct((Tn, D), BF16), grid=(Tn // tm,),
        in_specs=[row, row] + gates + [vec], out_specs=row, compiler_params=_cp("parallel"),
    )(a, go, *([proj] * (3 * GH)), gnw)


def _merge_bwd(dm, a, go, proj, gnw, name="merge_bwd", tm=256):
    Tn = a.shape[0]

    def body(dm_ref, a_ref, go_ref, *rest):
        gates = rest[:3 * GH]
        w_ref, da_ref, dgo_ref, dg_ref, dw_ref = rest[3 * GH:]
        wv = w_ref[...]
        dw = jnp.zeros((1, DV), F32)
        for h in range(GH):
            sl = slice(h * DV, (h + 1) * DV)
            dmv, av, gov, gr = dm_ref[:, sl], a_ref[:, sl], go_ref[:, sl], gates[h][...]
            sa, sb, sg = _sigmoid(gates[GH + h][...]), _sigmoid(gates[2 * GH + h][...]), _sigmoid(gr)
            r = lax.rsqrt(jnp.mean(gov * gov, axis=1, keepdims=True) + EPS)
            gn0 = gov * r
            gn = gn0 * wv
            silu = gr * sg
            dg2 = dmv * sb
            da_ref[:, sl] = dmv * sa
            dg_ref[:, D + h * DV:D + (h + 1) * DV] = (dmv * av * sa * (1.0 - sa)).astype(BF16)
            dg_ref[:, 2 * D + h * DV:2 * D + (h + 1) * DV] = (dg2 * gn * silu * (1.0 - sb)).astype(BF16)
            dg_ref[:, sl] = (dg2 * gn * (sg * (1.0 + gr * (1.0 - sg)))).astype(BF16)
            dgn = dg2 * silu
            dw = dw + jnp.sum(dgn * gn0, axis=0, keepdims=True)
            gg = dgn * wv
            dgo_ref[:, sl] = r * gg - gov * (r * r * r * jnp.mean(gg * gov, axis=1, keepdims=True))

        @pl.when(pl.program_id(0) == 0)
        def _():
            dw_ref[...] = dw

        @pl.when(pl.program_id(0) > 0)
        def _():
            dw_ref[...] += dw

    row, gates, vec = _merge_specs(tm)
    return pl.pallas_call(
        body, name=name,
        out_shape=(jax.ShapeDtypeStruct((Tn, D), F32), jax.ShapeDtypeStruct((Tn, D), F32), jax.ShapeDtypeStruct((Tn, 3 * D), BF16),
                   jax.ShapeDtypeStruct((1, DV), F32)),
        grid=(Tn // tm,), in_specs=[row, row, row] + gates + [vec],
        out_specs=(row, row, pl.BlockSpec((tm, 3 * D), lambda i: (i, 0)), vec), compiler_params=_cp("arbitrary"),
    )(dm, a, go, *([proj] * (3 * GH)), gnw)


def _ffn_up(v2, wgt, wut, name="ffn_up", tm=1024, tn=512):
    Tn = v2.shape[0]
    tm = min(tm, Tn)

    def body(v_ref, wg_ref, wu_ref, a_ref, b_ref, ff_ref):
        vv = v_ref[...]
        a = _dot(vv, wg_ref[...], tb=True)
        b = _dot(vv, wu_ref[...], tb=True)
        a_ref[...] = a.astype(BF16)
        b_ref[...] = b.astype(BF16)
        ff_ref[...] = (a * _sigmoid(a) * b).astype(BF16)

    w = pl.BlockSpec((tn, D), lambda j, i: (j, 0))
    act = pl.BlockSpec((tm, tn), lambda j, i: (i, j))
    return pl.pallas_call(
        body, name=name,
        out_shape=(jax.ShapeDtypeStruct((Tn, FH), BF16), jax.ShapeDtypeStruct((Tn, FH), BF16), jax.ShapeDtypeStruct((Tn, FH), BF16)),
        grid=(FH // tn, Tn // tm), in_specs=[pl.BlockSpec((tm, D), lambda j, i: (i, 0)), w, w], out_specs=(act, act, act),
        compiler_params=_cp("parallel", "parallel"),
    )(v2, wgt, wut)


def _ffn_dact(dh2b, wd, a, b, name="ffn_dact", tm=1024, tn=512):
    Tn = dh2b.shape[0]
    tm = min(tm, Tn)

    def body(d_ref, w_ref, a_ref, b_ref, da_ref, db_ref):
        dff = _dot(d_ref[...], w_ref[...], tb=True)
        av = a_ref[...].astype(F32)
        sg = _sigmoid(av)
        da_ref[...] = (dff * b_ref[...].astype(F32) * (sg * (1.0 + av * (1.0 - sg)))).astype(BF16)
        db_ref[...] = (dff * (av * sg)).astype(BF16)

    act = pl.BlockSpec((tm, tn), lambda j, i: (i, j))
    return pl.pallas_call(
        body, name=name,
        out_shape=(jax.ShapeDtypeStruct((Tn, FH), BF16), jax.ShapeDtypeStruct((Tn, FH), BF16)),
        grid=(FH // tn, Tn // tm),
        in_specs=[pl.BlockSpec((tm, D), lambda j, i: (i, 0)), pl.BlockSpec((tn, D), lambda j, i: (j, 0)), act, act],
        out_specs=(act, act), compiler_params=_cp("parallel", "parallel"),
    )(dh2b, wd, a, b)


def _adam_math(w, g, m, v):
    m2 = B1 * m + (1.0 - B1) * g
    v2 = B2 * v + (1.0 - B2) * (g * g)
    mh = m2 / (1.0 - B1 ** STEP)
    vh = v2 / (1.0 - B2 ** STEP)
    return -LR * (mh / (jnp.sqrt(vh) + AEPS) + WD * w), m2, v2


def _sum_blocks(o_ref, p_ref):
    g = o_ref[...].astype(F32)
    for j in range(p_ref.shape[0]):
        g = g + p_ref[j].astype(F32)
    return g


def _adamw(w, m, v, psums, parts, chip_idx, name, tr):
    R, C = w.shape

    def body(s_ref, w_ref, m_ref, v_ref, o_ref, p_ref, g_ref, d_ref, m2_ref, v2_ref):
        g = _sum_blocks(o_ref, p_ref)
        d, m2, v2 = _adam_math(w_ref[...], g, m_ref[...], v_ref[...])
        g_ref[...] = g
        d_ref[...] = d
        m2_ref[...] = m2
        v2_ref[...] = v2

    blk = pl.BlockSpec((tr, C), lambda i, s: (i, 0))
    out = jax.ShapeDtypeStruct((R, C), F32)
    grid_spec = pltpu.PrefetchScalarGridSpec(
        num_scalar_prefetch=1, grid=(R // tr,),
        in_specs=[blk, blk, blk, pl.BlockSpec((None, tr, C), lambda i, s: (s[0], i, 0)),
                  pl.BlockSpec((parts.shape[0], tr, C), lambda i, s: (0, i, 0))],
        out_specs=(blk, blk, blk, blk),
    )
    return pl.pallas_call(body, name=name, out_shape=(out, out, out, out), grid_spec=grid_spec, compiler_params=_cp("parallel"),
                          )(chip_idx, w, m, v, psums, parts)


def _adamw_given(w, m, v, g, name, tr, tc):
    R, C = w.shape

    def body(w_ref, m_ref, v_ref, g_ref, d_ref, m2_ref, v2_ref):
        d, m2, v2 = _adam_math(w_ref[...], g_ref[...], m_ref[...], v_ref[...])
        d_ref[...] = d
        m2_ref[...] = m2
        v2_ref[...] = v2

    blk = pl.BlockSpec((tr, tc), lambda i, j: (i, j))
    out = jax.ShapeDtypeStruct(w.shape, F32)
    return pl.pallas_call(body, name=name, out_shape=(out, out, out), grid=(pl.cdiv(R, tr), C // tc), in_specs=[blk] * 4,
                          out_specs=(blk, blk, blk), compiler_params=_cp("parallel", "parallel"))(w, m, v, g)


def _sum_parts(psums, parts, chip_idx, name, tr, tc):
    _, R, C = psums.shape

    def body(s_ref, o_ref, p_ref, g_ref):
        g_ref[...] = _sum_blocks(o_ref, p_ref)

    grid_spec = pltpu.PrefetchScalarGridSpec(
        num_scalar_prefetch=1, grid=(R // tr, C // tc),
        in_specs=[pl.BlockSpec((None, tr, tc), lambda i, j, s: (s[0], i, j)),
                  pl.BlockSpec((parts.shape[0], tr, tc), lambda i, j, s: (0, i, j))],
        out_specs=pl.BlockSpec((tr, tc), lambda i, j, s: (i, j)),
    )
    return pl.pallas_call(body, name=name, out_shape=jax.ShapeDtypeStruct((R, C), F32), grid_spec=grid_spec,
                          compiler_params=_cp("parallel", "parallel"))(chip_idx, psums, parts)


def _adamw_plain(w, m, v, g, name):
    def body(w_ref, m_ref, v_ref, g_ref, d_ref, m2_ref, v2_ref):
        d, m2, v2 = _adam_math(w_ref[...], g_ref[...], m_ref[...], v_ref[...])
        d_ref[...] = d
        m2_ref[...] = m2
        v2_ref[...] = v2

    out = jax.ShapeDtypeStruct(w.shape, F32)
    return pl.pallas_call(body, name=name, out_shape=(out, out, out))(w, m, v, g)


def _sum_devices(pack_all, name="sum_small"):
    def body(p_ref, o_ref):
        s = p_ref[0]
        for k in range(1, NDEV):
            s = s + p_ref[k]
        o_ref[...] = s

    return pl.pallas_call(body, name=name, out_shape=jax.ShapeDtypeStruct(pack_all.shape[1:], F32))(pack_all)


def _pair_add(g5, recv, c_idx, name, tr):
    _, _, R, C = g5.shape

    def body(c_ref, g_ref, r_ref, o_ref):
        o_ref[...] = (g_ref[...].astype(F32) + r_ref[...].astype(F32)).astype(BF16)

    grid_spec = pltpu.PrefetchScalarGridSpec(
        num_scalar_prefetch=1, grid=(4, R // tr),
        in_specs=[pl.BlockSpec((None, None, tr, C), lambda q, i, c: (q, c[0], i, 0)), pl.BlockSpec((None, tr, C), lambda q, i, c: (q, i, 0))],
        out_specs=pl.BlockSpec((None, tr, C), lambda q, i, c: (q, i, 0)),
    )
    return pl.pallas_call(
        body, name=name, out_shape=jax.ShapeDtypeStruct((4, R, C), BF16), grid_spec=grid_spec,
        compiler_params=_cp("parallel", "parallel"),
    )(c_idx, g5, recv)


_ANY = pl.BlockSpec(memory_space=pl.ANY)


def _mesh_pos():
    x, y, c = lax.axis_index("x"), lax.axis_index("y"), lax.axis_index("c")
    return x, y, c, [(1 - x, y), (x, 1 - y), (1 - x, 1 - y)]


def _gather_small(pack, name="gather_small"):
    def body(pk, pk_all, psend, precv, loc):
        x, y, c, chips = _mesh_pos()
        me_slot = 4 * x + 2 * y + c
        sib = (x, y, 1 - c)
        own = pltpu.make_async_copy(pk, pk_all.at[me_slot], loc)
        own.start()
        peers = [sib] + [(*chip, c) for chip in chips] + [(*chip, 1 - c) for chip in chips]
        small = [pltpu.make_async_remote_copy(src_ref=pk, dst_ref=pk_all.at[me_slot], send_sem=psend.at[k], recv_sem=precv.at[k],
                                              device_id=p, device_id_type=MESH) for k, p in enumerate(peers)]
        for d in small:
            d.start()
        for k, p in enumerate(peers):
            pltpu.make_async_remote_copy(src_ref=pk, dst_ref=pk_all.at[4 * p[0] + 2 * p[1] + p[2]], send_sem=psend.at[k],
                                         recv_sem=precv.at[k], device_id=p, device_id_type=MESH).wait_recv()
        for d in small:
            d.wait_send()
        own.wait()

    return pl.pallas_call(
        body, name=name, out_shape=jax.ShapeDtypeStruct((NDEV,) + pack.shape, pack.dtype), in_specs=[_ANY], out_specs=_ANY,
        scratch_shapes=[pltpu.SemaphoreType.DMA((7,)), pltpu.SemaphoreType.DMA((7,)), pltpu.SemaphoreType.DMA(())],
    )(pack)


def _main_row(g):
    return g if g < C_LR else g - RANK


def _window_pieces(lo, hi):
    out = []
    for a, b, where in ((lo, min(hi, C_LR), "main"), (max(lo, C_LR), min(hi, C_LR + RANK), "lr"), (max(lo, C_LR + RANK), hi, "main")):
        if a < b:
            out.append((a, b, where, _main_row(a) if where == "main" else a - C_LR))
    return out


def _assemble_w_in(windows, own, name="assemble_w_in"):
    edges = NDEV - 1

    def body(b_ref, own_ref, main_ref, lr_ref, buf, ebuf, in_sems, out_sems, esems):
        dev = 4 * lax.axis_index("x") + 2 * lax.axis_index("y") + lax.axis_index("c")

        def load(k):
            return pltpu.make_async_copy(b_ref.at[k], buf.at[k % 2], in_sems.at[k % 2])

        def start_load(k):
            pl.when(dev == k)(pltpu.make_async_copy(own_ref, buf.at[k % 2], in_sems.at[k % 2]).start)
            pl.when(dev != k)(load(k).start)

        lr_ref[RANK:, :] = jnp.zeros((LANE - RANK, D), BF16)
        start_load(0)
        pending, edge_out = [], []
        for k in range(NDEV):
            s = k % 2
            load(k).wait()
            if k:
                ebuf[k - 1] = buf[1 - s, WSTEP:WWIN, :] + buf[s, 0:16, :]
                edge_out.append(pltpu.make_async_copy(ebuf.at[k - 1], main_ref.at[pl.ds(_main_row(WSTEP * k), 16)], esems.at[k - 1]))
                edge_out[-1].start()
                for d in pending:
                    d.wait()
            if k + 1 < NDEV:
                start_load(k + 1)
            pending = []
            lo = WSTEP * k + (16 if k else 0)
            hi = WSTEP * k + (WWIN if k == NDEV - 1 else WSTEP)
            for a, b, where, dst in _window_pieces(lo, hi):
                if where == "lr":
                    lr_ref[dst:dst + b - a, :] = buf[s, a - WSTEP * k:b - WSTEP * k, :]
                else:
                    pending.append(pltpu.make_async_copy(buf.at[s, pl.ds(a - WSTEP * k, b - a)], main_ref.at[pl.ds(dst, b - a)],
                                                         out_sems.at[2 * s + len(pending)]))
                    pending[-1].start()
        for d in pending + edge_out:
            d.wait()

    return pl.pallas_call(
        body, name=name,
        out_shape=(jax.ShapeDtypeStruct((NMAIN, D), BF16), jax.ShapeDtypeStruct((LANE, D), BF16)),
        in_specs=[_ANY, _ANY], out_specs=(_ANY, pl.BlockSpec(memory_space=pltpu.VMEM)),
        scratch_shapes=[pltpu.VMEM((2, WWIN, D), BF16), pltpu.VMEM((edges, 16, D), BF16), pltpu.SemaphoreType.DMA((2,)),
                        pltpu.SemaphoreType.DMA((4,)), pltpu.SemaphoreType.DMA((edges,))],
        compiler_params=pltpu.CompilerParams(vmem_limit_bytes=VMEM_LIMIT),
    )(windows, own)


def _disassemble_exchange(d_main, d_lr, name="disassemble_exchange"):
    def body(main_ref, lr_ref, mine_ref, recv_ref, buf, in_sems, keep_sems, send_sems, recv_sems):
        x, y, c, _ = _mesh_pos()
        sib = (x, y, 1 - c)

        def loads(k):
            s, out = k % 2, []
            for a, b, where, src0 in _window_pieces(WSTEP * k, WSTEP * k + WWIN):
                if where == "main":
                    out.append(pltpu.make_async_copy(main_ref.at[pl.ds(src0, b - a)], buf.at[s, pl.ds(a - WSTEP * k, b - a)],
                                                     in_sems.at[2 * s + len(out)]))
            return out

        def keep(k):
            return pltpu.make_async_copy(buf.at[k % 2], mine_ref.at[k // 2], keep_sems.at[k % 2])

        def send(k):
            return _rcopy(buf.at[k % 2], recv_ref.at[k // 2], send_sems.at[k % 2], recv_sems.at[k // 2], sib)

        def store_start(k):
            pl.when(c == k % 2)(keep(k).start)
            pl.when(c != k % 2)(send(k).start)

        def store_wait(k):
            pl.when(c == k % 2)(keep(k).wait)
            pl.when(c != k % 2)(send(k).wait_send)

        for d in loads(0):
            d.start()
        for k in range(NDEV):
            for d in loads(k):
                d.wait()
            for a, b, where, src0 in _window_pieces(WSTEP * k, WSTEP * k + WWIN):
                if where == "lr":
                    buf[k % 2, a - WSTEP * k:b - WSTEP * k, :] = lr_ref[src0:src0 + b - a, :]
            if k:
                store_wait(k - 1)
            if k + 1 < NDEV:
                for d in loads(k + 1):
                    d.start()
            store_start(k)
        store_wait(NDEV - 1)
        for chip in range(NDEV // 2):
            _rcopy(buf.at[0], recv_ref.at[chip], send_sems.at[0], recv_sems.at[chip], sib).wait_recv()

    half = jax.ShapeDtypeStruct((NDEV // 2, WWIN, D), BF16)
    return pl.pallas_call(
        body, name=name, out_shape=(half, half),
        in_specs=[_ANY, pl.BlockSpec(memory_space=pltpu.VMEM)], out_specs=(_ANY, _ANY),
        scratch_shapes=[pltpu.VMEM((2, WWIN, D), BF16), pltpu.SemaphoreType.DMA((4,)), pltpu.SemaphoreType.DMA((2,)),
                        pltpu.SemaphoreType.DMA((2,)), pltpu.SemaphoreType.DMA((NDEV // 2,))],
        compiler_params=pltpu.CompilerParams(vmem_limit_bytes=VMEM_LIMIT),
    )(d_main, d_lr)


def _add_blocks(a, b, name, tr):
    _, R, C = a.shape

    def body(a_ref, b_ref, o_ref):
        o_ref[...] = (a_ref[...].astype(F32) + b_ref[...].astype(F32)).astype(BF16)

    blk = pl.BlockSpec((None, tr, C), lambda q, i: (q, i, 0))
    return pl.pallas_call(body, name=name, out_shape=jax.ShapeDtypeStruct(a.shape, BF16), grid=(a.shape[0], R // tr),
                          in_specs=[blk, blk], out_specs=blk, compiler_params=_cp("parallel", "parallel"))(a, b)


_HBM = pl.BlockSpec(memory_space=pltpu.HBM)
_SEM = pl.BlockSpec(memory_space=pltpu.SEMAPHORE)
_VMEM = pl.BlockSpec(memory_space=pltpu.VMEM)
_SIDE = pltpu.CompilerParams(has_side_effects=pltpu.SideEffectType.DATAFLOW_SIDE_EFFECTING)
_TOKEN = jax.ShapeDtypeStruct((8, LANE), F32)


def _hbm(a):
    return pltpu.with_memory_space_constraint(a, pltpu.HBM)


def _hbm_like(arrs):
    return tuple(pltpu.HBM(a.shape, a.dtype) for a in arrs)


def _tie(x, token):
    return x + token[0, 0].astype(x.dtype)


def _chip_copies(ins, lands, send, recv, nrel):
    x, y, c, chips = _mesh_pos()
    first = [sum(nrel[:a]) for a in range(len(ins))]
    return [pltpu.make_async_remote_copy(src_ref=ins[a].at[2 * chip[0] + chip[1]], dst_ref=lands[a].at[j], send_sem=send.at[first[a] + j],
                                         recv_sem=recv.at[first[a] + j], device_id=(*chip, c), device_id_type=MESH)
            for a in range(len(ins)) for j, chip in enumerate(chips[:nrel[a]])]


def _chip_start(psums, name, nrel=None):
    n = len(psums)
    nrel = nrel or [3] * n
    lands = [lax.empty((r,) + p.shape[1:], p.dtype) for r, p in zip(nrel, psums)]

    def body(*refs):
        for d in _chip_copies(refs[:n], refs[n:2 * n], refs[2 * n], refs[2 * n + 1], nrel):
            d.start()
        refs[-1][...] = jnp.zeros_like(refs[-1])

    sems = pltpu.SemaphoreType.DMA((sum(nrel),))
    out = pl.pallas_call(
        body, name=name, out_shape=(sems, sems) + _hbm_like(psums) + _hbm_like(lands) + (_TOKEN,),
        in_specs=[_HBM] * (2 * n), out_specs=(_SEM, _SEM) + (_HBM,) * (2 * n) + (_VMEM,),
        input_output_aliases={i: 2 + i for i in range(2 * n)}, compiler_params=_SIDE,
    )(*[_hbm(a) for a in list(psums) + lands])
    return out[0], out[1], list(out[2:2 + n]), list(out[2 + n:2 + 2 * n]), out[-1]


def _chip_wait(send, recv, psums, lands, after, name):
    n = len(psums)
    nrel = [l.shape[0] for l in lands]

    def body(*refs):
        for d in _chip_copies(refs[:n], refs[n:2 * n], refs[2 * n], refs[2 * n + 1], nrel):
            d.wait_send()
            d.wait_recv()

    out = pl.pallas_call(
        body, name=name, out_shape=_hbm_like(psums) + _hbm_like(lands),
        in_specs=[_HBM] * (2 * n) + [_SEM, _SEM, _ANY], out_specs=(_HBM,) * (2 * n),
        input_output_aliases={i: i for i in range(2 * n)}, compiler_params=_SIDE,
    )(*psums, *lands, send, recv, after)
    return list(out[:n]), list(out[n:])


def _hop_pos():
    x, y, c, _ = _mesh_pos()
    north = c == 1
    via = (jnp.where(north, 1 - x, x), jnp.where(north, y, 1 - y))
    return (*via, c), 2 * (1 - x) + (1 - y), jnp.where(north, 2 * x + (1 - y), 2 * (1 - x) + y)


def _hop_copies(ins, lands, send, recv):
    to, mine, _ = _hop_pos()
    return [pltpu.make_async_remote_copy(src_ref=ins[a].at[mine], dst_ref=lands[a], send_sem=send.at[a], recv_sem=recv.at[a],
                                         device_id=to, device_id_type=MESH) for a in range(len(ins))]


def _hop_start(psums, name):
    n = len(psums)
    lands = [lax.empty(p.shape[1:], p.dtype) for p in psums]

    def body(*refs):
        for d in _hop_copies(refs[:n], refs[n:2 * n], refs[2 * n], refs[2 * n + 1]):
            d.start()
        refs[-1][...] = jnp.zeros_like(refs[-1])

    sems = pltpu.SemaphoreType.DMA((n,))
    out = pl.pallas_call(
        body, name=name, out_shape=(sems, sems) + _hbm_like(psums) + _hbm_like(lands) + (_TOKEN,),
        in_specs=[_HBM] * (2 * n), out_specs=(_SEM, _SEM) + (_HBM,) * (2 * n) + (_VMEM,),
        input_output_aliases={i: 2 + i for i in range(2 * n)}, compiler_params=_SIDE,
    )(*[_hbm(a) for a in list(psums) + lands])
    return out[0], out[1], list(out[2:2 + n]), list(out[2 + n:2 + 2 * n]), out[-1]


def _hop_wait(send, recv, psums, lands, after, name):
    n = len(psums)

    def body(*refs):
        for d in _hop_copies(refs[:n], refs[n:2 * n], refs[2 * n], refs[2 * n + 1]):
            d.wait_send()
            d.wait_recv()

    out = pl.pallas_call(
        body, name=name, out_shape=_hbm_like(psums) + _hbm_like(lands),
        in_specs=[_HBM] * (2 * n) + [_SEM, _SEM, _ANY], out_specs=(_HBM,) * (2 * n),
        input_output_aliases={i: i for i in range(2 * n)}, compiler_params=_SIDE,
    )(*psums, *lands, send, recv, after)
    return list(out[:n]), list(out[n:])


def _hop_add(psums, land, idx, name, tr):
    _, R, C = psums.shape

    def body(s_ref, p_ref, l_ref, o_ref):
        o_ref[...] = (p_ref[...].astype(F32) + l_ref[...].astype(F32)).astype(BF16)

    blk = pl.BlockSpec((None, tr, C), lambda i, s: (s[0], i, 0))
    grid_spec = pltpu.PrefetchScalarGridSpec(num_scalar_prefetch=1, grid=(R // tr,),
                                             in_specs=[blk, pl.BlockSpec((tr, C), lambda i, s: (i, 0))], out_specs=blk)
    return pl.pallas_call(body, name=name, out_shape=jax.ShapeDtypeStruct(psums.shape, BF16), grid_spec=grid_spec,
                          input_output_aliases={1: 0}, compiler_params=_cp("parallel"))(idx, psums, land)


def _pair_copies(ins, lands, send, recv):
    x, y, c, _ = _mesh_pos()
    return [pltpu.make_async_remote_copy(src_ref=ins[a].at[:, 1 - c], dst_ref=lands[a], send_sem=send.at[a], recv_sem=recv.at[a],
                                         device_id=(x, y, 1 - c), device_id_type=MESH) for a in range(len(ins))]


def _pair_start(grads, name):
    n = len(grads)
    lands = [lax.empty((4,) + g.shape[2:], g.dtype) for g in grads]

    def body(*refs):
        for d in _pair_copies(refs[:n], refs[n:2 * n], refs[2 * n], refs[2 * n + 1]):
            d.start()
        refs[-1][...] = jnp.zeros_like(refs[-1])

    sems = pltpu.SemaphoreType.DMA((n,))
    out = pl.pallas_call(
        body, name=name, out_shape=(sems, sems) + _hbm_like(grads) + _hbm_like(lands) + (_TOKEN,),
        in_specs=[_HBM] * (2 * n), out_specs=(_SEM, _SEM) + (_HBM,) * (2 * n) + (_VMEM,),
        input_output_aliases={i: 2 + i for i in range(2 * n)}, compiler_params=_SIDE,
    )(*[_hbm(a) for a in list(grads) + lands])
    return out[0], out[1], list(out[2:2 + n]), list(out[2 + n:2 + 2 * n]), out[-1]


def _pair_wait(send, recv, grads, lands, after, name):
    n = len(grads)

    def body(*refs):
        for d in _pair_copies(refs[:n], refs[n:2 * n], refs[2 * n], refs[2 * n + 1]):
            d.wait_send()
            d.wait_recv()

    out = pl.pallas_call(
        body, name=name, out_shape=_hbm_like(grads) + _hbm_like(lands),
        in_specs=[_HBM] * (2 * n) + [_SEM, _SEM, _ANY], out_specs=(_HBM,) * (2 * n),
        input_output_aliases={i: i for i in range(2 * n)}, compiler_params=_SIDE,
    )(*grads, *lands, send, recv, after)
    return list(out[:n]), list(out[n:])


def _slot(chip, c):
    return 4 * chip[0] + 2 * chip[1] + c


def _gather_start(shards, lands, after, name):
    n = len(shards)

    def body(*refs):
        src, land, send, recv = refs[:n], refs[n:2 * n], refs[2 * n + 1], refs[2 * n + 2]
        x, y, c, chips = _mesh_pos()
        for a in range(n):
            for k, to in enumerate([(x, y, 1 - c)] + [(*chip, c) for chip in chips]):
                pltpu.make_async_remote_copy(src_ref=src[a], dst_ref=land[a].at[_slot((x, y), c)], send_sem=send.at[4 * a + k],
                                             recv_sem=recv.at[4 * a + k], device_id=to, device_id_type=MESH).start()
        refs[-1][...] = jnp.zeros_like(refs[-1])

    sems = pltpu.SemaphoreType.DMA((4 * n,))
    out = pl.pallas_call(
        body, name=name, out_shape=(sems, sems) + _hbm_like(shards) + _hbm_like(lands) + (_TOKEN,),
        in_specs=[_HBM] * (2 * n) + [_ANY], out_specs=(_SEM, _SEM) + (_HBM,) * (2 * n) + (_VMEM,),
        input_output_aliases={i: 2 + i for i in range(2 * n)}, compiler_params=_SIDE,
    )(*[_hbm(a) for a in list(shards) + list(lands)], after)
    return out[0], out[1], list(out[2:2 + n]), list(out[2 + n:2 + 2 * n]), out[-1]


def _gather_pass(lands, recv, after, name, first=0):
    n = len(lands)

    def body(*refs):
        land, recv1 = refs[:n], refs[n]
        send2, recv2 = refs[n + 2], refs[n + 3]
        x, y, c, chips = _mesh_pos()
        for a in range(n):
            for j, chip in enumerate(chips):
                blk = land[a].at[_slot(chip, c)]
                pltpu.make_async_remote_copy(src_ref=blk, dst_ref=blk, send_sem=send2.at[3 * a + j], recv_sem=recv1.at[4 * (first + a) + 1 + j],
                                             device_id=(*chip, c), device_id_type=MESH).wait_recv()
                pltpu.make_async_remote_copy(src_ref=blk, dst_ref=blk, send_sem=send2.at[3 * a + j], recv_sem=recv2.at[3 * a + j],
                                             device_id=(x, y, 1 - c), device_id_type=MESH).start()
        refs[-1][...] = jnp.zeros_like(refs[-1])

    sems = pltpu.SemaphoreType.DMA((3 * n,))
    out = pl.pallas_call(
        body, name=name, out_shape=(sems, sems) + _hbm_like(lands) + (_TOKEN,),
        in_specs=[_HBM] * n + [_SEM, _ANY], out_specs=(_SEM, _SEM) + (_HBM,) * n + (_VMEM,),
        input_output_aliases={i: 2 + i for i in range(n)}, compiler_params=_SIDE,
    )(*lands, recv, after)
    return out[0], out[1], list(out[2:2 + n]), out[-1]


def _gather_wait(shards, lands, send, recv, send2, recv2, after, name, first=0):
    n = len(lands)

    def body(*refs):
        src, land = refs[:n], refs[n:2 * n]
        send1, recv1, snd2, rcv2 = refs[2 * n:2 * n + 4]
        x, y, c, chips = _mesh_pos()
        sib = (x, y, 1 - c)
        for a in range(n):
            for k in range(4):
                pltpu.make_async_remote_copy(src_ref=src[a], dst_ref=land[a].at[_slot((x, y), c)], send_sem=send1.at[4 * (first + a) + k],
                                             recv_sem=recv1.at[4 * (first + a) + k], device_id=sib, device_id_type=MESH).wait_send()
            blk = land[a].at[_slot((x, y), 1 - c)]
            pltpu.make_async_remote_copy(src_ref=blk, dst_ref=blk, send_sem=send1.at[4 * (first + a)], recv_sem=recv1.at[4 * (first + a)],
                                         device_id=sib, device_id_type=MESH).wait_recv()
            for j, chip in enumerate(chips):
                mine, theirs = land[a].at[_slot(chip, c)], land[a].at[_slot(chip, 1 - c)]
                pltpu.make_async_remote_copy(src_ref=mine, dst_ref=mine, send_sem=snd2.at[3 * a + j], recv_sem=rcv2.at[3 * a + j],
                                             device_id=sib, device_id_type=MESH).wait_send()
                pltpu.make_async_remote_copy(src_ref=theirs, dst_ref=theirs, send_sem=snd2.at[3 * a + j], recv_sem=rcv2.at[3 * a + j],
                                             device_id=sib, device_id_type=MESH).wait_recv()

    out = pl.pallas_call(
        body, name=name, out_shape=_hbm_like(shards) + _hbm_like(lands),
        in_specs=[_HBM] * (2 * n) + [_SEM] * 4 + [_ANY], out_specs=(_HBM,) * (2 * n),
        input_output_aliases={i: i for i in range(2 * n)}, compiler_params=_SIDE,
    )(*shards, *lands, send, recv, send2, recv2, after)
    return list(out[n:])


def _win_tree():
    x, y, c, chips = _mesh_pos()
    north = c == 1
    handed = (jnp.where(north, 1 - x, x), jnp.where(north, y, 1 - y))
    hand_to = (jnp.where(north, x, 1 - x), jnp.where(north, 1 - y, y))
    return x, y, c, chips, handed, hand_to


def _blk(land, chip, c):
    return land.at[_slot(chip, c)]


def _rcopy(src, dst, send, recv, to):
    return pltpu.make_async_remote_copy(src_ref=src, dst_ref=dst, send_sem=send, recv_sem=recv, device_id=to, device_id_type=MESH)


def _win_start(shards, lands, name):
    n = len(shards)

    def body(*refs):
        src, land, send, recv = refs[:n], refs[n:2 * n], refs[2 * n], refs[2 * n + 1]
        x, y, c, chips, _, _ = _win_tree()
        for a in range(n):
            for k, to in enumerate([(x, y, 1 - c), (*chips[0], c), (*chips[1], c)]):
                _rcopy(src[a], _blk(land[a], (x, y), c), send.at[3 * a + k], recv.at[3 * a + k], to).start()
        refs[-1][...] = jnp.zeros_like(refs[-1])

    sems = pltpu.SemaphoreType.DMA((3 * n,))
    out = pl.pallas_call(
        body, name=name, out_shape=(sems, sems) + _hbm_like(shards) + _hbm_like(lands) + (_TOKEN,),
        in_specs=[_HBM] * (2 * n), out_specs=(_SEM, _SEM) + (_HBM,) * (2 * n) + (_VMEM,),
        input_output_aliases={i: 2 + i for i in range(2 * n)}, compiler_params=_SIDE,
    )(*[_hbm(a) for a in list(shards) + list(lands)])
    return out[0], out[1], list(out[2:2 + n]), list(out[2 + n:2 + 2 * n]), out[-1]


def _win_hand_on(lands, recv1, after, name):
    n, m = len(lands), len(after)

    def body(*refs):
        land, rcv1 = refs[:n], refs[n]
        send2, recv2 = refs[n + 1 + m], refs[n + 2 + m]
        x, y, c, chips, handed, hand_to = _win_tree()
        for a in range(n):
            for j in range(2):
                blk = _blk(land[a], chips[j], c)
                _rcopy(blk, blk, send2.at[3 * a], rcv1.at[3 * a + 1 + j], (*chips[j], c)).wait_recv()
            blk = _blk(land[a], handed, c)
            _rcopy(blk, blk, send2.at[3 * a], recv2.at[3 * a], (*hand_to, c)).start()
            for j in range(2):
                blk = _blk(land[a], chips[j], c)
                _rcopy(blk, blk, send2.at[3 * a + 1 + j], recv2.at[3 * a + 1 + j], (x, y, 1 - c)).start()
        refs[-1][...] = jnp.zeros_like(refs[-1])

    sems = pltpu.SemaphoreType.DMA((3 * n,))
    out = pl.pallas_call(
        body, name=name, out_shape=(sems, sems) + _hbm_like(lands) + (_TOKEN,),
        in_specs=[_HBM] * n + [_SEM] + [_ANY] * m, out_specs=(_SEM, _SEM) + (_HBM,) * n + (_VMEM,),
        input_output_aliases={i: 2 + i for i in range(n)}, compiler_params=_SIDE,
    )(*lands, recv1, *after)
    return out[0], out[1], list(out[2:2 + n]), out[-1]


def _win_last(lands, recv2, after, name):
    n, m = len(lands), len(after)

    def body(*refs):
        land, rcv2 = refs[:n], refs[n]
        send3, recv3 = refs[n + 1 + m], refs[n + 2 + m]
        x, y, c, chips, _, hand_to = _win_tree()
        for a in range(n):
            blk = _blk(land[a], chips[2], c)
            _rcopy(blk, blk, send3.at[a], rcv2.at[3 * a], (*hand_to, c)).wait_recv()
            _rcopy(blk, blk, send3.at[a], recv3.at[a], (x, y, 1 - c)).start()
        refs[-1][...] = jnp.zeros_like(refs[-1])

    sems = pltpu.SemaphoreType.DMA((n,))
    out = pl.pallas_call(
        body, name=name, out_shape=(sems, sems) + _hbm_like(lands) + (_TOKEN,),
        in_specs=[_HBM] * n + [_SEM] + [_ANY] * m, out_specs=(_SEM, _SEM) + (_HBM,) * n + (_VMEM,),
        input_output_aliases={i: 2 + i for i in range(n)}, compiler_params=_SIDE,
    )(*lands, recv2, *after)
    return out[0], out[1], list(out[2:2 + n]), out[-1]


def _win_wait(shards, lands, sems1, sems2, sems3, after, name):
    n = len(lands)

    def body(*refs):
        src, land = refs[:n], refs[n:2 * n]
        send1, recv1, send2, recv2, send3, recv3 = refs[2 * n:2 * n + 6]
        x, y, c, chips, handed, hand_to = _win_tree()
        sib = (x, y, 1 - c)
        for a in range(n):
            own = _blk(land[a], (x, y), c)
            for k in range(3):
                _rcopy(src[a], own, send1.at[3 * a + k], recv1.at[3 * a + k], sib).wait_send()
            blk = _blk(land[a], (x, y), 1 - c)
            _rcopy(blk, blk, send1.at[3 * a], recv1.at[3 * a], sib).wait_recv()
            blk = _blk(land[a], handed, c)
            _rcopy(blk, blk, send2.at[3 * a], recv2.at[3 * a], sib).wait_send()
            for j in range(2):
                mine, theirs = _blk(land[a], chips[j], c), _blk(land[a], chips[j], 1 - c)
                _rcopy(mine, mine, send2.at[3 * a + 1 + j], recv2.at[3 * a + 1 + j], sib).wait_send()
                _rcopy(theirs, theirs, send2.at[3 * a + 1 + j], recv2.at[3 * a + 1 + j], sib).wait_recv()
            mine, theirs = _blk(land[a], chips[2], c), _blk(land[a], chips[2], 1 - c)
            _rcopy(mine, mine, send3.at[a], recv3.at[a], sib).wait_send()
            _rcopy(theirs, theirs, send3.at[a], recv3.at[a], sib).wait_recv()

    out = pl.pallas_call(
        body, name=name, out_shape=_hbm_like(shards) + _hbm_like(lands),
        in_specs=[_HBM] * (2 * n) + [_SEM] * 6 + [_ANY], out_specs=(_HBM,) * (2 * n),
        input_output_aliases={i: i for i in range(2 * n)}, compiler_params=_SIDE,
    )(*shards, *lands, *sems1, *sems2, *sems3, after)
    return list(out[n:])


def _pad_to(v, n):
    return jnp.pad(v, [(0, 0)] * (v.ndim - 1) + [(0, n - v.shape[-1])])


def _pack_small(n1, gb, sk, gn, n2, fn, extra=None):
    parts = [n1.reshape(-1), gb.reshape(-1), sk.reshape(-1), gn.reshape(-1), n2.reshape(-1), fn.reshape(-1)]
    flat = jnp.concatenate(parts + ([extra.reshape(-1)] if extra is not None else []))
    return _pad_to(flat, SMALL_N).reshape(SMALL_ROWS, LANE)


def _unpack_small(p):
    f = p.reshape(-1)
    return (f[S_N1:S_GB].reshape(1, D), f[S_GB:S_SK].reshape(1, GH * DK), f[S_SK:S_GN].reshape(1, NQ), f[S_GN:S_N2].reshape(1, DV),
            f[S_N2:S_FN].reshape(1, D), f[S_FN:S_LOSS].reshape(D))


class _Comm:
    def __init__(self, rest_shards, rest_lands, after, c_idx):
        self.c_idx = c_idx
        self.send, self.recv, self.shards, self.lands, self.token = _gather_start(rest_shards, rest_lands, after, "gather_rest_start")

    def _pass(self, lo, hi, after, tag):
        send2, recv2, lands, token = _gather_pass(self.lands[lo:hi], self.recv, after, "gather_pass_" + tag, first=lo)
        self.passed = (lo, hi, send2, recv2, lands)
        return token

    def _wait(self, after, tag):
        lo, hi, send2, recv2, lands = self.passed
        return _gather_wait(self.shards[lo:hi], lands, self.send, self.recv, send2, recv2, after, "gather_wait_" + tag, first=lo)

    def mixed(self, gla_o, gla_norm_w):
        return _tie(gla_norm_w, self._pass(0, 1, gla_o, "out"))

    def w_out(self, merged, norm2_w):
        (wo_all,) = self._wait(merged, "out")
        return wo_all.reshape(D, D), _tie(norm2_w, self._pass(1, 3, merged, "up"))

    def w_up(self, v2):
        wg_all, wu_all = self._wait(v2, "up")
        self._pass(3, 4, v2, "down")
        return wg_all.reshape(FH, D), wu_all.reshape(FH, D)

    def w_down(self, ff):
        return self._wait(ff, "down")[0].reshape(FH, D)

    def _reduce(self, tag, names, grads, recv1, rows):
        psums = [_pair_add(g, r, self.c_idx, "pair_add_" + nm, g.shape[2]) for g, r, nm in zip(grads, recv1, names)]
        *flight, token = _chip_start(psums, "reduce_chips_start_" + tag)
        return dict(tag=tag, names=names, rows=rows, flight=flight), token

    def ffn_grads(self, d_wg, d_wu, d_wd):
        self.ffn_pair = _pair_start([d.reshape(4, 2, FS, D) for d in (d_wg, d_wu, d_wd)], "reduce_pair_start_ffn")
        return self.ffn_pair[-1]

    def ffn_reduce(self, dv2, norm2_w):
        send, recv, grads, lands, _ = self.ffn_pair
        grads, recv1 = _pair_wait(send, recv, grads, lands, dv2, "reduce_pair_wait_ffn")
        self.ffn, token = self._reduce("ffn", ["w_ffn_gate", "w_ffn_up", "w_ffn_down"], grads, recv1, [176, 176, 176])
        return _tie(norm2_w, token)

    def in_grads(self, d_wmain, d_wlr, w_lr):
        mine, recv = _disassemble_exchange(d_wmain, d_wlr)
        self.in_names, self.in_rows = ["w_in", "w_out"], [808, 256]
        *self.in_hop, token = _hop_start([_add_blocks(mine, recv, "pair_add_w_in", 808)], "reduce_hop_start_in")
        return _tie(w_lr, token)

    def in_reduce(self, d_wo):
        d_wo4 = d_wo.reshape(4, 2, D // NDEV, D)
        send, recv, (d_wo4,), lands, _ = _pair_start([d_wo4], "reduce_pair_start_out")
        (d_wo4,), (wo_recv,) = _pair_wait(send, recv, [d_wo4], lands, self.update(self.ffn, d_wo), "reduce_pair_wait_out")
        wo_psum = _pair_add(d_wo4, wo_recv, self.c_idx, "pair_add_w_out", 256)
        (psum,), (land,) = _hop_wait(*self.in_hop, wo_psum, "reduce_hop_wait_in")
        psum = _hop_add(psum, land, _hop_pos()[2].astype(jnp.int32).reshape(1), "hop_add_w_in", 808)
        *flight, token = _chip_start([psum, wo_psum], "reduce_chips_start_in", nrel=[2, 3])
        self.inw = dict(tag="in", names=self.in_names, rows=self.in_rows, flight=flight)
        return token


def _local_step(xs, tgt, u, norm1_w, gla_gate_b, attn_sinks, gla_norm_w, norm2_w, fnw, w_main, w_lr, w2p, comm):
    proj =_mm(u, w_main, tb=True, tm=1024, tn=1280, tk=D, name="in_proj")
    plr = _mm(u, w_lr, tb=True, tm=1024, tn=LANE, tk=D, name="in_proj_lr")
    attn_o = _attn_fwd(proj, attn_sinks)
    gla_o, states = _gla_fwd(proj, plr, w2p, gla_gate_b)
    merged = _merge_fwd(attn_o, gla_o, proj, comm.mixed(gla_o, gla_norm_w))
    wo, norm2_w = comm.w_out(merged, norm2_w)
    h1 = _mm(merged, wo, tm=1024, tn=512, tk=D, res=xs, name="out_proj")
    v2 = _rmsnorm_fwd(h1, norm2_w, "norm2_fwd")
    wg_all, wu_all = comm.w_up(v2)
    fa, fb, ff = _ffn_up(v2, wg_all, wu_all)
    wd_all = comm.w_down(ff)
    h2 = _mm(ff, wd_all, tm=1024, tn=1024, tk=FH // 2, res=h1, name="ffn_down")
    dh2, dh2b, d_fnw, loss_part = _loss_head(h2, fnw, tgt)

    da, db = _ffn_dact(dh2b, wd_all, fa, fb)
    Tn = xs.shape[0]
    d_wd = _mm(ff, dh2b, ta=True, tm=512, tn=D, tk=Tn, out_dtype=BF16, name="ffn_dwd")
    d_wg = _mm(da, v2, ta=True, tm=512, tn=D, tk=Tn, out_dtype=BF16, name="ffn_dwg")
    d_wu = _mm(db, v2, ta=True, tm=512, tn=D, tk=Tn, out_dtype=BF16, name="ffn_dwu")
    dv2 = _mm(da, wg_all, tm=1024, tn=1024, tk=FH // 2, after=comm.ffn_grads(d_wg, d_wu, d_wd), name="ffn_dv2_gate")
    dv2 = _mm(db, wu_all, tm=1024, tn=1024, tk=FH // 2, res=dv2, name="ffn_dv2_up")
    norm2_w = comm.ffn_reduce(dv2, norm2_w)
    dh1, dh1b, d_n2 = _rmsnorm_bwd(dv2, h1, norm2_w, dh2, "norm2_bwd")
    dmerged = _mm(dh1b, wo, tb=True, tm=1024, tn=512, tk=D, name="out_proj_dx")
    d_attn, d_gla, d_gates, d_gnw = _merge_bwd(dmerged, attn_o, gla_o, proj, gla_norm_w)
    d_q, d_kv, d_sinks = _attn_bwd(proj, attn_sinks, attn_o, d_attn)
    d_gqk, d_gv, d_plr, d_w2p, d_gb = _gla_bwd(proj, plr, w2p, gla_gate_b, states, d_gla)
    dproj = jnp.concatenate([d_q, d_kv, d_gqk, d_gv, d_gates], axis=1)
    d_wmain = _mm(dproj, u, ta=True, tm=640, tn=D, tk=xs.shape[0], out_dtype=BF16, name="in_proj_dw")
    d_wlr = _mm(d_plr, u, ta=True, tm=LANE, tn=1024, tk=xs.shape[0], out_dtype=BF16, name="in_proj_lr_dw")
    du_lr = _mm(d_plr, comm.in_grads(d_wmain, d_wlr, w_lr), tm=1024, tn=1024, tk=LANE, name="in_proj_lr_dx")
    d_wo = _mm(merged, dh1b, ta=True, tm=1024, tn=512, tk=xs.shape[0], out_dtype=BF16, after=du_lr, name="out_proj_dw")
    du = _mm(dproj, w_main, tm=1024, tn=1024, tk=2560, res=du_lr, after=comm.in_reduce(d_wo), name="in_proj_dx")
    dx, _, d_n1 = _rmsnorm_bwd(du, xs, norm1_w, dh1, "norm1_bwd")
    return dx, loss_part, d_w2p, d_gb, d_sinks, d_gnw, d_n1, d_n2, d_fnw


def kernel(x, norm1_w, w_in, gla_gate_w2, gla_gate_b, attn_sinks, gla_norm_w, w_out, norm2_w, w_ffn_gate, w_ffn_up, w_ffn_down, final_norm_w, loss_target, m_norm1_w, m_w_in, m_gla_gate_w2, m_gla_gate_b, m_attn_sinks, m_gla_norm_w, m_w_out, m_norm2_w, m_w_ffn_gate, m_w_ffn_up, m_w_ffn_down, m_final_norm_w, v_norm1_w, v_w_in, v_gla_gate_w2, v_gla_gate_b, v_attn_sinks, v_gla_norm_w, v_w_out, v_norm2_w, v_w_ffn_gate, v_w_ffn_up, v_w_ffn_down, v_final_norm_w):
    xs, tgt = x[0], loss_target[0]
    fnw = final_norm_w.reshape(1, D)
    c_idx = lax.axis_index("c").astype(jnp.int32).reshape(1)
    dev = 4 * lax.axis_index("x") + 2 * lax.axis_index("y") + lax.axis_index("c")

    chip_idx = (2 * lax.axis_index("x") + lax.axis_index("y")).astype(jnp.int32).reshape(1)

    shift = (WS - WSTEP) * dev
    edge = WWIN - WS
    window = lax.dynamic_slice(jnp.pad(jnp.transpose(w_in[0]).astype(BF16), ((edge, edge), (0, 0))), (edge - shift, 0), (WWIN, D))
    w2_land = lax.dynamic_update_slice(lax.empty((NDEV, RANK, LANE), F32), gla_gate_w2, (dev, 0, 0))
    *sems1, win_srcs, win_lands, tok = _win_start([window, gla_gate_w2[0]], [lax.empty((NDEV, WWIN, D), BF16), w2_land], "gather_in_start")
    tr2 = lambda t: jnp.transpose(t[0])
    rows3 = lambda t: jnp.transpose(t[0] + tok[0, 0])
    rest = [(w + tok[0, 0]).astype(BF16) for w in (w_out[0], tr2(w_ffn_gate), tr2(w_ffn_up), w_ffn_down[0])]
    rest_lands = [lax.dynamic_update_slice(lax.empty((NDEV,) + s.shape, s.dtype), s[None], (dev, 0, 0)) for s in rest]
    win3 = [rows3(t) for t in (w_in, m_w_in, v_w_in)]
    *sems2, win_lands, tok = _win_hand_on(win_lands, sems1[1], rest + rest_lands + win3, "gather_in_hand_on")
    u = _rmsnorm_fwd(xs, _tie(norm1_w, tok), "norm1_fwd")
    *sems3, win_lands, tok = _win_last(win_lands, sems2[1], [u], "gather_in_last")
    comm = _Comm(rest, rest_lands, tok, c_idx)
    win_all, w2_all = _win_wait(win_srcs, win_lands, sems1, sems2, sems3, comm.token, "gather_in_wait")
    w_main, w_lr = _assemble_w_in(win_all, window)
    w2p = jnp.pad(jnp.transpose(w2_all, (1, 0, 2)).reshape(RANK, GH * DK), ((0, LANE - RANK), (0, 0)))

    big = {}

    def update(grp, after):
        psums, parts = _chip_wait(*grp["flight"], after, "reduce_chips_wait_" + grp["tag"])
        for nm, ps, pt, tr in zip(grp["names"], psums, parts, grp["rows"]):
            w, m, v = {"w_in": (w_in, m_w_in, v_w_in), "w_out": (w_out, m_w_out, v_w_out), "w_ffn_gate": (w_ffn_gate, m_w_ffn_gate, v_w_ffn_gate),
                       "w_ffn_up": (w_ffn_up, m_w_ffn_up, v_w_ffn_up), "w_ffn_down": (w_ffn_down, m_w_ffn_down, v_w_ffn_down)}[nm]
            if nm == "w_in":
                g_win = _sum_parts(ps, pt, chip_idx, "sum_w_in", tr, 1024)
                g3 = lax.dynamic_slice(g_win, (shift, 0), (WS, D))
                out3 = (g3,) + tuple(_adamw_given(*win3, g3, "adamw_w_in", 536, 512))
                big[nm] = [jnp.transpose(t)[None] for t in out3]
            elif nm in ("w_ffn_gate", "w_ffn_up"):
                big[nm] = [jnp.transpose(t)[None] for t in _adamw(tr2(w), tr2(m), tr2(v), ps, pt, chip_idx, "adamw_" + nm, tr)]
            else:
                big[nm] = [t[None] for t in _adamw(w[0], m[0], v[0], ps, pt, chip_idx, "adamw_" + nm, tr)]
            after = big[nm][0]
        return after

    comm.update = update
    dx, loss_part, d_w2p, d_gb, d_sinks, d_gnw, d_n1, d_n2, d_fnw = _local_step(
        xs, tgt, u, norm1_w, gla_gate_b, attn_sinks, gla_norm_w, norm2_w, fnw, w_main, w_lr, w2p, comm)

    pack = jnp.concatenate([_pack_small(d_n1, d_gb, d_sinks, d_gnw, d_n2, d_fnw, loss_part),
                            d_w2p[:RANK].reshape(GW2_ROWS, LANE)], axis=0)
    small = _sum_devices(_gather_small(pack))

    update(comm.inw, dx)
    g_small = small[:SMALL_ROWS]
    sm = _adamw_plain(_pack_small(norm1_w, gla_gate_b, attn_sinks, gla_norm_w, norm2_w, final_norm_w),
                      _pack_small(m_norm1_w, m_gla_gate_b, m_attn_sinks, m_gla_norm_w, m_norm2_w, m_final_norm_w),
                      _pack_small(v_norm1_w, v_gla_gate_b, v_attn_sinks, v_gla_norm_w, v_norm2_w, v_final_norm_w), g_small, "adamw_small")
    g_w2 = lax.dynamic_slice_in_dim(small[SMALL_ROWS:].reshape(RANK, GH * DK), dev * LANE, LANE, axis=1)
    w2 = [g_w2[None]] + [t[None] for t in _adamw_plain(gla_gate_w2[0], m_gla_gate_w2[0], v_gla_gate_w2[0], g_w2, "adamw_w2")]
    loss = g_small.reshape(-1)[S_LOSS]

    sg, sd, sm2, sv2 = [_unpack_small(t) for t in (g_small,) + tuple(sm)]

    def group(i, s):
        return (s[0], big["w_in"][i], w2[i], s[1], s[2], s[3], big["w_out"][i], s[4], big["w_ffn_gate"][i], big["w_ffn_up"][i],
                big["w_ffn_down"][i], s[5])

    return (loss, dx[None], *group(0, sg), *group(1, sd), *group(2, sm2), *group(3, sv2))
```

```python
import jax
import jax.numpy as jnp
from jax import lax
from jax.experimental import pallas as pl
from jax.experimental.pallas import tpu as pltpu

F32, BF16 = jnp.float32, jnp.bfloat16
HIGHEST = lax.Precision.HIGHEST

D = 2048
HD, NQ, NKV, GRP, WIN = 64, 32, 4, 8, 128
GH, DK, DV, RANK, GC = 4, 256, 512, 16, 64
FH, NDEV = 5632, 8
FS = FH // NDEV
DIN = 12816
WS = DIN // NDEV
EPS = 1e-6
MASKV = -1e30
LANE = 128

C_AQ, C_AK, C_AV, C_GQ, C_GK, C_GV, C_GR, C_GA, C_GB, NMAIN = 0, 2048, 2304, 2560, 3584, 4608, 6656, 8704, 10752, 12800
C_LR = 6656
WSTEP, WWIN = 1600, 1616

LR, B1, B2, AEPS, WD, STEP = 0.001, 0.9, 0.999, 1e-08, 0.01, 10

S_N1, S_GB, S_SK, S_GN, S_N2, S_FN, S_LOSS, SMALL_N = 0, 2048, 3072, 3104, 3616, 5664, 7712, 8192
SMALL_ROWS = SMALL_N // LANE
GW2_ROWS = RANK * GH * DK // LANE

MESH = pl.DeviceIdType.MESH


def _dot(a, b, ta=False, tb=False, prec=None):
    dn = (((0,) if ta else (1,), (1,) if tb else (0,)), ((), ()))
    return lax.dot_general(a, b, dn, preferred_element_type=F32, precision=prec)


def _sigmoid(x):
    return 1.0 / (1.0 + jnp.exp(-x))


VMEM_LIMIT = 56 * 1024 * 1024


def _cp(*sem):
    return pltpu.CompilerParams(dimension_semantics=sem, vmem_limit_bytes=VMEM_LIMIT)


def _mm(a, b, *, ta=False, tb=False, tm, tn, tk, out_dtype=F32, res=None, after=None, name):
    M, K = (a.shape[1], a.shape[0]) if ta else a.shape
    N = b.shape[0] if tb else b.shape[1]
    tm, tn, tk = min(tm, M), min(tn, N), min(tk, K)
    nk = K // tk
    assert M % tm == 0 and N % tn == 0 and K % tk == 0
    a_spec = pl.BlockSpec((tk, tm), lambda i, j, k: (k, i)) if ta else pl.BlockSpec((tm, tk), lambda i, j, k: (i, k))
    b_spec = pl.BlockSpec((tn, tk), lambda i, j, k: (j, k)) if tb else pl.BlockSpec((tk, tn), lambda i, j, k: (k, j))
    o_spec = pl.BlockSpec((tm, tn), lambda i, j, k: (i, j))
    has_res = res is not None

    def body(*refs):
        a_ref, b_ref = refs[0], refs[1]
        r_ref = refs[2] if has_res else None
        o_ref = refs[2 + has_res + (after is not None)]
        p = _dot(a_ref[...].astype(BF16), b_ref[...].astype(BF16), ta, tb)
        if nk == 1:
            if has_res:
                p = p + r_ref[...]
            o_ref[...] = p.astype(out_dtype)
        else:
            acc = refs[-1]
            k = pl.program_id(2)

            @pl.when(k == 0)
            def _():
                acc[...] = (p + r_ref[...]) if has_res else p

            @pl.when(k > 0)
            def _():
                acc[...] += p

            @pl.when(k == nk - 1)
            def _():
                o_ref[...] = acc[...].astype(out_dtype)

    return pl.pallas_call(
        body, name=name,
        out_shape=jax.ShapeDtypeStruct((M, N), out_dtype),
        grid=(M // tm, N // tn, nk),
        in_specs=[a_spec, b_spec] + ([o_spec] if has_res else []) + ([pl.BlockSpec(memory_space=pl.ANY)] if after is not None else []),
        out_specs=o_spec,
        scratch_shapes=[pltpu.VMEM((tm, tn), F32)] if nk > 1 else [],
        compiler_params=_cp("parallel", "parallel", "arbitrary"),
    )(*((a, b) + ((res,) if has_res else ()) + ((after,) if after is not None else ())))


def _rmsnorm_fwd(x, w, name, tm=256):
    Tn = x.shape[0]

    def body(x_ref, w_ref, o_ref):
        xv = x_ref[...]
        r = lax.rsqrt(jnp.mean(xv * xv, axis=1, keepdims=True) + EPS)
        o_ref[...] = (xv * r * w_ref[...]).astype(BF16)

    return pl.pallas_call(
        body, name=name, out_shape=jax.ShapeDtypeStruct((Tn, D), BF16), grid=(Tn // tm,),
        in_specs=[pl.BlockSpec((tm, D), lambda i: (i, 0)), pl.BlockSpec((1, D), lambda i: (0, 0))],
        out_specs=pl.BlockSpec((tm, D), lambda i: (i, 0)), compiler_params=_cp("parallel"),
    )(x, w)


def _rmsnorm_bwd(dy, h, w, res, name, tm=256):
    Tn = h.shape[0]

    def body(dy_ref, h_ref, w_ref, res_ref, dh_ref, dhb_ref, dw_ref):
        hv, dyv = h_ref[...], dy_ref[...]
        r = lax.rsqrt(jnp.mean(hv * hv, axis=1, keepdims=True) + EPS)
        g = dyv * w_ref[...]
        dh = res_ref[...] + r * g - hv * (r * r * r * jnp.mean(g * hv, axis=1, keepdims=True))
        dh_ref[...] = dh
        dhb_ref[...] = dh.astype(BF16)
        part = jnp.sum(dyv * hv * r, axis=0, keepdims=True)

        @pl.when(pl.program_id(0) == 0)
        def _():
            dw_ref[...] = part

        @pl.when(pl.program_id(0) > 0)
        def _():
            dw_ref[...] += part

    row = pl.BlockSpec((tm, D), lambda i: (i, 0))
    vec = pl.BlockSpec((1, D), lambda i: (0, 0))
    return pl.pallas_call(
        body, name=name,
        out_shape=(jax.ShapeDtypeStruct((Tn, D), F32), jax.ShapeDtypeStruct((Tn, D), BF16), jax.ShapeDtypeStruct((1, D), F32)),
        grid=(Tn // tm,), in_specs=[row, row, vec, row], out_specs=(row, row, vec), compiler_params=_cp("arbitrary"),
    )(dy, h, w, res)


def _loss_head(h2, wf, tgt, name="loss_head", tm=256):
    Tn = h2.shape[0]

    def body(h_ref, w_ref, t_ref, dh_ref, dhb_ref, dw_ref, loss_ref):
        hv, wv = h_ref[...], w_ref[...]
        r = lax.rsqrt(jnp.mean(hv * hv, axis=1, keepdims=True) + EPS)
        hn = hv * r
        e = hn * wv - t_ref[...]
        dy = e * (1.0 / D)
        g = dy * wv
        dh = r * g - hv * (r * r * r * jnp.mean(g * hv, axis=1, keepdims=True))
        dh_ref[...] = dh
        dhb_ref[...] = dh.astype(BF16)
        part = jnp.sum(dy * hn, axis=0, keepdims=True)
        lpart = (0.5 / D) * jnp.sum(jnp.sum(e * e, axis=1, keepdims=True), axis=0, keepdims=True)

        @pl.when(pl.program_id(0) == 0)
        def _():
            dw_ref[...] = part
            loss_ref[...] = lpart

        @pl.when(pl.program_id(0) > 0)
        def _():
            dw_ref[...] += part
            loss_ref[...] += lpart

    row = pl.BlockSpec((tm, D), lambda i: (i, 0))
    vec = pl.BlockSpec((1, D), lambda i: (0, 0))
    one = pl.BlockSpec((1, 1), lambda i: (0, 0))
    return pl.pallas_call(
        body, name=name,
        out_shape=(jax.ShapeDtypeStruct((Tn, D), F32), jax.ShapeDtypeStruct((Tn, D), BF16), jax.ShapeDtypeStruct((1, D), F32),
                   jax.ShapeDtypeStruct((1, 1), F32)),
        grid=(Tn // tm,), in_specs=[row, vec, row], out_specs=(row, row, vec, one), compiler_params=_cp("arbitrary"),
    )(h2, wf, tgt)


def _attn_mask(n):
    qi = lax.broadcasted_iota(jnp.int32, (NKV, GRP * WIN, 2 * WIN), 1) % WIN
    ki = lax.broadcasted_iota(jnp.int32, (NKV, GRP * WIN, 2 * WIN), 2)
    rel = qi + WIN - ki
    return (rel >= 0) & (rel < WIN) & ((n > 0) | (ki >= WIN))


def _kv_heads(prev_ref, cur_ref):
    return jnp.stack([jnp.concatenate([prev_ref[:, h * HD:(h + 1) * HD], cur_ref[:, h * HD:(h + 1) * HD]], axis=0) for h in range(NKV)])


def _q_heads(ref):
    return jnp.stack([jnp.concatenate([ref[:, (h * GRP + g) * HD:(h * GRP + g + 1) * HD] for g in range(GRP)], axis=0) for h in range(NKV)])


def _attn_probs(q_ref, kc_ref, kp_ref, sink_ref, mask):
    kk = _kv_heads(kp_ref, kc_ref).astype(BF16)
    qs = _q_heads(q_ref).astype(BF16)
    s = jnp.einsum('hqd,hkd->hqk', qs, kk, preferred_element_type=F32) * (HD ** -0.5)
    s = jnp.where(mask, s, MASKV)
    sink = jnp.stack([jnp.concatenate([jnp.full((WIN, 1), sink_ref[0, h * GRP + g], F32) for g in range(GRP)], axis=0) for h in range(NKV)])
    m = jnp.maximum(jnp.max(s, axis=2, keepdims=True), sink)
    e = jnp.exp(s - m)
    es = jnp.exp(sink - m)
    inv = 1.0 / (jnp.sum(e, axis=2, keepdims=True) + es)
    return e * inv, es * inv, qs, kk


def _attn_specs(nb, last):
    cur = lambda n: jnp.minimum(n, last)
    prev = lambda n: jnp.maximum(jnp.minimum(n, last) - 1, 0)
    return [
        pl.BlockSpec((WIN, NQ * HD), lambda n: (cur(n), C_AQ // (NQ * HD))),
        pl.BlockSpec((WIN, NKV * HD), lambda n: (cur(n), C_AK // (NKV * HD))),
        pl.BlockSpec((WIN, NKV * HD), lambda n: (prev(n), C_AK // (NKV * HD))),
        pl.BlockSpec((WIN, NKV * HD), lambda n: (cur(n), C_AV // (NKV * HD))),
        pl.BlockSpec((WIN, NKV * HD), lambda n: (prev(n), C_AV // (NKV * HD))),
    ]


def _attn_fwd(proj, sinks, name="attn_fwd"):
    Tn = proj.shape[0]
    nb = Tn // WIN

    def body(q_ref, kc_ref, kp_ref, vc_ref, vp_ref, sink_ref, o_ref):
        p, _, _, _ = _attn_probs(q_ref, kc_ref, kp_ref, sink_ref, _attn_mask(pl.program_id(0)))
        o = jnp.einsum('hqk,hkd->hqd', p.astype(BF16), _kv_heads(vp_ref, vc_ref).astype(BF16), preferred_element_type=F32)
        for h in range(NKV):
            for g in range(GRP):
                o_ref[:, (h * GRP + g) * HD:(h * GRP + g + 1) * HD] = o[h, g * WIN:(g + 1) * WIN, :]

    return pl.pallas_call(
        body, name=name, out_shape=jax.ShapeDtypeStruct((Tn, D), F32), grid=(nb,),
        in_specs=_attn_specs(nb, nb - 1) + [pl.BlockSpec(memory_space=pltpu.SMEM)],
        out_specs=pl.BlockSpec((WIN, D), lambda n: (n, 0)), compiler_params=_cp("parallel"),
    )(proj, proj, proj, proj, proj, sinks)


def _attn_bwd(proj, sinks, o, do, name="attn_bwd"):
    Tn = proj.shape[0]
    nb = Tn // WIN
    KW = NKV * HD

    def body(q_ref, kc_ref, kp_ref, vc_ref, vp_ref, o_ref, do_ref, sink_ref, dq_ref, dkv_ref, dsk_ref, carry, cur):
        n = pl.program_id(0)

        @pl.when(n == 0)
        def _():
            carry[...] = jnp.zeros_like(carry)
            dsk_ref[...] = jnp.zeros_like(dsk_ref)

        @pl.when(n < nb)
        def _():
            p, ps, qs, kk = _attn_probs(q_ref, kc_ref, kp_ref, sink_ref, _attn_mask(n))
            vv = _kv_heads(vp_ref, vc_ref).astype(BF16)
            dos = _q_heads(do_ref)
            delta = jnp.sum(dos * _q_heads(o_ref), axis=2, keepdims=True)
            dosb = dos.astype(BF16)
            dp = jnp.einsum('hqd,hkd->hqk', dosb, vv, preferred_element_type=F32)
            ds = (p * (dp - delta) * (HD ** -0.5)).astype(BF16)
            dq = jnp.einsum('hqk,hkd->hqd', ds, kk, preferred_element_type=F32)
            dkk = jnp.einsum('hqk,hqd->hkd', ds, qs, preferred_element_type=F32)
            dvv = jnp.einsum('hqk,hqd->hkd', p.astype(BF16), dosb, preferred_element_type=F32)
            dsk = ps * delta
            for h in range(NKV):
                for g in range(GRP):
                    i = h * GRP + g
                    dq_ref[:, i * HD:(i + 1) * HD] = dq[h, g * WIN:(g + 1) * WIN, :].astype(BF16)
                    dsk_ref[:, i:i + 1] -= jnp.sum(dsk[h, g * WIN:(g + 1) * WIN, :], axis=0, keepdims=True)
                dkv_ref[:, h * HD:(h + 1) * HD] = (carry[:, h * HD:(h + 1) * HD] + dkk[h, :WIN, :]).astype(BF16)
                dkv_ref[:, KW + h * HD:KW + (h + 1) * HD] = (carry[:, KW + h * HD:KW + (h + 1) * HD] + dvv[h, :WIN, :]).astype(BF16)
                cur[:, h * HD:(h + 1) * HD] = dkk[h, WIN:, :]
                cur[:, KW + h * HD:KW + (h + 1) * HD] = dvv[h, WIN:, :]
            carry[...] = cur[...]

        @pl.when(n == nb)
        def _():
            dkv_ref[...] = carry[...].astype(BF16)

    last = nb - 1
    row = pl.BlockSpec((WIN, D), lambda n: (jnp.minimum(n, last), 0))
    return pl.pallas_call(
        body, name=name,
        out_shape=(jax.ShapeDtypeStruct((Tn, D), BF16), jax.ShapeDtypeStruct((Tn, 2 * KW), BF16), jax.ShapeDtypeStruct((1, NQ), F32)),
        grid=(nb + 1,),
        in_specs=_attn_specs(nb, last) + [row, row, pl.BlockSpec(memory_space=pltpu.SMEM)],
        out_specs=(row, pl.BlockSpec((WIN, 2 * KW), lambda n: (jnp.maximum(n - 1, 0), 0)), pl.BlockSpec((1, NQ), lambda n: (0, 0))),
        scratch_shapes=[pltpu.VMEM((WIN, 2 * KW), F32), pltpu.VMEM((WIN, 2 * KW), F32)],
        compiler_params=_cp("arbitrary"),
    )(proj, proj, proj, proj, proj, o, do, sinks)


def _tri(lower):
    r = lax.broadcasted_iota(jnp.int32, (GC, GC), 0)
    c = lax.broadcasted_iota(jnp.int32, (GC, GC), 1)
    return r >= c if lower else r <= c


def _per_head(a):
    return jnp.stack([a[:, h * DK:(h + 1) * DK] for h in range(GH)])


def _all_heads(a):
    return jnp.concatenate([a[h] for h in range(GH)], axis=1)


def _gla_gates(lr, w2_ref, gb_ref):
    logit = _dot(lr, w2_ref[...].astype(BF16)) + gb_ref[...]
    la = (jnp.minimum(logit, 0.0) - jnp.log(1.0 + jnp.exp(-jnp.abs(logit)))) * (1.0 / 16.0)
    g = _dot(_tri(True).astype(F32), la, prec=HIGHEST)
    return logit, g


def _bmm(spec, a, b):
    return jnp.einsum(spec, a, b, preferred_element_type=F32)


def _gla_specs(nc, rev):
    idx = (lambda n: nc - 1 - n) if rev else (lambda n: n)
    half = 2 * DK
    return (
        [pl.BlockSpec((GC, half), lambda n, j=j: (idx(n), C_GQ // half + j)) for j in range(2)]
        + [pl.BlockSpec((GC, half), lambda n, j=j: (idx(n), C_GK // half + j)) for j in range(2)]
        + [pl.BlockSpec((GC, DV), lambda n, h=h: (idx(n), C_GV // DV + h)) for h in range(GH)]
        + [pl.BlockSpec((GC, LANE), lambda n: (idx(n), 0)), pl.BlockSpec((LANE, GH * DK), lambda n: (0, 0)),
           pl.BlockSpec((1, GH * DK), lambda n: (0, 0))])


def _gla_heads(refs):
    return (lambda h: refs[h // 2][:, (h % 2) * DK:(h % 2 + 1) * DK], lambda h: refs[2 + h // 2][:, (h % 2) * DK:(h % 2 + 1) * DK],
            lambda h: refs[4 + h][...])


def _gla_fwd(proj, plr, w2p, gb, name="gla_fwd"):
    Tn = proj.shape[0]
    nc = Tn // GC

    def body(*refs):
        qh, kh, vh = _gla_heads(refs)
        lr_ref, w2_ref, gb_ref, o_ref, st_ref, S = refs[8:]

        @pl.when(pl.program_id(0) == 0)
        def _():
            S[...] = jnp.zeros_like(S)

        heads = lambda f: jnp.stack([f(h) for h in range(GH)])
        _, g_all = _gla_gates(lr_ref[...].astype(BF16), w2_ref, gb_ref)
        g = _per_head(g_all)
        gl = g[:, GC - 1:GC, :]
        k = heads(kh)
        v = heads(vh).astype(BF16)
        qd = (heads(qh) * (DK ** -0.5) * jnp.exp(g)).astype(BF16)
        ki = (k * jnp.exp(-g)).astype(BF16)
        ke = (k * jnp.exp(gl - g)).astype(BF16)
        att = jnp.where(_tri(True)[None], _bmm('hid,hjd->hij', qd, ki), 0.0).astype(BF16)
        sp = S[...]
        st_ref[0] = sp
        o = _bmm('hij,hjv->hiv', att, v) + _bmm('hid,hvd->hiv', qd, sp.astype(BF16))
        for h in range(GH):
            o_ref[:, h * DV:(h + 1) * DV] = o[h]
        S[...] = sp * jnp.exp(gl) + _bmm('hjv,hjd->hvd', v, ke)

    return pl.pallas_call(
        body, name=name,
        out_shape=(jax.ShapeDtypeStruct((Tn, GH * DV), F32), jax.ShapeDtypeStruct((nc, GH, DV, DK), F32)),
        grid=(nc,), in_specs=_gla_specs(nc, False),
        out_specs=(pl.BlockSpec((GC, GH * DV), lambda n: (n, 0)), pl.BlockSpec((1, GH, DV, DK), lambda n: (n, 0, 0, 0))),
        scratch_shapes=[pltpu.VMEM((GH, DV, DK), F32)], compiler_params=_cp("arbitrary"),
    )(*([proj] * 8), plr, w2p, gb)


def _gla_bwd(proj, plr, w2p, gb, states, do, name="gla_bwd"):
    Tn = proj.shape[0]
    nc = Tn // GC

    def body(*refs):
        qh, kh, vh = _gla_heads(refs)
        lr_ref, w2_ref, gb_ref, st_ref, do_ref, dqk_ref, dv_ref, dlr_ref, dw2_ref, dgb_ref, dS = refs[8:]

        @pl.when(pl.program_id(0) == 0)
        def _():
            dS[...] = jnp.zeros_like(dS)
            dw2_ref[...] = jnp.zeros_like(dw2_ref)
            dgb_ref[...] = jnp.zeros_like(dgb_ref)

        heads = lambda f: jnp.stack([f(h) for h in range(GH)])
        lr = lr_ref[...].astype(BF16)
        causal = _tri(True)[None]
        last_row = lax.broadcasted_iota(jnp.int32, (GH, GC, DK), 1) == GC - 1
        logit, g_all = _gla_gates(lr, w2_ref, gb_ref)
        g = _per_head(g_all)
        gl = g[:, GC - 1:GC, :]
        egl = jnp.exp(gl)
        eg, eng, ege = jnp.exp(g), jnp.exp(-g), jnp.exp(gl - g)
        k = heads(kh)
        v = heads(vh).astype(BF16)
        dob = heads(lambda h: do_ref[:, h * DV:(h + 1) * DV]).astype(BF16)
        qd = heads(qh) * (DK ** -0.5) * eg
        ki = k * eng
        ke = k * ege
        qdb, kib, keb = qd.astype(BF16), ki.astype(BF16), ke.astype(BF16)
        att = jnp.where(causal, _bmm('hid,hjd->hij', qdb, kib), 0.0).astype(BF16)
        datt = jnp.where(causal, _bmm('hiv,hjv->hij', dob, v), 0.0).astype(BF16)
        sp = st_ref[0]
        dsn = dS[...]
        dsnb = dsn.astype(BF16)
        dv = (_bmm('hij,hiv->hjv', att, dob) + _bmm('hjd,hvd->hjv', keb, dsnb)).astype(BF16)
        dqd = _bmm('hij,hjd->hid', datt, kib) + _bmm('hiv,hvd->hid', dob, sp.astype(BF16))
        dki = _bmm('hij,hid->hjd', datt, qdb)
        dke = _bmm('hjv,hvd->hjd', v, dsnb)
        ddec = jnp.sum(dsn * sp, axis=1, keepdims=True)
        dS[...] = dsn * egl + _bmm('hiv,hid->hvd', dob, qdb)
        dke_ke = dke * ke
        dgl = jnp.sum(dke_ke, axis=1, keepdims=True) + ddec * egl
        dg = dqd * qd - dki * ki - dke_ke + jnp.where(last_row, dgl, 0.0)
        dq = (dqd * ((DK ** -0.5) * eg)).astype(BF16)
        dk = (dki * eng + dke * ege).astype(BF16)
        for h in range(GH):
            dv_ref[:, h * DV:(h + 1) * DV] = dv[h]
            dqk_ref[:, h * DK:(h + 1) * DK] = dq[h]
            dqk_ref[:, GH * DK + h * DK:GH * DK + (h + 1) * DK] = dk[h]
        dla = _dot(_tri(False).astype(F32), _all_heads(dg), prec=HIGHEST)
        dlogit = dla * (1.0 / 16.0) * _sigmoid(-logit)
        dlb = dlogit.astype(BF16)
        dlr_ref[...] = _dot(dlb, w2_ref[...].astype(BF16), tb=True).astype(BF16)
        dw2_ref[...] += _dot(lr, dlb, ta=True)
        dgb_ref[...] += jnp.sum(dlogit, axis=0, keepdims=True)

    rev = lambda n: nc - 1 - n
    row = pl.BlockSpec((GC, GH * DV), lambda n: (rev(n), 0))
    return pl.pallas_call(
        body, name=name,
        out_shape=(jax.ShapeDtypeStruct((Tn, 2 * GH * DK), BF16), jax.ShapeDtypeStruct((Tn, GH * DV), BF16),
                   jax.ShapeDtypeStruct((Tn, LANE), BF16), jax.ShapeDtypeStruct((LANE, GH * DK), F32),
                   jax.ShapeDtypeStruct((1, GH * DK), F32)),
        grid=(nc,),
        in_specs=_gla_specs(nc, True) + [pl.BlockSpec((1, GH, DV, DK), lambda n: (rev(n), 0, 0, 0)), row],
        out_specs=(row, row, pl.BlockSpec((GC, LANE), lambda n: (rev(n), 0)), pl.BlockSpec((LANE, GH * DK), lambda n: (0, 0)),
                   pl.BlockSpec((1, GH * DK), lambda n: (0, 0))),
        scratch_shapes=[pltpu.VMEM((GH, DV, DK), F32)], compiler_params=_cp("arbitrary"),
    )(*([proj] * 8), plr, w2p, gb, states, do)


def _merge_specs(tm):
    row = pl.BlockSpec((tm, D), lambda i: (i, 0))
    gates = [pl.BlockSpec((tm, DV), lambda i, j=c // DV + h: (i, j)) for c in (C_GR, C_GA, C_GB) for h in range(GH)]
    return row, gates, pl.BlockSpec((1, DV), lambda i: (0, 0))


def _merge_fwd(a, go, proj, gnw, name="merge_fwd", tm=256):
    Tn = a.shape[0]

    def body(a_ref, go_ref, *rest):
        gates, w_ref, m_ref = rest[:3 * GH], rest[3 * GH], rest[3 * GH + 1]
        for h in range(GH):
            sl = slice(h * DV, (h + 1) * DV)
            gov = go_ref[:, sl]
            r = lax.rsqrt(jnp.mean(gov * gov, axis=1, keepdims=True) + EPS)
            gr = gates[h][...]
            g2 = gov * r * w_ref[...] * (gr * _sigmoid(gr))
            m_ref[:, sl] = (_sigmoid(gates[GH + h][...]) * a_ref[:, sl] + _sigmoid(gates[2 * GH + h][...]) * g2).astype(BF16)

    row, gates, vec = _merge_specs(tm)
    return pl.pallas_call(
        body, name=name, out_shape=jax.ShapeDtypeStruct((Tn, D), BF16), grid=(Tn // tm,),
        in_specs=[row, row] + gates + [vec], out_specs=row, compiler_params=_cp("parallel"),
    )(a, go, *([proj] * (3 * GH)), gnw)


def _merge_bwd(dm, a, go, proj, gnw, name="merge_bwd", tm=256):
    Tn = a.shape[0]

    def body(dm_ref, a_ref, go_ref, *rest):
        gates = rest[:3 * GH]
        w_ref, da_ref, dgo_ref, dg_ref, dw_ref = rest[3 * GH:]
        wv = w_ref[...]
        dw = jnp.zeros((1, DV), F32)
        for h in range(GH):
            sl = slice(h * DV, (h + 1) * DV)
            dmv, av, gov, gr = dm_ref[:, sl], a_ref[:, sl], go_ref[:, sl], gates[h][...]
            sa, sb, sg = _sigmoid(gates[GH + h][...]), _sigmoid(gates[2 * GH + h][...]), _sigmoid(gr)
            r = lax.rsqrt(jnp.mean(gov * gov, axis=1, keepdims=True) + EPS)
            gn0 = gov * r
            gn = gn0 * wv
            silu = gr * sg
            dg2 = dmv * sb
            da_ref[:, sl] = dmv * sa
            dg_ref[:, D + h * DV:D + (h + 1) * DV] = (dmv * av * sa * (1.0 - sa)).astype(BF16)
            dg_ref[:, 2 * D + h * DV:2 * D + (h + 1) * DV] = (dg2 * gn * silu * (1.0 - sb)).astype(BF16)
            dg_ref[:, sl] = (dg2 * gn * (sg * (1.0 + gr * (1.0 - sg)))).astype(BF16)
            dgn = dg2 * silu
            dw = dw + jnp.sum(dgn * gn0, axis=0, keepdims=True)
            gg = dgn * wv
            dgo_ref[:, sl] = r * gg - gov * (r * r * r * jnp.mean(gg * gov, axis=1, keepdims=True))

        @pl.when(pl.program_id(0) == 0)
        def _():
            dw_ref[...] = dw

        @pl.when(pl.program_id(0) > 0)
        def _():
            dw_ref[...] += dw

    row, gates, vec = _merge_specs(tm)
    return pl.pallas_call(
        body, name=name,
        out_shape=(jax.ShapeDtypeStruct((Tn, D), F32), jax.ShapeDtypeStruct((Tn, D), F32), jax.ShapeDtypeStruct((Tn, 3 * D), BF16),
                   jax.ShapeDtypeStruct((1, DV), F32)),
        grid=(Tn // tm,), in_specs=[row, row, row] + gates + [vec],
        out_specs=(row, row, pl.BlockSpec((tm, 3 * D), lambda i: (i, 0)), vec), compiler_params=_cp("arbitrary"),
    )(dm, a, go, *([proj] * (3 * GH)), gnw)


def _ffn_up(v2, wgt, wut, name="ffn_up", tm=1024, tn=512):
    Tn = v2.shape[0]
    tm = min(tm, Tn)

    def body(v_ref, wg_ref, wu_ref, a_ref, b_ref, ff_ref):
        vv = v_ref[...]
        a = _dot(vv, wg_ref[...], tb=True)
        b = _dot(vv, wu_ref[...], tb=True)
        a_ref[...] = a.astype(BF16)
        b_ref[...] = b.astype(BF16)
        ff_ref[...] = (a * _sigmoid(a) * b).astype(BF16)

    w = pl.BlockSpec((tn, D), lambda j, i: (j, 0))
    act = pl.BlockSpec((tm, tn), lambda j, i: (i, j))
    return pl.pallas_call(
        body, name=name,
        out_shape=(jax.ShapeDtypeStruct((Tn, FH), BF16), jax.ShapeDtypeStruct((Tn, FH), BF16), jax.ShapeDtypeStruct((Tn, FH), BF16)),
        grid=(FH // tn, Tn // tm), in_specs=[pl.BlockSpec((tm, D), lambda j, i: (i, 0)), w, w], out_specs=(act, act, act),
        compiler_params=_cp("parallel", "parallel"),
    )(v2, wgt, wut)


def _ffn_dact(dh2b, wd, a, b, name="ffn_dact", tm=1024, tn=512):
    Tn = dh2b.shape[0]
    tm = min(tm, Tn)

    def body(d_ref, w_ref, a_ref, b_ref, da_ref, db_ref):
        dff = _dot(d_ref[...], w_ref[...], tb=True)
        av = a_ref[...].astype(F32)
        sg = _sigmoid(av)
        da_ref[...] = (dff * b_ref[...].astype(F32) * (sg * (1.0 + av * (1.0 - sg)))).astype(BF16)
        db_ref[...] = (dff * (av * sg)).astype(BF16)

    act = pl.BlockSpec((tm, tn), lambda j, i: (i, j))
    return pl.pallas_call(
        body, name=name,
        out_shape=(jax.ShapeDtypeStruct((Tn, FH), BF16), jax.ShapeDtypeStruct((Tn, FH), BF16)),
        grid=(FH // tn, Tn // tm),
        in_specs=[pl.BlockSpec((tm, D), lambda j, i: (i, 0)), pl.BlockSpec((tn, D), lambda j, i: (j, 0)), act, act],
        out_specs=(act, act), compiler_params=_cp("parallel", "parallel"),
    )(dh2b, wd, a, b)


def _adam_math(w, g, m, v):
    m2 = B1 * m + (1.0 - B1) * g
    v2 = B2 * v + (1.0 - B2) * (g * g)
    mh = m2 / (1.0 - B1 ** STEP)
    vh = v2 / (1.0 - B2 ** STEP)
    return -LR * (mh / (jnp.sqrt(vh) + AEPS) + WD * w), m2, v2


def _sum_blocks(o_ref, p_ref):
    g = o_ref[...].astype(F32)
    for j in range(p_ref.shape[0]):
        g = g + p_ref[j].astype(F32)
    return g


def _adamw(w, m, v, psums, parts, chip_idx, name, tr):
    R, C = w.shape

    def body(s_ref, w_ref, m_ref, v_ref, o_ref, p_ref, g_ref, d_ref, m2_ref, v2_ref):
        g = _sum_blocks(o_ref, p_ref)
        d, m2, v2 = _adam_math(w_ref[...], g, m_ref[...], v_ref[...])
        g_ref[...] = g
        d_ref[...] = d
        m2_ref[...] = m2
        v2_ref[...] = v2

    blk = pl.BlockSpec((tr, C), lambda i, s: (i, 0))
    out = jax.ShapeDtypeStruct((R, C), F32)
    grid_spec = pltpu.PrefetchScalarGridSpec(
        num_scalar_prefetch=1, grid=(R // tr,),
        in_specs=[blk, blk, blk, pl.BlockSpec((None, tr, C), lambda i, s: (s[0], i, 0)),
                  pl.BlockSpec((parts.shape[0], tr, C), lambda i, s: (0, i, 0))],
        out_specs=(blk, blk, blk, blk),
    )
    return pl.pallas_call(body, name=name, out_shape=(out, out, out, out), grid_spec=grid_spec, compiler_params=_cp("parallel"),
                          )(chip_idx, w, m, v, psums, parts)


def _adamw_given(w, m, v, g, name, tr, tc):
    R, C = w.shape

    def body(w_ref, m_ref, v_ref, g_ref, d_ref, m2_ref, v2_ref):
        d, m2, v2 = _adam_math(w_ref[...], g_ref[...], m_ref[...], v_ref[...])
        d_ref[...] = d
        m2_ref[...] = m2
        v2_ref[...] = v2

    blk = pl.BlockSpec((tr, tc), lambda i, j: (i, j))
    out = jax.ShapeDtypeStruct(w.shape, F32)
    return pl.pallas_call(body, name=name, out_shape=(out, out, out), grid=(pl.cdiv(R, tr), C // tc), in_specs=[blk] * 4,
                          out_specs=(blk, blk, blk), compiler_params=_cp("parallel", "parallel"))(w, m, v, g)


def _sum_parts(psums, parts, chip_idx, name, tr, tc):
    _, R, C = psums.shape

    def body(s_ref, o_ref, p_ref, g_ref):
        g_ref[...] = _sum_blocks(o_ref, p_ref)

    grid_spec = pltpu.PrefetchScalarGridSpec(
        num_scalar_prefetch=1, grid=(R // tr, C // tc),
        in_specs=[pl.BlockSpec((None, tr, tc), lambda i, j, s: (s[0], i, j)),
                  pl.BlockSpec((parts.shape[0], tr, tc), lambda i, j, s: (0, i, j))],
        out_specs=pl.BlockSpec((tr, tc), lambda i, j, s: (i, j)),
    )
    return pl.pallas_call(body, name=name, out_shape=jax.ShapeDtypeStruct((R, C), F32), grid_spec=grid_spec,
                          compiler_params=_cp("parallel", "parallel"))(chip_idx, psums, parts)


def _adamw_plain(w, m, v, g, name):
    def body(w_ref, m_ref, v_ref, g_ref, d_ref, m2_ref, v2_ref):
        d, m2, v2 = _adam_math(w_ref[...], g_ref[...], m_ref[...], v_ref[...])
        d_ref[...] = d
        m2_ref[...] = m2
        v2_ref[...] = v2

    out = jax.ShapeDtypeStruct(w.shape, F32)
    return pl.pallas_call(body, name=name, out_shape=(out, out, out))(w, m, v, g)


def _sum_devices(pack_all, name="sum_small"):
    def body(p_ref, o_ref):
        s = p_ref[0]
        for k in range(1, NDEV):
            s = s + p_ref[k]
        o_ref[...] = s

    return pl.pallas_call(body, name=name, out_shape=jax.ShapeDtypeStruct(pack_all.shape[1:], F32))(pack_all)


def _pair_add(g5, recv, c_idx, name, tr):
    _, _, R, C = g5.shape

    def body(c_ref, g_ref, r_ref, o_ref):
        o_ref[...] = (g_ref[...].astype(F32) + r_ref[...].astype(F32)).astype(BF16)

    grid_spec = pltpu.PrefetchScalarGridSpec(
        num_scalar_prefetch=1, grid=(4, R // tr),
        in_specs=[pl.BlockSpec((None, None, tr, C), lambda q, i, c: (q, c[0], i, 0)), pl.BlockSpec((None, tr, C), lambda q, i, c: (q, i, 0))],
        out_specs=pl.BlockSpec((None, tr, C), lambda q, i, c: (q, i, 0)),
    )
    return pl.pallas_call(
        body, name=name, out_shape=jax.ShapeDtypeStruct((4, R, C), BF16), grid_spec=grid_spec,
        compiler_params=_cp("parallel", "parallel"),
    )(c_idx, g5, recv)


_ANY = pl.BlockSpec(memory_space=pl.ANY)


def _mesh_pos():
    x, y, c = lax.axis_index("x"), lax.axis_index("y"), lax.axis_index("c")
    return x, y, c, [(1 - x, y), (x, 1 - y), (1 - x, 1 - y)]


def _gather_small(pack, name="gather_small"):
    def body(pk, pk_all, psend, precv, loc):
        x, y, c, chips = _mesh_pos()
        me_slot = 4 * x + 2 * y + c
        sib = (x, y, 1 - c)
        own = pltpu.make_async_copy(pk, pk_all.at[me_slot], loc)
        own.start()
        peers = [sib] + [(*chip, c) for chip in chips] + [(*chip, 1 - c) for chip in chips]
        small = [pltpu.make_async_remote_copy(src_ref=pk, dst_ref=pk_all.at[me_slot], send_sem=psend.at[k], recv_sem=precv.at[k],
                                              device_id=p, device_id_type=MESH) for k, p in enumerate(peers)]
        for d in small:
            d.start()
        for k, p in enumerate(peers):
            pltpu.make_async_remote_copy(src_ref=pk, dst_ref=pk_all.at[4 * p[0] + 2 * p[1] + p[2]], send_sem=psend.at[k],
                                         recv_sem=precv.at[k], device_id=p, device_id_type=MESH).wait_recv()
        for d in small:
            d.wait_send()
        own.wait()

    return pl.pallas_call(
        body, name=name, out_shape=jax.ShapeDtypeStruct((NDEV,) + pack.shape, pack.dtype), in_specs=[_ANY], out_specs=_ANY,
        scratch_shapes=[pltpu.SemaphoreType.DMA((7,)), pltpu.SemaphoreType.DMA((7,)), pltpu.SemaphoreType.DMA(())],
    )(pack)


def _main_row(g):
    return g if g < C_LR else g - RANK


def _window_pieces(lo, hi):
    out = []
    for a, b, where in ((lo, min(hi, C_LR), "main"), (max(lo, C_LR), min(hi, C_LR + RANK), "lr"), (max(lo, C_LR + RANK), hi, "main")):
        if a < b:
            out.append((a, b, where, _main_row(a) if where == "main" else a - C_LR))
    return out


def _assemble_w_in(windows, own, name="assemble_w_in"):
    edges = NDEV - 1

    def body(b_ref, own_ref, main_ref, lr_ref, buf, ebuf, in_sems, out_sems, esems):
        dev = 4 * lax.axis_index("x") + 2 * lax.axis_index("y") + lax.axis_index("c")

        def load(k):
            return pltpu.make_async_copy(b_ref.at[k], buf.at[k % 2], in_sems.at[k % 2])

        def start_load(k):
            pl.when(dev == k)(pltpu.make_async_copy(own_ref, buf.at[k % 2], in_sems.at[k % 2]).start)
            pl.when(dev != k)(load(k).start)

        lr_ref[RANK:, :] = jnp.zeros((LANE - RANK, D), BF16)
        start_load(0)
        pending, edge_out = [], []
        for k in range(NDEV):
            s = k % 2
            load(k).wait()
            if k:
                ebuf[k - 1] = buf[1 - s, WSTEP:WWIN, :] + buf[s, 0:16, :]
                edge_out.append(pltpu.make_async_copy(ebuf.at[k - 1], main_ref.at[pl.ds(_main_row(WSTEP * k), 16)], esems.at[k - 1]))
                edge_out[-1].start()
                for d in pending:
                    d.wait()
            if k + 1 < NDEV:
                start_load(k + 1)
            pending = []
            lo = WSTEP * k + (16 if k else 0)
            hi = WSTEP * k + (WWIN if k == NDEV - 1 else WSTEP)
            for a, b, where, dst in _window_pieces(lo, hi):
                if where == "lr":
                    lr_ref[dst:dst + b - a, :] = buf[s, a - WSTEP * k:b - WSTEP * k, :]
                else:
                    pending.append(pltpu.make_async_copy(buf.at[s, pl.ds(a - WSTEP * k, b - a)], main_ref.at[pl.ds(dst, b - a)],
                                                         out_sems.at[2 * s + len(pending)]))
                    pending[-1].start()
        for d in pending + edge_out:
            d.wait()

    return pl.pallas_call(
        body, name=name,
        out_shape=(jax.ShapeDtypeStruct((NMAIN, D), BF16), jax.ShapeDtypeStruct((LANE, D), BF16)),
        in_specs=[_ANY, _ANY], out_specs=(_ANY, pl.BlockSpec(memory_space=pltpu.VMEM)),
        scratch_shapes=[pltpu.VMEM((2, WWIN, D), BF16), pltpu.VMEM((edges, 16, D), BF16), pltpu.SemaphoreType.DMA((2,)),
                        pltpu.SemaphoreType.DMA((4,)), pltpu.SemaphoreType.DMA((edges,))],
        compiler_params=pltpu.CompilerParams(vmem_limit_bytes=VMEM_LIMIT),
    )(windows, own)


def _disassemble_exchange(d_main, d_lr, name="disassemble_exchange"):
    def body(main_ref, lr_ref, mine_ref, recv_ref, buf, in_sems, keep_sems, send_sems, recv_sems):
        x, y, c, _ = _mesh_pos()
        sib = (x, y, 1 - c)

        def loads(k):
            s, out = k % 3, []
            for a, b, where, src0 in _window_pieces(WSTEP * k, WSTEP * k + WWIN):
                if where == "main":
                    out.append(pltpu.make_async_copy(main_ref.at[pl.ds(src0, b - a)], buf.at[s, pl.ds(a - WSTEP * k, b - a)],
                                                     in_sems.at[2 * s + len(out)]))
            return out

        def keep(k):
            return pltpu.make_async_copy(buf.at[k % 3], mine_ref.at[k // 2], keep_sems.at[k % 3])

        def send(k):
            return _rcopy(buf.at[k % 3], recv_ref.at[k // 2], send_sems.at[k % 3], recv_sems.at[k // 2], sib)

        def store_start(k):
            pl.when(c == k % 2)(keep(k).start)
            pl.when(c != k % 2)(send(k).start)

        def store_wait(k):
            pl.when(c == k % 2)(keep(k).wait)
            pl.when(c != k % 2)(send(k).wait_send)

        for k in range(2):
            for d in loads(k):
                d.start()
        for k in range(NDEV):
            for d in loads(k):
                d.wait()
            for a, b, where, src0 in _window_pieces(WSTEP * k, WSTEP * k + WWIN):
                if where == "lr":
                    buf[k % 3, a - WSTEP * k:b - WSTEP * k, :] = lr_ref[src0:src0 + b - a, :]
            store_start(k)
            if k + 2 < NDEV:
                if k:
                    store_wait(k - 1)
                for d in loads(k + 2):
                    d.start()
        for k in range(NDEV - 3, NDEV):
            store_wait(k)
        for chip in range(NDEV // 2):
            _rcopy(buf.at[0], recv_ref.at[chip], send_sems.at[0], recv_sems.at[chip], sib).wait_recv()

    half = jax.ShapeDtypeStruct((NDEV // 2, WWIN, D), BF16)
    return pl.pallas_call(
        body, name=name, out_shape=(half, half),
        in_specs=[_ANY, pl.BlockSpec(memory_space=pltpu.VMEM)], out_specs=(_ANY, _ANY),
        scratch_shapes=[pltpu.VMEM((3, WWIN, D), BF16), pltpu.SemaphoreType.DMA((6,)), pltpu.SemaphoreType.DMA((3,)),
                        pltpu.SemaphoreType.DMA((3,)), pltpu.SemaphoreType.DMA((NDEV // 2,))],
        compiler_params=pltpu.CompilerParams(vmem_limit_bytes=VMEM_LIMIT),
    )(d_main, d_lr)


def _add_blocks(a, b, name, tr):
    _, R, C = a.shape

    def body(a_ref, b_ref, o_ref):
        o_ref[...] = (a_ref[...].astype(F32) + b_ref[...].astype(F32)).astype(BF16)

    blk = pl.BlockSpec((None, tr, C), lambda q, i: (q, i, 0))
    return pl.pallas_call(body, name=name, out_shape=jax.ShapeDtypeStruct(a.shape, BF16), grid=(a.shape[0], R // tr),
                          in_specs=[blk, blk], out_specs=blk, compiler_params=_cp("parallel", "parallel"))(a, b)


_HBM = pl.BlockSpec(memory_space=pltpu.HBM)
_SEM = pl.BlockSpec(memory_space=pltpu.SEMAPHORE)
_VMEM = pl.BlockSpec(memory_space=pltpu.VMEM)
_SIDE = pltpu.CompilerParams(has_side_effects=pltpu.SideEffectType.DATAFLOW_SIDE_EFFECTING)
_TOKEN = jax.ShapeDtypeStruct((8, LANE), F32)


def _hbm(a):
    return pltpu.with_memory_space_constraint(a, pltpu.HBM)


def _hbm_like(arrs):
    return tuple(pltpu.HBM(a.shape, a.dtype) for a in arrs)


def _tie(x, token):
    return x + token[0, 0].astype(x.dtype)


def _chip_copies(ins, lands, send, recv, nrel):
    x, y, c, chips = _mesh_pos()
    first = [sum(nrel[:a]) for a in range(len(ins))]
    return [pltpu.make_async_remote_copy(src_ref=ins[a].at[2 * chip[0] + chip[1]], dst_ref=lands[a].at[j], send_sem=send.at[first[a] + j],
                                         recv_sem=recv.at[first[a] + j], device_id=(*chip, c), device_id_type=MESH)
            for a in range(len(ins)) for j, chip in enumerate(chips[:nrel[a]])]


def _chip_start(psums, name, nrel=None):
    n = len(psums)
    nrel = nrel or [3] * n
    lands = [lax.empty((r,) + p.shape[1:], p.dtype) for r, p in zip(nrel, psums)]

    def body(*refs):
        for d in _chip_copies(refs[:n], refs[n:2 * n], refs[2 * n], refs[2 * n + 1], nrel):
            d.start()
        refs[-1][...] = jnp.zeros_like(refs[-1])

    sems = pltpu.SemaphoreType.DMA((sum(nrel),))
    out = pl.pallas_call(
        body, name=name, out_shape=(sems, sems) + _hbm_like(psums) + _hbm_like(lands) + (_TOKEN,),
        in_specs=[_HBM] * (2 * n), out_specs=(_SEM, _SEM) + (_HBM,) * (2 * n) + (_VMEM,),
        input_output_aliases={i: 2 + i for i in range(2 * n)}, compiler_params=_SIDE,
    )(*[_hbm(a) for a in list(psums) + lands])
    return out[0], out[1], list(out[2:2 + n]), list(out[2 + n:2 + 2 * n]), out[-1]


def _chip_wait(send, recv, psums, lands, after, name):
    n = len(psums)
    nrel = [l.shape[0] for l in lands]

    def body(*refs):
        for d in _chip_copies(refs[:n], refs[n:2 * n], refs[2 * n], refs[2 * n + 1], nrel):
            d.wait_send()
            d.wait_recv()

    out = pl.pallas_call(
        body, name=name, out_shape=_hbm_like(psums) + _hbm_like(lands),
        in_specs=[_HBM] * (2 * n) + [_SEM, _SEM, _ANY], out_specs=(_HBM,) * (2 * n),
        input_output_aliases={i: i for i in range(2 * n)}, compiler_params=_SIDE,
    )(*psums, *lands, send, recv, after)
    return list(out[:n]), list(out[n:])


def _hop_pos():
    x, y, c, _ = _mesh_pos()
    north = c == 1
    via = (jnp.where(north, 1 - x, x), jnp.where(north, y, 1 - y))
    return (*via, c), 2 * (1 - x) + (1 - y), jnp.where(north, 2 * x + (1 - y), 2 * (1 - x) + y)


def _hop_copies(ins, lands, send, recv):
    to, mine, _ = _hop_pos()
    return [pltpu.make_async_remote_copy(src_ref=ins[a].at[mine], dst_ref=lands[a], send_sem=send.at[a], recv_sem=recv.at[a],
                                         device_id=to, device_id_type=MESH) for a in range(len(ins))]


def _hop_start(psums, name):
    n = len(psums)
    lands = [lax.empty(p.shape[1:], p.dtype) for p in psums]

    def body(*refs):
        for d in _hop_copies(refs[:n], refs[n:2 * n], refs[2 * n], refs[2 * n + 1]):
            d.start()
        refs[-1][...] = jnp.zeros_like(refs[-1])

    sems = pltpu.SemaphoreType.DMA((n,))
    out = pl.pallas_call(
        body, name=name, out_shape=(sems, sems) + _hbm_like(psums) + _hbm_like(lands) + (_TOKEN,),
        in_specs=[_HBM] * (2 * n), out_specs=(_SEM, _SEM) + (_HBM,) * (2 * n) + (_VMEM,),
        input_output_aliases={i: 2 + i for i in range(2 * n)}, compiler_params=_SIDE,
    )(*[_hbm(a) for a in list(psums) + lands])
    return out[0], out[1], list(out[2:2 + n]), list(out[2 + n:2 + 2 * n]), out[-1]


def _hop_wait(send, recv, psums, lands, after, name):
    n = len(psums)

    def body(*refs):
        for d in _hop_copies(refs[:n], refs[n:2 * n], refs[2 * n], refs[2 * n + 1]):
            d.wait_send()
            d.wait_recv()

    out = pl.pallas_call(
        body, name=name, out_shape=_hbm_like(psums) + _hbm_like(lands),
        in_specs=[_HBM] * (2 * n) + [_SEM, _SEM, _ANY], out_specs=(_HBM,) * (2 * n),
        input_output_aliases={i: i for i in range(2 * n)}, compiler_params=_SIDE,
    )(*psums, *lands, send, recv, after)
    return list(out[:n]), list(out[n:])


def _hop_add(psums, land, idx, name, tr):
    _, R, C = psums.shape

    def body(s_ref, p_ref, l_ref, o_ref):
        o_ref[...] = (p_ref[...].astype(F32) + l_ref[...].astype(F32)).astype(BF16)

    blk = pl.BlockSpec((None, tr, C), lambda i, s: (s[0], i, 0))
    grid_spec = pltpu.PrefetchScalarGridSpec(num_scalar_prefetch=1, grid=(R // tr,),
                                             in_specs=[blk, pl.BlockSpec((tr, C), lambda i, s: (i, 0))], out_specs=blk)
    return pl.pallas_call(body, name=name, out_shape=jax.ShapeDtypeStruct(psums.shape, BF16), grid_spec=grid_spec,
                          input_output_aliases={1: 0}, compiler_params=_cp("parallel"))(idx, psums, land)


def _pair_copies(ins, lands, send, recv):
    x, y, c, _ = _mesh_pos()
    return [pltpu.make_async_remote_copy(src_ref=ins[a].at[:, 1 - c], dst_ref=lands[a], send_sem=send.at[a], recv_sem=recv.at[a],
                                         device_id=(x, y, 1 - c), device_id_type=MESH) for a in range(len(ins))]


def _pair_start(grads, name):
    n = len(grads)
    lands = [lax.empty((4,) + g.shape[2:], g.dtype) for g in grads]

    def body(*refs):
        for d in _pair_copies(refs[:n], refs[n:2 * n], refs[2 * n], refs[2 * n + 1]):
            d.start()
        refs[-1][...] = jnp.zeros_like(refs[-1])

    sems = pltpu.SemaphoreType.DMA((n,))
    out = pl.pallas_call(
        body, name=name, out_shape=(sems, sems) + _hbm_like(grads) + _hbm_like(lands) + (_TOKEN,),
        in_specs=[_HBM] * (2 * n), out_specs=(_SEM, _SEM) + (_HBM,) * (2 * n) + (_VMEM,),
        input_output_aliases={i: 2 + i for i in range(2 * n)}, compiler_params=_SIDE,
    )(*[_hbm(a) for a in list(grads) + lands])
    return out[0], out[1], list(out[2:2 + n]), list(out[2 + n:2 + 2 * n]), out[-1]


def _pair_wait(send, recv, grads, lands, after, name):
    n = len(grads)

    def body(*refs):
        for d in _pair_copies(refs[:n], refs[n:2 * n], refs[2 * n], refs[2 * n + 1]):
            d.wait_send()
            d.wait_recv()

    out = pl.pallas_call(
        body, name=name, out_shape=_hbm_like(grads) + _hbm_like(lands),
        in_specs=[_HBM] * (2 * n) + [_SEM, _SEM, _ANY], out_specs=(_HBM,) * (2 * n),
        input_output_aliases={i: i for i in range(2 * n)}, compiler_params=_SIDE,
    )(*grads, *lands, send, recv, after)
    return list(out[:n]), list(out[n:])


def _slot(chip, c):
    return 4 * chip[0] + 2 * chip[1] + c


def _gather_start(shards, lands, after, name):
    n = len(shards)

    def body(*refs):
        src, land, send, recv = refs[:n], refs[n:2 * n], refs[2 * n + 1], refs[2 * n + 2]
        x, y, c, chips = _mesh_pos()
        for a in range(n):
            for k, to in enumerate([(x, y, 1 - c)] + [(*chip, c) for chip in chips]):
                pltpu.make_async_remote_copy(src_ref=src[a], dst_ref=land[a].at[_slot((x, y), c)], send_sem=send.at[4 * a + k],
                                             recv_sem=recv.at[4 * a + k], device_id=to, device_id_type=MESH).start()
        refs[-1][...] = jnp.zeros_like(refs[-1])

    sems = pltpu.SemaphoreType.DMA((4 * n,))
    out = pl.pallas_call(
        body, name=name, out_shape=(sems, sems) + _hbm_like(shards) + _hbm_like(lands) + (_TOKEN,),
        in_specs=[_HBM] * (2 * n) + [_ANY], out_specs=(_SEM, _SEM) + (_HBM,) * (2 * n) + (_VMEM,),
        input_output_aliases={i: 2 + i for i in range(2 * n)}, compiler_params=_SIDE,
    )(*[_hbm(a) for a in list(shards) + list(lands)], after)
    return out[0], out[1], list(out[2:2 + n]), list(out[2 + n:2 + 2 * n]), out[-1]


def _gather_pass(lands, recv, after, name, first=0):
    n = len(lands)

    def body(*refs):
        land, recv1 = refs[:n], refs[n]
        send2, recv2 = refs[n + 2], refs[n + 3]
        x, y, c, chips = _mesh_pos()
        for a in range(n):
            for j, chip in enumerate(chips):
                blk = land[a].at[_slot(chip, c)]
                pltpu.make_async_remote_copy(src_ref=blk, dst_ref=blk, send_sem=send2.at[3 * a + j], recv_sem=recv1.at[4 * (first + a) + 1 + j],
                                             device_id=(*chip, c), device_id_type=MESH).wait_recv()
                pltpu.make_async_remote_copy(src_ref=blk, dst_ref=blk, send_sem=send2.at[3 * a + j], recv_sem=recv2.at[3 * a + j],
                                             device_id=(x, y, 1 - c), device_id_type=MESH).start()
        refs[-1][...] = jnp.zeros_like(refs[-1])

    sems = pltpu.SemaphoreType.DMA((3 * n,))
    out = pl.pallas_call(
        body, name=name, out_shape=(sems, sems) + _hbm_like(lands) + (_TOKEN,),
        in_specs=[_HBM] * n + [_SEM, _ANY], out_specs=(_SEM, _SEM) + (_HBM,) * n + (_VMEM,),
        input_output_aliases={i: 2 + i for i in range(n)}, compiler_params=_SIDE,
    )(*lands, recv, after)
    return out[0], out[1], list(out[2:2 + n]), out[-1]


def _gather_wait(shards, lands, send, recv, send2, recv2, after, name, first=0):
    n = len(lands)

    def body(*refs):
        src, land = refs[:n], refs[n:2 * n]
        send1, recv1, snd2, rcv2 = refs[2 * n:2 * n + 4]
        x, y, c, chips = _mesh_pos()
        sib = (x, y, 1 - c)
        for a in range(n):
            for k in range(4):
                pltpu.make_async_remote_copy(src_ref=src[a], dst_ref=land[a].at[_slot((x, y), c)], send_sem=send1.at[4 * (first + a) + k],
                                             recv_sem=recv1.at[4 * (first + a) + k], device_id=sib, device_id_type=MESH).wait_send()
            blk = land[a].at[_slot((x, y), 1 - c)]
            pltpu.make_async_remote_copy(src_ref=blk, dst_ref=blk, send_sem=send1.at[4 * (first + a)], recv_sem=recv1.at[4 * (first + a)],
                                         device_id=sib, device_id_type=MESH).wait_recv()
            for j, chip in enumerate(chips):
                mine, theirs = land[a].at[_slot(chip, c)], land[a].at[_slot(chip, 1 - c)]
                pltpu.make_async_remote_copy(src_ref=mine, dst_ref=mine, send_sem=snd2.at[3 * a + j], recv_sem=rcv2.at[3 * a + j],
                                             device_id=sib, device_id_type=MESH).wait_send()
                pltpu.make_async_remote_copy(src_ref=theirs, dst_ref=theirs, send_sem=snd2.at[3 * a + j], recv_sem=rcv2.at[3 * a + j],
                                             device_id=sib, device_id_type=MESH).wait_recv()

    out = pl.pallas_call(
        body, name=name, out_shape=_hbm_like(shards) + _hbm_like(lands),
        in_specs=[_HBM] * (2 * n) + [_SEM] * 4 + [_ANY], out_specs=(_HBM,) * (2 * n),
        input_output_aliases={i: i for i in range(2 * n)}, compiler_params=_SIDE,
    )(*shards, *lands, send, recv, send2, recv2, after)
    return list(out[n:])


def _win_tree():
    x, y, c, chips = _mesh_pos()
    north = c == 1
    handed = (jnp.where(north, 1 - x, x), jnp.where(north, y, 1 - y))
    hand_to = (jnp.where(north, x, 1 - x), jnp.where(north, 1 - y, y))
    return x, y, c, chips, handed, hand_to


def _blk(land, chip, c):
    return land.at[_slot(chip, c)]


def _rcopy(src, dst, send, recv, to):
    return pltpu.make_async_remote_copy(src_ref=src, dst_ref=dst, send_sem=send, recv_sem=recv, device_id=to, device_id_type=MESH)


def _win_start(shards, lands, name):
    n = len(shards)

    def body(*refs):
        src, land, send, recv = refs[:n], refs[n:2 * n], refs[2 * n], refs[2 * n + 1]
        x, y, c, chips, _, _ = _win_tree()
        for a in range(n):
            for k, to in enumerate([(x, y, 1 - c), (*chips[0], c), (*chips[1], c)]):
                _rcopy(src[a], _blk(land[a], (x, y), c), send.at[3 * a + k], recv.at[3 * a + k], to).start()
        refs[-1][...] = jnp.zeros_like(refs[-1])

    sems = pltpu.SemaphoreType.DMA((3 * n,))
    out = pl.pallas_call(
        body, name=name, out_shape=(sems, sems) + _hbm_like(shards) + _hbm_like(lands) + (_TOKEN,),
        in_specs=[_HBM] * (2 * n), out_specs=(_SEM, _SEM) + (_HBM,) * (2 * n) + (_VMEM,),
        input_output_aliases={i: 2 + i for i in range(2 * n)}, compiler_params=_SIDE,
    )(*[_hbm(a) for a in list(shards) + list(lands)])
    return out[0], out[1], list(out[2:2 + n]), list(out[2 + n:2 + 2 * n]), out[-1]


def _win_hand_on(lands, recv1, after, name):
    n, m = len(lands), len(after)

    def body(*refs):
        land, rcv1 = refs[:n], refs[n]
        send2, recv2 = refs[n + 1 + m], refs[n + 2 + m]
        x, y, c, chips, handed, hand_to = _win_tree()
        for a in range(n):
            for j in range(2):
                blk = _blk(land[a], chips[j], c)
                _rcopy(blk, blk, send2.at[3 * a], rcv1.at[3 * a + 1 + j], (*chips[j], c)).wait_recv()
            blk = _blk(land[a], handed, c)
            _rcopy(blk, blk, send2.at[3 * a], recv2.at[3 * a], (*hand_to, c)).start()
            for j in range(2):
                blk = _blk(land[a], chips[j], c)
                _rcopy(blk, blk, send2.at[3 * a + 1 + j], recv2.at[3 * a + 1 + j], (x, y, 1 - c)).start()
        refs[-1][...] = jnp.zeros_like(refs[-1])

    sems = pltpu.SemaphoreType.DMA((3 * n,))
    out = pl.pallas_call(
        body, name=name, out_shape=(sems, sems) + _hbm_like(lands) + (_TOKEN,),
        in_specs=[_HBM] * n + [_SEM] + [_ANY] * m, out_specs=(_SEM, _SEM) + (_HBM,) * n + (_VMEM,),
        input_output_aliases={i: 2 + i for i in range(n)}, compiler_params=_SIDE,
    )(*lands, recv1, *after)
    return out[0], out[1], list(out[2:2 + n]), out[-1]


def _win_last(lands, recv2, after, name):
    n, m = len(lands), len(after)

    def body(*refs):
        land, rcv2 = refs[:n], refs[n]
        send3, recv3 = refs[n + 1 + m], refs[n + 2 + m]
        x, y, c, chips, _, hand_to = _win_tree()
        for a in range(n):
            blk = _blk(land[a], chips[2], c)
            _rcopy(blk, blk, send3.at[a], rcv2.at[3 * a], (*hand_to, c)).wait_recv()
            _rcopy(blk, blk, send3.at[a], recv3.at[a], (x, y, 1 - c)).start()
        refs[-1][...] = jnp.zeros_like(refs[-1])

    sems = pltpu.SemaphoreType.DMA((n,))
    out = pl.pallas_call(
        body, name=name, out_shape=(sems, sems) + _hbm_like(lands) + (_TOKEN,),
        in_specs=[_HBM] * n + [_SEM] + [_ANY] * m, out_specs=(_SEM, _SEM) + (_HBM,) * n + (_VMEM,),
        input_output_aliases={i: 2 + i for i in range(n)}, compiler_params=_SIDE,
    )(*lands, recv2, *after)
    return out[0], out[1], list(out[2:2 + n]), out[-1]


def _win_wait(shards, lands, sems1, sems2, sems3, after, name):
    n = len(lands)

    def body(*refs):
        src, land = refs[:n], refs[n:2 * n]
        send1, recv1, send2, recv2, send3, recv3 = refs[2 * n:2 * n + 6]
        x, y, c, chips, handed, hand_to = _win_tree()
        sib = (x, y, 1 - c)
        for a in range(n):
            own = _blk(land[a], (x, y), c)
            for k in range(3):
                _rcopy(src[a], own, send1.at[3 * a + k], recv1.at[3 * a + k], sib).wait_send()
            blk = _blk(land[a], (x, y), 1 - c)
            _rcopy(blk, blk, send1.at[3 * a], recv1.at[3 * a], sib).wait_recv()
            blk = _blk(land[a], handed, c)
            _rcopy(blk, blk, send2.at[3 * a], recv2.at[3 * a], sib).wait_send()
            for j in range(2):
                mine, theirs = _blk(land[a], chips[j], c), _blk(land[a], chips[j], 1 - c)
                _rcopy(mine, mine, send2.at[3 * a + 1 + j], recv2.at[3 * a + 1 + j], sib).wait_send()
                _rcopy(theirs, theirs, send2.at[3 * a + 1 + j], recv2.at[3 * a + 1 + j], sib).wait_recv()
            mine, theirs = _blk(land[a], chips[2], c), _blk(land[a], chips[2], 1 - c)
            _rcopy(mine, mine, send3.at[a], recv3.at[a], sib).wait_send()
            _rcopy(theirs, theirs, send3.at[a], recv3.at[a], sib).wait_recv()

    out = pl.pallas_call(
        body, name=name, out_shape=_hbm_like(shards) + _hbm_like(lands),
        in_specs=[_HBM] * (2 * n) + [_SEM] * 6 + [_ANY], out_specs=(_HBM,) * (2 * n),
        input_output_aliases={i: i for i in range(2 * n)}, compiler_params=_SIDE,
    )(*shards, *lands, *sems1, *sems2, *sems3, after)
    return list(out[n:])


def _pad_to(v, n):
    return jnp.pad(v, [(0, 0)] * (v.ndim - 1) + [(0, n - v.shape[-1])])


def _pack_small(n1, gb, sk, gn, n2, fn, extra=None):
    parts = [n1.reshape(-1), gb.reshape(-1), sk.reshape(-1), gn.reshape(-1), n2.reshape(-1), fn.reshape(-1)]
    flat = jnp.concatenate(parts + ([extra.reshape(-1)] if extra is not None else []))
    return _pad_to(flat, SMALL_N).reshape(SMALL_ROWS, LANE)


def _unpack_small(p):
    f = p.reshape(-1)
    return (f[S_N1:S_GB].reshape(1, D), f[S_GB:S_SK].reshape(1, GH * DK), f[S_SK:S_GN].reshape(1, NQ), f[S_GN:S_N2].reshape(1, DV),
            f[S_N2:S_FN].reshape(1, D), f[S_FN:S_LOSS].reshape(D))


class _Comm:
    def __init__(self, rest_shards, rest_lands, after, c_idx):
        self.c_idx = c_idx
        self.send, self.recv, self.shards, self.lands, self.token = _gather_start(rest_shards, rest_lands, after, "gather_rest_start")

    def _pass(self, lo, hi, after, tag):
        send2, recv2, lands, token = _gather_pass(self.lands[lo:hi], self.recv, after, "gather_pass_" + tag, first=lo)
        self.passed = (lo, hi, send2, recv2, lands)
        return token

    def _wait(self, after, tag):
        lo, hi, send2, recv2, lands = self.passed
        return _gather_wait(self.shards[lo:hi], lands, self.send, self.recv, send2, recv2, after, "gather_wait_" + tag, first=lo)

    def mixed(self, gla_o, gla_norm_w):
        return _tie(gla_norm_w, self._pass(0, 1, gla_o, "out"))

    def w_out(self, merged, norm2_w):
        (wo_all,) = self._wait(merged, "out")
        return wo_all.reshape(D, D), _tie(norm2_w, self._pass(1, 3, merged, "up"))

    def w_up(self, v2):
        wg_all, wu_all = self._wait(v2, "up")
        self._pass(3, 4, v2, "down")
        return wg_all.reshape(FH, D), wu_all.reshape(FH, D)

    def w_down(self, ff):
        return self._wait(ff, "down")[0].reshape(FH, D)

    def _reduce(self, tag, names, grads, recv1, rows):
        psums = [_pair_add(g, r, self.c_idx, "pair_add_" + nm, g.shape[2]) for g, r, nm in zip(grads, recv1, names)]
        *flight, token = _chip_start(psums, "reduce_chips_start_" + tag)
        return dict(tag=tag, names=names, rows=rows, flight=flight), token

    def ffn_grads(self, d_wg, d_wu, d_wd):
        self.ffn_pair = _pair_start([d.reshape(4, 2, FS, D) for d in (d_wg, d_wu, d_wd)], "reduce_pair_start_ffn")
        return self.ffn_pair[-1]

    def ffn_reduce(self, dv2, norm2_w):
        send, recv, grads, lands, _ = self.ffn_pair
        grads, recv1 = _pair_wait(send, recv, grads, lands, dv2, "reduce_pair_wait_ffn")
        self.ffn, token = self._reduce("ffn", ["w_ffn_gate", "w_ffn_up", "w_ffn_down"], grads, recv1, [176, 176, 176])
        return _tie(norm2_w, token)

    def in_grads(self, d_wmain, d_wlr, w_lr):
        mine, recv = _disassemble_exchange(d_wmain, d_wlr)
        self.in_names, self.in_rows = ["w_in", "w_out"], [808, 256]
        *self.in_hop, token = _hop_start([_add_blocks(mine, recv, "pair_add_w_in", 808)], "reduce_hop_start_in")
        return _tie(w_lr, token)

    def in_reduce(self, d_wo):
        d_wo4 = d_wo.reshape(4, 2, D // NDEV, D)
        send, recv, (d_wo4,), lands, _ = _pair_start([d_wo4], "reduce_pair_start_out")
        (d_wo4,), (wo_recv,) = _pair_wait(send, recv, [d_wo4], lands, self.update(self.ffn, d_wo), "reduce_pair_wait_out")
        wo_psum = _pair_add(d_wo4, wo_recv, self.c_idx, "pair_add_w_out", 256)
        (psum,), (land,) = _hop_wait(*self.in_hop, wo_psum, "reduce_hop_wait_in")
        psum = _hop_add(psum, land, _hop_pos()[2].astype(jnp.int32).reshape(1), "hop_add_w_in", 808)
        *flight, token = _chip_start([psum, wo_psum], "reduce_chips_start_in", nrel=[2, 3])
        self.inw = dict(tag="in", names=self.in_names, rows=self.in_rows, flight=flight)
        return token


def _local_step(xs, tgt, u, norm1_w, gla_gate_b, attn_sinks, gla_norm_w, norm2_w, fnw, w_main, w_lr, w2p, comm):
    proj =_mm(u, w_main, tb=True, tm=1024, tn=1280, tk=D, name="in_proj")
    plr = _mm(u, w_lr, tb=True, tm=1024, tn=LANE, tk=D, name="in_proj_lr")
    attn_o = _attn_fwd(proj, attn_sinks)
    gla_o, states = _gla_fwd(proj, plr, w2p, gla_gate_b)
    merged = _merge_fwd(attn_o, gla_o, proj, comm.mixed(gla_o, gla_norm_w))
    wo, norm2_w = comm.w_out(merged, norm2_w)
    h1 = _mm(merged, wo, tm=1024, tn=512, tk=D, res=xs, name="out_proj")
    v2 = _rmsnorm_fwd(h1, norm2_w, "norm2_fwd")
    wg_all, wu_all = comm.w_up(v2)
    fa, fb, ff = _ffn_up(v2, wg_all, wu_all)
    wd_all = comm.w_down(ff)
    h2 = _mm(ff, wd_all, tm=1024, tn=1024, tk=FH // 2, res=h1, name="ffn_down")
    dh2, dh2b, d_fnw, loss_part = _loss_head(h2, fnw, tgt)

    da, db = _ffn_dact(dh2b, wd_all, fa, fb)
    Tn = xs.shape[0]
    d_wd = _mm(ff, dh2b, ta=True, tm=512, tn=D, tk=Tn, out_dtype=BF16, name="ffn_dwd")
    d_wg = _mm(da, v2, ta=True, tm=512, tn=D, tk=Tn, out_dtype=BF16, name="ffn_dwg")
    d_wu = _mm(db, v2, ta=True, tm=512, tn=D, tk=Tn, out_dtype=BF16, name="ffn_dwu")
    dv2 = _mm(da, wg_all, tm=1024, tn=1024, tk=FH // 2, after=comm.ffn_grads(d_wg, d_wu, d_wd), name="ffn_dv2_gate")
    dv2 = _mm(db, wu_all, tm=1024, tn=1024, tk=FH // 2, res=dv2, name="ffn_dv2_up")
    norm2_w = comm.ffn_reduce(dv2, norm2_w)
    dh1, dh1b, d_n2 = _rmsnorm_bwd(dv2, h1, norm2_w, dh2, "norm2_bwd")
    dmerged = _mm(dh1b, wo, tb=True, tm=1024, tn=512, tk=D, name="out_proj_dx")
    d_attn, d_gla, d_gates, d_gnw = _merge_bwd(dmerged, attn_o, gla_o, proj, gla_norm_w)
    d_q, d_kv, d_sinks = _attn_bwd(proj, attn_sinks, attn_o, d_attn)
    d_gqk, d_gv, d_plr, d_w2p, d_gb = _gla_bwd(proj, plr, w2p, gla_gate_b, states, d_gla)
    dproj = jnp.concatenate([d_q, d_kv, d_gqk, d_gv, d_gates], axis=1)
    d_wmain = _mm(dproj, u, ta=True, tm=640, tn=D, tk=xs.shape[0], out_dtype=BF16, name="in_proj_dw")
    d_wlr = _mm(d_plr, u, ta=True, tm=LANE, tn=1024, tk=xs.shape[0], out_dtype=BF16, name="in_proj_lr_dw")
    du_lr = _mm(d_plr, comm.in_grads(d_wmain, d_wlr, w_lr), tm=1024, tn=1024, tk=LANE, name="in_proj_lr_dx")
    d_wo = _mm(merged, dh1b, ta=True, tm=1024, tn=512, tk=xs.shape[0], out_dtype=BF16, after=du_lr, name="out_proj_dw")
    du = _mm(dproj, w_main, tm=1024, tn=1024, tk=2560, res=du_lr, after=comm.in_reduce(d_wo), name="in_proj_dx")
    dx, _, d_n1 = _rmsnorm_bwd(du, xs, norm1_w, dh1, "norm1_bwd")
    return dx, loss_part, d_w2p, d_gb, d_sinks, d_gnw, d_n1, d_n2, d_fnw


def kernel(x, norm1_w, w_in, gla_gate_w2, gla_gate_b, attn_sinks, gla_norm_w, w_out, norm2_w, w_ffn_gate, w_ffn_up, w_ffn_down, final_norm_w, loss_target, m_norm1_w, m_w_in, m_gla_gate_w2, m_gla_gate_b, m_attn_sinks, m_gla_norm_w, m_w_out, m_norm2_w, m_w_ffn_gate, m_w_ffn_up, m_w_ffn_down, m_final_norm_w, v_norm1_w, v_w_in, v_gla_gate_w2, v_gla_gate_b, v_attn_sinks, v_gla_norm_w, v_w_out, v_norm2_w, v_w_ffn_gate, v_w_ffn_up, v_w_ffn_down, v_final_norm_w):
    xs, tgt = x[0], loss_target[0]
    fnw = final_norm_w.reshape(1, D)
    c_idx = lax.axis_index("c").astype(jnp.int32).reshape(1)
    dev = 4 * lax.axis_index("x") + 2 * lax.axis_index("y") + lax.axis_index("c")

    chip_idx = (2 * lax.axis_index("x") + lax.axis_index("y")).astype(jnp.int32).reshape(1)

    shift = (WS - WSTEP) * dev
    edge = WWIN - WS
    window = lax.dynamic_slice(jnp.pad(jnp.transpose(w_in[0]).astype(BF16), ((edge, edge), (0, 0))), (edge - shift, 0), (WWIN, D))
    w2_land = lax.dynamic_update_slice(lax.empty((NDEV, RANK, LANE), F32), gla_gate_w2, (dev, 0, 0))
    *sems1, win_srcs, win_lands, tok = _win_start([window, gla_gate_w2[0]], [lax.empty((NDEV, WWIN, D), BF16), w2_land], "gather_in_start")
    tr2 = lambda t: jnp.transpose(t[0])
    rows3 = lambda t: jnp.transpose(t[0] + tok[0, 0])
    rest = [(w + tok[0, 0]).astype(BF16) for w in (w_out[0], tr2(w_ffn_gate), tr2(w_ffn_up), w_ffn_down[0])]
    rest_lands = [lax.dynamic_update_slice(lax.empty((NDEV,) + s.shape, s.dtype), s[None], (dev, 0, 0)) for s in rest]
    win3 = [rows3(t) for t in (w_in, m_w_in, v_w_in)]
    *sems2, win_lands, tok = _win_hand_on(win_lands, sems1[1], rest + rest_lands + win3, "gather_in_hand_on")
    u = _rmsnorm_fwd(xs, _tie(norm1_w, tok), "norm1_fwd")
    *sems3, win_lands, tok = _win_last(win_lands, sems2[1], [u], "gather_in_last")
    comm = _Comm(rest, rest_lands, tok, c_idx)
    win_all, w2_all = _win_wait(win_srcs, win_lands, sems1, sems2, sems3, comm.token, "gather_in_wait")
    w_main, w_lr = _assemble_w_in(win_all, window)
    w2p = jnp.pad(jnp.transpose(w2_all, (1, 0, 2)).reshape(RANK, GH * DK), ((0, LANE - RANK), (0, 0)))

    big = {}

    def update(grp, after):
        psums, parts = _chip_wait(*grp["flight"], after, "reduce_chips_wait_" + grp["tag"])
        for nm, ps, pt, tr in zip(grp["names"], psums, parts, grp["rows"]):
            w, m, v = {"w_in": (w_in, m_w_in, v_w_in), "w_out": (w_out, m_w_out, v_w_out), "w_ffn_gate": (w_ffn_gate, m_w_ffn_gate, v_w_ffn_gate),
                       "w_ffn_up": (w_ffn_up, m_w_ffn_up, v_w_ffn_up), "w_ffn_down": (w_ffn_down, m_w_ffn_down, v_w_ffn_down)}[nm]
            if nm == "w_in":
                g_win = _sum_parts(ps, pt, chip_idx, "sum_w_in", tr, 1024)
                g3 = lax.dynamic_slice(g_win, (shift, 0), (WS, D))
                out3 = (g3,) + tuple(_adamw_given(*win3, g3, "adamw_w_in", 536, 512))
                big[nm] = [jnp.transpose(t)[None] for t in out3]
            elif nm in ("w_ffn_gate", "w_ffn_up"):
                big[nm] = [jnp.transpose(t)[None] for t in _adamw(tr2(w), tr2(m), tr2(v), ps, pt, chip_idx, "adamw_" + nm, tr)]
            else:
                big[nm] = [t[None] for t in _adamw(w[0], m[0], v[0], ps, pt, chip_idx, "adamw_" + nm, tr)]
            after = big[nm][0]
        return after

    comm.update = update
    dx, loss_part, d_w2p, d_gb, d_sinks, d_gnw, d_n1, d_n2, d_fnw = _local_step(
        xs, tgt, u, norm1_w, gla_gate_b, attn_sinks, gla_norm_w, norm2_w, fnw, w_main, w_lr, w2p, comm)

    pack = jnp.concatenate([_pack_small(d_n1, d_gb, d_sinks, d_gnw, d_n2, d_fnw, loss_part),
                            d_w2p[:RANK].reshape(GW2_ROWS, LANE)], axis=0)
    small = _sum_devices(_gather_small(pack))

    update(comm.inw, dx)
    g_small = small[:SMALL_ROWS]
    sm = _adamw_plain(_pack_small(norm1_w, gla_gate_b, attn_sinks, gla_norm_w, norm2_w, final_norm_w),
                      _pack_small(m_norm1_w, m_gla_gate_b, m_attn_sinks, m_gla_norm_w, m_norm2_w, m_final_norm_w),
                      _pack_small(v_norm1_w, v_gla_gate_b, v_attn_sinks, v_gla_norm_w, v_norm2_w, v_final_norm_w), g_small, "adamw_small")
    g_w2 = lax.dynamic_slice_in_dim(small[SMALL_ROWS:].reshape(RANK, GH * DK), dev * LANE, LANE, axis=1)
    w2 = [g_w2[None]] + [t[None] for t in _adamw_plain(gla_gate_w2[0], m_gla_gate_w2[0], v_gla_gate_w2[0], g_w2, "adamw_w2")]
    loss = g_small.reshape(-1)[S_LOSS]

    sg, sd, sm2, sv2 = [_unpack_small(t) for t in (g_small,) + tuple(sm)]

    def group(i, s):
        return (s[0], big["w_in"][i], w2[i], s[1], s[2], s[3], big["w_out"][i], s[4], big["w_ffn_gate"][i], big["w_ffn_up"][i],
                big["w_ffn_down"][i], s[5])

    return (loss, dx[None], *group(0, sg), *group(1, sd), *group(2, sm2), *group(3, sv2))
```

```python
import jax
import jax.numpy as jnp
from jax import lax
from jax.experimental import pallas as pl
from jax.experimental.pallas import tpu as pltpu

F32, BF16 = jnp.float32, jnp.bfloat16
HIGHEST = lax.Precision.HIGHEST

D = 2048
HD, NQ, NKV, GRP, WIN = 64, 32, 4, 8, 128
GH, DK, DV, RANK, GC = 4, 256, 512, 16, 64
FH, NDEV = 5632, 8
FS = FH // NDEV
DIN = 12816
WS = DIN // NDEV
EPS = 1e-6
MASKV = -1e30
LANE = 128

C_AQ, C_AK, C_AV, C_GQ, C_GK, C_GV, C_GR, C_GA, C_GB, NMAIN = 0, 2048, 2304, 2560, 3584, 4608, 6656, 8704, 10752, 12800
C_LR = 6656
WSTEP, WWIN = 1600, 1616

LR, B1, B2, AEPS, WD, STEP = 0.001, 0.9, 0.999, 1e-08, 0.01, 10

S_N1, S_GB, S_SK, S_GN, S_N2, S_FN, S_LOSS, SMALL_N = 0, 2048, 3072, 3104, 3616, 5664, 7712, 8192
SMALL_ROWS = SMALL_N // LANE
GW2_ROWS = RANK * GH * DK // LANE

MESH = pl.DeviceIdType.MESH


def _dot(a, b, ta=False, tb=False, prec=None):
    dn = (((0,) if ta else (1,), (1,) if tb else (0,)), ((), ()))
    return lax.dot_general(a, b, dn, preferred_element_type=F32, precision=prec)


def _sigmoid(x):
    return 1.0 / (1.0 + jnp.exp(-x))


VMEM_LIMIT = 56 * 1024 * 1024


def _cp(*sem):
    return pltpu.CompilerParams(dimension_semantics=sem, vmem_limit_bytes=VMEM_LIMIT)


def _mm(a, b, *, ta=False, tb=False, tm, tn, tk, out_dtype=F32, res=None, after=None, name):
    M, K = (a.shape[1], a.shape[0]) if ta else a.shape
    N = b.shape[0] if tb else b.shape[1]
    tm, tn, tk = min(tm, M), min(tn, N), min(tk, K)
    nk = K // tk
    assert M % tm == 0 and N % tn == 0 and K % tk == 0
    a_spec = pl.BlockSpec((tk, tm), lambda i, j, k: (k, i)) if ta else pl.BlockSpec((tm, tk), lambda i, j, k: (i, k))
    b_spec = pl.BlockSpec((tn, tk), lambda i, j, k: (j, k)) if tb else pl.BlockSpec((tk, tn), lambda i, j, k: (k, j))
    o_spec = pl.BlockSpec((tm, tn), lambda i, j, k: (i, j))
    has_res = res is not None

    def body(*refs):
        a_ref, b_ref = refs[0], refs[1]
        r_ref = refs[2] if has_res else None
        o_ref = refs[2 + has_res + (after is not None)]
        p = _dot(a_ref[...].astype(BF16), b_ref[...].astype(BF16), ta, tb)
        if nk == 1:
            if has_res:
                p = p + r_ref[...]
            o_ref[...] = p.astype(out_dtype)
        else:
            acc = refs[-1]
            k = pl.program_id(2)

            @pl.when(k == 0)
            def _():
                acc[...] = (p + r_ref[...]) if has_res else p

            @pl.when(k > 0)
            def _():
                acc[...] += p

            @pl.when(k == nk - 1)
            def _():
                o_ref[...] = acc[...].astype(out_dtype)

    return pl.pallas_call(
        body, name=name,
        out_shape=jax.ShapeDtypeStruct((M, N), out_dtype),
        grid=(M // tm, N // tn, nk),
        in_specs=[a_spec, b_spec] + ([o_spec] if has_res else []) + ([pl.BlockSpec(memory_space=pl.ANY)] if after is not None else []),
        out_specs=o_spec,
        scratch_shapes=[pltpu.VMEM((tm, tn), F32)] if nk > 1 else [],
        compiler_params=_cp("parallel", "parallel", "arbitrary"),
    )(*((a, b) + ((res,) if has_res else ()) + ((after,) if after is not None else ())))


def _rmsnorm_fwd(x, w, name, tm=256):
    Tn = x.shape[0]

    def body(x_ref, w_ref, o_ref):
        xv = x_ref[...]
        r = lax.rsqrt(jnp.mean(xv * xv, axis=1, keepdims=True) + EPS)
        o_ref[...] = (xv * r * w_ref[...]).astype(BF16)

    return pl.pallas_call(
        body, name=name, out_shape=jax.ShapeDtypeStruct((Tn, D), BF16), grid=(Tn // tm,),
        in_specs=[pl.BlockSpec((tm, D), lambda i: (i, 0)), pl.BlockSpec((1, D), lambda i: (0, 0))],
        out_specs=pl.BlockSpec((tm, D), lambda i: (i, 0)), compiler_params=_cp("parallel"),
    )(x, w)


def _rmsnorm_bwd(dy, h, w, res, name, tm=256):
    Tn = h.shape[0]

    def body(dy_ref, h_ref, w_ref, res_ref, dh_ref, dhb_ref, dw_ref):
        hv, dyv = h_ref[...], dy_ref[...]
        r = lax.rsqrt(jnp.mean(hv * hv, axis=1, keepdims=True) + EPS)
        g = dyv * w_ref[...]
        dh = res_ref[...] + r * g - hv * (r * r * r * jnp.mean(g * hv, axis=1, keepdims=True))
        dh_ref[...] = dh
        dhb_ref[...] = dh.astype(BF16)
        part = jnp.sum(dyv * hv * r, axis=0, keepdims=True)

        @pl.when(pl.program_id(0) == 0)
        def _():
            dw_ref[...] = part

        @pl.when(pl.program_id(0) > 0)
        def _():
            dw_ref[...] += part

    row = pl.BlockSpec((tm, D), lambda i: (i, 0))
    vec = pl.BlockSpec((1, D), lambda i: (0, 0))
    return pl.pallas_call(
        body, name=name,
        out_shape=(jax.ShapeDtypeStruct((Tn, D), F32), jax.ShapeDtypeStruct((Tn, D), BF16), jax.ShapeDtypeStruct((1, D), F32)),
        grid=(Tn // tm,), in_specs=[row, row, vec, row], out_specs=(row, row, vec), compiler_params=_cp("arbitrary"),
    )(dy, h, w, res)


def _loss_head(h2, wf, tgt, name="loss_head", tm=256):
    Tn = h2.shape[0]

    def body(h_ref, w_ref, t_ref, dh_ref, dhb_ref, dw_ref, loss_ref):
        hv, wv = h_ref[...], w_ref[...]
        r = lax.rsqrt(jnp.mean(hv * hv, axis=1, keepdims=True) + EPS)
        hn = hv * r
        e = hn * wv - t_ref[...]
        dy = e * (1.0 / D)
        g = dy * wv
        dh = r * g - hv * (r * r * r * jnp.mean(g * hv, axis=1, keepdims=True))
        dh_ref[...] = dh
        dhb_ref[...] = dh.astype(BF16)
        part = jnp.sum(dy * hn, axis=0, keepdims=True)
        lpart = (0.5 / D) * jnp.sum(jnp.sum(e * e, axis=1, keepdims=True), axis=0, keepdims=True)

        @pl.when(pl.program_id(0) == 0)
        def _():
            dw_ref[...] = part
            loss_ref[...] = lpart

        @pl.when(pl.program_id(0) > 0)
        def _():
            dw_ref[...] += part
            loss_ref[...] += lpart

    row = pl.BlockSpec((tm, D), lambda i: (i, 0))
    vec = pl.BlockSpec((1, D), lambda i: (0, 0))
    one = pl.BlockSpec((1, 1), lambda i: (0, 0))
    return pl.pallas_call(
        body, name=name,
        out_shape=(jax.ShapeDtypeStruct((Tn, D), F32), jax.ShapeDtypeStruct((Tn, D), BF16), jax.ShapeDtypeStruct((1, D), F32),
                   jax.ShapeDtypeStruct((1, 1), F32)),
        grid=(Tn // tm,), in_specs=[row, vec, row], out_specs=(row, row, vec, one), compiler_params=_cp("arbitrary"),
    )(h2, wf, tgt)


def _attn_mask(n):
    qi = lax.broadcasted_iota(jnp.int32, (NKV, GRP * WIN, 2 * WIN), 1) % WIN
    ki = lax.broadcasted_iota(jnp.int32, (NKV, GRP * WIN, 2 * WIN), 2)
    rel = qi + WIN - ki
    return (rel >= 0) & (rel < WIN) & ((n > 0) | (ki >= WIN))


def _kv_heads(prev_ref, cur_ref):
    return jnp.stack([jnp.concatenate([prev_ref[:, h * HD:(h + 1) * HD], cur_ref[:, h * HD:(h + 1) * HD]], axis=0) for h in range(NKV)])


def _q_heads(ref):
    return jnp.stack([jnp.concatenate([ref[:, (h * GRP + g) * HD:(h * GRP + g + 1) * HD] for g in range(GRP)], axis=0) for h in range(NKV)])


def _attn_probs(q_ref, kc_ref, kp_ref, sink_ref, mask):
    kk = _kv_heads(kp_ref, kc_ref).astype(BF16)
    qs = _q_heads(q_ref).astype(BF16)
    s = jnp.einsum('hqd,hkd->hqk', qs, kk, preferred_element_type=F32) * (HD ** -0.5)
    s = jnp.where(mask, s, MASKV)
    sink = jnp.stack([jnp.concatenate([jnp.full((WIN, 1), sink_ref[0, h * GRP + g], F32) for g in range(GRP)], axis=0) for h in range(NKV)])
    m = jnp.maximum(jnp.max(s, axis=2, keepdims=True), sink)
    e = jnp.exp(s - m)
    es = jnp.exp(sink - m)
    rows = jnp.einsum('hqk,hkn->hqn', e.astype(BF16), jnp.ones((NKV, 2 * WIN, LANE), BF16), preferred_element_type=F32)[:, :, :1]
    inv = 1.0 / (rows + es)
    return e * inv, es * inv, qs, kk


def _attn_specs(nb, last):
    cur = lambda n: jnp.minimum(n, last)
    prev = lambda n: jnp.maximum(jnp.minimum(n, last) - 1, 0)
    return [
        pl.BlockSpec((WIN, NQ * HD), lambda n: (cur(n), C_AQ // (NQ * HD))),
        pl.BlockSpec((WIN, NKV * HD), lambda n: (cur(n), C_AK // (NKV * HD))),
        pl.BlockSpec((WIN, NKV * HD), lambda n: (prev(n), C_AK // (NKV * HD))),
        pl.BlockSpec((WIN, NKV * HD), lambda n: (cur(n), C_AV // (NKV * HD))),
        pl.BlockSpec((WIN, NKV * HD), lambda n: (prev(n), C_AV // (NKV * HD))),
    ]


def _attn_fwd(proj, sinks, name="attn_fwd"):
    Tn = proj.shape[0]
    nb = Tn // WIN

    def body(q_ref, kc_ref, kp_ref, vc_ref, vp_ref, sink_ref, o_ref):
        p, _, _, _ = _attn_probs(q_ref, kc_ref, kp_ref, sink_ref, _attn_mask(pl.program_id(0)))
        o = jnp.einsum('hqk,hkd->hqd', p.astype(BF16), _kv_heads(vp_ref, vc_ref).astype(BF16), preferred_element_type=F32)
        for h in range(NKV):
            for g in range(GRP):
                o_ref[:, (h * GRP + g) * HD:(h * GRP + g + 1) * HD] = o[h, g * WIN:(g + 1) * WIN, :]

    return pl.pallas_call(
        body, name=name, out_shape=jax.ShapeDtypeStruct((Tn, D), F32), grid=(nb,),
        in_specs=_attn_specs(nb, nb - 1) + [pl.BlockSpec(memory_space=pltpu.SMEM)],
        out_specs=pl.BlockSpec((WIN, D), lambda n: (n, 0)), compiler_params=_cp("parallel"),
    )(proj, proj, proj, proj, proj, sinks)


def _attn_bwd(proj, sinks, o, do, name="attn_bwd"):
    Tn = proj.shape[0]
    nb = Tn // WIN
    KW = NKV * HD

    def body(q_ref, kc_ref, kp_ref, vc_ref, vp_ref, o_ref, do_ref, sink_ref, dq_ref, dkv_ref, dsk_ref, carry, cur):
        n = pl.program_id(0)

        @pl.when(n == 0)
        def _():
            carry[...] = jnp.zeros_like(carry)
            dsk_ref[...] = jnp.zeros_like(dsk_ref)

        @pl.when(n < nb)
        def _():
            p, ps, qs, kk = _attn_probs(q_ref, kc_ref, kp_ref, sink_ref, _attn_mask(n))
            vv = _kv_heads(vp_ref, vc_ref).astype(BF16)
            dos = _q_heads(do_ref)
            delta = jnp.sum(dos * _q_heads(o_ref), axis=2, keepdims=True)
            dosb = dos.astype(BF16)
            dp = jnp.einsum('hqd,hkd->hqk', dosb, vv, preferred_element_type=F32)
            ds = (p * (dp - delta) * (HD ** -0.5)).astype(BF16)
            dq = jnp.einsum('hqk,hkd->hqd', ds, kk, preferred_element_type=F32)
            dkk = jnp.einsum('hqk,hqd->hkd', ds, qs, preferred_element_type=F32)
            dvv = jnp.einsum('hqk,hqd->hkd', p.astype(BF16), dosb, preferred_element_type=F32)
            dsk = ps * delta
            for h in range(NKV):
                for g in range(GRP):
                    i = h * GRP + g
                    dq_ref[:, i * HD:(i + 1) * HD] = dq[h, g * WIN:(g + 1) * WIN, :].astype(BF16)
                    dsk_ref[:, i:i + 1] -= jnp.sum(dsk[h, g * WIN:(g + 1) * WIN, :], axis=0, keepdims=True)
                dkv_ref[:, h * HD:(h + 1) * HD] = (carry[:, h * HD:(h + 1) * HD] + dkk[h, :WIN, :]).astype(BF16)
                dkv_ref[:, KW + h * HD:KW + (h + 1) * HD] = (carry[:, KW + h * HD:KW + (h + 1) * HD] + dvv[h, :WIN, :]).astype(BF16)
                cur[:, h * HD:(h + 1) * HD] = dkk[h, WIN:, :]
                cur[:, KW + h * HD:KW + (h + 1) * HD] = dvv[h, WIN:, :]
            carry[...] = cur[...]

        @pl.when(n == nb)
        def _():
            dkv_ref[...] = carry[...].astype(BF16)

    last = nb - 1
    row = pl.BlockSpec((WIN, D), lambda n: (jnp.minimum(n, last), 0))
    return pl.pallas_call(
        body, name=name,
        out_shape=(jax.ShapeDtypeStruct((Tn, D), BF16), jax.ShapeDtypeStruct((Tn, 2 * KW), BF16), jax.ShapeDtypeStruct((1, NQ), F32)),
        grid=(nb + 1,),
        in_specs=_attn_specs(nb, last) + [row, row, pl.BlockSpec(memory_space=pltpu.SMEM)],
        out_specs=(row, pl.BlockSpec((WIN, 2 * KW), lambda n: (jnp.maximum(n - 1, 0), 0)), pl.BlockSpec((1, NQ), lambda n: (0, 0))),
        scratch_shapes=[pltpu.VMEM((WIN, 2 * KW), F32), pltpu.VMEM((WIN, 2 * KW), F32)],
        compiler_params=_cp("arbitrary"),
    )(proj, proj, proj, proj, proj, o, do, sinks)


def _tri(lower):
    r = lax.broadcasted_iota(jnp.int32, (GC, GC), 0)
    c = lax.broadcasted_iota(jnp.int32, (GC, GC), 1)
    return r >= c if lower else r <= c


def _per_head(a):
    return jnp.stack([a[:, h * DK:(h + 1) * DK] for h in range(GH)])


def _all_heads(a):
    return jnp.concatenate([a[h] for h in range(GH)], axis=1)


def _gla_gates(lr, w2_ref, gb_ref):
    logit = _dot(lr, w2_ref[...].astype(BF16)) + gb_ref[...]
    la = (jnp.minimum(logit, 0.0) - jnp.log(1.0 + jnp.exp(-jnp.abs(logit)))) * (1.0 / 16.0)
    g = _dot(_tri(True).astype(F32), la, prec=HIGHEST)
    return logit, g


def _bmm(spec, a, b):
    return jnp.einsum(spec, a, b, preferred_element_type=F32)


def _gla_specs(nc, rev):
    idx = (lambda n: nc - 1 - n) if rev else (lambda n: n)
    half = 2 * DK
    return (
        [pl.BlockSpec((GC, half), lambda n, j=j: (idx(n), C_GQ // half + j)) for j in range(2)]
        + [pl.BlockSpec((GC, half), lambda n, j=j: (idx(n), C_GK // half + j)) for j in range(2)]
        + [pl.BlockSpec((GC, DV), lambda n, h=h: (idx(n), C_GV // DV + h)) for h in range(GH)]
        + [pl.BlockSpec((GC, LANE), lambda n: (idx(n), 0)), pl.BlockSpec((LANE, GH * DK), lambda n: (0, 0)),
           pl.BlockSpec((1, GH * DK), lambda n: (0, 0))])


def _gla_heads(refs):
    return (lambda h: refs[h // 2][:, (h % 2) * DK:(h % 2 + 1) * DK], lambda h: refs[2 + h // 2][:, (h % 2) * DK:(h % 2 + 1) * DK],
            lambda h: refs[4 + h][...])


def _gla_fwd(proj, plr, w2p, gb, name="gla_fwd"):
    Tn = proj.shape[0]
    nc = Tn // GC

    def body(*refs):
        qh, kh, vh = _gla_heads(refs)
        lr_ref, w2_ref, gb_ref, o_ref, st_ref, S = refs[8:]

        @pl.when(pl.program_id(0) == 0)
        def _():
            S[...] = jnp.zeros_like(S)

        heads = lambda f: jnp.stack([f(h) for h in range(GH)])
        _, g_all = _gla_gates(lr_ref[...].astype(BF16), w2_ref, gb_ref)
        g = _per_head(g_all)
        gl = g[:, GC - 1:GC, :]
        k = heads(kh)
        v = heads(vh).astype(BF16)
        qd = (heads(qh) * (DK ** -0.5) * jnp.exp(g)).astype(BF16)
        ki = (k * jnp.exp(-g)).astype(BF16)
        ke = (k * jnp.exp(gl - g)).astype(BF16)
        att = jnp.where(_tri(True)[None], _bmm('hid,hjd->hij', qd, ki), 0.0).astype(BF16)
        sp = S[...]
        st_ref[0] = sp
        o = _bmm('hij,hjv->hiv', att, v) + _bmm('hid,hvd->hiv', qd, sp.astype(BF16))
        for h in range(GH):
            o_ref[:, h * DV:(h + 1) * DV] = o[h]
        S[...] = sp * jnp.exp(gl) + _bmm('hjv,hjd->hvd', v, ke)

    return pl.pallas_call(
        body, name=name,
        out_shape=(jax.ShapeDtypeStruct((Tn, GH * DV), F32), jax.ShapeDtypeStruct((nc, GH, DV, DK), F32)),
        grid=(nc,), in_specs=_gla_specs(nc, False),
        out_specs=(pl.BlockSpec((GC, GH * DV), lambda n: (n, 0)), pl.BlockSpec((1, GH, DV, DK), lambda n: (n, 0, 0, 0))),
        scratch_shapes=[pltpu.VMEM((GH, DV, DK), F32)], compiler_params=_cp("arbitrary"),
    )(*([proj] * 8), plr, w2p, gb)


def _gla_bwd(proj, plr, w2p, gb, states, do, name="gla_bwd"):
    Tn = proj.shape[0]
    nc = Tn // GC

    def body(*refs):
        qh, kh, vh = _gla_heads(refs)
        lr_ref, w2_ref, gb_ref, st_ref, do_ref, dqk_ref, dv_ref, dlr_ref, dw2_ref, dgb_ref, dS = refs[8:]

        @pl.when(pl.program_id(0) == 0)
        def _():
            dS[...] = jnp.zeros_like(dS)
            dw2_ref[...] = jnp.zeros_like(dw2_ref)
            dgb_ref[...] = jnp.zeros_like(dgb_ref)

        heads = lambda f: jnp.stack([f(h) for h in range(GH)])
        lr = lr_ref[...].astype(BF16)
        causal = _tri(True)[None]
        last_row = lax.broadcasted_iota(jnp.int32, (GH, GC, DK), 1) == GC - 1
        logit, g_all = _gla_gates(lr, w2_ref, gb_ref)
        g = _per_head(g_all)
        gl = g[:, GC - 1:GC, :]
        egl = jnp.exp(gl)
        eg, eng, ege = jnp.exp(g), jnp.exp(-g), jnp.exp(gl - g)
        k = heads(kh)
        v = heads(vh).astype(BF16)
        dob = heads(lambda h: do_ref[:, h * DV:(h + 1) * DV]).astype(BF16)
        qd = heads(qh) * (DK ** -0.5) * eg
        ki = k * eng
        ke = k * ege
        qdb, kib, keb = qd.astype(BF16), ki.astype(BF16), ke.astype(BF16)
        att = jnp.where(causal, _bmm('hid,hjd->hij', qdb, kib), 0.0).astype(BF16)
        datt = jnp.where(causal, _bmm('hiv,hjv->hij', dob, v), 0.0).astype(BF16)
        sp = st_ref[0]
        dsn = dS[...]
        dsnb = dsn.astype(BF16)
        dv = (_bmm('hij,hiv->hjv', att, dob) + _bmm('hjd,hvd->hjv', keb, dsnb)).astype(BF16)
        dqd = _bmm('hij,hjd->hid', datt, kib) + _bmm('hiv,hvd->hid', dob, sp.astype(BF16))
        dki = _bmm('hij,hid->hjd', datt, qdb)
        dke = _bmm('hjv,hvd->hjd', v, dsnb)
        ddec = jnp.sum(dsn * sp, axis=1, keepdims=True)
        dS[...] = dsn * egl + _bmm('hiv,hid->hvd', dob, qdb)
        dke_ke = dke * ke
        dgl = jnp.sum(dke_ke, axis=1, keepdims=True) + ddec * egl
        dg = dqd * qd - dki * ki - dke_ke + jnp.where(last_row, dgl, 0.0)
        dq = (dqd * ((DK ** -0.5) * eg)).astype(BF16)
        dk = (dki * eng + dke * ege).astype(BF16)
        for h in range(GH):
            dv_ref[:, h * DV:(h + 1) * DV] = dv[h]
            dqk_ref[:, h * DK:(h + 1) * DK] = dq[h]
            dqk_ref[:, GH * DK + h * DK:GH * DK + (h + 1) * DK] = dk[h]
        dla = _dot(_tri(False).astype(F32), _all_heads(dg), prec=HIGHEST)
        dlogit = dla * (1.0 / 16.0) * _sigmoid(-logit)
        dlb = dlogit.astype(BF16)
        dlr_ref[...] = _dot(dlb, w2_ref[...].astype(BF16), tb=True).astype(BF16)
        dw2_ref[...] += _dot(lr, dlb, ta=True)
        dgb_ref[...] += jnp.sum(dlogit, axis=0, keepdims=True)

    rev = lambda n: nc - 1 - n
    row = pl.BlockSpec((GC, GH * DV), lambda n: (rev(n), 0))
    return pl.pallas_call(
        body, name=name,
        out_shape=(jax.ShapeDtypeStruct((Tn, 2 * GH * DK), BF16), jax.ShapeDtypeStruct((Tn, GH * DV), BF16),
                   jax.ShapeDtypeStruct((Tn, LANE), BF16), jax.ShapeDtypeStruct((LANE, GH * DK), F32),
                   jax.ShapeDtypeStruct((1, GH * DK), F32)),
        grid=(nc,),
        in_specs=_gla_specs(nc, True) + [pl.BlockSpec((1, GH, DV, DK), lambda n: (rev(n), 0, 0, 0)), row],
        out_specs=(row, row, pl.BlockSpec((GC, LANE), lambda n: (rev(n), 0)), pl.BlockSpec((LANE, GH * DK), lambda n: (0, 0)),
                   pl.BlockSpec((1, GH * DK), lambda n: (0, 0))),
        scratch_shapes=[pltpu.VMEM((GH, DV, DK), F32)], compiler_params=_cp("arbitrary"),
    )(*([proj] * 8), plr, w2p, gb, states, do)


def _merge_specs(tm):
    row = pl.BlockSpec((tm, D), lambda i: (i, 0))
    gates = [pl.BlockSpec((tm, DV), lambda i, j=c // DV + h: (i, j)) for c in (C_GR, C_GA, C_GB) for h in range(GH)]
    return row, gates, pl.BlockSpec((1, DV), lambda i: (0, 0))


def _merge_fwd(a, go, proj, gnw, name="merge_fwd", tm=256):
    Tn = a.shape[0]

    def body(a_ref, go_ref, *rest):
        gates, w_ref, m_ref = rest[:3 * GH], rest[3 * GH], rest[3 * GH + 1]
        for h in range(GH):
            sl = slice(h * DV, (h + 1) * DV)
            gov = go_ref[:, sl]
            r = lax.rsqrt(jnp.mean(gov * gov, axis=1, keepdims=True) + EPS)
            gr = gates[h][...]
            g2 = gov * r * w_ref[...] * (gr * _sigmoid(gr))
            m_ref[:, sl] = (_sigmoid(gates[GH + h][...]) * a_ref[:, sl] + _sigmoid(gates[2 * GH + h][...]) * g2).astype(BF16)

    row, gates, vec = _merge_specs(tm)
    return pl.pallas_call(
        body, name=name, out_shape=jax.ShapeDtypeStruct((Tn, D), BF16), grid=(Tn // tm,),
        in_specs=[row, row] + gates + [vec], out_specs=row, compiler_params=_cp("parallel"),
    )(a, go, *([proj] * (3 * GH)), gnw)


def _merge_bwd(dm, a, go, proj, gnw, name="merge_bwd", tm=256):
    Tn = a.shape[0]

    def body(dm_ref, a_ref, go_ref, *rest):
        gates = rest[:3 * GH]
        w_ref, da_ref, dgo_ref, dg_ref, dw_ref = rest[3 * GH:]
        wv = w_ref[...]
        dw = jnp.zeros((1, DV), F32)
        for h in range(GH):
            sl = slice(h * DV, (h + 1) * DV)
            dmv, av, gov, gr = dm_ref[:, sl], a_ref[:, sl], go_ref[:, sl], gates[h][...]
            sa, sb, sg = _sigmoid(gates[GH + h][...]), _sigmoid(gates[2 * GH + h][...]), _sigmoid(gr)
            r = lax.rsqrt(jnp.mean(gov * gov, axis=1, keepdims=True) + EPS)
            gn0 = gov * r
            gn = gn0 * wv
            silu = gr * sg
            dg2 = dmv * sb
            da_ref[:, sl] = dmv * sa
            dg_ref[:, D + h * DV:D + (h + 1) * DV] = (dmv * av * sa * (1.0 - sa)).astype(BF16)
            dg_ref[:, 2 * D + h * DV:2 * D + (h + 1) * DV] = (dg2 * gn * silu * (1.0 - sb)).astype(BF16)
            dg_ref[:, sl] = (dg2 * gn * (sg * (1.0 + gr * (1.0 - sg)))).astype(BF16)
            dgn = dg2 * silu
            dw = dw + jnp.sum(dgn * gn0, axis=0, keepdims=True)
            gg = dgn * wv
            dgo_ref[:, sl] = r * gg - gov * (r * r * r * jnp.mean(gg * gov, axis=1, keepdims=True))

        @pl.when(pl.program_id(0) == 0)
        def _():
            dw_ref[...] = dw

        @pl.when(pl.program_id(0) > 0)
        def _():
            dw_ref[...] += dw

    row, gates, vec = _merge_specs(tm)
    return pl.pallas_call(
        body, name=name,
        out_shape=(jax.ShapeDtypeStruct((Tn, D), F32), jax.ShapeDtypeStruct((Tn, D), F32), jax.ShapeDtypeStruct((Tn, 3 * D), BF16),
                   jax.ShapeDtypeStruct((1, DV), F32)),
        grid=(Tn // tm,), in_specs=[row, row, row] + gates + [vec],
        out_specs=(row, row, pl.BlockSpec((tm, 3 * D), lambda i: (i, 0)), vec), compiler_params=_cp("arbitrary"),
    )(dm, a, go, *([proj] * (3 * GH)), gnw)


def _ffn_up(v2, wgt, wut, name="ffn_up", tm=1024, tn=512):
    Tn = v2.shape[0]
    tm = min(tm, Tn)

    def body(v_ref, wg_ref, wu_ref, a_ref, b_ref, ff_ref):
        vv = v_ref[...]
        a = _dot(vv, wg_ref[...], tb=True)
        b = _dot(vv, wu_ref[...], tb=True)
        a_ref[...] = a.astype(BF16)
        b_ref[...] = b.astype(BF16)
        ff_ref[...] = (a * _sigmoid(a) * b).astype(BF16)

    w = pl.BlockSpec((tn, D), lambda j, i: (j, 0))
    act = pl.BlockSpec((tm, tn), lambda j, i: (i, j))
    return pl.pallas_call(
        body, name=name,
        out_shape=(jax.ShapeDtypeStruct((Tn, FH), BF16), jax.ShapeDtypeStruct((Tn, FH), BF16), jax.ShapeDtypeStruct((Tn, FH), BF16)),
        grid=(FH // tn, Tn // tm), in_specs=[pl.BlockSpec((tm, D), lambda j, i: (i, 0)), w, w], out_specs=(act, act, act),
        compiler_params=_cp("parallel", "parallel"),
    )(v2, wgt, wut)


def _ffn_dact(dh2b, wd, a, b, name="ffn_dact", tm=1024, tn=512):
    Tn = dh2b.shape[0]
    tm = min(tm, Tn)

    def body(d_ref, w_ref, a_ref, b_ref, da_ref, db_ref):
        dff = _dot(d_ref[...], w_ref[...], tb=True)
        av = a_ref[...].astype(F32)
        sg = _sigmoid(av)
        da_ref[...] = (dff * b_ref[...].astype(F32) * (sg * (1.0 + av * (1.0 - sg)))).astype(BF16)
        db_ref[...] = (dff * (av * sg)).astype(BF16)

    act = pl.BlockSpec((tm, tn), lambda j, i: (i, j))
    return pl.pallas_call(
        body, name=name,
        out_shape=(jax.ShapeDtypeStruct((Tn, FH), BF16), jax.ShapeDtypeStruct((Tn, FH), BF16)),
        grid=(FH // tn, Tn // tm),
        in_specs=[pl.BlockSpec((tm, D), lambda j, i: (i, 0)), pl.BlockSpec((tn, D), lambda j, i: (j, 0)), act, act],
        out_specs=(act, act), compiler_params=_cp("parallel", "parallel"),
    )(dh2b, wd, a, b)


def _adam_math(w, g, m, v):
    m2 = B1 * m + (1.0 - B1) * g
    v2 = B2 * v + (1.0 - B2) * (g * g)
    mh = m2 / (1.0 - B1 ** STEP)
    vh = v2 / (1.0 - B2 ** STEP)
    return -LR * (mh / (jnp.sqrt(vh) + AEPS) + WD * w), m2, v2


def _sum_blocks(o_ref, p_ref):
    g = o_ref[...].astype(F32)
    for j in range(p_ref.shape[0]):
        g = g + p_ref[j].astype(F32)
    return g


def _adamw(w, m, v, psums, parts, chip_idx, name, tr):
    R, C = w.shape

    def body(s_ref, w_ref, m_ref, v_ref, o_ref, p_ref, g_ref, d_ref, m2_ref, v2_ref):
        g = _sum_blocks(o_ref, p_ref)
        d, m2, v2 = _adam_math(w_ref[...], g, m_ref[...], v_ref[...])
        g_ref[...] = g
        d_ref[...] = d
        m2_ref[...] = m2
        v2_ref[...] = v2

    blk = pl.BlockSpec((tr, C), lambda i, s: (i, 0))
    out = jax.ShapeDtypeStruct((R, C), F32)
    grid_spec = pltpu.PrefetchScalarGridSpec(
        num_scalar_prefetch=1, grid=(R // tr,),
        in_specs=[blk, blk, blk, pl.BlockSpec((None, tr, C), lambda i, s: (s[0], i, 0)),
                  pl.BlockSpec((parts.shape[0], tr, C), lambda i, s: (0, i, 0))],
        out_specs=(blk, blk, blk, blk),
    )
    return pl.pallas_call(body, name=name, out_shape=(out, out, out, out), grid_spec=grid_spec, compiler_params=_cp("parallel"),
                          )(chip_idx, w, m, v, psums, parts)


def _adamw_given(w, m, v, g, name, tr, tc):
    R, C = w.shape

    def body(w_ref, m_ref, v_ref, g_ref, d_ref, m2_ref, v2_ref):
        d, m2, v2 = _adam_math(w_ref[...], g_ref[...], m_ref[...], v_ref[...])
        d_ref[...] = d
        m2_ref[...] = m2
        v2_ref[...] = v2

    blk = pl.BlockSpec((tr, tc), lambda i, j: (i, j))
    out = jax.ShapeDtypeStruct(w.shape, F32)
    return pl.pallas_call(body, name=name, out_shape=(out, out, out), grid=(pl.cdiv(R, tr), C // tc), in_specs=[blk] * 4,
                          out_specs=(blk, blk, blk), compiler_params=_cp("parallel", "parallel"))(w, m, v, g)


def _sum_parts(psums, parts, chip_idx, name, tr, tc):
    _, R, C = psums.shape

    def body(s_ref, o_ref, p_ref, g_ref):
        g_ref[...] = _sum_blocks(o_ref, p_ref)

    grid_spec = pltpu.PrefetchScalarGridSpec(
        num_scalar_prefetch=1, grid=(R // tr, C // tc),
        in_specs=[pl.BlockSpec((None, tr, tc), lambda i, j, s: (s[0], i, j)),
                  pl.BlockSpec((parts.shape[0], tr, tc), lambda i, j, s: (0, i, j))],
        out_specs=pl.BlockSpec((tr, tc), lambda i, j, s: (i, j)),
    )
    return pl.pallas_call(body, name=name, out_shape=jax.ShapeDtypeStruct((R, C), F32), grid_spec=grid_spec,
                          compiler_params=_cp("parallel", "parallel"))(chip_idx, psums, parts)


def _adamw_plain(w, m, v, g, name):
    def body(w_ref, m_ref, v_ref, g_ref, d_ref, m2_ref, v2_ref):
        d, m2, v2 = _adam_math(w_ref[...], g_ref[...], m_ref[...], v_ref[...])
        d_ref[...] = d
        m2_ref[...] = m2
        v2_ref[...] = v2

    out = jax.ShapeDtypeStruct(w.shape, F32)
    return pl.pallas_call(body, name=name, out_shape=(out, out, out))(w, m, v, g)


def _sum_devices(pack_all, name="sum_small"):
    def body(p_ref, o_ref):
        s = p_ref[0]
        for k in range(1, NDEV):
            s = s + p_ref[k]
        o_ref[...] = s

    return pl.pallas_call(body, name=name, out_shape=jax.ShapeDtypeStruct(pack_all.shape[1:], F32))(pack_all)


def _pair_add(g5, recv, c_idx, name, tr):
    _, _, R, C = g5.shape

    def body(c_ref, g_ref, r_ref, o_ref):
        o_ref[...] = (g_ref[...].astype(F32) + r_ref[...].astype(F32)).astype(BF16)

    grid_spec = pltpu.PrefetchScalarGridSpec(
        num_scalar_prefetch=1, grid=(4, R // tr),
        in_specs=[pl.BlockSpec((None, None, tr, C), lambda q, i, c: (q, c[0], i, 0)), pl.BlockSpec((None, tr, C), lambda q, i, c: (q, i, 0))],
        out_specs=pl.BlockSpec((None, tr, C), lambda q, i, c: (q, i, 0)),
    )
    return pl.pallas_call(
        body, name=name, out_shape=jax.ShapeDtypeStruct((4, R, C), BF16), grid_spec=grid_spec,
        compiler_params=_cp("parallel", "parallel"),
    )(c_idx, g5, recv)


_ANY = pl.BlockSpec(memory_space=pl.ANY)


def _mesh_pos():
    x, y, c = lax.axis_index("x"), lax.axis_index("y"), lax.axis_index("c")
    return x, y, c, [(1 - x, y), (x, 1 - y), (1 - x, 1 - y)]


def _gather_small(pack, name="gather_small"):
    def body(pk, pk_all, psend, precv, loc):
        x, y, c, chips = _mesh_pos()
        me_slot = 4 * x + 2 * y + c
        sib = (x, y, 1 - c)
        own = pltpu.make_async_copy(pk, pk_all.at[me_slot], loc)
        own.start()
        peers = [sib] + [(*chip, c) for chip in chips] + [(*chip, 1 - c) for chip in chips]
        small = [pltpu.make_async_remote_copy(src_ref=pk, dst_ref=pk_all.at[me_slot], send_sem=psend.at[k], recv_sem=precv.at[k],
                                              device_id=p, device_id_type=MESH) for k, p in enumerate(peers)]
        for d in small:
            d.start()
        for k, p in enumerate(peers):
            pltpu.make_async_remote_copy(src_ref=pk, dst_ref=pk_all.at[4 * p[0] + 2 * p[1] + p[2]], send_sem=psend.at[k],
                                         recv_sem=precv.at[k], device_id=p, device_id_type=MESH).wait_recv()
        for d in small:
            d.wait_send()
        own.wait()

    return pl.pallas_call(
        body, name=name, out_shape=jax.ShapeDtypeStruct((NDEV,) + pack.shape, pack.dtype), in_specs=[_ANY], out_specs=_ANY,
        scratch_shapes=[pltpu.SemaphoreType.DMA((7,)), pltpu.SemaphoreType.DMA((7,)), pltpu.SemaphoreType.DMA(())],
    )(pack)


def _main_row(g):
    return g if g < C_LR else g - RANK


def _window_pieces(lo, hi):
    out = []
    for a, b, where in ((lo, min(hi, C_LR), "main"), (max(lo, C_LR), min(hi, C_LR + RANK), "lr"), (max(lo, C_LR + RANK), hi, "main")):
        if a < b:
            out.append((a, b, where, _main_row(a) if where == "main" else a - C_LR))
    return out


def _assemble_w_in(windows, own, name="assemble_w_in"):
    edges = NDEV - 1

    def body(b_ref, own_ref, main_ref, lr_ref, buf, ebuf, in_sems, out_sems, esems):
        dev = 4 * lax.axis_index("x") + 2 * lax.axis_index("y") + lax.axis_index("c")

        def load(k):
            return pltpu.make_async_copy(b_ref.at[k], buf.at[k % 2], in_sems.at[k % 2])

        def start_load(k):
            pl.when(dev == k)(pltpu.make_async_copy(own_ref, buf.at[k % 2], in_sems.at[k % 2]).start)
            pl.when(dev != k)(load(k).start)

        lr_ref[RANK:, :] = jnp.zeros((LANE - RANK, D), BF16)
        start_load(0)
        pending, edge_out = [], []
        for k in range(NDEV):
            s = k % 2
            load(k).wait()
            if k:
                ebuf[k - 1] = buf[1 - s, WSTEP:WWIN, :] + buf[s, 0:16, :]
                edge_out.append(pltpu.make_async_copy(ebuf.at[k - 1], main_ref.at[pl.ds(_main_row(WSTEP * k), 16)], esems.at[k - 1]))
                edge_out[-1].start()
                for d in pending:
                    d.wait()
            if k + 1 < NDEV:
                start_load(k + 1)
            pending = []
            lo = WSTEP * k + (16 if k else 0)
            hi = WSTEP * k + (WWIN if k == NDEV - 1 else WSTEP)
            for a, b, where, dst in _window_pieces(lo, hi):
                if where == "lr":
                    lr_ref[dst:dst + b - a, :] = buf[s, a - WSTEP * k:b - WSTEP * k, :]
                else:
                    pending.append(pltpu.make_async_copy(buf.at[s, pl.ds(a - WSTEP * k, b - a)], main_ref.at[pl.ds(dst, b - a)],
                                                         out_sems.at[2 * s + len(pending)]))
                    pending[-1].start()
        for d in pending + edge_out:
            d.wait()

    return pl.pallas_call(
        body, name=name,
        out_shape=(jax.ShapeDtypeStruct((NMAIN, D), BF16), jax.ShapeDtypeStruct((LANE, D), BF16)),
        in_specs=[_ANY, _ANY], out_specs=(_ANY, pl.BlockSpec(memory_space=pltpu.VMEM)),
        scratch_shapes=[pltpu.VMEM((2, WWIN, D), BF16), pltpu.VMEM((edges, 16, D), BF16), pltpu.SemaphoreType.DMA((2,)),
                        pltpu.SemaphoreType.DMA((4,)), pltpu.SemaphoreType.DMA((edges,))],
        compiler_params=pltpu.CompilerParams(vmem_limit_bytes=VMEM_LIMIT),
    )(windows, own)


def _disassemble_exchange(d_main, d_lr, name="disassemble_exchange"):
    def body(main_ref, lr_ref, mine_ref, recv_ref, buf, in_sems, keep_sems, send_sems, recv_sems):
        x, y, c, _ = _mesh_pos()
        sib = (x, y, 1 - c)

        def loads(k):
            s, out = k % 3, []
            for a, b, where, src0 in _window_pieces(WSTEP * k, WSTEP * k + WWIN):
                if where == "main":
                    out.append(pltpu.make_async_copy(main_ref.at[pl.ds(src0, b - a)], buf.at[s, pl.ds(a - WSTEP * k, b - a)],
                                                     in_sems.at[2 * s + len(out)]))
            return out

        def keep(k):
            return pltpu.make_async_copy(buf.at[k % 3], mine_ref.at[k // 2], keep_sems.at[k % 3])

        def send(k):
            return _rcopy(buf.at[k % 3], recv_ref.at[k // 2], send_sems.at[k % 3], recv_sems.at[k // 2], sib)

        def store_start(k):
            pl.when(c == k % 2)(keep(k).start)
            pl.when(c != k % 2)(send(k).start)

        def store_wait(k):
            pl.when(c == k % 2)(keep(k).wait)
            pl.when(c != k % 2)(send(k).wait_send)

        for k in range(2):
            for d in loads(k):
                d.start()
        for k in range(NDEV):
            for d in loads(k):
                d.wait()
            for a, b, where, src0 in _window_pieces(WSTEP * k, WSTEP * k + WWIN):
                if where == "lr":
                    buf[k % 3, a - WSTEP * k:b - WSTEP * k, :] = lr_ref[src0:src0 + b - a, :]
            store_start(k)
            if k + 2 < NDEV:
                if k:
                    store_wait(k - 1)
                for d in loads(k + 2):
                    d.start()
        for k in range(NDEV - 3, NDEV):
            store_wait(k)
        for chip in range(NDEV // 2):
            _rcopy(buf.at[0], recv_ref.at[chip], send_sems.at[0], recv_sems.at[chip], sib).wait_recv()

    half = jax.ShapeDtypeStruct((NDEV // 2, WWIN, D), BF16)
    return pl.pallas_call(
        body, name=name, out_shape=(half, half),
        in_specs=[_ANY, pl.BlockSpec(memory_space=pltpu.VMEM)], out_specs=(_ANY, _ANY),
        scratch_shapes=[pltpu.VMEM((3, WWIN, D), BF16), pltpu.SemaphoreType.DMA((6,)), pltpu.SemaphoreType.DMA((3,)),
                        pltpu.SemaphoreType.DMA((3,)), pltpu.SemaphoreType.DMA((NDEV // 2,))],
        compiler_params=pltpu.CompilerParams(vmem_limit_bytes=VMEM_LIMIT),
    )(d_main, d_lr)


def _add_blocks(a, b, name, tr):
    _, R, C = a.shape

    def body(a_ref, b_ref, o_ref):
        o_ref[...] = (a_ref[...].astype(F32) + b_ref[...].astype(F32)).astype(BF16)

    blk = pl.BlockSpec((None, tr, C), lambda q, i: (q, i, 0))
    return pl.pallas_call(body, name=name, out_shape=jax.ShapeDtypeStruct(a.shape, BF16), grid=(a.shape[0], R // tr),
                          in_specs=[blk, blk], out_specs=blk, compiler_params=_cp("parallel", "parallel"))(a, b)


_HBM = pl.BlockSpec(memory_space=pltpu.HBM)
_SEM = pl.BlockSpec(memory_space=pltpu.SEMAPHORE)
_VMEM = pl.BlockSpec(memory_space=pltpu.VMEM)
_SIDE = pltpu.CompilerParams(has_side_effects=pltpu.SideEffectType.DATAFLOW_SIDE_EFFECTING)
_TOKEN = jax.ShapeDtypeStruct((8, LANE), F32)


def _hbm(a):
    return pltpu.with_memory_space_constraint(a, pltpu.HBM)


def _hbm_like(arrs):
    return tuple(pltpu.HBM(a.shape, a.dtype) for a in arrs)


def _tie(x, token):
    return x + token[0, 0].astype(x.dtype)


def _chip_copies(ins, lands, send, recv, nrel):
    x, y, c, chips = _mesh_pos()
    first = [sum(nrel[:a]) for a in range(len(ins))]
    return [pltpu.make_async_remote_copy(src_ref=ins[a].at[2 * chip[0] + chip[1]], dst_ref=lands[a].at[j], send_sem=send.at[first[a] + j],
                                         recv_sem=recv.at[first[a] + j], device_id=(*chip, c), device_id_type=MESH)
            for a in range(len(ins)) for j, chip in enumerate(chips[:nrel[a]])]


def _chip_start(psums, name, nrel=None):
    n = len(psums)
    nrel = nrel or [3] * n
    lands = [lax.empty((r,) + p.shape[1:], p.dtype) for r, p in zip(nrel, psums)]

    def body(*refs):
        for d in _chip_copies(refs[:n], refs[n:2 * n], refs[2 * n], refs[2 * n + 1], nrel):
            d.start()
        refs[-1][...] = jnp.zeros_like(refs[-1])

    sems = pltpu.SemaphoreType.DMA((sum(nrel),))
    out = pl.pallas_call(
        body, name=name, out_shape=(sems, sems) + _hbm_like(psums) + _hbm_like(lands) + (_TOKEN,),
        in_specs=[_HBM] * (2 * n), out_specs=(_SEM, _SEM) + (_HBM,) * (2 * n) + (_VMEM,),
        input_output_aliases={i: 2 + i for i in range(2 * n)}, compiler_params=_SIDE,
    )(*[_hbm(a) for a in list(psums) + lands])
    return out[0], out[1], list(out[2:2 + n]), list(out[2 + n:2 + 2 * n]), out[-1]


def _chip_wait(send, recv, psums, lands, after, name):
    n = len(psums)
    nrel = [l.shape[0] for l in lands]

    def body(*refs):
        for d in _chip_copies(refs[:n], refs[n:2 * n], refs[2 * n], refs[2 * n + 1], nrel):
            d.wait_send()
            d.wait_recv()

    out = pl.pallas_call(
        body, name=name, out_shape=_hbm_like(psums) + _hbm_like(lands),
        in_specs=[_HBM] * (2 * n) + [_SEM, _SEM, _ANY], out_specs=(_HBM,) * (2 * n),
        input_output_aliases={i: i for i in range(2 * n)}, compiler_params=_SIDE,
    )(*psums, *lands, send, recv, after)
    return list(out[:n]), list(out[n:])


def _hop_pos():
    x, y, c, _ = _mesh_pos()
    north = c == 1
    via = (jnp.where(north, 1 - x, x), jnp.where(north, y, 1 - y))
    return (*via, c), 2 * (1 - x) + (1 - y), jnp.where(north, 2 * x + (1 - y), 2 * (1 - x) + y)


def _hop_copies(ins, lands, send, recv):
    to, mine, _ = _hop_pos()
    return [pltpu.make_async_remote_copy(src_ref=ins[a].at[mine], dst_ref=lands[a], send_sem=send.at[a], recv_sem=recv.at[a],
                                         device_id=to, device_id_type=MESH) for a in range(len(ins))]


def _hop_start(psums, name):
    n = len(psums)
    lands = [lax.empty(p.shape[1:], p.dtype) for p in psums]

    def body(*refs):
        for d in _hop_copies(refs[:n], refs[n:2 * n], refs[2 * n], refs[2 * n + 1]):
            d.start()
        refs[-1][...] = jnp.zeros_like(refs[-1])

    sems = pltpu.SemaphoreType.DMA((n,))
    out = pl.pallas_call(
        body, name=name, out_shape=(sems, sems) + _hbm_like(psums) + _hbm_like(lands) + (_TOKEN,),
        in_specs=[_HBM] * (2 * n), out_specs=(_SEM, _SEM) + (_HBM,) * (2 * n) + (_VMEM,),
        input_output_aliases={i: 2 + i for i in range(2 * n)}, compiler_params=_SIDE,
    )(*[_hbm(a) for a in list(psums) + lands])
    return out[0], out[1], list(out[2:2 + n]), list(out[2 + n:2 + 2 * n]), out[-1]


def _hop_wait(send, recv, psums, lands, after, name):
    n = len(psums)

    def body(*refs):
        for d in _hop_copies(refs[:n], refs[n:2 * n], refs[2 * n], refs[2 * n + 1]):
            d.wait_send()
            d.wait_recv()

    out = pl.pallas_call(
        body, name=name, out_shape=_hbm_like(psums) + _hbm_like(lands),
        in_specs=[_HBM] * (2 * n) + [_SEM, _SEM, _ANY], out_specs=(_HBM,) * (2 * n),
        input_output_aliases={i: i for i in range(2 * n)}, compiler_params=_SIDE,
    )(*psums, *lands, send, recv, after)
    return list(out[:n]), list(out[n:])


def _hop_add(psums, land, idx, name, tr):
    _, R, C = psums.shape

    def body(s_ref, p_ref, l_ref, o_ref):
        o_ref[...] = (p_ref[...].astype(F32) + l_ref[...].astype(F32)).astype(BF16)

    blk = pl.BlockSpec((None, tr, C), lambda i, s: (s[0], i, 0))
    grid_spec = pltpu.PrefetchScalarGridSpec(num_scalar_prefetch=1, grid=(R // tr,),
                                             in_specs=[blk, pl.BlockSpec((tr, C), lambda i, s: (i, 0))], out_specs=blk)
    return pl.pallas_call(body, name=name, out_shape=jax.ShapeDtypeStruct(psums.shape, BF16), grid_spec=grid_spec,
                          input_output_aliases={1: 0}, compiler_params=_cp("parallel"))(idx, psums, land)


def _pair_copies(ins, lands, send, recv):
    x, y, c, _ = _mesh_pos()
    return [pltpu.make_async_remote_copy(src_ref=ins[a].at[:, 1 - c], dst_ref=lands[a], send_sem=send.at[a], recv_sem=recv.at[a],
                                         device_id=(x, y, 1 - c), device_id_type=MESH) for a in range(len(ins))]


def _pair_start(grads, name):
    n = len(grads)
    lands = [lax.empty((4,) + g.shape[2:], g.dtype) for g in grads]

    def body(*refs):
        for d in _pair_copies(refs[:n], refs[n:2 * n], refs[2 * n], refs[2 * n + 1]):
            d.start()
        refs[-1][...] = jnp.zeros_like(refs[-1])

    sems = pltpu.SemaphoreType.DMA((n,))
    out = pl.pallas_call(
        body, name=name, out_shape=(sems, sems) + _hbm_like(grads) + _hbm_like(lands) + (_TOKEN,),
        in_specs=[_HBM] * (2 * n), out_specs=(_SEM, _SEM) + (_HBM,) * (2 * n) + (_VMEM,),
        input_output_aliases={i: 2 + i for i in range(2 * n)}, compiler_params=_SIDE,
    )(*[_hbm(a) for a in list(grads) + lands])
    return out[0], out[1], list(out[2:2 + n]), list(out[2 + n:2 + 2 * n]), out[-1]


def _pair_wait(send, recv, grads, lands, after, name):
    n = len(grads)

    def body(*refs):
        for d in _pair_copies(refs[:n], refs[n:2 * n], refs[2 * n], refs[2 * n + 1]):
            d.wait_send()
            d.wait_recv()

    out = pl.pallas_call(
        body, name=name, out_shape=_hbm_like(grads) + _hbm_like(lands),
        in_specs=[_HBM] * (2 * n) + [_SEM, _SEM, _ANY], out_specs=(_HBM,) * (2 * n),
        input_output_aliases={i: i for i in range(2 * n)}, compiler_params=_SIDE,
    )(*grads, *lands, send, recv, after)
    return list(out[:n]), list(out[n:])


def _slot(chip, c):
    return 4 * chip[0] + 2 * chip[1] + c


def _gather_start(shards, lands, after, name):
    n = len(shards)

    def body(*refs):
        src, land, send, recv = refs[:n], refs[n:2 * n], refs[2 * n + 1], refs[2 * n + 2]
        x, y, c, chips = _mesh_pos()
        for a in range(n):
            for k, to in enumerate([(x, y, 1 - c)] + [(*chip, c) for chip in chips]):
                pltpu.make_async_remote_copy(src_ref=src[a], dst_ref=land[a].at[_slot((x, y), c)], send_sem=send.at[4 * a + k],
                                             recv_sem=recv.at[4 * a + k], device_id=to, device_id_type=MESH).start()
        refs[-1][...] = jnp.zeros_like(refs[-1])

    sems = pltpu.SemaphoreType.DMA((4 * n,))
    out = pl.pallas_call(
        body, name=name, out_shape=(sems, sems) + _hbm_like(shards) + _hbm_like(lands) + (_TOKEN,),
        in_specs=[_HBM] * (2 * n) + [_ANY], out_specs=(_SEM, _SEM) + (_HBM,) * (2 * n) + (_VMEM,),
        input_output_aliases={i: 2 + i for i in range(2 * n)}, compiler_params=_SIDE,
    )(*[_hbm(a) for a in list(shards) + list(lands)], after)
    return out[0], out[1], list(out[2:2 + n]), list(out[2 + n:2 + 2 * n]), out[-1]


def _gather_pass(lands, recv, after, name, first=0):
    n = len(lands)

    def body(*refs):
        land, recv1 = refs[:n], refs[n]
        send2, recv2 = refs[n + 2], refs[n + 3]
        x, y, c, chips = _mesh_pos()
        for a in range(n):
            for j, chip in enumerate(chips):
                blk = land[a].at[_slot(chip, c)]
                pltpu.make_async_remote_copy(src_ref=blk, dst_ref=blk, send_sem=send2.at[3 * a + j], recv_sem=recv1.at[4 * (first + a) + 1 + j],
                                             device_id=(*chip, c), device_id_type=MESH).wait_recv()
                pltpu.make_async_remote_copy(src_ref=blk, dst_ref=blk, send_sem=send2.at[3 * a + j], recv_sem=recv2.at[3 * a + j],
                                             device_id=(x, y, 1 - c), device_id_type=MESH).start()
        refs[-1][...] = jnp.zeros_like(refs[-1])

    sems = pltpu.SemaphoreType.DMA((3 * n,))
    out = pl.pallas_call(
        body, name=name, out_shape=(sems, sems) + _hbm_like(lands) + (_TOKEN,),
        in_specs=[_HBM] * n + [_SEM, _ANY], out_specs=(_SEM, _SEM) + (_HBM,) * n + (_VMEM,),
        input_output_aliases={i: 2 + i for i in range(n)}, compiler_params=_SIDE,
    )(*lands, recv, after)
    return out[0], out[1], list(out[2:2 + n]), out[-1]


def _gather_wait(shards, lands, send, recv, send2, recv2, after, name, first=0):
    n = len(lands)

    def body(*refs):
        src, land = refs[:n], refs[n:2 * n]
        send1, recv1, snd2, rcv2 = refs[2 * n:2 * n + 4]
        x, y, c, chips = _mesh_pos()
        sib = (x, y, 1 - c)
        for a in range(n):
            for k in range(4):
                pltpu.make_async_remote_copy(src_ref=src[a], dst_ref=land[a].at[_slot((x, y), c)], send_sem=send1.at[4 * (first + a) + k],
                                             recv_sem=recv1.at[4 * (first + a) + k], device_id=sib, device_id_type=MESH).wait_send()
            blk = land[a].at[_slot((x, y), 1 - c)]
            pltpu.make_async_remote_copy(src_ref=blk, dst_ref=blk, send_sem=send1.at[4 * (first + a)], recv_sem=recv1.at[4 * (first + a)],
                                         device_id=sib, device_id_type=MESH).wait_recv()
            for j, chip in enumerate(chips):
                mine, theirs = land[a].at[_slot(chip, c)], land[a].at[_slot(chip, 1 - c)]
                pltpu.make_async_remote_copy(src_ref=mine, dst_ref=mine, send_sem=snd2.at[3 * a + j], recv_sem=rcv2.at[3 * a + j],
                                             device_id=sib, device_id_type=MESH).wait_send()
                pltpu.make_async_remote_copy(src_ref=theirs, dst_ref=theirs, send_sem=snd2.at[3 * a + j], recv_sem=rcv2.at[3 * a + j],
                                             device_id=sib, device_id_type=MESH).wait_recv()

    out = pl.pallas_call(
        body, name=name, out_shape=_hbm_like(shards) + _hbm_like(lands),
        in_specs=[_HBM] * (2 * n) + [_SEM] * 4 + [_ANY], out_specs=(_HBM,) * (2 * n),
        input_output_aliases={i: i for i in range(2 * n)}, compiler_params=_SIDE,
    )(*shards, *lands, send, recv, send2, recv2, after)
    return list(out[n:])


def _win_tree():
    x, y, c, chips = _mesh_pos()
    north = c == 1
    handed = (jnp.where(north, 1 - x, x), jnp.where(north, y, 1 - y))
    hand_to = (jnp.where(north, x, 1 - x), jnp.where(north, 1 - y, y))
    return x, y, c, chips, handed, hand_to


def _blk(land, chip, c):
    return land.at[_slot(chip, c)]


def _rcopy(src, dst, send, recv, to):
    return pltpu.make_async_remote_copy(src_ref=src, dst_ref=dst, send_sem=send, recv_sem=recv, device_id=to, device_id_type=MESH)


def _win_start(shards, lands, name):
    n = len(shards)

    def body(*refs):
        src, land, send, recv = refs[:n], refs[n:2 * n], refs[2 * n], refs[2 * n + 1]
        x, y, c, chips, _, _ = _win_tree()
        for a in range(n):
            for k, to in enumerate([(x, y, 1 - c), (*chips[0], c), (*chips[1], c)]):
                _rcopy(src[a], _blk(land[a], (x, y), c), send.at[3 * a + k], recv.at[3 * a + k], to).start()
        refs[-1][...] = jnp.zeros_like(refs[-1])

    sems = pltpu.SemaphoreType.DMA((3 * n,))
    out = pl.pallas_call(
        body, name=name, out_shape=(sems, sems) + _hbm_like(shards) + _hbm_like(lands) + (_TOKEN,),
        in_specs=[_HBM] * (2 * n), out_specs=(_SEM, _SEM) + (_HBM,) * (2 * n) + (_VMEM,),
        input_output_aliases={i: 2 + i for i in range(2 * n)}, compiler_params=_SIDE,
    )(*[_hbm(a) for a in list(shards) + list(lands)])
    return out[0], out[1], list(out[2:2 + n]), list(out[2 + n:2 + 2 * n]), out[-1]


def _win_hand_on(lands, recv1, after, name):
    n, m = len(lands), len(after)

    def body(*refs):
        land, rcv1 = refs[:n], refs[n]
        send2, recv2 = refs[n + 1 + m], refs[n + 2 + m]
        x, y, c, chips, handed, hand_to = _win_tree()
        for a in range(n):
            for j in range(2):
                blk = _blk(land[a], chips[j], c)
                _rcopy(blk, blk, send2.at[3 * a], rcv1.at[3 * a + 1 + j], (*chips[j], c)).wait_recv()
            blk = _blk(land[a], handed, c)
            _rcopy(blk, blk, send2.at[3 * a], recv2.at[3 * a], (*hand_to, c)).start()
            for j in range(2):
                blk = _blk(land[a], chips[j], c)
                _rcopy(blk, blk, send2.at[3 * a + 1 + j], recv2.at[3 * a + 1 + j], (x, y, 1 - c)).start()
        refs[-1][...] = jnp.zeros_like(refs[-1])

    sems = pltpu.SemaphoreType.DMA((3 * n,))
    out = pl.pallas_call(
        body, name=name, out_shape=(sems, sems) + _hbm_like(lands) + (_TOKEN,),
        in_specs=[_HBM] * n + [_SEM] + [_ANY] * m, out_specs=(_SEM, _SEM) + (_HBM,) * n + (_VMEM,),
        input_output_aliases={i: 2 + i for i in range(n)}, compiler_params=_SIDE,
    )(*lands, recv1, *after)
    return out[0], out[1], list(out[2:2 + n]), out[-1]


def _win_last(lands, recv2, after, name):
    n, m = len(lands), len(after)

    def body(*refs):
        land, rcv2 = refs[:n], refs[n]
        send3, recv3 = refs[n + 1 + m], refs[n + 2 + m]
        x, y, c, chips, _, hand_to = _win_tree()
        for a in range(n):
            blk = _blk(land[a], chips[2], c)
            _rcopy(blk, blk, send3.at[a], rcv2.at[3 * a], (*hand_to, c)).wait_recv()
            _rcopy(blk, blk, send3.at[a], recv3.at[a], (x, y, 1 - c)).start()
        refs[-1][...] = jnp.zeros_like(refs[-1])

    sems = pltpu.SemaphoreType.DMA((n,))
    out = pl.pallas_call(
        body, name=name, out_shape=(sems, sems) + _hbm_like(lands) + (_TOKEN,),
        in_specs=[_HBM] * n + [_SEM] + [_ANY] * m, out_specs=(_SEM, _SEM) + (_HBM,) * n + (_VMEM,),
        input_output_aliases={i: 2 + i for i in range(n)}, compiler_params=_SIDE,
    )(*lands, recv2, *after)
    return out[0], out[1], list(out[2:2 + n]), out[-1]


def _win_wait(shards, lands, sems1, sems2, sems3, after, name):
    n = len(lands)

    def body(*refs):
        src, land = refs[:n], refs[n:2 * n]
        send1, recv1, send2, recv2, send3, recv3 = refs[2 * n:2 * n + 6]
        x, y, c, chips, handed, hand_to = _win_tree()
        sib = (x, y, 1 - c)
        for a in range(n):
            own = _blk(land[a], (x, y), c)
            for k in range(3):
                _rcopy(src[a], own, send1.at[3 * a + k], recv1.at[3 * a + k], sib).wait_send()
            blk = _blk(land[a], (x, y), 1 - c)
            _rcopy(blk, blk, send1.at[3 * a], recv1.at[3 * a], sib).wait_recv()
            blk = _blk(land[a], handed, c)
            _rcopy(blk, blk, send2.at[3 * a], recv2.at[3 * a], sib).wait_send()
            for j in range(2):
                mine, theirs = _blk(land[a], chips[j], c), _blk(land[a], chips[j], 1 - c)
                _rcopy(mine, mine, send2.at[3 * a + 1 + j], recv2.at[3 * a + 1 + j], sib).wait_send()
                _rcopy(theirs, theirs, send2.at[3 * a + 1 + j], recv2.at[3 * a + 1 + j], sib).wait_recv()
            mine, theirs = _blk(land[a], chips[2], c), _blk(land[a], chips[2], 1 - c)
            _rcopy(mine, mine, send3.at[a], recv3.at[a], sib).wait_send()
            _rcopy(theirs, theirs, send3.at[a], recv3.at[a], sib).wait_recv()

    out = pl.pallas_call(
        body, name=name, out_shape=_hbm_like(shards) + _hbm_like(lands),
        in_specs=[_HBM] * (2 * n) + [_SEM] * 6 + [_ANY], out_specs=(_HBM,) * (2 * n),
        input_output_aliases={i: i for i in range(2 * n)}, compiler_params=_SIDE,
    )(*shards, *lands, *sems1, *sems2, *sems3, after)
    return list(out[n:])


def _pad_to(v, n):
    return jnp.pad(v, [(0, 0)] * (v.ndim - 1) + [(0, n - v.shape[-1])])


def _pack_small(n1, gb, sk, gn, n2, fn, extra=None):
    parts = [n1.reshape(-1), gb.reshape(-1), sk.reshape(-1), gn.reshape(-1), n2.reshape(-1), fn.reshape(-1)]
    flat = jnp.concatenate(parts + ([extra.reshape(-1)] if extra is not None else []))
    return _pad_to(flat, SMALL_N).reshape(SMALL_ROWS, LANE)


def _unpack_small(p):
    f = p.reshape(-1)
    return (f[S_N1:S_GB].reshape(1, D), f[S_GB:S_SK].reshape(1, GH * DK), f[S_SK:S_GN].reshape(1, NQ), f[S_GN:S_N2].reshape(1, DV),
            f[S_N2:S_FN].reshape(1, D), f[S_FN:S_LOSS].reshape(D))


class _Comm:
    def __init__(self, rest_shards, rest_lands, after, c_idx):
        self.c_idx = c_idx
        self.send, self.recv, self.shards, self.lands, self.token = _gather_start(rest_shards, rest_lands, after, "gather_rest_start")

    def _pass(self, lo, hi, after, tag):
        send2, recv2, lands, token = _gather_pass(self.lands[lo:hi], self.recv, after, "gather_pass_" + tag, first=lo)
        self.passed = (lo, hi, send2, recv2, lands)
        return token

    def _wait(self, after, tag):
        lo, hi, send2, recv2, lands = self.passed
        return _gather_wait(self.shards[lo:hi], lands, self.send, self.recv, send2, recv2, after, "gather_wait_" + tag, first=lo)

    def mixed(self, gla_o, gla_norm_w):
        return _tie(gla_norm_w, self._pass(0, 1, gla_o, "out"))

    def w_out(self, merged, norm2_w):
        (wo_all,) = self._wait(merged, "out")
        return wo_all.reshape(D, D), _tie(norm2_w, self._pass(1, 3, merged, "up"))

    def w_up(self, v2):
        wg_all, wu_all = self._wait(v2, "up")
        self._pass(3, 4, v2, "down")
        return wg_all.reshape(FH, D), wu_all.reshape(FH, D)

    def w_down(self, ff):
        return self._wait(ff, "down")[0].reshape(FH, D)

    def _reduce(self, tag, names, grads, recv1, rows):
        psums = [_pair_add(g, r, self.c_idx, "pair_add_" + nm, g.shape[2]) for g, r, nm in zip(grads, recv1, names)]
        *flight, token = _chip_start(psums, "reduce_chips_start_" + tag)
        return dict(tag=tag, names=names, rows=rows, flight=flight), token

    def ffn_grads(self, d_wg, d_wu, d_wd):
        self.ffn_pair = _pair_start([d.reshape(4, 2, FS, D) for d in (d_wg, d_wu, d_wd)], "reduce_pair_start_ffn")
        return self.ffn_pair[-1]

    def ffn_reduce(self, dv2, norm2_w):
        send, recv, grads, lands, _ = self.ffn_pair
        grads, recv1 = _pair_wait(send, recv, grads, lands, dv2, "reduce_pair_wait_ffn")
        self.ffn, token = self._reduce("ffn", ["w_ffn_gate", "w_ffn_up", "w_ffn_down"], grads, recv1, [176, 176, 176])
        return _tie(norm2_w, token)

    def in_grads(self, d_wmain, d_wlr, w_lr):
        mine, recv = _disassemble_exchange(d_wmain, d_wlr)
        self.in_names, self.in_rows = ["w_in", "w_out"], [808, 256]
        *self.in_hop, token = _hop_start([_add_blocks(mine, recv, "pair_add_w_in", 808)], "reduce_hop_start_in")
        return _tie(w_lr, token)

    def in_reduce(self, d_wo):
        d_wo4 = d_wo.reshape(4, 2, D // NDEV, D)
        send, recv, (d_wo4,), lands, _ = _pair_start([d_wo4], "reduce_pair_start_out")
        (d_wo4,), (wo_recv,) = _pair_wait(send, recv, [d_wo4], lands, self.update(self.ffn, d_wo), "reduce_pair_wait_out")
        wo_psum = _pair_add(d_wo4, wo_recv, self.c_idx, "pair_add_w_out", 256)
        (psum,), (land,) = _hop_wait(*self.in_hop, wo_psum, "reduce_hop_wait_in")
        psum = _hop_add(psum, land, _hop_pos()[2].astype(jnp.int32).reshape(1), "hop_add_w_in", 808)
        *flight, token = _chip_start([psum, wo_psum], "reduce_chips_start_in", nrel=[2, 3])
        self.inw = dict(tag="in", names=self.in_names, rows=self.in_rows, flight=flight)
        return token


def _local_step(xs, tgt, u, norm1_w, gla_gate_b, attn_sinks, gla_norm_w, norm2_w, fnw, w_main, w_lr, w2p, comm):
    proj =_mm(u, w_main, tb=True, tm=1024, tn=1280, tk=D, name="in_proj")
    plr = _mm(u, w_lr, tb=True, tm=1024, tn=LANE, tk=D, name="in_proj_lr")
    attn_o = _attn_fwd(proj, attn_sinks)
    gla_o, states = _gla_fwd(proj, plr, w2p, gla_gate_b)
    merged = _merge_fwd(attn_o, gla_o, proj, comm.mixed(gla_o, gla_norm_w))
    wo, norm2_w = comm.w_out(merged, norm2_w)
    h1 = _mm(merged, wo, tm=1024, tn=512, tk=D, res=xs, name="out_proj")
    v2 = _rmsnorm_fwd(h1, norm2_w, "norm2_fwd")
    wg_all, wu_all = comm.w_up(v2)
    fa, fb, ff = _ffn_up(v2, wg_all, wu_all)
    wd_all = comm.w_down(ff)
    h2 = _mm(ff, wd_all, tm=1024, tn=1024, tk=FH // 2, res=h1, name="ffn_down")
    dh2, dh2b, d_fnw, loss_part = _loss_head(h2, fnw, tgt)

    da, db = _ffn_dact(dh2b, wd_all, fa, fb)
    Tn = xs.shape[0]
    d_wd = _mm(ff, dh2b, ta=True, tm=512, tn=D, tk=Tn, out_dtype=BF16, name="ffn_dwd")
    d_wg = _mm(da, v2, ta=True, tm=512, tn=D, tk=Tn, out_dtype=BF16, name="ffn_dwg")
    d_wu = _mm(db, v2, ta=True, tm=512, tn=D, tk=Tn, out_dtype=BF16, name="ffn_dwu")
    dv2 = _mm(da, wg_all, tm=1024, tn=1024, tk=FH // 2, after=comm.ffn_grads(d_wg, d_wu, d_wd), name="ffn_dv2_gate")
    dv2 = _mm(db, wu_all, tm=1024, tn=1024, tk=FH // 2, res=dv2, name="ffn_dv2_up")
    norm2_w = comm.ffn_reduce(dv2, norm2_w)
    dh1, dh1b, d_n2 = _rmsnorm_bwd(dv2, h1, norm2_w, dh2, "norm2_bwd")
    dmerged = _mm(dh1b, wo, tb=True, tm=1024, tn=512, tk=D, name="out_proj_dx")
    d_attn, d_gla, d_gates, d_gnw = _merge_bwd(dmerged, attn_o, gla_o, proj, gla_norm_w)
    d_q, d_kv, d_sinks = _attn_bwd(proj, attn_sinks, attn_o, d_attn)
    d_gqk, d_gv, d_plr, d_w2p, d_gb = _gla_bwd(proj, plr, w2p, gla_gate_b, states, d_gla)
    dproj = jnp.concatenate([d_q, d_kv, d_gqk, d_gv, d_gates], axis=1)
    d_wmain = _mm(dproj, u, ta=True, tm=640, tn=D, tk=xs.shape[0], out_dtype=BF16, name="in_proj_dw")
    d_wlr = _mm(d_plr, u, ta=True, tm=LANE, tn=1024, tk=xs.shape[0], out_dtype=BF16, name="in_proj_lr_dw")
    du_lr = _mm(d_plr, comm.in_grads(d_wmain, d_wlr, w_lr), tm=1024, tn=1024, tk=LANE, name="in_proj_lr_dx")
    d_wo = _mm(merged, dh1b, ta=True, tm=1024, tn=512, tk=xs.shape[0], out_dtype=BF16, after=du_lr, name="out_proj_dw")
    du = _mm(dproj, w_main, tm=1024, tn=1024, tk=2560, res=du_lr, after=comm.in_reduce(d_wo), name="in_proj_dx")
    dx, _, d_n1 = _rmsnorm_bwd(du, xs, norm1_w, dh1, "norm1_bwd")
    return dx, loss_part, d_w2p, d_gb, d_sinks, d_gnw, d_n1, d_n2, d_fnw


def kernel(x, norm1_w, w_in, gla_gate_w2, gla_gate_b, attn_sinks, gla_norm_w, w_out, norm2_w, w_ffn_gate, w_ffn_up, w_ffn_down, final_norm_w, loss_target, m_norm1_w, m_w_in, m_gla_gate_w2, m_gla_gate_b, m_attn_sinks, m_gla_norm_w, m_w_out, m_norm2_w, m_w_ffn_gate, m_w_ffn_up, m_w_ffn_down, m_final_norm_w, v_norm1_w, v_w_in, v_gla_gate_w2, v_gla_gate_b, v_attn_sinks, v_gla_norm_w, v_w_out, v_norm2_w, v_w_ffn_gate, v_w_ffn_up, v_w_ffn_down, v_final_norm_w):
    xs, tgt = x[0], loss_target[0]
    fnw = final_norm_w.reshape(1, D)
    c_idx = lax.axis_index("c").astype(jnp.int32).reshape(1)
    dev = 4 * lax.axis_index("x") + 2 * lax.axis_index("y") + lax.axis_index("c")

    chip_idx = (2 * lax.axis_index("x") + lax.axis_index("y")).astype(jnp.int32).reshape(1)

    shift = (WS - WSTEP) * dev
    edge = WWIN - WS
    window = lax.dynamic_slice(jnp.pad(jnp.transpose(w_in[0]).astype(BF16), ((edge, edge), (0, 0))), (edge - shift, 0), (WWIN, D))
    w2_land = lax.dynamic_update_slice(lax.empty((NDEV, RANK, LANE), F32), gla_gate_w2, (dev, 0, 0))
    *sems1, win_srcs, win_lands, tok = _win_start([window, gla_gate_w2[0]], [lax.empty((NDEV, WWIN, D), BF16), w2_land], "gather_in_start")
    tr2 = lambda t: jnp.transpose(t[0])
    rows3 = lambda t: jnp.transpose(t[0] + tok[0, 0])
    rest = [(w + tok[0, 0]).astype(BF16) for w in (w_out[0], tr2(w_ffn_gate), tr2(w_ffn_up), w_ffn_down[0])]
    rest_lands = [lax.dynamic_update_slice(lax.empty((NDEV,) + s.shape, s.dtype), s[None], (dev, 0, 0)) for s in rest]
    win3 = [rows3(t) for t in (w_in, m_w_in, v_w_in)]
    *sems2, win_lands, tok = _win_hand_on(win_lands, sems1[1], rest + rest_lands + win3, "gather_in_hand_on")
    u = _rmsnorm_fwd(xs, _tie(norm1_w, tok), "norm1_fwd")
    *sems3, win_lands, tok = _win_last(win_lands, sems2[1], [u], "gather_in_last")
    comm = _Comm(rest, rest_lands, tok, c_idx)
    win_all, w2_all = _win_wait(win_srcs, win_lands, sems1, sems2, sems3, comm.token, "gather_in_wait")
    w_main, w_lr = _assemble_w_in(win_all, window)
    w2p = jnp.pad(jnp.transpose(w2_all, (1, 0, 2)).reshape(RANK, GH * DK), ((0, LANE - RANK), (0, 0)))

    big = {}

    def update(grp, after):
        psums, parts = _chip_wait(*grp["flight"], after, "reduce_chips_wait_" + grp["tag"])
        for nm, ps, pt, tr in zip(grp["names"], psums, parts, grp["rows"]):
            w, m, v = {"w_in": (w_in, m_w_in, v_w_in), "w_out": (w_out, m_w_out, v_w_out), "w_ffn_gate": (w_ffn_gate, m_w_ffn_gate, v_w_ffn_gate),
                       "w_ffn_up": (w_ffn_up, m_w_ffn_up, v_w_ffn_up), "w_ffn_down": (w_ffn_down, m_w_ffn_down, v_w_ffn_down)}[nm]
            if nm == "w_in":
                g_win = _sum_parts(ps, pt, chip_idx, "sum_w_in", tr, 1024)
                g3 = lax.dynamic_slice(g_win, (shift, 0), (WS, D))
                out3 = (g3,) + tuple(_adamw_given(*win3, g3, "adamw_w_in", 536, 512))
                big[nm] = [jnp.transpose(t)[None] for t in out3]
            elif nm in ("w_ffn_gate", "w_ffn_up"):
                big[nm] = [jnp.transpose(t)[None] for t in _adamw(tr2(w), tr2(m), tr2(v), ps, pt, chip_idx, "adamw_" + nm, tr)]
            else:
                big[nm] = [t[None] for t in _adamw(w[0], m[0], v[0], ps, pt, chip_idx, "adamw_" + nm, tr)]
            after = big[nm][0]
        return after

    comm.update = update
    dx, loss_part, d_w2p, d_gb, d_sinks, d_gnw, d_n1, d_n2, d_fnw = _local_step(
        xs, tgt, u, norm1_w, gla_gate_b, attn_sinks, gla_norm_w, norm2_w, fnw, w_main, w_lr, w2p, comm)

    pack = jnp.concatenate([_pack_small(d_n1, d_gb, d_sinks, d_gnw, d_n2, d_fnw, loss_part),
                            d_w2p[:RANK].reshape(GW2_ROWS, LANE)], axis=0)
    small = _sum_devices(_gather_small(pack))

    update(comm.inw, dx)
    g_small = small[:SMALL_ROWS]
    sm = _adamw_plain(_pack_small(norm1_w, gla_gate_b, attn_sinks, gla_norm_w, norm2_w, final_norm_w),
                      _pack_small(m_norm1_w, m_gla_gate_b, m_attn_sinks, m_gla_norm_w, m_norm2_w, m_final_norm_w),
                      _pack_small(v_norm1_w, v_gla_gate_b, v_attn_sinks, v_gla_norm_w, v_norm2_w, v_final_norm_w), g_small, "adamw_small")
    g_w2 = lax.dynamic_slice_in_dim(small[SMALL_ROWS:].reshape(RANK, GH * DK), dev * LANE, LANE, axis=1)
    w2 = [g_w2[None]] + [t[None] for t in _adamw_plain(gla_gate_w2[0], m_gla_gate_w2[0], v_gla_gate_w2[0], g_w2, "adamw_w2")]
    loss = g_small.reshape(-1)[S_LOSS]

    sg, sd, sm2, sv2 = [_unpack_small(t) for t in (g_small,) + tuple(sm)]

    def group(i, s):
        return (s[0], big["w_in"][i], w2[i], s[1], s[2], s[3], big["w_out"][i], s[4], big["w_ffn_gate"][i], big["w_ffn_up"][i],
                big["w_ffn_down"][i], s[5])

    return (loss, dx[None], *group(0, sg), *group(1, sd), *group(2, sm2), *group(3, sv2))
```
